```python
import jax, jax.numpy as jnp
from jax import lax
import numpy as np

D_MODEL = 1024
BATCH = 32
SEQ = 256
DEPTH = 1
DEC_BATCH = 4
DEC_SEQ = 2048
PAST_LEN = 256

GRID_W = 64
GLA_HEADS = 4
GLA_DK = 128
GLA_DV = 256
GLA_LOWRANK = 16
GLA_TAU = 16.0
GLA_CHUNK = 64
POOL_GROUPS = 4
POOL_GROUP_DIM = 128
POOL_WINDOWS = (2, 4, 8, 16)
N_EXPERTS = 32
TOP_K = 4
D_FF = 1024
SWIGLU_LIMIT = 7.0
SWIGLU_ALPHA = 1.702
MOE_BLOCK = 256
NORM_EPS = 1e-6
N_MOD = 6

QK_W = GLA_HEADS * GLA_DK
V_W = GLA_HEADS * GLA_DV
POOL_W = POOL_GROUPS * POOL_GROUP_DIM
IN_SIZES = (QK_W, QK_W, V_W, V_W, 2 * GLA_LOWRANK, POOL_W, 2 * D_MODEL)
IN_W = 2 * QK_W + 2 * V_W + 2 * GLA_LOWRANK + POOL_W + 2 * D_MODEL

kernel_name = 'gla_pool_moe_prefix_diffusion_step'


def rmsnorm(x, g):
    xf = x.astype(jnp.float32)
    y = xf * lax.rsqrt(jnp.mean(xf * xf, axis=-1, keepdims=True) + NORM_EPS)
    return (y * g.astype(jnp.float32)).astype(x.dtype)


def split_cols(z):
    offs = np.cumsum(IN_SIZES)[:-1].tolist()
    return jnp.split(z, offs, axis=-1)


def gla_scan(q, k, v, g, s0):
    b, t, h, _ = q.shape
    dv = v.shape[-1]
    n = t // GLA_CHUNK

    def to_chunks(a):
        return a.reshape(b, n, GLA_CHUNK, h, a.shape[-1]).transpose(1, 0, 3, 2, 4)

    qc, kc, vc, gc = to_chunks(q), to_chunks(k), to_chunks(v), to_chunks(g)
    lower = jnp.tril(jnp.ones((GLA_CHUNK, GLA_CHUNK), dtype=bool))[:, :, None]

    def step(s, inp):
        qi, ki, vi, gi = inp
        bcum = jnp.cumsum(gi, axis=2)
        o_inter = jnp.einsum('bhtd,bhdv->bhtv', qi * jnp.exp(bcum), s)
        diff = bcum[:, :, :, None, :] - bcum[:, :, None, :, :]
        decay = jnp.exp(jnp.where(lower, diff, -jnp.inf))
        att = jnp.einsum('bhtd,bhsd,bhtsd->bhts', qi, ki, decay)
        o = o_inter + jnp.einsum('bhts,bhsv->bhtv', att, vi)
        blast = bcum[:, :, -1:, :]
        s_new = jnp.exp(blast[:, :, 0, :, None]) * s + jnp.einsum('bhsd,bhsv->bhdv', ki * jnp.exp(blast - bcum), vi)
        return s_new, o

    s_fin, oc = lax.scan(step, s0, (qc, kc, vc, gc))
    o = oc.transpose(1, 0, 3, 2, 4).reshape(b, t, h, dv)
    return o, s_fin


def gla_bidir(q, k, v, g_f, g_b, s_f0, s_b0):
    o_f, s_f = gla_scan(q, k, v, g_f, s_f0)
    o_b, s_b = gla_scan(q[:, ::-1], k[:, ::-1], v[:, ::-1], g_b[:, ::-1], s_b0)
    return o_f + o_b[:, ::-1], s_f, s_b


def _centred_bounds(n, w):
    pos = jnp.arange(n)
    lo = jnp.clip(pos - w // 2, 0, n - 1)
    hi = jnp.clip(pos + w // 2 - 1, 0, n - 1)
    return lo, hi


def pool_seq(x):
    _, t, _ = x.shape
    xf = x.astype(jnp.float32)
    p = jnp.pad(jnp.cumsum(xf, axis=1), ((0, 0), (1, 0), (0, 0)))
    outs = []
    for gi, w in enumerate(POOL_WINDOWS):
        pg = p[..., gi * POOL_GROUP_DIM:(gi + 1) * POOL_GROUP_DIM]
        lo, hi = _centred_bounds(t, w)
        s = jnp.take(pg, hi + 1, axis=1) - jnp.take(pg, lo, axis=1)
        cnt = (hi - lo + 1).astype(jnp.float32)[None, :, None]
        outs.append(s / cnt)
    return (jnp.concatenate(outs, axis=-1) - xf).astype(x.dtype)


def pool_grid(x):
    bsz, t, cw = x.shape
    rows = t // GRID_W
    xf = x.astype(jnp.float32).reshape(bsz, rows, GRID_W, cw)
    sat = jnp.pad(jnp.cumsum(jnp.cumsum(xf, axis=1), axis=2), ((0, 0), (1, 0), (1, 0), (0, 0)))
    outs = []
    for gi, w in enumerate(POOL_WINDOWS):
        sg = sat[..., gi * POOL_GROUP_DIM:(gi + 1) * POOL_GROUP_DIM]
        rlo, rhi = _centred_bounds(rows, w)
        clo, chi = _centred_bounds(GRID_W, w)

        def corner(ri, ci):
            return jnp.take(jnp.take(sg, ri, axis=1), ci, axis=2)

        s = corner(rhi + 1, chi + 1) - corner(rlo, chi + 1) - corner(rhi + 1, clo) + corner(rlo, clo)
        cnt = ((rhi - rlo + 1)[:, None] * (chi - clo + 1)[None, :]).astype(jnp.float32)[None, :, :, None]
        outs.append(s / cnt)
    pooled = jnp.concatenate(outs, axis=-1) - xf
    return pooled.reshape(bsz, t, cw).astype(x.dtype)


def mixer(h, s_f0, s_b0, pool_fn, w_in, w_alpha, b_alpha, gla_norm_g, w_pool_grp, pool_scale,
          w_branch_gla, w_branch_pool, w_out):
    bsz, t, _ = h.shape
    f32 = jnp.float32
    q, k, v, og, alr, xp, mg = split_cols(h @ w_in)
    q = q.reshape(bsz, t, GLA_HEADS, GLA_DK).astype(f32) * (GLA_DK ** -0.5)
    k = k.reshape(bsz, t, GLA_HEADS, GLA_DK).astype(f32)
    v = v.reshape(bsz, t, GLA_HEADS, GLA_DV).astype(f32)
    a_logit = jnp.einsum('btjr,jrk->btjk', alr.reshape(bsz, t, 2, GLA_LOWRANK), w_alpha) + b_alpha
    log_alpha = jax.nn.log_sigmoid(a_logit.astype(f32)) / GLA_TAU
    g_f = log_alpha[:, :, 0].reshape(bsz, t, GLA_HEADS, GLA_DK)
    g_b = log_alpha[:, :, 1].reshape(bsz, t, GLA_HEADS, GLA_DK)
    o, s_f, s_b = gla_bidir(q, k, v, g_f, g_b, s_f0.astype(f32), s_b0.astype(f32))
    o = rmsnorm(o, gla_norm_g).reshape(bsz, t, V_W).astype(h.dtype) * jax.nn.silu(og)
    br_gla = o @ w_branch_gla
    pooled = pool_fn(xp).reshape(bsz, t, POOL_GROUPS, POOL_GROUP_DIM)
    pm = jnp.einsum('btgc,gcd->btgd', pooled, w_pool_grp).reshape(bsz, t, POOL_W) * pool_scale
    br_pool = pm @ w_branch_pool
    gates = jax.nn.sigmoid(mg).reshape(bsz, t, 2, D_MODEL)
    merged = gates[:, :, 0] * br_gla + gates[:, :, 1] * br_pool
    return merged @ w_out, s_f, s_b


def moe(h, w_router, b_router, w_gate, b_gate, w_up, b_up, w_down, b_down):
    n, d = h.shape
    f32 = jnp.float32
    logits = (h @ w_router).astype(f32) + b_router.astype(f32)
    top_logit, top_idx = lax.top_k(logits, TOP_K)
    top_w = jax.nn.softmax(top_logit, axis=-1)
    nk = n * TOP_K
    flat_e = top_idx.reshape(nk)
    flat_tok = jnp.repeat(jnp.arange(n, dtype=jnp.int32), TOP_K)
    flat_w = top_w.reshape(nk)
    order = jnp.argsort(flat_e)
    e_sorted = flat_e[order]
    counts = jnp.zeros((N_EXPERTS,), jnp.int32).at[flat_e].add(1)
    padded = (counts + MOE_BLOCK - 1) // MOE_BLOCK * MOE_BLOCK
    pad_end = jnp.cumsum(padded)
    pad_start = pad_end - padded
    start = jnp.cumsum(counts) - counts
    dest = pad_start[e_sorted] + jnp.arange(nk, dtype=jnp.int32) - start[e_sorted]
    n_blocks = -(-(nk + N_EXPERTS * (MOE_BLOCK - 1)) // MOE_BLOCK)
    n_slots = n_blocks * MOE_BLOCK
    slot_tok = jnp.zeros((n_slots,), jnp.int32).at[dest].set(flat_tok[order])
    slot_w = jnp.zeros((n_slots,), f32).at[dest].set(flat_w[order])
    block_e = jnp.minimum(jnp.searchsorted(pad_end, jnp.arange(n_blocks) * MOE_BLOCK, side='right'), N_EXPERTS - 1)

    def run_block(args):
        tok, e = args
        xb = h[tok]
        gate = jnp.minimum(xb @ w_gate[e] + b_gate[e], SWIGLU_LIMIT)
        up = jnp.clip(xb @ w_up[e] + b_up[e], -SWIGLU_LIMIT, SWIGLU_LIMIT)
        act = (up + 1.0) * (gate * jax.nn.sigmoid(SWIGLU_ALPHA * gate))
        return act @ w_down[e] + b_down[e]

    yb = lax.map(run_block, (slot_tok.reshape(n_blocks, MOE_BLOCK), block_e))
    y = yb.reshape(n_slots, d).astype(f32) * slot_w[:, None]
    return jnp.zeros((n, d), f32).at[slot_tok].add(y).astype(h.dtype)


def layer(x, mod, s_f0, s_b0, pool_fn, norm1_g, w_in, w_alpha, b_alpha, gla_norm_g, w_pool_grp, pool_scale,
          w_branch_gla, w_branch_pool, w_out, norm2_g, w_router, b_router, w_gate, b_gate, w_up, b_up,
          w_down, b_down):
    shift1, scale1, gate1, shift2, scale2, gate2 = jnp.split(mod, N_MOD, axis=-1)
    h = rmsnorm(x, norm1_g) * (1.0 + scale1) + shift1
    m, s_f, s_b = mixer(h, s_f0, s_b0, pool_fn, w_in, w_alpha, b_alpha, gla_norm_g, w_pool_grp, pool_scale,
                        w_branch_gla, w_branch_pool, w_out)
    x = x + gate1 * m
    h = rmsnorm(x, norm2_g) * (1.0 + scale2) + shift2
    f = moe(h.reshape(-1, D_MODEL), w_router, b_router, w_gate, b_gate, w_up, b_up, w_down, b_down)
    x = x + gate2 * f.reshape(x.shape)
    return x, s_f, s_b


def setup_inputs(seed: int = 0) -> dict:
    key = jax.random.key(seed)
    ks = jax.random.split(key, 32)
    nrm = jax.random.normal
    f32 = jnp.float32
    L, D, E, F = DEPTH, D_MODEL, N_EXPERTS, D_FF
    st_shape = (DEC_BATCH, DEPTH, GLA_HEADS, GLA_DK, GLA_DV)
    return {
        'x_prompt': nrm(ks[0], (BATCH, SEQ, D), f32),
        'x_sample': nrm(ks[1], (DEC_BATCH, DEC_SEQ, D), f32),
        'state_gla_fwd': nrm(ks[2], st_shape, f32),
        'state_gla_bwd': nrm(ks[3], st_shape, f32),
        'c': nrm(ks[4], (DEC_BATCH, D), f32),
        'c_ctx': nrm(ks[5], (D,), f32),
        'norm1_g': 1.0 + 0.02 * nrm(ks[6], (L, D), f32),
        'w_mod': nrm(ks[7], (L, D, N_MOD * D), f32) * D ** -0.5,
        'b_mod': 0.02 * nrm(ks[8], (L, N_MOD * D), f32),
        'w_in': nrm(ks[9], (L, D, IN_W), f32) * D ** -0.5,
        'w_alpha': nrm(ks[10], (L, 2, GLA_LOWRANK, QK_W), f32) * GLA_LOWRANK ** -0.5,
        'b_alpha': 0.02 * nrm(ks[11], (L, 2, QK_W), f32),
        'gla_norm_g': 1.0 + 0.02 * nrm(ks[12], (L, GLA_HEADS, GLA_DV), f32),
        'w_pool_grp': nrm(ks[13], (L, POOL_GROUPS, POOL_GROUP_DIM, POOL_GROUP_DIM), f32) * POOL_GROUP_DIM ** -0.5,
        'pool_scale': 1.0 + 0.02 * nrm(ks[14], (L, POOL_W), f32),
        'w_branch_gla': nrm(ks[15], (L, V_W, D), f32) * V_W ** -0.5,
        'w_branch_pool': nrm(ks[16], (L, POOL_W, D), f32) * POOL_W ** -0.5,
        'w_out': nrm(ks[17], (L, D, D), f32) * D ** -0.5,
        'norm2_g': 1.0 + 0.02 * nrm(ks[18], (L, D), f32),
        'w_router': nrm(ks[19], (L, D, E), f32) * D ** -0.5,
        'b_router': 0.01 * nrm(ks[20], (L, E), f32),
        'w_gate': nrm(ks[21], (L, E, D, F), f32) * D ** -0.5,
        'b_gate': 0.01 * nrm(ks[22], (L, E, F), f32),
        'w_up': nrm(ks[23], (L, E, D, F), f32) * D ** -0.5,
        'b_up': 0.01 * nrm(ks[24], (L, E, F), f32),
        'w_down': nrm(ks[25], (L, E, F, D), f32) * F ** -0.5,
        'b_down': 0.01 * nrm(ks[26], (L, E, D), f32),
        'final_norm_g': 1.0 + 0.02 * nrm(ks[27], (D,), f32),
    }


def reference(x_prompt, x_sample, state_gla_fwd, state_gla_bwd, c, c_ctx, norm1_g, w_mod, b_mod, w_in, w_alpha,
              b_alpha, gla_norm_g, w_pool_grp, pool_scale, w_branch_gla, w_branch_pool, w_out, norm2_g, w_router,
              b_router, w_gate, b_gate, w_up, b_up, w_down, b_down, final_norm_g):
    bp = x_prompt.shape[0]
    zero_state = jnp.zeros((bp, GLA_HEADS, GLA_DK, GLA_DV), jnp.float32)
    xc, xs = x_prompt, x_sample
    new_f, new_b = [], []
    for l in range(DEPTH):
        lp = (norm1_g[l], w_in[l], w_alpha[l], b_alpha[l], gla_norm_g[l], w_pool_grp[l], pool_scale[l],
              w_branch_gla[l], w_branch_pool[l], w_out[l], norm2_g[l], w_router[l], b_router[l], w_gate[l],
              b_gate[l], w_up[l], b_up[l], w_down[l], b_down[l])
        mod_ctx = (jax.nn.silu(c_ctx) @ w_mod[l] + b_mod[l])[None, None, :]
        xc, s_f, s_b = layer(xc, mod_ctx, zero_state, zero_state, pool_seq, *lp)
        new_f.append(s_f)
        new_b.append(s_b)
        mod_lat = (jax.nn.silu(c) @ w_mod[l] + b_mod[l])[:, None, :]
        xs, _, _ = layer(xs, mod_lat, state_gla_fwd[:, l], state_gla_bwd[:, l], pool_grid, *lp)
    y_prompt = rmsnorm(xc, final_norm_g)
    y_sample = rmsnorm(xs, final_norm_g)
    new_state_gla_fwd = jnp.stack(new_f, axis=1).astype(x_prompt.dtype)
    new_state_gla_bwd = jnp.stack(new_b, axis=1).astype(x_prompt.dtype)
    return (y_prompt, y_sample, new_state_gla_fwd, new_state_gla_bwd)
```

```python
import functools

import jax
import jax.numpy as jnp
from jax import lax
from jax.experimental import pallas as pl
from jax.experimental.pallas import tpu as pltpu

F32 = jnp.float32
BF16 = jnp.bfloat16
I32 = jnp.int32

D_MODEL = 1024
BATCH = 32
SEQ = 256
DEC_BATCH = 4
DEC_SEQ = 2048
GRID_W = 64
GLA_HEADS = 4
GLA_DK = 128
GLA_DV = 256
GLA_LOWRANK = 16
GLA_TAU = 16.0
GLA_CHUNK = 64
POOL_GROUPS = 4
POOL_GROUP_DIM = 128
POOL_WINDOWS = (2, 4, 8, 16)
N_EXPERTS = 32
TOP_K = 4
D_FF = 1024
SWIGLU_LIMIT = 7.0
SWIGLU_ALPHA = 1.702
MOE_BLOCK = 256
NORM_EPS = 1e-6
N_MOD = 6

QK_W = GLA_HEADS * GLA_DK
V_W = GLA_HEADS * GLA_DV
POOL_W = POOL_GROUPS * POOL_GROUP_DIM
MAIN_W = 2 * QK_W + 2 * V_W
ALR_W = 2 * GLA_LOWRANK
MG_W = 2 * D_MODEL

N_CTX = BATCH * SEQ
N_LAT = DEC_BATCH * DEC_SEQ
N_TOK = N_CTX + N_LAT
N_SLOT_BLOCKS = -(-(N_TOK * TOP_K + N_EXPERTS * (MOE_BLOCK - 1)) // MOE_BLOCK)
N_SLOTS = N_SLOT_BLOCKS * MOE_BLOCK

LANES = 128
TOK_TILE = 256
ROUTE_TILE = 512
VMEM_LIMIT = 56 * 1024 * 1024

NT_DIMS = (((1,), (1,)), ((), ()))
TN_DIMS = (((0,), (0,)), ((), ()))


def _params(semantics, vmem=VMEM_LIMIT):
    return pltpu.CompilerParams(dimension_semantics=semantics, vmem_limit_bytes=vmem)


def _split_bf16(a):
    hi = a.astype(BF16)
    lo = (a - hi.astype(F32)).astype(BF16)
    return hi, lo


def _dot(a, b):
    return jnp.dot(a, b, preferred_element_type=F32)


def _dot3(a, b):
    a_hi, a_lo = _split_bf16(a)
    b_hi, b_lo = _split_bf16(b)
    return _dot(a_hi, b_hi) + _dot(a_lo, b_hi) + _dot(a_hi, b_lo)


def _sigmoid(x):
    return 1.0 / (1.0 + jnp.exp(-x))


def _rms(x):
    return x * lax.rsqrt(jnp.mean(x * x, axis=-1, keepdims=True) + NORM_EPS)


def _mod_row(t):
    ctx_tiles = N_CTX // TOK_TILE
    return jnp.where(t < ctx_tiles, 0, 1 + (t - ctx_tiles) // (DEC_SEQ // TOK_TILE))


def _mod_kernel(c_ref, w_ref, b_ref, o_ref):
    c = c_ref[...]
    o_ref[...] = _dot3(c * _sigmoid(c), w_ref[...]) + b_ref[...]


def _mod_call(cvec, w_mod, b_mod):
    rows = cvec.shape[0]
    return pl.pallas_call(
        _mod_kernel,
        out_shape=jax.ShapeDtypeStruct((rows, N_MOD * D_MODEL), F32),
        grid=(N_MOD,),
        in_specs=[
            pl.BlockSpec((rows, D_MODEL), lambda j: (0, 0)),
            pl.BlockSpec((D_MODEL, D_MODEL), lambda j: (0, j)),
            pl.BlockSpec((1, D_MODEL), lambda j: (0, j)),
        ],
        out_specs=pl.BlockSpec((rows, D_MODEL), lambda j: (0, j)),
        compiler_params=_params(("arbitrary",)),
        name="mod",
    )(cvec, w_mod, b_mod)


def _inproj_kernel(x_ref, mod_ref, g_ref, wmain_ref, walr_ref, wxp_ref, wmg_ref,
                   q_ref, k_ref, v_ref, og_ref, alr_ref, xp_ref, mg_ref):
    mod = mod_ref[0]
    shift1 = mod[:, 0:D_MODEL]
    scale1 = mod[:, D_MODEL:2 * D_MODEL]
    h = _rms(x_ref[...]) * g_ref[...]
    h = (h * (1.0 + scale1) + shift1).astype(BF16)
    z = _dot(h, wmain_ref[...])
    q_ref[...] = (z[:, 0:QK_W] * (GLA_DK ** -0.5)).astype(BF16)
    k_ref[...] = z[:, QK_W:2 * QK_W].astype(BF16)
    v_ref[...] = z[:, 2 * QK_W:2 * QK_W + V_W].astype(BF16)
    og_ref[...] = z[:, 2 * QK_W + V_W:MAIN_W].astype(BF16)
    alr_ref[...] = _dot(h, walr_ref[...])
    xp_ref[...] = _dot(h, wxp_ref[...])
    mg_ref[...] = _dot(h, wmg_ref[...]).astype(BF16)


def _inproj_call(x, mod3, norm1_g, w_main, w_alr, w_xp, w_mg):
    n_tiles = N_TOK // TOK_TILE
    row = lambda t: (t, 0)
    const = lambda t: (0, 0)
    widths = (QK_W, QK_W, V_W, V_W, ALR_W, POOL_W, MG_W)
    dtypes = (BF16, BF16, BF16, BF16, F32, F32, BF16)
    return pl.pallas_call(
        _inproj_kernel,
        out_shape=[jax.ShapeDtypeStruct((N_TOK, w), dt) for w, dt in zip(widths, dtypes)],
        grid=(n_tiles,),
        in_specs=[
            pl.BlockSpec((TOK_TILE, D_MODEL), row),
            pl.BlockSpec((1, 1, N_MOD * D_MODEL), lambda t: (_mod_row(t), 0, 0)),
            pl.BlockSpec((1, D_MODEL), const),
            pl.BlockSpec((D_MODEL, MAIN_W), const),
            pl.BlockSpec((D_MODEL, ALR_W), const),
            pl.BlockSpec((D_MODEL, POOL_W), const),
            pl.BlockSpec((D_MODEL, MG_W), const),
        ],
        out_specs=[pl.BlockSpec((TOK_TILE, w), row) for w in widths],
        compiler_params=_params(("arbitrary",)),
        name="inproj",
    )(x, mod3, norm1_g, w_main, w_alr, w_xp, w_mg)


def _gla_kernel(*refs, rev, n_chunk, has_s0, want_state):
    q_ref, k_ref, v_ref, alr_ref, wa_ref, ba_ref = refs[:6]
    pos = 6
    s0_ref = None
    if has_s0:
        s0_ref = refs[pos]
        pos += 1
    o_ref = refs[pos]
    pos += 1
    sout_ref = None
    if want_state:
        sout_ref = refs[pos]
        pos += 1
    st_ref = refs[pos]

    c = pl.program_id(1)

    @pl.when(c == 0)
    def _():
        for h in range(GLA_HEADS):
            if has_s0:
                st_ref[h] = s0_ref[0, h].T
            else:
                st_ref[h] = jnp.zeros((GLA_DV, GLA_DK), F32)

    a = _dot3(alr_ref[...], wa_ref[...]) + ba_ref[...]
    g = (jnp.minimum(a, 0.0) - jnp.log(1.0 + jnp.exp(-jnp.abs(a)))) * (1.0 / GLA_TAU)

    row = lax.broadcasted_iota(I32, (GLA_CHUNK, GLA_CHUNK), 0)
    col = lax.broadcasted_iota(I32, (GLA_CHUNK, GLA_CHUNK), 1)
    tri = (col >= row) if rev else (col <= row)
    tri_b = jnp.where(tri, 1.0, 0.0).astype(BF16)
    g_hi, g_lo = _split_bf16(g)
    bcum = _dot(tri_b, g_hi) + _dot(tri_b, g_lo)
    blast = bcum[0:1] if rev else bcum[GLA_CHUNK - 1:GLA_CHUNK]
    bmid = bcum[GLA_CHUNK // 2:GLA_CHUNK // 2 + 1]
    e_q = jnp.exp(bcum - bmid)
    e_k = jnp.exp(bmid - bcum)
    e_in = jnp.exp(bcum)
    e_out = jnp.exp(blast - bcum)
    e_last = jnp.exp(blast)

    q = q_ref[...].astype(F32)
    k = k_ref[...].astype(F32)
    for h in range(GLA_HEADS):
        ks = slice(h * GLA_DK, (h + 1) * GLA_DK)
        vs = slice(h * GLA_DV, (h + 1) * GLA_DV)
        qh = q[:, ks]
        kh = k[:, ks]
        vh = v_ref[:, vs]
        att = lax.dot_general((qh * e_q[:, ks]).astype(BF16), (kh * e_k[:, ks]).astype(BF16),
                              NT_DIMS, preferred_element_type=F32)
        att = jnp.where(tri, att, 0.0).astype(BF16)
        st = st_ref[h]
        o_inter = lax.dot_general((qh * e_in[:, ks]).astype(BF16), st.astype(BF16),
                                  NT_DIMS, preferred_element_type=F32)
        o_ref[:, vs] = o_inter + _dot(att, vh)
        upd = lax.dot_general(vh, (kh * e_out[:, ks]).astype(BF16), TN_DIMS,
                              preferred_element_type=F32)
        st_ref[h] = st * e_last[:, ks] + upd

    if want_state:
        @pl.when(c == n_chunk - 1)
        def _():
            for h in range(GLA_HEADS):
                sout_ref[0, h] = st_ref[h].T


def _gla_call(q, k, v, alr, wa, ba, s0, *, rev, row0, n_seq, n_chunk, want_state):
    blk0 = row0 // GLA_CHUNK

    def row(s, c):
        cc = (n_chunk - 1 - c) if rev else c
        return (blk0 + s * n_chunk + cc, 0)

    const = lambda s, c: (0, 0)
    in_specs = [
        pl.BlockSpec((GLA_CHUNK, QK_W), row),
        pl.BlockSpec((GLA_CHUNK, QK_W), row),
        pl.BlockSpec((GLA_CHUNK, V_W), row),
        pl.BlockSpec((GLA_CHUNK, ALR_W), row),
        pl.BlockSpec((ALR_W, QK_W), const),
        pl.BlockSpec((1, QK_W), const),
    ]
    args = [q, k, v, alr, wa, ba]
    st_block = (1, GLA_HEADS, GLA_DK, GLA_DV)
    if s0 is not None:
        in_specs.append(pl.BlockSpec(st_block, lambda s, c: (s, 0, 0, 0)))
        args.append(s0)
    n_rows = n_seq * n_chunk * GLA_CHUNK
    out_shape = [jax.ShapeDtypeStruct((n_rows, V_W), F32)]
    out_specs = [pl.BlockSpec((GLA_CHUNK, V_W),
                              lambda s, c: (s * n_chunk + ((n_chunk - 1 - c) if rev else c), 0))]
    if want_state:
        out_shape.append(jax.ShapeDtypeStruct((n_seq,) + st_block[1:], F32))
        out_specs.append(pl.BlockSpec(st_block, lambda s, c: (s, 0, 0, 0)))
    return pl.pallas_call(
        functools.partial(_gla_kernel, rev=rev, n_chunk=n_chunk, has_s0=s0 is not None,
                          want_state=want_state),
        out_shape=out_shape,
        grid=(n_seq, n_chunk),
        in_specs=in_specs,
        out_specs=out_specs,
        scratch_shapes=[pltpu.VMEM((GLA_HEADS, GLA_DV, GLA_DK), F32)],
        compiler_params=_params(("arbitrary", "arbitrary")),
        name="gla_%s_%d" % ("bwd" if rev else "fwd", n_chunk),
    )(*args)


def _band(n, w, block):
    row = lax.broadcasted_iota(I32, (n, n), 0)
    col = lax.broadcasted_iota(I32, (n, n), 1)
    inside = (col >= row - w // 2) & (col <= row + w // 2 - 1)
    if block < n:
        inside = inside & ((row // block) == (col // block))
    return jnp.where(inside, 1.0, 0.0).astype(BF16)


def _win_count(p, n, w):
    return jnp.minimum(p + w // 2 - 1, n - 1) - jnp.maximum(p - w // 2, 0) + 1


def _pool_ctx_kernel(x_ref, o_ref):
    p = lax.broadcasted_iota(I32, (SEQ, POOL_GROUP_DIM), 0)
    for gi, w in enumerate(POOL_WINDOWS):
        cs = slice(gi * POOL_GROUP_DIM, (gi + 1) * POOL_GROUP_DIM)
        x = x_ref[:, cs]
        hi, lo = _split_bf16(x)
        band = _band(SEQ, w, SEQ)
        s = _dot(band, hi) + _dot(band, lo)
        cnt = _win_count(p, SEQ, w).astype(F32)
        o_ref[:, cs] = s / cnt - x


def _pool_ctx_call(xp):
    spec = pl.BlockSpec((SEQ, POOL_W), lambda b: (b, 0))
    return pl.pallas_call(
        _pool_ctx_kernel,
        out_shape=jax.ShapeDtypeStruct((N_CTX, POOL_W), F32),
        grid=(BATCH,),
        in_specs=[spec],
        out_specs=spec,
        compiler_params=_params(("arbitrary",)),
        name="pool_ctx",
    )(xp)


POOL_HALO = (max(POOL_WINDOWS) // 2) * GRID_W


def _pool_lat_kernel(x_ref, o_ref, pad_ref):
    rows = DEC_SEQ // GRID_W
    p = lax.broadcasted_iota(I32, (DEC_SEQ, POOL_GROUP_DIM), 0)
    r = p // GRID_W
    cidx = p % GRID_W
    zeros = jnp.zeros((POOL_HALO, POOL_GROUP_DIM), F32)
    pad_ref[0:POOL_HALO, :] = zeros
    pad_ref[POOL_HALO + DEC_SEQ:2 * POOL_HALO + DEC_SEQ, :] = zeros
    for gi, w in enumerate(POOL_WINDOWS):
        cs = slice(gi * POOL_GROUP_DIM, (gi + 1) * POOL_GROUP_DIM)
        band = _band(TOK_TILE, w, GRID_W)
        for t in range(DEC_SEQ // TOK_TILE):
            rs = slice(t * TOK_TILE, (t + 1) * TOK_TILE)
            hi, lo = _split_bf16(x_ref[rs, cs])
            pad_ref[POOL_HALO + t * TOK_TILE:POOL_HALO + (t + 1) * TOK_TILE, :] = (
                _dot(band, hi) + _dot(band, lo))
        acc = jnp.zeros((DEC_SEQ, POOL_GROUP_DIM), F32)
        for dr in range(-(w // 2), w // 2):
            start = POOL_HALO + dr * GRID_W
            acc = acc + pad_ref[start:start + DEC_SEQ, :]
        cnt = (_win_count(r, rows, w) * _win_count(cidx, GRID_W, w)).astype(F32)
        o_ref[:, cs] = acc / cnt - x_ref[:, cs]


def _pool_lat_call(xp):
    blk0 = N_CTX // DEC_SEQ
    return pl.pallas_call(
        _pool_lat_kernel,
        out_shape=jax.ShapeDtypeStruct((N_LAT, POOL_W), F32),
        grid=(DEC_BATCH,),
        in_specs=[pl.BlockSpec((DEC_SEQ, POOL_W), lambda b: (blk0 + b, 0))],
        out_specs=pl.BlockSpec((DEC_SEQ, POOL_W), lambda b: (b, 0)),
        scratch_shapes=[pltpu.VMEM((DEC_SEQ + 2 * POOL_HALO, POOL_GROUP_DIM), F32)],
        compiler_params=_params(("arbitrary",)),
        name="pool_lat",
    )(xp)


def _post_kernel(x_ref, mod_ref, of_ref, ob_ref, og_ref, pl_ref, mg_ref, gng_ref, wpg_ref, psc_ref,
                 wbg_ref, wbp_ref, wout_ref, n2g_ref, wr_ref, br_ref,
                 x1_ref, h2_ref, idx_ref, tw_ref):
    mod = mod_ref[0]
    gate1 = mod[:, 2 * D_MODEL:3 * D_MODEL]
    shift2 = mod[:, 3 * D_MODEL:4 * D_MODEL]
    scale2 = mod[:, 4 * D_MODEL:5 * D_MODEL]

    o = of_ref[...] + ob_ref[...]
    og = og_ref[...].astype(F32)
    gated = []
    for h in range(GLA_HEADS):
        vs = slice(h * GLA_DV, (h + 1) * GLA_DV)
        oh = _rms(o[:, vs]) * gng_ref[:, vs]
        ogh = og[:, vs]
        gated.append((oh * (ogh * _sigmoid(ogh))).astype(BF16))
    br_gla = _dot(jnp.concatenate(gated, axis=-1), wbg_ref[...])

    pm = []
    for gi in range(POOL_GROUPS):
        cs = slice(gi * POOL_GROUP_DIM, (gi + 1) * POOL_GROUP_DIM)
        pmg = _dot(pl_ref[:, cs].astype(BF16), wpg_ref[gi]) * psc_ref[:, cs]
        pm.append(pmg.astype(BF16))
    br_pool = _dot(jnp.concatenate(pm, axis=-1), wbp_ref[...])

    mg = mg_ref[...].astype(F32)
    merged = _sigmoid(mg[:, 0:D_MODEL]) * br_gla + _sigmoid(mg[:, D_MODEL:MG_W]) * br_pool
    m = _dot(merged.astype(BF16), wout_ref[...])
    x1 = x_ref[...] + gate1 * m
    x1_ref[...] = x1
    h2 = _rms(x1) * n2g_ref[...]
    h2 = h2 * (1.0 + scale2) + shift2
    h2_ref[...] = h2

    logits = _dot3(h2, wr_ref[...]) + br_ref[...]
    lane = lax.broadcasted_iota(I32, (TOK_TILE, LANES), 1)
    lane_f = lane.astype(F32)
    neg = jnp.float32(-jnp.inf)
    cur = jnp.where(lane < N_EXPERTS, logits, neg)
    vals, idxs = [], []
    for _ in range(TOP_K):
        mx = jnp.max(cur, axis=-1, keepdims=True)
        ix = jnp.min(jnp.where(cur == mx, lane_f, float(LANES)), axis=-1, keepdims=True)
        vals.append(mx)
        idxs.append(ix)
        cur = jnp.where(lane_f == ix, neg, cur)
    ex = [jnp.exp(vv - vals[0]) for vv in vals]
    tot = ex[0] + ex[1] + ex[2] + ex[3]
    idx_out = jnp.zeros((TOK_TILE, LANES), F32)
    w_out = jnp.zeros((TOK_TILE, LANES), F32)
    for kk in range(TOP_K):
        idx_out = jnp.where(lane == kk, idxs[kk], idx_out)
        w_out = jnp.where(lane == kk, ex[kk] / tot, w_out)
    idx_ref[...] = idx_out.astype(I32)
    tw_ref[...] = w_out


def _post_call(x, mod3, o_f, o_b, og, pooled, mg, gng, wpg, psc, wbg, wbp, wout, n2g, wr, br):
    n_tiles = N_TOK // TOK_TILE
    row = lambda t: (t, 0)
    const = lambda t: (0, 0)
    return pl.pallas_call(
        _post_kernel,
        out_shape=[
            jax.ShapeDtypeStruct((N_TOK, D_MODEL), F32),
            jax.ShapeDtypeStruct((N_TOK, D_MODEL), F32),
            jax.ShapeDtypeStruct((N_TOK, LANES), I32),
            jax.ShapeDtypeStruct((N_TOK, LANES), F32),
        ],
        grid=(n_tiles,),
        in_specs=[
            pl.BlockSpec((TOK_TILE, D_MODEL), row),
            pl.BlockSpec((1, 1, N_MOD * D_MODEL), lambda t: (_mod_row(t), 0, 0)),
            pl.BlockSpec((TOK_TILE, V_W), row),
            pl.BlockSpec((TOK_TILE, V_W), row),
            pl.BlockSpec((TOK_TILE, V_W), row),
            pl.BlockSpec((TOK_TILE, POOL_W), row),
            pl.BlockSpec((TOK_TILE, MG_W), row),
            pl.BlockSpec((1, V_W), const),
            pl.BlockSpec((POOL_GROUPS, POOL_GROUP_DIM, POOL_GROUP_DIM), lambda t: (0, 0, 0)),
            pl.BlockSpec((1, POOL_W), const),
            pl.BlockSpec((V_W, D_MODEL), const),
            pl.BlockSpec((POOL_W, D_MODEL), const),
            pl.BlockSpec((D_MODEL, D_MODEL), const),
            pl.BlockSpec((1, D_MODEL), const),
            pl.BlockSpec((D_MODEL, LANES), const),
            pl.BlockSpec((1, LANES), const),
        ],
        out_specs=[
            pl.BlockSpec((TOK_TILE, D_MODEL), row),
            pl.BlockSpec((TOK_TILE, D_MODEL), row),
            pl.BlockSpec((TOK_TILE, LANES), row),
            pl.BlockSpec((TOK_TILE, LANES), row),
        ],
        compiler_params=_params(("arbitrary",)),
        name="post",
    )(x, mod3, o_f, o_b, og, pooled, mg, gng, wpg, psc, wbg, wbp, wout, n2g, wr, br)


def _route_kernel(idx_ref, rank_ref, cnt_ref, carry_ref):
    t = pl.program_id(0)

    @pl.when(t == 0)
    def _():
        carry_ref[...] = jnp.zeros((1, LANES), F32)

    idx = idx_ref[...]
    lane = lax.broadcasted_iota(I32, (ROUTE_TILE, LANES), 1)
    sel = [lane == idx[:, kk:kk + 1] for kk in range(TOP_K)]
    onehot = jnp.zeros((ROUTE_TILE, LANES), F32)
    for kk in range(TOP_K):
        onehot = onehot + jnp.where(sel[kk], 1.0, 0.0)
    row = lax.broadcasted_iota(I32, (ROUTE_TILE, ROUTE_TILE), 0)
    col = lax.broadcasted_iota(I32, (ROUTE_TILE, ROUTE_TILE), 1)
    strict = jnp.where(col < row, 1.0, 0.0).astype(BF16)
    before = _dot(strict, onehot.astype(BF16)) + carry_ref[...]
    rank = jnp.zeros((ROUTE_TILE, LANES), F32)
    for kk in range(TOP_K):
        rk = jnp.sum(jnp.where(sel[kk], before, 0.0), axis=-1, keepdims=True)
        rank = jnp.where(lane == kk, rk, rank)
    rank_ref[...] = rank.astype(I32)
    carry_ref[...] = carry_ref[...] + jnp.sum(onehot, axis=0, keepdims=True)
    cnt_ref[...] = jnp.broadcast_to(carry_ref[...], (8, LANES))


def _route_call(idx):
    return pl.pallas_call(
        _route_kernel,
        out_shape=[
            jax.ShapeDtypeStruct((N_TOK, LANES), I32),
            jax.ShapeDtypeStruct((8, LANES), F32),
        ],
        grid=(N_TOK // ROUTE_TILE,),
        in_specs=[pl.BlockSpec((ROUTE_TILE, LANES), lambda t: (t, 0))],
        out_specs=[
            pl.BlockSpec((ROUTE_TILE, LANES), lambda t: (t, 0)),
            pl.BlockSpec((8, LANES), lambda t: (0, 0)),
        ],
        scratch_shapes=[pltpu.VMEM((1, LANES), F32)],
        compiler_params=_params(("arbitrary",)),
        name="route",
    )(idx)


def _row_copy(src, dst, sem):
    return pltpu.make_async_copy(src, dst, sem)


def _dispatch_kernel(pend_ref, be_ref, pos_ref, h_ref, hs_ref, zero_ref, sem):
    t = pl.program_id(0)

    @pl.when(t == 0)
    def _():
        zero_ref[...] = jnp.zeros((MOE_BLOCK, D_MODEL), F32)

        def has_padding(b):
            return (b + 1) * MOE_BLOCK >= pend_ref[be_ref[b]]

        def block_copy(b):
            start = pl.multiple_of(b * MOE_BLOCK, MOE_BLOCK)
            return _row_copy(zero_ref, hs_ref.at[pl.ds(start, MOE_BLOCK)], sem)

        def fill(b, carry):
            @pl.when(has_padding(b))
            def _():
                block_copy(b).start()
            return carry

        def fill_wait(b, carry):
            @pl.when(has_padding(b))
            def _():
                block_copy(b).wait()
            return carry

        lax.fori_loop(0, N_SLOT_BLOCKS, fill, 0)
        lax.fori_loop(0, N_SLOT_BLOCKS, fill_wait, 0)

    def issue(i, carry):
        r = i // TOP_K
        _row_copy(h_ref.at[pl.ds(r, 1)], hs_ref.at[pl.ds(pos_ref[0, 0, i], 1)], sem).start()
        return carry

    lax.fori_loop(0, TOK_TILE * TOP_K, issue, 0, unroll=8)

    def drain(i, carry):
        _row_copy(h_ref.at[pl.ds(0, 1)], hs_ref.at[pl.ds(0, 1)], sem).wait()
        return carry

    lax.fori_loop(0, TOK_TILE * TOP_K, drain, 0, unroll=8)


def _dispatch_call(pad_end, block_e, pos3, h2):
    n_tiles = N_TOK // TOK_TILE
    return pl.pallas_call(
        _dispatch_kernel,
        out_shape=jax.ShapeDtypeStruct((N_SLOTS, D_MODEL), F32),
        grid_spec=pltpu.PrefetchScalarGridSpec(
            num_scalar_prefetch=2,
            grid=(n_tiles,),
            in_specs=[
                pl.BlockSpec((1, 1, TOK_TILE * TOP_K), lambda t, pe, be: (t, 0, 0),
                             memory_space=pltpu.SMEM),
                pl.BlockSpec((TOK_TILE, D_MODEL), lambda t, pe, be: (t, 0)),
            ],
            out_specs=pl.BlockSpec(memory_space=pl.ANY),
            scratch_shapes=[pltpu.VMEM((MOE_BLOCK, D_MODEL), F32), pltpu.SemaphoreType.DMA],
        ),
        compiler_params=_params(("arbitrary",)),
        name="dispatch",
    )(pad_end, block_e, pos3, h2)


def _moe_kernel(be_ref, nu_ref, x_ref, wg_ref, bg_ref, wu_ref, bu_ref, wd_ref, bd_ref, y_ref,
                wgb_ref, wub_ref, wdb_ref):
    b = pl.program_id(0)
    n_used = nu_ref[0]
    e = be_ref[b]
    prev = be_ref[jnp.maximum(b - 1, 0)]
    live = b < n_used

    @pl.when(live & ((b == 0) | (e != prev)))
    def _():
        wgb_ref[...] = wg_ref[0].astype(BF16)
        wub_ref[...] = wu_ref[0].astype(BF16)
        wdb_ref[...] = wd_ref[0].astype(BF16)

    @pl.when(live)
    def _():
        x = x_ref[...].astype(BF16)
        gate = jnp.minimum(_dot(x, wgb_ref[...]) + bg_ref[0], SWIGLU_LIMIT)
        up = jnp.clip(_dot(x, wub_ref[...]) + bu_ref[0], -SWIGLU_LIMIT, SWIGLU_LIMIT)
        act = (up + 1.0) * (gate * _sigmoid(SWIGLU_ALPHA * gate))
        y_ref[...] = _dot(act.astype(BF16), wdb_ref[...]) + bd_ref[0]

    @pl.when(jnp.logical_not(live))
    def _():
        y_ref[...] = jnp.zeros((MOE_BLOCK, D_MODEL), F32)


def _moe_call(block_e, n_used, hs, w_gate, b_gate, w_up, b_up, w_down, b_down):
    def blk(b, be, nu):
        return jnp.minimum(b, nu[0] - 1)

    row = lambda b, be, nu: (blk(b, be, nu), 0)
    wsel = lambda b, be, nu: (be[blk(b, be, nu)], 0, 0)
    return pl.pallas_call(
        _moe_kernel,
        out_shape=jax.ShapeDtypeStruct((N_SLOTS, D_MODEL), F32),
        grid_spec=pltpu.PrefetchScalarGridSpec(
            num_scalar_prefetch=2,
            grid=(N_SLOT_BLOCKS,),
            in_specs=[
                pl.BlockSpec((MOE_BLOCK, D_MODEL), row),
                pl.BlockSpec((1, D_MODEL, D_FF), wsel),
                pl.BlockSpec((1, 1, D_FF), wsel),
                pl.BlockSpec((1, D_MODEL, D_FF), wsel),
                pl.BlockSpec((1, 1, D_FF), wsel),
                pl.BlockSpec((1, D_FF, D_MODEL), wsel),
                pl.BlockSpec((1, 1, D_MODEL), wsel),
            ],
            out_specs=pl.BlockSpec((MOE_BLOCK, D_MODEL), lambda b, be, nu: (b, 0)),
            scratch_shapes=[
                pltpu.VMEM((D_MODEL, D_FF), BF16),
                pltpu.VMEM((D_MODEL, D_FF), BF16),
                pltpu.VMEM((D_FF, D_MODEL), BF16),
            ],
        ),
        compiler_params=_params(("arbitrary",)),
        name="moe",
    )(block_e, n_used, hs, w_gate, b_gate, w_up, b_up, w_down, b_down)


def _combine_kernel(pos_ref, x1_ref, mod_ref, tw_ref, fg_ref, y_hbm, out_ref, buf_ref, sem):
    def issue(i, carry):
        r = i // TOP_K
        kk = i % TOP_K
        _row_copy(y_hbm.at[pl.ds(pos_ref[0, 0, i], 1)], buf_ref.at[kk, pl.ds(r, 1)], sem).start()
        return carry

    lax.fori_loop(0, TOK_TILE * TOP_K, issue, 0, unroll=8)

    def drain(i, carry):
        _row_copy(y_hbm.at[pl.ds(0, 1)], buf_ref.at[0, pl.ds(0, 1)], sem).wait()
        return carry

    lax.fori_loop(0, TOK_TILE * TOP_K, drain, 0, unroll=8)

    tw = tw_ref[...]
    f = jnp.zeros((TOK_TILE, D_MODEL), F32)
    for kk in range(TOP_K):
        f = f + buf_ref[kk] * tw[:, kk:kk + 1]
    gate2 = mod_ref[0][:, 5 * D_MODEL:6 * D_MODEL]
    x2 = x1_ref[...] + gate2 * f
    out_ref[...] = _rms(x2) * fg_ref[...]


def _combine_call(pos3, x1, mod3, tw, final_g, y):
    n_tiles = N_TOK // TOK_TILE
    return pl.pallas_call(
        _combine_kernel,
        out_shape=jax.ShapeDtypeStruct((N_TOK, D_MODEL), F32),
        grid=(n_tiles,),
        in_specs=[
            pl.BlockSpec((1, 1, TOK_TILE * TOP_K), lambda t: (t, 0, 0), memory_space=pltpu.SMEM),
            pl.BlockSpec((TOK_TILE, D_MODEL), lambda t: (t, 0)),
            pl.BlockSpec((1, 1, N_MOD * D_MODEL), lambda t: (_mod_row(t), 0, 0)),
            pl.BlockSpec((TOK_TILE, LANES), lambda t: (t, 0)),
            pl.BlockSpec((1, D_MODEL), lambda t: (0, 0)),
            pl.BlockSpec(memory_space=pl.ANY),
        ],
        out_specs=pl.BlockSpec((TOK_TILE, D_MODEL), lambda t: (t, 0)),
        scratch_shapes=[pltpu.VMEM((TOP_K, TOK_TILE, D_MODEL), F32), pltpu.SemaphoreType.DMA],
        compiler_params=_params(("arbitrary",)),
        name="combine",
    )(pos3, x1, mod3, tw, final_g, y)


def kernel(x_prompt, x_sample, state_gla_fwd, state_gla_bwd, c, c_ctx, norm1_g, w_mod, b_mod, w_in,
           w_alpha, b_alpha, gla_norm_g, w_pool_grp, pool_scale, w_branch_gla, w_branch_pool, w_out,
           norm2_g, w_router, b_router, w_gate, b_gate, w_up, b_up, w_down, b_down, final_norm_g):
    l = 0
    x = jnp.concatenate([x_prompt.reshape(N_CTX, D_MODEL), x_sample.reshape(N_LAT, D_MODEL)], axis=0)

    cvec = jnp.concatenate([c_ctx[None, :], c, jnp.zeros((8 - 1 - DEC_BATCH, D_MODEL), F32)], axis=0)
    mod = _mod_call(cvec, w_mod[l], b_mod[l][None, :])
    mod3 = mod.reshape(8, 1, N_MOD * D_MODEL)

    w_in_b = w_in[l].astype(BF16)
    w_main = w_in_b[:, :MAIN_W]
    w_alr = w_in_b[:, MAIN_W:MAIN_W + ALR_W]
    w_xp = w_in_b[:, MAIN_W + ALR_W:MAIN_W + ALR_W + POOL_W]
    w_mg = w_in_b[:, MAIN_W + ALR_W + POOL_W:]
    q, k, v, og, alr, xp, mg = _inproj_call(x, mod3, norm1_g[l][None, :], w_main, w_alr, w_xp, w_mg)

    zpad = jnp.zeros((GLA_LOWRANK, QK_W), F32)
    wa_f = jnp.concatenate([w_alpha[l, 0], zpad], axis=0)
    wa_b = jnp.concatenate([zpad, w_alpha[l, 1]], axis=0)
    ba_f = b_alpha[l, 0][None, :]
    ba_b = b_alpha[l, 1][None, :]
    ctx = dict(row0=0, n_seq=BATCH, n_chunk=SEQ // GLA_CHUNK, want_state=True)
    lat = dict(row0=N_CTX, n_seq=DEC_BATCH, n_chunk=DEC_SEQ // GLA_CHUNK, want_state=False)
    of_c, s_f = _gla_call(q, k, v, alr, wa_f, ba_f, None, rev=False, **ctx)
    ob_c, s_b = _gla_call(q, k, v, alr, wa_b, ba_b, None, rev=True, **ctx)
    (of_l,) = _gla_call(q, k, v, alr, wa_f, ba_f, state_gla_fwd[:, l], rev=False, **lat)
    (ob_l,) = _gla_call(q, k, v, alr, wa_b, ba_b, state_gla_bwd[:, l], rev=True, **lat)
    o_f = jnp.concatenate([of_c, of_l], axis=0)
    o_b = jnp.concatenate([ob_c, ob_l], axis=0)

    pooled = jnp.concatenate([_pool_ctx_call(xp), _pool_lat_call(xp)], axis=0)

    w_router_pad = jnp.pad(w_router[l], ((0, 0), (0, LANES - N_EXPERTS)))
    b_router_pad = jnp.pad(b_router[l], (0, LANES - N_EXPERTS))[None, :]
    x1, h2, top_idx, top_w = _post_call(
        x, mod3, o_f, o_b, og, pooled, mg,
        gla_norm_g[l].reshape(1, V_W), w_pool_grp[l].astype(BF16), pool_scale[l][None, :],
        w_branch_gla[l].astype(BF16), w_branch_pool[l].astype(BF16), w_out[l].astype(BF16),
        norm2_g[l][None, :], w_router_pad, b_router_pad)

    rank, cnt = _route_call(top_idx)
    counts = cnt[0, :N_EXPERTS].astype(I32)
    padded = (counts + MOE_BLOCK - 1) // MOE_BLOCK * MOE_BLOCK
    pad_end = jnp.cumsum(padded)
    pad_start = pad_end - padded
    block_e = jnp.minimum(
        jnp.searchsorted(pad_end, jnp.arange(N_SLOT_BLOCKS, dtype=I32) * MOE_BLOCK, side='right'),
        N_EXPERTS - 1).astype(I32)
    n_used = (pad_end[-1:] // MOE_BLOCK).astype(I32)
    pos = pad_start[top_idx[:, :TOP_K]] + rank[:, :TOP_K]
    pos3 = pos.reshape(N_TOK // TOK_TILE, 1, TOK_TILE * TOP_K).astype(I32)
    hs = _dispatch_call(pad_end.astype(I32), block_e, pos3, h2)
    y = _moe_call(block_e, n_used, hs,
                  w_gate[l], b_gate[l][:, None, :], w_up[l], b_up[l][:, None, :],
                  w_down[l], b_down[l][:, None, :])
    out = _combine_call(pos3, x1, mod3, top_w, final_norm_g[None, :], y)

    y_prompt = out[:N_CTX].reshape(BATCH, SEQ, D_MODEL)
    y_sample = out[N_CTX:].reshape(DEC_BATCH, DEC_SEQ, D_MODEL)
    return (y_prompt, y_sample, s_f[:, None], s_b[:, None])
```

```python
import functools

import jax
import jax.numpy as jnp
from jax import lax
from jax.experimental import pallas as pl
from jax.experimental.pallas import tpu as pltpu

F32 = jnp.float32
BF16 = jnp.bfloat16
I32 = jnp.int32

D_MODEL = 1024
BATCH = 32
SEQ = 256
DEC_BATCH = 4
DEC_SEQ = 2048
GRID_W = 64
GLA_HEADS = 4
GLA_DK = 128
GLA_DV = 256
GLA_LOWRANK = 16
GLA_TAU = 16.0
GLA_CHUNK = 64
POOL_GROUPS = 4
POOL_GROUP_DIM = 128
POOL_WINDOWS = (2, 4, 8, 16)
N_EXPERTS = 32
TOP_K = 4
D_FF = 1024
SWIGLU_LIMIT = 7.0
SWIGLU_ALPHA = 1.702
MOE_BLOCK = 256
NORM_EPS = 1e-6
N_MOD = 6

QK_W = GLA_HEADS * GLA_DK
V_W = GLA_HEADS * GLA_DV
POOL_W = POOL_GROUPS * POOL_GROUP_DIM
MAIN_W = 2 * QK_W + 2 * V_W
ALR_W = 2 * GLA_LOWRANK
MG_W = 2 * D_MODEL

N_CTX = BATCH * SEQ
N_LAT = DEC_BATCH * DEC_SEQ
N_TOK = N_CTX + N_LAT
N_SLOT_BLOCKS = -(-(N_TOK * TOP_K + N_EXPERTS * (MOE_BLOCK - 1)) // MOE_BLOCK)
N_SLOTS = N_SLOT_BLOCKS * MOE_BLOCK

LANES = 128
TOK_TILE = 256
N_TILES = N_TOK // TOK_TILE
CTX_TILES = N_CTX // TOK_TILE
LAT_TILES_PER_SEQ = DEC_SEQ // TOK_TILE
ROUTE_TILE = 512
VMEM_LIMIT = 56 * 1024 * 1024

GLA_SEQS = 4
CTX_CHUNKS = SEQ // GLA_CHUNK
LAT_CHUNKS = DEC_SEQ // GLA_CHUNK
CHUNKS_PER_TILE = TOK_TILE // GLA_CHUNK
GLA_CTX_STEPS = (BATCH // GLA_SEQS) * CTX_CHUNKS
TILE_GRID = 8

NT_DIMS = (((1,), (1,)), ((), ()))
TN_DIMS = (((0,), (0,)), ((), ()))

assert DEC_BATCH == GLA_SEQS and SEQ == TOK_TILE and N_TILES == TILE_GRID * TILE_GRID


def _params(semantics, vmem=VMEM_LIMIT):
    return pltpu.CompilerParams(dimension_semantics=semantics, vmem_limit_bytes=vmem)


def _split_bf16(a):
    hi = a.astype(BF16)
    lo = (a - hi.astype(F32)).astype(BF16)
    return hi, lo


def _dot(a, b):
    return jnp.dot(a, b, preferred_element_type=F32)


def _dot3(a, b):
    a_hi, a_lo = _split_bf16(a)
    b_hi, b_lo = _split_bf16(b)
    return _dot(a_hi, b_hi) + _dot(a_lo, b_hi) + _dot(a_hi, b_lo)


def _sigmoid(x):
    return 1.0 / (1.0 + jnp.exp(-x))


def _rms(x):
    return x * lax.rsqrt(jnp.mean(x * x, axis=-1, keepdims=True) + NORM_EPS)


def _mod_row(t):
    return jnp.where(t < CTX_TILES, 0, 1 + (t - CTX_TILES) // LAT_TILES_PER_SEQ)


def _store_tile(t):
    u = t - CTX_TILES
    return jnp.where(t < CTX_TILES, t,
                     CTX_TILES + DEC_BATCH * (u % LAT_TILES_PER_SEQ) + u // LAT_TILES_PER_SEQ)


def _ctx_tile(t):
    return jnp.minimum(t, CTX_TILES - 1)


def _lat_tile(t):
    return jnp.maximum(t - CTX_TILES, 0)


def _mod_kernel(c_ref, w_ref, b_ref, o_ref):
    c = c_ref[...]
    o_ref[...] = _dot3(c * _sigmoid(c), w_ref[...]) + b_ref[...]


def _mod_call(cvec, w_mod, b_mod):
    rows = cvec.shape[0]
    return pl.pallas_call(
        _mod_kernel,
        out_shape=jax.ShapeDtypeStruct((rows, N_MOD * D_MODEL), F32),
        grid=(N_MOD,),
        in_specs=[
            pl.BlockSpec((rows, D_MODEL), lambda j: (0, 0)),
            pl.BlockSpec((D_MODEL, D_MODEL), lambda j: (0, j)),
            pl.BlockSpec((1, D_MODEL), lambda j: (0, j)),
        ],
        out_specs=pl.BlockSpec((rows, D_MODEL), lambda j: (0, j)),
        compiler_params=_params(("arbitrary",)),
        name="mod",
    )(cvec, w_mod, b_mod)


def _inproj_kernel(xc_ref, xl_ref, mod_ref, g_ref, wmain_ref, walr_ref, wxp_ref, wmg_ref,
                   q_ref, k_ref, v_ref, og_ref, alr_ref, xp_ref, mg_ref):
    t = pl.program_id(0)
    x = jnp.where(t < CTX_TILES, xc_ref[...], xl_ref[...])
    mod = mod_ref[0]
    shift1 = mod[:, 0:D_MODEL]
    scale1 = mod[:, D_MODEL:2 * D_MODEL]
    h = _rms(x) * g_ref[...]
    h = (h * (1.0 + scale1) + shift1).astype(BF16)
    z = _dot(h, wmain_ref[...])
    q_ref[...] = (z[:, 0:QK_W] * (GLA_DK ** -0.5)).astype(BF16)
    k_ref[...] = z[:, QK_W:2 * QK_W].astype(BF16)
    v_ref[...] = z[:, 2 * QK_W:2 * QK_W + V_W].astype(BF16)
    og_ref[...] = z[:, 2 * QK_W + V_W:MAIN_W].astype(BF16)
    alr_ref[...] = _dot(h, walr_ref[...])
    xp_ref[...] = _dot(h, wxp_ref[...])
    mg_ref[...] = _dot(h, wmg_ref[...]).astype(BF16)


def _inproj_call(x_ctx, x_lat, mod3, norm1_g, w_main, w_alr, w_xp, w_mg):
    const = lambda t: (0, 0)
    stored = lambda t: (_store_tile(t), 0)
    widths = (QK_W, QK_W, V_W, V_W, ALR_W, POOL_W, MG_W)
    dtypes = (BF16, BF16, BF16, BF16, F32, F32, BF16)
    return pl.pallas_call(
        _inproj_kernel,
        out_shape=[jax.ShapeDtypeStruct((N_TOK, w), dt) for w, dt in zip(widths, dtypes)],
        grid=(N_TILES,),
        in_specs=[
            pl.BlockSpec((TOK_TILE, D_MODEL), lambda t: (_ctx_tile(t), 0)),
            pl.BlockSpec((TOK_TILE, D_MODEL), lambda t: (_lat_tile(t), 0)),
            pl.BlockSpec((1, 1, N_MOD * D_MODEL), lambda t: (_mod_row(t), 0, 0)),
            pl.BlockSpec((1, D_MODEL), const),
            pl.BlockSpec((D_MODEL, MAIN_W), const),
            pl.BlockSpec((D_MODEL, ALR_W), const),
            pl.BlockSpec((D_MODEL, POOL_W), const),
            pl.BlockSpec((D_MODEL, MG_W), const),
        ],
        out_specs=[pl.BlockSpec((TOK_TILE, w), stored) for w in widths],
        compiler_params=_params(("arbitrary",)),
        name="inproj",
    )(x_ctx, x_lat, mod3, norm1_g, w_main, w_alr, w_xp, w_mg)


def _gla_direction(q_ref, k_ref, v_ref, alr_ref, wa_ref, ba_ref, o_ref, st_ref, slot0, rev):
    rows = GLA_SEQS * GLA_CHUNK
    alr = jnp.concatenate([alr_ref[0, s] for s in range(GLA_SEQS)], axis=0)
    a = _dot3(alr, wa_ref[...]) + ba_ref[...]
    g = (jnp.minimum(a, 0.0) - jnp.log(1.0 + jnp.exp(-jnp.abs(a)))) * (1.0 / GLA_TAU)

    row = lax.broadcasted_iota(I32, (rows, rows), 0)
    col = lax.broadcasted_iota(I32, (rows, rows), 1)
    same = (row // GLA_CHUNK) == (col // GLA_CHUNK)
    tri_all = same & ((col >= row) if rev else (col <= row))
    tri_b = jnp.where(tri_all, 1.0, 0.0).astype(BF16)
    g_hi, g_lo = _split_bf16(g)
    bcum_all = _dot(tri_b, g_hi) + _dot(tri_b, g_lo)

    r64 = lax.broadcasted_iota(I32, (GLA_CHUNK, GLA_CHUNK), 0)
    c64 = lax.broadcasted_iota(I32, (GLA_CHUNK, GLA_CHUNK), 1)
    tri = (c64 >= r64) if rev else (c64 <= r64)

    for s in range(GLA_SEQS):
        bcum = bcum_all[s * GLA_CHUNK:(s + 1) * GLA_CHUNK]
        blast = bcum[0:1] if rev else bcum[GLA_CHUNK - 1:GLA_CHUNK]
        bmid = bcum[GLA_CHUNK // 2:GLA_CHUNK // 2 + 1]
        e_q = jnp.exp(bcum - bmid)
        e_k = jnp.exp(bmid - bcum)
        e_in = jnp.exp(bcum)
        e_out = jnp.exp(blast - bcum)
        e_last = jnp.exp(blast)
        q = q_ref[0, s].astype(F32)
        k = k_ref[0, s].astype(F32)
        for h in range(GLA_HEADS):
            ks = slice(h * GLA_DK, (h + 1) * GLA_DK)
            vs = slice(h * GLA_DV, (h + 1) * GLA_DV)
            qh = q[:, ks]
            kh = k[:, ks]
            vh = v_ref[0, s, :, vs]
            att = lax.dot_general((qh * e_q[:, ks]).astype(BF16), (kh * e_k[:, ks]).astype(BF16),
                                  NT_DIMS, preferred_element_type=F32)
            att = jnp.where(tri, att, 0.0).astype(BF16)
            st = st_ref[slot0 + s, h]
            o_inter = lax.dot_general((qh * e_in[:, ks]).astype(BF16), st.astype(BF16),
                                      NT_DIMS, preferred_element_type=F32)
            o_ref[0, s, :, vs] = o_inter + _dot(att, vh)
            upd = lax.dot_general(vh, (kh * e_out[:, ks]).astype(BF16), TN_DIMS,
                                  preferred_element_type=F32)
            st_ref[slot0 + s, h] = st * e_last[:, ks] + upd


def _gla_kernel(qf_ref, kf_ref, vf_ref, af_ref, qb_ref, kb_ref, vb_ref, ab_ref,
                waf_ref, baf_ref, wab_ref, bab_ref, s0f_ref, s0b_ref,
                of_ref, ob_ref, sf_ref, sb_ref, st_ref):
    i = pl.program_id(0)
    is_ctx = i < GLA_CTX_STEPS
    chunk = jnp.where(is_ctx, i % CTX_CHUNKS, i - GLA_CTX_STEPS)

    @pl.when(is_ctx & (chunk == 0))
    def _():
        st_ref[...] = jnp.zeros(st_ref.shape, F32)

    @pl.when(i == GLA_CTX_STEPS)
    def _():
        for s in range(GLA_SEQS):
            for h in range(GLA_HEADS):
                st_ref[s, h] = s0f_ref[s, h].T
                st_ref[GLA_SEQS + s, h] = s0b_ref[s, h].T

    _gla_direction(qf_ref, kf_ref, vf_ref, af_ref, waf_ref, baf_ref, of_ref, st_ref, 0, False)
    _gla_direction(qb_ref, kb_ref, vb_ref, ab_ref, wab_ref, bab_ref, ob_ref, st_ref, GLA_SEQS, True)

    @pl.when(is_ctx & (chunk == CTX_CHUNKS - 1))
    def _():
        for s in range(GLA_SEQS):
            for h in range(GLA_HEADS):
                sf_ref[s, h] = st_ref[s, h].T
                sb_ref[s, h] = st_ref[GLA_SEQS + s, h].T


def _gla_block(i, rev):
    is_ctx = i < GLA_CTX_STEPS
    group = i // CTX_CHUNKS
    c_ctx = i % CTX_CHUNKS
    c_lat = i - GLA_CTX_STEPS
    if rev:
        c_ctx = CTX_CHUNKS - 1 - c_ctx
        c_lat = LAT_CHUNKS - 1 - c_lat
    j = c_lat // CHUNKS_PER_TILE
    per_row = TILE_GRID // GLA_SEQS
    a = jnp.where(is_ctx, group // per_row, CTX_TILES // TILE_GRID + j // per_row)
    b = jnp.where(is_ctx, group % per_row, j % per_row)
    c = jnp.where(is_ctx, c_ctx, c_lat % CHUNKS_PER_TILE)
    return (a, b, c, 0)


def _gla_call(q, k, v, alr, wa_f, ba_f, wa_b, ba_b, s0_f, s0_b):
    def view(arr):
        return arr.reshape(TILE_GRID, TILE_GRID, TOK_TILE, arr.shape[-1])

    def spec(width, rev):
        return pl.BlockSpec((1, GLA_SEQS, GLA_CHUNK, width), lambda i: _gla_block(i, rev))

    const = lambda i: (0, 0)
    st_block = (GLA_SEQS, GLA_HEADS, GLA_DK, GLA_DV)
    whole_state = pl.BlockSpec(st_block, lambda i: (0, 0, 0, 0))
    ctx_state = pl.BlockSpec(
        st_block, lambda i: (jnp.minimum(i // CTX_CHUNKS, BATCH // GLA_SEQS - 1), 0, 0, 0))
    in_specs = []
    for rev in (False, True):
        in_specs += [spec(QK_W, rev), spec(QK_W, rev), spec(V_W, rev), spec(ALR_W, rev)]
    in_specs += [pl.BlockSpec((ALR_W, QK_W), const), pl.BlockSpec((1, QK_W), const)] * 2
    in_specs += [whole_state, whole_state]
    o_shape = jax.ShapeDtypeStruct((TILE_GRID, TILE_GRID, TOK_TILE, V_W), F32)
    s_shape = jax.ShapeDtypeStruct((BATCH, GLA_HEADS, GLA_DK, GLA_DV), F32)
    qv, kv, vv, av = view(q), view(k), view(v), view(alr)
    o_f, o_b, s_f, s_b = pl.pallas_call(
        _gla_kernel,
        out_shape=[o_shape, o_shape, s_shape, s_shape],
        grid=(GLA_CTX_STEPS + LAT_CHUNKS,),
        in_specs=in_specs,
        out_specs=[spec(V_W, False), spec(V_W, True), ctx_state, ctx_state],
        scratch_shapes=[pltpu.VMEM((2 * GLA_SEQS, GLA_HEADS, GLA_DV, GLA_DK), F32)],
        compiler_params=_params(("arbitrary",)),
        name="gla",
    )(qv, kv, vv, av, qv, kv, vv, av, wa_f, ba_f, wa_b, ba_b, s0_f, s0_b)
    return o_f.reshape(N_TOK, V_W), o_b.reshape(N_TOK, V_W), s_f, s_b


def _band(n, w, block):
    row = lax.broadcasted_iota(I32, (n, n), 0)
    col = lax.broadcasted_iota(I32, (n, n), 1)
    inside = (col >= row - w // 2) & (col <= row + w // 2 - 1)
    if block < n:
        inside = inside & ((row // block) == (col // block))
    return jnp.where(inside, 1.0, 0.0).astype(BF16)


def _win_count(p, n, w):
    return jnp.minimum(p + w // 2 - 1, n - 1) - jnp.maximum(p - w // 2, 0) + 1


def _pool_ctx_kernel(x_ref, o_ref):
    p = lax.broadcasted_iota(I32, (SEQ, POOL_GROUP_DIM), 0)
    for gi, w in enumerate(POOL_WINDOWS):
        cs = slice(gi * POOL_GROUP_DIM, (gi + 1) * POOL_GROUP_DIM)
        x = x_ref[:, cs]
        hi, lo = _split_bf16(x)
        band = _band(SEQ, w, SEQ)
        s = _dot(band, hi) + _dot(band, lo)
        cnt = _win_count(p, SEQ, w).astype(F32)
        o_ref[:, cs] = s / cnt - x


def _pool_ctx_call(xp):
    spec = pl.BlockSpec((SEQ, POOL_W), lambda b: (b, 0))
    return pl.pallas_call(
        _pool_ctx_kernel,
        out_shape=jax.ShapeDtypeStruct((N_CTX, POOL_W), F32),
        grid=(BATCH,),
        in_specs=[spec],
        out_specs=spec,
        compiler_params=_params(("arbitrary",)),
        name="pool_ctx",
    )(xp)


POOL_HALO = (max(POOL_WINDOWS) // 2) * GRID_W


def _pool_lat_kernel(x_ref, o_ref, pad_ref):
    rows = DEC_SEQ // GRID_W
    p = lax.broadcasted_iota(I32, (DEC_SEQ, POOL_GROUP_DIM), 0)
    r = p // GRID_W
    cidx = p % GRID_W
    zeros = jnp.zeros((POOL_HALO, POOL_GROUP_DIM), F32)
    pad_ref[0:POOL_HALO, :] = zeros
    pad_ref[POOL_HALO + DEC_SEQ:2 * POOL_HALO + DEC_SEQ, :] = zeros
    for gi, w in enumerate(POOL_WINDOWS):
        cs = slice(gi * POOL_GROUP_DIM, (gi + 1) * POOL_GROUP_DIM)
        band = _band(TOK_TILE, w, GRID_W)
        for t in range(LAT_TILES_PER_SEQ):
            hi, lo = _split_bf16(x_ref[t, 0, :, cs])
            pad_ref[POOL_HALO + t * TOK_TILE:POOL_HALO + (t + 1) * TOK_TILE, :] = (
                _dot(band, hi) + _dot(band, lo))
        acc = jnp.zeros((DEC_SEQ, POOL_GROUP_DIM), F32)
        for dr in range(-(w // 2), w // 2):
            start = POOL_HALO + dr * GRID_W
            acc = acc + pad_ref[start:start + DEC_SEQ, :]
        cnt = (_win_count(r, rows, w) * _win_count(cidx, GRID_W, w)).astype(F32)
        pooled = acc / cnt
        for t in range(LAT_TILES_PER_SEQ):
            rs = slice(t * TOK_TILE, (t + 1) * TOK_TILE)
            o_ref[t, 0, :, cs] = pooled[rs] - x_ref[t, 0, :, cs]


def _pool_lat_call(xp):
    view = xp.reshape(N_TILES // DEC_BATCH, DEC_BATCH, TOK_TILE, POOL_W)
    blk = (LAT_TILES_PER_SEQ, 1, TOK_TILE, POOL_W)
    out = pl.pallas_call(
        _pool_lat_kernel,
        out_shape=jax.ShapeDtypeStruct((LAT_TILES_PER_SEQ, DEC_BATCH, TOK_TILE, POOL_W), F32),
        grid=(DEC_BATCH,),
        in_specs=[pl.BlockSpec(blk, lambda s: (CTX_TILES // DEC_BATCH // LAT_TILES_PER_SEQ, s, 0, 0))],
        out_specs=pl.BlockSpec(blk, lambda s: (0, s, 0, 0)),
        scratch_shapes=[pltpu.VMEM((DEC_SEQ + 2 * POOL_HALO, POOL_GROUP_DIM), F32)],
        compiler_params=_params(("arbitrary",)),
        name="pool_lat",
    )(view)
    return out.reshape(N_LAT, POOL_W)


def _post_kernel(xc_ref, xl_ref, mod_ref, of_ref, ob_ref, og_ref, pc_ref, pl_ref, mg_ref, gng_ref,
                 wpg_ref, psc_ref, wbg_ref, wbp_ref, wout_ref, n2g_ref, wr_ref, br_ref,
                 x1_ref, h2_ref, idx_ref, tw_ref):
    t = pl.program_id(0)
    is_ctx = t < CTX_TILES
    x = jnp.where(is_ctx, xc_ref[...], xl_ref[...])
    pooled = jnp.where(is_ctx, pc_ref[...], pl_ref[...])
    mod = mod_ref[0]
    gate1 = mod[:, 2 * D_MODEL:3 * D_MODEL]
    shift2 = mod[:, 3 * D_MODEL:4 * D_MODEL]
    scale2 = mod[:, 4 * D_MODEL:5 * D_MODEL]

    o = of_ref[...] + ob_ref[...]
    og = og_ref[...].astype(F32)
    gated = []
    for h in range(GLA_HEADS):
        vs = slice(h * GLA_DV, (h + 1) * GLA_DV)
        oh = _rms(o[:, vs]) * gng_ref[:, vs]
        ogh = og[:, vs]
        gated.append((oh * (ogh * _sigmoid(ogh))).astype(BF16))
    br_gla = _dot(jnp.concatenate(gated, axis=-1), wbg_ref[...])

    pm = []
    for gi in range(POOL_GROUPS):
        cs = slice(gi * POOL_GROUP_DIM, (gi + 1) * POOL_GROUP_DIM)
        pmg = _dot(pooled[:, cs].astype(BF16), wpg_ref[gi]) * psc_ref[:, cs]
        pm.append(pmg.astype(BF16))
    br_pool = _dot(jnp.concatenate(pm, axis=-1), wbp_ref[...])

    mg = mg_ref[...].astype(F32)
    merged = _sigmoid(mg[:, 0:D_MODEL]) * br_gla + _sigmoid(mg[:, D_MODEL:MG_W]) * br_pool
    m = _dot(merged.astype(BF16), wout_ref[...])
    x1 = x + gate1 * m
    x1_ref[...] = x1
    h2 = _rms(x1) * n2g_ref[...]
    h2 = h2 * (1.0 + scale2) + shift2
    h2_ref[...] = h2

    logits = _dot3(h2, wr_ref[...]) + br_ref[...]
    lane = lax.broadcasted_iota(I32, (TOK_TILE, LANES), 1)
    lane_f = lane.astype(F32)
    neg = jnp.float32(-jnp.inf)
    cur = jnp.where(lane < N_EXPERTS, logits, neg)
    vals, idxs = [], []
    for _ in range(TOP_K):
        mx = jnp.max(cur, axis=-1, keepdims=True)
        ix = jnp.min(jnp.where(cur == mx, lane_f, float(LANES)), axis=-1, keepdims=True)
        vals.append(mx)
        idxs.append(ix)
        cur = jnp.where(lane_f == ix, neg, cur)
    ex = [jnp.exp(vv - vals[0]) for vv in vals]
    tot = ex[0] + ex[1] + ex[2] + ex[3]
    idx_out = jnp.zeros((TOK_TILE, LANES), F32)
    w_out = jnp.zeros((TOK_TILE, LANES), F32)
    for kk in range(TOP_K):
        idx_out = jnp.where(lane == kk, idxs[kk], idx_out)
        w_out = jnp.where(lane == kk, ex[kk] / tot, w_out)
    idx_ref[...] = idx_out.astype(I32)
    tw_ref[...] = w_out


def _post_call(x_ctx, x_lat, mod3, o_f, o_b, og, pooled_c, pooled_l, mg, gng, wpg, psc, wbg, wbp,
               wout, n2g, wr, br):
    row = lambda t: (t, 0)
    const = lambda t: (0, 0)
    stored = lambda t: (_store_tile(t), 0)
    return pl.pallas_call(
        _post_kernel,
        out_shape=[
            jax.ShapeDtypeStruct((N_TOK, D_MODEL), F32),
            jax.ShapeDtypeStruct((N_TOK, D_MODEL), F32),
            jax.ShapeDtypeStruct((N_TOK, LANES), I32),
            jax.ShapeDtypeStruct((N_TOK, LANES), F32),
        ],
        grid=(N_TILES,),
        in_specs=[
            pl.BlockSpec((TOK_TILE, D_MODEL), lambda t: (_ctx_tile(t), 0)),
            pl.BlockSpec((TOK_TILE, D_MODEL), lambda t: (_lat_tile(t), 0)),
            pl.BlockSpec((1, 1, N_MOD * D_MODEL), lambda t: (_mod_row(t), 0, 0)),
            pl.BlockSpec((TOK_TILE, V_W), stored),
            pl.BlockSpec((TOK_TILE, V_W), stored),
            pl.BlockSpec((TOK_TILE, V_W), stored),
            pl.BlockSpec((TOK_TILE, POOL_W), lambda t: (_ctx_tile(t), 0)),
            pl.BlockSpec((TOK_TILE, POOL_W),
                         lambda t: (jnp.maximum(_store_tile(t) - CTX_TILES, 0), 0)),
            pl.BlockSpec((TOK_TILE, MG_W), stored),
            pl.BlockSpec((1, V_W), const),
            pl.BlockSpec((POOL_GROUPS, POOL_GROUP_DIM, POOL_GROUP_DIM), lambda t: (0, 0, 0)),
            pl.BlockSpec((1, POOL_W), const),
            pl.BlockSpec((V_W, D_MODEL), const),
            pl.BlockSpec((POOL_W, D_MODEL), const),
            pl.BlockSpec((D_MODEL, D_MODEL), const),
            pl.BlockSpec((1, D_MODEL), const),
            pl.BlockSpec((D_MODEL, LANES), const),
            pl.BlockSpec((1, LANES), const),
        ],
        out_specs=[
            pl.BlockSpec((TOK_TILE, D_MODEL), row),
            pl.BlockSpec((TOK_TILE, D_MODEL), row),
            pl.BlockSpec((TOK_TILE, LANES), row),
            pl.BlockSpec((TOK_TILE, LANES), row),
        ],
        compiler_params=_params(("arbitrary",)),
        name="post",
    )(x_ctx, x_lat, mod3, o_f, o_b, og, pooled_c, pooled_l, mg, gng, wpg, psc, wbg, wbp, wout, n2g,
      wr, br)


def _route_kernel(idx_ref, rank_ref, cnt_ref, carry_ref):
    t = pl.program_id(0)

    @pl.when(t == 0)
    def _():
        carry_ref[...] = jnp.zeros((1, LANES), F32)

    idx = idx_ref[...]
    lane = lax.broadcasted_iota(I32, (ROUTE_TILE, LANES), 1)
    sel = [lane == idx[:, kk:kk + 1] for kk in range(TOP_K)]
    onehot = jnp.zeros((ROUTE_TILE, LANES), F32)
    for kk in range(TOP_K):
        onehot = onehot + jnp.where(sel[kk], 1.0, 0.0)
    row = lax.broadcasted_iota(I32, (ROUTE_TILE, ROUTE_TILE), 0)
    col = lax.broadcasted_iota(I32, (ROUTE_TILE, ROUTE_TILE), 1)
    strict = jnp.where(col < row, 1.0, 0.0).astype(BF16)
    before = _dot(strict, onehot.astype(BF16)) + carry_ref[...]
    rank = jnp.zeros((ROUTE_TILE, LANES), F32)
    for kk in range(TOP_K):
        rk = jnp.sum(jnp.where(sel[kk], before, 0.0), axis=-1, keepdims=True)
        rank = jnp.where(lane == kk, rk, rank)
    rank_ref[...] = rank.astype(I32)
    carry_ref[...] = carry_ref[...] + jnp.sum(onehot, axis=0, keepdims=True)
    cnt_ref[...] = jnp.broadcast_to(carry_ref[...], (8, LANES))


def _route_call(idx):
    return pl.pallas_call(
        _route_kernel,
        out_shape=[
            jax.ShapeDtypeStruct((N_TOK, LANES), I32),
            jax.ShapeDtypeStruct((8, LANES), F32),
        ],
        grid=(N_TOK // ROUTE_TILE,),
        in_specs=[pl.BlockSpec((ROUTE_TILE, LANES), lambda t: (t, 0))],
        out_specs=[
            pl.BlockSpec((ROUTE_TILE, LANES), lambda t: (t, 0)),
            pl.BlockSpec((8, LANES), lambda t: (0, 0)),
        ],
        scratch_shapes=[pltpu.VMEM((1, LANES), F32)],
        compiler_params=_params(("arbitrary",)),
        name="route",
    )(idx)


WAIT_UNROLL = 32
ROWS_PER_TILE = TOK_TILE * TOP_K
TOP_K_SHIFT = TOP_K.bit_length() - 1
assert TOP_K == 1 << TOP_K_SHIFT


def _token_of(i):
    if isinstance(i, int):
        return i >> TOP_K_SHIFT
    return lax.shift_right_logical(i, TOP_K_SHIFT)


def _choice_of(i):
    return i & (TOP_K - 1)


def _dispatch_kernel(pend_ref, be_ref, pos_ref, h_ref, hs_hbm, zero_ref, sem_z, sem):
    t = pl.program_id(0)

    @pl.when(t == 0)
    def _():
        zero_ref[...] = jnp.zeros((MOE_BLOCK, D_MODEL), F32)

        def has_padding(b):
            return (b + 1) * MOE_BLOCK >= pend_ref[be_ref[b]]

        def block_copy(b):
            start = pl.multiple_of(b * MOE_BLOCK, MOE_BLOCK)
            return pltpu.make_async_copy(zero_ref, hs_hbm.at[pl.ds(start, MOE_BLOCK)], sem_z)

        def fill(b, carry):
            @pl.when(has_padding(b))
            def _():
                block_copy(b).start()
            return carry

        def fill_wait(b, carry):
            @pl.when(has_padding(b))
            def _():
                block_copy(b).wait()
            return carry

        lax.fori_loop(0, N_SLOT_BLOCKS, fill, 0)
        lax.fori_loop(0, N_SLOT_BLOCKS, fill_wait, 0)

    def row_copy(j):
        return pltpu.make_async_copy(h_ref.at[pl.ds(_token_of(j), 1)],
                                     hs_hbm.at[pl.ds(pos_ref[t * ROWS_PER_TILE + j], 1)], sem)

    def issue(j, c):
        row_copy(j).start()
        return c

    def wait_one(j, c):
        row_copy(0).wait()
        return c

    lax.fori_loop(0, ROWS_PER_TILE, issue, 0, unroll=8)
    lax.fori_loop(0, ROWS_PER_TILE, wait_one, 0, unroll=WAIT_UNROLL)


def _dispatch_call(pad_end, block_e, pos_flat, h2):
    return pl.pallas_call(
        _dispatch_kernel,
        out_shape=jax.ShapeDtypeStruct((N_SLOTS, D_MODEL), F32),
        grid_spec=pltpu.PrefetchScalarGridSpec(
            num_scalar_prefetch=3,
            grid=(N_TILES,),
            in_specs=[pl.BlockSpec((TOK_TILE, D_MODEL), lambda t, pe, be, ps: (t, 0))],
            out_specs=pl.BlockSpec(memory_space=pl.ANY),
            scratch_shapes=[pltpu.VMEM((MOE_BLOCK, D_MODEL), F32), pltpu.SemaphoreType.DMA,
                            pltpu.SemaphoreType.DMA],
        ),
        compiler_params=_params(("arbitrary",)),
        name="dispatch",
    )(pad_end, block_e, pos_flat, h2)


def _moe_kernel(be_ref, nu_ref, x_ref, wg_ref, bg_ref, wu_ref, bu_ref, wd_ref, bd_ref, y_ref,
                wgb_ref, wub_ref, wdb_ref):
    b = pl.program_id(0)
    n_used = nu_ref[0]
    e = be_ref[b]
    prev = be_ref[jnp.maximum(b - 1, 0)]
    live = b < n_used

    @pl.when(live & ((b == 0) | (e != prev)))
    def _():
        wgb_ref[...] = wg_ref[0].astype(BF16)
        wub_ref[...] = wu_ref[0].astype(BF16)
        wdb_ref[...] = wd_ref[0].astype(BF16)

    @pl.when(live)
    def _():
        x = x_ref[...].astype(BF16)
        gate = jnp.minimum(_dot(x, wgb_ref[...]) + bg_ref[0], SWIGLU_LIMIT)
        up = jnp.clip(_dot(x, wub_ref[...]) + bu_ref[0], -SWIGLU_LIMIT, SWIGLU_LIMIT)
        act = (up + 1.0) * (gate * _sigmoid(SWIGLU_ALPHA * gate))
        y_ref[...] = _dot(act.astype(BF16), wdb_ref[...]) + bd_ref[0]

    @pl.when(jnp.logical_not(live))
    def _():
        y_ref[...] = jnp.zeros((MOE_BLOCK, D_MODEL), F32)


def _moe_call(block_e, n_used, hs, w_gate, b_gate, w_up, b_up, w_down, b_down):
    def blk(b, be, nu):
        return jnp.minimum(b, nu[0] - 1)

    row = lambda b, be, nu: (blk(b, be, nu), 0)
    wsel = lambda b, be, nu: (be[blk(b, be, nu)], 0, 0)
    return pl.pallas_call(
        _moe_kernel,
        out_shape=jax.ShapeDtypeStruct((N_SLOTS, D_MODEL), F32),
        grid_spec=pltpu.PrefetchScalarGridSpec(
            num_scalar_prefetch=2,
            grid=(N_SLOT_BLOCKS,),
            in_specs=[
                pl.BlockSpec((MOE_BLOCK, D_MODEL), row),
                pl.BlockSpec((1, D_MODEL, D_FF), wsel),
                pl.BlockSpec((1, 1, D_FF), wsel),
                pl.BlockSpec((1, D_MODEL, D_FF), wsel),
                pl.BlockSpec((1, 1, D_FF), wsel),
                pl.BlockSpec((1, D_FF, D_MODEL), wsel),
                pl.BlockSpec((1, 1, D_MODEL), wsel),
            ],
            out_specs=pl.BlockSpec((MOE_BLOCK, D_MODEL), lambda b, be, nu: (b, 0)),
            scratch_shapes=[
                pltpu.VMEM((D_MODEL, D_FF), BF16),
                pltpu.VMEM((D_MODEL, D_FF), BF16),
                pltpu.VMEM((D_FF, D_MODEL), BF16),
            ],
        ),
        compiler_params=_params(("arbitrary",)),
        name="moe",
    )(block_e, n_used, hs, w_gate, b_gate, w_up, b_up, w_down, b_down)


def _combine_kernel(pos_ref, x1_ref, mod_ref, tw_ref, fg_ref, y_hbm, out_ref, buf_ref, sems):
    t = pl.program_id(0)

    def row_copy(tile, j, slot):
        return pltpu.make_async_copy(
            y_hbm.at[pl.ds(pos_ref[tile * ROWS_PER_TILE + j], 1)],
            buf_ref.at[slot, _choice_of(j), pl.ds(_token_of(j), 1)], sems.at[slot])

    def issue_tile(tile, slot):
        def issue(j, c):
            row_copy(tile, j, slot).start()
            return c
        lax.fori_loop(0, ROWS_PER_TILE, issue, 0, unroll=8)

    @pl.when(t == 0)
    def _():
        issue_tile(0, 0)

    @pl.when(t + 1 < N_TILES)
    def _():
        issue_tile(t + 1, (t + 1) % 2)

    slot = t % 2

    def wait_one(j, c):
        row_copy(0, 0, slot).wait()
        return c

    lax.fori_loop(0, ROWS_PER_TILE, wait_one, 0, unroll=WAIT_UNROLL)

    tw = tw_ref[...]
    f = jnp.zeros((TOK_TILE, D_MODEL), F32)
    for kk in range(TOP_K):
        f = f + buf_ref[slot, kk] * tw[:, kk:kk + 1]
    gate2 = mod_ref[0][:, 5 * D_MODEL:6 * D_MODEL]
    x2 = x1_ref[...] + gate2 * f
    out_ref[...] = _rms(x2) * fg_ref[...]


def _combine_call(pos_flat, x1, mod3, tw, final_g, y):
    return pl.pallas_call(
        _combine_kernel,
        out_shape=jax.ShapeDtypeStruct((N_TOK, D_MODEL), F32),
        grid_spec=pltpu.PrefetchScalarGridSpec(
            num_scalar_prefetch=1,
            grid=(N_TILES,),
            in_specs=[
                pl.BlockSpec((TOK_TILE, D_MODEL), lambda t, p: (t, 0)),
                pl.BlockSpec((1, 1, N_MOD * D_MODEL), lambda t, p: (_mod_row(t), 0, 0)),
                pl.BlockSpec((TOK_TILE, LANES), lambda t, p: (t, 0)),
                pl.BlockSpec((1, D_MODEL), lambda t, p: (0, 0)),
                pl.BlockSpec(memory_space=pl.ANY),
            ],
            out_specs=pl.BlockSpec((TOK_TILE, D_MODEL), lambda t, p: (t, 0)),
            scratch_shapes=[pltpu.VMEM((2, TOP_K, TOK_TILE, D_MODEL), F32),
                            pltpu.SemaphoreType.DMA((2,))],
        ),
        compiler_params=_params(("arbitrary",)),
        name="combine",
    )(pos_flat, x1, mod3, tw, final_g, y)


def kernel(x_prompt, x_sample, state_gla_fwd, state_gla_bwd, c, c_ctx, norm1_g, w_mod, b_mod, w_in,
           w_alpha, b_alpha, gla_norm_g, w_pool_grp, pool_scale, w_branch_gla, w_branch_pool, w_out,
           norm2_g, w_router, b_router, w_gate, b_gate, w_up, b_up, w_down, b_down, final_norm_g):
    l = 0
    x_ctx = x_prompt.reshape(N_CTX, D_MODEL)
    x_lat = x_sample.reshape(N_LAT, D_MODEL)

    cvec = jnp.concatenate([c_ctx[None, :], c, jnp.zeros((8 - 1 - DEC_BATCH, D_MODEL), F32)], axis=0)
    mod = _mod_call(cvec, w_mod[l], b_mod[l][None, :])
    mod3 = mod.reshape(8, 1, N_MOD * D_MODEL)

    w_in_b = w_in[l].astype(BF16)
    w_main = w_in_b[:, :MAIN_W]
    w_alr = w_in_b[:, MAIN_W:MAIN_W + ALR_W]
    w_xp = w_in_b[:, MAIN_W + ALR_W:MAIN_W + ALR_W + POOL_W]
    w_mg = w_in_b[:, MAIN_W + ALR_W + POOL_W:]
    q, k, v, og, alr, xp, mg = _inproj_call(x_ctx, x_lat, mod3, norm1_g[l][None, :],
                                            w_main, w_alr, w_xp, w_mg)

    zpad = jnp.zeros((GLA_LOWRANK, QK_W), F32)
    wa_f = jnp.concatenate([w_alpha[l, 0], zpad], axis=0)
    wa_b = jnp.concatenate([zpad, w_alpha[l, 1]], axis=0)
    o_f, o_b, s_f, s_b = _gla_call(q, k, v, alr, wa_f, b_alpha[l, 0][None, :], wa_b,
                                   b_alpha[l, 1][None, :], state_gla_fwd[:, l], state_gla_bwd[:, l])

    pooled_c = _pool_ctx_call(xp)
    pooled_l = _pool_lat_call(xp)

    w_router_pad = jnp.pad(w_router[l], ((0, 0), (0, LANES - N_EXPERTS)))
    b_router_pad = jnp.pad(b_router[l], (0, LANES - N_EXPERTS))[None, :]
    x1, h2, top_idx, top_w = _post_call(
        x_ctx, x_lat, mod3, o_f, o_b, og, pooled_c, pooled_l, mg,
        gla_norm_g[l].reshape(1, V_W), w_pool_grp[l].astype(BF16), pool_scale[l][None, :],
        w_branch_gla[l].astype(BF16), w_branch_pool[l].astype(BF16), w_out[l].astype(BF16),
        norm2_g[l][None, :], w_router_pad, b_router_pad)

    rank, cnt = _route_call(top_idx)
    counts = cnt[0, :N_EXPERTS].astype(I32)
    padded = (counts + MOE_BLOCK - 1) // MOE_BLOCK * MOE_BLOCK
    pad_end = jnp.cumsum(padded).astype(I32)
    pad_start = pad_end - padded
    block_first = jnp.arange(N_SLOT_BLOCKS, dtype=I32) * MOE_BLOCK
    block_e = jnp.minimum(jnp.sum((pad_end[None, :] <= block_first[:, None]).astype(I32), axis=1),
                          N_EXPERTS - 1).astype(I32)
    n_used = (pad_end[-1:] // MOE_BLOCK).astype(I32)
    experts = jnp.arange(N_EXPERTS, dtype=I32)
    tk = top_idx[:, :TOP_K]
    pos = jnp.sum(jnp.where(tk[:, :, None] == experts, pad_start, 0), axis=-1) + rank[:, :TOP_K]
    pos_flat = pos.reshape(N_TOK * TOP_K).astype(I32)

    hs = _dispatch_call(pad_end, block_e, pos_flat, h2)
    y = _moe_call(block_e, n_used, hs,
                  w_gate[l], b_gate[l][:, None, :], w_up[l], b_up[l][:, None, :],
                  w_down[l], b_down[l][:, None, :])
    out = _combine_call(pos_flat, x1, mod3, top_w, final_norm_g[None, :], y)

    y_prompt = out[:N_CTX].reshape(BATCH, SEQ, D_MODEL)
    y_sample = out[N_CTX:].reshape(DEC_BATCH, DEC_SEQ, D_MODEL)
    return (y_prompt, y_sample, s_f[:, None], s_b[:, None])
```

```python
import functools

import jax
import jax.numpy as jnp
from jax import lax
from jax.experimental import pallas as pl
from jax.experimental.pallas import tpu as pltpu
from jax.experimental.pallas import tpu_sc as plsc

F32 = jnp.float32
BF16 = jnp.bfloat16
I32 = jnp.int32

D_MODEL = 1024
BATCH = 32
SEQ = 256
DEC_BATCH = 4
DEC_SEQ = 2048
GRID_W = 64
GLA_HEADS = 4
GLA_DK = 128
GLA_DV = 256
GLA_LOWRANK = 16
GLA_TAU = 16.0
GLA_CHUNK = 64
POOL_GROUPS = 4
POOL_GROUP_DIM = 128
POOL_WINDOWS = (2, 4, 8, 16)
N_EXPERTS = 32
TOP_K = 4
D_FF = 1024
SWIGLU_LIMIT = 7.0
SWIGLU_ALPHA = 1.702
MOE_BLOCK = 256
NORM_EPS = 1e-6
N_MOD = 6

QK_W = GLA_HEADS * GLA_DK
V_W = GLA_HEADS * GLA_DV
POOL_W = POOL_GROUPS * POOL_GROUP_DIM
MAIN_W = 2 * QK_W + 2 * V_W
ALR_W = 2 * GLA_LOWRANK
MG_W = 2 * D_MODEL

N_CTX = BATCH * SEQ
N_LAT = DEC_BATCH * DEC_SEQ
N_TOK = N_CTX + N_LAT
N_SLOT_BLOCKS = -(-(N_TOK * TOP_K + N_EXPERTS * (MOE_BLOCK - 1)) // MOE_BLOCK)
N_SLOTS = N_SLOT_BLOCKS * MOE_BLOCK

LANES = 128
TOK_TILE = 256
N_TILES = N_TOK // TOK_TILE
CTX_TILES = N_CTX // TOK_TILE
LAT_TILES_PER_SEQ = DEC_SEQ // TOK_TILE
ROUTE_TILE = 512
VMEM_LIMIT = 56 * 1024 * 1024

GLA_SEQS = 4
CTX_CHUNKS = SEQ // GLA_CHUNK
LAT_CHUNKS = DEC_SEQ // GLA_CHUNK
CHUNKS_PER_TILE = TOK_TILE // GLA_CHUNK
GLA_CTX_STEPS = (BATCH // GLA_SEQS) * CTX_CHUNKS
TILE_GRID = 8

NT_DIMS = (((1,), (1,)), ((), ()))
TN_DIMS = (((0,), (0,)), ((), ()))

assert DEC_BATCH == GLA_SEQS and SEQ == TOK_TILE and N_TILES == TILE_GRID * TILE_GRID


def _params(semantics, vmem=VMEM_LIMIT):
    return pltpu.CompilerParams(dimension_semantics=semantics, vmem_limit_bytes=vmem)


def _split_bf16(a):
    hi = a.astype(BF16)
    lo = (a - hi.astype(F32)).astype(BF16)
    return hi, lo


def _dot(a, b):
    return jnp.dot(a, b, preferred_element_type=F32)


def _dot3(a, b):
    a_hi, a_lo = _split_bf16(a)
    b_hi, b_lo = _split_bf16(b)
    return _dot(a_hi, b_hi) + _dot(a_lo, b_hi) + _dot(a_hi, b_lo)


def _sigmoid(x):
    return 1.0 / (1.0 + jnp.exp(-x))


def _rms(x):
    return x * lax.rsqrt(jnp.mean(x * x, axis=-1, keepdims=True) + NORM_EPS)


def _mod_row(t):
    return jnp.where(t < CTX_TILES, 0, 1 + (t - CTX_TILES) // LAT_TILES_PER_SEQ)


def _store_tile(t):
    u = t - CTX_TILES
    return jnp.where(t < CTX_TILES, t,
                     CTX_TILES + DEC_BATCH * (u % LAT_TILES_PER_SEQ) + u // LAT_TILES_PER_SEQ)


def _ctx_tile(t):
    return jnp.minimum(t, CTX_TILES - 1)


def _lat_tile(t):
    return jnp.maximum(t - CTX_TILES, 0)


def _mod_kernel(c_ref, w_ref, b_ref, o_ref):
    c = c_ref[...]
    o_ref[...] = _dot3(c * _sigmoid(c), w_ref[...]) + b_ref[...]


def _mod_call(cvec, w_mod, b_mod):
    rows = cvec.shape[0]
    return pl.pallas_call(
        _mod_kernel,
        out_shape=jax.ShapeDtypeStruct((rows, N_MOD * D_MODEL), F32),
        grid=(N_MOD,),
        in_specs=[
            pl.BlockSpec((rows, D_MODEL), lambda j: (0, 0)),
            pl.BlockSpec((D_MODEL, D_MODEL), lambda j: (0, j)),
            pl.BlockSpec((1, D_MODEL), lambda j: (0, j)),
        ],
        out_specs=pl.BlockSpec((rows, D_MODEL), lambda j: (0, j)),
        compiler_params=_params(("arbitrary",)),
        name="mod",
    )(cvec, w_mod, b_mod)


def _inproj_kernel(xc_ref, xl_ref, mod_ref, g_ref, wmain_ref, walr_ref, wxp_ref, wmg_ref,
                   q_ref, k_ref, v_ref, og_ref, alr_ref, xp_ref, mg_ref):
    t = pl.program_id(0)
    x = jnp.where(t < CTX_TILES, xc_ref[...], xl_ref[...])
    mod = mod_ref[0]
    shift1 = mod[:, 0:D_MODEL]
    scale1 = mod[:, D_MODEL:2 * D_MODEL]
    h = _rms(x) * g_ref[...]
    h = (h * (1.0 + scale1) + shift1).astype(BF16)
    z = _dot(h, wmain_ref[...])
    q_ref[...] = (z[:, 0:QK_W] * (GLA_DK ** -0.5)).astype(BF16)
    k_ref[...] = z[:, QK_W:2 * QK_W].astype(BF16)
    v_ref[...] = z[:, 2 * QK_W:2 * QK_W + V_W].astype(BF16)
    og_ref[...] = z[:, 2 * QK_W + V_W:MAIN_W].astype(BF16)
    alr_ref[...] = _dot(h, walr_ref[...])
    xp_ref[...] = _dot(h, wxp_ref[...])
    mg_ref[...] = _dot(h, wmg_ref[...]).astype(BF16)


def _inproj_call(x_ctx, x_lat, mod3, norm1_g, w_main, w_alr, w_xp, w_mg):
    const = lambda t: (0, 0)
    stored = lambda t: (_store_tile(t), 0)
    widths = (QK_W, QK_W, V_W, V_W, ALR_W, POOL_W, MG_W)
    dtypes = (BF16, BF16, BF16, BF16, F32, F32, BF16)
    return pl.pallas_call(
        _inproj_kernel,
        out_shape=[jax.ShapeDtypeStruct((N_TOK, w), dt) for w, dt in zip(widths, dtypes)],
        grid=(N_TILES,),
        in_specs=[
            pl.BlockSpec((TOK_TILE, D_MODEL), lambda t: (_ctx_tile(t), 0)),
            pl.BlockSpec((TOK_TILE, D_MODEL), lambda t: (_lat_tile(t), 0)),
            pl.BlockSpec((1, 1, N_MOD * D_MODEL), lambda t: (_mod_row(t), 0, 0)),
            pl.BlockSpec((1, D_MODEL), const),
            pl.BlockSpec((D_MODEL, MAIN_W), const),
            pl.BlockSpec((D_MODEL, ALR_W), const),
            pl.BlockSpec((D_MODEL, POOL_W), const),
            pl.BlockSpec((D_MODEL, MG_W), const),
        ],
        out_specs=[pl.BlockSpec((TOK_TILE, w), stored) for w in widths],
        compiler_params=_params(("arbitrary",)),
        name="inproj",
    )(x_ctx, x_lat, mod3, norm1_g, w_main, w_alr, w_xp, w_mg)


def _gla_direction(q_ref, k_ref, v_ref, alr_ref, wa_ref, ba_ref, o_ref, st_ref, slot0, rev):
    rows = GLA_SEQS * GLA_CHUNK
    alr = jnp.concatenate([alr_ref[0, s] for s in range(GLA_SEQS)], axis=0)
    a = _dot3(alr, wa_ref[...]) + ba_ref[...]
    g = (jnp.minimum(a, 0.0) - jnp.log(1.0 + jnp.exp(-jnp.abs(a)))) * (1.0 / GLA_TAU)

    row = lax.broadcasted_iota(I32, (rows, rows), 0)
    col = lax.broadcasted_iota(I32, (rows, rows), 1)
    same = (row // GLA_CHUNK) == (col // GLA_CHUNK)
    tri_all = same & ((col >= row) if rev else (col <= row))
    tri_b = jnp.where(tri_all, 1.0, 0.0).astype(BF16)
    g_hi, g_lo = _split_bf16(g)
    bcum_all = _dot(tri_b, g_hi) + _dot(tri_b, g_lo)

    r64 = lax.broadcasted_iota(I32, (GLA_CHUNK, GLA_CHUNK), 0)
    c64 = lax.broadcasted_iota(I32, (GLA_CHUNK, GLA_CHUNK), 1)
    tri = (c64 >= r64) if rev else (c64 <= r64)

    for s in range(GLA_SEQS):
        bcum = bcum_all[s * GLA_CHUNK:(s + 1) * GLA_CHUNK]
        blast = bcum[0:1] if rev else bcum[GLA_CHUNK - 1:GLA_CHUNK]
        bmid = bcum[GLA_CHUNK // 2:GLA_CHUNK // 2 + 1]
        e_q = jnp.exp(bcum - bmid)
        e_k = jnp.exp(bmid - bcum)
        e_in = jnp.exp(bcum)
        e_out = jnp.exp(blast - bcum)
        e_last = jnp.exp(blast)
        q = q_ref[0, s].astype(F32)
        k = k_ref[0, s].astype(F32)
        for h in range(GLA_HEADS):
            ks = slice(h * GLA_DK, (h + 1) * GLA_DK)
            vs = slice(h * GLA_DV, (h + 1) * GLA_DV)
            qh = q[:, ks]
            kh = k[:, ks]
            vh = v_ref[0, s, :, vs]
            att = lax.dot_general((qh * e_q[:, ks]).astype(BF16), (kh * e_k[:, ks]).astype(BF16),
                                  NT_DIMS, preferred_element_type=F32)
            att = jnp.where(tri, att, 0.0).astype(BF16)
            st = st_ref[slot0 + s, h]
            o_inter = lax.dot_general((qh * e_in[:, ks]).astype(BF16), st.astype(BF16),
                                      NT_DIMS, preferred_element_type=F32)
            o_ref[0, s, :, vs] = o_inter + _dot(att, vh)
            upd = lax.dot_general(vh, (kh * e_out[:, ks]).astype(BF16), TN_DIMS,
                                  preferred_element_type=F32)
            st_ref[slot0 + s, h] = st * e_last[:, ks] + upd


def _gla_kernel(qf_ref, kf_ref, vf_ref, af_ref, qb_ref, kb_ref, vb_ref, ab_ref,
                waf_ref, baf_ref, wab_ref, bab_ref, s0f_ref, s0b_ref,
                of_ref, ob_ref, sf_ref, sb_ref, st_ref):
    i = pl.program_id(0)
    is_ctx = i < GLA_CTX_STEPS
    chunk = jnp.where(is_ctx, i % CTX_CHUNKS, i - GLA_CTX_STEPS)

    @pl.when(is_ctx & (chunk == 0))
    def _():
        st_ref[...] = jnp.zeros(st_ref.shape, F32)

    @pl.when(i == GLA_CTX_STEPS)
    def _():
        for s in range(GLA_SEQS):
            for h in range(GLA_HEADS):
                st_ref[s, h] = s0f_ref[s, h].T
                st_ref[GLA_SEQS + s, h] = s0b_ref[s, h].T

    _gla_direction(qf_ref, kf_ref, vf_ref, af_ref, waf_ref, baf_ref, of_ref, st_ref, 0, False)
    _gla_direction(qb_ref, kb_ref, vb_ref, ab_ref, wab_ref, bab_ref, ob_ref, st_ref, GLA_SEQS, True)

    @pl.when(is_ctx & (chunk == CTX_CHUNKS - 1))
    def _():
        for s in range(GLA_SEQS):
            for h in range(GLA_HEADS):
                sf_ref[s, h] = st_ref[s, h].T
                sb_ref[s, h] = st_ref[GLA_SEQS + s, h].T


def _gla_block(i, rev):
    is_ctx = i < GLA_CTX_STEPS
    group = i // CTX_CHUNKS
    c_ctx = i % CTX_CHUNKS
    c_lat = i - GLA_CTX_STEPS
    if rev:
        c_ctx = CTX_CHUNKS - 1 - c_ctx
        c_lat = LAT_CHUNKS - 1 - c_lat
    j = c_lat // CHUNKS_PER_TILE
    per_row = TILE_GRID // GLA_SEQS
    a = jnp.where(is_ctx, group // per_row, CTX_TILES // TILE_GRID + j // per_row)
    b = jnp.where(is_ctx, group % per_row, j % per_row)
    c = jnp.where(is_ctx, c_ctx, c_lat % CHUNKS_PER_TILE)
    return (a, b, c, 0)


def _gla_call(q, k, v, alr, wa_f, ba_f, wa_b, ba_b, s0_f, s0_b):
    def view(arr):
        return arr.reshape(TILE_GRID, TILE_GRID, TOK_TILE, arr.shape[-1])

    def spec(width, rev):
        return pl.BlockSpec((1, GLA_SEQS, GLA_CHUNK, width), lambda i: _gla_block(i, rev))

    const = lambda i: (0, 0)
    st_block = (GLA_SEQS, GLA_HEADS, GLA_DK, GLA_DV)
    whole_state = pl.BlockSpec(st_block, lambda i: (0, 0, 0, 0))
    ctx_state = pl.BlockSpec(
        st_block, lambda i: (jnp.minimum(i // CTX_CHUNKS, BATCH // GLA_SEQS - 1), 0, 0, 0))
    in_specs = []
    for rev in (False, True):
        in_specs += [spec(QK_W, rev), spec(QK_W, rev), spec(V_W, rev), spec(ALR_W, rev)]
    in_specs += [pl.BlockSpec((ALR_W, QK_W), const), pl.BlockSpec((1, QK_W), const)] * 2
    in_specs += [whole_state, whole_state]
    o_shape = jax.ShapeDtypeStruct((TILE_GRID, TILE_GRID, TOK_TILE, V_W), F32)
    s_shape = jax.ShapeDtypeStruct((BATCH, GLA_HEADS, GLA_DK, GLA_DV), F32)
    qv, kv, vv, av = view(q), view(k), view(v), view(alr)
    o_f, o_b, s_f, s_b = pl.pallas_call(
        _gla_kernel,
        out_shape=[o_shape, o_shape, s_shape, s_shape],
        grid=(GLA_CTX_STEPS + LAT_CHUNKS,),
        in_specs=in_specs,
        out_specs=[spec(V_W, False), spec(V_W, True), ctx_state, ctx_state],
        scratch_shapes=[pltpu.VMEM((2 * GLA_SEQS, GLA_HEADS, GLA_DV, GLA_DK), F32)],
        compiler_params=_params(("arbitrary",)),
        name="gla",
    )(qv, kv, vv, av, qv, kv, vv, av, wa_f, ba_f, wa_b, ba_b, s0_f, s0_b)
    return o_f.reshape(N_TOK, V_W), o_b.reshape(N_TOK, V_W), s_f, s_b


def _band(n, w, block):
    row = lax.broadcasted_iota(I32, (n, n), 0)
    col = lax.broadcasted_iota(I32, (n, n), 1)
    inside = (col >= row - w // 2) & (col <= row + w // 2 - 1)
    if block < n:
        inside = inside & ((row // block) == (col // block))
    return jnp.where(inside, 1.0, 0.0).astype(BF16)


def _win_count(p, n, w):
    return jnp.minimum(p + w // 2 - 1, n - 1) - jnp.maximum(p - w // 2, 0) + 1


def _pool_ctx_kernel(x_ref, o_ref):
    p = lax.broadcasted_iota(I32, (SEQ, POOL_GROUP_DIM), 0)
    for gi, w in enumerate(POOL_WINDOWS):
        cs = slice(gi * POOL_GROUP_DIM, (gi + 1) * POOL_GROUP_DIM)
        x = x_ref[:, cs]
        hi, lo = _split_bf16(x)
        band = _band(SEQ, w, SEQ)
        s = _dot(band, hi) + _dot(band, lo)
        cnt = _win_count(p, SEQ, w).astype(F32)
        o_ref[:, cs] = s / cnt - x


def _pool_ctx_call(xp):
    spec = pl.BlockSpec((SEQ, POOL_W), lambda b: (b, 0))
    return pl.pallas_call(
        _pool_ctx_kernel,
        out_shape=jax.ShapeDtypeStruct((N_CTX, POOL_W), F32),
        grid=(BATCH,),
        in_specs=[spec],
        out_specs=spec,
        compiler_params=_params(("arbitrary",)),
        name="pool_ctx",
    )(xp)


POOL_HALO = (max(POOL_WINDOWS) // 2) * GRID_W


def _pool_lat_kernel(x_ref, o_ref, pad_ref):
    rows = DEC_SEQ // GRID_W
    p = lax.broadcasted_iota(I32, (DEC_SEQ, POOL_GROUP_DIM), 0)
    r = p // GRID_W
    cidx = p % GRID_W
    zeros = jnp.zeros((POOL_HALO, POOL_GROUP_DIM), F32)
    pad_ref[0:POOL_HALO, :] = zeros
    pad_ref[POOL_HALO + DEC_SEQ:2 * POOL_HALO + DEC_SEQ, :] = zeros
    for gi, w in enumerate(POOL_WINDOWS):
        cs = slice(gi * POOL_GROUP_DIM, (gi + 1) * POOL_GROUP_DIM)
        band = _band(TOK_TILE, w, GRID_W)
        for t in range(LAT_TILES_PER_SEQ):
            hi, lo = _split_bf16(x_ref[t, 0, :, cs])
            pad_ref[POOL_HALO + t * TOK_TILE:POOL_HALO + (t + 1) * TOK_TILE, :] = (
                _dot(band, hi) + _dot(band, lo))
        acc = jnp.zeros((DEC_SEQ, POOL_GROUP_DIM), F32)
        for dr in range(-(w // 2), w // 2):
            start = POOL_HALO + dr * GRID_W
            acc = acc + pad_ref[start:start + DEC_SEQ, :]
        cnt = (_win_count(r, rows, w) * _win_count(cidx, GRID_W, w)).astype(F32)
        pooled = acc / cnt
        for t in range(LAT_TILES_PER_SEQ):
            rs = slice(t * TOK_TILE, (t + 1) * TOK_TILE)
            o_ref[t, 0, :, cs] = pooled[rs] - x_ref[t, 0, :, cs]


def _pool_lat_call(xp):
    view = xp.reshape(N_TILES // DEC_BATCH, DEC_BATCH, TOK_TILE, POOL_W)
    blk = (LAT_TILES_PER_SEQ, 1, TOK_TILE, POOL_W)
    out = pl.pallas_call(
        _pool_lat_kernel,
        out_shape=jax.ShapeDtypeStruct((LAT_TILES_PER_SEQ, DEC_BATCH, TOK_TILE, POOL_W), F32),
        grid=(DEC_BATCH,),
        in_specs=[pl.BlockSpec(blk, lambda s: (CTX_TILES // DEC_BATCH // LAT_TILES_PER_SEQ, s, 0, 0))],
        out_specs=pl.BlockSpec(blk, lambda s: (0, s, 0, 0)),
        scratch_shapes=[pltpu.VMEM((DEC_SEQ + 2 * POOL_HALO, POOL_GROUP_DIM), F32)],
        compiler_params=_params(("arbitrary",)),
        name="pool_lat",
    )(view)
    return out.reshape(N_LAT, POOL_W)


def _post_kernel(xc_ref, xl_ref, mod_ref, of_ref, ob_ref, og_ref, pc_ref, pl_ref, mg_ref, gng_ref,
                 wpg_ref, psc_ref, wbg_ref, wbp_ref, wout_ref, n2g_ref, wr_ref, br_ref,
                 x1_ref, h2_ref, idx_ref, tw_ref):
    t = pl.program_id(0)
    is_ctx = t < CTX_TILES
    x = jnp.where(is_ctx, xc_ref[...], xl_ref[...])
    pooled = jnp.where(is_ctx, pc_ref[...], pl_ref[...])
    mod = mod_ref[0]
    gate1 = mod[:, 2 * D_MODEL:3 * D_MODEL]
    shift2 = mod[:, 3 * D_MODEL:4 * D_MODEL]
    scale2 = mod[:, 4 * D_MODEL:5 * D_MODEL]

    o = of_ref[...] + ob_ref[...]
    og = og_ref[...].astype(F32)
    gated = []
    for h in range(GLA_HEADS):
        vs = slice(h * GLA_DV, (h + 1) * GLA_DV)
        oh = _rms(o[:, vs]) * gng_ref[:, vs]
        ogh = og[:, vs]
        gated.append((oh * (ogh * _sigmoid(ogh))).astype(BF16))
    br_gla = _dot(jnp.concatenate(gated, axis=-1), wbg_ref[...])

    pm = []
    for gi in range(POOL_GROUPS):
        cs = slice(gi * POOL_GROUP_DIM, (gi + 1) * POOL_GROUP_DIM)
        pmg = _dot(pooled[:, cs].astype(BF16), wpg_ref[gi]) * psc_ref[:, cs]
        pm.append(pmg.astype(BF16))
    br_pool = _dot(jnp.concatenate(pm, axis=-1), wbp_ref[...])

    mg = mg_ref[...].astype(F32)
    merged = _sigmoid(mg[:, 0:D_MODEL]) * br_gla + _sigmoid(mg[:, D_MODEL:MG_W]) * br_pool
    m = _dot(merged.astype(BF16), wout_ref[...])
    x1 = x + gate1 * m
    x1_ref[...] = x1
    h2 = _rms(x1) * n2g_ref[...]
    h2 = h2 * (1.0 + scale2) + shift2
    h2_ref[...] = h2

    logits = _dot3(h2, wr_ref[...]) + br_ref[...]
    lane = lax.broadcasted_iota(I32, (TOK_TILE, LANES), 1)
    lane_f = lane.astype(F32)
    neg = jnp.float32(-jnp.inf)
    cur = jnp.where(lane < N_EXPERTS, logits, neg)
    vals, idxs = [], []
    for _ in range(TOP_K):
        mx = jnp.max(cur, axis=-1, keepdims=True)
        ix = jnp.min(jnp.where(cur == mx, lane_f, float(LANES)), axis=-1, keepdims=True)
        vals.append(mx)
        idxs.append(ix)
        cur = jnp.where(lane_f == ix, neg, cur)
    ex = [jnp.exp(vv - vals[0]) for vv in vals]
    tot = ex[0] + ex[1] + ex[2] + ex[3]
    idx_out = jnp.zeros((TOK_TILE, LANES), F32)
    w_out = jnp.zeros((TOK_TILE, LANES), F32)
    for kk in range(TOP_K):
        idx_out = jnp.where(lane == kk, idxs[kk], idx_out)
        w_out = jnp.where(lane == kk, ex[kk] / tot, w_out)
    idx_ref[...] = idx_out.astype(I32)
    tw_ref[...] = w_out


def _post_call(x_ctx, x_lat, mod3, o_f, o_b, og, pooled_c, pooled_l, mg, gng, wpg, psc, wbg, wbp,
               wout, n2g, wr, br):
    row = lambda t: (t, 0)
    const = lambda t: (0, 0)
    stored = lambda t: (_store_tile(t), 0)
    return pl.pallas_call(
        _post_kernel,
        out_shape=[
            jax.ShapeDtypeStruct((N_TOK, D_MODEL), F32),
            jax.ShapeDtypeStruct((N_TOK, D_MODEL), F32),
            jax.ShapeDtypeStruct((N_TOK, LANES), I32),
            jax.ShapeDtypeStruct((N_TOK, LANES), F32),
        ],
        grid=(N_TILES,),
        in_specs=[
            pl.BlockSpec((TOK_TILE, D_MODEL), lambda t: (_ctx_tile(t), 0)),
            pl.BlockSpec((TOK_TILE, D_MODEL), lambda t: (_lat_tile(t), 0)),
            pl.BlockSpec((1, 1, N_MOD * D_MODEL), lambda t: (_mod_row(t), 0, 0)),
            pl.BlockSpec((TOK_TILE, V_W), stored),
            pl.BlockSpec((TOK_TILE, V_W), stored),
            pl.BlockSpec((TOK_TILE, V_W), stored),
            pl.BlockSpec((TOK_TILE, POOL_W), lambda t: (_ctx_tile(t), 0)),
            pl.BlockSpec((TOK_TILE, POOL_W),
                         lambda t: (jnp.maximum(_store_tile(t) - CTX_TILES, 0), 0)),
            pl.BlockSpec((TOK_TILE, MG_W), stored),
            pl.BlockSpec((1, V_W), const),
            pl.BlockSpec((POOL_GROUPS, POOL_GROUP_DIM, POOL_GROUP_DIM), lambda t: (0, 0, 0)),
            pl.BlockSpec((1, POOL_W), const),
            pl.BlockSpec((V_W, D_MODEL), const),
            pl.BlockSpec((POOL_W, D_MODEL), const),
            pl.BlockSpec((D_MODEL, D_MODEL), const),
            pl.BlockSpec((1, D_MODEL), const),
            pl.BlockSpec((D_MODEL, LANES), const),
            pl.BlockSpec((1, LANES), const),
        ],
        out_specs=[
            pl.BlockSpec((TOK_TILE, D_MODEL), row),
            pl.BlockSpec((TOK_TILE, D_MODEL), row),
            pl.BlockSpec((TOK_TILE, LANES), row),
            pl.BlockSpec((TOK_TILE, LANES), row),
        ],
        compiler_params=_params(("arbitrary",)),
        name="post",
    )(x_ctx, x_lat, mod3, o_f, o_b, og, pooled_c, pooled_l, mg, gng, wpg, psc, wbg, wbp, wout, n2g,
      wr, br)


def _route_kernel(idx_ref, rank_ref, cnt_ref, carry_ref):
    t = pl.program_id(0)

    @pl.when(t == 0)
    def _():
        carry_ref[...] = jnp.zeros((1, LANES), F32)

    idx = idx_ref[...]
    lane = lax.broadcasted_iota(I32, (ROUTE_TILE, LANES), 1)
    sel = [lane == idx[:, kk:kk + 1] for kk in range(TOP_K)]
    onehot = jnp.zeros((ROUTE_TILE, LANES), F32)
    for kk in range(TOP_K):
        onehot = onehot + jnp.where(sel[kk], 1.0, 0.0)
    row = lax.broadcasted_iota(I32, (ROUTE_TILE, ROUTE_TILE), 0)
    col = lax.broadcasted_iota(I32, (ROUTE_TILE, ROUTE_TILE), 1)
    strict = jnp.where(col < row, 1.0, 0.0).astype(BF16)
    before = _dot(strict, onehot.astype(BF16)) + carry_ref[...]
    rank = jnp.zeros((ROUTE_TILE, LANES), F32)
    for kk in range(TOP_K):
        rk = jnp.sum(jnp.where(sel[kk], before, 0.0), axis=-1, keepdims=True)
        rank = jnp.where(lane == kk, rk, rank)
    rank_ref[...] = rank.astype(I32)
    carry_ref[...] = carry_ref[...] + jnp.sum(onehot, axis=0, keepdims=True)
    cnt_ref[...] = jnp.broadcast_to(carry_ref[...], (8, LANES))


def _route_call(idx):
    return pl.pallas_call(
        _route_kernel,
        out_shape=[
            jax.ShapeDtypeStruct((N_TOK, LANES), I32),
            jax.ShapeDtypeStruct((8, LANES), F32),
        ],
        grid=(N_TOK // ROUTE_TILE,),
        in_specs=[pl.BlockSpec((ROUTE_TILE, LANES), lambda t: (t, 0))],
        out_specs=[
            pl.BlockSpec((ROUTE_TILE, LANES), lambda t: (t, 0)),
            pl.BlockSpec((8, LANES), lambda t: (0, 0)),
        ],
        scratch_shapes=[pltpu.VMEM((1, LANES), F32)],
        compiler_params=_params(("arbitrary",)),
        name="route",
    )(idx)


WAIT_UNROLL = 32
ROWS_PER_TILE = TOK_TILE * TOP_K
TOP_K_SHIFT = TOP_K.bit_length() - 1
assert TOP_K == 1 << TOP_K_SHIFT


def _token_of(i):
    if isinstance(i, int):
        return i >> TOP_K_SHIFT
    return lax.shift_right_logical(i, TOP_K_SHIFT)


def _choice_of(i):
    return i & (TOP_K - 1)


def _dispatch_kernel(pend_ref, be_ref, pos_ref, h_ref, hs_hbm, zero_ref, sem_z, sem):
    t = pl.program_id(0)

    @pl.when(t == 0)
    def _():
        zero_ref[...] = jnp.zeros((MOE_BLOCK, D_MODEL), F32)

        def has_padding(b):
            return (b + 1) * MOE_BLOCK >= pend_ref[be_ref[b]]

        def block_copy(b):
            start = pl.multiple_of(b * MOE_BLOCK, MOE_BLOCK)
            return pltpu.make_async_copy(zero_ref, hs_hbm.at[pl.ds(start, MOE_BLOCK)], sem_z)

        def fill(b, carry):
            @pl.when(has_padding(b))
            def _():
                block_copy(b).start()
            return carry

        def fill_wait(b, carry):
            @pl.when(has_padding(b))
            def _():
                block_copy(b).wait()
            return carry

        lax.fori_loop(0, N_SLOT_BLOCKS, fill, 0)
        lax.fori_loop(0, N_SLOT_BLOCKS, fill_wait, 0)

    def row_copy(j):
        return pltpu.make_async_copy(h_ref.at[pl.ds(_token_of(j), 1)],
                                     hs_hbm.at[pl.ds(pos_ref[t * ROWS_PER_TILE + j], 1)], sem)

    def issue(j, c):
        row_copy(j).start()
        return c

    def wait_one(j, c):
        row_copy(0).wait()
        return c

    lax.fori_loop(0, ROWS_PER_TILE, issue, 0, unroll=8)
    lax.fori_loop(0, ROWS_PER_TILE, wait_one, 0, unroll=WAIT_UNROLL)


def _dispatch_call(pad_end, block_e, pos_flat, h2):
    return pl.pallas_call(
        _dispatch_kernel,
        out_shape=jax.ShapeDtypeStruct((N_SLOTS, D_MODEL), F32),
        grid_spec=pltpu.PrefetchScalarGridSpec(
            num_scalar_prefetch=3,
            grid=(N_TILES,),
            in_specs=[pl.BlockSpec((TOK_TILE, D_MODEL), lambda t, pe, be, ps: (t, 0))],
            out_specs=pl.BlockSpec(memory_space=pl.ANY),
            scratch_shapes=[pltpu.VMEM((MOE_BLOCK, D_MODEL), F32), pltpu.SemaphoreType.DMA,
                            pltpu.SemaphoreType.DMA],
        ),
        compiler_params=_params(("arbitrary",)),
        name="dispatch",
    )(pad_end, block_e, pos_flat, h2)


def _moe_kernel(be_ref, nu_ref, x_ref, wg_ref, bg_ref, wu_ref, bu_ref, wd_ref, bd_ref, y_ref,
                wgb_ref, wub_ref, wdb_ref):
    b = pl.program_id(0)
    n_used = nu_ref[0]
    e = be_ref[b]
    prev = be_ref[jnp.maximum(b - 1, 0)]
    live = b < n_used

    @pl.when(live & ((b == 0) | (e != prev)))
    def _():
        wgb_ref[...] = wg_ref[0].astype(BF16)
        wub_ref[...] = wu_ref[0].astype(BF16)
        wdb_ref[...] = wd_ref[0].astype(BF16)

    @pl.when(live)
    def _():
        x = x_ref[...].astype(BF16)
        gate = jnp.minimum(_dot(x, wgb_ref[...]) + bg_ref[0], SWIGLU_LIMIT)
        up = jnp.clip(_dot(x, wub_ref[...]) + bu_ref[0], -SWIGLU_LIMIT, SWIGLU_LIMIT)
        act = (up + 1.0) * (gate * _sigmoid(SWIGLU_ALPHA * gate))
        y_ref[...] = _dot(act.astype(BF16), wdb_ref[...]) + bd_ref[0]

    @pl.when(jnp.logical_not(live))
    def _():
        y_ref[...] = jnp.zeros((MOE_BLOCK, D_MODEL), F32)


def _moe_call(block_e, n_used, hs, w_gate, b_gate, w_up, b_up, w_down, b_down):
    def blk(b, be, nu):
        return jnp.minimum(b, nu[0] - 1)

    row = lambda b, be, nu: (blk(b, be, nu), 0)
    wsel = lambda b, be, nu: (be[blk(b, be, nu)], 0, 0)
    return pl.pallas_call(
        _moe_kernel,
        out_shape=jax.ShapeDtypeStruct((N_SLOTS, D_MODEL), F32),
        grid_spec=pltpu.PrefetchScalarGridSpec(
            num_scalar_prefetch=2,
            grid=(N_SLOT_BLOCKS,),
            in_specs=[
                pl.BlockSpec((MOE_BLOCK, D_MODEL), row),
                pl.BlockSpec((1, D_MODEL, D_FF), wsel),
                pl.BlockSpec((1, 1, D_FF), wsel),
                pl.BlockSpec((1, D_MODEL, D_FF), wsel),
                pl.BlockSpec((1, 1, D_FF), wsel),
                pl.BlockSpec((1, D_FF, D_MODEL), wsel),
                pl.BlockSpec((1, 1, D_MODEL), wsel),
            ],
            out_specs=pl.BlockSpec((MOE_BLOCK, D_MODEL), lambda b, be, nu: (b, 0)),
            scratch_shapes=[
                pltpu.VMEM((D_MODEL, D_FF), BF16),
                pltpu.VMEM((D_MODEL, D_FF), BF16),
                pltpu.VMEM((D_FF, D_MODEL), BF16),
            ],
        ),
        compiler_params=_params(("arbitrary",)),
        name="moe",
    )(block_e, n_used, hs, w_gate, b_gate, w_up, b_up, w_down, b_down)


SC_CORES = 2
SC_SUBCORES = 16
SC_WORKERS = SC_CORES * SC_SUBCORES
SC_ROWS = 64


def _sc_gather_rows(table, idx):
    n_idx = idx.shape[0]
    width = table.shape[1]
    per_worker = n_idx // SC_WORKERS
    n_chunks = per_worker // SC_ROWS
    assert n_chunks * SC_ROWS * SC_WORKERS == n_idx
    mesh = plsc.VectorSubcoreMesh(core_axis_name="c", subcore_axis_name="s")

    @functools.partial(
        pl.kernel, mesh=mesh,
        out_type=jax.ShapeDtypeStruct((n_idx, width), table.dtype),
        scratch_types=[pltpu.VMEM((SC_ROWS,), I32), pltpu.VMEM((SC_ROWS, width), table.dtype),
                       pltpu.SemaphoreType.DMA],
        name="sc_gather",
    )
    def gather(table_hbm, idx_hbm, out_hbm, idx_v, rows_v, sem):
        worker = lax.axis_index("s") * SC_CORES + lax.axis_index("c")
        base = worker * per_worker

        @pl.loop(0, n_chunks)
        def _(ch):
            off = pl.multiple_of(base + ch * SC_ROWS, SC_ROWS)
            pltpu.sync_copy(idx_hbm.at[pl.ds(off, SC_ROWS)], idx_v)
            pltpu.async_copy(table_hbm.at[idx_v], rows_v, sem).wait()
            pltpu.sync_copy(rows_v, out_hbm.at[pl.ds(off, SC_ROWS)])

    return gather(table, idx)


def _combine_kernel(x1_ref, mod_ref, tw_ref, fg_ref, g_ref, out_ref):
    tw = tw_ref[...]
    f = jnp.zeros((TOK_TILE, D_MODEL), F32)
    for kk in range(TOP_K):
        f = f + g_ref[:, kk * D_MODEL:(kk + 1) * D_MODEL] * tw[:, kk:kk + 1]
    gate2 = mod_ref[0][:, 5 * D_MODEL:6 * D_MODEL]
    x2 = x1_ref[...] + gate2 * f
    out_ref[...] = _rms(x2) * fg_ref[...]


def _combine_call(x1, mod3, tw, final_g, gathered):
    return pl.pallas_call(
        _combine_kernel,
        out_shape=jax.ShapeDtypeStruct((N_TOK, D_MODEL), F32),
        grid=(N_TILES,),
        in_specs=[
            pl.BlockSpec((TOK_TILE, D_MODEL), lambda t: (t, 0)),
            pl.BlockSpec((1, 1, N_MOD * D_MODEL), lambda t: (_mod_row(t), 0, 0)),
            pl.BlockSpec((TOK_TILE, LANES), lambda t: (t, 0)),
            pl.BlockSpec((1, D_MODEL), lambda t: (0, 0)),
            pl.BlockSpec((TOK_TILE, TOP_K * D_MODEL), lambda t: (t, 0)),
        ],
        out_specs=pl.BlockSpec((TOK_TILE, D_MODEL), lambda t: (t, 0)),
        compiler_params=_params(("arbitrary",)),
        name="combine",
    )(x1, mod3, tw, final_g, gathered)


def kernel(x_prompt, x_sample, state_gla_fwd, state_gla_bwd, c, c_ctx, norm1_g, w_mod, b_mod, w_in,
           w_alpha, b_alpha, gla_norm_g, w_pool_grp, pool_scale, w_branch_gla, w_branch_pool, w_out,
           norm2_g, w_router, b_router, w_gate, b_gate, w_up, b_up, w_down, b_down, final_norm_g):
    l = 0
    x_ctx = x_prompt.reshape(N_CTX, D_MODEL)
    x_lat = x_sample.reshape(N_LAT, D_MODEL)

    cvec = jnp.concatenate([c_ctx[None, :], c, jnp.zeros((8 - 1 - DEC_BATCH, D_MODEL), F32)], axis=0)
    mod = _mod_call(cvec, w_mod[l], b_mod[l][None, :])
    mod3 = mod.reshape(8, 1, N_MOD * D_MODEL)

    w_in_b = w_in[l].astype(BF16)
    w_main = w_in_b[:, :MAIN_W]
    w_alr = w_in_b[:, MAIN_W:MAIN_W + ALR_W]
    w_xp = w_in_b[:, MAIN_W + ALR_W:MAIN_W + ALR_W + POOL_W]
    w_mg = w_in_b[:, MAIN_W + ALR_W + POOL_W:]
    q, k, v, og, alr, xp, mg = _inproj_call(x_ctx, x_lat, mod3, norm1_g[l][None, :],
                                            w_main, w_alr, w_xp, w_mg)

    zpad = jnp.zeros((GLA_LOWRANK, QK_W), F32)
    wa_f = jnp.concatenate([w_alpha[l, 0], zpad], axis=0)
    wa_b = jnp.concatenate([zpad, w_alpha[l, 1]], axis=0)
    o_f, o_b, s_f, s_b = _gla_call(q, k, v, alr, wa_f, b_alpha[l, 0][None, :], wa_b,
                                   b_alpha[l, 1][None, :], state_gla_fwd[:, l], state_gla_bwd[:, l])

    pooled_c = _pool_ctx_call(xp)
    pooled_l = _pool_lat_call(xp)

    w_router_pad = jnp.pad(w_router[l], ((0, 0), (0, LANES - N_EXPERTS)))
    b_router_pad = jnp.pad(b_router[l], (0, LANES - N_EXPERTS))[None, :]
    x1, h2, top_idx, top_w = _post_call(
        x_ctx, x_lat, mod3, o_f, o_b, og, pooled_c, pooled_l, mg,
        gla_norm_g[l].reshape(1, V_W), w_pool_grp[l].astype(BF16), pool_scale[l][None, :],
        w_branch_gla[l].astype(BF16), w_branch_pool[l].astype(BF16), w_out[l].astype(BF16),
        norm2_g[l][None, :], w_router_pad, b_router_pad)

    rank, cnt = _route_call(top_idx)
    counts = cnt[0, :N_EXPERTS].astype(I32)
    padded = (counts + MOE_BLOCK - 1) // MOE_BLOCK * MOE_BLOCK
    pad_end = jnp.cumsum(padded).astype(I32)
    pad_start = pad_end - padded
    block_first = jnp.arange(N_SLOT_BLOCKS, dtype=I32) * MOE_BLOCK
    block_e = jnp.minimum(jnp.sum((pad_end[None, :] <= block_first[:, None]).astype(I32), axis=1),
                          N_EXPERTS - 1).astype(I32)
    n_used = (pad_end[-1:] // MOE_BLOCK).astype(I32)
    experts = jnp.arange(N_EXPERTS, dtype=I32)
    tk = top_idx[:, :TOP_K]
    pos = jnp.sum(jnp.where(tk[:, :, None] == experts, pad_start, 0), axis=-1) + rank[:, :TOP_K]
    pos_flat = pos.reshape(N_TOK * TOP_K).astype(I32)

    hs = _dispatch_call(pad_end, block_e, pos_flat, h2)
    y = _moe_call(block_e, n_used, hs,
                  w_gate[l], b_gate[l][:, None, :], w_up[l], b_up[l][:, None, :],
                  w_down[l], b_down[l][:, None, :])
    gathered = _sc_gather_rows(y, pos_flat).reshape(N_TOK, TOP_K * D_MODEL)
    out = _combine_call(x1, mod3, top_w, final_norm_g[None, :], gathered)

    y_prompt = out[:N_CTX].reshape(BATCH, SEQ, D_MODEL)
    y_sample = out[N_CTX:].reshape(DEC_BATCH, DEC_SEQ, D_MODEL)
    return (y_prompt, y_sample, s_f[:, None], s_b[:, None])
```

```python
import functools

import jax
import jax.numpy as jnp
from jax import lax
from jax.experimental import pallas as pl
from jax.experimental.pallas import tpu as pltpu
from jax.experimental.pallas import tpu_sc as plsc

F32 = jnp.float32
BF16 = jnp.bfloat16
I32 = jnp.int32

D_MODEL = 1024
BATCH = 32
SEQ = 256
DEC_BATCH = 4
DEC_SEQ = 2048
GRID_W = 64
GLA_HEADS = 4
GLA_DK = 128
GLA_DV = 256
GLA_LOWRANK = 16
GLA_TAU = 16.0
GLA_CHUNK = 64
POOL_GROUPS = 4
POOL_GROUP_DIM = 128
POOL_WINDOWS = (2, 4, 8, 16)
N_EXPERTS = 32
TOP_K = 4
D_FF = 1024
SWIGLU_LIMIT = 7.0
SWIGLU_ALPHA = 1.702
MOE_BLOCK = 256
NORM_EPS = 1e-6
N_MOD = 6

QK_W = GLA_HEADS * GLA_DK
V_W = GLA_HEADS * GLA_DV
POOL_W = POOL_GROUPS * POOL_GROUP_DIM
MAIN_W = 2 * QK_W + 2 * V_W
ALR_W = 2 * GLA_LOWRANK
MG_W = 2 * D_MODEL

N_CTX = BATCH * SEQ
N_LAT = DEC_BATCH * DEC_SEQ
N_TOK = N_CTX + N_LAT
N_SLOT_BLOCKS = -(-(N_TOK * TOP_K + N_EXPERTS * (MOE_BLOCK - 1)) // MOE_BLOCK)
N_SLOTS = N_SLOT_BLOCKS * MOE_BLOCK

LANES = 128
TOK_TILE = 256
N_TILES = N_TOK // TOK_TILE
CTX_TILES = N_CTX // TOK_TILE
LAT_TILES_PER_SEQ = DEC_SEQ // TOK_TILE
ROUTE_TILE = 512
VMEM_LIMIT = 56 * 1024 * 1024

GLA_SEQS = 4
CTX_CHUNKS = SEQ // GLA_CHUNK
LAT_CHUNKS = DEC_SEQ // GLA_CHUNK
CHUNKS_PER_TILE = TOK_TILE // GLA_CHUNK
GLA_CTX_STEPS = (BATCH // GLA_SEQS) * CTX_CHUNKS
TILE_GRID = 8

NT_DIMS = (((1,), (1,)), ((), ()))
TN_DIMS = (((0,), (0,)), ((), ()))

assert DEC_BATCH == GLA_SEQS and SEQ == TOK_TILE and N_TILES == TILE_GRID * TILE_GRID


def _params(semantics, vmem=VMEM_LIMIT):
    return pltpu.CompilerParams(dimension_semantics=semantics, vmem_limit_bytes=vmem)


def _split_bf16(a):
    hi = a.astype(BF16)
    lo = (a - hi.astype(F32)).astype(BF16)
    return hi, lo


def _dot(a, b):
    return jnp.dot(a, b, preferred_element_type=F32)


def _dot3(a, b):
    a_hi, a_lo = _split_bf16(a)
    b_hi, b_lo = _split_bf16(b)
    return _dot(a_hi, b_hi) + _dot(a_lo, b_hi) + _dot(a_hi, b_lo)


def _sigmoid(x):
    return 1.0 / (1.0 + jnp.exp(-x))


def _rms(x):
    return x * lax.rsqrt(jnp.mean(x * x, axis=-1, keepdims=True) + NORM_EPS)


def _mod_row(t):
    return jnp.where(t < CTX_TILES, 0, 1 + (t - CTX_TILES) // LAT_TILES_PER_SEQ)


def _store_tile(t):
    u = t - CTX_TILES
    return jnp.where(t < CTX_TILES, t,
                     CTX_TILES + DEC_BATCH * (u % LAT_TILES_PER_SEQ) + u // LAT_TILES_PER_SEQ)


def _ctx_tile(t):
    return jnp.minimum(t, CTX_TILES - 1)


def _lat_tile(t):
    return jnp.maximum(t - CTX_TILES, 0)


def _mod_kernel(c_ref, w_ref, b_ref, o_ref):
    c = c_ref[...]
    o_ref[...] = _dot3(c * _sigmoid(c), w_ref[...]) + b_ref[...]


def _mod_call(cvec, w_mod, b_mod):
    rows = cvec.shape[0]
    return pl.pallas_call(
        _mod_kernel,
        out_shape=jax.ShapeDtypeStruct((rows, N_MOD * D_MODEL), F32),
        grid=(N_MOD,),
        in_specs=[
            pl.BlockSpec((rows, D_MODEL), lambda j: (0, 0)),
            pl.BlockSpec((D_MODEL, D_MODEL), lambda j: (0, j)),
            pl.BlockSpec((1, D_MODEL), lambda j: (0, j)),
        ],
        out_specs=pl.BlockSpec((rows, D_MODEL), lambda j: (0, j)),
        compiler_params=_params(("arbitrary",)),
        name="mod",
    )(cvec, w_mod, b_mod)


def _inproj_kernel(xc_ref, xl_ref, mod_ref, g_ref, wmain_ref, walr_ref, wxp_ref, wmg_ref,
                   q_ref, k_ref, v_ref, og_ref, alr_ref, xp_ref, mg_ref):
    t = pl.program_id(0)
    x = jnp.where(t < CTX_TILES, xc_ref[...], xl_ref[...])
    mod = mod_ref[0]
    shift1 = mod[:, 0:D_MODEL]
    scale1 = mod[:, D_MODEL:2 * D_MODEL]
    h = _rms(x) * g_ref[...]
    h = (h * (1.0 + scale1) + shift1).astype(BF16)
    z = _dot(h, wmain_ref[...])
    q_ref[...] = (z[:, 0:QK_W] * (GLA_DK ** -0.5)).astype(BF16)
    k_ref[...] = z[:, QK_W:2 * QK_W].astype(BF16)
    v_ref[...] = z[:, 2 * QK_W:2 * QK_W + V_W].astype(BF16)
    og_ref[...] = z[:, 2 * QK_W + V_W:MAIN_W].astype(BF16)
    alr_ref[...] = _dot(h, walr_ref[...])
    xp_ref[...] = _dot(h, wxp_ref[...])
    mg_ref[...] = _dot(h, wmg_ref[...]).astype(BF16)


def _inproj_call(x_ctx, x_lat, mod3, norm1_g, w_main, w_alr, w_xp, w_mg):
    const = lambda t: (0, 0)
    stored = lambda t: (_store_tile(t), 0)
    widths = (QK_W, QK_W, V_W, V_W, ALR_W, POOL_W, MG_W)
    dtypes = (BF16, BF16, BF16, BF16, F32, F32, BF16)
    return pl.pallas_call(
        _inproj_kernel,
        out_shape=[jax.ShapeDtypeStruct((N_TOK, w), dt) for w, dt in zip(widths, dtypes)],
        grid=(N_TILES,),
        in_specs=[
            pl.BlockSpec((TOK_TILE, D_MODEL), lambda t: (_ctx_tile(t), 0)),
            pl.BlockSpec((TOK_TILE, D_MODEL), lambda t: (_lat_tile(t), 0)),
            pl.BlockSpec((1, 1, N_MOD * D_MODEL), lambda t: (_mod_row(t), 0, 0)),
            pl.BlockSpec((1, D_MODEL), const),
            pl.BlockSpec((D_MODEL, MAIN_W), const),
            pl.BlockSpec((D_MODEL, ALR_W), const),
            pl.BlockSpec((D_MODEL, POOL_W), const),
            pl.BlockSpec((D_MODEL, MG_W), const),
        ],
        out_specs=[pl.BlockSpec((TOK_TILE, w), stored) for w in widths],
        compiler_params=_params(("arbitrary",)),
        name="inproj",
    )(x_ctx, x_lat, mod3, norm1_g, w_main, w_alr, w_xp, w_mg)


def _gla_direction(q_ref, k_ref, v_ref, alr_ref, wa_ref, ba_ref, o_ref, st_ref, slot0, rev):
    rows = GLA_SEQS * GLA_CHUNK
    alr = jnp.concatenate([alr_ref[0, s] for s in range(GLA_SEQS)], axis=0)
    a = _dot3(alr, wa_ref[...]) + ba_ref[...]
    g = (jnp.minimum(a, 0.0) - jnp.log(1.0 + jnp.exp(-jnp.abs(a)))) * (1.0 / GLA_TAU)

    row = lax.broadcasted_iota(I32, (rows, rows), 0)
    col = lax.broadcasted_iota(I32, (rows, rows), 1)
    same = (row // GLA_CHUNK) == (col // GLA_CHUNK)
    tri_all = same & ((col >= row) if rev else (col <= row))
    tri_b = jnp.where(tri_all, 1.0, 0.0).astype(BF16)
    g_hi, g_lo = _split_bf16(g)
    bcum_all = _dot(tri_b, g_hi) + _dot(tri_b, g_lo)

    r64 = lax.broadcasted_iota(I32, (GLA_CHUNK, GLA_CHUNK), 0)
    c64 = lax.broadcasted_iota(I32, (GLA_CHUNK, GLA_CHUNK), 1)
    tri = (c64 >= r64) if rev else (c64 <= r64)

    for s in range(GLA_SEQS):
        bcum = bcum_all[s * GLA_CHUNK:(s + 1) * GLA_CHUNK]
        blast = bcum[0:1] if rev else bcum[GLA_CHUNK - 1:GLA_CHUNK]
        bmid = bcum[GLA_CHUNK // 2:GLA_CHUNK // 2 + 1]
        e_q = jnp.exp(bcum - bmid)
        e_k = jnp.exp(bmid - bcum)
        e_in = jnp.exp(bcum)
        e_out = jnp.exp(blast - bcum)
        e_last = jnp.exp(blast)
        q = q_ref[0, s].astype(F32)
        k = k_ref[0, s].astype(F32)
        for h in range(GLA_HEADS):
            ks = slice(h * GLA_DK, (h + 1) * GLA_DK)
            vs = slice(h * GLA_DV, (h + 1) * GLA_DV)
            qh = q[:, ks]
            kh = k[:, ks]
            vh = v_ref[0, s, :, vs]
            att = lax.dot_general((qh * e_q[:, ks]).astype(BF16), (kh * e_k[:, ks]).astype(BF16),
                                  NT_DIMS, preferred_element_type=F32)
            att = jnp.where(tri, att, 0.0).astype(BF16)
            st = st_ref[slot0 + s, h]
            o_inter = lax.dot_general((qh * e_in[:, ks]).astype(BF16), st.astype(BF16),
                                      NT_DIMS, preferred_element_type=F32)
            o_ref[0, s, :, vs] = o_inter + _dot(att, vh)
            upd = lax.dot_general(vh, (kh * e_out[:, ks]).astype(BF16), TN_DIMS,
                                  preferred_element_type=F32)
            st_ref[slot0 + s, h] = st * e_last[:, ks] + upd


def _gla_kernel(qf_ref, kf_ref, vf_ref, af_ref, qb_ref, kb_ref, vb_ref, ab_ref,
                waf_ref, baf_ref, wab_ref, bab_ref, s0f_ref, s0b_ref,
                of_ref, ob_ref, sf_ref, sb_ref, st_ref):
    i = pl.program_id(0)
    is_ctx = i < GLA_CTX_STEPS
    chunk = jnp.where(is_ctx, i % CTX_CHUNKS, i - GLA_CTX_STEPS)

    @pl.when(is_ctx & (chunk == 0))
    def _():
        st_ref[...] = jnp.zeros(st_ref.shape, F32)

    @pl.when(i == GLA_CTX_STEPS)
    def _():
        for s in range(GLA_SEQS):
            for h in range(GLA_HEADS):
                st_ref[s, h] = s0f_ref[s, h].T
                st_ref[GLA_SEQS + s, h] = s0b_ref[s, h].T

    _gla_direction(qf_ref, kf_ref, vf_ref, af_ref, waf_ref, baf_ref, of_ref, st_ref, 0, False)
    _gla_direction(qb_ref, kb_ref, vb_ref, ab_ref, wab_ref, bab_ref, ob_ref, st_ref, GLA_SEQS, True)

    @pl.when(is_ctx & (chunk == CTX_CHUNKS - 1))
    def _():
        for s in range(GLA_SEQS):
            for h in range(GLA_HEADS):
                sf_ref[s, h] = st_ref[s, h].T
                sb_ref[s, h] = st_ref[GLA_SEQS + s, h].T


def _gla_block(i, rev):
    is_ctx = i < GLA_CTX_STEPS
    group = i // CTX_CHUNKS
    c_ctx = i % CTX_CHUNKS
    c_lat = i - GLA_CTX_STEPS
    if rev:
        c_ctx = CTX_CHUNKS - 1 - c_ctx
        c_lat = LAT_CHUNKS - 1 - c_lat
    j = c_lat // CHUNKS_PER_TILE
    per_row = TILE_GRID // GLA_SEQS
    a = jnp.where(is_ctx, group // per_row, CTX_TILES // TILE_GRID + j // per_row)
    b = jnp.where(is_ctx, group % per_row, j % per_row)
    c = jnp.where(is_ctx, c_ctx, c_lat % CHUNKS_PER_TILE)
    return (a, b, c, 0)


def _gla_call(q, k, v, alr, wa_f, ba_f, wa_b, ba_b, s0_f, s0_b):
    def view(arr):
        return arr.reshape(TILE_GRID, TILE_GRID, TOK_TILE, arr.shape[-1])

    def spec(width, rev):
        return pl.BlockSpec((1, GLA_SEQS, GLA_CHUNK, width), lambda i: _gla_block(i, rev))

    const = lambda i: (0, 0)
    st_block = (GLA_SEQS, GLA_HEADS, GLA_DK, GLA_DV)
    whole_state = pl.BlockSpec(st_block, lambda i: (0, 0, 0, 0))
    ctx_state = pl.BlockSpec(
        st_block, lambda i: (jnp.minimum(i // CTX_CHUNKS, BATCH // GLA_SEQS - 1), 0, 0, 0))
    in_specs = []
    for rev in (False, True):
        in_specs += [spec(QK_W, rev), spec(QK_W, rev), spec(V_W, rev), spec(ALR_W, rev)]
    in_specs += [pl.BlockSpec((ALR_W, QK_W), const), pl.BlockSpec((1, QK_W), const)] * 2
    in_specs += [whole_state, whole_state]
    o_shape = jax.ShapeDtypeStruct((TILE_GRID, TILE_GRID, TOK_TILE, V_W), F32)
    s_shape = jax.ShapeDtypeStruct((BATCH, GLA_HEADS, GLA_DK, GLA_DV), F32)
    qv, kv, vv, av = view(q), view(k), view(v), view(alr)
    o_f, o_b, s_f, s_b = pl.pallas_call(
        _gla_kernel,
        out_shape=[o_shape, o_shape, s_shape, s_shape],
        grid=(GLA_CTX_STEPS + LAT_CHUNKS,),
        in_specs=in_specs,
        out_specs=[spec(V_W, False), spec(V_W, True), ctx_state, ctx_state],
        scratch_shapes=[pltpu.VMEM((2 * GLA_SEQS, GLA_HEADS, GLA_DV, GLA_DK), F32)],
        compiler_params=_params(("arbitrary",)),
        name="gla",
    )(qv, kv, vv, av, qv, kv, vv, av, wa_f, ba_f, wa_b, ba_b, s0_f, s0_b)
    return o_f.reshape(N_TOK, V_W), o_b.reshape(N_TOK, V_W), s_f, s_b


def _band(n, w, block):
    row = lax.broadcasted_iota(I32, (n, n), 0)
    col = lax.broadcasted_iota(I32, (n, n), 1)
    inside = (col >= row - w // 2) & (col <= row + w // 2 - 1)
    if block < n:
        inside = inside & ((row // block) == (col // block))
    return jnp.where(inside, 1.0, 0.0).astype(BF16)


def _win_count(p, n, w):
    return jnp.minimum(p + w // 2 - 1, n - 1) - jnp.maximum(p - w // 2, 0) + 1


def _pool_ctx_kernel(x_ref, o_ref):
    p = lax.broadcasted_iota(I32, (SEQ, POOL_GROUP_DIM), 0)
    for gi, w in enumerate(POOL_WINDOWS):
        cs = slice(gi * POOL_GROUP_DIM, (gi + 1) * POOL_GROUP_DIM)
        x = x_ref[:, cs]
        hi, lo = _split_bf16(x)
        band = _band(SEQ, w, SEQ)
        s = _dot(band, hi) + _dot(band, lo)
        cnt = _win_count(p, SEQ, w).astype(F32)
        o_ref[:, cs] = s / cnt - x


def _pool_ctx_call(xp):
    spec = pl.BlockSpec((SEQ, POOL_W), lambda b: (b, 0))
    return pl.pallas_call(
        _pool_ctx_kernel,
        out_shape=jax.ShapeDtypeStruct((N_CTX, POOL_W), F32),
        grid=(BATCH,),
        in_specs=[spec],
        out_specs=spec,
        compiler_params=_params(("arbitrary",)),
        name="pool_ctx",
    )(xp)


POOL_HALO = (max(POOL_WINDOWS) // 2) * GRID_W


def _pool_lat_kernel(x_ref, o_ref, pad_ref):
    rows = DEC_SEQ // GRID_W
    p = lax.broadcasted_iota(I32, (DEC_SEQ, POOL_GROUP_DIM), 0)
    r = p // GRID_W
    cidx = p % GRID_W
    zeros = jnp.zeros((POOL_HALO, POOL_GROUP_DIM), F32)
    pad_ref[0:POOL_HALO, :] = zeros
    pad_ref[POOL_HALO + DEC_SEQ:2 * POOL_HALO + DEC_SEQ, :] = zeros
    for gi, w in enumerate(POOL_WINDOWS):
        cs = slice(gi * POOL_GROUP_DIM, (gi + 1) * POOL_GROUP_DIM)
        band = _band(TOK_TILE, w, GRID_W)
        for t in range(LAT_TILES_PER_SEQ):
            hi, lo = _split_bf16(x_ref[t, 0, :, cs])
            pad_ref[POOL_HALO + t * TOK_TILE:POOL_HALO + (t + 1) * TOK_TILE, :] = (
                _dot(band, hi) + _dot(band, lo))
        acc = jnp.zeros((DEC_SEQ, POOL_GROUP_DIM), F32)
        for dr in range(-(w // 2), w // 2):
            start = POOL_HALO + dr * GRID_W
            acc = acc + pad_ref[start:start + DEC_SEQ, :]
        cnt = (_win_count(r, rows, w) * _win_count(cidx, GRID_W, w)).astype(F32)
        pooled = acc / cnt
        for t in range(LAT_TILES_PER_SEQ):
            rs = slice(t * TOK_TILE, (t + 1) * TOK_TILE)
            o_ref[t, 0, :, cs] = pooled[rs] - x_ref[t, 0, :, cs]


def _pool_lat_call(xp):
    view = xp.reshape(N_TILES // DEC_BATCH, DEC_BATCH, TOK_TILE, POOL_W)
    blk = (LAT_TILES_PER_SEQ, 1, TOK_TILE, POOL_W)
    out = pl.pallas_call(
        _pool_lat_kernel,
        out_shape=jax.ShapeDtypeStruct((LAT_TILES_PER_SEQ, DEC_BATCH, TOK_TILE, POOL_W), F32),
        grid=(DEC_BATCH,),
        in_specs=[pl.BlockSpec(blk, lambda s: (CTX_TILES // DEC_BATCH // LAT_TILES_PER_SEQ, s, 0, 0))],
        out_specs=pl.BlockSpec(blk, lambda s: (0, s, 0, 0)),
        scratch_shapes=[pltpu.VMEM((DEC_SEQ + 2 * POOL_HALO, POOL_GROUP_DIM), F32)],
        compiler_params=_params(("arbitrary",)),
        name="pool_lat",
    )(view)
    return out.reshape(N_LAT, POOL_W)


def _post_kernel(xc_ref, xl_ref, mod_ref, of_ref, ob_ref, og_ref, pc_ref, pl_ref, mg_ref, gng_ref,
                 wpg_ref, psc_ref, wbg_ref, wbp_ref, wout_ref, n2g_ref, wr_ref, br_ref,
                 x1_ref, h2_ref, idx_ref, tw_ref):
    t = pl.program_id(0)
    is_ctx = t < CTX_TILES
    x = jnp.where(is_ctx, xc_ref[...], xl_ref[...])
    pooled = jnp.where(is_ctx, pc_ref[...], pl_ref[...])
    mod = mod_ref[0]
    gate1 = mod[:, 2 * D_MODEL:3 * D_MODEL]
    shift2 = mod[:, 3 * D_MODEL:4 * D_MODEL]
    scale2 = mod[:, 4 * D_MODEL:5 * D_MODEL]

    o = of_ref[...] + ob_ref[...]
    og = og_ref[...].astype(F32)
    gated = []
    for h in range(GLA_HEADS):
        vs = slice(h * GLA_DV, (h + 1) * GLA_DV)
        oh = _rms(o[:, vs]) * gng_ref[:, vs]
        ogh = og[:, vs]
        gated.append((oh * (ogh * _sigmoid(ogh))).astype(BF16))
    br_gla = _dot(jnp.concatenate(gated, axis=-1), wbg_ref[...])

    pm = []
    for gi in range(POOL_GROUPS):
        cs = slice(gi * POOL_GROUP_DIM, (gi + 1) * POOL_GROUP_DIM)
        pmg = _dot(pooled[:, cs].astype(BF16), wpg_ref[gi]) * psc_ref[:, cs]
        pm.append(pmg.astype(BF16))
    br_pool = _dot(jnp.concatenate(pm, axis=-1), wbp_ref[...])

    mg = mg_ref[...].astype(F32)
    merged = _sigmoid(mg[:, 0:D_MODEL]) * br_gla + _sigmoid(mg[:, D_MODEL:MG_W]) * br_pool
    m = _dot(merged.astype(BF16), wout_ref[...])
    x1 = x + gate1 * m
    x1_ref[...] = x1
    h2 = _rms(x1) * n2g_ref[...]
    h2 = h2 * (1.0 + scale2) + shift2
    h2_ref[...] = h2

    logits = _dot3(h2, wr_ref[...]) + br_ref[...]
    lane = lax.broadcasted_iota(I32, (TOK_TILE, LANES), 1)
    lane_f = lane.astype(F32)
    neg = jnp.float32(-jnp.inf)
    cur = jnp.where(lane < N_EXPERTS, logits, neg)
    vals, idxs = [], []
    for _ in range(TOP_K):
        mx = jnp.max(cur, axis=-1, keepdims=True)
        ix = jnp.min(jnp.where(cur == mx, lane_f, float(LANES)), axis=-1, keepdims=True)
        vals.append(mx)
        idxs.append(ix)
        cur = jnp.where(lane_f == ix, neg, cur)
    ex = [jnp.exp(vv - vals[0]) for vv in vals]
    tot = ex[0] + ex[1] + ex[2] + ex[3]
    idx_out = jnp.zeros((TOK_TILE, LANES), F32)
    w_out = jnp.zeros((TOK_TILE, LANES), F32)
    for kk in range(TOP_K):
        idx_out = jnp.where(lane == kk, idxs[kk], idx_out)
        w_out = jnp.where(lane == kk, ex[kk] / tot, w_out)
    idx_ref[...] = idx_out.astype(I32)
    tw_ref[...] = w_out


def _post_call(x_ctx, x_lat, mod3, o_f, o_b, og, pooled_c, pooled_l, mg, gng, wpg, psc, wbg, wbp,
               wout, n2g, wr, br):
    row = lambda t: (t, 0)
    const = lambda t: (0, 0)
    stored = lambda t: (_store_tile(t), 0)
    return pl.pallas_call(
        _post_kernel,
        out_shape=[
            jax.ShapeDtypeStruct((N_TOK, D_MODEL), F32),
            jax.ShapeDtypeStruct((N_TOK, D_MODEL), F32),
            jax.ShapeDtypeStruct((N_TOK, LANES), I32),
            jax.ShapeDtypeStruct((N_TOK, LANES), F32),
        ],
        grid=(N_TILES,),
        in_specs=[
            pl.BlockSpec((TOK_TILE, D_MODEL), lambda t: (_ctx_tile(t), 0)),
            pl.BlockSpec((TOK_TILE, D_MODEL), lambda t: (_lat_tile(t), 0)),
            pl.BlockSpec((1, 1, N_MOD * D_MODEL), lambda t: (_mod_row(t), 0, 0)),
            pl.BlockSpec((TOK_TILE, V_W), stored),
            pl.BlockSpec((TOK_TILE, V_W), stored),
            pl.BlockSpec((TOK_TILE, V_W), stored),
            pl.BlockSpec((TOK_TILE, POOL_W), lambda t: (_ctx_tile(t), 0)),
            pl.BlockSpec((TOK_TILE, POOL_W),
                         lambda t: (jnp.maximum(_store_tile(t) - CTX_TILES, 0), 0)),
            pl.BlockSpec((TOK_TILE, MG_W), stored),
            pl.BlockSpec((1, V_W), const),
            pl.BlockSpec((POOL_GROUPS, POOL_GROUP_DIM, POOL_GROUP_DIM), lambda t: (0, 0, 0)),
            pl.BlockSpec((1, POOL_W), const),
            pl.BlockSpec((V_W, D_MODEL), const),
            pl.BlockSpec((POOL_W, D_MODEL), const),
            pl.BlockSpec((D_MODEL, D_MODEL), const),
            pl.BlockSpec((1, D_MODEL), const),
            pl.BlockSpec((D_MODEL, LANES), const),
            pl.BlockSpec((1, LANES), const),
        ],
        out_specs=[
            pl.BlockSpec((TOK_TILE, D_MODEL), row),
            pl.BlockSpec((TOK_TILE, D_MODEL), row),
            pl.BlockSpec((TOK_TILE, LANES), row),
            pl.BlockSpec((TOK_TILE, LANES), row),
        ],
        compiler_params=_params(("arbitrary",)),
        name="post",
    )(x_ctx, x_lat, mod3, o_f, o_b, og, pooled_c, pooled_l, mg, gng, wpg, psc, wbg, wbp, wout, n2g,
      wr, br)


def _route_kernel(idx_ref, rank_ref, cnt_ref, carry_ref):
    t = pl.program_id(0)

    @pl.when(t == 0)
    def _():
        carry_ref[...] = jnp.zeros((1, LANES), F32)

    idx = idx_ref[...]
    lane = lax.broadcasted_iota(I32, (ROUTE_TILE, LANES), 1)
    sel = [lane == idx[:, kk:kk + 1] for kk in range(TOP_K)]
    onehot = jnp.zeros((ROUTE_TILE, LANES), F32)
    for kk in range(TOP_K):
        onehot = onehot + jnp.where(sel[kk], 1.0, 0.0)
    row = lax.broadcasted_iota(I32, (ROUTE_TILE, ROUTE_TILE), 0)
    col = lax.broadcasted_iota(I32, (ROUTE_TILE, ROUTE_TILE), 1)
    strict = jnp.where(col < row, 1.0, 0.0).astype(BF16)
    before = _dot(strict, onehot.astype(BF16)) + carry_ref[...]
    rank = jnp.zeros((ROUTE_TILE, LANES), F32)
    for kk in range(TOP_K):
        rk = jnp.sum(jnp.where(sel[kk], before, 0.0), axis=-1, keepdims=True)
        rank = jnp.where(lane == kk, rk, rank)
    rank_ref[...] = rank.astype(I32)
    carry_ref[...] = carry_ref[...] + jnp.sum(onehot, axis=0, keepdims=True)
    cnt_ref[...] = jnp.broadcast_to(carry_ref[...], (8, LANES))


def _route_call(idx):
    return pl.pallas_call(
        _route_kernel,
        out_shape=[
            jax.ShapeDtypeStruct((N_TOK, LANES), I32),
            jax.ShapeDtypeStruct((8, LANES), F32),
        ],
        grid=(N_TOK // ROUTE_TILE,),
        in_specs=[pl.BlockSpec((ROUTE_TILE, LANES), lambda t: (t, 0))],
        out_specs=[
            pl.BlockSpec((ROUTE_TILE, LANES), lambda t: (t, 0)),
            pl.BlockSpec((8, LANES), lambda t: (0, 0)),
        ],
        scratch_shapes=[pltpu.VMEM((1, LANES), F32)],
        compiler_params=_params(("arbitrary",)),
        name="route",
    )(idx)


def _moe_kernel(be_ref, nu_ref, x_ref, wg_ref, bg_ref, wu_ref, bu_ref, wd_ref, bd_ref, y_ref,
                wgb_ref, wub_ref, wdb_ref):
    b = pl.program_id(0)
    n_used = nu_ref[0]
    e = be_ref[b]
    prev = be_ref[jnp.maximum(b - 1, 0)]
    live = b < n_used

    @pl.when(live & ((b == 0) | (e != prev)))
    def _():
        wgb_ref[...] = wg_ref[0].astype(BF16)
        wub_ref[...] = wu_ref[0].astype(BF16)
        wdb_ref[...] = wd_ref[0].astype(BF16)

    @pl.when(live)
    def _():
        x = x_ref[...].astype(BF16)
        gate = jnp.minimum(_dot(x, wgb_ref[...]) + bg_ref[0], SWIGLU_LIMIT)
        up = jnp.clip(_dot(x, wub_ref[...]) + bu_ref[0], -SWIGLU_LIMIT, SWIGLU_LIMIT)
        act = (up + 1.0) * (gate * _sigmoid(SWIGLU_ALPHA * gate))
        y_ref[...] = _dot(act.astype(BF16), wdb_ref[...]) + bd_ref[0]

    @pl.when(jnp.logical_not(live))
    def _():
        y_ref[...] = jnp.zeros((MOE_BLOCK, D_MODEL), F32)


def _moe_call(block_e, n_used, hs, w_gate, b_gate, w_up, b_up, w_down, b_down):
    def blk(b, be, nu):
        return jnp.minimum(b, nu[0] - 1)

    row = lambda b, be, nu: (blk(b, be, nu), 0)
    wsel = lambda b, be, nu: (be[blk(b, be, nu)], 0, 0)
    return pl.pallas_call(
        _moe_kernel,
        out_shape=jax.ShapeDtypeStruct((N_SLOTS, D_MODEL), F32),
        grid_spec=pltpu.PrefetchScalarGridSpec(
            num_scalar_prefetch=2,
            grid=(N_SLOT_BLOCKS,),
            in_specs=[
                pl.BlockSpec((MOE_BLOCK, D_MODEL), row),
                pl.BlockSpec((1, D_MODEL, D_FF), wsel),
                pl.BlockSpec((1, 1, D_FF), wsel),
                pl.BlockSpec((1, D_MODEL, D_FF), wsel),
                pl.BlockSpec((1, 1, D_FF), wsel),
                pl.BlockSpec((1, D_FF, D_MODEL), wsel),
                pl.BlockSpec((1, 1, D_MODEL), wsel),
            ],
            out_specs=pl.BlockSpec((MOE_BLOCK, D_MODEL), lambda b, be, nu: (b, 0)),
            scratch_shapes=[
                pltpu.VMEM((D_MODEL, D_FF), BF16),
                pltpu.VMEM((D_MODEL, D_FF), BF16),
                pltpu.VMEM((D_FF, D_MODEL), BF16),
            ],
        ),
        compiler_params=_params(("arbitrary",)),
        name="moe",
    )(block_e, n_used, hs, w_gate, b_gate, w_up, b_up, w_down, b_down)


SC_CORES = 2
SC_SUBCORES = 16
SC_WORKERS = SC_CORES * SC_SUBCORES
SC_ROWS = 64


def _sc_gather_rows(table, idx):
    n_idx = idx.shape[0]
    width = table.shape[1]
    per_worker = n_idx // SC_WORKERS
    n_chunks = per_worker // SC_ROWS
    assert n_chunks * SC_ROWS * SC_WORKERS == n_idx
    mesh = plsc.VectorSubcoreMesh(core_axis_name="c", subcore_axis_name="s")

    @functools.partial(
        pl.kernel, mesh=mesh,
        out_type=jax.ShapeDtypeStruct((n_idx, width), table.dtype),
        scratch_types=[pltpu.VMEM((SC_ROWS,), I32), pltpu.VMEM((SC_ROWS, width), table.dtype),
                       pltpu.SemaphoreType.DMA],
        name="sc_gather",
    )
    def gather(table_hbm, idx_hbm, out_hbm, idx_v, rows_v, sem):
        worker = lax.axis_index("s") * SC_CORES + lax.axis_index("c")
        base = worker * per_worker

        @pl.loop(0, n_chunks)
        def _(ch):
            off = pl.multiple_of(base + ch * SC_ROWS, SC_ROWS)
            pltpu.sync_copy(idx_hbm.at[pl.ds(off, SC_ROWS)], idx_v)
            pltpu.async_copy(table_hbm.at[idx_v], rows_v, sem).wait()
            pltpu.sync_copy(rows_v, out_hbm.at[pl.ds(off, SC_ROWS)])

    return gather(table, idx)


def _sc_scatter_rows(rows, idx3, n_out):
    n_rows, width = rows.shape
    n_chunks = n_rows // SC_ROWS // SC_WORKERS
    assert n_chunks * SC_ROWS * SC_WORKERS == n_rows and idx3.shape == (n_rows // SC_ROWS, TOP_K, SC_ROWS)
    mesh = plsc.VectorSubcoreMesh(core_axis_name="c", subcore_axis_name="s")

    @functools.partial(
        pl.kernel, mesh=mesh,
        out_type=jax.ShapeDtypeStruct((n_out, width), rows.dtype),
        scratch_types=[pltpu.VMEM((TOP_K, SC_ROWS), I32), pltpu.VMEM((SC_ROWS, width), rows.dtype),
                       pltpu.SemaphoreType.DMA],
        name="sc_scatter",
    )
    def scatter(rows_hbm, idx_hbm, out_hbm, idx_v, rows_v, sem):
        worker = lax.axis_index("s") * SC_CORES + lax.axis_index("c")

        @pl.loop(0, n_chunks)
        def _(ch):
            chunk = worker * n_chunks + ch
            pltpu.sync_copy(idx_hbm.at[chunk], idx_v)
            pltpu.sync_copy(rows_hbm.at[pl.ds(pl.multiple_of(chunk * SC_ROWS, SC_ROWS), SC_ROWS)],
                            rows_v)
            for kk in range(TOP_K):
                pltpu.async_copy(rows_v, out_hbm.at[idx_v.at[kk]], sem).wait()

    return scatter(rows, idx3)


def _combine_kernel(x1_ref, mod_ref, tw_ref, fg_ref, g_ref, out_ref):
    tw = tw_ref[...]
    f = jnp.zeros((TOK_TILE, D_MODEL), F32)
    for kk in range(TOP_K):
        f = f + g_ref[kk] * tw[:, kk:kk + 1]
    gate2 = mod_ref[0][:, 5 * D_MODEL:6 * D_MODEL]
    x2 = x1_ref[...] + gate2 * f
    out_ref[...] = _rms(x2) * fg_ref[...]


def _combine_call(x1, mod3, tw, final_g, gathered):
    return pl.pallas_call(
        _combine_kernel,
        out_shape=jax.ShapeDtypeStruct((N_TOK, D_MODEL), F32),
        grid=(N_TILES,),
        in_specs=[
            pl.BlockSpec((TOK_TILE, D_MODEL), lambda t: (t, 0)),
            pl.BlockSpec((1, 1, N_MOD * D_MODEL), lambda t: (_mod_row(t), 0, 0)),
            pl.BlockSpec((TOK_TILE, LANES), lambda t: (t, 0)),
            pl.BlockSpec((1, D_MODEL), lambda t: (0, 0)),
            pl.BlockSpec((TOP_K, TOK_TILE, D_MODEL), lambda t: (0, t, 0)),
        ],
        out_specs=pl.BlockSpec((TOK_TILE, D_MODEL), lambda t: (t, 0)),
        compiler_params=_params(("arbitrary",)),
        name="combine",
    )(x1, mod3, tw, final_g, gathered)


def kernel(x_prompt, x_sample, state_gla_fwd, state_gla_bwd, c, c_ctx, norm1_g, w_mod, b_mod, w_in,
           w_alpha, b_alpha, gla_norm_g, w_pool_grp, pool_scale, w_branch_gla, w_branch_pool, w_out,
           norm2_g, w_router, b_router, w_gate, b_gate, w_up, b_up, w_down, b_down, final_norm_g):
    l = 0
    x_ctx = x_prompt.reshape(N_CTX, D_MODEL)
    x_lat = x_sample.reshape(N_LAT, D_MODEL)

    cvec = jnp.concatenate([c_ctx[None, :], c, jnp.zeros((8 - 1 - DEC_BATCH, D_MODEL), F32)], axis=0)
    mod = _mod_call(cvec, w_mod[l], b_mod[l][None, :])
    mod3 = mod.reshape(8, 1, N_MOD * D_MODEL)

    w_in_b = w_in[l].astype(BF16)
    w_main = w_in_b[:, :MAIN_W]
    w_alr = w_in_b[:, MAIN_W:MAIN_W + ALR_W]
    w_xp = w_in_b[:, MAIN_W + ALR_W:MAIN_W + ALR_W + POOL_W]
    w_mg = w_in_b[:, MAIN_W + ALR_W + POOL_W:]
    q, k, v, og, alr, xp, mg = _inproj_call(x_ctx, x_lat, mod3, norm1_g[l][None, :],
                                            w_main, w_alr, w_xp, w_mg)

    zpad = jnp.zeros((GLA_LOWRANK, QK_W), F32)
    wa_f = jnp.concatenate([w_alpha[l, 0], zpad], axis=0)
    wa_b = jnp.concatenate([zpad, w_alpha[l, 1]], axis=0)
    o_f, o_b, s_f, s_b = _gla_call(q, k, v, alr, wa_f, b_alpha[l, 0][None, :], wa_b,
                                   b_alpha[l, 1][None, :], state_gla_fwd[:, l], state_gla_bwd[:, l])

    pooled_c = _pool_ctx_call(xp)
    pooled_l = _pool_lat_call(xp)

    w_router_pad = jnp.pad(w_router[l], ((0, 0), (0, LANES - N_EXPERTS)))
    b_router_pad = jnp.pad(b_router[l], (0, LANES - N_EXPERTS))[None, :]
    x1, h2, top_idx, top_w = _post_call(
        x_ctx, x_lat, mod3, o_f, o_b, og, pooled_c, pooled_l, mg,
        gla_norm_g[l].reshape(1, V_W), w_pool_grp[l].astype(BF16), pool_scale[l][None, :],
        w_branch_gla[l].astype(BF16), w_branch_pool[l].astype(BF16), w_out[l].astype(BF16),
        norm2_g[l][None, :], w_router_pad, b_router_pad)

    rank, cnt = _route_call(top_idx)
    counts = cnt[0, :N_EXPERTS].astype(I32)
    padded = (counts + MOE_BLOCK - 1) // MOE_BLOCK * MOE_BLOCK
    pad_end = jnp.cumsum(padded).astype(I32)
    pad_start = pad_end - padded
    block_first = jnp.arange(N_SLOT_BLOCKS, dtype=I32) * MOE_BLOCK
    block_e = jnp.minimum(jnp.sum((pad_end[None, :] <= block_first[:, None]).astype(I32), axis=1),
                          N_EXPERTS - 1).astype(I32)
    n_used = (pad_end[-1:] // MOE_BLOCK).astype(I32)
    experts = jnp.arange(N_EXPERTS, dtype=I32)
    tk = top_idx[:, :TOP_K]
    pos = jnp.sum(jnp.where(tk[:, :, None] == experts, pad_start, 0), axis=-1) + rank[:, :TOP_K]
    pos = pos.astype(I32)
    pos_by_choice = pos.T
    pos_chunks = pos_by_choice.reshape(TOP_K, N_TOK // SC_ROWS, SC_ROWS).transpose(1, 0, 2)

    hs = _sc_scatter_rows(h2, pos_chunks, N_SLOTS)
    y = _moe_call(block_e, n_used, hs,
                  w_gate[l], b_gate[l][:, None, :], w_up[l], b_up[l][:, None, :],
                  w_down[l], b_down[l][:, None, :])
    gathered = _sc_gather_rows(y, pos_by_choice.reshape(TOP_K * N_TOK))
    out = _combine_call(x1, mod3, top_w, final_norm_g[None, :],
                        gathered.reshape(TOP_K, N_TOK, D_MODEL))

    y_prompt = out[:N_CTX].reshape(BATCH, SEQ, D_MODEL)
    y_sample = out[N_CTX:].reshape(DEC_BATCH, DEC_SEQ, D_MODEL)
    return (y_prompt, y_sample, s_f[:, None], s_b[:, None])
```

```python
import functools

import jax
import jax.numpy as jnp
from jax import lax
from jax.experimental import pallas as pl
from jax.experimental.pallas import tpu as pltpu
from jax.experimental.pallas import tpu_sc as plsc

F32 = jnp.float32
BF16 = jnp.bfloat16
I32 = jnp.int32

D_MODEL = 1024
BATCH = 32
SEQ = 256
DEC_BATCH = 4
DEC_SEQ = 2048
GRID_W = 64
GLA_HEADS = 4
GLA_DK = 128
GLA_DV = 256
GLA_LOWRANK = 16
GLA_TAU = 16.0
GLA_CHUNK = 64
POOL_GROUPS = 4
POOL_GROUP_DIM = 128
POOL_WINDOWS = (2, 4, 8, 16)
N_EXPERTS = 32
TOP_K = 4
D_FF = 1024
SWIGLU_LIMIT = 7.0
SWIGLU_ALPHA = 1.702
MOE_BLOCK = 256
NORM_EPS = 1e-6
N_MOD = 6

QK_W = GLA_HEADS * GLA_DK
V_W = GLA_HEADS * GLA_DV
POOL_W = POOL_GROUPS * POOL_GROUP_DIM
MAIN_W = 2 * QK_W + 2 * V_W
ALR_W = 2 * GLA_LOWRANK
MG_W = 2 * D_MODEL

N_CTX = BATCH * SEQ
N_LAT = DEC_BATCH * DEC_SEQ
N_TOK = N_CTX + N_LAT
N_SLOT_BLOCKS = -(-(N_TOK * TOP_K + N_EXPERTS * (MOE_BLOCK - 1)) // MOE_BLOCK)
N_SLOTS = N_SLOT_BLOCKS * MOE_BLOCK

LANES = 128
TOK_TILE = 256
N_TILES = N_TOK // TOK_TILE
CTX_TILES = N_CTX // TOK_TILE
LAT_TILES_PER_SEQ = DEC_SEQ // TOK_TILE
ROUTE_TILE = 512
VMEM_LIMIT = 56 * 1024 * 1024

GLA_SEQS = 4
CTX_CHUNKS = SEQ // GLA_CHUNK
LAT_CHUNKS = DEC_SEQ // GLA_CHUNK
CHUNKS_PER_TILE = TOK_TILE // GLA_CHUNK
GLA_CTX_STEPS = (BATCH // GLA_SEQS) * CTX_CHUNKS
TILE_GRID = 8

NT_DIMS = (((1,), (1,)), ((), ()))
TN_DIMS = (((0,), (0,)), ((), ()))

assert DEC_BATCH == GLA_SEQS and SEQ == TOK_TILE and N_TILES == TILE_GRID * TILE_GRID


def _params(semantics, vmem=VMEM_LIMIT):
    return pltpu.CompilerParams(dimension_semantics=semantics, vmem_limit_bytes=vmem)


def _split_bf16(a):
    hi = a.astype(BF16)
    lo = (a - hi.astype(F32)).astype(BF16)
    return hi, lo


def _dot(a, b):
    return jnp.dot(a, b, preferred_element_type=F32)


def _dot3(a, b):
    a_hi, a_lo = _split_bf16(a)
    b_hi, b_lo = _split_bf16(b)
    return _dot(a_hi, b_hi) + _dot(a_lo, b_hi) + _dot(a_hi, b_lo)


def _sigmoid(x):
    return 1.0 / (1.0 + jnp.exp(-x))


HALF_W = D_MODEL // 2
HIGH_HALF_MASK = -65536


def _pack_halves(lo, hi):
    lo_bits = pltpu.bitcast(lo.astype(BF16).astype(F32), I32)
    hi_bits = pltpu.bitcast(hi.astype(BF16).astype(F32), I32)
    return lax.shift_right_logical(lo_bits, 16) | (hi_bits & HIGH_HALF_MASK)


def _unpack_halves(words):
    lo = pltpu.bitcast(lax.shift_left(words, 16), F32)
    hi = pltpu.bitcast(words & HIGH_HALF_MASK, F32)
    return lo, hi


def _rms(x):
    return x * lax.rsqrt(jnp.mean(x * x, axis=-1, keepdims=True) + NORM_EPS)


def _mod_row(t):
    return jnp.where(t < CTX_TILES, 0, 1 + (t - CTX_TILES) // LAT_TILES_PER_SEQ)


def _store_tile(t):
    u = t - CTX_TILES
    return jnp.where(t < CTX_TILES, t,
                     CTX_TILES + DEC_BATCH * (u % LAT_TILES_PER_SEQ) + u // LAT_TILES_PER_SEQ)


def _ctx_tile(t):
    return jnp.minimum(t, CTX_TILES - 1)


def _lat_tile(t):
    return jnp.maximum(t - CTX_TILES, 0)


def _mod_kernel(c_ref, w_ref, b_ref, o_ref):
    c = c_ref[...]
    o_ref[...] = _dot3(c * _sigmoid(c), w_ref[...]) + b_ref[...]


def _mod_call(cvec, w_mod, b_mod):
    rows = cvec.shape[0]
    return pl.pallas_call(
        _mod_kernel,
        out_shape=jax.ShapeDtypeStruct((rows, N_MOD * D_MODEL), F32),
        grid=(N_MOD,),
        in_specs=[
            pl.BlockSpec((rows, D_MODEL), lambda j: (0, 0)),
            pl.BlockSpec((D_MODEL, D_MODEL), lambda j: (0, j)),
            pl.BlockSpec((1, D_MODEL), lambda j: (0, j)),
        ],
        out_specs=pl.BlockSpec((rows, D_MODEL), lambda j: (0, j)),
        compiler_params=_params(("arbitrary",)),
        name="mod",
    )(cvec, w_mod, b_mod)


def _inproj_kernel(xc_ref, xl_ref, mod_ref, g_ref, wmain_ref, walr_ref, wxp_ref, wmg_ref,
                   q_ref, k_ref, v_ref, og_ref, alr_ref, xp_ref, mg_ref):
    t = pl.program_id(0)
    x = jnp.where(t < CTX_TILES, xc_ref[...], xl_ref[...])
    mod = mod_ref[0]
    shift1 = mod[:, 0:D_MODEL]
    scale1 = mod[:, D_MODEL:2 * D_MODEL]
    h = _rms(x) * g_ref[...]
    h = (h * (1.0 + scale1) + shift1).astype(BF16)
    z = _dot(h, wmain_ref[...])
    q_ref[...] = (z[:, 0:QK_W] * (GLA_DK ** -0.5)).astype(BF16)
    k_ref[...] = z[:, QK_W:2 * QK_W].astype(BF16)
    v_ref[...] = z[:, 2 * QK_W:2 * QK_W + V_W].astype(BF16)
    og_ref[...] = z[:, 2 * QK_W + V_W:MAIN_W].astype(BF16)
    alr_ref[...] = _dot(h, walr_ref[...])
    xp_ref[...] = _dot(h, wxp_ref[...])
    mg_ref[...] = _dot(h, wmg_ref[...]).astype(BF16)


def _inproj_call(x_ctx, x_lat, mod3, norm1_g, w_main, w_alr, w_xp, w_mg):
    const = lambda t: (0, 0)
    stored = lambda t: (_store_tile(t), 0)
    widths = (QK_W, QK_W, V_W, V_W, ALR_W, POOL_W, MG_W)
    dtypes = (BF16, BF16, BF16, BF16, F32, F32, BF16)
    return pl.pallas_call(
        _inproj_kernel,
        out_shape=[jax.ShapeDtypeStruct((N_TOK, w), dt) for w, dt in zip(widths, dtypes)],
        grid=(N_TILES,),
        in_specs=[
            pl.BlockSpec((TOK_TILE, D_MODEL), lambda t: (_ctx_tile(t), 0)),
            pl.BlockSpec((TOK_TILE, D_MODEL), lambda t: (_lat_tile(t), 0)),
            pl.BlockSpec((1, 1, N_MOD * D_MODEL), lambda t: (_mod_row(t), 0, 0)),
            pl.BlockSpec((1, D_MODEL), const),
            pl.BlockSpec((D_MODEL, MAIN_W), const),
            pl.BlockSpec((D_MODEL, ALR_W), const),
            pl.BlockSpec((D_MODEL, POOL_W), const),
            pl.BlockSpec((D_MODEL, MG_W), const),
        ],
        out_specs=[pl.BlockSpec((TOK_TILE, w), stored) for w in widths],
        compiler_params=_params(("arbitrary",)),
        name="inproj",
    )(x_ctx, x_lat, mod3, norm1_g, w_main, w_alr, w_xp, w_mg)


def _gla_direction(q_ref, k_ref, v_ref, alr_ref, wa_ref, ba_ref, o_ref, st_ref, slot0, rev):
    rows = GLA_SEQS * GLA_CHUNK
    alr = jnp.concatenate([alr_ref[0, s] for s in range(GLA_SEQS)], axis=0)
    a = _dot3(alr, wa_ref[...]) + ba_ref[...]
    g = (jnp.minimum(a, 0.0) - jnp.log(1.0 + jnp.exp(-jnp.abs(a)))) * (1.0 / GLA_TAU)

    row = lax.broadcasted_iota(I32, (rows, rows), 0)
    col = lax.broadcasted_iota(I32, (rows, rows), 1)
    same = (row // GLA_CHUNK) == (col // GLA_CHUNK)
    tri_all = same & ((col >= row) if rev else (col <= row))
    tri_b = jnp.where(tri_all, 1.0, 0.0).astype(BF16)
    g_hi, g_lo = _split_bf16(g)
    bcum_all = _dot(tri_b, g_hi) + _dot(tri_b, g_lo)

    r64 = lax.broadcasted_iota(I32, (GLA_CHUNK, GLA_CHUNK), 0)
    c64 = lax.broadcasted_iota(I32, (GLA_CHUNK, GLA_CHUNK), 1)
    tri = (c64 >= r64) if rev else (c64 <= r64)

    for s in range(GLA_SEQS):
        bcum = bcum_all[s * GLA_CHUNK:(s + 1) * GLA_CHUNK]
        blast = bcum[0:1] if rev else bcum[GLA_CHUNK - 1:GLA_CHUNK]
        bmid = bcum[GLA_CHUNK // 2:GLA_CHUNK // 2 + 1]
        e_q = jnp.exp(bcum - bmid)
        e_k = jnp.exp(bmid - bcum)
        e_in = jnp.exp(bcum)
        e_out = jnp.exp(blast - bcum)
        e_last = jnp.exp(blast)
        q = q_ref[0, s].astype(F32)
        k = k_ref[0, s].astype(F32)
        for h in range(GLA_HEADS):
            ks = slice(h * GLA_DK, (h + 1) * GLA_DK)
            vs = slice(h * GLA_DV, (h + 1) * GLA_DV)
            qh = q[:, ks]
            kh = k[:, ks]
            vh = v_ref[0, s, :, vs]
            att = lax.dot_general((qh * e_q[:, ks]).astype(BF16), (kh * e_k[:, ks]).astype(BF16),
                                  NT_DIMS, preferred_element_type=F32)
            att = jnp.where(tri, att, 0.0).astype(BF16)
            st = st_ref[slot0 + s, h]
            o_inter = lax.dot_general((qh * e_in[:, ks]).astype(BF16), st.astype(BF16),
                                      NT_DIMS, preferred_element_type=F32)
            o_ref[0, s, :, vs] = o_inter + _dot(att, vh)
            upd = lax.dot_general(vh, (kh * e_out[:, ks]).astype(BF16), TN_DIMS,
                                  preferred_element_type=F32)
            st_ref[slot0 + s, h] = st * e_last[:, ks] + upd


def _gla_kernel(qf_ref, kf_ref, vf_ref, af_ref, qb_ref, kb_ref, vb_ref, ab_ref,
                waf_ref, baf_ref, wab_ref, bab_ref, s0f_ref, s0b_ref,
                of_ref, ob_ref, sf_ref, sb_ref, st_ref):
    i = pl.program_id(0)
    is_ctx = i < GLA_CTX_STEPS
    chunk = jnp.where(is_ctx, i % CTX_CHUNKS, i - GLA_CTX_STEPS)

    @pl.when(is_ctx & (chunk == 0))
    def _():
        st_ref[...] = jnp.zeros(st_ref.shape, F32)

    @pl.when(i == GLA_CTX_STEPS)
    def _():
        for s in range(GLA_SEQS):
            for h in range(GLA_HEADS):
                st_ref[s, h] = s0f_ref[s, h].T
                st_ref[GLA_SEQS + s, h] = s0b_ref[s, h].T

    _gla_direction(qf_ref, kf_ref, vf_ref, af_ref, waf_ref, baf_ref, of_ref, st_ref, 0, False)
    _gla_direction(qb_ref, kb_ref, vb_ref, ab_ref, wab_ref, bab_ref, ob_ref, st_ref, GLA_SEQS, True)

    @pl.when(is_ctx & (chunk == CTX_CHUNKS - 1))
    def _():
        for s in range(GLA_SEQS):
            for h in range(GLA_HEADS):
                sf_ref[s, h] = st_ref[s, h].T
                sb_ref[s, h] = st_ref[GLA_SEQS + s, h].T


def _gla_block(i, rev):
    is_ctx = i < GLA_CTX_STEPS
    group = i // CTX_CHUNKS
    c_ctx = i % CTX_CHUNKS
    c_lat = i - GLA_CTX_STEPS
    if rev:
        c_ctx = CTX_CHUNKS - 1 - c_ctx
        c_lat = LAT_CHUNKS - 1 - c_lat
    j = c_lat // CHUNKS_PER_TILE
    per_row = TILE_GRID // GLA_SEQS
    a = jnp.where(is_ctx, group // per_row, CTX_TILES // TILE_GRID + j // per_row)
    b = jnp.where(is_ctx, group % per_row, j % per_row)
    c = jnp.where(is_ctx, c_ctx, c_lat % CHUNKS_PER_TILE)
    return (a, b, c, 0)


def _gla_call(q, k, v, alr, wa_f, ba_f, wa_b, ba_b, s0_f, s0_b):
    def view(arr):
        return arr.reshape(TILE_GRID, TILE_GRID, TOK_TILE, arr.shape[-1])

    def spec(width, rev):
        return pl.BlockSpec((1, GLA_SEQS, GLA_CHUNK, width), lambda i: _gla_block(i, rev))

    const = lambda i: (0, 0)
    st_block = (GLA_SEQS, GLA_HEADS, GLA_DK, GLA_DV)
    whole_state = pl.BlockSpec(st_block, lambda i: (0, 0, 0, 0))
    ctx_state = pl.BlockSpec(
        st_block, lambda i: (jnp.minimum(i // CTX_CHUNKS, BATCH // GLA_SEQS - 1), 0, 0, 0))
    in_specs = []
    for rev in (False, True):
        in_specs += [spec(QK_W, rev), spec(QK_W, rev), spec(V_W, rev), spec(ALR_W, rev)]
    in_specs += [pl.BlockSpec((ALR_W, QK_W), const), pl.BlockSpec((1, QK_W), const)] * 2
    in_specs += [whole_state, whole_state]
    o_shape = jax.ShapeDtypeStruct((TILE_GRID, TILE_GRID, TOK_TILE, V_W), F32)
    s_shape = jax.ShapeDtypeStruct((BATCH, GLA_HEADS, GLA_DK, GLA_DV), F32)
    qv, kv, vv, av = view(q), view(k), view(v), view(alr)
    o_f, o_b, s_f, s_b = pl.pallas_call(
        _gla_kernel,
        out_shape=[o_shape, o_shape, s_shape, s_shape],
        grid=(GLA_CTX_STEPS + LAT_CHUNKS,),
        in_specs=in_specs,
        out_specs=[spec(V_W, False), spec(V_W, True), ctx_state, ctx_state],
        scratch_shapes=[pltpu.VMEM((2 * GLA_SEQS, GLA_HEADS, GLA_DV, GLA_DK), F32)],
        compiler_params=_params(("arbitrary",)),
        name="gla",
    )(qv, kv, vv, av, qv, kv, vv, av, wa_f, ba_f, wa_b, ba_b, s0_f, s0_b)
    return o_f.reshape(N_TOK, V_W), o_b.reshape(N_TOK, V_W), s_f, s_b


def _band(n, w, block):
    row = lax.broadcasted_iota(I32, (n, n), 0)
    col = lax.broadcasted_iota(I32, (n, n), 1)
    inside = (col >= row - w // 2) & (col <= row + w // 2 - 1)
    if block < n:
        inside = inside & ((row // block) == (col // block))
    return jnp.where(inside, 1.0, 0.0).astype(BF16)


def _win_count(p, n, w):
    return jnp.minimum(p + w // 2 - 1, n - 1) - jnp.maximum(p - w // 2, 0) + 1


def _pool_ctx_kernel(x_ref, o_ref):
    p = lax.broadcasted_iota(I32, (SEQ, POOL_GROUP_DIM), 0)
    for gi, w in enumerate(POOL_WINDOWS):
        cs = slice(gi * POOL_GROUP_DIM, (gi + 1) * POOL_GROUP_DIM)
        x = x_ref[:, cs]
        hi, lo = _split_bf16(x)
        band = _band(SEQ, w, SEQ)
        s = _dot(band, hi) + _dot(band, lo)
        cnt = _win_count(p, SEQ, w).astype(F32)
        o_ref[:, cs] = s / cnt - x


def _pool_ctx_call(xp):
    spec = pl.BlockSpec((SEQ, POOL_W), lambda b: (b, 0))
    return pl.pallas_call(
        _pool_ctx_kernel,
        out_shape=jax.ShapeDtypeStruct((N_CTX, POOL_W), F32),
        grid=(BATCH,),
        in_specs=[spec],
        out_specs=spec,
        compiler_params=_params(("arbitrary",)),
        name="pool_ctx",
    )(xp)


POOL_HALO = (max(POOL_WINDOWS) // 2) * GRID_W


def _pool_lat_kernel(x_ref, o_ref, pad_ref):
    rows = DEC_SEQ // GRID_W
    p = lax.broadcasted_iota(I32, (DEC_SEQ, POOL_GROUP_DIM), 0)
    r = p // GRID_W
    cidx = p % GRID_W
    zeros = jnp.zeros((POOL_HALO, POOL_GROUP_DIM), F32)
    pad_ref[0:POOL_HALO, :] = zeros
    pad_ref[POOL_HALO + DEC_SEQ:2 * POOL_HALO + DEC_SEQ, :] = zeros
    for gi, w in enumerate(POOL_WINDOWS):
        cs = slice(gi * POOL_GROUP_DIM, (gi + 1) * POOL_GROUP_DIM)
        band = _band(TOK_TILE, w, GRID_W)
        for t in range(LAT_TILES_PER_SEQ):
            hi, lo = _split_bf16(x_ref[t, 0, :, cs])
            pad_ref[POOL_HALO + t * TOK_TILE:POOL_HALO + (t + 1) * TOK_TILE, :] = (
                _dot(band, hi) + _dot(band, lo))
        acc = jnp.zeros((DEC_SEQ, POOL_GROUP_DIM), F32)
        for dr in range(-(w // 2), w // 2):
            start = POOL_HALO + dr * GRID_W
            acc = acc + pad_ref[start:start + DEC_SEQ, :]
        cnt = (_win_count(r, rows, w) * _win_count(cidx, GRID_W, w)).astype(F32)
        pooled = acc / cnt
        for t in range(LAT_TILES_PER_SEQ):
            rs = slice(t * TOK_TILE, (t + 1) * TOK_TILE)
            o_ref[t, 0, :, cs] = pooled[rs] - x_ref[t, 0, :, cs]


def _pool_lat_call(xp):
    view = xp.reshape(N_TILES // DEC_BATCH, DEC_BATCH, TOK_TILE, POOL_W)
    blk = (LAT_TILES_PER_SEQ, 1, TOK_TILE, POOL_W)
    out = pl.pallas_call(
        _pool_lat_kernel,
        out_shape=jax.ShapeDtypeStruct((LAT_TILES_PER_SEQ, DEC_BATCH, TOK_TILE, POOL_W), F32),
        grid=(DEC_BATCH,),
        in_specs=[pl.BlockSpec(blk, lambda s: (CTX_TILES // DEC_BATCH // LAT_TILES_PER_SEQ, s, 0, 0))],
        out_specs=pl.BlockSpec(blk, lambda s: (0, s, 0, 0)),
        scratch_shapes=[pltpu.VMEM((DEC_SEQ + 2 * POOL_HALO, POOL_GROUP_DIM), F32)],
        compiler_params=_params(("arbitrary",)),
        name="pool_lat",
    )(view)
    return out.reshape(N_LAT, POOL_W)


def _post_kernel(xc_ref, xl_ref, mod_ref, of_ref, ob_ref, og_ref, pc_ref, pl_ref, mg_ref, gng_ref,
                 wpg_ref, psc_ref, wbg_ref, wbp_ref, wout_ref, n2g_ref, wr_ref, br_ref,
                 x1_ref, h2_ref, idx_ref, tw_ref):
    t = pl.program_id(0)
    is_ctx = t < CTX_TILES
    x = jnp.where(is_ctx, xc_ref[...], xl_ref[...])
    pooled = jnp.where(is_ctx, pc_ref[...], pl_ref[...])
    mod = mod_ref[0]
    gate1 = mod[:, 2 * D_MODEL:3 * D_MODEL]
    shift2 = mod[:, 3 * D_MODEL:4 * D_MODEL]
    scale2 = mod[:, 4 * D_MODEL:5 * D_MODEL]

    o = of_ref[...] + ob_ref[...]
    og = og_ref[...].astype(F32)
    gated = []
    for h in range(GLA_HEADS):
        vs = slice(h * GLA_DV, (h + 1) * GLA_DV)
        oh = _rms(o[:, vs]) * gng_ref[:, vs]
        ogh = og[:, vs]
        gated.append((oh * (ogh * _sigmoid(ogh))).astype(BF16))
    br_gla = _dot(jnp.concatenate(gated, axis=-1), wbg_ref[...])

    pm = []
    for gi in range(POOL_GROUPS):
        cs = slice(gi * POOL_GROUP_DIM, (gi + 1) * POOL_GROUP_DIM)
        pmg = _dot(pooled[:, cs].astype(BF16), wpg_ref[gi]) * psc_ref[:, cs]
        pm.append(pmg.astype(BF16))
    br_pool = _dot(jnp.concatenate(pm, axis=-1), wbp_ref[...])

    mg = mg_ref[...].astype(F32)
    merged = _sigmoid(mg[:, 0:D_MODEL]) * br_gla + _sigmoid(mg[:, D_MODEL:MG_W]) * br_pool
    m = _dot(merged.astype(BF16), wout_ref[...])
    x1 = x + gate1 * m
    x1_ref[...] = x1
    h2 = _rms(x1) * n2g_ref[...]
    h2 = h2 * (1.0 + scale2) + shift2
    h2_ref[...] = _pack_halves(h2[:, :HALF_W], h2[:, HALF_W:])

    logits = _dot3(h2, wr_ref[...]) + br_ref[...]
    lane = lax.broadcasted_iota(I32, (TOK_TILE, LANES), 1)
    lane_f = lane.astype(F32)
    neg = jnp.float32(-jnp.inf)
    cur = jnp.where(lane < N_EXPERTS, logits, neg)
    vals, idxs = [], []
    for _ in range(TOP_K):
        mx = jnp.max(cur, axis=-1, keepdims=True)
        ix = jnp.min(jnp.where(cur == mx, lane_f, float(LANES)), axis=-1, keepdims=True)
        vals.append(mx)
        idxs.append(ix)
        cur = jnp.where(lane_f == ix, neg, cur)
    ex = [jnp.exp(vv - vals[0]) for vv in vals]
    tot = ex[0] + ex[1] + ex[2] + ex[3]
    idx_out = jnp.zeros((TOK_TILE, LANES), F32)
    w_out = jnp.zeros((TOK_TILE, LANES), F32)
    for kk in range(TOP_K):
        idx_out = jnp.where(lane == kk, idxs[kk], idx_out)
        w_out = jnp.where(lane == kk, ex[kk] / tot, w_out)
    idx_ref[...] = idx_out.astype(I32)
    tw_ref[...] = w_out


def _post_call(x_ctx, x_lat, mod3, o_f, o_b, og, pooled_c, pooled_l, mg, gng, wpg, psc, wbg, wbp,
               wout, n2g, wr, br):
    row = lambda t: (t, 0)
    const = lambda t: (0, 0)
    stored = lambda t: (_store_tile(t), 0)
    return pl.pallas_call(
        _post_kernel,
        out_shape=[
            jax.ShapeDtypeStruct((N_TOK, D_MODEL), F32),
            jax.ShapeDtypeStruct((N_TOK, HALF_W), I32),
            jax.ShapeDtypeStruct((N_TOK, LANES), I32),
            jax.ShapeDtypeStruct((N_TOK, LANES), F32),
        ],
        grid=(N_TILES,),
        in_specs=[
            pl.BlockSpec((TOK_TILE, D_MODEL), lambda t: (_ctx_tile(t), 0)),
            pl.BlockSpec((TOK_TILE, D_MODEL), lambda t: (_lat_tile(t), 0)),
            pl.BlockSpec((1, 1, N_MOD * D_MODEL), lambda t: (_mod_row(t), 0, 0)),
            pl.BlockSpec((TOK_TILE, V_W), stored),
            pl.BlockSpec((TOK_TILE, V_W), stored),
            pl.BlockSpec((TOK_TILE, V_W), stored),
            pl.BlockSpec((TOK_TILE, POOL_W), lambda t: (_ctx_tile(t), 0)),
            pl.BlockSpec((TOK_TILE, POOL_W),
                         lambda t: (jnp.maximum(_store_tile(t) - CTX_TILES, 0), 0)),
            pl.BlockSpec((TOK_TILE, MG_W), stored),
            pl.BlockSpec((1, V_W), const),
            pl.BlockSpec((POOL_GROUPS, POOL_GROUP_DIM, POOL_GROUP_DIM), lambda t: (0, 0, 0)),
            pl.BlockSpec((1, POOL_W), const),
            pl.BlockSpec((V_W, D_MODEL), const),
            pl.BlockSpec((POOL_W, D_MODEL), const),
            pl.BlockSpec((D_MODEL, D_MODEL), const),
            pl.BlockSpec((1, D_MODEL), const),
            pl.BlockSpec((D_MODEL, LANES), const),
            pl.BlockSpec((1, LANES), const),
        ],
        out_specs=[
            pl.BlockSpec((TOK_TILE, D_MODEL), row),
            pl.BlockSpec((TOK_TILE, HALF_W), row),
            pl.BlockSpec((TOK_TILE, LANES), row),
            pl.BlockSpec((TOK_TILE, LANES), row),
        ],
        compiler_params=_params(("arbitrary",)),
        name="post",
    )(x_ctx, x_lat, mod3, o_f, o_b, og, pooled_c, pooled_l, mg, gng, wpg, psc, wbg, wbp, wout, n2g,
      wr, br)


def _route_kernel(idx_ref, rank_ref, cnt_ref, carry_ref):
    t = pl.program_id(0)

    @pl.when(t == 0)
    def _():
        carry_ref[...] = jnp.zeros((1, LANES), F32)

    idx = idx_ref[...]
    lane = lax.broadcasted_iota(I32, (ROUTE_TILE, LANES), 1)
    sel = [lane == idx[:, kk:kk + 1] for kk in range(TOP_K)]
    onehot = jnp.zeros((ROUTE_TILE, LANES), F32)
    for kk in range(TOP_K):
        onehot = onehot + jnp.where(sel[kk], 1.0, 0.0)
    row = lax.broadcasted_iota(I32, (ROUTE_TILE, ROUTE_TILE), 0)
    col = lax.broadcasted_iota(I32, (ROUTE_TILE, ROUTE_TILE), 1)
    strict = jnp.where(col < row, 1.0, 0.0).astype(BF16)
    before = _dot(strict, onehot.astype(BF16)) + carry_ref[...]
    rank = jnp.zeros((ROUTE_TILE, LANES), F32)
    for kk in range(TOP_K):
        rk = jnp.sum(jnp.where(sel[kk], before, 0.0), axis=-1, keepdims=True)
        rank = jnp.where(lane == kk, rk, rank)
    rank_ref[...] = rank.astype(I32)
    carry_ref[...] = carry_ref[...] + jnp.sum(onehot, axis=0, keepdims=True)
    cnt_ref[...] = jnp.broadcast_to(carry_ref[...], (8, LANES))


def _route_call(idx):
    return pl.pallas_call(
        _route_kernel,
        out_shape=[
            jax.ShapeDtypeStruct((N_TOK, LANES), I32),
            jax.ShapeDtypeStruct((8, LANES), F32),
        ],
        grid=(N_TOK // ROUTE_TILE,),
        in_specs=[pl.BlockSpec((ROUTE_TILE, LANES), lambda t: (t, 0))],
        out_specs=[
            pl.BlockSpec((ROUTE_TILE, LANES), lambda t: (t, 0)),
            pl.BlockSpec((8, LANES), lambda t: (0, 0)),
        ],
        scratch_shapes=[pltpu.VMEM((1, LANES), F32)],
        compiler_params=_params(("arbitrary",)),
        name="route",
    )(idx)


def _moe_kernel(be_ref, nu_ref, x_ref, wg_ref, bg_ref, wu_ref, bu_ref, wd_ref, bd_ref, y_ref,
                wgb_ref, wub_ref, wdb_ref):
    b = pl.program_id(0)
    n_used = nu_ref[0]
    e = be_ref[b]
    prev = be_ref[jnp.maximum(b - 1, 0)]
    live = b < n_used

    @pl.when(live & ((b == 0) | (e != prev)))
    def _():
        wgb_ref[...] = wg_ref[0].astype(BF16)
        wub_ref[...] = wu_ref[0].astype(BF16)
        wdb_ref[...] = wd_ref[0].astype(BF16)

    @pl.when(live)
    def _():
        x_lo, x_hi = _unpack_halves(x_ref[...])
        x = jnp.concatenate([x_lo.astype(BF16), x_hi.astype(BF16)], axis=-1)
        gate = jnp.minimum(_dot(x, wgb_ref[...]) + bg_ref[0], SWIGLU_LIMIT)
        up = jnp.clip(_dot(x, wub_ref[...]) + bu_ref[0], -SWIGLU_LIMIT, SWIGLU_LIMIT)
        act = (up + 1.0) * (gate * _sigmoid(SWIGLU_ALPHA * gate))
        y = _dot(act.astype(BF16), wdb_ref[...]) + bd_ref[0]
        y_ref[...] = _pack_halves(y[:, :HALF_W], y[:, HALF_W:])

    @pl.when(jnp.logical_not(live))
    def _():
        y_ref[...] = jnp.zeros((MOE_BLOCK, HALF_W), I32)


def _moe_call(block_e, n_used, hs, w_gate, b_gate, w_up, b_up, w_down, b_down):
    def blk(b, be, nu):
        return jnp.minimum(b, nu[0] - 1)

    row = lambda b, be, nu: (blk(b, be, nu), 0)
    wsel = lambda b, be, nu: (be[blk(b, be, nu)], 0, 0)
    return pl.pallas_call(
        _moe_kernel,
        out_shape=jax.ShapeDtypeStruct((N_SLOTS, HALF_W), I32),
        grid_spec=pltpu.PrefetchScalarGridSpec(
            num_scalar_prefetch=2,
            grid=(N_SLOT_BLOCKS,),
            in_specs=[
                pl.BlockSpec((MOE_BLOCK, HALF_W), row),
                pl.BlockSpec((1, D_MODEL, D_FF), wsel),
                pl.BlockSpec((1, 1, D_FF), wsel),
                pl.BlockSpec((1, D_MODEL, D_FF), wsel),
                pl.BlockSpec((1, 1, D_FF), wsel),
                pl.BlockSpec((1, D_FF, D_MODEL), wsel),
                pl.BlockSpec((1, 1, D_MODEL), wsel),
            ],
            out_specs=pl.BlockSpec((MOE_BLOCK, HALF_W), lambda b, be, nu: (b, 0)),
            scratch_shapes=[
                pltpu.VMEM((D_MODEL, D_FF), BF16),
                pltpu.VMEM((D_MODEL, D_FF), BF16),
                pltpu.VMEM((D_FF, D_MODEL), BF16),
            ],
        ),
        compiler_params=_params(("arbitrary",)),
        name="moe",
    )(block_e, n_used, hs, w_gate, b_gate, w_up, b_up, w_down, b_down)


SC_CORES = 2
SC_SUBCORES = 16
SC_WORKERS = SC_CORES * SC_SUBCORES
SC_ROWS = 128


def _sc_gather_rows(table, idx):
    n_idx = idx.shape[0]
    width = table.shape[1]
    per_worker = n_idx // SC_WORKERS
    n_chunks = per_worker // SC_ROWS
    assert n_chunks * SC_ROWS * SC_WORKERS == n_idx
    mesh = plsc.VectorSubcoreMesh(core_axis_name="c", subcore_axis_name="s")

    @functools.partial(
        pl.kernel, mesh=mesh,
        out_type=jax.ShapeDtypeStruct((n_idx, width), table.dtype),
        scratch_types=[pltpu.VMEM((SC_ROWS,), I32), pltpu.VMEM((SC_ROWS, width), table.dtype),
                       pltpu.SemaphoreType.DMA],
        name="sc_gather",
    )
    def gather(table_hbm, idx_hbm, out_hbm, idx_v, rows_v, sem):
        worker = lax.axis_index("s") * SC_CORES + lax.axis_index("c")
        base = worker * per_worker

        @pl.loop(0, n_chunks)
        def _(ch):
            off = pl.multiple_of(base + ch * SC_ROWS, SC_ROWS)
            pltpu.sync_copy(idx_hbm.at[pl.ds(off, SC_ROWS)], idx_v)
            pltpu.async_copy(table_hbm.at[idx_v], rows_v, sem).wait()
            pltpu.sync_copy(rows_v, out_hbm.at[pl.ds(off, SC_ROWS)])

    return gather(table, idx)


def _sc_scatter_rows(rows, idx3, n_out):
    n_rows, width = rows.shape
    n_chunks = n_rows // SC_ROWS // SC_WORKERS
    assert n_chunks * SC_ROWS * SC_WORKERS == n_rows and idx3.shape == (n_rows // SC_ROWS, TOP_K, SC_ROWS)
    mesh = plsc.VectorSubcoreMesh(core_axis_name="c", subcore_axis_name="s")

    @functools.partial(
        pl.kernel, mesh=mesh,
        out_type=jax.ShapeDtypeStruct((n_out, width), rows.dtype),
        scratch_types=[pltpu.VMEM((TOP_K, SC_ROWS), I32), pltpu.VMEM((SC_ROWS, width), rows.dtype),
                       pltpu.SemaphoreType.DMA],
        name="sc_scatter",
    )
    def scatter(rows_hbm, idx_hbm, out_hbm, idx_v, rows_v, sem):
        worker = lax.axis_index("s") * SC_CORES + lax.axis_index("c")

        @pl.loop(0, n_chunks)
        def _(ch):
            chunk = worker * n_chunks + ch
            pltpu.sync_copy(idx_hbm.at[chunk], idx_v)
            pltpu.sync_copy(rows_hbm.at[pl.ds(pl.multiple_of(chunk * SC_ROWS, SC_ROWS), SC_ROWS)],
                            rows_v)
            for kk in range(TOP_K):
                pltpu.async_copy(rows_v, out_hbm.at[idx_v.at[kk]], sem).wait()

    return scatter(rows, idx3)


def _combine_kernel(x1_ref, mod_ref, tw_ref, fg_ref, g_ref, out_ref):
    tw = tw_ref[...]
    f_lo = jnp.zeros((TOK_TILE, HALF_W), F32)
    f_hi = jnp.zeros((TOK_TILE, HALF_W), F32)
    for kk in range(TOP_K):
        lo, hi = _unpack_halves(g_ref[kk])
        f_lo = f_lo + lo * tw[:, kk:kk + 1]
        f_hi = f_hi + hi * tw[:, kk:kk + 1]
    gate2 = mod_ref[0][:, 5 * D_MODEL:6 * D_MODEL]
    x2 = x1_ref[...] + gate2 * jnp.concatenate([f_lo, f_hi], axis=-1)
    out_ref[...] = _rms(x2) * fg_ref[...]


def _combine_call(x1, mod3, tw, final_g, gathered):
    return pl.pallas_call(
        _combine_kernel,
        out_shape=jax.ShapeDtypeStruct((N_TOK, D_MODEL), F32),
        grid=(N_TILES,),
        in_specs=[
            pl.BlockSpec((TOK_TILE, D_MODEL), lambda t: (t, 0)),
            pl.BlockSpec((1, 1, N_MOD * D_MODEL), lambda t: (_mod_row(t), 0, 0)),
            pl.BlockSpec((TOK_TILE, LANES), lambda t: (t, 0)),
            pl.BlockSpec((1, D_MODEL), lambda t: (0, 0)),
            pl.BlockSpec((TOP_K, TOK_TILE, HALF_W), lambda t: (0, t, 0)),
        ],
        out_specs=pl.BlockSpec((TOK_TILE, D_MODEL), lambda t: (t, 0)),
        compiler_params=_params(("arbitrary",)),
        name="combine",
    )(x1, mod3, tw, final_g, gathered)


def kernel(x_prompt, x_sample, state_gla_fwd, state_gla_bwd, c, c_ctx, norm1_g, w_mod, b_mod, w_in,
           w_alpha, b_alpha, gla_norm_g, w_pool_grp, pool_scale, w_branch_gla, w_branch_pool, w_out,
           norm2_g, w_router, b_router, w_gate, b_gate, w_up, b_up, w_down, b_down, final_norm_g):
    l = 0
    x_ctx = x_prompt.reshape(N_CTX, D_MODEL)
    x_lat = x_sample.reshape(N_LAT, D_MODEL)

    cvec = jnp.concatenate([c_ctx[None, :], c, jnp.zeros((8 - 1 - DEC_BATCH, D_MODEL), F32)], axis=0)
    mod = _mod_call(cvec, w_mod[l], b_mod[l][None, :])
    mod3 = mod.reshape(8, 1, N_MOD * D_MODEL)

    w_in_b = w_in[l].astype(BF16)
    w_main = w_in_b[:, :MAIN_W]
    w_alr = w_in_b[:, MAIN_W:MAIN_W + ALR_W]
    w_xp = w_in_b[:, MAIN_W + ALR_W:MAIN_W + ALR_W + POOL_W]
    w_mg = w_in_b[:, MAIN_W + ALR_W + POOL_W:]
    q, k, v, og, alr, xp, mg = _inproj_call(x_ctx, x_lat, mod3, norm1_g[l][None, :],
                                            w_main, w_alr, w_xp, w_mg)

    zpad = jnp.zeros((GLA_LOWRANK, QK_W), F32)
    wa_f = jnp.concatenate([w_alpha[l, 0], zpad], axis=0)
    wa_b = jnp.concatenate([zpad, w_alpha[l, 1]], axis=0)
    o_f, o_b, s_f, s_b = _gla_call(q, k, v, alr, wa_f, b_alpha[l, 0][None, :], wa_b,
                                   b_alpha[l, 1][None, :], state_gla_fwd[:, l], state_gla_bwd[:, l])

    pooled_c = _pool_ctx_call(xp)
    pooled_l = _pool_lat_call(xp)

    w_router_pad = jnp.pad(w_router[l], ((0, 0), (0, LANES - N_EXPERTS)))
    b_router_pad = jnp.pad(b_router[l], (0, LANES - N_EXPERTS))[None, :]
    x1, h2, top_idx, top_w = _post_call(
        x_ctx, x_lat, mod3, o_f, o_b, og, pooled_c, pooled_l, mg,
        gla_norm_g[l].reshape(1, V_W), w_pool_grp[l].astype(BF16), pool_scale[l][None, :],
        w_branch_gla[l].astype(BF16), w_branch_pool[l].astype(BF16), w_out[l].astype(BF16),
        norm2_g[l][None, :], w_router_pad, b_router_pad)

    rank, cnt = _route_call(top_idx)
    counts = cnt[0, :N_EXPERTS].astype(I32)
    padded = (counts + MOE_BLOCK - 1) // MOE_BLOCK * MOE_BLOCK
    pad_end = jnp.cumsum(padded).astype(I32)
    pad_start = pad_end - padded
    block_first = jnp.arange(N_SLOT_BLOCKS, dtype=I32) * MOE_BLOCK
    block_e = jnp.minimum(jnp.sum((pad_end[None, :] <= block_first[:, None]).astype(I32), axis=1),
                          N_EXPERTS - 1).astype(I32)
    n_used = (pad_end[-1:] // MOE_BLOCK).astype(I32)
    experts = jnp.arange(N_EXPERTS, dtype=I32)
    tk = top_idx[:, :TOP_K]
    pos = jnp.sum(jnp.where(tk[:, :, None] == experts, pad_start, 0), axis=-1) + rank[:, :TOP_K]
    pos = pos.astype(I32)
    pos_by_choice = pos.T
    pos_chunks = pos_by_choice.reshape(TOP_K, N_TOK // SC_ROWS, SC_ROWS).transpose(1, 0, 2)

    hs = _sc_scatter_rows(h2, pos_chunks, N_SLOTS)
    y = _moe_call(block_e, n_used, hs,
                  w_gate[l], b_gate[l][:, None, :], w_up[l], b_up[l][:, None, :],
                  w_down[l], b_down[l][:, None, :])
    gathered = _sc_gather_rows(y, pos_by_choice.reshape(TOP_K * N_TOK))
    out = _combine_call(x1, mod3, top_w, final_norm_g[None, :],
                        gathered.reshape(TOP_K, N_TOK, HALF_W))

    y_prompt = out[:N_CTX].reshape(BATCH, SEQ, D_MODEL)
    y_sample = out[N_CTX:].reshape(DEC_BATCH, DEC_SEQ, D_MODEL)
    return (y_prompt, y_sample, s_f[:, None], s_b[:, None])
```

```python
import functools

import jax
import jax.numpy as jnp
from jax import lax
from jax.experimental import pallas as pl
from jax.experimental.pallas import tpu as pltpu
from jax.experimental.pallas import tpu_sc as plsc

F32 = jnp.float32
BF16 = jnp.bfloat16
I32 = jnp.int32

D_MODEL = 1024
BATCH = 32
SEQ = 256
DEC_BATCH = 4
DEC_SEQ = 2048
GRID_W = 64
GLA_HEADS = 4
GLA_DK = 128
GLA_DV = 256
GLA_LOWRANK = 16
GLA_TAU = 16.0
GLA_CHUNK = 64
POOL_GROUPS = 4
POOL_GROUP_DIM = 128
POOL_WINDOWS = (2, 4, 8, 16)
N_EXPERTS = 32
TOP_K = 4
D_FF = 1024
SWIGLU_LIMIT = 7.0
SWIGLU_ALPHA = 1.702
MOE_BLOCK = 256
NORM_EPS = 1e-6
N_MOD = 6

QK_W = GLA_HEADS * GLA_DK
V_W = GLA_HEADS * GLA_DV
POOL_W = POOL_GROUPS * POOL_GROUP_DIM
MAIN_W = 2 * QK_W + 2 * V_W
ALR_W = 2 * GLA_LOWRANK
MG_W = 2 * D_MODEL

N_CTX = BATCH * SEQ
N_LAT = DEC_BATCH * DEC_SEQ
N_TOK = N_CTX + N_LAT
N_SLOT_BLOCKS = -(-(N_TOK * TOP_K + N_EXPERTS * (MOE_BLOCK - 1)) // MOE_BLOCK)
N_SLOTS = N_SLOT_BLOCKS * MOE_BLOCK

LANES = 128
TOK_TILE = 256
N_TILES = N_TOK // TOK_TILE
CTX_TILES = N_CTX // TOK_TILE
LAT_TILES_PER_SEQ = DEC_SEQ // TOK_TILE
ROUTE_TILE = 512
VMEM_LIMIT = 56 * 1024 * 1024

GLA_SEQS = 4
CTX_CHUNKS = SEQ // GLA_CHUNK
LAT_CHUNKS = DEC_SEQ // GLA_CHUNK
CHUNKS_PER_TILE = TOK_TILE // GLA_CHUNK
GLA_CTX_STEPS = (BATCH // GLA_SEQS) * CTX_CHUNKS
TILE_GRID = 8

NT_DIMS = (((1,), (1,)), ((), ()))
TN_DIMS = (((0,), (0,)), ((), ()))

assert DEC_BATCH == GLA_SEQS and SEQ == TOK_TILE and N_TILES == TILE_GRID * TILE_GRID


def _params(semantics, vmem=VMEM_LIMIT):
    return pltpu.CompilerParams(dimension_semantics=semantics, vmem_limit_bytes=vmem)


def _split_bf16(a):
    hi = a.astype(BF16)
    lo = (a - hi.astype(F32)).astype(BF16)
    return hi, lo


def _dot(a, b):
    return jnp.dot(a, b, preferred_element_type=F32)


def _dot3(a, b):
    a_hi, a_lo = _split_bf16(a)
    b_hi, b_lo = _split_bf16(b)
    return _dot(a_hi, b_hi) + _dot(a_lo, b_hi) + _dot(a_hi, b_lo)


def _sigmoid(x):
    return 1.0 / (1.0 + jnp.exp(-x))


HALF_W = D_MODEL // 2
HIGH_HALF_MASK = -65536


def _pack_halves(lo, hi):
    lo_bits = pltpu.bitcast(lo.astype(BF16).astype(F32), I32)
    hi_bits = pltpu.bitcast(hi.astype(BF16).astype(F32), I32)
    return lax.shift_right_logical(lo_bits, 16) | (hi_bits & HIGH_HALF_MASK)


def _unpack_halves(words):
    lo = pltpu.bitcast(lax.shift_left(words, 16), F32)
    hi = pltpu.bitcast(words & HIGH_HALF_MASK, F32)
    return lo, hi


def _rms(x):
    return x * lax.rsqrt(jnp.mean(x * x, axis=-1, keepdims=True) + NORM_EPS)


def _mod_row(t):
    return jnp.where(t < CTX_TILES, 0, 1 + (t - CTX_TILES) // LAT_TILES_PER_SEQ)


def _store_tile(t):
    u = t - CTX_TILES
    return jnp.where(t < CTX_TILES, t,
                     CTX_TILES + DEC_BATCH * (u % LAT_TILES_PER_SEQ) + u // LAT_TILES_PER_SEQ)


def _ctx_tile(t):
    return jnp.minimum(t, CTX_TILES - 1)


def _lat_tile(t):
    return jnp.maximum(t - CTX_TILES, 0)


def _mod_kernel(c_ref, w_ref, b_ref, o_ref):
    c = c_ref[...]
    o_ref[...] = _dot3(c * _sigmoid(c), w_ref[...]) + b_ref[...]


def _mod_call(cvec, w_mod, b_mod):
    rows = cvec.shape[0]
    return pl.pallas_call(
        _mod_kernel,
        out_shape=jax.ShapeDtypeStruct((rows, N_MOD * D_MODEL), F32),
        grid=(N_MOD,),
        in_specs=[
            pl.BlockSpec((rows, D_MODEL), lambda j: (0, 0)),
            pl.BlockSpec((D_MODEL, D_MODEL), lambda j: (0, j)),
            pl.BlockSpec((1, D_MODEL), lambda j: (0, j)),
        ],
        out_specs=pl.BlockSpec((rows, D_MODEL), lambda j: (0, j)),
        compiler_params=_params(("arbitrary",)),
        name="mod",
    )(cvec, w_mod, b_mod)


def _inproj_kernel(xc_ref, xl_ref, mod_ref, g_ref, wmain_ref, walr_ref, wxp_ref, wmg_ref,
                   q_ref, k_ref, v_ref, og_ref, alr_ref, xp_ref, mg_ref):
    t = pl.program_id(0)
    x = jnp.where(t < CTX_TILES, xc_ref[...], xl_ref[...])
    mod = mod_ref[0]
    shift1 = mod[:, 0:D_MODEL]
    scale1 = mod[:, D_MODEL:2 * D_MODEL]
    h = _rms(x) * g_ref[...]
    h = (h * (1.0 + scale1) + shift1).astype(BF16)
    z = _dot(h, wmain_ref[...])
    q_ref[...] = (z[:, 0:QK_W] * (GLA_DK ** -0.5)).astype(BF16)
    k_ref[...] = z[:, QK_W:2 * QK_W].astype(BF16)
    v_ref[...] = z[:, 2 * QK_W:2 * QK_W + V_W].astype(BF16)
    og_ref[...] = z[:, 2 * QK_W + V_W:MAIN_W].astype(BF16)
    alr_ref[...] = _dot(h, walr_ref[...])
    xp_ref[...] = _dot(h, wxp_ref[...])
    mg_ref[...] = _dot(h, wmg_ref[...]).astype(BF16)


def _inproj_call(x_ctx, x_lat, mod3, norm1_g, w_main, w_alr, w_xp, w_mg):
    const = lambda t: (0, 0)
    stored = lambda t: (_store_tile(t), 0)
    widths = (QK_W, QK_W, V_W, V_W, ALR_W, POOL_W, MG_W)
    dtypes = (BF16, BF16, BF16, BF16, F32, F32, BF16)
    return pl.pallas_call(
        _inproj_kernel,
        out_shape=[jax.ShapeDtypeStruct((N_TOK, w), dt) for w, dt in zip(widths, dtypes)],
        grid=(N_TILES,),
        in_specs=[
            pl.BlockSpec((TOK_TILE, D_MODEL), lambda t: (_ctx_tile(t), 0)),
            pl.BlockSpec((TOK_TILE, D_MODEL), lambda t: (_lat_tile(t), 0)),
            pl.BlockSpec((1, 1, N_MOD * D_MODEL), lambda t: (_mod_row(t), 0, 0)),
            pl.BlockSpec((1, D_MODEL), const),
            pl.BlockSpec((D_MODEL, MAIN_W), const),
            pl.BlockSpec((D_MODEL, ALR_W), const),
            pl.BlockSpec((D_MODEL, POOL_W), const),
            pl.BlockSpec((D_MODEL, MG_W), const),
        ],
        out_specs=[pl.BlockSpec((TOK_TILE, w), stored) for w in widths],
        compiler_params=_params(("arbitrary",)),
        name="inproj",
    )(x_ctx, x_lat, mod3, norm1_g, w_main, w_alr, w_xp, w_mg)


def _gla_direction(q_ref, k_ref, v_ref, alr_ref, wa_ref, ba_ref, o_ref, st_ref, slot0, rev):
    rows = GLA_SEQS * GLA_CHUNK
    alr = jnp.concatenate([alr_ref[0, s] for s in range(GLA_SEQS)], axis=0)
    a = _dot3(alr, wa_ref[...]) + ba_ref[...]
    g = (jnp.minimum(a, 0.0) - jnp.log(1.0 + jnp.exp(-jnp.abs(a)))) * (1.0 / GLA_TAU)

    row = lax.broadcasted_iota(I32, (rows, rows), 0)
    col = lax.broadcasted_iota(I32, (rows, rows), 1)
    same = (row // GLA_CHUNK) == (col // GLA_CHUNK)
    tri_all = same & ((col >= row) if rev else (col <= row))
    tri_b = jnp.where(tri_all, 1.0, 0.0).astype(BF16)
    g_hi, g_lo = _split_bf16(g)
    bcum_all = _dot(tri_b, g_hi) + _dot(tri_b, g_lo)

    r64 = lax.broadcasted_iota(I32, (GLA_CHUNK, GLA_CHUNK), 0)
    c64 = lax.broadcasted_iota(I32, (GLA_CHUNK, GLA_CHUNK), 1)
    tri = (c64 >= r64) if rev else (c64 <= r64)

    for s in range(GLA_SEQS):
        bcum = bcum_all[s * GLA_CHUNK:(s + 1) * GLA_CHUNK]
        blast = bcum[0:1] if rev else bcum[GLA_CHUNK - 1:GLA_CHUNK]
        bmid = bcum[GLA_CHUNK // 2:GLA_CHUNK // 2 + 1]
        e_q = jnp.exp(bcum - bmid)
        e_k = jnp.exp(bmid - bcum)
        e_in = jnp.exp(bcum)
        e_out = jnp.exp(blast - bcum)
        e_last = jnp.exp(blast)
        q = q_ref[0, s].astype(F32)
        k = k_ref[0, s].astype(F32)
        for h in range(GLA_HEADS):
            ks = slice(h * GLA_DK, (h + 1) * GLA_DK)
            vs = slice(h * GLA_DV, (h + 1) * GLA_DV)
            qh = q[:, ks]
            kh = k[:, ks]
            vh = v_ref[0, s, :, vs]
            att = lax.dot_general((qh * e_q[:, ks]).astype(BF16), (kh * e_k[:, ks]).astype(BF16),
                                  NT_DIMS, preferred_element_type=F32)
            att = jnp.where(tri, att, 0.0).astype(BF16)
            st = st_ref[slot0 + s, h]
            o_inter = lax.dot_general((qh * e_in[:, ks]).astype(BF16), st.astype(BF16),
                                      NT_DIMS, preferred_element_type=F32)
            o_ref[0, s, :, vs] = o_inter + _dot(att, vh)
            upd = lax.dot_general(vh, (kh * e_out[:, ks]).astype(BF16), TN_DIMS,
                                  preferred_element_type=F32)
            st_ref[slot0 + s, h] = st * e_last[:, ks] + upd


def _gla_kernel(qf_ref, kf_ref, vf_ref, af_ref, qb_ref, kb_ref, vb_ref, ab_ref,
                waf_ref, baf_ref, wab_ref, bab_ref, s0f_ref, s0b_ref,
                of_ref, ob_ref, sf_ref, sb_ref, st_ref):
    i = pl.program_id(0)
    is_ctx = i < GLA_CTX_STEPS
    chunk = jnp.where(is_ctx, i % CTX_CHUNKS, i - GLA_CTX_STEPS)

    @pl.when(is_ctx & (chunk == 0))
    def _():
        st_ref[...] = jnp.zeros(st_ref.shape, F32)

    @pl.when(i == GLA_CTX_STEPS)
    def _():
        for s in range(GLA_SEQS):
            for h in range(GLA_HEADS):
                st_ref[s, h] = s0f_ref[s, h].T
                st_ref[GLA_SEQS + s, h] = s0b_ref[s, h].T

    _gla_direction(qf_ref, kf_ref, vf_ref, af_ref, waf_ref, baf_ref, of_ref, st_ref, 0, False)
    _gla_direction(qb_ref, kb_ref, vb_ref, ab_ref, wab_ref, bab_ref, ob_ref, st_ref, GLA_SEQS, True)

    @pl.when(is_ctx & (chunk == CTX_CHUNKS - 1))
    def _():
        for s in range(GLA_SEQS):
            for h in range(GLA_HEADS):
                sf_ref[s, h] = st_ref[s, h].T
                sb_ref[s, h] = st_ref[GLA_SEQS + s, h].T


def _gla_block(i, rev):
    is_ctx = i < GLA_CTX_STEPS
    group = i // CTX_CHUNKS
    c_ctx = i % CTX_CHUNKS
    c_lat = i - GLA_CTX_STEPS
    if rev:
        c_ctx = CTX_CHUNKS - 1 - c_ctx
        c_lat = LAT_CHUNKS - 1 - c_lat
    j = c_lat // CHUNKS_PER_TILE
    per_row = TILE_GRID // GLA_SEQS
    a = jnp.where(is_ctx, group // per_row, CTX_TILES // TILE_GRID + j // per_row)
    b = jnp.where(is_ctx, group % per_row, j % per_row)
    c = jnp.where(is_ctx, c_ctx, c_lat % CHUNKS_PER_TILE)
    return (a, b, c, 0)


def _gla_call(q, k, v, alr, wa_f, ba_f, wa_b, ba_b, s0_f, s0_b):
    def view(arr):
        return arr.reshape(TILE_GRID, TILE_GRID, TOK_TILE, arr.shape[-1])

    def spec(width, rev):
        return pl.BlockSpec((1, GLA_SEQS, GLA_CHUNK, width), lambda i: _gla_block(i, rev))

    const = lambda i: (0, 0)
    st_block = (GLA_SEQS, GLA_HEADS, GLA_DK, GLA_DV)
    whole_state = pl.BlockSpec(st_block, lambda i: (0, 0, 0, 0))
    ctx_state = pl.BlockSpec(
        st_block, lambda i: (jnp.minimum(i // CTX_CHUNKS, BATCH // GLA_SEQS - 1), 0, 0, 0))
    in_specs = []
    for rev in (False, True):
        in_specs += [spec(QK_W, rev), spec(QK_W, rev), spec(V_W, rev), spec(ALR_W, rev)]
    in_specs += [pl.BlockSpec((ALR_W, QK_W), const), pl.BlockSpec((1, QK_W), const)] * 2
    in_specs += [whole_state, whole_state]
    o_shape = jax.ShapeDtypeStruct((TILE_GRID, TILE_GRID, TOK_TILE, V_W), F32)
    s_shape = jax.ShapeDtypeStruct((BATCH, GLA_HEADS, GLA_DK, GLA_DV), F32)
    qv, kv, vv, av = view(q), view(k), view(v), view(alr)
    o_f, o_b, s_f, s_b = pl.pallas_call(
        _gla_kernel,
        out_shape=[o_shape, o_shape, s_shape, s_shape],
        grid=(GLA_CTX_STEPS + LAT_CHUNKS,),
        in_specs=in_specs,
        out_specs=[spec(V_W, False), spec(V_W, True), ctx_state, ctx_state],
        scratch_shapes=[pltpu.VMEM((2 * GLA_SEQS, GLA_HEADS, GLA_DV, GLA_DK), F32)],
        compiler_params=_params(("arbitrary",)),
        name="gla",
    )(qv, kv, vv, av, qv, kv, vv, av, wa_f, ba_f, wa_b, ba_b, s0_f, s0_b)
    return o_f.reshape(N_TOK, V_W), o_b.reshape(N_TOK, V_W), s_f, s_b


def _band(n, w, block):
    row = lax.broadcasted_iota(I32, (n, n), 0)
    col = lax.broadcasted_iota(I32, (n, n), 1)
    inside = (col >= row - w // 2) & (col <= row + w // 2 - 1)
    if block < n:
        inside = inside & ((row // block) == (col // block))
    return jnp.where(inside, 1.0, 0.0).astype(BF16)


def _win_count(p, n, w):
    return jnp.minimum(p + w // 2 - 1, n - 1) - jnp.maximum(p - w // 2, 0) + 1


def _pool_ctx_kernel(x_ref, o_ref):
    p = lax.broadcasted_iota(I32, (SEQ, POOL_GROUP_DIM), 0)
    for gi, w in enumerate(POOL_WINDOWS):
        cs = slice(gi * POOL_GROUP_DIM, (gi + 1) * POOL_GROUP_DIM)
        x = x_ref[:, cs]
        hi, lo = _split_bf16(x)
        band = _band(SEQ, w, SEQ)
        s = _dot(band, hi) + _dot(band, lo)
        cnt = _win_count(p, SEQ, w).astype(F32)
        o_ref[:, cs] = s / cnt - x


def _pool_ctx_call(xp):
    spec = pl.BlockSpec((SEQ, POOL_W), lambda b: (b, 0))
    return pl.pallas_call(
        _pool_ctx_kernel,
        out_shape=jax.ShapeDtypeStruct((N_CTX, POOL_W), F32),
        grid=(BATCH,),
        in_specs=[spec],
        out_specs=spec,
        compiler_params=_params(("arbitrary",)),
        name="pool_ctx",
    )(xp)


POOL_HALO = (max(POOL_WINDOWS) // 2) * GRID_W


def _pool_lat_kernel(x_ref, o_ref, pad_ref):
    rows = DEC_SEQ // GRID_W
    p = lax.broadcasted_iota(I32, (DEC_SEQ, POOL_GROUP_DIM), 0)
    r = p // GRID_W
    cidx = p % GRID_W
    zeros = jnp.zeros((POOL_HALO, POOL_GROUP_DIM), F32)
    pad_ref[0:POOL_HALO, :] = zeros
    pad_ref[POOL_HALO + DEC_SEQ:2 * POOL_HALO + DEC_SEQ, :] = zeros
    for gi, w in enumerate(POOL_WINDOWS):
        cs = slice(gi * POOL_GROUP_DIM, (gi + 1) * POOL_GROUP_DIM)
        band = _band(TOK_TILE, w, GRID_W)
        for t in range(LAT_TILES_PER_SEQ):
            hi, lo = _split_bf16(x_ref[t, 0, :, cs])
            pad_ref[POOL_HALO + t * TOK_TILE:POOL_HALO + (t + 1) * TOK_TILE, :] = (
                _dot(band, hi) + _dot(band, lo))
        acc = jnp.zeros((DEC_SEQ, POOL_GROUP_DIM), F32)
        for dr in range(-(w // 2), w // 2):
            start = POOL_HALO + dr * GRID_W
            acc = acc + pad_ref[start:start + DEC_SEQ, :]
        cnt = (_win_count(r, rows, w) * _win_count(cidx, GRID_W, w)).astype(F32)
        pooled = acc / cnt
        for t in range(LAT_TILES_PER_SEQ):
            rs = slice(t * TOK_TILE, (t + 1) * TOK_TILE)
            o_ref[t, 0, :, cs] = pooled[rs] - x_ref[t, 0, :, cs]


def _pool_lat_call(xp):
    view = xp.reshape(N_TILES // DEC_BATCH, DEC_BATCH, TOK_TILE, POOL_W)
    blk = (LAT_TILES_PER_SEQ, 1, TOK_TILE, POOL_W)
    out = pl.pallas_call(
        _pool_lat_kernel,
        out_shape=jax.ShapeDtypeStruct((LAT_TILES_PER_SEQ, DEC_BATCH, TOK_TILE, POOL_W), F32),
        grid=(DEC_BATCH,),
        in_specs=[pl.BlockSpec(blk, lambda s: (CTX_TILES // DEC_BATCH // LAT_TILES_PER_SEQ, s, 0, 0))],
        out_specs=pl.BlockSpec(blk, lambda s: (0, s, 0, 0)),
        scratch_shapes=[pltpu.VMEM((DEC_SEQ + 2 * POOL_HALO, POOL_GROUP_DIM), F32)],
        compiler_params=_params(("arbitrary",)),
        name="pool_lat",
    )(view)
    return out.reshape(N_LAT, POOL_W)


def _post_kernel(xc_ref, xl_ref, mod_ref, of_ref, ob_ref, og_ref, pc_ref, pl_ref, mg_ref, gng_ref,
                 wpg_ref, psc_ref, wbg_ref, wbp_ref, wout_ref, n2g_ref, wr_ref, br_ref,
                 x1_ref, h2_ref, idx_ref, tw_ref):
    t = pl.program_id(0)
    is_ctx = t < CTX_TILES
    x = jnp.where(is_ctx, xc_ref[...], xl_ref[...])
    pooled = jnp.where(is_ctx, pc_ref[...], pl_ref[...])
    mod = mod_ref[0]
    gate1 = mod[:, 2 * D_MODEL:3 * D_MODEL]
    shift2 = mod[:, 3 * D_MODEL:4 * D_MODEL]
    scale2 = mod[:, 4 * D_MODEL:5 * D_MODEL]

    o = of_ref[...] + ob_ref[...]
    og = og_ref[...].astype(F32)
    gated = []
    for h in range(GLA_HEADS):
        vs = slice(h * GLA_DV, (h + 1) * GLA_DV)
        oh = _rms(o[:, vs]) * gng_ref[:, vs]
        ogh = og[:, vs]
        gated.append((oh * (ogh * _sigmoid(ogh))).astype(BF16))
    br_gla = _dot(jnp.concatenate(gated, axis=-1), wbg_ref[...])

    pm = []
    for gi in range(POOL_GROUPS):
        cs = slice(gi * POOL_GROUP_DIM, (gi + 1) * POOL_GROUP_DIM)
        pmg = _dot(pooled[:, cs].astype(BF16), wpg_ref[gi]) * psc_ref[:, cs]
        pm.append(pmg.astype(BF16))
    br_pool = _dot(jnp.concatenate(pm, axis=-1), wbp_ref[...])

    mg = mg_ref[...].astype(F32)
    merged = _sigmoid(mg[:, 0:D_MODEL]) * br_gla + _sigmoid(mg[:, D_MODEL:MG_W]) * br_pool
    m = _dot(merged.astype(BF16), wout_ref[...])
    x1 = x + gate1 * m
    x1_ref[...] = x1
    h2 = _rms(x1) * n2g_ref[...]
    h2 = h2 * (1.0 + scale2) + shift2
    h2_ref[...] = _pack_halves(h2[:, :HALF_W], h2[:, HALF_W:])

    logits = _dot3(h2, wr_ref[...]) + br_ref[...]
    lane = lax.broadcasted_iota(I32, (TOK_TILE, LANES), 1)
    lane_f = lane.astype(F32)
    neg = jnp.float32(-jnp.inf)
    cur = jnp.where(lane < N_EXPERTS, logits, neg)
    vals, idxs = [], []
    for _ in range(TOP_K):
        mx = jnp.max(cur, axis=-1, keepdims=True)
        ix = jnp.min(jnp.where(cur == mx, lane_f, float(LANES)), axis=-1, keepdims=True)
        vals.append(mx)
        idxs.append(ix)
        cur = jnp.where(lane_f == ix, neg, cur)
    ex = [jnp.exp(vv - vals[0]) for vv in vals]
    tot = ex[0] + ex[1] + ex[2] + ex[3]
    idx_out = jnp.zeros((TOK_TILE, LANES), F32)
    w_out = jnp.zeros((TOK_TILE, LANES), F32)
    for kk in range(TOP_K):
        idx_out = jnp.where(lane == kk, idxs[kk], idx_out)
        w_out = jnp.where(lane == kk, ex[kk] / tot, w_out)
    idx_ref[...] = idx_out.astype(I32)
    tw_ref[...] = w_out


def _post_call(x_ctx, x_lat, mod3, o_f, o_b, og, pooled_c, pooled_l, mg, gng, wpg, psc, wbg, wbp,
               wout, n2g, wr, br):
    row = lambda t: (t, 0)
    const = lambda t: (0, 0)
    stored = lambda t: (_store_tile(t), 0)
    return pl.pallas_call(
        _post_kernel,
        out_shape=[
            jax.ShapeDtypeStruct((N_TOK, D_MODEL), F32),
            jax.ShapeDtypeStruct((N_TOK, HALF_W), I32),
            jax.ShapeDtypeStruct((N_TOK, LANES), I32),
            jax.ShapeDtypeStruct((N_TOK, LANES), F32),
        ],
        grid=(N_TILES,),
        in_specs=[
            pl.BlockSpec((TOK_TILE, D_MODEL), lambda t: (_ctx_tile(t), 0)),
            pl.BlockSpec((TOK_TILE, D_MODEL), lambda t: (_lat_tile(t), 0)),
            pl.BlockSpec((1, 1, N_MOD * D_MODEL), lambda t: (_mod_row(t), 0, 0)),
            pl.BlockSpec((TOK_TILE, V_W), stored),
            pl.BlockSpec((TOK_TILE, V_W), stored),
            pl.BlockSpec((TOK_TILE, V_W), stored),
            pl.BlockSpec((TOK_TILE, POOL_W), lambda t: (_ctx_tile(t), 0)),
            pl.BlockSpec((TOK_TILE, POOL_W),
                         lambda t: (jnp.maximum(_store_tile(t) - CTX_TILES, 0), 0)),
            pl.BlockSpec((TOK_TILE, MG_W), stored),
            pl.BlockSpec((1, V_W), const),
            pl.BlockSpec((POOL_GROUPS, POOL_GROUP_DIM, POOL_GROUP_DIM), lambda t: (0, 0, 0)),
            pl.BlockSpec((1, POOL_W), const),
            pl.BlockSpec((V_W, D_MODEL), const),
            pl.BlockSpec((POOL_W, D_MODEL), const),
            pl.BlockSpec((D_MODEL, D_MODEL), const),
            pl.BlockSpec((1, D_MODEL), const),
            pl.BlockSpec((D_MODEL, LANES), const),
            pl.BlockSpec((1, LANES), const),
        ],
        out_specs=[
            pl.BlockSpec((TOK_TILE, D_MODEL), row),
            pl.BlockSpec((TOK_TILE, HALF_W), row),
            pl.BlockSpec((TOK_TILE, LANES), row),
            pl.BlockSpec((TOK_TILE, LANES), row),
        ],
        compiler_params=_params(("arbitrary",)),
        name="post",
    )(x_ctx, x_lat, mod3, o_f, o_b, og, pooled_c, pooled_l, mg, gng, wpg, psc, wbg, wbp, wout, n2g,
      wr, br)


def _route_kernel(idx_ref, rank_ref, cnt_ref, carry_ref):
    t = pl.program_id(0)

    @pl.when(t == 0)
    def _():
        carry_ref[...] = jnp.zeros((1, LANES), F32)

    idx = idx_ref[...]
    lane = lax.broadcasted_iota(I32, (ROUTE_TILE, LANES), 1)
    sel = [lane == idx[:, kk:kk + 1] for kk in range(TOP_K)]
    onehot = jnp.zeros((ROUTE_TILE, LANES), F32)
    for kk in range(TOP_K):
        onehot = onehot + jnp.where(sel[kk], 1.0, 0.0)
    row = lax.broadcasted_iota(I32, (ROUTE_TILE, ROUTE_TILE), 0)
    col = lax.broadcasted_iota(I32, (ROUTE_TILE, ROUTE_TILE), 1)
    strict = jnp.where(col < row, 1.0, 0.0).astype(BF16)
    before = _dot(strict, onehot.astype(BF16)) + carry_ref[...]
    rank = jnp.zeros((ROUTE_TILE, LANES), F32)
    for kk in range(TOP_K):
        rk = jnp.sum(jnp.where(sel[kk], before, 0.0), axis=-1, keepdims=True)
        rank = jnp.where(lane == kk, rk, rank)
    rank_ref[...] = rank.astype(I32)
    carry_ref[...] = carry_ref[...] + jnp.sum(onehot, axis=0, keepdims=True)
    cnt_ref[...] = jnp.broadcast_to(carry_ref[...], (8, LANES))


def _route_call(idx):
    return pl.pallas_call(
        _route_kernel,
        out_shape=[
            jax.ShapeDtypeStruct((N_TOK, LANES), I32),
            jax.ShapeDtypeStruct((8, LANES), F32),
        ],
        grid=(N_TOK // ROUTE_TILE,),
        in_specs=[pl.BlockSpec((ROUTE_TILE, LANES), lambda t: (t, 0))],
        out_specs=[
            pl.BlockSpec((ROUTE_TILE, LANES), lambda t: (t, 0)),
            pl.BlockSpec((8, LANES), lambda t: (0, 0)),
        ],
        scratch_shapes=[pltpu.VMEM((1, LANES), F32)],
        compiler_params=_params(("arbitrary",)),
        name="route",
    )(idx)


def _moe_kernel(fb_ref, nb_ref, nu_ref, x_hbm, wg_ref, bg_ref, wu_ref, bu_ref, wd_ref, bd_ref, y_hbm,
                xbuf, ybuf, wgb_ref, wub_ref, wdb_ref, sem_x, sem_y):
    e = pl.program_id(0)
    n_blocks = nb_ref[e]
    first = fb_ref[e]

    def rows_of(block):
        return pl.ds(pl.multiple_of(block * MOE_BLOCK, MOE_BLOCK), MOE_BLOCK)

    def x_copy(block, slot):
        return pltpu.make_async_copy(x_hbm.at[rows_of(block)], xbuf.at[slot], sem_x.at[slot])

    def y_copy(block, slot):
        return pltpu.make_async_copy(ybuf.at[slot], y_hbm.at[rows_of(block)], sem_y.at[slot])

    @pl.when(n_blocks > 0)
    def _():
        x_copy(first, 0).start()
        wgb_ref[...] = wg_ref[0].astype(BF16)
        wub_ref[...] = wu_ref[0].astype(BF16)
        wdb_ref[...] = wd_ref[0].astype(BF16)

        def body(j, carry):
            slot = j & 1
            x_copy(first + j, slot).wait()

            @pl.when(j + 1 < n_blocks)
            def _():
                x_copy(first + j + 1, 1 - slot).start()

            x_lo, x_hi = _unpack_halves(xbuf[slot])
            x = jnp.concatenate([x_lo.astype(BF16), x_hi.astype(BF16)], axis=-1)
            gate = jnp.minimum(_dot(x, wgb_ref[...]) + bg_ref[0], SWIGLU_LIMIT)
            up = jnp.clip(_dot(x, wub_ref[...]) + bu_ref[0], -SWIGLU_LIMIT, SWIGLU_LIMIT)
            act = (up + 1.0) * (gate * _sigmoid(SWIGLU_ALPHA * gate))
            y = _dot(act.astype(BF16), wdb_ref[...]) + bd_ref[0]

            @pl.when(j >= 2)
            def _():
                y_copy(first + j - 2, slot).wait()

            ybuf[slot] = _pack_halves(y[:, :HALF_W], y[:, HALF_W:])
            y_copy(first + j, slot).start()
            return carry

        lax.fori_loop(0, n_blocks, body, 0)

        @pl.when(n_blocks >= 2)
        def _():
            y_copy(first + n_blocks - 2, n_blocks & 1).wait()

        y_copy(first + n_blocks - 1, (n_blocks - 1) & 1).wait()

    @pl.when(e == N_EXPERTS - 1)
    def _():
        ybuf[0] = jnp.zeros((MOE_BLOCK, HALF_W), I32)

        def clear(b, carry):
            cp = y_copy(b, 0)
            cp.start()
            cp.wait()
            return carry

        lax.fori_loop(nu_ref[0], N_SLOT_BLOCKS, clear, 0)


def _moe_call(first_block, n_blocks, n_used, hs, w_gate, b_gate, w_up, b_up, w_down, b_down):
    wsel = lambda e, fb, nb, nu: (e, 0, 0)
    any_spec = pl.BlockSpec(memory_space=pl.ANY)
    return pl.pallas_call(
        _moe_kernel,
        out_shape=jax.ShapeDtypeStruct((N_SLOTS, HALF_W), I32),
        grid_spec=pltpu.PrefetchScalarGridSpec(
            num_scalar_prefetch=3,
            grid=(N_EXPERTS,),
            in_specs=[
                any_spec,
                pl.BlockSpec((1, D_MODEL, D_FF), wsel),
                pl.BlockSpec((1, 1, D_FF), wsel),
                pl.BlockSpec((1, D_MODEL, D_FF), wsel),
                pl.BlockSpec((1, 1, D_FF), wsel),
                pl.BlockSpec((1, D_FF, D_MODEL), wsel),
                pl.BlockSpec((1, 1, D_MODEL), wsel),
            ],
            out_specs=any_spec,
            scratch_shapes=[
                pltpu.VMEM((2, MOE_BLOCK, HALF_W), I32),
                pltpu.VMEM((2, MOE_BLOCK, HALF_W), I32),
                pltpu.VMEM((D_MODEL, D_FF), BF16),
                pltpu.VMEM((D_MODEL, D_FF), BF16),
                pltpu.VMEM((D_FF, D_MODEL), BF16),
                pltpu.SemaphoreType.DMA((2,)),
                pltpu.SemaphoreType.DMA((2,)),
            ],
        ),
        compiler_params=_params(("arbitrary",)),
        name="moe",
    )(first_block, n_blocks, n_used, hs, w_gate, b_gate, w_up, b_up, w_down, b_down)


SC_CORES = 2
SC_SUBCORES = 16
SC_WORKERS = SC_CORES * SC_SUBCORES
SC_ROWS = 128


def _sc_gather_rows(table, idx):
    n_idx = idx.shape[0]
    width = table.shape[1]
    per_worker = n_idx // SC_WORKERS
    n_chunks = per_worker // SC_ROWS
    assert n_chunks * SC_ROWS * SC_WORKERS == n_idx
    mesh = plsc.VectorSubcoreMesh(core_axis_name="c", subcore_axis_name="s")

    @functools.partial(
        pl.kernel, mesh=mesh,
        out_type=jax.ShapeDtypeStruct((n_idx, width), table.dtype),
        scratch_types=[pltpu.VMEM((SC_ROWS,), I32), pltpu.VMEM((SC_ROWS, width), table.dtype),
                       pltpu.SemaphoreType.DMA],
        name="sc_gather",
    )
    def gather(table_hbm, idx_hbm, out_hbm, idx_v, rows_v, sem):
        worker = lax.axis_index("s") * SC_CORES + lax.axis_index("c")
        base = worker * per_worker

        @pl.loop(0, n_chunks)
        def _(ch):
            off = pl.multiple_of(base + ch * SC_ROWS, SC_ROWS)
            pltpu.sync_copy(idx_hbm.at[pl.ds(off, SC_ROWS)], idx_v)
            pltpu.async_copy(table_hbm.at[idx_v], rows_v, sem).wait()
            pltpu.sync_copy(rows_v, out_hbm.at[pl.ds(off, SC_ROWS)])

    return gather(table, idx)


def _sc_scatter_rows(rows, idx3, n_out):
    n_rows, width = rows.shape
    n_chunks = n_rows // SC_ROWS // SC_WORKERS
    assert n_chunks * SC_ROWS * SC_WORKERS == n_rows and idx3.shape == (n_rows // SC_ROWS, TOP_K, SC_ROWS)
    mesh = plsc.VectorSubcoreMesh(core_axis_name="c", subcore_axis_name="s")

    @functools.partial(
        pl.kernel, mesh=mesh,
        out_type=jax.ShapeDtypeStruct((n_out, width), rows.dtype),
        scratch_types=[pltpu.VMEM((TOP_K, SC_ROWS), I32), pltpu.VMEM((SC_ROWS, width), rows.dtype),
                       pltpu.SemaphoreType.DMA],
        name="sc_scatter",
    )
    def scatter(rows_hbm, idx_hbm, out_hbm, idx_v, rows_v, sem):
        worker = lax.axis_index("s") * SC_CORES + lax.axis_index("c")

        @pl.loop(0, n_chunks)
        def _(ch):
            chunk = worker * n_chunks + ch
            pltpu.sync_copy(idx_hbm.at[chunk], idx_v)
            pltpu.sync_copy(rows_hbm.at[pl.ds(pl.multiple_of(chunk * SC_ROWS, SC_ROWS), SC_ROWS)],
                            rows_v)
            for kk in range(TOP_K):
                pltpu.async_copy(rows_v, out_hbm.at[idx_v.at[kk]], sem).wait()

    return scatter(rows, idx3)


def _combine_kernel(x1_ref, mod_ref, tw_ref, fg_ref, g_ref, out_ref):
    tw = tw_ref[...]
    f_lo = jnp.zeros((TOK_TILE, HALF_W), F32)
    f_hi = jnp.zeros((TOK_TILE, HALF_W), F32)
    for kk in range(TOP_K):
        lo, hi = _unpack_halves(g_ref[kk])
        f_lo = f_lo + lo * tw[:, kk:kk + 1]
        f_hi = f_hi + hi * tw[:, kk:kk + 1]
    gate2 = mod_ref[0][:, 5 * D_MODEL:6 * D_MODEL]
    x2 = x1_ref[...] + gate2 * jnp.concatenate([f_lo, f_hi], axis=-1)
    out_ref[...] = _rms(x2) * fg_ref[...]


def _combine_call(x1, mod3, tw, final_g, gathered):
    return pl.pallas_call(
        _combine_kernel,
        out_shape=jax.ShapeDtypeStruct((N_TOK, D_MODEL), F32),
        grid=(N_TILES,),
        in_specs=[
            pl.BlockSpec((TOK_TILE, D_MODEL), lambda t: (t, 0)),
            pl.BlockSpec((1, 1, N_MOD * D_MODEL), lambda t: (_mod_row(t), 0, 0)),
            pl.BlockSpec((TOK_TILE, LANES), lambda t: (t, 0)),
            pl.BlockSpec((1, D_MODEL), lambda t: (0, 0)),
            pl.BlockSpec((TOP_K, TOK_TILE, HALF_W), lambda t: (0, t, 0)),
        ],
        out_specs=pl.BlockSpec((TOK_TILE, D_MODEL), lambda t: (t, 0)),
        compiler_params=_params(("arbitrary",)),
        name="combine",
    )(x1, mod3, tw, final_g, gathered)


def kernel(x_prompt, x_sample, state_gla_fwd, state_gla_bwd, c, c_ctx, norm1_g, w_mod, b_mod, w_in,
           w_alpha, b_alpha, gla_norm_g, w_pool_grp, pool_scale, w_branch_gla, w_branch_pool, w_out,
           norm2_g, w_router, b_router, w_gate, b_gate, w_up, b_up, w_down, b_down, final_norm_g):
    l = 0
    x_ctx = x_prompt.reshape(N_CTX, D_MODEL)
    x_lat = x_sample.reshape(N_LAT, D_MODEL)

    cvec = jnp.concatenate([c_ctx[None, :], c, jnp.zeros((8 - 1 - DEC_BATCH, D_MODEL), F32)], axis=0)
    mod = _mod_call(cvec, w_mod[l], b_mod[l][None, :])
    mod3 = mod.reshape(8, 1, N_MOD * D_MODEL)

    w_in_b = w_in[l].astype(BF16)
    w_main = w_in_b[:, :MAIN_W]
    w_alr = w_in_b[:, MAIN_W:MAIN_W + ALR_W]
    w_xp = w_in_b[:, MAIN_W + ALR_W:MAIN_W + ALR_W + POOL_W]
    w_mg = w_in_b[:, MAIN_W + ALR_W + POOL_W:]
    q, k, v, og, alr, xp, mg = _inproj_call(x_ctx, x_lat, mod3, norm1_g[l][None, :],
                                            w_main, w_alr, w_xp, w_mg)

    zpad = jnp.zeros((GLA_LOWRANK, QK_W), F32)
    wa_f = jnp.concatenate([w_alpha[l, 0], zpad], axis=0)
    wa_b = jnp.concatenate([zpad, w_alpha[l, 1]], axis=0)
    o_f, o_b, s_f, s_b = _gla_call(q, k, v, alr, wa_f, b_alpha[l, 0][None, :], wa_b,
                                   b_alpha[l, 1][None, :], state_gla_fwd[:, l], state_gla_bwd[:, l])

    pooled_c = _pool_ctx_call(xp)
    pooled_l = _pool_lat_call(xp)

    w_router_pad = jnp.pad(w_router[l], ((0, 0), (0, LANES - N_EXPERTS)))
    b_router_pad = jnp.pad(b_router[l], (0, LANES - N_EXPERTS))[None, :]
    x1, h2, top_idx, top_w = _post_call(
        x_ctx, x_lat, mod3, o_f, o_b, og, pooled_c, pooled_l, mg,
        gla_norm_g[l].reshape(1, V_W), w_pool_grp[l].astype(BF16), pool_scale[l][None, :],
        w_branch_gla[l].astype(BF16), w_branch_pool[l].astype(BF16), w_out[l].astype(BF16),
        norm2_g[l][None, :], w_router_pad, b_router_pad)

    rank, cnt = _route_call(top_idx)
    counts = cnt[0, :N_EXPERTS].astype(I32)
    padded = (counts + MOE_BLOCK - 1) // MOE_BLOCK * MOE_BLOCK
    pad_end = jnp.cumsum(padded).astype(I32)
    pad_start = pad_end - padded
    n_used = (pad_end[-1:] // MOE_BLOCK).astype(I32)
    experts = jnp.arange(N_EXPERTS, dtype=I32)
    tk = top_idx[:, :TOP_K]
    pos = jnp.sum(jnp.where(tk[:, :, None] == experts, pad_start, 0), axis=-1) + rank[:, :TOP_K]
    pos = pos.astype(I32)
    pos_by_choice = pos.T
    pos_chunks = pos_by_choice.reshape(TOP_K, N_TOK // SC_ROWS, SC_ROWS).transpose(1, 0, 2)

    hs = _sc_scatter_rows(h2, pos_chunks, N_SLOTS)
    y = _moe_call(pad_start // MOE_BLOCK, padded // MOE_BLOCK, n_used, hs,
                  w_gate[l], b_gate[l][:, None, :], w_up[l], b_up[l][:, None, :],
                  w_down[l], b_down[l][:, None, :])
    gathered = _sc_gather_rows(y, pos_by_choice.reshape(TOP_K * N_TOK))
    out = _combine_call(x1, mod3, top_w, final_norm_g[None, :],
                        gathered.reshape(TOP_K, N_TOK, HALF_W))

    y_prompt = out[:N_CTX].reshape(BATCH, SEQ, D_MODEL)
    y_sample = out[N_CTX:].reshape(DEC_BATCH, DEC_SEQ, D_MODEL)
    return (y_prompt, y_sample, s_f[:, None], s_b[:, None])
```

```python
import functools

import jax
import jax.numpy as jnp
from jax import lax
from jax.experimental import pallas as pl
from jax.experimental.pallas import tpu as pltpu
from jax.experimental.pallas import tpu_sc as plsc

F32 = jnp.float32
BF16 = jnp.bfloat16
I32 = jnp.int32

D_MODEL = 1024
BATCH = 32
SEQ = 256
DEC_BATCH = 4
DEC_SEQ = 2048
GRID_W = 64
GLA_HEADS = 4
GLA_DK = 128
GLA_DV = 256
GLA_LOWRANK = 16
GLA_TAU = 16.0
GLA_CHUNK = 64
POOL_GROUPS = 4
POOL_GROUP_DIM = 128
POOL_WINDOWS = (2, 4, 8, 16)
N_EXPERTS = 32
TOP_K = 4
D_FF = 1024
SWIGLU_LIMIT = 7.0
SWIGLU_ALPHA = 1.702
MOE_BLOCK = 256
NORM_EPS = 1e-6
N_MOD = 6

QK_W = GLA_HEADS * GLA_DK
V_W = GLA_HEADS * GLA_DV
POOL_W = POOL_GROUPS * POOL_GROUP_DIM
MAIN_W = 2 * QK_W + 2 * V_W
ALR_W = 2 * GLA_LOWRANK
MG_W = 2 * D_MODEL

N_CTX = BATCH * SEQ
N_LAT = DEC_BATCH * DEC_SEQ
N_TOK = N_CTX + N_LAT
N_SLOT_BLOCKS = -(-(N_TOK * TOP_K + N_EXPERTS * (MOE_BLOCK - 1)) // MOE_BLOCK)
N_SLOTS = N_SLOT_BLOCKS * MOE_BLOCK

LANES = 128
TOK_TILE = 256
N_TILES = N_TOK // TOK_TILE
CTX_TILES = N_CTX // TOK_TILE
LAT_TILES_PER_SEQ = DEC_SEQ // TOK_TILE
ROUTE_TILE = 512
VMEM_LIMIT = 56 * 1024 * 1024

GLA_SEQS = 4
CTX_CHUNKS = SEQ // GLA_CHUNK
LAT_CHUNKS = DEC_SEQ // GLA_CHUNK
CHUNKS_PER_TILE = TOK_TILE // GLA_CHUNK
GLA_CTX_STEPS = (BATCH // GLA_SEQS) * CTX_CHUNKS
TILE_GRID = 8

NT_DIMS = (((1,), (1,)), ((), ()))
TN_DIMS = (((0,), (0,)), ((), ()))

assert DEC_BATCH == GLA_SEQS and SEQ == TOK_TILE and N_TILES == TILE_GRID * TILE_GRID


def _params(semantics, vmem=VMEM_LIMIT):
    return pltpu.CompilerParams(dimension_semantics=semantics, vmem_limit_bytes=vmem)


def _split_bf16(a):
    hi = a.astype(BF16)
    lo = (a - hi.astype(F32)).astype(BF16)
    return hi, lo


def _dot(a, b):
    return jnp.dot(a, b, preferred_element_type=F32)


def _dot3(a, b):
    a_hi, a_lo = _split_bf16(a)
    b_hi, b_lo = _split_bf16(b)
    return _dot(a_hi, b_hi) + _dot(a_lo, b_hi) + _dot(a_hi, b_lo)


def _sigmoid(x):
    return 1.0 / (1.0 + jnp.exp(-x))


HALF_W = D_MODEL // 2
HIGH_HALF_MASK = -65536


def _pack_halves(lo, hi):
    lo_bits = pltpu.bitcast(lo.astype(BF16).astype(F32), I32)
    hi_bits = pltpu.bitcast(hi.astype(BF16).astype(F32), I32)
    return lax.shift_right_logical(lo_bits, 16) | (hi_bits & HIGH_HALF_MASK)


def _unpack_halves(words):
    lo = pltpu.bitcast(lax.shift_left(words, 16), F32)
    hi = pltpu.bitcast(words & HIGH_HALF_MASK, F32)
    return lo, hi


def _rms(x):
    return x * lax.rsqrt(jnp.mean(x * x, axis=-1, keepdims=True) + NORM_EPS)


def _mod_row(t):
    return jnp.where(t < CTX_TILES, 0, 1 + (t - CTX_TILES) // LAT_TILES_PER_SEQ)


def _store_tile(t):
    u = t - CTX_TILES
    return jnp.where(t < CTX_TILES, t,
                     CTX_TILES + DEC_BATCH * (u % LAT_TILES_PER_SEQ) + u // LAT_TILES_PER_SEQ)


def _ctx_tile(t):
    return jnp.minimum(t, CTX_TILES - 1)


def _lat_tile(t):
    return jnp.maximum(t - CTX_TILES, 0)


def _mod_kernel(c_ref, w_ref, b_ref, o_ref):
    c = c_ref[...]
    o_ref[...] = _dot3(c * _sigmoid(c), w_ref[...]) + b_ref[...]


def _mod_call(cvec, w_mod, b_mod):
    rows = cvec.shape[0]
    return pl.pallas_call(
        _mod_kernel,
        out_shape=jax.ShapeDtypeStruct((rows, N_MOD * D_MODEL), F32),
        grid=(N_MOD,),
        in_specs=[
            pl.BlockSpec((rows, D_MODEL), lambda j: (0, 0)),
            pl.BlockSpec((D_MODEL, D_MODEL), lambda j: (0, j)),
            pl.BlockSpec((1, D_MODEL), lambda j: (0, j)),
        ],
        out_specs=pl.BlockSpec((rows, D_MODEL), lambda j: (0, j)),
        compiler_params=_params(("arbitrary",)),
        name="mod",
    )(cvec, w_mod, b_mod)


def _inproj_kernel(xc_ref, xl_ref, mod_ref, g_ref, wmain_ref, walr_ref, wxp_ref, wmg_ref,
                   q_ref, k_ref, v_ref, og_ref, alr_ref, xp_ref, mg_ref):
    t = pl.program_id(0)
    x = jnp.where(t < CTX_TILES, xc_ref[...], xl_ref[...])
    mod = mod_ref[0]
    shift1 = mod[:, 0:D_MODEL]
    scale1 = mod[:, D_MODEL:2 * D_MODEL]
    h = _rms(x) * g_ref[...]
    h = (h * (1.0 + scale1) + shift1).astype(BF16)
    z = _dot(h, wmain_ref[...])
    q_ref[...] = (z[:, 0:QK_W] * (GLA_DK ** -0.5)).astype(BF16)
    k_ref[...] = z[:, QK_W:2 * QK_W].astype(BF16)
    v_ref[...] = z[:, 2 * QK_W:2 * QK_W + V_W].astype(BF16)
    og_ref[...] = z[:, 2 * QK_W + V_W:MAIN_W].astype(BF16)
    alr_ref[...] = _dot(h, walr_ref[...])
    xp_ref[...] = _dot(h, wxp_ref[...])
    mg_ref[...] = _dot(h, wmg_ref[...]).astype(BF16)


def _inproj_call(x_ctx, x_lat, mod3, norm1_g, w_main, w_alr, w_xp, w_mg):
    const = lambda t: (0, 0)
    stored = lambda t: (_store_tile(t), 0)
    widths = (QK_W, QK_W, V_W, V_W, ALR_W, POOL_W, MG_W)
    dtypes = (BF16, BF16, BF16, BF16, F32, F32, BF16)
    return pl.pallas_call(
        _inproj_kernel,
        out_shape=[jax.ShapeDtypeStruct((N_TOK, w), dt) for w, dt in zip(widths, dtypes)],
        grid=(N_TILES,),
        in_specs=[
            pl.BlockSpec((TOK_TILE, D_MODEL), lambda t: (_ctx_tile(t), 0)),
            pl.BlockSpec((TOK_TILE, D_MODEL), lambda t: (_lat_tile(t), 0)),
            pl.BlockSpec((1, 1, N_MOD * D_MODEL), lambda t: (_mod_row(t), 0, 0)),
            pl.BlockSpec((1, D_MODEL), const),
            pl.BlockSpec((D_MODEL, MAIN_W), const),
            pl.BlockSpec((D_MODEL, ALR_W), const),
            pl.BlockSpec((D_MODEL, POOL_W), const),
            pl.BlockSpec((D_MODEL, MG_W), const),
        ],
        out_specs=[pl.BlockSpec((TOK_TILE, w), stored) for w in widths],
        compiler_params=_params(("arbitrary",)),
        name="inproj",
    )(x_ctx, x_lat, mod3, norm1_g, w_main, w_alr, w_xp, w_mg)


def _gla_direction(q_ref, k_ref, v_ref, alr_ref, wa_ref, ba_ref, o_ref, st_ref, slot0, rev):
    rows = GLA_SEQS * GLA_CHUNK
    alr = jnp.concatenate([alr_ref[0, s] for s in range(GLA_SEQS)], axis=0)
    a = _dot3(alr, wa_ref[...]) + ba_ref[...]
    g = (jnp.minimum(a, 0.0) - jnp.log(1.0 + jnp.exp(-jnp.abs(a)))) * (1.0 / GLA_TAU)

    row = lax.broadcasted_iota(I32, (rows, rows), 0)
    col = lax.broadcasted_iota(I32, (rows, rows), 1)
    same = (row // GLA_CHUNK) == (col // GLA_CHUNK)
    tri_all = same & ((col >= row) if rev else (col <= row))
    tri_b = jnp.where(tri_all, 1.0, 0.0).astype(BF16)
    g_hi, g_lo = _split_bf16(g)
    bcum_all = _dot(tri_b, g_hi) + _dot(tri_b, g_lo)

    r64 = lax.broadcasted_iota(I32, (GLA_CHUNK, GLA_CHUNK), 0)
    c64 = lax.broadcasted_iota(I32, (GLA_CHUNK, GLA_CHUNK), 1)
    tri = (c64 >= r64) if rev else (c64 <= r64)

    for s in range(GLA_SEQS):
        bcum = bcum_all[s * GLA_CHUNK:(s + 1) * GLA_CHUNK]
        blast = bcum[0:1] if rev else bcum[GLA_CHUNK - 1:GLA_CHUNK]
        bmid = bcum[GLA_CHUNK // 2:GLA_CHUNK // 2 + 1]
        e_q = jnp.exp(bcum - bmid)
        e_k = jnp.exp(bmid - bcum)
        e_in = jnp.exp(bcum)
        e_out = jnp.exp(blast - bcum)
        e_last = jnp.exp(blast)
        q = q_ref[0, s].astype(F32)
        k = k_ref[0, s].astype(F32)
        for h in range(GLA_HEADS):
            ks = slice(h * GLA_DK, (h + 1) * GLA_DK)
            vs = slice(h * GLA_DV, (h + 1) * GLA_DV)
            qh = q[:, ks]
            kh = k[:, ks]
            vh = v_ref[0, s, :, vs]
            att = lax.dot_general((qh * e_q[:, ks]).astype(BF16), (kh * e_k[:, ks]).astype(BF16),
                                  NT_DIMS, preferred_element_type=F32)
            att = jnp.where(tri, att, 0.0).astype(BF16)
            st = st_ref[slot0 + s, h]
            o_inter = lax.dot_general((qh * e_in[:, ks]).astype(BF16), st.astype(BF16),
                                      NT_DIMS, preferred_element_type=F32)
            o_ref[0, s, :, vs] = o_inter + _dot(att, vh)
            upd = lax.dot_general(vh, (kh * e_out[:, ks]).astype(BF16), TN_DIMS,
                                  preferred_element_type=F32)
            st_ref[slot0 + s, h] = st * e_last[:, ks] + upd


def _gla_kernel(qf_ref, kf_ref, vf_ref, af_ref, qb_ref, kb_ref, vb_ref, ab_ref,
                waf_ref, baf_ref, wab_ref, bab_ref, s0f_ref, s0b_ref,
                of_ref, ob_ref, sf_ref, sb_ref, st_ref):
    i = pl.program_id(0)
    is_ctx = i < GLA_CTX_STEPS
    chunk = jnp.where(is_ctx, i % CTX_CHUNKS, i - GLA_CTX_STEPS)

    @pl.when(is_ctx & (chunk == 0))
    def _():
        st_ref[...] = jnp.zeros(st_ref.shape, F32)

    @pl.when(i == GLA_CTX_STEPS)
    def _():
        for s in range(GLA_SEQS):
            for h in range(GLA_HEADS):
                st_ref[s, h] = s0f_ref[s, h].T
                st_ref[GLA_SEQS + s, h] = s0b_ref[s, h].T

    _gla_direction(qf_ref, kf_ref, vf_ref, af_ref, waf_ref, baf_ref, of_ref, st_ref, 0, False)
    _gla_direction(qb_ref, kb_ref, vb_ref, ab_ref, wab_ref, bab_ref, ob_ref, st_ref, GLA_SEQS, True)

    @pl.when(is_ctx & (chunk == CTX_CHUNKS - 1))
    def _():
        for s in range(GLA_SEQS):
            for h in range(GLA_HEADS):
                sf_ref[s, h] = st_ref[s, h].T
                sb_ref[s, h] = st_ref[GLA_SEQS + s, h].T


def _gla_block(i, rev):
    is_ctx = i < GLA_CTX_STEPS
    group = i // CTX_CHUNKS
    c_ctx = i % CTX_CHUNKS
    c_lat = i - GLA_CTX_STEPS
    if rev:
        c_ctx = CTX_CHUNKS - 1 - c_ctx
        c_lat = LAT_CHUNKS - 1 - c_lat
    j = c_lat // CHUNKS_PER_TILE
    per_row = TILE_GRID // GLA_SEQS
    a = jnp.where(is_ctx, group // per_row, CTX_TILES // TILE_GRID + j // per_row)
    b = jnp.where(is_ctx, group % per_row, j % per_row)
    c = jnp.where(is_ctx, c_ctx, c_lat % CHUNKS_PER_TILE)
    return (a, b, c, 0)


def _gla_call(q, k, v, alr, wa_f, ba_f, wa_b, ba_b, s0_f, s0_b):
    def view(arr):
        return arr.reshape(TILE_GRID, TILE_GRID, TOK_TILE, arr.shape[-1])

    def spec(width, rev):
        return pl.BlockSpec((1, GLA_SEQS, GLA_CHUNK, width), lambda i: _gla_block(i, rev))

    const = lambda i: (0, 0)
    st_block = (GLA_SEQS, GLA_HEADS, GLA_DK, GLA_DV)
    whole_state = pl.BlockSpec(st_block, lambda i: (0, 0, 0, 0))
    ctx_state = pl.BlockSpec(
        st_block, lambda i: (jnp.minimum(i // CTX_CHUNKS, BATCH // GLA_SEQS - 1), 0, 0, 0))
    in_specs = []
    for rev in (False, True):
        in_specs += [spec(QK_W, rev), spec(QK_W, rev), spec(V_W, rev), spec(ALR_W, rev)]
    in_specs += [pl.BlockSpec((ALR_W, QK_W), const), pl.BlockSpec((1, QK_W), const)] * 2
    in_specs += [whole_state, whole_state]
    o_shape = jax.ShapeDtypeStruct((TILE_GRID, TILE_GRID, TOK_TILE, V_W), F32)
    s_shape = jax.ShapeDtypeStruct((BATCH, GLA_HEADS, GLA_DK, GLA_DV), F32)
    qv, kv, vv, av = view(q), view(k), view(v), view(alr)
    o_f, o_b, s_f, s_b = pl.pallas_call(
        _gla_kernel,
        out_shape=[o_shape, o_shape, s_shape, s_shape],
        grid=(GLA_CTX_STEPS + LAT_CHUNKS,),
        in_specs=in_specs,
        out_specs=[spec(V_W, False), spec(V_W, True), ctx_state, ctx_state],
        scratch_shapes=[pltpu.VMEM((2 * GLA_SEQS, GLA_HEADS, GLA_DV, GLA_DK), F32)],
        compiler_params=_params(("arbitrary",)),
        name="gla",
    )(qv, kv, vv, av, qv, kv, vv, av, wa_f, ba_f, wa_b, ba_b, s0_f, s0_b)
    return o_f.reshape(N_TOK, V_W), o_b.reshape(N_TOK, V_W), s_f, s_b


def _band(n, w, block):
    row = lax.broadcasted_iota(I32, (n, n), 0)
    col = lax.broadcasted_iota(I32, (n, n), 1)
    inside = (col >= row - w // 2) & (col <= row + w // 2 - 1)
    if block < n:
        inside = inside & ((row // block) == (col // block))
    return jnp.where(inside, 1.0, 0.0).astype(BF16)


def _win_count(p, n, w):
    return jnp.minimum(p + w // 2 - 1, n - 1) - jnp.maximum(p - w // 2, 0) + 1


def _pool_ctx_kernel(x_ref, o_ref):
    p = lax.broadcasted_iota(I32, (SEQ, POOL_GROUP_DIM), 0)
    for gi, w in enumerate(POOL_WINDOWS):
        cs = slice(gi * POOL_GROUP_DIM, (gi + 1) * POOL_GROUP_DIM)
        x = x_ref[:, cs]
        hi, lo = _split_bf16(x)
        band = _band(SEQ, w, SEQ)
        s = _dot(band, hi) + _dot(band, lo)
        cnt = _win_count(p, SEQ, w).astype(F32)
        o_ref[:, cs] = s / cnt - x


def _pool_ctx_call(xp):
    spec = pl.BlockSpec((SEQ, POOL_W), lambda b: (b, 0))
    return pl.pallas_call(
        _pool_ctx_kernel,
        out_shape=jax.ShapeDtypeStruct((N_CTX, POOL_W), F32),
        grid=(BATCH,),
        in_specs=[spec],
        out_specs=spec,
        compiler_params=_params(("arbitrary",)),
        name="pool_ctx",
    )(xp)


POOL_HALO = (max(POOL_WINDOWS) // 2) * GRID_W


def _pool_lat_kernel(x_ref, o_ref, pad_ref):
    rows = DEC_SEQ // GRID_W
    p = lax.broadcasted_iota(I32, (DEC_SEQ, POOL_GROUP_DIM), 0)
    r = p // GRID_W
    cidx = p % GRID_W
    zeros = jnp.zeros((POOL_HALO, POOL_GROUP_DIM), F32)
    pad_ref[0:POOL_HALO, :] = zeros
    pad_ref[POOL_HALO + DEC_SEQ:2 * POOL_HALO + DEC_SEQ, :] = zeros
    for gi, w in enumerate(POOL_WINDOWS):
        cs = slice(gi * POOL_GROUP_DIM, (gi + 1) * POOL_GROUP_DIM)
        band = _band(TOK_TILE, w, GRID_W)
        for t in range(LAT_TILES_PER_SEQ):
            hi, lo = _split_bf16(x_ref[t, 0, :, cs])
            pad_ref[POOL_HALO + t * TOK_TILE:POOL_HALO + (t + 1) * TOK_TILE, :] = (
                _dot(band, hi) + _dot(band, lo))
        acc = jnp.zeros((DEC_SEQ, POOL_GROUP_DIM), F32)
        for dr in range(-(w // 2), w // 2):
            start = POOL_HALO + dr * GRID_W
            acc = acc + pad_ref[start:start + DEC_SEQ, :]
        cnt = (_win_count(r, rows, w) * _win_count(cidx, GRID_W, w)).astype(F32)
        pooled = acc / cnt
        for t in range(LAT_TILES_PER_SEQ):
            rs = slice(t * TOK_TILE, (t + 1) * TOK_TILE)
            o_ref[t, 0, :, cs] = pooled[rs] - x_ref[t, 0, :, cs]


def _pool_lat_call(xp):
    view = xp.reshape(N_TILES // DEC_BATCH, DEC_BATCH, TOK_TILE, POOL_W)
    blk = (LAT_TILES_PER_SEQ, 1, TOK_TILE, POOL_W)
    out = pl.pallas_call(
        _pool_lat_kernel,
        out_shape=jax.ShapeDtypeStruct((LAT_TILES_PER_SEQ, DEC_BATCH, TOK_TILE, POOL_W), F32),
        grid=(DEC_BATCH,),
        in_specs=[pl.BlockSpec(blk, lambda s: (CTX_TILES // DEC_BATCH // LAT_TILES_PER_SEQ, s, 0, 0))],
        out_specs=pl.BlockSpec(blk, lambda s: (0, s, 0, 0)),
        scratch_shapes=[pltpu.VMEM((DEC_SEQ + 2 * POOL_HALO, POOL_GROUP_DIM), F32)],
        compiler_params=_params(("arbitrary",)),
        name="pool_lat",
    )(view)
    return out.reshape(N_LAT, POOL_W)


def _post_kernel(xc_ref, xl_ref, mod_ref, of_ref, ob_ref, og_ref, pc_ref, pl_ref, mg_ref, gng_ref,
                 wpg_ref, psc_ref, wbg_ref, wbp_ref, wout_ref, n2g_ref, wr_ref, br_ref,
                 x1_ref, h2_ref, idx_ref, tw_ref):
    t = pl.program_id(0)
    is_ctx = t < CTX_TILES
    x = jnp.where(is_ctx, xc_ref[...], xl_ref[...])
    pooled = jnp.where(is_ctx, pc_ref[...], pl_ref[...])
    mod = mod_ref[0]
    gate1 = mod[:, 2 * D_MODEL:3 * D_MODEL]
    shift2 = mod[:, 3 * D_MODEL:4 * D_MODEL]
    scale2 = mod[:, 4 * D_MODEL:5 * D_MODEL]

    o = of_ref[...] + ob_ref[...]
    og = og_ref[...].astype(F32)
    gated = []
    for h in range(GLA_HEADS):
        vs = slice(h * GLA_DV, (h + 1) * GLA_DV)
        oh = _rms(o[:, vs]) * gng_ref[:, vs]
        ogh = og[:, vs]
        gated.append((oh * (ogh * _sigmoid(ogh))).astype(BF16))
    br_gla = _dot(jnp.concatenate(gated, axis=-1), wbg_ref[...])

    pm = []
    for gi in range(POOL_GROUPS):
        cs = slice(gi * POOL_GROUP_DIM, (gi + 1) * POOL_GROUP_DIM)
        pmg = _dot(pooled[:, cs].astype(BF16), wpg_ref[gi]) * psc_ref[:, cs]
        pm.append(pmg.astype(BF16))
    br_pool = _dot(jnp.concatenate(pm, axis=-1), wbp_ref[...])

    mg = mg_ref[...].astype(F32)
    merged = _sigmoid(mg[:, 0:D_MODEL]) * br_gla + _sigmoid(mg[:, D_MODEL:MG_W]) * br_pool
    m = _dot(merged.astype(BF16), wout_ref[...])
    x1 = x + gate1 * m
    x1_ref[...] = x1
    h2 = _rms(x1) * n2g_ref[...]
    h2 = h2 * (1.0 + scale2) + shift2
    h2_ref[...] = _pack_halves(h2[:, :HALF_W], h2[:, HALF_W:])

    logits = _dot3(h2, wr_ref[...]) + br_ref[...]
    lane = lax.broadcasted_iota(I32, (TOK_TILE, LANES), 1)
    lane_f = lane.astype(F32)
    neg = jnp.float32(-jnp.inf)
    cur = jnp.where(lane < N_EXPERTS, logits, neg)
    vals, idxs = [], []
    for _ in range(TOP_K):
        mx = jnp.max(cur, axis=-1, keepdims=True)
        ix = jnp.min(jnp.where(cur == mx, lane_f, float(LANES)), axis=-1, keepdims=True)
        vals.append(mx)
        idxs.append(ix)
        cur = jnp.where(lane_f == ix, neg, cur)
    ex = [jnp.exp(vv - vals[0]) for vv in vals]
    tot = ex[0] + ex[1] + ex[2] + ex[3]
    idx_out = jnp.zeros((TOK_TILE, LANES), F32)
    w_out = jnp.zeros((TOK_TILE, LANES), F32)
    for kk in range(TOP_K):
        idx_out = jnp.where(lane == kk, idxs[kk], idx_out)
        w_out = jnp.where(lane == kk, ex[kk] / tot, w_out)
    idx_ref[...] = idx_out.astype(I32)
    tw_ref[...] = w_out


def _post_call(x_ctx, x_lat, mod3, o_f, o_b, og, pooled_c, pooled_l, mg, gng, wpg, psc, wbg, wbp,
               wout, n2g, wr, br):
    row = lambda t: (t, 0)
    const = lambda t: (0, 0)
    stored = lambda t: (_store_tile(t), 0)
    return pl.pallas_call(
        _post_kernel,
        out_shape=[
            jax.ShapeDtypeStruct((N_TOK, D_MODEL), F32),
            jax.ShapeDtypeStruct((N_TOK, HALF_W), I32),
            jax.ShapeDtypeStruct((N_TOK, LANES), I32),
            jax.ShapeDtypeStruct((N_TOK, LANES), F32),
        ],
        grid=(N_TILES,),
        in_specs=[
            pl.BlockSpec((TOK_TILE, D_MODEL), lambda t: (_ctx_tile(t), 0)),
            pl.BlockSpec((TOK_TILE, D_MODEL), lambda t: (_lat_tile(t), 0)),
            pl.BlockSpec((1, 1, N_MOD * D_MODEL), lambda t: (_mod_row(t), 0, 0)),
            pl.BlockSpec((TOK_TILE, V_W), stored),
            pl.BlockSpec((TOK_TILE, V_W), stored),
            pl.BlockSpec((TOK_TILE, V_W), stored),
            pl.BlockSpec((TOK_TILE, POOL_W), lambda t: (_ctx_tile(t), 0)),
            pl.BlockSpec((TOK_TILE, POOL_W),
                         lambda t: (jnp.maximum(_store_tile(t) - CTX_TILES, 0), 0)),
            pl.BlockSpec((TOK_TILE, MG_W), stored),
            pl.BlockSpec((1, V_W), const),
            pl.BlockSpec((POOL_GROUPS, POOL_GROUP_DIM, POOL_GROUP_DIM), lambda t: (0, 0, 0)),
            pl.BlockSpec((1, POOL_W), const),
            pl.BlockSpec((V_W, D_MODEL), const),
            pl.BlockSpec((POOL_W, D_MODEL), const),
            pl.BlockSpec((D_MODEL, D_MODEL), const),
            pl.BlockSpec((1, D_MODEL), const),
            pl.BlockSpec((D_MODEL, LANES), const),
            pl.BlockSpec((1, LANES), const),
        ],
        out_specs=[
            pl.BlockSpec((TOK_TILE, D_MODEL), row),
            pl.BlockSpec((TOK_TILE, HALF_W), row),
            pl.BlockSpec((TOK_TILE, LANES), row),
            pl.BlockSpec((TOK_TILE, LANES), row),
        ],
        compiler_params=_params(("arbitrary",)),
        name="post",
    )(x_ctx, x_lat, mod3, o_f, o_b, og, pooled_c, pooled_l, mg, gng, wpg, psc, wbg, wbp, wout, n2g,
      wr, br)


def _route_kernel(idx_ref, rank_ref, cnt_ref, carry_ref):
    t = pl.program_id(0)

    @pl.when(t == 0)
    def _():
        carry_ref[...] = jnp.zeros((1, LANES), F32)

    idx = idx_ref[...]
    lane = lax.broadcasted_iota(I32, (ROUTE_TILE, LANES), 1)
    sel = [lane == idx[:, kk:kk + 1] for kk in range(TOP_K)]
    onehot = jnp.zeros((ROUTE_TILE, LANES), F32)
    for kk in range(TOP_K):
        onehot = onehot + jnp.where(sel[kk], 1.0, 0.0)
    row = lax.broadcasted_iota(I32, (ROUTE_TILE, ROUTE_TILE), 0)
    col = lax.broadcasted_iota(I32, (ROUTE_TILE, ROUTE_TILE), 1)
    strict = jnp.where(col < row, 1.0, 0.0).astype(BF16)
    before = _dot(strict, onehot.astype(BF16)) + carry_ref[...]
    rank = jnp.zeros((ROUTE_TILE, LANES), F32)
    for kk in range(TOP_K):
        rk = jnp.sum(jnp.where(sel[kk], before, 0.0), axis=-1, keepdims=True)
        rank = jnp.where(lane == kk, rk, rank)
    rank_ref[...] = rank.astype(I32)
    carry_ref[...] = carry_ref[...] + jnp.sum(onehot, axis=0, keepdims=True)
    cnt_ref[...] = jnp.broadcast_to(carry_ref[...], (8, LANES))


def _route_call(idx):
    return pl.pallas_call(
        _route_kernel,
        out_shape=[
            jax.ShapeDtypeStruct((N_TOK, LANES), I32),
            jax.ShapeDtypeStruct((8, LANES), F32),
        ],
        grid=(N_TOK // ROUTE_TILE,),
        in_specs=[pl.BlockSpec((ROUTE_TILE, LANES), lambda t: (t, 0))],
        out_specs=[
            pl.BlockSpec((ROUTE_TILE, LANES), lambda t: (t, 0)),
            pl.BlockSpec((8, LANES), lambda t: (0, 0)),
        ],
        scratch_shapes=[pltpu.VMEM((1, LANES), F32)],
        compiler_params=_params(("arbitrary",)),
        name="route",
    )(idx)


FF_TILE = 256


def _moe_kernel(fb_ref, nb_ref, nu_ref, x_hbm, wg_ref, bg_ref, wu_ref, bu_ref, wd_ref, bd_ref, y_hbm,
                xbuf, ybuf, wgb_ref, wub_ref, wdb_ref, sem_x, sem_y):
    e = pl.program_id(0)
    n_blocks = nb_ref[e]
    first = fb_ref[e]

    def rows_of(block):
        return pl.ds(pl.multiple_of(block * MOE_BLOCK, MOE_BLOCK), MOE_BLOCK)

    def x_copy(block, slot):
        return pltpu.make_async_copy(x_hbm.at[rows_of(block)], xbuf.at[slot], sem_x.at[slot])

    def y_copy(block, slot):
        return pltpu.make_async_copy(ybuf.at[slot], y_hbm.at[rows_of(block)], sem_y.at[slot])

    @pl.when(n_blocks > 0)
    def _():
        x_copy(first, 0).start()
        wgb_ref[...] = wg_ref[0].astype(BF16)
        wub_ref[...] = wu_ref[0].astype(BF16)
        wdb_ref[...] = wd_ref[0].astype(BF16)

        def body(j, carry):
            slot = j & 1
            x_copy(first + j, slot).wait()

            @pl.when(j + 1 < n_blocks)
            def _():
                x_copy(first + j + 1, 1 - slot).start()

            x_lo, x_hi = _unpack_halves(xbuf[slot])
            x = jnp.concatenate([x_lo.astype(BF16), x_hi.astype(BF16)], axis=-1)
            y = None
            for n in range(D_FF // FF_TILE):
                cs = slice(n * FF_TILE, (n + 1) * FF_TILE)
                gate = jnp.minimum(_dot(x, wgb_ref[:, cs]) + bg_ref[0][:, cs], SWIGLU_LIMIT)
                up = jnp.clip(_dot(x, wub_ref[:, cs]) + bu_ref[0][:, cs],
                              -SWIGLU_LIMIT, SWIGLU_LIMIT)
                act = (up + 1.0) * (gate * _sigmoid(SWIGLU_ALPHA * gate))
                part = _dot(act.astype(BF16), wdb_ref[cs, :])
                y = part if y is None else y + part
            y = y + bd_ref[0]

            @pl.when(j >= 2)
            def _():
                y_copy(first + j - 2, slot).wait()

            ybuf[slot] = _pack_halves(y[:, :HALF_W], y[:, HALF_W:])
            y_copy(first + j, slot).start()
            return carry

        lax.fori_loop(0, n_blocks, body, 0)

        @pl.when(n_blocks >= 2)
        def _():
            y_copy(first + n_blocks - 2, n_blocks & 1).wait()

        y_copy(first + n_blocks - 1, (n_blocks - 1) & 1).wait()

    @pl.when(e == N_EXPERTS - 1)
    def _():
        ybuf[0] = jnp.zeros((MOE_BLOCK, HALF_W), I32)

        def clear(b, carry):
            cp = y_copy(b, 0)
            cp.start()
            cp.wait()
            return carry

        lax.fori_loop(nu_ref[0], N_SLOT_BLOCKS, clear, 0)


def _moe_call(first_block, n_blocks, n_used, hs, w_gate, b_gate, w_up, b_up, w_down, b_down):
    wsel = lambda e, fb, nb, nu: (e, 0, 0)
    any_spec = pl.BlockSpec(memory_space=pl.ANY)
    return pl.pallas_call(
        _moe_kernel,
        out_shape=jax.ShapeDtypeStruct((N_SLOTS, HALF_W), I32),
        grid_spec=pltpu.PrefetchScalarGridSpec(
            num_scalar_prefetch=3,
            grid=(N_EXPERTS,),
            in_specs=[
                any_spec,
                pl.BlockSpec((1, D_MODEL, D_FF), wsel),
                pl.BlockSpec((1, 1, D_FF), wsel),
                pl.BlockSpec((1, D_MODEL, D_FF), wsel),
                pl.BlockSpec((1, 1, D_FF), wsel),
                pl.BlockSpec((1, D_FF, D_MODEL), wsel),
                pl.BlockSpec((1, 1, D_MODEL), wsel),
            ],
            out_specs=any_spec,
            scratch_shapes=[
                pltpu.VMEM((2, MOE_BLOCK, HALF_W), I32),
                pltpu.VMEM((2, MOE_BLOCK, HALF_W), I32),
                pltpu.VMEM((D_MODEL, D_FF), BF16),
                pltpu.VMEM((D_MODEL, D_FF), BF16),
                pltpu.VMEM((D_FF, D_MODEL), BF16),
                pltpu.SemaphoreType.DMA((2,)),
                pltpu.SemaphoreType.DMA((2,)),
            ],
        ),
        compiler_params=_params(("arbitrary",)),
        name="moe",
    )(first_block, n_blocks, n_used, hs, w_gate, b_gate, w_up, b_up, w_down, b_down)


SC_CORES = 2
SC_SUBCORES = 16
SC_WORKERS = SC_CORES * SC_SUBCORES
SC_ROWS = 128


def _sc_gather_rows(table, idx):
    n_idx = idx.shape[0]
    width = table.shape[1]
    per_worker = n_idx // SC_WORKERS
    n_chunks = per_worker // SC_ROWS
    assert n_chunks * SC_ROWS * SC_WORKERS == n_idx
    mesh = plsc.VectorSubcoreMesh(core_axis_name="c", subcore_axis_name="s")

    @functools.partial(
        pl.kernel, mesh=mesh,
        out_type=jax.ShapeDtypeStruct((n_idx, width), table.dtype),
        scratch_types=[pltpu.VMEM((SC_ROWS,), I32), pltpu.VMEM((SC_ROWS, width), table.dtype),
                       pltpu.SemaphoreType.DMA],
        name="sc_gather",
    )
    def gather(table_hbm, idx_hbm, out_hbm, idx_v, rows_v, sem):
        worker = lax.axis_index("s") * SC_CORES + lax.axis_index("c")
        base = worker * per_worker

        @pl.loop(0, n_chunks)
        def _(ch):
            off = pl.multiple_of(base + ch * SC_ROWS, SC_ROWS)
            pltpu.sync_copy(idx_hbm.at[pl.ds(off, SC_ROWS)], idx_v)
            pltpu.async_copy(table_hbm.at[idx_v], rows_v, sem).wait()
            pltpu.sync_copy(rows_v, out_hbm.at[pl.ds(off, SC_ROWS)])

    return gather(table, idx)


def _sc_scatter_rows(rows, idx3, n_out):
    n_rows, width = rows.shape
    n_chunks = n_rows // SC_ROWS // SC_WORKERS
    assert n_chunks * SC_ROWS * SC_WORKERS == n_rows and idx3.shape == (n_rows // SC_ROWS, TOP_K, SC_ROWS)
    mesh = plsc.VectorSubcoreMesh(core_axis_name="c", subcore_axis_name="s")

    @functools.partial(
        pl.kernel, mesh=mesh,
        out_type=jax.ShapeDtypeStruct((n_out, width), rows.dtype),
        scratch_types=[pltpu.VMEM((TOP_K, SC_ROWS), I32), pltpu.VMEM((SC_ROWS, width), rows.dtype),
                       pltpu.SemaphoreType.DMA],
        name="sc_scatter",
    )
    def scatter(rows_hbm, idx_hbm, out_hbm, idx_v, rows_v, sem):
        worker = lax.axis_index("s") * SC_CORES + lax.axis_index("c")

        @pl.loop(0, n_chunks)
        def _(ch):
            chunk = worker * n_chunks + ch
            pltpu.sync_copy(idx_hbm.at[chunk], idx_v)
            pltpu.sync_copy(rows_hbm.at[pl.ds(pl.multiple_of(chunk * SC_ROWS, SC_ROWS), SC_ROWS)],
                            rows_v)
            for kk in range(TOP_K):
                pltpu.async_copy(rows_v, out_hbm.at[idx_v.at[kk]], sem).wait()

    return scatter(rows, idx3)


def _combine_kernel(x1_ref, mod_ref, tw_ref, fg_ref, g_ref, out_ref):
    tw = tw_ref[...]
    f_lo = jnp.zeros((TOK_TILE, HALF_W), F32)
    f_hi = jnp.zeros((TOK_TILE, HALF_W), F32)
    for kk in range(TOP_K):
        lo, hi = _unpack_halves(g_ref[kk])
        f_lo = f_lo + lo * tw[:, kk:kk + 1]
        f_hi = f_hi + hi * tw[:, kk:kk + 1]
    gate2 = mod_ref[0][:, 5 * D_MODEL:6 * D_MODEL]
    x2 = x1_ref[...] + gate2 * jnp.concatenate([f_lo, f_hi], axis=-1)
    out_ref[...] = _rms(x2) * fg_ref[...]


def _combine_call(x1, mod3, tw, final_g, gathered):
    return pl.pallas_call(
        _combine_kernel,
        out_shape=jax.ShapeDtypeStruct((N_TOK, D_MODEL), F32),
        grid=(N_TILES,),
        in_specs=[
            pl.BlockSpec((TOK_TILE, D_MODEL), lambda t: (t, 0)),
            pl.BlockSpec((1, 1, N_MOD * D_MODEL), lambda t: (_mod_row(t), 0, 0)),
            pl.BlockSpec((TOK_TILE, LANES), lambda t: (t, 0)),
            pl.BlockSpec((1, D_MODEL), lambda t: (0, 0)),
            pl.BlockSpec((TOP_K, TOK_TILE, HALF_W), lambda t: (0, t, 0)),
        ],
        out_specs=pl.BlockSpec((TOK_TILE, D_MODEL), lambda t: (t, 0)),
        compiler_params=_params(("arbitrary",)),
        name="combine",
    )(x1, mod3, tw, final_g, gathered)


def kernel(x_prompt, x_sample, state_gla_fwd, state_gla_bwd, c, c_ctx, norm1_g, w_mod, b_mod, w_in,
           w_alpha, b_alpha, gla_norm_g, w_pool_grp, pool_scale, w_branch_gla, w_branch_pool, w_out,
           norm2_g, w_router, b_router, w_gate, b_gate, w_up, b_up, w_down, b_down, final_norm_g):
    l = 0
    x_ctx = x_prompt.reshape(N_CTX, D_MODEL)
    x_lat = x_sample.reshape(N_LAT, D_MODEL)

    cvec = jnp.concatenate([c_ctx[None, :], c, jnp.zeros((8 - 1 - DEC_BATCH, D_MODEL), F32)], axis=0)
    mod = _mod_call(cvec, w_mod[l], b_mod[l][None, :])
    mod3 = mod.reshape(8, 1, N_MOD * D_MODEL)

    w_in_b = w_in[l].astype(BF16)
    w_main = w_in_b[:, :MAIN_W]
    w_alr = w_in_b[:, MAIN_W:MAIN_W + ALR_W]
    w_xp = w_in_b[:, MAIN_W + ALR_W:MAIN_W + ALR_W + POOL_W]
    w_mg = w_in_b[:, MAIN_W + ALR_W + POOL_W:]
    q, k, v, og, alr, xp, mg = _inproj_call(x_ctx, x_lat, mod3, norm1_g[l][None, :],
                                            w_main, w_alr, w_xp, w_mg)

    zpad = jnp.zeros((GLA_LOWRANK, QK_W), F32)
    wa_f = jnp.concatenate([w_alpha[l, 0], zpad], axis=0)
    wa_b = jnp.concatenate([zpad, w_alpha[l, 1]], axis=0)
    o_f, o_b, s_f, s_b = _gla_call(q, k, v, alr, wa_f, b_alpha[l, 0][None, :], wa_b,
                                   b_alpha[l, 1][None, :], state_gla_fwd[:, l], state_gla_bwd[:, l])

    pooled_c = _pool_ctx_call(xp)
    pooled_l = _pool_lat_call(xp)

    w_router_pad = jnp.pad(w_router[l], ((0, 0), (0, LANES - N_EXPERTS)))
    b_router_pad = jnp.pad(b_router[l], (0, LANES - N_EXPERTS))[None, :]
    x1, h2, top_idx, top_w = _post_call(
        x_ctx, x_lat, mod3, o_f, o_b, og, pooled_c, pooled_l, mg,
        gla_norm_g[l].reshape(1, V_W), w_pool_grp[l].astype(BF16), pool_scale[l][None, :],
        w_branch_gla[l].astype(BF16), w_branch_pool[l].astype(BF16), w_out[l].astype(BF16),
        norm2_g[l][None, :], w_router_pad, b_router_pad)

    rank, cnt = _route_call(top_idx)
    counts = cnt[0, :N_EXPERTS].astype(I32)
    padded = (counts + MOE_BLOCK - 1) // MOE_BLOCK * MOE_BLOCK
    pad_end = jnp.cumsum(padded).astype(I32)
    pad_start = pad_end - padded
    n_used = (pad_end[-1:] // MOE_BLOCK).astype(I32)
    experts = jnp.arange(N_EXPERTS, dtype=I32)
    tk = top_idx[:, :TOP_K]
    pos = jnp.sum(jnp.where(tk[:, :, None] == experts, pad_start, 0), axis=-1) + rank[:, :TOP_K]
    pos = pos.astype(I32)
    pos_by_choice = pos.T
    pos_chunks = pos_by_choice.reshape(TOP_K, N_TOK // SC_ROWS, SC_ROWS).transpose(1, 0, 2)

    hs = _sc_scatter_rows(h2, pos_chunks, N_SLOTS)
    y = _moe_call(pad_start // MOE_BLOCK, padded // MOE_BLOCK, n_used, hs,
                  w_gate[l], b_gate[l][:, None, :], w_up[l], b_up[l][:, None, :],
                  w_down[l], b_down[l][:, None, :])
    gathered = _sc_gather_rows(y, pos_by_choice.reshape(TOP_K * N_TOK))
    out = _combine_call(x1, mod3, top_w, final_norm_g[None, :],
                        gathered.reshape(TOP_K, N_TOK, HALF_W))

    y_prompt = out[:N_CTX].reshape(BATCH, SEQ, D_MODEL)
    y_sample = out[N_CTX:].reshape(DEC_BATCH, DEC_SEQ, D_MODEL)
    return (y_prompt, y_sample, s_f[:, None], s_b[:, None])
```

```python
import functools

import jax
import jax.numpy as jnp
from jax import lax
from jax.experimental import pallas as pl
from jax.experimental.pallas import tpu as pltpu
from jax.experimental.pallas import tpu_sc as plsc

F32 = jnp.float32
BF16 = jnp.bfloat16
I32 = jnp.int32

D_MODEL = 1024
BATCH = 32
SEQ = 256
DEC_BATCH = 4
DEC_SEQ = 2048
GRID_W = 64
GLA_HEADS = 4
GLA_DK = 128
GLA_DV = 256
GLA_LOWRANK = 16
GLA_TAU = 16.0
GLA_CHUNK = 64
POOL_GROUPS = 4
POOL_GROUP_DIM = 128
POOL_WINDOWS = (2, 4, 8, 16)
N_EXPERTS = 32
TOP_K = 4
D_FF = 1024
SWIGLU_LIMIT = 7.0
SWIGLU_ALPHA = 1.702
MOE_BLOCK = 512
NORM_EPS = 1e-6
N_MOD = 6

QK_W = GLA_HEADS * GLA_DK
V_W = GLA_HEADS * GLA_DV
POOL_W = POOL_GROUPS * POOL_GROUP_DIM
MAIN_W = 2 * QK_W + 2 * V_W
ALR_W = 2 * GLA_LOWRANK
MG_W = 2 * D_MODEL

N_CTX = BATCH * SEQ
N_LAT = DEC_BATCH * DEC_SEQ
N_TOK = N_CTX + N_LAT
N_SLOT_BLOCKS = -(-(N_TOK * TOP_K + N_EXPERTS * (MOE_BLOCK - 1)) // MOE_BLOCK)
N_SLOTS = N_SLOT_BLOCKS * MOE_BLOCK

LANES = 128
TOK_TILE = 256
N_TILES = N_TOK // TOK_TILE
CTX_TILES = N_CTX // TOK_TILE
LAT_TILES_PER_SEQ = DEC_SEQ // TOK_TILE
ROUTE_TILE = 512
VMEM_LIMIT = 56 * 1024 * 1024

GLA_SEQS = 4
CTX_CHUNKS = SEQ // GLA_CHUNK
LAT_CHUNKS = DEC_SEQ // GLA_CHUNK
CHUNKS_PER_TILE = TOK_TILE // GLA_CHUNK
GLA_CTX_STEPS = (BATCH // GLA_SEQS) * CTX_CHUNKS
TILE_GRID = 8

NT_DIMS = (((1,), (1,)), ((), ()))
TN_DIMS = (((0,), (0,)), ((), ()))

assert DEC_BATCH == GLA_SEQS and SEQ == TOK_TILE and N_TILES == TILE_GRID * TILE_GRID


def _params(semantics, vmem=VMEM_LIMIT):
    return pltpu.CompilerParams(dimension_semantics=semantics, vmem_limit_bytes=vmem)


def _split_bf16(a):
    hi = a.astype(BF16)
    lo = (a - hi.astype(F32)).astype(BF16)
    return hi, lo


def _dot(a, b):
    return jnp.dot(a, b, preferred_element_type=F32)


def _dot3(a, b):
    a_hi, a_lo = _split_bf16(a)
    b_hi, b_lo = _split_bf16(b)
    return _dot(a_hi, b_hi) + _dot(a_lo, b_hi) + _dot(a_hi, b_lo)


def _sigmoid(x):
    return 1.0 / (1.0 + jnp.exp(-x))


HALF_W = D_MODEL // 2
HIGH_HALF_MASK = -65536


def _pack_halves(lo, hi):
    lo_bits = pltpu.bitcast(lo.astype(BF16).astype(F32), I32)
    hi_bits = pltpu.bitcast(hi.astype(BF16).astype(F32), I32)
    return lax.shift_right_logical(lo_bits, 16) | (hi_bits & HIGH_HALF_MASK)


def _unpack_halves(words):
    lo = pltpu.bitcast(lax.shift_left(words, 16), F32)
    hi = pltpu.bitcast(words & HIGH_HALF_MASK, F32)
    return lo, hi


def _rms(x):
    return x * lax.rsqrt(jnp.mean(x * x, axis=-1, keepdims=True) + NORM_EPS)


def _mod_row(t):
    return jnp.where(t < CTX_TILES, 0, 1 + (t - CTX_TILES) // LAT_TILES_PER_SEQ)


def _store_tile(t):
    u = t - CTX_TILES
    return jnp.where(t < CTX_TILES, t,
                     CTX_TILES + DEC_BATCH * (u % LAT_TILES_PER_SEQ) + u // LAT_TILES_PER_SEQ)


def _ctx_tile(t):
    return jnp.minimum(t, CTX_TILES - 1)


def _lat_tile(t):
    return jnp.maximum(t - CTX_TILES, 0)


def _mod_kernel(c_ref, w_ref, b_ref, o_ref):
    c = c_ref[...]
    o_ref[...] = _dot3(c * _sigmoid(c), w_ref[...]) + b_ref[...]


def _mod_call(cvec, w_mod, b_mod):
    rows = cvec.shape[0]
    return pl.pallas_call(
        _mod_kernel,
        out_shape=jax.ShapeDtypeStruct((rows, N_MOD * D_MODEL), F32),
        grid=(N_MOD,),
        in_specs=[
            pl.BlockSpec((rows, D_MODEL), lambda j: (0, 0)),
            pl.BlockSpec((D_MODEL, D_MODEL), lambda j: (0, j)),
            pl.BlockSpec((1, D_MODEL), lambda j: (0, j)),
        ],
        out_specs=pl.BlockSpec((rows, D_MODEL), lambda j: (0, j)),
        compiler_params=_params(("arbitrary",)),
        name="mod",
    )(cvec, w_mod, b_mod)


def _inproj_kernel(xc_ref, xl_ref, mod_ref, g_ref, wmain_ref, walr_ref, wxp_ref, wmg_ref,
                   q_ref, k_ref, v_ref, og_ref, alr_ref, xp_ref, mg_ref):
    t = pl.program_id(0)
    x = jnp.where(t < CTX_TILES, xc_ref[...], xl_ref[...])
    mod = mod_ref[0]
    shift1 = mod[:, 0:D_MODEL]
    scale1 = mod[:, D_MODEL:2 * D_MODEL]
    h = _rms(x) * g_ref[...]
    h = (h * (1.0 + scale1) + shift1).astype(BF16)
    z = _dot(h, wmain_ref[...])
    q_ref[...] = (z[:, 0:QK_W] * (GLA_DK ** -0.5)).astype(BF16)
    k_ref[...] = z[:, QK_W:2 * QK_W].astype(BF16)
    v_ref[...] = z[:, 2 * QK_W:2 * QK_W + V_W].astype(BF16)
    og_ref[...] = z[:, 2 * QK_W + V_W:MAIN_W].astype(BF16)
    alr_ref[...] = _dot(h, walr_ref[...])
    xp_ref[...] = _dot(h, wxp_ref[...])
    mg_ref[...] = _dot(h, wmg_ref[...]).astype(BF16)


def _inproj_call(x_ctx, x_lat, mod3, norm1_g, w_main, w_alr, w_xp, w_mg):
    const = lambda t: (0, 0)
    stored = lambda t: (_store_tile(t), 0)
    widths = (QK_W, QK_W, V_W, V_W, ALR_W, POOL_W, MG_W)
    dtypes = (BF16, BF16, BF16, BF16, F32, F32, BF16)
    return pl.pallas_call(
        _inproj_kernel,
        out_shape=[jax.ShapeDtypeStruct((N_TOK, w), dt) for w, dt in zip(widths, dtypes)],
        grid=(N_TILES,),
        in_specs=[
            pl.BlockSpec((TOK_TILE, D_MODEL), lambda t: (_ctx_tile(t), 0)),
            pl.BlockSpec((TOK_TILE, D_MODEL), lambda t: (_lat_tile(t), 0)),
            pl.BlockSpec((1, 1, N_MOD * D_MODEL), lambda t: (_mod_row(t), 0, 0)),
            pl.BlockSpec((1, D_MODEL), const),
            pl.BlockSpec((D_MODEL, MAIN_W), const),
            pl.BlockSpec((D_MODEL, ALR_W), const),
            pl.BlockSpec((D_MODEL, POOL_W), const),
            pl.BlockSpec((D_MODEL, MG_W), const),
        ],
        out_specs=[pl.BlockSpec((TOK_TILE, w), stored) for w in widths],
        compiler_params=_params(("arbitrary",)),
        name="inproj",
    )(x_ctx, x_lat, mod3, norm1_g, w_main, w_alr, w_xp, w_mg)


def _gla_direction(q_ref, k_ref, v_ref, alr_ref, wa_ref, ba_ref, o_ref, st_ref, slot0, rev):
    rows = GLA_SEQS * GLA_CHUNK
    alr = jnp.concatenate([alr_ref[0, s] for s in range(GLA_SEQS)], axis=0)
    a = _dot3(alr, wa_ref[...]) + ba_ref[...]
    g = (jnp.minimum(a, 0.0) - jnp.log(1.0 + jnp.exp(-jnp.abs(a)))) * (1.0 / GLA_TAU)

    row = lax.broadcasted_iota(I32, (rows, rows), 0)
    col = lax.broadcasted_iota(I32, (rows, rows), 1)
    same = (row // GLA_CHUNK) == (col // GLA_CHUNK)
    tri_all = same & ((col >= row) if rev else (col <= row))
    tri_b = jnp.where(tri_all, 1.0, 0.0).astype(BF16)
    g_hi, g_lo = _split_bf16(g)
    bcum_all = _dot(tri_b, g_hi) + _dot(tri_b, g_lo)

    r64 = lax.broadcasted_iota(I32, (GLA_CHUNK, GLA_CHUNK), 0)
    c64 = lax.broadcasted_iota(I32, (GLA_CHUNK, GLA_CHUNK), 1)
    tri = (c64 >= r64) if rev else (c64 <= r64)

    for s in range(GLA_SEQS):
        bcum = bcum_all[s * GLA_CHUNK:(s + 1) * GLA_CHUNK]
        blast = bcum[0:1] if rev else bcum[GLA_CHUNK - 1:GLA_CHUNK]
        bmid = bcum[GLA_CHUNK // 2:GLA_CHUNK // 2 + 1]
        e_q = jnp.exp(bcum - bmid)
        e_k = jnp.exp(bmid - bcum)
        e_in = jnp.exp(bcum)
        e_out = jnp.exp(blast - bcum)
        e_last = jnp.exp(blast)
        q = q_ref[0, s].astype(F32)
        k = k_ref[0, s].astype(F32)
        for h in range(GLA_HEADS):
            ks = slice(h * GLA_DK, (h + 1) * GLA_DK)
            vs = slice(h * GLA_DV, (h + 1) * GLA_DV)
            qh = q[:, ks]
            kh = k[:, ks]
            vh = v_ref[0, s, :, vs]
            att = lax.dot_general((qh * e_q[:, ks]).astype(BF16), (kh * e_k[:, ks]).astype(BF16),
                                  NT_DIMS, preferred_element_type=F32)
            att = jnp.where(tri, att, 0.0).astype(BF16)
            st = st_ref[slot0 + s, h]
            o_inter = lax.dot_general((qh * e_in[:, ks]).astype(BF16), st.astype(BF16),
                                      NT_DIMS, preferred_element_type=F32)
            o_ref[0, s, :, vs] = o_inter + _dot(att, vh)
            upd = lax.dot_general(vh, (kh * e_out[:, ks]).astype(BF16), TN_DIMS,
                                  preferred_element_type=F32)
            st_ref[slot0 + s, h] = st * e_last[:, ks] + upd


def _gla_kernel(qf_ref, kf_ref, vf_ref, af_ref, qb_ref, kb_ref, vb_ref, ab_ref,
                waf_ref, baf_ref, wab_ref, bab_ref, s0f_ref, s0b_ref,
                of_ref, ob_ref, sf_ref, sb_ref, st_ref):
    i = pl.program_id(0)
    is_ctx = i < GLA_CTX_STEPS
    chunk = jnp.where(is_ctx, i % CTX_CHUNKS, i - GLA_CTX_STEPS)

    @pl.when(is_ctx & (chunk == 0))
    def _():
        st_ref[...] = jnp.zeros(st_ref.shape, F32)

    @pl.when(i == GLA_CTX_STEPS)
    def _():
        for s in range(GLA_SEQS):
            for h in range(GLA_HEADS):
                st_ref[s, h] = s0f_ref[s, h].T
                st_ref[GLA_SEQS + s, h] = s0b_ref[s, h].T

    _gla_direction(qf_ref, kf_ref, vf_ref, af_ref, waf_ref, baf_ref, of_ref, st_ref, 0, False)
    _gla_direction(qb_ref, kb_ref, vb_ref, ab_ref, wab_ref, bab_ref, ob_ref, st_ref, GLA_SEQS, True)

    @pl.when(is_ctx & (chunk == CTX_CHUNKS - 1))
    def _():
        for s in range(GLA_SEQS):
            for h in range(GLA_HEADS):
                sf_ref[s, h] = st_ref[s, h].T
                sb_ref[s, h] = st_ref[GLA_SEQS + s, h].T


def _gla_block(i, rev):
    is_ctx = i < GLA_CTX_STEPS
    group = i // CTX_CHUNKS
    c_ctx = i % CTX_CHUNKS
    c_lat = i - GLA_CTX_STEPS
    if rev:
        c_ctx = CTX_CHUNKS - 1 - c_ctx
        c_lat = LAT_CHUNKS - 1 - c_lat
    j = c_lat // CHUNKS_PER_TILE
    per_row = TILE_GRID // GLA_SEQS
    a = jnp.where(is_ctx, group // per_row, CTX_TILES // TILE_GRID + j // per_row)
    b = jnp.where(is_ctx, group % per_row, j % per_row)
    c = jnp.where(is_ctx, c_ctx, c_lat % CHUNKS_PER_TILE)
    return (a, b, c, 0)


def _gla_call(q, k, v, alr, wa_f, ba_f, wa_b, ba_b, s0_f, s0_b):
    def view(arr):
        return arr.reshape(TILE_GRID, TILE_GRID, TOK_TILE, arr.shape[-1])

    def spec(width, rev):
        return pl.BlockSpec((1, GLA_SEQS, GLA_CHUNK, width), lambda i: _gla_block(i, rev))

    const = lambda i: (0, 0)
    st_block = (GLA_SEQS, GLA_HEADS, GLA_DK, GLA_DV)
    whole_state = pl.BlockSpec(st_block, lambda i: (0, 0, 0, 0))
    ctx_state = pl.BlockSpec(
        st_block, lambda i: (jnp.minimum(i // CTX_CHUNKS, BATCH // GLA_SEQS - 1), 0, 0, 0))
    in_specs = []
    for rev in (False, True):
        in_specs += [spec(QK_W, rev), spec(QK_W, rev), spec(V_W, rev), spec(ALR_W, rev)]
    in_specs += [pl.BlockSpec((ALR_W, QK_W), const), pl.BlockSpec((1, QK_W), const)] * 2
    in_specs += [whole_state, whole_state]
    o_shape = jax.ShapeDtypeStruct((TILE_GRID, TILE_GRID, TOK_TILE, V_W), F32)
    s_shape = jax.ShapeDtypeStruct((BATCH, GLA_HEADS, GLA_DK, GLA_DV), F32)
    qv, kv, vv, av = view(q), view(k), view(v), view(alr)
    o_f, o_b, s_f, s_b = pl.pallas_call(
        _gla_kernel,
        out_shape=[o_shape, o_shape, s_shape, s_shape],
        grid=(GLA_CTX_STEPS + LAT_CHUNKS,),
        in_specs=in_specs,
        out_specs=[spec(V_W, False), spec(V_W, True), ctx_state, ctx_state],
        scratch_shapes=[pltpu.VMEM((2 * GLA_SEQS, GLA_HEADS, GLA_DV, GLA_DK), F32)],
        compiler_params=_params(("arbitrary",)),
        name="gla",
    )(qv, kv, vv, av, qv, kv, vv, av, wa_f, ba_f, wa_b, ba_b, s0_f, s0_b)
    return o_f.reshape(N_TOK, V_W), o_b.reshape(N_TOK, V_W), s_f, s_b


def _band(n, w, block):
    row = lax.broadcasted_iota(I32, (n, n), 0)
    col = lax.broadcasted_iota(I32, (n, n), 1)
    inside = (col >= row - w // 2) & (col <= row + w // 2 - 1)
    if block < n:
        inside = inside & ((row // block) == (col // block))
    return jnp.where(inside, 1.0, 0.0).astype(BF16)


def _win_count(p, n, w):
    return jnp.minimum(p + w // 2 - 1, n - 1) - jnp.maximum(p - w // 2, 0) + 1


def _pool_ctx_kernel(x_ref, o_ref):
    p = lax.broadcasted_iota(I32, (SEQ, POOL_GROUP_DIM), 0)
    for gi, w in enumerate(POOL_WINDOWS):
        cs = slice(gi * POOL_GROUP_DIM, (gi + 1) * POOL_GROUP_DIM)
        x = x_ref[:, cs]
        hi, lo = _split_bf16(x)
        band = _band(SEQ, w, SEQ)
        s = _dot(band, hi) + _dot(band, lo)
        cnt = _win_count(p, SEQ, w).astype(F32)
        o_ref[:, cs] = s / cnt - x


def _pool_ctx_call(xp):
    spec = pl.BlockSpec((SEQ, POOL_W), lambda b: (b, 0))
    return pl.pallas_call(
        _pool_ctx_kernel,
        out_shape=jax.ShapeDtypeStruct((N_CTX, POOL_W), F32),
        grid=(BATCH,),
        in_specs=[spec],
        out_specs=spec,
        compiler_params=_params(("arbitrary",)),
        name="pool_ctx",
    )(xp)


POOL_HALO = (max(POOL_WINDOWS) // 2) * GRID_W


def _pool_lat_kernel(x_ref, o_ref, pad_ref):
    rows = DEC_SEQ // GRID_W
    p = lax.broadcasted_iota(I32, (DEC_SEQ, POOL_GROUP_DIM), 0)
    r = p // GRID_W
    cidx = p % GRID_W
    zeros = jnp.zeros((POOL_HALO, POOL_GROUP_DIM), F32)
    pad_ref[0:POOL_HALO, :] = zeros
    pad_ref[POOL_HALO + DEC_SEQ:2 * POOL_HALO + DEC_SEQ, :] = zeros
    for gi, w in enumerate(POOL_WINDOWS):
        cs = slice(gi * POOL_GROUP_DIM, (gi + 1) * POOL_GROUP_DIM)
        band = _band(TOK_TILE, w, GRID_W)
        for t in range(LAT_TILES_PER_SEQ):
            hi, lo = _split_bf16(x_ref[t, 0, :, cs])
            pad_ref[POOL_HALO + t * TOK_TILE:POOL_HALO + (t + 1) * TOK_TILE, :] = (
                _dot(band, hi) + _dot(band, lo))
        acc = jnp.zeros((DEC_SEQ, POOL_GROUP_DIM), F32)
        for dr in range(-(w // 2), w // 2):
            start = POOL_HALO + dr * GRID_W
            acc = acc + pad_ref[start:start + DEC_SEQ, :]
        cnt = (_win_count(r, rows, w) * _win_count(cidx, GRID_W, w)).astype(F32)
        pooled = acc / cnt
        for t in range(LAT_TILES_PER_SEQ):
            rs = slice(t * TOK_TILE, (t + 1) * TOK_TILE)
            o_ref[t, 0, :, cs] = pooled[rs] - x_ref[t, 0, :, cs]


def _pool_lat_call(xp):
    view = xp.reshape(N_TILES // DEC_BATCH, DEC_BATCH, TOK_TILE, POOL_W)
    blk = (LAT_TILES_PER_SEQ, 1, TOK_TILE, POOL_W)
    out = pl.pallas_call(
        _pool_lat_kernel,
        out_shape=jax.ShapeDtypeStruct((LAT_TILES_PER_SEQ, DEC_BATCH, TOK_TILE, POOL_W), F32),
        grid=(DEC_BATCH,),
        in_specs=[pl.BlockSpec(blk, lambda s: (CTX_TILES // DEC_BATCH // LAT_TILES_PER_SEQ, s, 0, 0))],
        out_specs=pl.BlockSpec(blk, lambda s: (0, s, 0, 0)),
        scratch_shapes=[pltpu.VMEM((DEC_SEQ + 2 * POOL_HALO, POOL_GROUP_DIM), F32)],
        compiler_params=_params(("arbitrary",)),
        name="pool_lat",
    )(view)
    return out.reshape(N_LAT, POOL_W)


def _post_kernel(xc_ref, xl_ref, mod_ref, of_ref, ob_ref, og_ref, pc_ref, pl_ref, mg_ref, gng_ref,
                 wpg_ref, psc_ref, wbg_ref, wbp_ref, wout_ref, n2g_ref, wr_ref, br_ref,
                 x1_ref, h2_ref, idx_ref, tw_ref):
    t = pl.program_id(0)
    is_ctx = t < CTX_TILES
    x = jnp.where(is_ctx, xc_ref[...], xl_ref[...])
    pooled = jnp.where(is_ctx, pc_ref[...], pl_ref[...])
    mod = mod_ref[0]
    gate1 = mod[:, 2 * D_MODEL:3 * D_MODEL]
    shift2 = mod[:, 3 * D_MODEL:4 * D_MODEL]
    scale2 = mod[:, 4 * D_MODEL:5 * D_MODEL]

    o = of_ref[...] + ob_ref[...]
    og = og_ref[...].astype(F32)
    gated = []
    for h in range(GLA_HEADS):
        vs = slice(h * GLA_DV, (h + 1) * GLA_DV)
        oh = _rms(o[:, vs]) * gng_ref[:, vs]
        ogh = og[:, vs]
        gated.append((oh * (ogh * _sigmoid(ogh))).astype(BF16))
    br_gla = _dot(jnp.concatenate(gated, axis=-1), wbg_ref[...])

    pm = []
    for gi in range(POOL_GROUPS):
        cs = slice(gi * POOL_GROUP_DIM, (gi + 1) * POOL_GROUP_DIM)
        pmg = _dot(pooled[:, cs].astype(BF16), wpg_ref[gi]) * psc_ref[:, cs]
        pm.append(pmg.astype(BF16))
    br_pool = _dot(jnp.concatenate(pm, axis=-1), wbp_ref[...])

    mg = mg_ref[...].astype(F32)
    merged = _sigmoid(mg[:, 0:D_MODEL]) * br_gla + _sigmoid(mg[:, D_MODEL:MG_W]) * br_pool
    m = _dot(merged.astype(BF16), wout_ref[...])
    x1 = x + gate1 * m
    x1_ref[...] = x1
    h2 = _rms(x1) * n2g_ref[...]
    h2 = h2 * (1.0 + scale2) + shift2
    h2_ref[...] = _pack_halves(h2[:, :HALF_W], h2[:, HALF_W:])

    logits = _dot3(h2, wr_ref[...]) + br_ref[...]
    lane = lax.broadcasted_iota(I32, (TOK_TILE, LANES), 1)
    lane_f = lane.astype(F32)
    neg = jnp.float32(-jnp.inf)
    cur = jnp.where(lane < N_EXPERTS, logits, neg)
    vals, idxs = [], []
    for _ in range(TOP_K):
        mx = jnp.max(cur, axis=-1, keepdims=True)
        ix = jnp.min(jnp.where(cur == mx, lane_f, float(LANES)), axis=-1, keepdims=True)
        vals.append(mx)
        idxs.append(ix)
        cur = jnp.where(lane_f == ix, neg, cur)
    ex = [jnp.exp(vv - vals[0]) for vv in vals]
    tot = ex[0] + ex[1] + ex[2] + ex[3]
    idx_out = jnp.zeros((TOK_TILE, LANES), F32)
    w_out = jnp.zeros((TOK_TILE, LANES), F32)
    for kk in range(TOP_K):
        idx_out = jnp.where(lane == kk, idxs[kk], idx_out)
        w_out = jnp.where(lane == kk, ex[kk] / tot, w_out)
    idx_ref[...] = idx_out.astype(I32)
    tw_ref[...] = w_out


def _post_call(x_ctx, x_lat, mod3, o_f, o_b, og, pooled_c, pooled_l, mg, gng, wpg, psc, wbg, wbp,
               wout, n2g, wr, br):
    row = lambda t: (t, 0)
    const = lambda t: (0, 0)
    stored = lambda t: (_store_tile(t), 0)
    return pl.pallas_call(
        _post_kernel,
        out_shape=[
            jax.ShapeDtypeStruct((N_TOK, D_MODEL), F32),
            jax.ShapeDtypeStruct((N_TOK, HALF_W), I32),
            jax.ShapeDtypeStruct((N_TOK, LANES), I32),
            jax.ShapeDtypeStruct((N_TOK, LANES), F32),
        ],
        grid=(N_TILES,),
        in_specs=[
            pl.BlockSpec((TOK_TILE, D_MODEL), lambda t: (_ctx_tile(t), 0)),
            pl.BlockSpec((TOK_TILE, D_MODEL), lambda t: (_lat_tile(t), 0)),
            pl.BlockSpec((1, 1, N_MOD * D_MODEL), lambda t: (_mod_row(t), 0, 0)),
            pl.BlockSpec((TOK_TILE, V_W), stored),
            pl.BlockSpec((TOK_TILE, V_W), stored),
            pl.BlockSpec((TOK_TILE, V_W), stored),
            pl.BlockSpec((TOK_TILE, POOL_W), lambda t: (_ctx_tile(t), 0)),
            pl.BlockSpec((TOK_TILE, POOL_W),
                         lambda t: (jnp.maximum(_store_tile(t) - CTX_TILES, 0), 0)),
            pl.BlockSpec((TOK_TILE, MG_W), stored),
            pl.BlockSpec((1, V_W), const),
            pl.BlockSpec((POOL_GROUPS, POOL_GROUP_DIM, POOL_GROUP_DIM), lambda t: (0, 0, 0)),
            pl.BlockSpec((1, POOL_W), const),
            pl.BlockSpec((V_W, D_MODEL), const),
            pl.BlockSpec((POOL_W, D_MODEL), const),
            pl.BlockSpec((D_MODEL, D_MODEL), const),
            pl.BlockSpec((1, D_MODEL), const),
            pl.BlockSpec((D_MODEL, LANES), const),
            pl.BlockSpec((1, LANES), const),
        ],
        out_specs=[
            pl.BlockSpec((TOK_TILE, D_MODEL), row),
            pl.BlockSpec((TOK_TILE, HALF_W), row),
            pl.BlockSpec((TOK_TILE, LANES), row),
            pl.BlockSpec((TOK_TILE, LANES), row),
        ],
        compiler_params=_params(("arbitrary",)),
        name="post",
    )(x_ctx, x_lat, mod3, o_f, o_b, og, pooled_c, pooled_l, mg, gng, wpg, psc, wbg, wbp, wout, n2g,
      wr, br)


def _route_kernel(idx_ref, rank_ref, cnt_ref, carry_ref):
    t = pl.program_id(0)

    @pl.when(t == 0)
    def _():
        carry_ref[...] = jnp.zeros((1, LANES), F32)

    idx = idx_ref[...]
    lane = lax.broadcasted_iota(I32, (ROUTE_TILE, LANES), 1)
    sel = [lane == idx[:, kk:kk + 1] for kk in range(TOP_K)]
    onehot = jnp.zeros((ROUTE_TILE, LANES), F32)
    for kk in range(TOP_K):
        onehot = onehot + jnp.where(sel[kk], 1.0, 0.0)
    row = lax.broadcasted_iota(I32, (ROUTE_TILE, ROUTE_TILE), 0)
    col = lax.broadcasted_iota(I32, (ROUTE_TILE, ROUTE_TILE), 1)
    strict = jnp.where(col < row, 1.0, 0.0).astype(BF16)
    before = _dot(strict, onehot.astype(BF16)) + carry_ref[...]
    rank = jnp.zeros((ROUTE_TILE, LANES), F32)
    for kk in range(TOP_K):
        rk = jnp.sum(jnp.where(sel[kk], before, 0.0), axis=-1, keepdims=True)
        rank = jnp.where(lane == kk, rk, rank)
    rank_ref[...] = rank.astype(I32)
    carry_ref[...] = carry_ref[...] + jnp.sum(onehot, axis=0, keepdims=True)
    cnt_ref[...] = jnp.broadcast_to(carry_ref[...], (8, LANES))


def _route_call(idx):
    return pl.pallas_call(
        _route_kernel,
        out_shape=[
            jax.ShapeDtypeStruct((N_TOK, LANES), I32),
            jax.ShapeDtypeStruct((8, LANES), F32),
        ],
        grid=(N_TOK // ROUTE_TILE,),
        in_specs=[pl.BlockSpec((ROUTE_TILE, LANES), lambda t: (t, 0))],
        out_specs=[
            pl.BlockSpec((ROUTE_TILE, LANES), lambda t: (t, 0)),
            pl.BlockSpec((8, LANES), lambda t: (0, 0)),
        ],
        scratch_shapes=[pltpu.VMEM((1, LANES), F32)],
        compiler_params=_params(("arbitrary",)),
        name="route",
    )(idx)


def _moe_kernel(fb_ref, nb_ref, nu_ref, x_hbm, wg_ref, bg_ref, wu_ref, bu_ref, wd_ref, bd_ref, y_hbm,
                xbuf, ybuf, wgb_ref, wub_ref, wdb_ref, sem_x, sem_y):
    e = pl.program_id(0)
    n_blocks = nb_ref[e]
    first = fb_ref[e]

    def rows_of(block):
        return pl.ds(pl.multiple_of(block * MOE_BLOCK, MOE_BLOCK), MOE_BLOCK)

    def x_copy(block, slot):
        return pltpu.make_async_copy(x_hbm.at[rows_of(block)], xbuf.at[slot], sem_x.at[slot])

    def y_copy(block, slot):
        return pltpu.make_async_copy(ybuf.at[slot], y_hbm.at[rows_of(block)], sem_y.at[slot])

    @pl.when(n_blocks > 0)
    def _():
        x_copy(first, 0).start()
        wgb_ref[...] = wg_ref[0].astype(BF16)
        wub_ref[...] = wu_ref[0].astype(BF16)
        wdb_ref[...] = wd_ref[0].astype(BF16)

        def body(j, carry):
            slot = j & 1
            x_copy(first + j, slot).wait()

            @pl.when(j + 1 < n_blocks)
            def _():
                x_copy(first + j + 1, 1 - slot).start()

            x_lo, x_hi = _unpack_halves(xbuf[slot])
            x = jnp.concatenate([x_lo.astype(BF16), x_hi.astype(BF16)], axis=-1)
            gate = jnp.minimum(_dot(x, wgb_ref[...]) + bg_ref[0], SWIGLU_LIMIT)
            up = jnp.clip(_dot(x, wub_ref[...]) + bu_ref[0], -SWIGLU_LIMIT, SWIGLU_LIMIT)
            act = (up + 1.0) * (gate * _sigmoid(SWIGLU_ALPHA * gate))
            y = _dot(act.astype(BF16), wdb_ref[...]) + bd_ref[0]

            @pl.when(j >= 2)
            def _():
                y_copy(first + j - 2, slot).wait()

            ybuf[slot] = _pack_halves(y[:, :HALF_W], y[:, HALF_W:])
            y_copy(first + j, slot).start()
            return carry

        lax.fori_loop(0, n_blocks, body, 0)

        @pl.when(n_blocks >= 2)
        def _():
            y_copy(first + n_blocks - 2, n_blocks & 1).wait()

        y_copy(first + n_blocks - 1, (n_blocks - 1) & 1).wait()

    @pl.when(e == N_EXPERTS - 1)
    def _():
        ybuf[0] = jnp.zeros((MOE_BLOCK, HALF_W), I32)

        def clear(b, carry):
            cp = y_copy(b, 0)
            cp.start()
            cp.wait()
            return carry

        lax.fori_loop(nu_ref[0], N_SLOT_BLOCKS, clear, 0)


def _moe_call(first_block, n_blocks, n_used, hs, w_gate, b_gate, w_up, b_up, w_down, b_down):
    wsel = lambda e, fb, nb, nu: (e, 0, 0)
    any_spec = pl.BlockSpec(memory_space=pl.ANY)
    return pl.pallas_call(
        _moe_kernel,
        out_shape=jax.ShapeDtypeStruct((N_SLOTS, HALF_W), I32),
        grid_spec=pltpu.PrefetchScalarGridSpec(
            num_scalar_prefetch=3,
            grid=(N_EXPERTS,),
            in_specs=[
                any_spec,
                pl.BlockSpec((1, D_MODEL, D_FF), wsel),
                pl.BlockSpec((1, 1, D_FF), wsel),
                pl.BlockSpec((1, D_MODEL, D_FF), wsel),
                pl.BlockSpec((1, 1, D_FF), wsel),
                pl.BlockSpec((1, D_FF, D_MODEL), wsel),
                pl.BlockSpec((1, 1, D_MODEL), wsel),
            ],
            out_specs=any_spec,
            scratch_shapes=[
                pltpu.VMEM((2, MOE_BLOCK, HALF_W), I32),
                pltpu.VMEM((2, MOE_BLOCK, HALF_W), I32),
                pltpu.VMEM((D_MODEL, D_FF), BF16),
                pltpu.VMEM((D_MODEL, D_FF), BF16),
                pltpu.VMEM((D_FF, D_MODEL), BF16),
                pltpu.SemaphoreType.DMA((2,)),
                pltpu.SemaphoreType.DMA((2,)),
            ],
        ),
        compiler_params=_params(("arbitrary",)),
        name="moe",
    )(first_block, n_blocks, n_used, hs, w_gate, b_gate, w_up, b_up, w_down, b_down)


SC_CORES = 2
SC_SUBCORES = 16
SC_WORKERS = SC_CORES * SC_SUBCORES
SC_ROWS = 128


def _sc_gather_rows(table, idx):
    n_idx = idx.shape[0]
    width = table.shape[1]
    per_worker = n_idx // SC_WORKERS
    n_chunks = per_worker // SC_ROWS
    assert n_chunks * SC_ROWS * SC_WORKERS == n_idx
    mesh = plsc.VectorSubcoreMesh(core_axis_name="c", subcore_axis_name="s")

    @functools.partial(
        pl.kernel, mesh=mesh,
        out_type=jax.ShapeDtypeStruct((n_idx, width), table.dtype),
        scratch_types=[pltpu.VMEM((SC_ROWS,), I32), pltpu.VMEM((SC_ROWS, width), table.dtype),
                       pltpu.SemaphoreType.DMA],
        name="sc_gather",
    )
    def gather(table_hbm, idx_hbm, out_hbm, idx_v, rows_v, sem):
        worker = lax.axis_index("s") * SC_CORES + lax.axis_index("c")
        base = worker * per_worker

        @pl.loop(0, n_chunks)
        def _(ch):
            off = pl.multiple_of(base + ch * SC_ROWS, SC_ROWS)
            pltpu.sync_copy(idx_hbm.at[pl.ds(off, SC_ROWS)], idx_v)
            pltpu.async_copy(table_hbm.at[idx_v], rows_v, sem).wait()
            pltpu.sync_copy(rows_v, out_hbm.at[pl.ds(off, SC_ROWS)])

    return gather(table, idx)


def _sc_scatter_rows(rows, idx3, n_out):
    n_rows, width = rows.shape
    n_chunks = n_rows // SC_ROWS // SC_WORKERS
    assert n_chunks * SC_ROWS * SC_WORKERS == n_rows and idx3.shape == (n_rows // SC_ROWS, TOP_K, SC_ROWS)
    mesh = plsc.VectorSubcoreMesh(core_axis_name="c", subcore_axis_name="s")

    @functools.partial(
        pl.kernel, mesh=mesh,
        out_type=jax.ShapeDtypeStruct((n_out, width), rows.dtype),
        scratch_types=[pltpu.VMEM((TOP_K, SC_ROWS), I32), pltpu.VMEM((SC_ROWS, width), rows.dtype),
                       pltpu.SemaphoreType.DMA],
        name="sc_scatter",
    )
    def scatter(rows_hbm, idx_hbm, out_hbm, idx_v, rows_v, sem):
        worker = lax.axis_index("s") * SC_CORES + lax.axis_index("c")

        @pl.loop(0, n_chunks)
        def _(ch):
            chunk = worker * n_chunks + ch
            pltpu.sync_copy(idx_hbm.at[chunk], idx_v)
            pltpu.sync_copy(rows_hbm.at[pl.ds(pl.multiple_of(chunk * SC_ROWS, SC_ROWS), SC_ROWS)],
                            rows_v)
            for kk in range(TOP_K):
                pltpu.async_copy(rows_v, out_hbm.at[idx_v.at[kk]], sem).wait()

    return scatter(rows, idx3)


def _combine_kernel(x1_ref, mod_ref, tw_ref, fg_ref, g_ref, out_ref):
    tw = tw_ref[...]
    f_lo = jnp.zeros((TOK_TILE, HALF_W), F32)
    f_hi = jnp.zeros((TOK_TILE, HALF_W), F32)
    for kk in range(TOP_K):
        lo, hi = _unpack_halves(g_ref[kk])
        f_lo = f_lo + lo * tw[:, kk:kk + 1]
        f_hi = f_hi + hi * tw[:, kk:kk + 1]
    gate2 = mod_ref[0][:, 5 * D_MODEL:6 * D_MODEL]
    x2 = x1_ref[...] + gate2 * jnp.concatenate([f_lo, f_hi], axis=-1)
    out_ref[...] = _rms(x2) * fg_ref[...]


def _combine_call(x1, mod3, tw, final_g, gathered):
    return pl.pallas_call(
        _combine_kernel,
        out_shape=jax.ShapeDtypeStruct((N_TOK, D_MODEL), F32),
        grid=(N_TILES,),
        in_specs=[
            pl.BlockSpec((TOK_TILE, D_MODEL), lambda t: (t, 0)),
            pl.BlockSpec((1, 1, N_MOD * D_MODEL), lambda t: (_mod_row(t), 0, 0)),
            pl.BlockSpec((TOK_TILE, LANES), lambda t: (t, 0)),
            pl.BlockSpec((1, D_MODEL), lambda t: (0, 0)),
            pl.BlockSpec((TOP_K, TOK_TILE, HALF_W), lambda t: (0, t, 0)),
        ],
        out_specs=pl.BlockSpec((TOK_TILE, D_MODEL), lambda t: (t, 0)),
        compiler_params=_params(("arbitrary",)),
        name="combine",
    )(x1, mod3, tw, final_g, gathered)


def kernel(x_prompt, x_sample, state_gla_fwd, state_gla_bwd, c, c_ctx, norm1_g, w_mod, b_mod, w_in,
           w_alpha, b_alpha, gla_norm_g, w_pool_grp, pool_scale, w_branch_gla, w_branch_pool, w_out,
           norm2_g, w_router, b_router, w_gate, b_gate, w_up, b_up, w_down, b_down, final_norm_g):
    l = 0
    x_ctx = x_prompt.reshape(N_CTX, D_MODEL)
    x_lat = x_sample.reshape(N_LAT, D_MODEL)

    cvec = jnp.concatenate([c_ctx[None, :], c, jnp.zeros((8 - 1 - DEC_BATCH, D_MODEL), F32)], axis=0)
    mod = _mod_call(cvec, w_mod[l], b_mod[l][None, :])
    mod3 = mod.reshape(8, 1, N_MOD * D_MODEL)

    w_in_b = w_in[l].astype(BF16)
    w_main = w_in_b[:, :MAIN_W]
    w_alr = w_in_b[:, MAIN_W:MAIN_W + ALR_W]
    w_xp = w_in_b[:, MAIN_W + ALR_W:MAIN_W + ALR_W + POOL_W]
    w_mg = w_in_b[:, MAIN_W + ALR_W + POOL_W:]
    q, k, v, og, alr, xp, mg = _inproj_call(x_ctx, x_lat, mod3, norm1_g[l][None, :],
                                            w_main, w_alr, w_xp, w_mg)

    zpad = jnp.zeros((GLA_LOWRANK, QK_W), F32)
    wa_f = jnp.concatenate([w_alpha[l, 0], zpad], axis=0)
    wa_b = jnp.concatenate([zpad, w_alpha[l, 1]], axis=0)
    o_f, o_b, s_f, s_b = _gla_call(q, k, v, alr, wa_f, b_alpha[l, 0][None, :], wa_b,
                                   b_alpha[l, 1][None, :], state_gla_fwd[:, l], state_gla_bwd[:, l])

    pooled_c = _pool_ctx_call(xp)
    pooled_l = _pool_lat_call(xp)

    w_router_pad = jnp.pad(w_router[l], ((0, 0), (0, LANES - N_EXPERTS)))
    b_router_pad = jnp.pad(b_router[l], (0, LANES - N_EXPERTS))[None, :]
    x1, h2, top_idx, top_w = _post_call(
        x_ctx, x_lat, mod3, o_f, o_b, og, pooled_c, pooled_l, mg,
        gla_norm_g[l].reshape(1, V_W), w_pool_grp[l].astype(BF16), pool_scale[l][None, :],
        w_branch_gla[l].astype(BF16), w_branch_pool[l].astype(BF16), w_out[l].astype(BF16),
        norm2_g[l][None, :], w_router_pad, b_router_pad)

    rank, cnt = _route_call(top_idx)
    counts = cnt[0, :N_EXPERTS].astype(I32)
    padded = (counts + MOE_BLOCK - 1) // MOE_BLOCK * MOE_BLOCK
    pad_end = jnp.cumsum(padded).astype(I32)
    pad_start = pad_end - padded
    n_used = (pad_end[-1:] // MOE_BLOCK).astype(I32)
    experts = jnp.arange(N_EXPERTS, dtype=I32)
    tk = top_idx[:, :TOP_K]
    pos = jnp.sum(jnp.where(tk[:, :, None] == experts, pad_start, 0), axis=-1) + rank[:, :TOP_K]
    pos = pos.astype(I32)
    pos_by_choice = pos.T
    pos_chunks = pos_by_choice.reshape(TOP_K, N_TOK // SC_ROWS, SC_ROWS).transpose(1, 0, 2)

    hs = _sc_scatter_rows(h2, pos_chunks, N_SLOTS)
    y = _moe_call(pad_start // MOE_BLOCK, padded // MOE_BLOCK, n_used, hs,
                  w_gate[l], b_gate[l][:, None, :], w_up[l], b_up[l][:, None, :],
                  w_down[l], b_down[l][:, None, :])
    gathered = _sc_gather_rows(y, pos_by_choice.reshape(TOP_K * N_TOK))
    out = _combine_call(x1, mod3, top_w, final_norm_g[None, :],
                        gathered.reshape(TOP_K, N_TOK, HALF_W))

    y_prompt = out[:N_CTX].reshape(BATCH, SEQ, D_MODEL)
    y_sample = out[N_CTX:].reshape(DEC_BATCH, DEC_SEQ, D_MODEL)
    return (y_prompt, y_sample, s_f[:, None], s_b[:, None])
```

```python
import functools

import jax
import jax.numpy as jnp
from jax import lax
from jax.experimental import pallas as pl
from jax.experimental.pallas import tpu as pltpu
from jax.experimental.pallas import tpu_sc as plsc

F32 = jnp.float32
BF16 = jnp.bfloat16
I32 = jnp.int32

D_MODEL = 1024
BATCH = 32
SEQ = 256
DEC_BATCH = 4
DEC_SEQ = 2048
GRID_W = 64
GLA_HEADS = 4
GLA_DK = 128
GLA_DV = 256
GLA_LOWRANK = 16
GLA_TAU = 16.0
GLA_CHUNK = 64
POOL_GROUPS = 4
POOL_GROUP_DIM = 128
POOL_WINDOWS = (2, 4, 8, 16)
N_EXPERTS = 32
TOP_K = 4
D_FF = 1024
SWIGLU_LIMIT = 7.0
SWIGLU_ALPHA = 1.702
MOE_BLOCK = 256
NORM_EPS = 1e-6
N_MOD = 6

QK_W = GLA_HEADS * GLA_DK
V_W = GLA_HEADS * GLA_DV
POOL_W = POOL_GROUPS * POOL_GROUP_DIM
MAIN_W = 2 * QK_W + 2 * V_W
ALR_W = 2 * GLA_LOWRANK
MG_W = 2 * D_MODEL

N_CTX = BATCH * SEQ
N_LAT = DEC_BATCH * DEC_SEQ
N_TOK = N_CTX + N_LAT
N_SLOT_BLOCKS = -(-(N_TOK * TOP_K + N_EXPERTS * (MOE_BLOCK - 1)) // MOE_BLOCK)
N_SLOTS = N_SLOT_BLOCKS * MOE_BLOCK

LANES = 128
TOK_TILE = 256
N_TILES = N_TOK // TOK_TILE
CTX_TILES = N_CTX // TOK_TILE
LAT_TILES_PER_SEQ = DEC_SEQ // TOK_TILE
ROUTE_TILE = 512
VMEM_LIMIT = 56 * 1024 * 1024

GLA_SEQS = 4
CTX_CHUNKS = SEQ // GLA_CHUNK
LAT_CHUNKS = DEC_SEQ // GLA_CHUNK
CHUNKS_PER_TILE = TOK_TILE // GLA_CHUNK
GLA_CTX_STEPS = (BATCH // GLA_SEQS) * CTX_CHUNKS
TILE_GRID = 8

NT_DIMS = (((1,), (1,)), ((), ()))
TN_DIMS = (((0,), (0,)), ((), ()))

assert DEC_BATCH == GLA_SEQS and SEQ == TOK_TILE and N_TILES == TILE_GRID * TILE_GRID


def _params(semantics, vmem=VMEM_LIMIT):
    return pltpu.CompilerParams(dimension_semantics=semantics, vmem_limit_bytes=vmem)


def _split_bf16(a):
    hi = a.astype(BF16)
    lo = (a - hi.astype(F32)).astype(BF16)
    return hi, lo


def _dot(a, b):
    return jnp.dot(a, b, preferred_element_type=F32)


def _dot3(a, b):
    a_hi, a_lo = _split_bf16(a)
    b_hi, b_lo = _split_bf16(b)
    return _dot(a_hi, b_hi) + _dot(a_lo, b_hi) + _dot(a_hi, b_lo)


def _sigmoid(x):
    return 1.0 / (1.0 + jnp.exp(-x))


HALF_W = D_MODEL // 2
HIGH_HALF_MASK = -65536


def _pack_halves(lo, hi):
    lo_bits = pltpu.bitcast(lo.astype(BF16).astype(F32), I32)
    hi_bits = pltpu.bitcast(hi.astype(BF16).astype(F32), I32)
    return lax.shift_right_logical(lo_bits, 16) | (hi_bits & HIGH_HALF_MASK)


def _unpack_halves(words):
    lo = pltpu.bitcast(lax.shift_left(words, 16), F32)
    hi = pltpu.bitcast(words & HIGH_HALF_MASK, F32)
    return lo, hi


def _rms(x):
    return x * lax.rsqrt(jnp.mean(x * x, axis=-1, keepdims=True) + NORM_EPS)


def _mod_row(t):
    return jnp.where(t < CTX_TILES, 0, 1 + (t - CTX_TILES) // LAT_TILES_PER_SEQ)


def _store_tile(t):
    u = t - CTX_TILES
    return jnp.where(t < CTX_TILES, t,
                     CTX_TILES + DEC_BATCH * (u % LAT_TILES_PER_SEQ) + u // LAT_TILES_PER_SEQ)


def _ctx_tile(t):
    return jnp.minimum(t, CTX_TILES - 1)


def _lat_tile(t):
    return jnp.maximum(t - CTX_TILES, 0)


def _mod_kernel(c_ref, w_ref, b_ref, o_ref):
    c = c_ref[...]
    o_ref[...] = _dot3(c * _sigmoid(c), w_ref[...]) + b_ref[...]


def _mod_call(cvec, w_mod, b_mod):
    rows = cvec.shape[0]
    return pl.pallas_call(
        _mod_kernel,
        out_shape=jax.ShapeDtypeStruct((rows, N_MOD * D_MODEL), F32),
        grid=(N_MOD,),
        in_specs=[
            pl.BlockSpec((rows, D_MODEL), lambda j: (0, 0)),
            pl.BlockSpec((D_MODEL, D_MODEL), lambda j: (0, j)),
            pl.BlockSpec((1, D_MODEL), lambda j: (0, j)),
        ],
        out_specs=pl.BlockSpec((rows, D_MODEL), lambda j: (0, j)),
        compiler_params=_params(("arbitrary",)),
        name="mod",
    )(cvec, w_mod, b_mod)


def _inproj_kernel(xc_ref, xl_ref, mod_ref, g_ref, wmain_ref, walr_ref, wxp_ref, wmg_ref,
                   q_ref, k_ref, v_ref, og_ref, alr_ref, xp_ref, mg_ref):
    t = pl.program_id(0)
    x = jnp.where(t < CTX_TILES, xc_ref[...], xl_ref[...])
    mod = mod_ref[0]
    shift1 = mod[:, 0:D_MODEL]
    scale1 = mod[:, D_MODEL:2 * D_MODEL]
    h = _rms(x) * g_ref[...]
    h = (h * (1.0 + scale1) + shift1).astype(BF16)
    z = _dot(h, wmain_ref[...])
    q_ref[...] = (z[:, 0:QK_W] * (GLA_DK ** -0.5)).astype(BF16)
    k_ref[...] = z[:, QK_W:2 * QK_W].astype(BF16)
    v_ref[...] = z[:, 2 * QK_W:2 * QK_W + V_W].astype(BF16)
    og_ref[...] = z[:, 2 * QK_W + V_W:MAIN_W].astype(BF16)
    alr_ref[...] = _dot(h, walr_ref[...])
    xp_ref[...] = _dot(h, wxp_ref[...])
    mg_ref[...] = _dot(h, wmg_ref[...]).astype(BF16)


def _inproj_call(x_ctx, x_lat, mod3, norm1_g, w_main, w_alr, w_xp, w_mg):
    const = lambda t: (0, 0)
    stored = lambda t: (_store_tile(t), 0)
    widths = (QK_W, QK_W, V_W, V_W, ALR_W, POOL_W, MG_W)
    dtypes = (BF16, BF16, BF16, BF16, F32, F32, BF16)
    return pl.pallas_call(
        _inproj_kernel,
        out_shape=[jax.ShapeDtypeStruct((N_TOK, w), dt) for w, dt in zip(widths, dtypes)],
        grid=(N_TILES,),
        in_specs=[
            pl.BlockSpec((TOK_TILE, D_MODEL), lambda t: (_ctx_tile(t), 0)),
            pl.BlockSpec((TOK_TILE, D_MODEL), lambda t: (_lat_tile(t), 0)),
            pl.BlockSpec((1, 1, N_MOD * D_MODEL), lambda t: (_mod_row(t), 0, 0)),
            pl.BlockSpec((1, D_MODEL), const),
            pl.BlockSpec((D_MODEL, MAIN_W), const),
            pl.BlockSpec((D_MODEL, ALR_W), const),
            pl.BlockSpec((D_MODEL, POOL_W), const),
            pl.BlockSpec((D_MODEL, MG_W), const),
        ],
        out_specs=[pl.BlockSpec((TOK_TILE, w), stored) for w in widths],
        compiler_params=_params(("arbitrary",)),
        name="inproj",
    )(x_ctx, x_lat, mod3, norm1_g, w_main, w_alr, w_xp, w_mg)


def _gla_direction(q_ref, k_ref, v_ref, alr_ref, wa_ref, ba_ref, o_ref, st_ref, slot0, rev):
    rows = GLA_SEQS * GLA_CHUNK
    alr = jnp.concatenate([alr_ref[0, s] for s in range(GLA_SEQS)], axis=0)
    a = _dot3(alr, wa_ref[...]) + ba_ref[...]
    g = (jnp.minimum(a, 0.0) - jnp.log(1.0 + jnp.exp(-jnp.abs(a)))) * (1.0 / GLA_TAU)

    row = lax.broadcasted_iota(I32, (rows, rows), 0)
    col = lax.broadcasted_iota(I32, (rows, rows), 1)
    same = (row // GLA_CHUNK) == (col // GLA_CHUNK)
    tri_all = same & ((col >= row) if rev else (col <= row))
    tri_b = jnp.where(tri_all, 1.0, 0.0).astype(BF16)
    g_hi, g_lo = _split_bf16(g)
    bcum_all = _dot(tri_b, g_hi) + _dot(tri_b, g_lo)

    r64 = lax.broadcasted_iota(I32, (GLA_CHUNK, GLA_CHUNK), 0)
    c64 = lax.broadcasted_iota(I32, (GLA_CHUNK, GLA_CHUNK), 1)
    tri = (c64 >= r64) if rev else (c64 <= r64)

    for s in range(GLA_SEQS):
        bcum = bcum_all[s * GLA_CHUNK:(s + 1) * GLA_CHUNK]
        blast = bcum[0:1] if rev else bcum[GLA_CHUNK - 1:GLA_CHUNK]
        bmid = bcum[GLA_CHUNK // 2:GLA_CHUNK // 2 + 1]
        e_q = jnp.exp(bcum - bmid)
        e_k = jnp.exp(bmid - bcum)
        e_in = jnp.exp(bcum)
        e_out = jnp.exp(blast - bcum)
        e_last = jnp.exp(blast)
        q = q_ref[0, s].astype(F32)
        k = k_ref[0, s].astype(F32)
        for h in range(GLA_HEADS):
            ks = slice(h * GLA_DK, (h + 1) * GLA_DK)
            vs = slice(h * GLA_DV, (h + 1) * GLA_DV)
            qh = q[:, ks]
            kh = k[:, ks]
            vh = v_ref[0, s, :, vs]
            att = lax.dot_general((qh * e_q[:, ks]).astype(BF16), (kh * e_k[:, ks]).astype(BF16),
                                  NT_DIMS, preferred_element_type=F32)
            att = jnp.where(tri, att, 0.0).astype(BF16)
            st = st_ref[slot0 + s, h]
            o_inter = lax.dot_general((qh * e_in[:, ks]).astype(BF16), st.astype(BF16),
                                      NT_DIMS, preferred_element_type=F32)
            o_ref[0, s, :, vs] = o_inter + _dot(att, vh)
            upd = lax.dot_general(vh, (kh * e_out[:, ks]).astype(BF16), TN_DIMS,
                                  preferred_element_type=F32)
            st_ref[slot0 + s, h] = st * e_last[:, ks] + upd


def _gla_kernel(qf_ref, kf_ref, vf_ref, af_ref, qb_ref, kb_ref, vb_ref, ab_ref,
                waf_ref, baf_ref, wab_ref, bab_ref, s0f_ref, s0b_ref,
                of_ref, ob_ref, sf_ref, sb_ref, st_ref):
    i = pl.program_id(0)
    is_ctx = i < GLA_CTX_STEPS
    chunk = jnp.where(is_ctx, i % CTX_CHUNKS, i - GLA_CTX_STEPS)

    @pl.when(is_ctx & (chunk == 0))
    def _():
        st_ref[...] = jnp.zeros(st_ref.shape, F32)

    @pl.when(i == GLA_CTX_STEPS)
    def _():
        for s in range(GLA_SEQS):
            for h in range(GLA_HEADS):
                st_ref[s, h] = s0f_ref[s, h].T
                st_ref[GLA_SEQS + s, h] = s0b_ref[s, h].T

    _gla_direction(qf_ref, kf_ref, vf_ref, af_ref, waf_ref, baf_ref, of_ref, st_ref, 0, False)
    _gla_direction(qb_ref, kb_ref, vb_ref, ab_ref, wab_ref, bab_ref, ob_ref, st_ref, GLA_SEQS, True)

    @pl.when(is_ctx & (chunk == CTX_CHUNKS - 1))
    def _():
        for s in range(GLA_SEQS):
            for h in range(GLA_HEADS):
                sf_ref[s, h] = st_ref[s, h].T
                sb_ref[s, h] = st_ref[GLA_SEQS + s, h].T


def _gla_block(i, rev):
    is_ctx = i < GLA_CTX_STEPS
    group = i // CTX_CHUNKS
    c_ctx = i % CTX_CHUNKS
    c_lat = i - GLA_CTX_STEPS
    if rev:
        c_ctx = CTX_CHUNKS - 1 - c_ctx
        c_lat = LAT_CHUNKS - 1 - c_lat
    j = c_lat // CHUNKS_PER_TILE
    per_row = TILE_GRID // GLA_SEQS
    a = jnp.where(is_ctx, group // per_row, CTX_TILES // TILE_GRID + j // per_row)
    b = jnp.where(is_ctx, group % per_row, j % per_row)
    c = jnp.where(is_ctx, c_ctx, c_lat % CHUNKS_PER_TILE)
    return (a, b, c, 0)


def _gla_call(q, k, v, alr, wa_f, ba_f, wa_b, ba_b, s0_f, s0_b):
    def view(arr):
        return arr.reshape(TILE_GRID, TILE_GRID, TOK_TILE, arr.shape[-1])

    def spec(width, rev):
        return pl.BlockSpec((1, GLA_SEQS, GLA_CHUNK, width), lambda i: _gla_block(i, rev))

    const = lambda i: (0, 0)
    st_block = (GLA_SEQS, GLA_HEADS, GLA_DK, GLA_DV)
    whole_state = pl.BlockSpec(st_block, lambda i: (0, 0, 0, 0))
    ctx_state = pl.BlockSpec(
        st_block, lambda i: (jnp.minimum(i // CTX_CHUNKS, BATCH // GLA_SEQS - 1), 0, 0, 0))
    in_specs = []
    for rev in (False, True):
        in_specs += [spec(QK_W, rev), spec(QK_W, rev), spec(V_W, rev), spec(ALR_W, rev)]
    in_specs += [pl.BlockSpec((ALR_W, QK_W), const), pl.BlockSpec((1, QK_W), const)] * 2
    in_specs += [whole_state, whole_state]
    o_shape = jax.ShapeDtypeStruct((TILE_GRID, TILE_GRID, TOK_TILE, V_W), F32)
    s_shape = jax.ShapeDtypeStruct((BATCH, GLA_HEADS, GLA_DK, GLA_DV), F32)
    qv, kv, vv, av = view(q), view(k), view(v), view(alr)
    o_f, o_b, s_f, s_b = pl.pallas_call(
        _gla_kernel,
        out_shape=[o_shape, o_shape, s_shape, s_shape],
        grid=(GLA_CTX_STEPS + LAT_CHUNKS,),
        in_specs=in_specs,
        out_specs=[spec(V_W, False), spec(V_W, True), ctx_state, ctx_state],
        scratch_shapes=[pltpu.VMEM((2 * GLA_SEQS, GLA_HEADS, GLA_DV, GLA_DK), F32)],
        compiler_params=_params(("arbitrary",)),
        name="gla",
    )(qv, kv, vv, av, qv, kv, vv, av, wa_f, ba_f, wa_b, ba_b, s0_f, s0_b)
    return o_f.reshape(N_TOK, V_W), o_b.reshape(N_TOK, V_W), s_f, s_b


def _band(n, w, block):
    row = lax.broadcasted_iota(I32, (n, n), 0)
    col = lax.broadcasted_iota(I32, (n, n), 1)
    inside = (col >= row - w // 2) & (col <= row + w // 2 - 1)
    if block < n:
        inside = inside & ((row // block) == (col // block))
    return jnp.where(inside, 1.0, 0.0).astype(BF16)


def _win_count(p, n, w):
    return jnp.minimum(p + w // 2 - 1, n - 1) - jnp.maximum(p - w // 2, 0) + 1


def _pool_ctx_kernel(x_ref, o_ref):
    p = lax.broadcasted_iota(I32, (SEQ, POOL_GROUP_DIM), 0)
    for gi, w in enumerate(POOL_WINDOWS):
        cs = slice(gi * POOL_GROUP_DIM, (gi + 1) * POOL_GROUP_DIM)
        x = x_ref[:, cs]
        hi, lo = _split_bf16(x)
        band = _band(SEQ, w, SEQ)
        s = _dot(band, hi) + _dot(band, lo)
        cnt = _win_count(p, SEQ, w).astype(F32)
        o_ref[:, cs] = s / cnt - x


def _pool_ctx_call(xp):
    spec = pl.BlockSpec((SEQ, POOL_W), lambda b: (b, 0))
    return pl.pallas_call(
        _pool_ctx_kernel,
        out_shape=jax.ShapeDtypeStruct((N_CTX, POOL_W), F32),
        grid=(BATCH,),
        in_specs=[spec],
        out_specs=spec,
        compiler_params=_params(("arbitrary",)),
        name="pool_ctx",
    )(xp)


POOL_HALO = (max(POOL_WINDOWS) // 2) * GRID_W


def _pool_lat_kernel(x_ref, o_ref, pad_ref):
    rows = DEC_SEQ // GRID_W
    p = lax.broadcasted_iota(I32, (DEC_SEQ, POOL_GROUP_DIM), 0)
    r = p // GRID_W
    cidx = p % GRID_W
    zeros = jnp.zeros((POOL_HALO, POOL_GROUP_DIM), F32)
    pad_ref[0:POOL_HALO, :] = zeros
    pad_ref[POOL_HALO + DEC_SEQ:2 * POOL_HALO + DEC_SEQ, :] = zeros
    for gi, w in enumerate(POOL_WINDOWS):
        cs = slice(gi * POOL_GROUP_DIM, (gi + 1) * POOL_GROUP_DIM)
        band = _band(TOK_TILE, w, GRID_W)
        for t in range(LAT_TILES_PER_SEQ):
            hi, lo = _split_bf16(x_ref[t, 0, :, cs])
            pad_ref[POOL_HALO + t * TOK_TILE:POOL_HALO + (t + 1) * TOK_TILE, :] = (
                _dot(band, hi) + _dot(band, lo))
        acc = jnp.zeros((DEC_SEQ, POOL_GROUP_DIM), F32)
        for dr in range(-(w // 2), w // 2):
            start = POOL_HALO + dr * GRID_W
            acc = acc + pad_ref[start:start + DEC_SEQ, :]
        cnt = (_win_count(r, rows, w) * _win_count(cidx, GRID_W, w)).astype(F32)
        pooled = acc / cnt
        for t in range(LAT_TILES_PER_SEQ):
            rs = slice(t * TOK_TILE, (t + 1) * TOK_TILE)
            o_ref[t, 0, :, cs] = pooled[rs] - x_ref[t, 0, :, cs]


def _pool_lat_call(xp):
    view = xp.reshape(N_TILES // DEC_BATCH, DEC_BATCH, TOK_TILE, POOL_W)
    blk = (LAT_TILES_PER_SEQ, 1, TOK_TILE, POOL_W)
    out = pl.pallas_call(
        _pool_lat_kernel,
        out_shape=jax.ShapeDtypeStruct((LAT_TILES_PER_SEQ, DEC_BATCH, TOK_TILE, POOL_W), F32),
        grid=(DEC_BATCH,),
        in_specs=[pl.BlockSpec(blk, lambda s: (CTX_TILES // DEC_BATCH // LAT_TILES_PER_SEQ, s, 0, 0))],
        out_specs=pl.BlockSpec(blk, lambda s: (0, s, 0, 0)),
        scratch_shapes=[pltpu.VMEM((DEC_SEQ + 2 * POOL_HALO, POOL_GROUP_DIM), F32)],
        compiler_params=_params(("arbitrary",)),
        name="pool_lat",
    )(view)
    return out.reshape(N_LAT, POOL_W)


def _post_kernel(xc_ref, xl_ref, mod_ref, of_ref, ob_ref, og_ref, pc_ref, pl_ref, mg_ref, gng_ref,
                 wpg_ref, psc_ref, wbg_ref, wbp_ref, wout_ref, n2g_ref, wr_ref, br_ref,
                 x1_ref, h2_ref, idx_ref, tw_ref):
    t = pl.program_id(0)
    is_ctx = t < CTX_TILES
    x = jnp.where(is_ctx, xc_ref[...], xl_ref[...])
    pooled = jnp.where(is_ctx, pc_ref[...], pl_ref[...])
    mod = mod_ref[0]
    gate1 = mod[:, 2 * D_MODEL:3 * D_MODEL]
    shift2 = mod[:, 3 * D_MODEL:4 * D_MODEL]
    scale2 = mod[:, 4 * D_MODEL:5 * D_MODEL]

    o = of_ref[...] + ob_ref[...]
    og = og_ref[...].astype(F32)
    gated = []
    for h in range(GLA_HEADS):
        vs = slice(h * GLA_DV, (h + 1) * GLA_DV)
        oh = _rms(o[:, vs]) * gng_ref[:, vs]
        ogh = og[:, vs]
        gated.append((oh * (ogh * _sigmoid(ogh))).astype(BF16))
    br_gla = _dot(jnp.concatenate(gated, axis=-1), wbg_ref[...])

    pm = []
    for gi in range(POOL_GROUPS):
        cs = slice(gi * POOL_GROUP_DIM, (gi + 1) * POOL_GROUP_DIM)
        pmg = _dot(pooled[:, cs].astype(BF16), wpg_ref[gi]) * psc_ref[:, cs]
        pm.append(pmg.astype(BF16))
    br_pool = _dot(jnp.concatenate(pm, axis=-1), wbp_ref[...])

    mg = mg_ref[...].astype(F32)
    merged = _sigmoid(mg[:, 0:D_MODEL]) * br_gla + _sigmoid(mg[:, D_MODEL:MG_W]) * br_pool
    m = _dot(merged.astype(BF16), wout_ref[...])
    x1 = x + gate1 * m
    x1_ref[...] = x1
    h2 = _rms(x1) * n2g_ref[...]
    h2 = h2 * (1.0 + scale2) + shift2
    h2_ref[...] = _pack_halves(h2[:, :HALF_W], h2[:, HALF_W:])

    logits = _dot3(h2, wr_ref[...]) + br_ref[...]
    lane = lax.broadcasted_iota(I32, (TOK_TILE, LANES), 1)
    lane_f = lane.astype(F32)
    neg = jnp.float32(-jnp.inf)
    cur = jnp.where(lane < N_EXPERTS, logits, neg)
    vals, idxs = [], []
    for _ in range(TOP_K):
        mx = jnp.max(cur, axis=-1, keepdims=True)
        ix = jnp.min(jnp.where(cur == mx, lane_f, float(LANES)), axis=-1, keepdims=True)
        vals.append(mx)
        idxs.append(ix)
        cur = jnp.where(lane_f == ix, neg, cur)
    ex = [jnp.exp(vv - vals[0]) for vv in vals]
    tot = ex[0] + ex[1] + ex[2] + ex[3]
    idx_out = jnp.zeros((TOK_TILE, LANES), F32)
    w_out = jnp.zeros((TOK_TILE, LANES), F32)
    for kk in range(TOP_K):
        idx_out = jnp.where(lane == kk, idxs[kk], idx_out)
        w_out = jnp.where(lane == kk, ex[kk] / tot, w_out)
    idx_ref[...] = idx_out.astype(I32)
    tw_ref[...] = w_out


def _post_call(x_ctx, x_lat, mod3, o_f, o_b, og, pooled_c, pooled_l, mg, gng, wpg, psc, wbg, wbp,
               wout, n2g, wr, br):
    row = lambda t: (t, 0)
    const = lambda t: (0, 0)
    stored = lambda t: (_store_tile(t), 0)
    return pl.pallas_call(
        _post_kernel,
        out_shape=[
            jax.ShapeDtypeStruct((N_TOK, D_MODEL), F32),
            jax.ShapeDtypeStruct((N_TOK, HALF_W), I32),
            jax.ShapeDtypeStruct((N_TOK, LANES), I32),
            jax.ShapeDtypeStruct((N_TOK, LANES), F32),
        ],
        grid=(N_TILES,),
        in_specs=[
            pl.BlockSpec((TOK_TILE, D_MODEL), lambda t: (_ctx_tile(t), 0)),
            pl.BlockSpec((TOK_TILE, D_MODEL), lambda t: (_lat_tile(t), 0)),
            pl.BlockSpec((1, 1, N_MOD * D_MODEL), lambda t: (_mod_row(t), 0, 0)),
            pl.BlockSpec((TOK_TILE, V_W), stored),
            pl.BlockSpec((TOK_TILE, V_W), stored),
            pl.BlockSpec((TOK_TILE, V_W), stored),
            pl.BlockSpec((TOK_TILE, POOL_W), lambda t: (_ctx_tile(t), 0)),
            pl.BlockSpec((TOK_TILE, POOL_W),
                         lambda t: (jnp.maximum(_store_tile(t) - CTX_TILES, 0), 0)),
            pl.BlockSpec((TOK_TILE, MG_W), stored),
            pl.BlockSpec((1, V_W), const),
            pl.BlockSpec((POOL_GROUPS, POOL_GROUP_DIM, POOL_GROUP_DIM), lambda t: (0, 0, 0)),
            pl.BlockSpec((1, POOL_W), const),
            pl.BlockSpec((V_W, D_MODEL), const),
            pl.BlockSpec((POOL_W, D_MODEL), const),
            pl.BlockSpec((D_MODEL, D_MODEL), const),
            pl.BlockSpec((1, D_MODEL), const),
            pl.BlockSpec((D_MODEL, LANES), const),
            pl.BlockSpec((1, LANES), const),
        ],
        out_specs=[
            pl.BlockSpec((TOK_TILE, D_MODEL), row),
            pl.BlockSpec((TOK_TILE, HALF_W), row),
            pl.BlockSpec((TOK_TILE, LANES), row),
            pl.BlockSpec((TOK_TILE, LANES), row),
        ],
        compiler_params=_params(("arbitrary",)),
        name="post",
    )(x_ctx, x_lat, mod3, o_f, o_b, og, pooled_c, pooled_l, mg, gng, wpg, psc, wbg, wbp, wout, n2g,
      wr, br)


def _route_kernel(idx_ref, rank_ref, cnt_ref, carry_ref):
    t = pl.program_id(0)

    @pl.when(t == 0)
    def _():
        carry_ref[...] = jnp.zeros((1, LANES), F32)

    idx = idx_ref[...]
    lane = lax.broadcasted_iota(I32, (ROUTE_TILE, LANES), 1)
    sel = [lane == idx[:, kk:kk + 1] for kk in range(TOP_K)]
    onehot = jnp.zeros((ROUTE_TILE, LANES), F32)
    for kk in range(TOP_K):
        onehot = onehot + jnp.where(sel[kk], 1.0, 0.0)
    row = lax.broadcasted_iota(I32, (ROUTE_TILE, ROUTE_TILE), 0)
    col = lax.broadcasted_iota(I32, (ROUTE_TILE, ROUTE_TILE), 1)
    strict = jnp.where(col < row, 1.0, 0.0).astype(BF16)
    before = _dot(strict, onehot.astype(BF16)) + carry_ref[...]
    rank = jnp.zeros((ROUTE_TILE, LANES), F32)
    for kk in range(TOP_K):
        rk = jnp.sum(jnp.where(sel[kk], before, 0.0), axis=-1, keepdims=True)
        rank = jnp.where(lane == kk, rk, rank)
    rank_ref[...] = rank.astype(I32)
    carry_ref[...] = carry_ref[...] + jnp.sum(onehot, axis=0, keepdims=True)
    cnt_ref[...] = jnp.broadcast_to(carry_ref[...], (8, LANES))


def _route_call(idx):
    return pl.pallas_call(
        _route_kernel,
        out_shape=[
            jax.ShapeDtypeStruct((N_TOK, LANES), I32),
            jax.ShapeDtypeStruct((8, LANES), F32),
        ],
        grid=(N_TOK // ROUTE_TILE,),
        in_specs=[pl.BlockSpec((ROUTE_TILE, LANES), lambda t: (t, 0))],
        out_specs=[
            pl.BlockSpec((ROUTE_TILE, LANES), lambda t: (t, 0)),
            pl.BlockSpec((8, LANES), lambda t: (0, 0)),
        ],
        scratch_shapes=[pltpu.VMEM((1, LANES), F32)],
        compiler_params=_params(("arbitrary",)),
        name="route",
    )(idx)


FF_TILE = 256


def _moe_kernel(be_ref, nu_ref, x_ref, wg_ref, bg_ref, wu_ref, bu_ref, wd_ref, bd_ref, y_ref,
                wgb_ref, wub_ref, wdb_ref):
    b = pl.program_id(0)
    n_used = nu_ref[0]
    e = be_ref[b]
    prev = be_ref[jnp.maximum(b - 1, 0)]
    live = b < n_used

    @pl.when(live & ((b == 0) | (e != prev)))
    def _():
        wgb_ref[...] = wg_ref[0].astype(BF16)
        wub_ref[...] = wu_ref[0].astype(BF16)
        wdb_ref[...] = wd_ref[0].astype(BF16)

    @pl.when(live)
    def _():
        x_lo, x_hi = _unpack_halves(x_ref[...])
        x = jnp.concatenate([x_lo.astype(BF16), x_hi.astype(BF16)], axis=-1)
        acts = []
        for n in range(D_FF // FF_TILE):
            cs = slice(n * FF_TILE, (n + 1) * FF_TILE)
            gate = jnp.minimum(_dot(x, wgb_ref[:, cs]) + bg_ref[0][:, cs], SWIGLU_LIMIT)
            up = jnp.clip(_dot(x, wub_ref[:, cs]) + bu_ref[0][:, cs], -SWIGLU_LIMIT, SWIGLU_LIMIT)
            acts.append(((up + 1.0) * (gate * _sigmoid(SWIGLU_ALPHA * gate))).astype(BF16))
        y = _dot(jnp.concatenate(acts, axis=-1), wdb_ref[...]) + bd_ref[0]
        y_ref[...] = _pack_halves(y[:, :HALF_W], y[:, HALF_W:])

    @pl.when(jnp.logical_not(live))
    def _():
        y_ref[...] = jnp.zeros((MOE_BLOCK, HALF_W), I32)


def _moe_call(block_e, n_used, hs, w_gate, b_gate, w_up, b_up, w_down, b_down):
    def blk(b, be, nu):
        return jnp.minimum(b, nu[0] - 1)

    row = lambda b, be, nu: (blk(b, be, nu), 0)
    wsel = lambda b, be, nu: (be[blk(b, be, nu)], 0, 0)
    return pl.pallas_call(
        _moe_kernel,
        out_shape=jax.ShapeDtypeStruct((N_SLOTS, HALF_W), I32),
        grid_spec=pltpu.PrefetchScalarGridSpec(
            num_scalar_prefetch=2,
            grid=(N_SLOT_BLOCKS,),
            in_specs=[
                pl.BlockSpec((MOE_BLOCK, HALF_W), row),
                pl.BlockSpec((1, D_MODEL, D_FF), wsel),
                pl.BlockSpec((1, 1, D_FF), wsel),
                pl.BlockSpec((1, D_MODEL, D_FF), wsel),
                pl.BlockSpec((1, 1, D_FF), wsel),
                pl.BlockSpec((1, D_FF, D_MODEL), wsel),
                pl.BlockSpec((1, 1, D_MODEL), wsel),
            ],
            out_specs=pl.BlockSpec((MOE_BLOCK, HALF_W), lambda b, be, nu: (b, 0)),
            scratch_shapes=[
                pltpu.VMEM((D_MODEL, D_FF), BF16),
                pltpu.VMEM((D_MODEL, D_FF), BF16),
                pltpu.VMEM((D_FF, D_MODEL), BF16),
            ],
        ),
        compiler_params=_params(("arbitrary",)),
        name="moe",
    )(block_e, n_used, hs, w_gate, b_gate, w_up, b_up, w_down, b_down)


SC_CORES = 2
SC_SUBCORES = 16
SC_WORKERS = SC_CORES * SC_SUBCORES
SC_ROWS = 128


def _sc_gather_rows(table, idx):
    n_idx = idx.shape[0]
    width = table.shape[1]
    per_worker = n_idx // SC_WORKERS
    n_chunks = per_worker // SC_ROWS
    assert n_chunks * SC_ROWS * SC_WORKERS == n_idx
    mesh = plsc.VectorSubcoreMesh(core_axis_name="c", subcore_axis_name="s")

    @functools.partial(
        pl.kernel, mesh=mesh,
        out_type=jax.ShapeDtypeStruct((n_idx, width), table.dtype),
        scratch_types=[pltpu.VMEM((SC_ROWS,), I32), pltpu.VMEM((SC_ROWS, width), table.dtype),
                       pltpu.SemaphoreType.DMA],
        name="sc_gather",
    )
    def gather(table_hbm, idx_hbm, out_hbm, idx_v, rows_v, sem):
        worker = lax.axis_index("s") * SC_CORES + lax.axis_index("c")
        base = worker * per_worker

        @pl.loop(0, n_chunks)
        def _(ch):
            off = pl.multiple_of(base + ch * SC_ROWS, SC_ROWS)
            pltpu.sync_copy(idx_hbm.at[pl.ds(off, SC_ROWS)], idx_v)
            pltpu.async_copy(table_hbm.at[idx_v], rows_v, sem).wait()
            pltpu.sync_copy(rows_v, out_hbm.at[pl.ds(off, SC_ROWS)])

    return gather(table, idx)


def _sc_scatter_rows(rows, idx3, n_out):
    n_rows, width = rows.shape
    n_chunks = n_rows // SC_ROWS // SC_WORKERS
    assert n_chunks * SC_ROWS * SC_WORKERS == n_rows and idx3.shape == (n_rows // SC_ROWS, TOP_K, SC_ROWS)
    mesh = plsc.VectorSubcoreMesh(core_axis_name="c", subcore_axis_name="s")

    @functools.partial(
        pl.kernel, mesh=mesh,
        out_type=jax.ShapeDtypeStruct((n_out, width), rows.dtype),
        scratch_types=[pltpu.VMEM((TOP_K, SC_ROWS), I32), pltpu.VMEM((SC_ROWS, width), rows.dtype),
                       pltpu.SemaphoreType.DMA],
        name="sc_scatter",
    )
    def scatter(rows_hbm, idx_hbm, out_hbm, idx_v, rows_v, sem):
        worker = lax.axis_index("s") * SC_CORES + lax.axis_index("c")

        @pl.loop(0, n_chunks)
        def _(ch):
            chunk = worker * n_chunks + ch
            pltpu.sync_copy(idx_hbm.at[chunk], idx_v)
            pltpu.sync_copy(rows_hbm.at[pl.ds(pl.multiple_of(chunk * SC_ROWS, SC_ROWS), SC_ROWS)],
                            rows_v)
            for kk in range(TOP_K):
                pltpu.async_copy(rows_v, out_hbm.at[idx_v.at[kk]], sem).wait()

    return scatter(rows, idx3)


def _combine_kernel(x1_ref, mod_ref, tw_ref, fg_ref, g_ref, out_ref):
    tw = tw_ref[...]
    f_lo = jnp.zeros((TOK_TILE, HALF_W), F32)
    f_hi = jnp.zeros((TOK_TILE, HALF_W), F32)
    for kk in range(TOP_K):
        lo, hi = _unpack_halves(g_ref[kk])
        f_lo = f_lo + lo * tw[:, kk:kk + 1]
        f_hi = f_hi + hi * tw[:, kk:kk + 1]
    gate2 = mod_ref[0][:, 5 * D_MODEL:6 * D_MODEL]
    x2 = x1_ref[...] + gate2 * jnp.concatenate([f_lo, f_hi], axis=-1)
    out_ref[...] = _rms(x2) * fg_ref[...]


def _combine_call(x1, mod3, tw, final_g, gathered):
    return pl.pallas_call(
        _combine_kernel,
        out_shape=jax.ShapeDtypeStruct((N_TOK, D_MODEL), F32),
        grid=(N_TILES,),
        in_specs=[
            pl.BlockSpec((TOK_TILE, D_MODEL), lambda t: (t, 0)),
            pl.BlockSpec((1, 1, N_MOD * D_MODEL), lambda t: (_mod_row(t), 0, 0)),
            pl.BlockSpec((TOK_TILE, LANES), lambda t: (t, 0)),
            pl.BlockSpec((1, D_MODEL), lambda t: (0, 0)),
            pl.BlockSpec((TOP_K, TOK_TILE, HALF_W), lambda t: (0, t, 0)),
        ],
        out_specs=pl.BlockSpec((TOK_TILE, D_MODEL), lambda t: (t, 0)),
        compiler_params=_params(("arbitrary",)),
        name="combine",
    )(x1, mod3, tw, final_g, gathered)


def kernel(x_prompt, x_sample, state_gla_fwd, state_gla_bwd, c, c_ctx, norm1_g, w_mod, b_mod, w_in,
           w_alpha, b_alpha, gla_norm_g, w_pool_grp, pool_scale, w_branch_gla, w_branch_pool, w_out,
           norm2_g, w_router, b_router, w_gate, b_gate, w_up, b_up, w_down, b_down, final_norm_g):
    l = 0
    x_ctx = x_prompt.reshape(N_CTX, D_MODEL)
    x_lat = x_sample.reshape(N_LAT, D_MODEL)

    cvec = jnp.concatenate([c_ctx[None, :], c, jnp.zeros((8 - 1 - DEC_BATCH, D_MODEL), F32)], axis=0)
    mod = _mod_call(cvec, w_mod[l], b_mod[l][None, :])
    mod3 = mod.reshape(8, 1, N_MOD * D_MODEL)

    w_in_b = w_in[l].astype(BF16)
    w_main = w_in_b[:, :MAIN_W]
    w_alr = w_in_b[:, MAIN_W:MAIN_W + ALR_W]
    w_xp = w_in_b[:, MAIN_W + ALR_W:MAIN_W + ALR_W + POOL_W]
    w_mg = w_in_b[:, MAIN_W + ALR_W + POOL_W:]
    q, k, v, og, alr, xp, mg = _inproj_call(x_ctx, x_lat, mod3, norm1_g[l][None, :],
                                            w_main, w_alr, w_xp, w_mg)

    zpad = jnp.zeros((GLA_LOWRANK, QK_W), F32)
    wa_f = jnp.concatenate([w_alpha[l, 0], zpad], axis=0)
    wa_b = jnp.concatenate([zpad, w_alpha[l, 1]], axis=0)
    o_f, o_b, s_f, s_b = _gla_call(q, k, v, alr, wa_f, b_alpha[l, 0][None, :], wa_b,
                                   b_alpha[l, 1][None, :], state_gla_fwd[:, l], state_gla_bwd[:, l])

    pooled_c = _pool_ctx_call(xp)
    pooled_l = _pool_lat_call(xp)

    w_router_pad = jnp.pad(w_router[l], ((0, 0), (0, LANES - N_EXPERTS)))
    b_router_pad = jnp.pad(b_router[l], (0, LANES - N_EXPERTS))[None, :]
    x1, h2, top_idx, top_w = _post_call(
        x_ctx, x_lat, mod3, o_f, o_b, og, pooled_c, pooled_l, mg,
        gla_norm_g[l].reshape(1, V_W), w_pool_grp[l].astype(BF16), pool_scale[l][None, :],
        w_branch_gla[l].astype(BF16), w_branch_pool[l].astype(BF16), w_out[l].astype(BF16),
        norm2_g[l][None, :], w_router_pad, b_router_pad)

    rank, cnt = _route_call(top_idx)
    counts = cnt[0, :N_EXPERTS].astype(I32)
    padded = (counts + MOE_BLOCK - 1) // MOE_BLOCK * MOE_BLOCK
    pad_end = jnp.cumsum(padded).astype(I32)
    pad_start = pad_end - padded
    block_first = jnp.arange(N_SLOT_BLOCKS, dtype=I32) * MOE_BLOCK
    block_e = jnp.minimum(jnp.sum((pad_end[None, :] <= block_first[:, None]).astype(I32), axis=1),
                          N_EXPERTS - 1).astype(I32)
    n_used = (pad_end[-1:] // MOE_BLOCK).astype(I32)
    experts = jnp.arange(N_EXPERTS, dtype=I32)
    tk = top_idx[:, :TOP_K]
    pos = jnp.sum(jnp.where(tk[:, :, None] == experts, pad_start, 0), axis=-1) + rank[:, :TOP_K]
    pos = pos.astype(I32)
    pos_by_choice = pos.T
    pos_chunks = pos_by_choice.reshape(TOP_K, N_TOK // SC_ROWS, SC_ROWS).transpose(1, 0, 2)

    hs = _sc_scatter_rows(h2, pos_chunks, N_SLOTS)
    y = _moe_call(block_e, n_used, hs,
                  w_gate[l], b_gate[l][:, None, :], w_up[l], b_up[l][:, None, :],
                  w_down[l], b_down[l][:, None, :])
    gathered = _sc_gather_rows(y, pos_by_choice.reshape(TOP_K * N_TOK))
    out = _combine_call(x1, mod3, top_w, final_norm_g[None, :],
                        gathered.reshape(TOP_K, N_TOK, HALF_W))

    y_prompt = out[:N_CTX].reshape(BATCH, SEQ, D_MODEL)
    y_sample = out[N_CTX:].reshape(DEC_BATCH, DEC_SEQ, D_MODEL)
    return (y_prompt, y_sample, s_f[:, None], s_b[:, None])
```

```python
import functools

import jax
import jax.numpy as jnp
from jax import lax
from jax.experimental import pallas as pl
from jax.experimental.pallas import tpu as pltpu
from jax.experimental.pallas import tpu_sc as plsc

F32 = jnp.float32
BF16 = jnp.bfloat16
I32 = jnp.int32

D_MODEL = 1024
BATCH = 32
SEQ = 256
DEC_BATCH = 4
DEC_SEQ = 2048
GRID_W = 64
GLA_HEADS = 4
GLA_DK = 128
GLA_DV = 256
GLA_LOWRANK = 16
GLA_TAU = 16.0
GLA_CHUNK = 64
POOL_GROUPS = 4
POOL_GROUP_DIM = 128
POOL_WINDOWS = (2, 4, 8, 16)
N_EXPERTS = 32
TOP_K = 4
D_FF = 1024
SWIGLU_LIMIT = 7.0
SWIGLU_ALPHA = 1.702
MOE_BLOCK = 256
NORM_EPS = 1e-6
N_MOD = 6

QK_W = GLA_HEADS * GLA_DK
V_W = GLA_HEADS * GLA_DV
POOL_W = POOL_GROUPS * POOL_GROUP_DIM
MAIN_W = 2 * QK_W + 2 * V_W
ALR_W = 2 * GLA_LOWRANK
MG_W = 2 * D_MODEL

N_CTX = BATCH * SEQ
N_LAT = DEC_BATCH * DEC_SEQ
N_TOK = N_CTX + N_LAT
N_SLOT_BLOCKS = -(-(N_TOK * TOP_K + N_EXPERTS * (MOE_BLOCK - 1)) // MOE_BLOCK)
N_SLOTS = N_SLOT_BLOCKS * MOE_BLOCK

LANES = 128
TOK_TILE = 256
N_TILES = N_TOK // TOK_TILE
CTX_TILES = N_CTX // TOK_TILE
LAT_TILES_PER_SEQ = DEC_SEQ // TOK_TILE
ROUTE_TILE = 512
VMEM_LIMIT = 56 * 1024 * 1024

GLA_SEQS = 4
CTX_CHUNKS = SEQ // GLA_CHUNK
LAT_CHUNKS = DEC_SEQ // GLA_CHUNK
CHUNKS_PER_TILE = TOK_TILE // GLA_CHUNK
GLA_CTX_STEPS = (BATCH // GLA_SEQS) * CTX_CHUNKS
TILE_GRID = 8

NT_DIMS = (((1,), (1,)), ((), ()))
TN_DIMS = (((0,), (0,)), ((), ()))

assert DEC_BATCH == GLA_SEQS and SEQ == TOK_TILE and N_TILES == TILE_GRID * TILE_GRID


def _params(semantics, vmem=VMEM_LIMIT):
    return pltpu.CompilerParams(dimension_semantics=semantics, vmem_limit_bytes=vmem)


def _split_bf16(a):
    hi = a.astype(BF16)
    lo = (a - hi.astype(F32)).astype(BF16)
    return hi, lo


def _dot(a, b):
    return jnp.dot(a, b, preferred_element_type=F32)


def _dot3(a, b):
    a_hi, a_lo = _split_bf16(a)
    b_hi, b_lo = _split_bf16(b)
    return _dot(a_hi, b_hi) + _dot(a_lo, b_hi) + _dot(a_hi, b_lo)


def _sigmoid(x):
    return 1.0 / (1.0 + jnp.exp(-x))


HALF_W = D_MODEL // 2
HIGH_HALF_MASK = -65536


def _pack_halves(lo, hi):
    lo_bits = pltpu.bitcast(lo.astype(BF16).astype(F32), I32)
    hi_bits = pltpu.bitcast(hi.astype(BF16).astype(F32), I32)
    return lax.shift_right_logical(lo_bits, 16) | (hi_bits & HIGH_HALF_MASK)


def _unpack_halves(words):
    lo = pltpu.bitcast(lax.shift_left(words, 16), F32)
    hi = pltpu.bitcast(words & HIGH_HALF_MASK, F32)
    return lo, hi


def _rms(x):
    return x * lax.rsqrt(jnp.mean(x * x, axis=-1, keepdims=True) + NORM_EPS)


def _mod_row(t):
    return jnp.where(t < CTX_TILES, 0, 1 + (t - CTX_TILES) // LAT_TILES_PER_SEQ)


def _store_tile(t):
    u = t - CTX_TILES
    return jnp.where(t < CTX_TILES, t,
                     CTX_TILES + DEC_BATCH * (u % LAT_TILES_PER_SEQ) + u // LAT_TILES_PER_SEQ)


def _ctx_tile(t):
    return jnp.minimum(t, CTX_TILES - 1)


def _lat_tile(t):
    return jnp.maximum(t - CTX_TILES, 0)


def _mod_kernel(c_ref, w_ref, b_ref, o_ref):
    c = c_ref[...]
    o_ref[...] = _dot3(c * _sigmoid(c), w_ref[...]) + b_ref[...]


def _mod_call(cvec, w_mod, b_mod):
    rows = cvec.shape[0]
    return pl.pallas_call(
        _mod_kernel,
        out_shape=jax.ShapeDtypeStruct((rows, N_MOD * D_MODEL), F32),
        grid=(N_MOD,),
        in_specs=[
            pl.BlockSpec((rows, D_MODEL), lambda j: (0, 0)),
            pl.BlockSpec((D_MODEL, D_MODEL), lambda j: (0, j)),
            pl.BlockSpec((1, D_MODEL), lambda j: (0, j)),
        ],
        out_specs=pl.BlockSpec((rows, D_MODEL), lambda j: (0, j)),
        compiler_params=_params(("arbitrary",)),
        name="mod",
    )(cvec, w_mod, b_mod)


def _inproj_kernel(xc_ref, xl_ref, mod_ref, g_ref, wmain_ref, walr_ref, wxp_ref, wmg_ref,
                   q_ref, k_ref, v_ref, og_ref, alr_ref, xp_ref, mg_ref):
    t = pl.program_id(0)
    x = jnp.where(t < CTX_TILES, xc_ref[...], xl_ref[...])
    mod = mod_ref[0]
    shift1 = mod[:, 0:D_MODEL]
    scale1 = mod[:, D_MODEL:2 * D_MODEL]
    h = _rms(x) * g_ref[...]
    h = (h * (1.0 + scale1) + shift1).astype(BF16)
    z = _dot(h, wmain_ref[...])
    q_ref[...] = (z[:, 0:QK_W] * (GLA_DK ** -0.5)).astype(BF16)
    k_ref[...] = z[:, QK_W:2 * QK_W].astype(BF16)
    v_ref[...] = z[:, 2 * QK_W:2 * QK_W + V_W].astype(BF16)
    og_ref[...] = z[:, 2 * QK_W + V_W:MAIN_W].astype(BF16)
    alr_ref[...] = _dot(h, walr_ref[...])
    xp_ref[...] = _dot(h, wxp_ref[...])
    mg_ref[...] = _dot(h, wmg_ref[...]).astype(BF16)


def _inproj_call(x_ctx, x_lat, mod3, norm1_g, w_main, w_alr, w_xp, w_mg):
    const = lambda t: (0, 0)
    stored = lambda t: (_store_tile(t), 0)
    widths = (QK_W, QK_W, V_W, V_W, ALR_W, POOL_W, MG_W)
    dtypes = (BF16, BF16, BF16, BF16, F32, F32, BF16)
    return pl.pallas_call(
        _inproj_kernel,
        out_shape=[jax.ShapeDtypeStruct((N_TOK, w), dt) for w, dt in zip(widths, dtypes)],
        grid=(N_TILES,),
        in_specs=[
            pl.BlockSpec((TOK_TILE, D_MODEL), lambda t: (_ctx_tile(t), 0)),
            pl.BlockSpec((TOK_TILE, D_MODEL), lambda t: (_lat_tile(t), 0)),
            pl.BlockSpec((1, 1, N_MOD * D_MODEL), lambda t: (_mod_row(t), 0, 0)),
            pl.BlockSpec((1, D_MODEL), const),
            pl.BlockSpec((D_MODEL, MAIN_W), const),
            pl.BlockSpec((D_MODEL, ALR_W), const),
            pl.BlockSpec((D_MODEL, POOL_W), const),
            pl.BlockSpec((D_MODEL, MG_W), const),
        ],
        out_specs=[pl.BlockSpec((TOK_TILE, w), stored) for w in widths],
        compiler_params=_params(("arbitrary",)),
        name="inproj",
    )(x_ctx, x_lat, mod3, norm1_g, w_main, w_alr, w_xp, w_mg)


def _gla_direction(q_ref, k_ref, v_ref, alr_ref, wa_ref, ba_ref, o_ref, st_ref, slot0, rev):
    rows = GLA_SEQS * GLA_CHUNK
    alr = jnp.concatenate([alr_ref[0, s] for s in range(GLA_SEQS)], axis=0)
    a = _dot3(alr, wa_ref[...]) + ba_ref[...]
    g = (jnp.minimum(a, 0.0) - jnp.log(1.0 + jnp.exp(-jnp.abs(a)))) * (1.0 / GLA_TAU)

    row = lax.broadcasted_iota(I32, (rows, rows), 0)
    col = lax.broadcasted_iota(I32, (rows, rows), 1)
    same = (row // GLA_CHUNK) == (col // GLA_CHUNK)
    tri_all = same & ((col >= row) if rev else (col <= row))
    tri_b = jnp.where(tri_all, 1.0, 0.0).astype(BF16)
    g_hi, g_lo = _split_bf16(g)
    bcum_all = _dot(tri_b, g_hi) + _dot(tri_b, g_lo)

    r64 = lax.broadcasted_iota(I32, (GLA_CHUNK, GLA_CHUNK), 0)
    c64 = lax.broadcasted_iota(I32, (GLA_CHUNK, GLA_CHUNK), 1)
    tri = (c64 >= r64) if rev else (c64 <= r64)

    for s in range(GLA_SEQS):
        bcum = bcum_all[s * GLA_CHUNK:(s + 1) * GLA_CHUNK]
        blast = bcum[0:1] if rev else bcum[GLA_CHUNK - 1:GLA_CHUNK]
        bmid = bcum[GLA_CHUNK // 2:GLA_CHUNK // 2 + 1]
        e_q = jnp.exp(bcum - bmid)
        e_k = jnp.exp(bmid - bcum)
        e_in = jnp.exp(bcum)
        e_out = jnp.exp(blast - bcum)
        e_last = jnp.exp(blast)
        q = q_ref[0, s].astype(F32)
        k = k_ref[0, s].astype(F32)
        for h in range(GLA_HEADS):
            ks = slice(h * GLA_DK, (h + 1) * GLA_DK)
            vs = slice(h * GLA_DV, (h + 1) * GLA_DV)
            qh = q[:, ks]
            kh = k[:, ks]
            vh = v_ref[0, s, :, vs]
            att = lax.dot_general((qh * e_q[:, ks]).astype(BF16), (kh * e_k[:, ks]).astype(BF16),
                                  NT_DIMS, preferred_element_type=F32)
            att = jnp.where(tri, att, 0.0).astype(BF16)
            st = st_ref[slot0 + s, h]
            o_inter = lax.dot_general((qh * e_in[:, ks]).astype(BF16), st.astype(BF16),
                                      NT_DIMS, preferred_element_type=F32)
            o_ref[0, s, :, vs] = o_inter + _dot(att, vh)
            upd = lax.dot_general(vh, (kh * e_out[:, ks]).astype(BF16), TN_DIMS,
                                  preferred_element_type=F32)
            st_ref[slot0 + s, h] = st * e_last[:, ks] + upd


def _gla_kernel(qf_ref, kf_ref, vf_ref, af_ref, qb_ref, kb_ref, vb_ref, ab_ref,
                waf_ref, baf_ref, wab_ref, bab_ref, s0f_ref, s0b_ref,
                of_ref, ob_ref, sf_ref, sb_ref, st_ref):
    i = pl.program_id(0)
    is_ctx = i < GLA_CTX_STEPS
    chunk = jnp.where(is_ctx, i % CTX_CHUNKS, i - GLA_CTX_STEPS)

    @pl.when(is_ctx & (chunk == 0))
    def _():
        st_ref[...] = jnp.zeros(st_ref.shape, F32)

    @pl.when(i == GLA_CTX_STEPS)
    def _():
        for s in range(GLA_SEQS):
            for h in range(GLA_HEADS):
                st_ref[s, h] = s0f_ref[s, h].T
                st_ref[GLA_SEQS + s, h] = s0b_ref[s, h].T

    _gla_direction(qf_ref, kf_ref, vf_ref, af_ref, waf_ref, baf_ref, of_ref, st_ref, 0, False)
    _gla_direction(qb_ref, kb_ref, vb_ref, ab_ref, wab_ref, bab_ref, ob_ref, st_ref, GLA_SEQS, True)

    @pl.when(is_ctx & (chunk == CTX_CHUNKS - 1))
    def _():
        for s in range(GLA_SEQS):
            for h in range(GLA_HEADS):
                sf_ref[s, h] = st_ref[s, h].T
                sb_ref[s, h] = st_ref[GLA_SEQS + s, h].T


def _gla_block(i, rev):
    is_ctx = i < GLA_CTX_STEPS
    group = i // CTX_CHUNKS
    c_ctx = i % CTX_CHUNKS
    c_lat = i - GLA_CTX_STEPS
    if rev:
        c_ctx = CTX_CHUNKS - 1 - c_ctx
        c_lat = LAT_CHUNKS - 1 - c_lat
    j = c_lat // CHUNKS_PER_TILE
    per_row = TILE_GRID // GLA_SEQS
    a = jnp.where(is_ctx, group // per_row, CTX_TILES // TILE_GRID + j // per_row)
    b = jnp.where(is_ctx, group % per_row, j % per_row)
    c = jnp.where(is_ctx, c_ctx, c_lat % CHUNKS_PER_TILE)
    return (a, b, c, 0)


def _gla_call(q, k, v, alr, wa_f, ba_f, wa_b, ba_b, s0_f, s0_b):
    def view(arr):
        return arr.reshape(TILE_GRID, TILE_GRID, TOK_TILE, arr.shape[-1])

    def spec(width, rev):
        return pl.BlockSpec((1, GLA_SEQS, GLA_CHUNK, width), lambda i: _gla_block(i, rev))

    const = lambda i: (0, 0)
    st_block = (GLA_SEQS, GLA_HEADS, GLA_DK, GLA_DV)
    whole_state = pl.BlockSpec(st_block, lambda i: (0, 0, 0, 0))
    ctx_state = pl.BlockSpec(
        st_block, lambda i: (jnp.minimum(i // CTX_CHUNKS, BATCH // GLA_SEQS - 1), 0, 0, 0))
    in_specs = []
    for rev in (False, True):
        in_specs += [spec(QK_W, rev), spec(QK_W, rev), spec(V_W, rev), spec(ALR_W, rev)]
    in_specs += [pl.BlockSpec((ALR_W, QK_W), const), pl.BlockSpec((1, QK_W), const)] * 2
    in_specs += [whole_state, whole_state]
    o_shape = jax.ShapeDtypeStruct((TILE_GRID, TILE_GRID, TOK_TILE, V_W), F32)
    s_shape = jax.ShapeDtypeStruct((BATCH, GLA_HEADS, GLA_DK, GLA_DV), F32)
    qv, kv, vv, av = view(q), view(k), view(v), view(alr)
    o_f, o_b, s_f, s_b = pl.pallas_call(
        _gla_kernel,
        out_shape=[o_shape, o_shape, s_shape, s_shape],
        grid=(GLA_CTX_STEPS + LAT_CHUNKS,),
        in_specs=in_specs,
        out_specs=[spec(V_W, False), spec(V_W, True), ctx_state, ctx_state],
        scratch_shapes=[pltpu.VMEM((2 * GLA_SEQS, GLA_HEADS, GLA_DV, GLA_DK), F32)],
        compiler_params=_params(("arbitrary",)),
        name="gla",
    )(qv, kv, vv, av, qv, kv, vv, av, wa_f, ba_f, wa_b, ba_b, s0_f, s0_b)
    return o_f.reshape(N_TOK, V_W), o_b.reshape(N_TOK, V_W), s_f, s_b


def _band(n, w, block):
    row = lax.broadcasted_iota(I32, (n, n), 0)
    col = lax.broadcasted_iota(I32, (n, n), 1)
    inside = (col >= row - w // 2) & (col <= row + w // 2 - 1)
    if block < n:
        inside = inside & ((row // block) == (col // block))
    return jnp.where(inside, 1.0, 0.0).astype(BF16)


def _win_count(p, n, w):
    return jnp.minimum(p + w // 2 - 1, n - 1) - jnp.maximum(p - w // 2, 0) + 1


def _pool_ctx_kernel(x_ref, o_ref):
    p = lax.broadcasted_iota(I32, (SEQ, POOL_GROUP_DIM), 0)
    for gi, w in enumerate(POOL_WINDOWS):
        cs = slice(gi * POOL_GROUP_DIM, (gi + 1) * POOL_GROUP_DIM)
        x = x_ref[:, cs]
        hi, lo = _split_bf16(x)
        band = _band(SEQ, w, SEQ)
        s = _dot(band, hi) + _dot(band, lo)
        cnt = _win_count(p, SEQ, w).astype(F32)
        o_ref[:, cs] = s / cnt - x


def _pool_ctx_call(xp):
    spec = pl.BlockSpec((SEQ, POOL_W), lambda b: (b, 0))
    return pl.pallas_call(
        _pool_ctx_kernel,
        out_shape=jax.ShapeDtypeStruct((N_CTX, POOL_W), F32),
        grid=(BATCH,),
        in_specs=[spec],
        out_specs=spec,
        compiler_params=_params(("arbitrary",)),
        name="pool_ctx",
    )(xp)


POOL_HALO = (max(POOL_WINDOWS) // 2) * GRID_W


def _pool_lat_kernel(x_ref, o_ref, pad_ref):
    rows = DEC_SEQ // GRID_W
    p = lax.broadcasted_iota(I32, (DEC_SEQ, POOL_GROUP_DIM), 0)
    r = p // GRID_W
    cidx = p % GRID_W
    zeros = jnp.zeros((POOL_HALO, POOL_GROUP_DIM), F32)
    pad_ref[0:POOL_HALO, :] = zeros
    pad_ref[POOL_HALO + DEC_SEQ:2 * POOL_HALO + DEC_SEQ, :] = zeros
    for gi, w in enumerate(POOL_WINDOWS):
        cs = slice(gi * POOL_GROUP_DIM, (gi + 1) * POOL_GROUP_DIM)
        band = _band(TOK_TILE, w, GRID_W)
        for t in range(LAT_TILES_PER_SEQ):
            hi, lo = _split_bf16(x_ref[t, 0, :, cs])
            pad_ref[POOL_HALO + t * TOK_TILE:POOL_HALO + (t + 1) * TOK_TILE, :] = (
                _dot(band, hi) + _dot(band, lo))
        acc = jnp.zeros((DEC_SEQ, POOL_GROUP_DIM), F32)
        for dr in range(-(w // 2), w // 2):
            start = POOL_HALO + dr * GRID_W
            acc = acc + pad_ref[start:start + DEC_SEQ, :]
        cnt = (_win_count(r, rows, w) * _win_count(cidx, GRID_W, w)).astype(F32)
        pooled = acc / cnt
        for t in range(LAT_TILES_PER_SEQ):
            rs = slice(t * TOK_TILE, (t + 1) * TOK_TILE)
            o_ref[t, 0, :, cs] = pooled[rs] - x_ref[t, 0, :, cs]


def _pool_lat_call(xp):
    view = xp.reshape(N_TILES // DEC_BATCH, DEC_BATCH, TOK_TILE, POOL_W)
    blk = (LAT_TILES_PER_SEQ, 1, TOK_TILE, POOL_W)
    out = pl.pallas_call(
        _pool_lat_kernel,
        out_shape=jax.ShapeDtypeStruct((LAT_TILES_PER_SEQ, DEC_BATCH, TOK_TILE, POOL_W), F32),
        grid=(DEC_BATCH,),
        in_specs=[pl.BlockSpec(blk, lambda s: (CTX_TILES // DEC_BATCH // LAT_TILES_PER_SEQ, s, 0, 0))],
        out_specs=pl.BlockSpec(blk, lambda s: (0, s, 0, 0)),
        scratch_shapes=[pltpu.VMEM((DEC_SEQ + 2 * POOL_HALO, POOL_GROUP_DIM), F32)],
        compiler_params=_params(("arbitrary",)),
        name="pool_lat",
    )(view)
    return out.reshape(N_LAT, POOL_W)


def _post_kernel(xc_ref, xl_ref, mod_ref, of_ref, ob_ref, og_ref, pc_ref, pl_ref, mg_ref, gng_ref,
                 wpg_ref, psc_ref, wbg_ref, wbp_ref, wout_ref, n2g_ref, wr_ref, br_ref,
                 x1_ref, h2_ref, idx_ref, tw_ref):
    t = pl.program_id(0)
    is_ctx = t < CTX_TILES
    x = jnp.where(is_ctx, xc_ref[...], xl_ref[...])
    pooled = jnp.where(is_ctx, pc_ref[...], pl_ref[...])
    mod = mod_ref[0]
    gate1 = mod[:, 2 * D_MODEL:3 * D_MODEL]
    shift2 = mod[:, 3 * D_MODEL:4 * D_MODEL]
    scale2 = mod[:, 4 * D_MODEL:5 * D_MODEL]

    o = of_ref[...] + ob_ref[...]
    og = og_ref[...].astype(F32)
    gated = []
    for h in range(GLA_HEADS):
        vs = slice(h * GLA_DV, (h + 1) * GLA_DV)
        oh = _rms(o[:, vs]) * gng_ref[:, vs]
        ogh = og[:, vs]
        gated.append((oh * (ogh * _sigmoid(ogh))).astype(BF16))
    br_gla = _dot(jnp.concatenate(gated, axis=-1), wbg_ref[...])

    pm = []
    for gi in range(POOL_GROUPS):
        cs = slice(gi * POOL_GROUP_DIM, (gi + 1) * POOL_GROUP_DIM)
        pmg = _dot(pooled[:, cs].astype(BF16), wpg_ref[gi]) * psc_ref[:, cs]
        pm.append(pmg.astype(BF16))
    br_pool = _dot(jnp.concatenate(pm, axis=-1), wbp_ref[...])

    mg = mg_ref[...].astype(F32)
    merged = _sigmoid(mg[:, 0:D_MODEL]) * br_gla + _sigmoid(mg[:, D_MODEL:MG_W]) * br_pool
    m = _dot(merged.astype(BF16), wout_ref[...])
    x1 = x + gate1 * m
    x1_ref[...] = x1
    h2 = _rms(x1) * n2g_ref[...]
    h2 = h2 * (1.0 + scale2) + shift2
    h2_ref[...] = _pack_halves(h2[:, :HALF_W], h2[:, HALF_W:])

    logits = _dot3(h2, wr_ref[...]) + br_ref[...]
    lane = lax.broadcasted_iota(I32, (TOK_TILE, LANES), 1)
    lane_f = lane.astype(F32)
    neg = jnp.float32(-jnp.inf)
    cur = jnp.where(lane < N_EXPERTS, logits, neg)
    vals, idxs = [], []
    for _ in range(TOP_K):
        mx = jnp.max(cur, axis=-1, keepdims=True)
        ix = jnp.min(jnp.where(cur == mx, lane_f, float(LANES)), axis=-1, keepdims=True)
        vals.append(mx)
        idxs.append(ix)
        cur = jnp.where(lane_f == ix, neg, cur)
    ex = [jnp.exp(vv - vals[0]) for vv in vals]
    tot = ex[0] + ex[1] + ex[2] + ex[3]
    idx_out = jnp.zeros((TOK_TILE, LANES), F32)
    w_out = jnp.zeros((TOK_TILE, LANES), F32)
    for kk in range(TOP_K):
        idx_out = jnp.where(lane == kk, idxs[kk], idx_out)
        w_out = jnp.where(lane == kk, ex[kk] / tot, w_out)
    idx_ref[...] = idx_out.astype(I32)
    tw_ref[...] = w_out


def _post_call(x_ctx, x_lat, mod3, o_f, o_b, og, pooled_c, pooled_l, mg, gng, wpg, psc, wbg, wbp,
               wout, n2g, wr, br):
    row = lambda t: (t, 0)
    const = lambda t: (0, 0)
    stored = lambda t: (_store_tile(t), 0)
    return pl.pallas_call(
        _post_kernel,
        out_shape=[
            jax.ShapeDtypeStruct((N_TOK, D_MODEL), F32),
            jax.ShapeDtypeStruct((N_TOK, HALF_W), I32),
            jax.ShapeDtypeStruct((N_TOK, LANES), I32),
            jax.ShapeDtypeStruct((N_TOK, LANES), F32),
        ],
        grid=(N_TILES,),
        in_specs=[
            pl.BlockSpec((TOK_TILE, D_MODEL), lambda t: (_ctx_tile(t), 0)),
            pl.BlockSpec((TOK_TILE, D_MODEL), lambda t: (_lat_tile(t), 0)),
            pl.BlockSpec((1, 1, N_MOD * D_MODEL), lambda t: (_mod_row(t), 0, 0)),
            pl.BlockSpec((TOK_TILE, V_W), stored),
            pl.BlockSpec((TOK_TILE, V_W), stored),
            pl.BlockSpec((TOK_TILE, V_W), stored),
            pl.BlockSpec((TOK_TILE, POOL_W), lambda t: (_ctx_tile(t), 0)),
            pl.BlockSpec((TOK_TILE, POOL_W),
                         lambda t: (jnp.maximum(_store_tile(t) - CTX_TILES, 0), 0)),
            pl.BlockSpec((TOK_TILE, MG_W), stored),
            pl.BlockSpec((1, V_W), const),
            pl.BlockSpec((POOL_GROUPS, POOL_GROUP_DIM, POOL_GROUP_DIM), lambda t: (0, 0, 0)),
            pl.BlockSpec((1, POOL_W), const),
            pl.BlockSpec((V_W, D_MODEL), const),
            pl.BlockSpec((POOL_W, D_MODEL), const),
            pl.BlockSpec((D_MODEL, D_MODEL), const),
            pl.BlockSpec((1, D_MODEL), const),
            pl.BlockSpec((D_MODEL, LANES), const),
            pl.BlockSpec((1, LANES), const),
        ],
        out_specs=[
            pl.BlockSpec((TOK_TILE, D_MODEL), row),
            pl.BlockSpec((TOK_TILE, HALF_W), row),
            pl.BlockSpec((TOK_TILE, LANES), row),
            pl.BlockSpec((TOK_TILE, LANES), row),
        ],
        compiler_params=_params(("arbitrary",)),
        name="post",
    )(x_ctx, x_lat, mod3, o_f, o_b, og, pooled_c, pooled_l, mg, gng, wpg, psc, wbg, wbp, wout, n2g,
      wr, br)


def _route_kernel(idx_ref, rank_ref, cnt_ref, carry_ref):
    t = pl.program_id(0)

    @pl.when(t == 0)
    def _():
        carry_ref[...] = jnp.zeros((1, LANES), F32)

    idx = idx_ref[...]
    lane = lax.broadcasted_iota(I32, (ROUTE_TILE, LANES), 1)
    sel = [lane == idx[:, kk:kk + 1] for kk in range(TOP_K)]
    onehot = jnp.zeros((ROUTE_TILE, LANES), F32)
    for kk in range(TOP_K):
        onehot = onehot + jnp.where(sel[kk], 1.0, 0.0)
    row = lax.broadcasted_iota(I32, (ROUTE_TILE, ROUTE_TILE), 0)
    col = lax.broadcasted_iota(I32, (ROUTE_TILE, ROUTE_TILE), 1)
    strict = jnp.where(col < row, 1.0, 0.0).astype(BF16)
    before = _dot(strict, onehot.astype(BF16)) + carry_ref[...]
    rank = jnp.zeros((ROUTE_TILE, LANES), F32)
    for kk in range(TOP_K):
        rk = jnp.sum(jnp.where(sel[kk], before, 0.0), axis=-1, keepdims=True)
        rank = jnp.where(lane == kk, rk, rank)
    rank_ref[...] = rank.astype(I32)
    carry_ref[...] = carry_ref[...] + jnp.sum(onehot, axis=0, keepdims=True)
    cnt_ref[...] = jnp.broadcast_to(carry_ref[...], (8, LANES))


def _route_call(idx):
    return pl.pallas_call(
        _route_kernel,
        out_shape=[
            jax.ShapeDtypeStruct((N_TOK, LANES), I32),
            jax.ShapeDtypeStruct((8, LANES), F32),
        ],
        grid=(N_TOK // ROUTE_TILE,),
        in_specs=[pl.BlockSpec((ROUTE_TILE, LANES), lambda t: (t, 0))],
        out_specs=[
            pl.BlockSpec((ROUTE_TILE, LANES), lambda t: (t, 0)),
            pl.BlockSpec((8, LANES), lambda t: (0, 0)),
        ],
        scratch_shapes=[pltpu.VMEM((1, LANES), F32)],
        compiler_params=_params(("arbitrary",)),
        name="route",
    )(idx)


def _moe_kernel(be_ref, nu_ref, x_ref, wg_ref, bg_ref, wu_ref, bu_ref, wd_ref, bd_ref, y_ref,
                wgu_ref, wdb_ref):
    b = pl.program_id(0)
    n_used = nu_ref[0]
    e = be_ref[b]
    prev = be_ref[jnp.maximum(b - 1, 0)]
    live = b < n_used

    @pl.when(live & ((b == 0) | (e != prev)))
    def _():
        wgu_ref[:, :D_FF] = wg_ref[0].astype(BF16)
        wgu_ref[:, D_FF:] = wu_ref[0].astype(BF16)
        wdb_ref[...] = wd_ref[0].astype(BF16)

    @pl.when(live)
    def _():
        x_lo, x_hi = _unpack_halves(x_ref[...])
        x = jnp.concatenate([x_lo.astype(BF16), x_hi.astype(BF16)], axis=-1)
        gu = _dot(x, wgu_ref[...])
        gate = jnp.minimum(gu[:, :D_FF] + bg_ref[0], SWIGLU_LIMIT)
        up = jnp.clip(gu[:, D_FF:] + bu_ref[0], -SWIGLU_LIMIT, SWIGLU_LIMIT)
        act = (up + 1.0) * (gate * _sigmoid(SWIGLU_ALPHA * gate))
        y = _dot(act.astype(BF16), wdb_ref[...]) + bd_ref[0]
        y_ref[...] = _pack_halves(y[:, :HALF_W], y[:, HALF_W:])

    @pl.when(jnp.logical_not(live))
    def _():
        y_ref[...] = jnp.zeros((MOE_BLOCK, HALF_W), I32)


def _moe_call(block_e, n_used, hs, w_gate, b_gate, w_up, b_up, w_down, b_down):
    def blk(b, be, nu):
        return jnp.minimum(b, nu[0] - 1)

    row = lambda b, be, nu: (blk(b, be, nu), 0)
    wsel = lambda b, be, nu: (be[blk(b, be, nu)], 0, 0)
    return pl.pallas_call(
        _moe_kernel,
        out_shape=jax.ShapeDtypeStruct((N_SLOTS, HALF_W), I32),
        grid_spec=pltpu.PrefetchScalarGridSpec(
            num_scalar_prefetch=2,
            grid=(N_SLOT_BLOCKS,),
            in_specs=[
                pl.BlockSpec((MOE_BLOCK, HALF_W), row),
                pl.BlockSpec((1, D_MODEL, D_FF), wsel),
                pl.BlockSpec((1, 1, D_FF), wsel),
                pl.BlockSpec((1, D_MODEL, D_FF), wsel),
                pl.BlockSpec((1, 1, D_FF), wsel),
                pl.BlockSpec((1, D_FF, D_MODEL), wsel),
                pl.BlockSpec((1, 1, D_MODEL), wsel),
            ],
            out_specs=pl.BlockSpec((MOE_BLOCK, HALF_W), lambda b, be, nu: (b, 0)),
            scratch_shapes=[
                pltpu.VMEM((D_MODEL, 2 * D_FF), BF16),
                pltpu.VMEM((D_FF, D_MODEL), BF16),
            ],
        ),
        compiler_params=_params(("arbitrary",)),
        name="moe",
    )(block_e, n_used, hs, w_gate, b_gate, w_up, b_up, w_down, b_down)


SC_CORES = 2
SC_SUBCORES = 16
SC_WORKERS = SC_CORES * SC_SUBCORES
SC_ROWS = 128


def _sc_gather_rows(table, idx):
    n_idx = idx.shape[0]
    width = table.shape[1]
    per_worker = n_idx // SC_WORKERS
    n_chunks = per_worker // SC_ROWS
    assert n_chunks * SC_ROWS * SC_WORKERS == n_idx
    mesh = plsc.VectorSubcoreMesh(core_axis_name="c", subcore_axis_name="s")

    @functools.partial(
        pl.kernel, mesh=mesh,
        out_type=jax.ShapeDtypeStruct((n_idx, width), table.dtype),
        scratch_types=[pltpu.VMEM((SC_ROWS,), I32), pltpu.VMEM((SC_ROWS, width), table.dtype),
                       pltpu.SemaphoreType.DMA],
        name="sc_gather",
    )
    def gather(table_hbm, idx_hbm, out_hbm, idx_v, rows_v, sem):
        worker = lax.axis_index("s") * SC_CORES + lax.axis_index("c")
        base = worker * per_worker

        @pl.loop(0, n_chunks)
        def _(ch):
            off = pl.multiple_of(base + ch * SC_ROWS, SC_ROWS)
            pltpu.sync_copy(idx_hbm.at[pl.ds(off, SC_ROWS)], idx_v)
            pltpu.async_copy(table_hbm.at[idx_v], rows_v, sem).wait()
            pltpu.sync_copy(rows_v, out_hbm.at[pl.ds(off, SC_ROWS)])

    return gather(table, idx)


def _sc_scatter_rows(rows, idx3, n_out):
    n_rows, width = rows.shape
    n_chunks = n_rows // SC_ROWS // SC_WORKERS
    assert n_chunks * SC_ROWS * SC_WORKERS == n_rows and idx3.shape == (n_rows // SC_ROWS, TOP_K, SC_ROWS)
    mesh = plsc.VectorSubcoreMesh(core_axis_name="c", subcore_axis_name="s")

    @functools.partial(
        pl.kernel, mesh=mesh,
        out_type=jax.ShapeDtypeStruct((n_out, width), rows.dtype),
        scratch_types=[pltpu.VMEM((TOP_K, SC_ROWS), I32), pltpu.VMEM((SC_ROWS, width), rows.dtype),
                       pltpu.SemaphoreType.DMA],
        name="sc_scatter",
    )
    def scatter(rows_hbm, idx_hbm, out_hbm, idx_v, rows_v, sem):
        worker = lax.axis_index("s") * SC_CORES + lax.axis_index("c")

        @pl.loop(0, n_chunks)
        def _(ch):
            chunk = worker * n_chunks + ch
            pltpu.sync_copy(idx_hbm.at[chunk], idx_v)
            pltpu.sync_copy(rows_hbm.at[pl.ds(pl.multiple_of(chunk * SC_ROWS, SC_ROWS), SC_ROWS)],
                            rows_v)
            for kk in range(TOP_K):
                pltpu.async_copy(rows_v, out_hbm.at[idx_v.at[kk]], sem).wait()

    return scatter(rows, idx3)


def _combine_kernel(x1_ref, mod_ref, tw_ref, fg_ref, g_ref, out_ref):
    tw = tw_ref[...]
    f_lo = jnp.zeros((TOK_TILE, HALF_W), F32)
    f_hi = jnp.zeros((TOK_TILE, HALF_W), F32)
    for kk in range(TOP_K):
        lo, hi = _unpack_halves(g_ref[kk])
        f_lo = f_lo + lo * tw[:, kk:kk + 1]
        f_hi = f_hi + hi * tw[:, kk:kk + 1]
    gate2 = mod_ref[0][:, 5 * D_MODEL:6 * D_MODEL]
    x2 = x1_ref[...] + gate2 * jnp.concatenate([f_lo, f_hi], axis=-1)
    out_ref[...] = _rms(x2) * fg_ref[...]


def _combine_call(x1, mod3, tw, final_g, gathered):
    return pl.pallas_call(
        _combine_kernel,
        out_shape=jax.ShapeDtypeStruct((N_TOK, D_MODEL), F32),
        grid=(N_TILES,),
        in_specs=[
            pl.BlockSpec((TOK_TILE, D_MODEL), lambda t: (t, 0)),
            pl.BlockSpec((1, 1, N_MOD * D_MODEL), lambda t: (_mod_row(t), 0, 0)),
            pl.BlockSpec((TOK_TILE, LANES), lambda t: (t, 0)),
            pl.BlockSpec((1, D_MODEL), lambda t: (0, 0)),
            pl.BlockSpec((TOP_K, TOK_TILE, HALF_W), lambda t: (0, t, 0)),
        ],
        out_specs=pl.BlockSpec((TOK_TILE, D_MODEL), lambda t: (t, 0)),
        compiler_params=_params(("arbitrary",)),
        name="combine",
    )(x1, mod3, tw, final_g, gathered)


def kernel(x_prompt, x_sample, state_gla_fwd, state_gla_bwd, c, c_ctx, norm1_g, w_mod, b_mod, w_in,
           w_alpha, b_alpha, gla_norm_g, w_pool_grp, pool_scale, w_branch_gla, w_branch_pool, w_out,
           norm2_g, w_router, b_router, w_gate, b_gate, w_up, b_up, w_down, b_down, final_norm_g):
    l = 0
    x_ctx = x_prompt.reshape(N_CTX, D_MODEL)
    x_lat = x_sample.reshape(N_LAT, D_MODEL)

    cvec = jnp.concatenate([c_ctx[None, :], c, jnp.zeros((8 - 1 - DEC_BATCH, D_MODEL), F32)], axis=0)
    mod = _mod_call(cvec, w_mod[l], b_mod[l][None, :])
    mod3 = mod.reshape(8, 1, N_MOD * D_MODEL)

    w_in_b = w_in[l].astype(BF16)
    w_main = w_in_b[:, :MAIN_W]
    w_alr = w_in_b[:, MAIN_W:MAIN_W + ALR_W]
    w_xp = w_in_b[:, MAIN_W + ALR_W:MAIN_W + ALR_W + POOL_W]
    w_mg = w_in_b[:, MAIN_W + ALR_W + POOL_W:]
    q, k, v, og, alr, xp, mg = _inproj_call(x_ctx, x_lat, mod3, norm1_g[l][None, :],
                                            w_main, w_alr, w_xp, w_mg)

    zpad = jnp.zeros((GLA_LOWRANK, QK_W), F32)
    wa_f = jnp.concatenate([w_alpha[l, 0], zpad], axis=0)
    wa_b = jnp.concatenate([zpad, w_alpha[l, 1]], axis=0)
    o_f, o_b, s_f, s_b = _gla_call(q, k, v, alr, wa_f, b_alpha[l, 0][None, :], wa_b,
                                   b_alpha[l, 1][None, :], state_gla_fwd[:, l], state_gla_bwd[:, l])

    pooled_c = _pool_ctx_call(xp)
    pooled_l = _pool_lat_call(xp)

    w_router_pad = jnp.pad(w_router[l], ((0, 0), (0, LANES - N_EXPERTS)))
    b_router_pad = jnp.pad(b_router[l], (0, LANES - N_EXPERTS))[None, :]
    x1, h2, top_idx, top_w = _post_call(
        x_ctx, x_lat, mod3, o_f, o_b, og, pooled_c, pooled_l, mg,
        gla_norm_g[l].reshape(1, V_W), w_pool_grp[l].astype(BF16), pool_scale[l][None, :],
        w_branch_gla[l].astype(BF16), w_branch_pool[l].astype(BF16), w_out[l].astype(BF16),
        norm2_g[l][None, :], w_router_pad, b_router_pad)

    rank, cnt = _route_call(top_idx)
    counts = cnt[0, :N_EXPERTS].astype(I32)
    padded = (counts + MOE_BLOCK - 1) // MOE_BLOCK * MOE_BLOCK
    pad_end = jnp.cumsum(padded).astype(I32)
    pad_start = pad_end - padded
    block_first = jnp.arange(N_SLOT_BLOCKS, dtype=I32) * MOE_BLOCK
    block_e = jnp.minimum(jnp.sum((pad_end[None, :] <= block_first[:, None]).astype(I32), axis=1),
                          N_EXPERTS - 1).astype(I32)
    n_used = (pad_end[-1:] // MOE_BLOCK).astype(I32)
    experts = jnp.arange(N_EXPERTS, dtype=I32)
    tk = top_idx[:, :TOP_K]
    pos = jnp.sum(jnp.where(tk[:, :, None] == experts, pad_start, 0), axis=-1) + rank[:, :TOP_K]
    pos = pos.astype(I32)
    pos_by_choice = pos.T
    pos_chunks = pos_by_choice.reshape(TOP_K, N_TOK // SC_ROWS, SC_ROWS).transpose(1, 0, 2)

    hs = _sc_scatter_rows(h2, pos_chunks, N_SLOTS)
    y = _moe_call(block_e, n_used, hs,
                  w_gate[l], b_gate[l][:, None, :], w_up[l], b_up[l][:, None, :],
                  w_down[l], b_down[l][:, None, :])
    gathered = _sc_gather_rows(y, pos_by_choice.reshape(TOP_K * N_TOK))
    out = _combine_call(x1, mod3, top_w, final_norm_g[None, :],
                        gathered.reshape(TOP_K, N_TOK, HALF_W))

    y_prompt = out[:N_CTX].reshape(BATCH, SEQ, D_MODEL)
    y_sample = out[N_CTX:].reshape(DEC_BATCH, DEC_SEQ, D_MODEL)
    return (y_prompt, y_sample, s_f[:, None], s_b[:, None])
```

```python
import functools

import jax
import jax.numpy as jnp
from jax import lax
from jax.experimental import pallas as pl
from jax.experimental.pallas import tpu as pltpu
from jax.experimental.pallas import tpu_sc as plsc

F32 = jnp.float32
BF16 = jnp.bfloat16
I32 = jnp.int32

D_MODEL = 1024
BATCH = 32
SEQ = 256
DEC_BATCH = 4
DEC_SEQ = 2048
GRID_W = 64
GLA_HEADS = 4
GLA_DK = 128
GLA_DV = 256
GLA_LOWRANK = 16
GLA_TAU = 16.0
GLA_CHUNK = 64
POOL_GROUPS = 4
POOL_GROUP_DIM = 128
POOL_WINDOWS = (2, 4, 8, 16)
N_EXPERTS = 32
TOP_K = 4
D_FF = 1024
SWIGLU_LIMIT = 7.0
SWIGLU_ALPHA = 1.702
MOE_BLOCK = 256
NORM_EPS = 1e-6
N_MOD = 6

QK_W = GLA_HEADS * GLA_DK
V_W = GLA_HEADS * GLA_DV
POOL_W = POOL_GROUPS * POOL_GROUP_DIM
MAIN_W = 2 * QK_W + 2 * V_W
ALR_W = 2 * GLA_LOWRANK
MG_W = 2 * D_MODEL

N_CTX = BATCH * SEQ
N_LAT = DEC_BATCH * DEC_SEQ
N_TOK = N_CTX + N_LAT
N_SLOT_BLOCKS = -(-(N_TOK * TOP_K + N_EXPERTS * (MOE_BLOCK - 1)) // MOE_BLOCK)
N_SLOTS = N_SLOT_BLOCKS * MOE_BLOCK

LANES = 128
TOK_TILE = 256
N_TILES = N_TOK // TOK_TILE
CTX_TILES = N_CTX // TOK_TILE
LAT_TILES_PER_SEQ = DEC_SEQ // TOK_TILE
ROUTE_TILE = 512
VMEM_LIMIT = 56 * 1024 * 1024

GLA_SEQS = 4
CTX_CHUNKS = SEQ // GLA_CHUNK
LAT_CHUNKS = DEC_SEQ // GLA_CHUNK
CHUNKS_PER_TILE = TOK_TILE // GLA_CHUNK
GLA_CTX_STEPS = (BATCH // GLA_SEQS) * CTX_CHUNKS
TILE_GRID = 8

NT_DIMS = (((1,), (1,)), ((), ()))
TN_DIMS = (((0,), (0,)), ((), ()))

assert DEC_BATCH == GLA_SEQS and SEQ == TOK_TILE and N_TILES == TILE_GRID * TILE_GRID


def _params(semantics, vmem=VMEM_LIMIT):
    return pltpu.CompilerParams(dimension_semantics=semantics, vmem_limit_bytes=vmem)


def _split_bf16(a):
    hi = a.astype(BF16)
    lo = (a - hi.astype(F32)).astype(BF16)
    return hi, lo


def _dot(a, b):
    return jnp.dot(a, b, preferred_element_type=F32)


def _dot3(a, b):
    a_hi, a_lo = _split_bf16(a)
    b_hi, b_lo = _split_bf16(b)
    return _dot(a_hi, b_hi) + _dot(a_lo, b_hi) + _dot(a_hi, b_lo)


def _sigmoid(x):
    return 1.0 / (1.0 + jnp.exp(-x))


HALF_W = D_MODEL // 2
HIGH_HALF_MASK = -65536


def _pack_halves(lo, hi):
    lo_bits = pltpu.bitcast(lo.astype(BF16).astype(F32), I32)
    hi_bits = pltpu.bitcast(hi.astype(BF16).astype(F32), I32)
    return lax.shift_right_logical(lo_bits, 16) | (hi_bits & HIGH_HALF_MASK)


def _unpack_halves(words):
    lo = pltpu.bitcast(lax.shift_left(words, 16), F32)
    hi = pltpu.bitcast(words & HIGH_HALF_MASK, F32)
    return lo, hi


def _rms(x):
    return x * lax.rsqrt(jnp.mean(x * x, axis=-1, keepdims=True) + NORM_EPS)


def _mod_row(t):
    return jnp.where(t < CTX_TILES, 0, 1 + (t - CTX_TILES) // LAT_TILES_PER_SEQ)


def _store_tile(t):
    u = t - CTX_TILES
    return jnp.where(t < CTX_TILES, t,
                     CTX_TILES + DEC_BATCH * (u % LAT_TILES_PER_SEQ) + u // LAT_TILES_PER_SEQ)


def _ctx_tile(t):
    return jnp.minimum(t, CTX_TILES - 1)


def _lat_tile(t):
    return jnp.maximum(t - CTX_TILES, 0)


def _mod_kernel(c_ref, w_ref, b_ref, o_ref):
    c = c_ref[...]
    o_ref[...] = _dot3(c * _sigmoid(c), w_ref[...]) + b_ref[...]


def _mod_call(cvec, w_mod, b_mod):
    rows = cvec.shape[0]
    return pl.pallas_call(
        _mod_kernel,
        out_shape=jax.ShapeDtypeStruct((rows, N_MOD * D_MODEL), F32),
        grid=(N_MOD,),
        in_specs=[
            pl.BlockSpec((rows, D_MODEL), lambda j: (0, 0)),
            pl.BlockSpec((D_MODEL, D_MODEL), lambda j: (0, j)),
            pl.BlockSpec((1, D_MODEL), lambda j: (0, j)),
        ],
        out_specs=pl.BlockSpec((rows, D_MODEL), lambda j: (0, j)),
        compiler_params=_params(("arbitrary",)),
        name="mod",
    )(cvec, w_mod, b_mod)


def _inproj_kernel(xc_ref, xl_ref, mod_ref, g_ref, wmain_ref, walr_ref, wxp_ref, wmg_ref,
                   q_ref, k_ref, v_ref, og_ref, alr_ref, xp_ref, mg_ref):
    t = pl.program_id(0)
    x = jnp.where(t < CTX_TILES, xc_ref[...], xl_ref[...])
    mod = mod_ref[0]
    shift1 = mod[:, 0:D_MODEL]
    scale1 = mod[:, D_MODEL:2 * D_MODEL]
    h = _rms(x) * g_ref[...]
    h = (h * (1.0 + scale1) + shift1).astype(BF16)
    z = _dot(h, wmain_ref[...])
    q_ref[...] = (z[:, 0:QK_W] * (GLA_DK ** -0.5)).astype(BF16)
    k_ref[...] = z[:, QK_W:2 * QK_W].astype(BF16)
    v_ref[...] = z[:, 2 * QK_W:2 * QK_W + V_W].astype(BF16)
    og_ref[...] = z[:, 2 * QK_W + V_W:MAIN_W].astype(BF16)
    alr_ref[...] = _dot(h, walr_ref[...])
    xp_ref[...] = _dot(h, wxp_ref[...])
    mg_ref[...] = _dot(h, wmg_ref[...]).astype(BF16)


def _inproj_call(x_ctx, x_lat, mod3, norm1_g, w_main, w_alr, w_xp, w_mg):
    const = lambda t: (0, 0)
    stored = lambda t: (_store_tile(t), 0)
    widths = (QK_W, QK_W, V_W, V_W, ALR_W, POOL_W, MG_W)
    dtypes = (BF16, BF16, BF16, BF16, F32, F32, BF16)
    return pl.pallas_call(
        _inproj_kernel,
        out_shape=[jax.ShapeDtypeStruct((N_TOK, w), dt) for w, dt in zip(widths, dtypes)],
        grid=(N_TILES,),
        in_specs=[
            pl.BlockSpec((TOK_TILE, D_MODEL), lambda t: (_ctx_tile(t), 0)),
            pl.BlockSpec((TOK_TILE, D_MODEL), lambda t: (_lat_tile(t), 0)),
            pl.BlockSpec((1, 1, N_MOD * D_MODEL), lambda t: (_mod_row(t), 0, 0)),
            pl.BlockSpec((1, D_MODEL), const),
            pl.BlockSpec((D_MODEL, MAIN_W), const),
            pl.BlockSpec((D_MODEL, ALR_W), const),
            pl.BlockSpec((D_MODEL, POOL_W), const),
            pl.BlockSpec((D_MODEL, MG_W), const),
        ],
        out_specs=[pl.BlockSpec((TOK_TILE, w), stored) for w in widths],
        compiler_params=_params(("arbitrary",)),
        name="inproj",
    )(x_ctx, x_lat, mod3, norm1_g, w_main, w_alr, w_xp, w_mg)


def _gla_direction(q_ref, k_ref, v_ref, alr_ref, wa_ref, ba_ref, o_ref, st_ref, slot0, rev):
    rows = GLA_SEQS * GLA_CHUNK
    alr = jnp.concatenate([alr_ref[0, s] for s in range(GLA_SEQS)], axis=0)
    a = _dot3(alr, wa_ref[...]) + ba_ref[...]
    g = (jnp.minimum(a, 0.0) - jnp.log(1.0 + jnp.exp(-jnp.abs(a)))) * (1.0 / GLA_TAU)

    row = lax.broadcasted_iota(I32, (rows, rows), 0)
    col = lax.broadcasted_iota(I32, (rows, rows), 1)
    same = (row // GLA_CHUNK) == (col // GLA_CHUNK)
    tri_all = same & ((col >= row) if rev else (col <= row))
    tri_b = jnp.where(tri_all, 1.0, 0.0).astype(BF16)
    g_hi, g_lo = _split_bf16(g)
    bcum_all = _dot(tri_b, g_hi) + _dot(tri_b, g_lo)

    r64 = lax.broadcasted_iota(I32, (GLA_CHUNK, GLA_CHUNK), 0)
    c64 = lax.broadcasted_iota(I32, (GLA_CHUNK, GLA_CHUNK), 1)
    tri = (c64 >= r64) if rev else (c64 <= r64)

    for s in range(GLA_SEQS):
        bcum = bcum_all[s * GLA_CHUNK:(s + 1) * GLA_CHUNK]
        blast = bcum[0:1] if rev else bcum[GLA_CHUNK - 1:GLA_CHUNK]
        bmid = bcum[GLA_CHUNK // 2:GLA_CHUNK // 2 + 1]
        e_q = jnp.exp(bcum - bmid)
        e_k = jnp.exp(bmid - bcum)
        e_in = jnp.exp(bcum)
        e_out = jnp.exp(blast - bcum)
        e_last = jnp.exp(blast)
        q = q_ref[0, s].astype(F32)
        k = k_ref[0, s].astype(F32)
        for h in range(GLA_HEADS):
            ks = slice(h * GLA_DK, (h + 1) * GLA_DK)
            vs = slice(h * GLA_DV, (h + 1) * GLA_DV)
            qh = q[:, ks]
            kh = k[:, ks]
            vh = v_ref[0, s, :, vs]
            att = lax.dot_general((qh * e_q[:, ks]).astype(BF16), (kh * e_k[:, ks]).astype(BF16),
                                  NT_DIMS, preferred_element_type=F32)
            att = jnp.where(tri, att, 0.0).astype(BF16)
            st = st_ref[slot0 + s, h]
            o_inter = lax.dot_general((qh * e_in[:, ks]).astype(BF16), st.astype(BF16),
                                      NT_DIMS, preferred_element_type=F32)
            o_ref[0, s, :, vs] = o_inter + _dot(att, vh)
            upd = lax.dot_general(vh, (kh * e_out[:, ks]).astype(BF16), TN_DIMS,
                                  preferred_element_type=F32)
            st_ref[slot0 + s, h] = st * e_last[:, ks] + upd


def _gla_kernel(qf_ref, kf_ref, vf_ref, af_ref, qb_ref, kb_ref, vb_ref, ab_ref,
                waf_ref, baf_ref, wab_ref, bab_ref, s0f_ref, s0b_ref,
                of_ref, ob_ref, sf_ref, sb_ref, st_ref):
    i = pl.program_id(0)
    is_ctx = i < GLA_CTX_STEPS
    chunk = jnp.where(is_ctx, i % CTX_CHUNKS, i - GLA_CTX_STEPS)

    @pl.when(is_ctx & (chunk == 0))
    def _():
        st_ref[...] = jnp.zeros(st_ref.shape, F32)

    @pl.when(i == GLA_CTX_STEPS)
    def _():
        for s in range(GLA_SEQS):
            for h in range(GLA_HEADS):
                st_ref[s, h] = s0f_ref[s, h].T
                st_ref[GLA_SEQS + s, h] = s0b_ref[s, h].T

    _gla_direction(qf_ref, kf_ref, vf_ref, af_ref, waf_ref, baf_ref, of_ref, st_ref, 0, False)
    _gla_direction(qb_ref, kb_ref, vb_ref, ab_ref, wab_ref, bab_ref, ob_ref, st_ref, GLA_SEQS, True)

    @pl.when(is_ctx & (chunk == CTX_CHUNKS - 1))
    def _():
        for s in range(GLA_SEQS):
            for h in range(GLA_HEADS):
                sf_ref[s, h] = st_ref[s, h].T
                sb_ref[s, h] = st_ref[GLA_SEQS + s, h].T


def _gla_block(i, rev):
    is_ctx = i < GLA_CTX_STEPS
    group = i // CTX_CHUNKS
    c_ctx = i % CTX_CHUNKS
    c_lat = i - GLA_CTX_STEPS
    if rev:
        c_ctx = CTX_CHUNKS - 1 - c_ctx
        c_lat = LAT_CHUNKS - 1 - c_lat
    j = c_lat // CHUNKS_PER_TILE
    per_row = TILE_GRID // GLA_SEQS
    a = jnp.where(is_ctx, group // per_row, CTX_TILES // TILE_GRID + j // per_row)
    b = jnp.where(is_ctx, group % per_row, j % per_row)
    c = jnp.where(is_ctx, c_ctx, c_lat % CHUNKS_PER_TILE)
    return (a, b, c, 0)


def _gla_call(q, k, v, alr, wa_f, ba_f, wa_b, ba_b, s0_f, s0_b):
    def view(arr):
        return arr.reshape(TILE_GRID, TILE_GRID, TOK_TILE, arr.shape[-1])

    def spec(width, rev):
        return pl.BlockSpec((1, GLA_SEQS, GLA_CHUNK, width), lambda i: _gla_block(i, rev))

    const = lambda i: (0, 0)
    st_block = (GLA_SEQS, GLA_HEADS, GLA_DK, GLA_DV)
    whole_state = pl.BlockSpec(st_block, lambda i: (0, 0, 0, 0))
    ctx_state = pl.BlockSpec(
        st_block, lambda i: (jnp.minimum(i // CTX_CHUNKS, BATCH // GLA_SEQS - 1), 0, 0, 0))
    in_specs = []
    for rev in (False, True):
        in_specs += [spec(QK_W, rev), spec(QK_W, rev), spec(V_W, rev), spec(ALR_W, rev)]
    in_specs += [pl.BlockSpec((ALR_W, QK_W), const), pl.BlockSpec((1, QK_W), const)] * 2
    in_specs += [whole_state, whole_state]
    o_shape = jax.ShapeDtypeStruct((TILE_GRID, TILE_GRID, TOK_TILE, V_W), F32)
    s_shape = jax.ShapeDtypeStruct((BATCH, GLA_HEADS, GLA_DK, GLA_DV), F32)
    qv, kv, vv, av = view(q), view(k), view(v), view(alr)
    o_f, o_b, s_f, s_b = pl.pallas_call(
        _gla_kernel,
        out_shape=[o_shape, o_shape, s_shape, s_shape],
        grid=(GLA_CTX_STEPS + LAT_CHUNKS,),
        in_specs=in_specs,
        out_specs=[spec(V_W, False), spec(V_W, True), ctx_state, ctx_state],
        scratch_shapes=[pltpu.VMEM((2 * GLA_SEQS, GLA_HEADS, GLA_DV, GLA_DK), F32)],
        compiler_params=_params(("arbitrary",)),
        name="gla",
    )(qv, kv, vv, av, qv, kv, vv, av, wa_f, ba_f, wa_b, ba_b, s0_f, s0_b)
    return o_f.reshape(N_TOK, V_W), o_b.reshape(N_TOK, V_W), s_f, s_b


def _band(n, w, block):
    row = lax.broadcasted_iota(I32, (n, n), 0)
    col = lax.broadcasted_iota(I32, (n, n), 1)
    inside = (col >= row - w // 2) & (col <= row + w // 2 - 1)
    if block < n:
        inside = inside & ((row // block) == (col // block))
    return jnp.where(inside, 1.0, 0.0).astype(BF16)


def _win_count(p, n, w):
    return jnp.minimum(p + w // 2 - 1, n - 1) - jnp.maximum(p - w // 2, 0) + 1


def _pool_ctx_kernel(x_ref, o_ref):
    p = lax.broadcasted_iota(I32, (SEQ, POOL_GROUP_DIM), 0)
    for gi, w in enumerate(POOL_WINDOWS):
        cs = slice(gi * POOL_GROUP_DIM, (gi + 1) * POOL_GROUP_DIM)
        x = x_ref[:, cs]
        hi, lo = _split_bf16(x)
        band = _band(SEQ, w, SEQ)
        s = _dot(band, hi) + _dot(band, lo)
        cnt = _win_count(p, SEQ, w).astype(F32)
        o_ref[:, cs] = s / cnt - x


def _pool_ctx_call(xp):
    spec = pl.BlockSpec((SEQ, POOL_W), lambda b: (b, 0))
    return pl.pallas_call(
        _pool_ctx_kernel,
        out_shape=jax.ShapeDtypeStruct((N_CTX, POOL_W), F32),
        grid=(BATCH,),
        in_specs=[spec],
        out_specs=spec,
        compiler_params=_params(("arbitrary",)),
        name="pool_ctx",
    )(xp)


POOL_HALO = (max(POOL_WINDOWS) // 2) * GRID_W


def _pool_lat_kernel(x_ref, o_ref, pad_ref):
    rows = DEC_SEQ // GRID_W
    p = lax.broadcasted_iota(I32, (DEC_SEQ, POOL_GROUP_DIM), 0)
    r = p // GRID_W
    cidx = p % GRID_W
    zeros = jnp.zeros((POOL_HALO, POOL_GROUP_DIM), F32)
    pad_ref[0:POOL_HALO, :] = zeros
    pad_ref[POOL_HALO + DEC_SEQ:2 * POOL_HALO + DEC_SEQ, :] = zeros
    for gi, w in enumerate(POOL_WINDOWS):
        cs = slice(gi * POOL_GROUP_DIM, (gi + 1) * POOL_GROUP_DIM)
        band = _band(TOK_TILE, w, GRID_W)
        for t in range(LAT_TILES_PER_SEQ):
            hi, lo = _split_bf16(x_ref[t, 0, :, cs])
            pad_ref[POOL_HALO + t * TOK_TILE:POOL_HALO + (t + 1) * TOK_TILE, :] = (
                _dot(band, hi) + _dot(band, lo))
        acc = jnp.zeros((DEC_SEQ, POOL_GROUP_DIM), F32)
        for dr in range(-(w // 2), w // 2):
            start = POOL_HALO + dr * GRID_W
            acc = acc + pad_ref[start:start + DEC_SEQ, :]
        cnt = (_win_count(r, rows, w) * _win_count(cidx, GRID_W, w)).astype(F32)
        pooled = acc / cnt
        for t in range(LAT_TILES_PER_SEQ):
            rs = slice(t * TOK_TILE, (t + 1) * TOK_TILE)
            o_ref[t, 0, :, cs] = pooled[rs] - x_ref[t, 0, :, cs]


def _pool_lat_call(xp):
    view = xp.reshape(N_TILES // DEC_BATCH, DEC_BATCH, TOK_TILE, POOL_W)
    blk = (LAT_TILES_PER_SEQ, 1, TOK_TILE, POOL_W)
    out = pl.pallas_call(
        _pool_lat_kernel,
        out_shape=jax.ShapeDtypeStruct((LAT_TILES_PER_SEQ, DEC_BATCH, TOK_TILE, POOL_W), F32),
        grid=(DEC_BATCH,),
        in_specs=[pl.BlockSpec(blk, lambda s: (CTX_TILES // DEC_BATCH // LAT_TILES_PER_SEQ, s, 0, 0))],
        out_specs=pl.BlockSpec(blk, lambda s: (0, s, 0, 0)),
        scratch_shapes=[pltpu.VMEM((DEC_SEQ + 2 * POOL_HALO, POOL_GROUP_DIM), F32)],
        compiler_params=_params(("arbitrary",)),
        name="pool_lat",
    )(view)
    return out.reshape(N_LAT, POOL_W)


def _post_kernel(xc_ref, xl_ref, mod_ref, of_ref, ob_ref, og_ref, pc_ref, pl_ref, mg_ref, gng_ref,
                 wpg_ref, psc_ref, wbg_ref, wbp_ref, wout_ref, n2g_ref, wr_ref, br_ref,
                 x1_ref, h2_ref, idx_ref, tw_ref):
    t = pl.program_id(0)
    is_ctx = t < CTX_TILES
    x = jnp.where(is_ctx, xc_ref[...], xl_ref[...])
    pooled = jnp.where(is_ctx, pc_ref[...], pl_ref[...])
    mod = mod_ref[0]
    gate1 = mod[:, 2 * D_MODEL:3 * D_MODEL]
    shift2 = mod[:, 3 * D_MODEL:4 * D_MODEL]
    scale2 = mod[:, 4 * D_MODEL:5 * D_MODEL]

    o = of_ref[...] + ob_ref[...]
    og = og_ref[...].astype(F32)
    gated = []
    for h in range(GLA_HEADS):
        vs = slice(h * GLA_DV, (h + 1) * GLA_DV)
        oh = _rms(o[:, vs]) * gng_ref[:, vs]
        ogh = og[:, vs]
        gated.append((oh * (ogh * _sigmoid(ogh))).astype(BF16))
    br_gla = _dot(jnp.concatenate(gated, axis=-1), wbg_ref[...])

    pm = []
    for gi in range(POOL_GROUPS):
        cs = slice(gi * POOL_GROUP_DIM, (gi + 1) * POOL_GROUP_DIM)
        pmg = _dot(pooled[:, cs].astype(BF16), wpg_ref[gi]) * psc_ref[:, cs]
        pm.append(pmg.astype(BF16))
    br_pool = _dot(jnp.concatenate(pm, axis=-1), wbp_ref[...])

    mg = mg_ref[...].astype(F32)
    merged = _sigmoid(mg[:, 0:D_MODEL]) * br_gla + _sigmoid(mg[:, D_MODEL:MG_W]) * br_pool
    m = _dot(merged.astype(BF16), wout_ref[...])
    x1 = x + gate1 * m
    x1_ref[...] = x1
    h2 = _rms(x1) * n2g_ref[...]
    h2 = h2 * (1.0 + scale2) + shift2
    h2_ref[...] = _pack_halves(h2[:, :HALF_W], h2[:, HALF_W:])

    logits = _dot3(h2, wr_ref[...]) + br_ref[...]
    lane = lax.broadcasted_iota(I32, (TOK_TILE, LANES), 1)
    lane_f = lane.astype(F32)
    neg = jnp.float32(-jnp.inf)
    cur = jnp.where(lane < N_EXPERTS, logits, neg)
    vals, idxs = [], []
    for _ in range(TOP_K):
        mx = jnp.max(cur, axis=-1, keepdims=True)
        ix = jnp.min(jnp.where(cur == mx, lane_f, float(LANES)), axis=-1, keepdims=True)
        vals.append(mx)
        idxs.append(ix)
        cur = jnp.where(lane_f == ix, neg, cur)
    ex = [jnp.exp(vv - vals[0]) for vv in vals]
    tot = ex[0] + ex[1] + ex[2] + ex[3]
    idx_out = jnp.zeros((TOK_TILE, LANES), F32)
    w_out = jnp.zeros((TOK_TILE, LANES), F32)
    for kk in range(TOP_K):
        idx_out = jnp.where(lane == kk, idxs[kk], idx_out)
        w_out = jnp.where(lane == kk, ex[kk] / tot, w_out)
    idx_ref[...] = idx_out.astype(I32)
    tw_ref[...] = w_out


def _post_call(x_ctx, x_lat, mod3, o_f, o_b, og, pooled_c, pooled_l, mg, gng, wpg, psc, wbg, wbp,
               wout, n2g, wr, br):
    row = lambda t: (t, 0)
    const = lambda t: (0, 0)
    stored = lambda t: (_store_tile(t), 0)
    return pl.pallas_call(
        _post_kernel,
        out_shape=[
            jax.ShapeDtypeStruct((N_TOK, D_MODEL), F32),
            jax.ShapeDtypeStruct((N_TOK, HALF_W), I32),
            jax.ShapeDtypeStruct((N_TOK, LANES), I32),
            jax.ShapeDtypeStruct((N_TOK, LANES), F32),
        ],
        grid=(N_TILES,),
        in_specs=[
            pl.BlockSpec((TOK_TILE, D_MODEL), lambda t: (_ctx_tile(t), 0)),
            pl.BlockSpec((TOK_TILE, D_MODEL), lambda t: (_lat_tile(t), 0)),
            pl.BlockSpec((1, 1, N_MOD * D_MODEL), lambda t: (_mod_row(t), 0, 0)),
            pl.BlockSpec((TOK_TILE, V_W), stored),
            pl.BlockSpec((TOK_TILE, V_W), stored),
            pl.BlockSpec((TOK_TILE, V_W), stored),
            pl.BlockSpec((TOK_TILE, POOL_W), lambda t: (_ctx_tile(t), 0)),
            pl.BlockSpec((TOK_TILE, POOL_W),
                         lambda t: (jnp.maximum(_store_tile(t) - CTX_TILES, 0), 0)),
            pl.BlockSpec((TOK_TILE, MG_W), stored),
            pl.BlockSpec((1, V_W), const),
            pl.BlockSpec((POOL_GROUPS, POOL_GROUP_DIM, POOL_GROUP_DIM), lambda t: (0, 0, 0)),
            pl.BlockSpec((1, POOL_W), const),
            pl.BlockSpec((V_W, D_MODEL), const),
            pl.BlockSpec((POOL_W, D_MODEL), const),
            pl.BlockSpec((D_MODEL, D_MODEL), const),
            pl.BlockSpec((1, D_MODEL), const),
            pl.BlockSpec((D_MODEL, LANES), const),
            pl.BlockSpec((1, LANES), const),
        ],
        out_specs=[
            pl.BlockSpec((TOK_TILE, D_MODEL), row),
            pl.BlockSpec((TOK_TILE, HALF_W), row),
            pl.BlockSpec((TOK_TILE, LANES), row),
            pl.BlockSpec((TOK_TILE, LANES), row),
        ],
        compiler_params=_params(("arbitrary",)),
        name="post",
    )(x_ctx, x_lat, mod3, o_f, o_b, og, pooled_c, pooled_l, mg, gng, wpg, psc, wbg, wbp, wout, n2g,
      wr, br)


def _route_kernel(idx_ref, rank_ref, cnt_ref, carry_ref):
    t = pl.program_id(0)

    @pl.when(t == 0)
    def _():
        carry_ref[...] = jnp.zeros((1, LANES), F32)

    idx = idx_ref[...]
    lane = lax.broadcasted_iota(I32, (ROUTE_TILE, LANES), 1)
    sel = [lane == idx[:, kk:kk + 1] for kk in range(TOP_K)]
    onehot = jnp.zeros((ROUTE_TILE, LANES), F32)
    for kk in range(TOP_K):
        onehot = onehot + jnp.where(sel[kk], 1.0, 0.0)
    row = lax.broadcasted_iota(I32, (ROUTE_TILE, ROUTE_TILE), 0)
    col = lax.broadcasted_iota(I32, (ROUTE_TILE, ROUTE_TILE), 1)
    strict = jnp.where(col < row, 1.0, 0.0).astype(BF16)
    before = _dot(strict, onehot.astype(BF16)) + carry_ref[...]
    rank = jnp.zeros((ROUTE_TILE, LANES), F32)
    for kk in range(TOP_K):
        rk = jnp.sum(jnp.where(sel[kk], before, 0.0), axis=-1, keepdims=True)
        rank = jnp.where(lane == kk, rk, rank)
    rank_ref[...] = rank.astype(I32)
    carry_ref[...] = carry_ref[...] + jnp.sum(onehot, axis=0, keepdims=True)
    cnt_ref[...] = jnp.broadcast_to(carry_ref[...], (8, LANES))


def _route_call(idx):
    return pl.pallas_call(
        _route_kernel,
        out_shape=[
            jax.ShapeDtypeStruct((N_TOK, LANES), I32),
            jax.ShapeDtypeStruct((8, LANES), F32),
        ],
        grid=(N_TOK // ROUTE_TILE,),
        in_specs=[pl.BlockSpec((ROUTE_TILE, LANES), lambda t: (t, 0))],
        out_specs=[
            pl.BlockSpec((ROUTE_TILE, LANES), lambda t: (t, 0)),
            pl.BlockSpec((8, LANES), lambda t: (0, 0)),
        ],
        scratch_shapes=[pltpu.VMEM((1, LANES), F32)],
        compiler_params=_params(("arbitrary",)),
        name="route",
    )(idx)


def _moe_kernel(be_ref, nu_ref, x_ref, wg_ref, bg_ref, wu_ref, bu_ref, wd_ref, bd_ref, y_ref,
                wgu_ref, wdb_ref):
    b = pl.program_id(0)
    n_used = nu_ref[0]
    e = be_ref[b]
    prev = be_ref[jnp.maximum(b - 1, 0)]
    live = b < n_used

    @pl.when(live & ((b == 0) | (e != prev)))
    def _():
        wgu_ref[:, :D_FF] = wg_ref[0].astype(BF16)
        wgu_ref[:, D_FF:] = wu_ref[0].astype(BF16)
        wdb_ref[...] = wd_ref[0].astype(BF16)

    @pl.when(live)
    def _():
        x_lo, x_hi = _unpack_halves(x_ref[...])
        x = jnp.concatenate([x_lo.astype(BF16), x_hi.astype(BF16)], axis=-1)
        gu = _dot(x, wgu_ref[...])
        gate = jnp.minimum(gu[:, :D_FF] + bg_ref[0], SWIGLU_LIMIT)
        up = jnp.clip(gu[:, D_FF:] + bu_ref[0], -SWIGLU_LIMIT, SWIGLU_LIMIT)
        act = (up + 1.0) * (gate * _sigmoid(SWIGLU_ALPHA * gate))
        y = _dot(act.astype(BF16), wdb_ref[...]) + bd_ref[0]
        y_ref[...] = _pack_halves(y[:, :HALF_W], y[:, HALF_W:])

    @pl.when(jnp.logical_not(live))
    def _():
        y_ref[...] = jnp.zeros((MOE_BLOCK, HALF_W), I32)


def _moe_call(block_e, n_used, hs, w_gate, b_gate, w_up, b_up, w_down, b_down):
    def blk(b, be, nu):
        return jnp.minimum(b, nu[0] - 1)

    row = lambda b, be, nu: (blk(b, be, nu), 0)
    wsel = lambda b, be, nu: (be[blk(b, be, nu)], 0, 0)
    return pl.pallas_call(
        _moe_kernel,
        out_shape=jax.ShapeDtypeStruct((N_SLOTS, HALF_W), I32),
        grid_spec=pltpu.PrefetchScalarGridSpec(
            num_scalar_prefetch=2,
            grid=(N_SLOT_BLOCKS,),
            in_specs=[
                pl.BlockSpec((MOE_BLOCK, HALF_W), row),
                pl.BlockSpec((1, D_MODEL, D_FF), wsel),
                pl.BlockSpec((1, 1, D_FF), wsel),
                pl.BlockSpec((1, D_MODEL, D_FF), wsel),
                pl.BlockSpec((1, 1, D_FF), wsel),
                pl.BlockSpec((1, D_FF, D_MODEL), wsel),
                pl.BlockSpec((1, 1, D_MODEL), wsel),
            ],
            out_specs=pl.BlockSpec((MOE_BLOCK, HALF_W), lambda b, be, nu: (b, 0)),
            scratch_shapes=[
                pltpu.VMEM((D_MODEL, 2 * D_FF), BF16),
                pltpu.VMEM((D_FF, D_MODEL), BF16),
            ],
        ),
        compiler_params=_params(("arbitrary",)),
        name="moe",
    )(block_e, n_used, hs, w_gate, b_gate, w_up, b_up, w_down, b_down)


SC_CORES = 2
SC_SUBCORES = 16
SC_WORKERS = SC_CORES * SC_SUBCORES
SC_ROWS = 128


def _sc_gather_rows(table, idx):
    n_idx = idx.shape[0]
    width = table.shape[1]
    per_worker = n_idx // SC_WORKERS
    n_chunks = per_worker // SC_ROWS
    assert n_chunks * SC_ROWS * SC_WORKERS == n_idx
    mesh = plsc.VectorSubcoreMesh(core_axis_name="c", subcore_axis_name="s")

    @functools.partial(
        pl.kernel, mesh=mesh,
        out_type=jax.ShapeDtypeStruct((n_idx, width), table.dtype),
        scratch_types=[pltpu.VMEM((SC_ROWS,), I32), pltpu.VMEM((SC_ROWS, width), table.dtype),
                       pltpu.SemaphoreType.DMA],
        name="sc_gather",
    )
    def gather(table_hbm, idx_hbm, out_hbm, idx_v, rows_v, sem):
        worker = lax.axis_index("s") * SC_CORES + lax.axis_index("c")
        base = worker * per_worker

        @pl.loop(0, n_chunks)
        def _(ch):
            off = pl.multiple_of(base + ch * SC_ROWS, SC_ROWS)
            pltpu.sync_copy(idx_hbm.at[pl.ds(off, SC_ROWS)], idx_v)
            pltpu.async_copy(table_hbm.at[idx_v], rows_v, sem).wait()
            pltpu.sync_copy(rows_v, out_hbm.at[pl.ds(off, SC_ROWS)])

    return gather(table, idx)


def _sc_scatter_rows(rows, idx3, n_out):
    n_rows, width = rows.shape
    n_chunks = n_rows // SC_ROWS // SC_WORKERS
    assert n_chunks * SC_ROWS * SC_WORKERS == n_rows and idx3.shape == (n_rows // SC_ROWS, TOP_K, SC_ROWS)
    mesh = plsc.VectorSubcoreMesh(core_axis_name="c", subcore_axis_name="s")

    @functools.partial(
        pl.kernel, mesh=mesh,
        out_type=jax.ShapeDtypeStruct((n_out, width), rows.dtype),
        scratch_types=[pltpu.VMEM((TOP_K, SC_ROWS), I32), pltpu.VMEM((SC_ROWS, width), rows.dtype),
                       pltpu.SemaphoreType.DMA],
        name="sc_scatter",
    )
    def scatter(rows_hbm, idx_hbm, out_hbm, idx_v, rows_v, sem):
        worker = lax.axis_index("s") * SC_CORES + lax.axis_index("c")

        @pl.loop(0, n_chunks)
        def _(ch):
            chunk = worker * n_chunks + ch
            pltpu.sync_copy(idx_hbm.at[chunk], idx_v)
            pltpu.sync_copy(rows_hbm.at[pl.ds(pl.multiple_of(chunk * SC_ROWS, SC_ROWS), SC_ROWS)],
                            rows_v)
            for kk in range(TOP_K):
                pltpu.async_copy(rows_v, out_hbm.at[idx_v.at[kk]], sem).wait()

    return scatter(rows, idx3)


def _combine_kernel(x1_ref, mod_ref, tw_ref, fg_ref, g_ref, out_ref):
    tw = tw_ref[...]
    f_lo = jnp.zeros((TOK_TILE, HALF_W), F32)
    f_hi = jnp.zeros((TOK_TILE, HALF_W), F32)
    for kk in range(TOP_K):
        lo, hi = _unpack_halves(g_ref[kk])
        f_lo = f_lo + lo * tw[:, kk:kk + 1]
        f_hi = f_hi + hi * tw[:, kk:kk + 1]
    gate2 = mod_ref[0][:, 5 * D_MODEL:6 * D_MODEL]
    x2 = x1_ref[...] + gate2 * jnp.concatenate([f_lo, f_hi], axis=-1)
    out_ref[...] = _rms(x2) * fg_ref[...]


def _combine_call(x1, mod3, tw, final_g, gathered, tile0, n_tiles, name):
    return pl.pallas_call(
        _combine_kernel,
        out_shape=jax.ShapeDtypeStruct((n_tiles * TOK_TILE, D_MODEL), F32),
        grid=(n_tiles,),
        in_specs=[
            pl.BlockSpec((TOK_TILE, D_MODEL), lambda t: (tile0 + t, 0)),
            pl.BlockSpec((1, 1, N_MOD * D_MODEL), lambda t: (_mod_row(tile0 + t), 0, 0)),
            pl.BlockSpec((TOK_TILE, LANES), lambda t: (tile0 + t, 0)),
            pl.BlockSpec((1, D_MODEL), lambda t: (0, 0)),
            pl.BlockSpec((TOP_K, TOK_TILE, HALF_W), lambda t: (0, t, 0)),
        ],
        out_specs=pl.BlockSpec((TOK_TILE, D_MODEL), lambda t: (t, 0)),
        compiler_params=_params(("arbitrary",)),
        name=name,
    )(x1, mod3, tw, final_g, gathered)


def kernel(x_prompt, x_sample, state_gla_fwd, state_gla_bwd, c, c_ctx, norm1_g, w_mod, b_mod, w_in,
           w_alpha, b_alpha, gla_norm_g, w_pool_grp, pool_scale, w_branch_gla, w_branch_pool, w_out,
           norm2_g, w_router, b_router, w_gate, b_gate, w_up, b_up, w_down, b_down, final_norm_g):
    l = 0
    x_ctx = x_prompt.reshape(N_CTX, D_MODEL)
    x_lat = x_sample.reshape(N_LAT, D_MODEL)

    cvec = jnp.concatenate([c_ctx[None, :], c, jnp.zeros((8 - 1 - DEC_BATCH, D_MODEL), F32)], axis=0)
    mod = _mod_call(cvec, w_mod[l], b_mod[l][None, :])
    mod3 = mod.reshape(8, 1, N_MOD * D_MODEL)

    w_in_b = w_in[l].astype(BF16)
    w_main = w_in_b[:, :MAIN_W]
    w_alr = w_in_b[:, MAIN_W:MAIN_W + ALR_W]
    w_xp = w_in_b[:, MAIN_W + ALR_W:MAIN_W + ALR_W + POOL_W]
    w_mg = w_in_b[:, MAIN_W + ALR_W + POOL_W:]
    q, k, v, og, alr, xp, mg = _inproj_call(x_ctx, x_lat, mod3, norm1_g[l][None, :],
                                            w_main, w_alr, w_xp, w_mg)

    zpad = jnp.zeros((GLA_LOWRANK, QK_W), F32)
    wa_f = jnp.concatenate([w_alpha[l, 0], zpad], axis=0)
    wa_b = jnp.concatenate([zpad, w_alpha[l, 1]], axis=0)
    o_f, o_b, s_f, s_b = _gla_call(q, k, v, alr, wa_f, b_alpha[l, 0][None, :], wa_b,
                                   b_alpha[l, 1][None, :], state_gla_fwd[:, l], state_gla_bwd[:, l])

    pooled_c = _pool_ctx_call(xp)
    pooled_l = _pool_lat_call(xp)

    w_router_pad = jnp.pad(w_router[l], ((0, 0), (0, LANES - N_EXPERTS)))
    b_router_pad = jnp.pad(b_router[l], (0, LANES - N_EXPERTS))[None, :]
    x1, h2, top_idx, top_w = _post_call(
        x_ctx, x_lat, mod3, o_f, o_b, og, pooled_c, pooled_l, mg,
        gla_norm_g[l].reshape(1, V_W), w_pool_grp[l].astype(BF16), pool_scale[l][None, :],
        w_branch_gla[l].astype(BF16), w_branch_pool[l].astype(BF16), w_out[l].astype(BF16),
        norm2_g[l][None, :], w_router_pad, b_router_pad)

    rank, cnt = _route_call(top_idx)
    counts = cnt[0, :N_EXPERTS].astype(I32)
    padded = (counts + MOE_BLOCK - 1) // MOE_BLOCK * MOE_BLOCK
    pad_end = jnp.cumsum(padded).astype(I32)
    pad_start = pad_end - padded
    block_first = jnp.arange(N_SLOT_BLOCKS, dtype=I32) * MOE_BLOCK
    block_e = jnp.minimum(jnp.sum((pad_end[None, :] <= block_first[:, None]).astype(I32), axis=1),
                          N_EXPERTS - 1).astype(I32)
    n_used = (pad_end[-1:] // MOE_BLOCK).astype(I32)
    experts = jnp.arange(N_EXPERTS, dtype=I32)
    tk = top_idx[:, :TOP_K]
    pos = jnp.sum(jnp.where(tk[:, :, None] == experts, pad_start, 0), axis=-1) + rank[:, :TOP_K]
    pos = pos.astype(I32)
    pos_by_choice = pos.T
    pos_chunks = pos_by_choice.reshape(TOP_K, N_TOK // SC_ROWS, SC_ROWS).transpose(1, 0, 2)

    hs = _sc_scatter_rows(h2, pos_chunks, N_SLOTS)
    y = _moe_call(block_e, n_used, hs,
                  w_gate[l], b_gate[l][:, None, :], w_up[l], b_up[l][:, None, :],
                  w_down[l], b_down[l][:, None, :])
    outs = []
    for name, tok0, n_tok in (("combine_ctx", 0, N_CTX), ("combine_lat", N_CTX, N_LAT)):
        idx = pos_by_choice[:, tok0:tok0 + n_tok].reshape(TOP_K * n_tok)
        gathered = _sc_gather_rows(y, idx).reshape(TOP_K, n_tok, HALF_W)
        outs.append(_combine_call(x1, mod3, top_w, final_norm_g[None, :], gathered,
                                  tok0 // TOK_TILE, n_tok // TOK_TILE, name))
    y_prompt = outs[0].reshape(BATCH, SEQ, D_MODEL)
    y_sample = outs[1].reshape(DEC_BATCH, DEC_SEQ, D_MODEL)
    return (y_prompt, y_sample, s_f[:, None], s_b[:, None])
```

```python
import functools

import jax
import jax.numpy as jnp
from jax import lax
from jax.experimental import pallas as pl
from jax.experimental.pallas import tpu as pltpu
from jax.experimental.pallas import tpu_sc as plsc

F32 = jnp.float32
BF16 = jnp.bfloat16
I32 = jnp.int32

D_MODEL = 1024
BATCH = 32
SEQ = 256
DEC_BATCH = 4
DEC_SEQ = 2048
GRID_W = 64
GLA_HEADS = 4
GLA_DK = 128
GLA_DV = 256
GLA_LOWRANK = 16
GLA_TAU = 16.0
GLA_CHUNK = 64
POOL_GROUPS = 4
POOL_GROUP_DIM = 128
POOL_WINDOWS = (2, 4, 8, 16)
N_EXPERTS = 32
TOP_K = 4
D_FF = 1024
SWIGLU_LIMIT = 7.0
SWIGLU_ALPHA = 1.702
MOE_BLOCK = 256
NORM_EPS = 1e-6
N_MOD = 6

QK_W = GLA_HEADS * GLA_DK
V_W = GLA_HEADS * GLA_DV
POOL_W = POOL_GROUPS * POOL_GROUP_DIM
MAIN_W = 2 * QK_W + 2 * V_W
ALR_W = 2 * GLA_LOWRANK
MG_W = 2 * D_MODEL

N_CTX = BATCH * SEQ
N_LAT = DEC_BATCH * DEC_SEQ
N_TOK = N_CTX + N_LAT
N_SLOT_BLOCKS = -(-(N_TOK * TOP_K + N_EXPERTS * (MOE_BLOCK - 1)) // MOE_BLOCK)
N_SLOTS = N_SLOT_BLOCKS * MOE_BLOCK

LANES = 128
TOK_TILE = 256
N_TILES = N_TOK // TOK_TILE
CTX_TILES = N_CTX // TOK_TILE
LAT_TILES_PER_SEQ = DEC_SEQ // TOK_TILE
ROUTE_TILE = 512
VMEM_LIMIT = 56 * 1024 * 1024

GLA_SEQS = 4
CTX_CHUNKS = SEQ // GLA_CHUNK
LAT_CHUNKS = DEC_SEQ // GLA_CHUNK
CHUNKS_PER_TILE = TOK_TILE // GLA_CHUNK
GLA_CTX_STEPS = (BATCH // GLA_SEQS) * CTX_CHUNKS
TILE_GRID = 8

NT_DIMS = (((1,), (1,)), ((), ()))
TN_DIMS = (((0,), (0,)), ((), ()))

assert DEC_BATCH == GLA_SEQS and SEQ == TOK_TILE and N_TILES == TILE_GRID * TILE_GRID


def _params(semantics, vmem=VMEM_LIMIT):
    return pltpu.CompilerParams(dimension_semantics=semantics, vmem_limit_bytes=vmem)


def _split_bf16(a):
    hi = a.astype(BF16)
    lo = (a - hi.astype(F32)).astype(BF16)
    return hi, lo


def _dot(a, b):
    return jnp.dot(a, b, preferred_element_type=F32)


def _dot3(a, b):
    a_hi, a_lo = _split_bf16(a)
    b_hi, b_lo = _split_bf16(b)
    return _dot(a_hi, b_hi) + _dot(a_lo, b_hi) + _dot(a_hi, b_lo)


def _sigmoid(x):
    return 1.0 / (1.0 + jnp.exp(-x))


HALF_W = D_MODEL // 2
HIGH_HALF_MASK = -65536


def _pack_halves(lo, hi):
    lo_bits = pltpu.bitcast(lo.astype(BF16).astype(F32), I32)
    hi_bits = pltpu.bitcast(hi.astype(BF16).astype(F32), I32)
    return lax.shift_right_logical(lo_bits, 16) | (hi_bits & HIGH_HALF_MASK)


def _unpack_halves(words):
    lo = pltpu.bitcast(lax.shift_left(words, 16), F32)
    hi = pltpu.bitcast(words & HIGH_HALF_MASK, F32)
    return lo, hi


def _rms(x):
    return x * lax.rsqrt(jnp.mean(x * x, axis=-1, keepdims=True) + NORM_EPS)


def _mod_row(t):
    return jnp.where(t < CTX_TILES, 0, 1 + (t - CTX_TILES) // LAT_TILES_PER_SEQ)


def _store_tile(t):
    u = t - CTX_TILES
    return jnp.where(t < CTX_TILES, t,
                     CTX_TILES + DEC_BATCH * (u % LAT_TILES_PER_SEQ) + u // LAT_TILES_PER_SEQ)


def _ctx_tile(t):
    return jnp.minimum(t, CTX_TILES - 1)


def _lat_tile(t):
    return jnp.maximum(t - CTX_TILES, 0)


def _mod_kernel(c_ref, w_ref, b_ref, o_ref):
    c = c_ref[...]
    o_ref[...] = _dot3(c * _sigmoid(c), w_ref[...]) + b_ref[...]


def _mod_call(cvec, w_mod, b_mod):
    rows = cvec.shape[0]
    return pl.pallas_call(
        _mod_kernel,
        out_shape=jax.ShapeDtypeStruct((rows, N_MOD * D_MODEL), F32),
        grid=(N_MOD,),
        in_specs=[
            pl.BlockSpec((rows, D_MODEL), lambda j: (0, 0)),
            pl.BlockSpec((D_MODEL, D_MODEL), lambda j: (0, j)),
            pl.BlockSpec((1, D_MODEL), lambda j: (0, j)),
        ],
        out_specs=pl.BlockSpec((rows, D_MODEL), lambda j: (0, j)),
        compiler_params=_params(("arbitrary",)),
        name="mod",
    )(cvec, w_mod, b_mod)


def _inproj_kernel(xc_ref, xl_ref, mod_ref, g_ref, wmain_ref, walr_ref, wxp_ref, wmg_ref,
                   q_ref, k_ref, v_ref, og_ref, alr_ref, xp_ref, mg_ref):
    t = pl.program_id(0)
    x = jnp.where(t < CTX_TILES, xc_ref[...], xl_ref[...])
    mod = mod_ref[0]
    shift1 = mod[:, 0:D_MODEL]
    scale1 = mod[:, D_MODEL:2 * D_MODEL]
    h = _rms(x) * g_ref[...]
    h = (h * (1.0 + scale1) + shift1).astype(BF16)
    z = _dot(h, wmain_ref[...])
    q_ref[...] = (z[:, 0:QK_W] * (GLA_DK ** -0.5)).astype(BF16)
    k_ref[...] = z[:, QK_W:2 * QK_W].astype(BF16)
    v_ref[...] = z[:, 2 * QK_W:2 * QK_W + V_W].astype(BF16)
    og_ref[...] = z[:, 2 * QK_W + V_W:MAIN_W].astype(BF16)
    alr_ref[...] = _dot(h, walr_ref[...])
    xp_ref[...] = _dot(h, wxp_ref[...])
    mg_ref[...] = _dot(h, wmg_ref[...]).astype(BF16)


def _inproj_call(x_ctx, x_lat, mod3, norm1_g, w_main, w_alr, w_xp, w_mg):
    const = lambda t: (0, 0)
    stored = lambda t: (_store_tile(t), 0)
    widths = (QK_W, QK_W, V_W, V_W, ALR_W, POOL_W, MG_W)
    dtypes = (BF16, BF16, BF16, BF16, F32, F32, BF16)
    return pl.pallas_call(
        _inproj_kernel,
        out_shape=[jax.ShapeDtypeStruct((N_TOK, w), dt) for w, dt in zip(widths, dtypes)],
        grid=(N_TILES,),
        in_specs=[
            pl.BlockSpec((TOK_TILE, D_MODEL), lambda t: (_ctx_tile(t), 0)),
            pl.BlockSpec((TOK_TILE, D_MODEL), lambda t: (_lat_tile(t), 0)),
            pl.BlockSpec((1, 1, N_MOD * D_MODEL), lambda t: (_mod_row(t), 0, 0)),
            pl.BlockSpec((1, D_MODEL), const),
            pl.BlockSpec((D_MODEL, MAIN_W), const),
            pl.BlockSpec((D_MODEL, ALR_W), const),
            pl.BlockSpec((D_MODEL, POOL_W), const),
            pl.BlockSpec((D_MODEL, MG_W), const),
        ],
        out_specs=[pl.BlockSpec((TOK_TILE, w), stored) for w in widths],
        compiler_params=_params(("arbitrary",)),
        name="inproj",
    )(x_ctx, x_lat, mod3, norm1_g, w_main, w_alr, w_xp, w_mg)


def _gla_direction(q_ref, k_ref, v_ref, alr_ref, wa_ref, ba_ref, o_ref, st_ref, slot0, rev):
    rows = GLA_SEQS * GLA_CHUNK
    alr = jnp.concatenate([alr_ref[0, s] for s in range(GLA_SEQS)], axis=0)
    a = _dot3(alr, wa_ref[...]) + ba_ref[...]
    g = (jnp.minimum(a, 0.0) - jnp.log(1.0 + jnp.exp(-jnp.abs(a)))) * (1.0 / GLA_TAU)

    row = lax.broadcasted_iota(I32, (rows, rows), 0)
    col = lax.broadcasted_iota(I32, (rows, rows), 1)
    same = (row // GLA_CHUNK) == (col // GLA_CHUNK)
    tri_all = same & ((col >= row) if rev else (col <= row))
    tri_b = jnp.where(tri_all, 1.0, 0.0).astype(BF16)
    g_hi, g_lo = _split_bf16(g)
    bcum_all = _dot(tri_b, g_hi) + _dot(tri_b, g_lo)

    r64 = lax.broadcasted_iota(I32, (GLA_CHUNK, GLA_CHUNK), 0)
    c64 = lax.broadcasted_iota(I32, (GLA_CHUNK, GLA_CHUNK), 1)
    tri = (c64 >= r64) if rev else (c64 <= r64)

    for s in range(GLA_SEQS):
        bcum = bcum_all[s * GLA_CHUNK:(s + 1) * GLA_CHUNK]
        blast = bcum[0:1] if rev else bcum[GLA_CHUNK - 1:GLA_CHUNK]
        bmid = bcum[GLA_CHUNK // 2:GLA_CHUNK // 2 + 1]
        e_q = jnp.exp(bcum - bmid)
        e_k = jnp.exp(bmid - bcum)
        e_in = jnp.exp(bcum)
        e_out = jnp.exp(blast - bcum)
        e_last = jnp.exp(blast)
        q = q_ref[0, s].astype(F32)
        k = k_ref[0, s].astype(F32)
        for h in range(GLA_HEADS):
            ks = slice(h * GLA_DK, (h + 1) * GLA_DK)
            vs = slice(h * GLA_DV, (h + 1) * GLA_DV)
            qh = q[:, ks]
            kh = k[:, ks]
            vh = v_ref[0, s, :, vs]
            att = lax.dot_general((qh * e_q[:, ks]).astype(BF16), (kh * e_k[:, ks]).astype(BF16),
                                  NT_DIMS, preferred_element_type=F32)
            att = jnp.where(tri, att, 0.0).astype(BF16)
            st = st_ref[slot0 + s, h]
            o_inter = lax.dot_general((qh * e_in[:, ks]).astype(BF16), st.astype(BF16),
                                      NT_DIMS, preferred_element_type=F32)
            o_ref[0, s, :, vs] = o_inter + _dot(att, vh)
            upd = lax.dot_general(vh, (kh * e_out[:, ks]).astype(BF16), TN_DIMS,
                                  preferred_element_type=F32)
            st_ref[slot0 + s, h] = st * e_last[:, ks] + upd


def _gla_kernel(qf_ref, kf_ref, vf_ref, af_ref, qb_ref, kb_ref, vb_ref, ab_ref,
                waf_ref, baf_ref, wab_ref, bab_ref, s0f_ref, s0b_ref,
                of_ref, ob_ref, sf_ref, sb_ref, st_ref):
    i = pl.program_id(0)
    is_ctx = i < GLA_CTX_STEPS
    chunk = jnp.where(is_ctx, i % CTX_CHUNKS, i - GLA_CTX_STEPS)

    @pl.when(is_ctx & (chunk == 0))
    def _():
        st_ref[...] = jnp.zeros(st_ref.shape, F32)

    @pl.when(i == GLA_CTX_STEPS)
    def _():
        for s in range(GLA_SEQS):
            for h in range(GLA_HEADS):
                st_ref[s, h] = s0f_ref[s, h].T
                st_ref[GLA_SEQS + s, h] = s0b_ref[s, h].T

    _gla_direction(qf_ref, kf_ref, vf_ref, af_ref, waf_ref, baf_ref, of_ref, st_ref, 0, False)
    _gla_direction(qb_ref, kb_ref, vb_ref, ab_ref, wab_ref, bab_ref, ob_ref, st_ref, GLA_SEQS, True)

    @pl.when(is_ctx & (chunk == CTX_CHUNKS - 1))
    def _():
        for s in range(GLA_SEQS):
            for h in range(GLA_HEADS):
                sf_ref[s, h] = st_ref[s, h].T
                sb_ref[s, h] = st_ref[GLA_SEQS + s, h].T


def _gla_block(i, rev):
    is_ctx = i < GLA_CTX_STEPS
    group = i // CTX_CHUNKS
    c_ctx = i % CTX_CHUNKS
    c_lat = i - GLA_CTX_STEPS
    if rev:
        c_ctx = CTX_CHUNKS - 1 - c_ctx
        c_lat = LAT_CHUNKS - 1 - c_lat
    j = c_lat // CHUNKS_PER_TILE
    per_row = TILE_GRID // GLA_SEQS
    a = jnp.where(is_ctx, group // per_row, CTX_TILES // TILE_GRID + j // per_row)
    b = jnp.where(is_ctx, group % per_row, j % per_row)
    c = jnp.where(is_ctx, c_ctx, c_lat % CHUNKS_PER_TILE)
    return (a, b, c, 0)


def _gla_call(q, k, v, alr, wa_f, ba_f, wa_b, ba_b, s0_f, s0_b):
    def view(arr):
        return arr.reshape(TILE_GRID, TILE_GRID, TOK_TILE, arr.shape[-1])

    def spec(width, rev):
        return pl.BlockSpec((1, GLA_SEQS, GLA_CHUNK, width), lambda i: _gla_block(i, rev))

    const = lambda i: (0, 0)
    st_block = (GLA_SEQS, GLA_HEADS, GLA_DK, GLA_DV)
    whole_state = pl.BlockSpec(st_block, lambda i: (0, 0, 0, 0))
    ctx_state = pl.BlockSpec(
        st_block, lambda i: (jnp.minimum(i // CTX_CHUNKS, BATCH // GLA_SEQS - 1), 0, 0, 0))
    in_specs = []
    for rev in (False, True):
        in_specs += [spec(QK_W, rev), spec(QK_W, rev), spec(V_W, rev), spec(ALR_W, rev)]
    in_specs += [pl.BlockSpec((ALR_W, QK_W), const), pl.BlockSpec((1, QK_W), const)] * 2
    in_specs += [whole_state, whole_state]
    o_shape = jax.ShapeDtypeStruct((TILE_GRID, TILE_GRID, TOK_TILE, V_W), F32)
    s_shape = jax.ShapeDtypeStruct((BATCH, GLA_HEADS, GLA_DK, GLA_DV), F32)
    qv, kv, vv, av = view(q), view(k), view(v), view(alr)
    o_f, o_b, s_f, s_b = pl.pallas_call(
        _gla_kernel,
        out_shape=[o_shape, o_shape, s_shape, s_shape],
        grid=(GLA_CTX_STEPS + LAT_CHUNKS,),
        in_specs=in_specs,
        out_specs=[spec(V_W, False), spec(V_W, True), ctx_state, ctx_state],
        scratch_shapes=[pltpu.VMEM((2 * GLA_SEQS, GLA_HEADS, GLA_DV, GLA_DK), F32)],
        compiler_params=_params(("arbitrary",)),
        name="gla",
    )(qv, kv, vv, av, qv, kv, vv, av, wa_f, ba_f, wa_b, ba_b, s0_f, s0_b)
    return o_f.reshape(N_TOK, V_W), o_b.reshape(N_TOK, V_W), s_f, s_b


def _band(n, w, block):
    row = lax.broadcasted_iota(I32, (n, n), 0)
    col = lax.broadcasted_iota(I32, (n, n), 1)
    inside = (col >= row - w // 2) & (col <= row + w // 2 - 1)
    if block < n:
        inside = inside & ((row // block) == (col // block))
    return jnp.where(inside, 1.0, 0.0).astype(BF16)


def _win_count(p, n, w):
    return jnp.minimum(p + w // 2 - 1, n - 1) - jnp.maximum(p - w // 2, 0) + 1


def _pool_ctx_kernel(x_ref, o_ref, band_ref):
    @pl.when(pl.program_id(0) == 0)
    def _():
        for gi, w in enumerate(POOL_WINDOWS):
            band_ref[gi] = _band(SEQ, w, SEQ)

    p = lax.broadcasted_iota(I32, (SEQ, POOL_GROUP_DIM), 0)
    for gi, w in enumerate(POOL_WINDOWS):
        cs = slice(gi * POOL_GROUP_DIM, (gi + 1) * POOL_GROUP_DIM)
        x = x_ref[:, cs]
        hi, lo = _split_bf16(x)
        band = band_ref[gi]
        s = _dot(band, hi) + _dot(band, lo)
        cnt = _win_count(p, SEQ, w).astype(F32)
        o_ref[:, cs] = s / cnt - x


def _pool_ctx_call(xp):
    spec = pl.BlockSpec((SEQ, POOL_W), lambda b: (b, 0))
    return pl.pallas_call(
        _pool_ctx_kernel,
        out_shape=jax.ShapeDtypeStruct((N_CTX, POOL_W), F32),
        grid=(BATCH,),
        in_specs=[spec],
        out_specs=spec,
        scratch_shapes=[pltpu.VMEM((POOL_GROUPS, SEQ, SEQ), BF16)],
        compiler_params=_params(("arbitrary",)),
        name="pool_ctx",
    )(xp)


POOL_HALO = (max(POOL_WINDOWS) // 2) * GRID_W


def _pool_lat_kernel(x_ref, o_ref, pad_ref):
    rows = DEC_SEQ // GRID_W
    p = lax.broadcasted_iota(I32, (DEC_SEQ, POOL_GROUP_DIM), 0)
    r = p // GRID_W
    cidx = p % GRID_W
    zeros = jnp.zeros((POOL_HALO, POOL_GROUP_DIM), F32)
    pad_ref[0:POOL_HALO, :] = zeros
    pad_ref[POOL_HALO + DEC_SEQ:2 * POOL_HALO + DEC_SEQ, :] = zeros
    for gi, w in enumerate(POOL_WINDOWS):
        cs = slice(gi * POOL_GROUP_DIM, (gi + 1) * POOL_GROUP_DIM)
        band = _band(TOK_TILE, w, GRID_W)
        for t in range(LAT_TILES_PER_SEQ):
            hi, lo = _split_bf16(x_ref[t, 0, :, cs])
            pad_ref[POOL_HALO + t * TOK_TILE:POOL_HALO + (t + 1) * TOK_TILE, :] = (
                _dot(band, hi) + _dot(band, lo))
        acc = jnp.zeros((DEC_SEQ, POOL_GROUP_DIM), F32)
        for dr in range(-(w // 2), w // 2):
            start = POOL_HALO + dr * GRID_W
            acc = acc + pad_ref[start:start + DEC_SEQ, :]
        cnt = (_win_count(r, rows, w) * _win_count(cidx, GRID_W, w)).astype(F32)
        pooled = acc / cnt
        for t in range(LAT_TILES_PER_SEQ):
            rs = slice(t * TOK_TILE, (t + 1) * TOK_TILE)
            o_ref[t, 0, :, cs] = pooled[rs] - x_ref[t, 0, :, cs]


def _pool_lat_call(xp):
    view = xp.reshape(N_TILES // DEC_BATCH, DEC_BATCH, TOK_TILE, POOL_W)
    blk = (LAT_TILES_PER_SEQ, 1, TOK_TILE, POOL_W)
    out = pl.pallas_call(
        _pool_lat_kernel,
        out_shape=jax.ShapeDtypeStruct((LAT_TILES_PER_SEQ, DEC_BATCH, TOK_TILE, POOL_W), F32),
        grid=(DEC_BATCH,),
        in_specs=[pl.BlockSpec(blk, lambda s: (CTX_TILES // DEC_BATCH // LAT_TILES_PER_SEQ, s, 0, 0))],
        out_specs=pl.BlockSpec(blk, lambda s: (0, s, 0, 0)),
        scratch_shapes=[pltpu.VMEM((DEC_SEQ + 2 * POOL_HALO, POOL_GROUP_DIM), F32)],
        compiler_params=_params(("arbitrary",)),
        name="pool_lat",
    )(view)
    return out.reshape(N_LAT, POOL_W)


def _post_kernel(xc_ref, xl_ref, mod_ref, of_ref, ob_ref, og_ref, pc_ref, pl_ref, mg_ref, gng_ref,
                 wpg_ref, psc_ref, wbg_ref, wbp_ref, wout_ref, n2g_ref, wr_ref, br_ref,
                 x1_ref, h2_ref, idx_ref, tw_ref):
    t = pl.program_id(0)
    is_ctx = t < CTX_TILES
    x = jnp.where(is_ctx, xc_ref[...], xl_ref[...])
    pooled = jnp.where(is_ctx, pc_ref[...], pl_ref[...])
    mod = mod_ref[0]
    gate1 = mod[:, 2 * D_MODEL:3 * D_MODEL]
    shift2 = mod[:, 3 * D_MODEL:4 * D_MODEL]
    scale2 = mod[:, 4 * D_MODEL:5 * D_MODEL]

    o = of_ref[...] + ob_ref[...]
    og = og_ref[...].astype(F32)
    gated = []
    for h in range(GLA_HEADS):
        vs = slice(h * GLA_DV, (h + 1) * GLA_DV)
        oh = _rms(o[:, vs]) * gng_ref[:, vs]
        ogh = og[:, vs]
        gated.append((oh * (ogh * _sigmoid(ogh))).astype(BF16))
    br_gla = _dot(jnp.concatenate(gated, axis=-1), wbg_ref[...])

    pm = []
    for gi in range(POOL_GROUPS):
        cs = slice(gi * POOL_GROUP_DIM, (gi + 1) * POOL_GROUP_DIM)
        pmg = _dot(pooled[:, cs].astype(BF16), wpg_ref[gi]) * psc_ref[:, cs]
        pm.append(pmg.astype(BF16))
    br_pool = _dot(jnp.concatenate(pm, axis=-1), wbp_ref[...])

    mg = mg_ref[...].astype(F32)
    merged = _sigmoid(mg[:, 0:D_MODEL]) * br_gla + _sigmoid(mg[:, D_MODEL:MG_W]) * br_pool
    m = _dot(merged.astype(BF16), wout_ref[...])
    x1 = x + gate1 * m
    x1_ref[...] = x1
    h2 = _rms(x1) * n2g_ref[...]
    h2 = h2 * (1.0 + scale2) + shift2
    h2_ref[...] = _pack_halves(h2[:, :HALF_W], h2[:, HALF_W:])

    logits = _dot3(h2, wr_ref[...]) + br_ref[...]
    lane = lax.broadcasted_iota(I32, (TOK_TILE, LANES), 1)
    lane_f = lane.astype(F32)
    neg = jnp.float32(-jnp.inf)
    cur = jnp.where(lane < N_EXPERTS, logits, neg)
    vals, idxs = [], []
    for _ in range(TOP_K):
        mx = jnp.max(cur, axis=-1, keepdims=True)
        ix = jnp.min(jnp.where(cur == mx, lane_f, float(LANES)), axis=-1, keepdims=True)
        vals.append(mx)
        idxs.append(ix)
        cur = jnp.where(lane_f == ix, neg, cur)
    ex = [jnp.exp(vv - vals[0]) for vv in vals]
    tot = ex[0] + ex[1] + ex[2] + ex[3]
    idx_out = jnp.zeros((TOK_TILE, LANES), F32)
    w_out = jnp.zeros((TOK_TILE, LANES), F32)
    for kk in range(TOP_K):
        idx_out = jnp.where(lane == kk, idxs[kk], idx_out)
        w_out = jnp.where(lane == kk, ex[kk] / tot, w_out)
    idx_ref[...] = idx_out.astype(I32)
    tw_ref[...] = w_out


def _post_call(x_ctx, x_lat, mod3, o_f, o_b, og, pooled_c, pooled_l, mg, gng, wpg, psc, wbg, wbp,
               wout, n2g, wr, br):
    row = lambda t: (t, 0)
    const = lambda t: (0, 0)
    stored = lambda t: (_store_tile(t), 0)
    return pl.pallas_call(
        _post_kernel,
        out_shape=[
            jax.ShapeDtypeStruct((N_TOK, D_MODEL), F32),
            jax.ShapeDtypeStruct((N_TOK, HALF_W), I32),
            jax.ShapeDtypeStruct((N_TOK, LANES), I32),
            jax.ShapeDtypeStruct((N_TOK, LANES), F32),
        ],
        grid=(N_TILES,),
        in_specs=[
            pl.BlockSpec((TOK_TILE, D_MODEL), lambda t: (_ctx_tile(t), 0)),
            pl.BlockSpec((TOK_TILE, D_MODEL), lambda t: (_lat_tile(t), 0)),
            pl.BlockSpec((1, 1, N_MOD * D_MODEL), lambda t: (_mod_row(t), 0, 0)),
            pl.BlockSpec((TOK_TILE, V_W), stored),
            pl.BlockSpec((TOK_TILE, V_W), stored),
            pl.BlockSpec((TOK_TILE, V_W), stored),
            pl.BlockSpec((TOK_TILE, POOL_W), lambda t: (_ctx_tile(t), 0)),
            pl.BlockSpec((TOK_TILE, POOL_W),
                         lambda t: (jnp.maximum(_store_tile(t) - CTX_TILES, 0), 0)),
            pl.BlockSpec((TOK_TILE, MG_W), stored),
            pl.BlockSpec((1, V_W), const),
            pl.BlockSpec((POOL_GROUPS, POOL_GROUP_DIM, POOL_GROUP_DIM), lambda t: (0, 0, 0)),
            pl.BlockSpec((1, POOL_W), const),
            pl.BlockSpec((V_W, D_MODEL), const),
            pl.BlockSpec((POOL_W, D_MODEL), const),
            pl.BlockSpec((D_MODEL, D_MODEL), const),
            pl.BlockSpec((1, D_MODEL), const),
            pl.BlockSpec((D_MODEL, LANES), const),
            pl.BlockSpec((1, LANES), const),
        ],
        out_specs=[
            pl.BlockSpec((TOK_TILE, D_MODEL), row),
            pl.BlockSpec((TOK_TILE, HALF_W), row),
            pl.BlockSpec((TOK_TILE, LANES), row),
            pl.BlockSpec((TOK_TILE, LANES), row),
        ],
        compiler_params=_params(("arbitrary",)),
        name="post",
    )(x_ctx, x_lat, mod3, o_f, o_b, og, pooled_c, pooled_l, mg, gng, wpg, psc, wbg, wbp, wout, n2g,
      wr, br)


def _route_kernel(idx_ref, rank_ref, cnt_ref, carry_ref, strict_ref):
    t = pl.program_id(0)

    @pl.when(t == 0)
    def _():
        carry_ref[...] = jnp.zeros((1, LANES), F32)
        row = lax.broadcasted_iota(I32, (ROUTE_TILE, ROUTE_TILE), 0)
        col = lax.broadcasted_iota(I32, (ROUTE_TILE, ROUTE_TILE), 1)
        strict_ref[...] = jnp.where(col < row, 1.0, 0.0).astype(BF16)

    idx = idx_ref[...]
    lane = lax.broadcasted_iota(I32, (ROUTE_TILE, LANES), 1)
    sel = [lane == idx[:, kk:kk + 1] for kk in range(TOP_K)]
    onehot = jnp.zeros((ROUTE_TILE, LANES), F32)
    for kk in range(TOP_K):
        onehot = onehot + jnp.where(sel[kk], 1.0, 0.0)
    before = _dot(strict_ref[...], onehot.astype(BF16)) + carry_ref[...]
    rank = jnp.zeros((ROUTE_TILE, LANES), F32)
    for kk in range(TOP_K):
        rk = jnp.sum(jnp.where(sel[kk], before, 0.0), axis=-1, keepdims=True)
        rank = jnp.where(lane == kk, rk, rank)
    rank_ref[...] = rank.astype(I32)
    carry_ref[...] = carry_ref[...] + jnp.sum(onehot, axis=0, keepdims=True)
    cnt_ref[...] = jnp.broadcast_to(carry_ref[...], (8, LANES))


def _route_call(idx):
    return pl.pallas_call(
        _route_kernel,
        out_shape=[
            jax.ShapeDtypeStruct((N_TOK, LANES), I32),
            jax.ShapeDtypeStruct((8, LANES), F32),
        ],
        grid=(N_TOK // ROUTE_TILE,),
        in_specs=[pl.BlockSpec((ROUTE_TILE, LANES), lambda t: (t, 0))],
        out_specs=[
            pl.BlockSpec((ROUTE_TILE, LANES), lambda t: (t, 0)),
            pl.BlockSpec((8, LANES), lambda t: (0, 0)),
        ],
        scratch_shapes=[pltpu.VMEM((1, LANES), F32), pltpu.VMEM((ROUTE_TILE, ROUTE_TILE), BF16)],
        compiler_params=_params(("arbitrary",)),
        name="route",
    )(idx)


def _moe_kernel(be_ref, nu_ref, x_ref, wg_ref, bg_ref, wu_ref, bu_ref, wd_ref, bd_ref, y_ref,
                wgu_ref, wdb_ref):
    b = pl.program_id(0)
    n_used = nu_ref[0]
    e = be_ref[b]
    prev = be_ref[jnp.maximum(b - 1, 0)]
    live = b < n_used

    @pl.when(live & ((b == 0) | (e != prev)))
    def _():
        wgu_ref[:, :D_FF] = wg_ref[0].astype(BF16)
        wgu_ref[:, D_FF:] = wu_ref[0].astype(BF16)
        wdb_ref[...] = wd_ref[0].astype(BF16)

    @pl.when(live)
    def _():
        x_lo, x_hi = _unpack_halves(x_ref[...])
        x = jnp.concatenate([x_lo.astype(BF16), x_hi.astype(BF16)], axis=-1)
        gu = _dot(x, wgu_ref[...])
        gate = jnp.minimum(gu[:, :D_FF] + bg_ref[0], SWIGLU_LIMIT)
        up = jnp.clip(gu[:, D_FF:] + bu_ref[0], -SWIGLU_LIMIT, SWIGLU_LIMIT)
        act = (up + 1.0) * (gate * _sigmoid(SWIGLU_ALPHA * gate))
        y = _dot(act.astype(BF16), wdb_ref[...]) + bd_ref[0]
        y_ref[...] = _pack_halves(y[:, :HALF_W], y[:, HALF_W:])

    @pl.when(jnp.logical_not(live))
    def _():
        y_ref[...] = jnp.zeros((MOE_BLOCK, HALF_W), I32)


def _moe_call(block_e, n_used, hs, w_gate, b_gate, w_up, b_up, w_down, b_down):
    def blk(b, be, nu):
        return jnp.minimum(b, nu[0] - 1)

    row = lambda b, be, nu: (blk(b, be, nu), 0)
    wsel = lambda b, be, nu: (be[blk(b, be, nu)], 0, 0)
    return pl.pallas_call(
        _moe_kernel,
        out_shape=jax.ShapeDtypeStruct((N_SLOTS, HALF_W), I32),
        grid_spec=pltpu.PrefetchScalarGridSpec(
            num_scalar_prefetch=2,
            grid=(N_SLOT_BLOCKS,),
            in_specs=[
                pl.BlockSpec((MOE_BLOCK, HALF_W), row),
                pl.BlockSpec((1, D_MODEL, D_FF), wsel),
                pl.BlockSpec((1, 1, D_FF), wsel),
                pl.BlockSpec((1, D_MODEL, D_FF), wsel),
                pl.BlockSpec((1, 1, D_FF), wsel),
                pl.BlockSpec((1, D_FF, D_MODEL), wsel),
                pl.BlockSpec((1, 1, D_MODEL), wsel),
            ],
            out_specs=pl.BlockSpec((MOE_BLOCK, HALF_W), lambda b, be, nu: (b, 0)),
            scratch_shapes=[
                pltpu.VMEM((D_MODEL, 2 * D_FF), BF16),
                pltpu.VMEM((D_FF, D_MODEL), BF16),
            ],
        ),
        compiler_params=_params(("arbitrary",)),
        name="moe",
    )(block_e, n_used, hs, w_gate, b_gate, w_up, b_up, w_down, b_down)


SC_CORES = 2
SC_SUBCORES = 16
SC_WORKERS = SC_CORES * SC_SUBCORES
SC_ROWS = 128
COMBINE_CHUNKS = 2
assert N_CTX == N_LAT


def _sc_gather_rows(table, idx):
    n_idx = idx.shape[0]
    width = table.shape[1]
    per_worker = n_idx // SC_WORKERS
    n_chunks = per_worker // SC_ROWS
    assert n_chunks * SC_ROWS * SC_WORKERS == n_idx
    mesh = plsc.VectorSubcoreMesh(core_axis_name="c", subcore_axis_name="s")

    @functools.partial(
        pl.kernel, mesh=mesh,
        out_type=jax.ShapeDtypeStruct((n_idx, width), table.dtype),
        scratch_types=[pltpu.VMEM((SC_ROWS,), I32), pltpu.VMEM((SC_ROWS, width), table.dtype),
                       pltpu.SemaphoreType.DMA],
        name="sc_gather",
    )
    def gather(table_hbm, idx_hbm, out_hbm, idx_v, rows_v, sem):
        worker = lax.axis_index("s") * SC_CORES + lax.axis_index("c")
        base = worker * per_worker

        @pl.loop(0, n_chunks)
        def _(ch):
            off = pl.multiple_of(base + ch * SC_ROWS, SC_ROWS)
            pltpu.sync_copy(idx_hbm.at[pl.ds(off, SC_ROWS)], idx_v)
            pltpu.async_copy(table_hbm.at[idx_v], rows_v, sem).wait()
            pltpu.sync_copy(rows_v, out_hbm.at[pl.ds(off, SC_ROWS)])

    return gather(table, idx)


def _sc_scatter_rows(rows, idx3, n_out):
    n_rows, width = rows.shape
    n_chunks = n_rows // SC_ROWS // SC_WORKERS
    assert n_chunks * SC_ROWS * SC_WORKERS == n_rows and idx3.shape == (n_rows // SC_ROWS, TOP_K, SC_ROWS)
    mesh = plsc.VectorSubcoreMesh(core_axis_name="c", subcore_axis_name="s")

    @functools.partial(
        pl.kernel, mesh=mesh,
        out_type=jax.ShapeDtypeStruct((n_out, width), rows.dtype),
        scratch_types=[pltpu.VMEM((TOP_K, SC_ROWS), I32), pltpu.VMEM((SC_ROWS, width), rows.dtype),
                       pltpu.SemaphoreType.DMA],
        name="sc_scatter",
    )
    def scatter(rows_hbm, idx_hbm, out_hbm, idx_v, rows_v, sem):
        worker = lax.axis_index("s") * SC_CORES + lax.axis_index("c")

        @pl.loop(0, n_chunks)
        def _(ch):
            chunk = worker * n_chunks + ch
            pltpu.sync_copy(idx_hbm.at[chunk], idx_v)
            pltpu.sync_copy(rows_hbm.at[pl.ds(pl.multiple_of(chunk * SC_ROWS, SC_ROWS), SC_ROWS)],
                            rows_v)
            for kk in range(TOP_K):
                pltpu.async_copy(rows_v, out_hbm.at[idx_v.at[kk]], sem).wait()

    return scatter(rows, idx3)


def _combine_kernel(x1_ref, mod_ref, tw_ref, fg_ref, g_ref, *rest):
    out_ref = rest[-1]
    tw = tw_ref[...]
    f_lo = jnp.zeros((TOK_TILE, HALF_W), F32)
    f_hi = jnp.zeros((TOK_TILE, HALF_W), F32)
    for kk in range(TOP_K):
        lo, hi = _unpack_halves(g_ref[kk])
        f_lo = f_lo + lo * tw[:, kk:kk + 1]
        f_hi = f_hi + hi * tw[:, kk:kk + 1]
    gate2 = mod_ref[0][:, 5 * D_MODEL:6 * D_MODEL]
    x2 = x1_ref[...] + gate2 * jnp.concatenate([f_lo, f_hi], axis=-1)
    out_ref[...] = _rms(x2) * fg_ref[...]


def _combine_call(x1, mod3, tw, final_g, gathered, partial, tile0, out_tile0, out_tiles, name):
    n_tiles = gathered.shape[1] // TOK_TILE
    in_specs = [
        pl.BlockSpec((TOK_TILE, D_MODEL), lambda t: (tile0 + t, 0)),
        pl.BlockSpec((1, 1, N_MOD * D_MODEL), lambda t: (_mod_row(tile0 + t), 0, 0)),
        pl.BlockSpec((TOK_TILE, LANES), lambda t: (tile0 + t, 0)),
        pl.BlockSpec((1, D_MODEL), lambda t: (0, 0)),
        pl.BlockSpec((TOP_K, TOK_TILE, HALF_W), lambda t: (0, t, 0)),
    ]
    args = [x1, mod3, tw, final_g, gathered]
    aliases = {}
    if partial is not None:
        in_specs.append(pl.BlockSpec(memory_space=pl.ANY))
        args.append(partial)
        aliases = {len(args) - 1: 0}
    return pl.pallas_call(
        _combine_kernel,
        out_shape=jax.ShapeDtypeStruct((out_tiles * TOK_TILE, D_MODEL), F32),
        grid=(n_tiles,),
        in_specs=in_specs,
        out_specs=pl.BlockSpec((TOK_TILE, D_MODEL), lambda t: (out_tile0 + t, 0)),
        input_output_aliases=aliases,
        compiler_params=_params(("arbitrary",)),
        name=name,
    )(*args)


def kernel(x_prompt, x_sample, state_gla_fwd, state_gla_bwd, c, c_ctx, norm1_g, w_mod, b_mod, w_in,
           w_alpha, b_alpha, gla_norm_g, w_pool_grp, pool_scale, w_branch_gla, w_branch_pool, w_out,
           norm2_g, w_router, b_router, w_gate, b_gate, w_up, b_up, w_down, b_down, final_norm_g):
    l = 0
    x_ctx = x_prompt.reshape(N_CTX, D_MODEL)
    x_lat = x_sample.reshape(N_LAT, D_MODEL)

    cvec = jnp.concatenate([c_ctx[None, :], c, jnp.zeros((8 - 1 - DEC_BATCH, D_MODEL), F32)], axis=0)
    mod = _mod_call(cvec, w_mod[l], b_mod[l][None, :])
    mod3 = mod.reshape(8, 1, N_MOD * D_MODEL)

    w_in_b = w_in[l].astype(BF16)
    w_main = w_in_b[:, :MAIN_W]
    w_alr = w_in_b[:, MAIN_W:MAIN_W + ALR_W]
    w_xp = w_in_b[:, MAIN_W + ALR_W:MAIN_W + ALR_W + POOL_W]
    w_mg = w_in_b[:, MAIN_W + ALR_W + POOL_W:]
    q, k, v, og, alr, xp, mg = _inproj_call(x_ctx, x_lat, mod3, norm1_g[l][None, :],
                                            w_main, w_alr, w_xp, w_mg)

    zpad = jnp.zeros((GLA_LOWRANK, QK_W), F32)
    wa_f = jnp.concatenate([w_alpha[l, 0], zpad], axis=0)
    wa_b = jnp.concatenate([zpad, w_alpha[l, 1]], axis=0)
    o_f, o_b, s_f, s_b = _gla_call(q, k, v, alr, wa_f, b_alpha[l, 0][None, :], wa_b,
                                   b_alpha[l, 1][None, :], state_gla_fwd[:, l], state_gla_bwd[:, l])

    pooled_c = _pool_ctx_call(xp)
    pooled_l = _pool_lat_call(xp)

    w_router_pad = jnp.pad(w_router[l], ((0, 0), (0, LANES - N_EXPERTS)))
    b_router_pad = jnp.pad(b_router[l], (0, LANES - N_EXPERTS))[None, :]
    x1, h2, top_idx, top_w = _post_call(
        x_ctx, x_lat, mod3, o_f, o_b, og, pooled_c, pooled_l, mg,
        gla_norm_g[l].reshape(1, V_W), w_pool_grp[l].astype(BF16), pool_scale[l][None, :],
        w_branch_gla[l].astype(BF16), w_branch_pool[l].astype(BF16), w_out[l].astype(BF16),
        norm2_g[l][None, :], w_router_pad, b_router_pad)

    rank, cnt = _route_call(top_idx)
    counts = cnt[0, :N_EXPERTS].astype(I32)
    padded = (counts + MOE_BLOCK - 1) // MOE_BLOCK * MOE_BLOCK
    pad_end = jnp.cumsum(padded).astype(I32)
    pad_start = pad_end - padded
    block_first = jnp.arange(N_SLOT_BLOCKS, dtype=I32) * MOE_BLOCK
    block_e = jnp.minimum(jnp.sum((pad_end[None, :] <= block_first[:, None]).astype(I32), axis=1),
                          N_EXPERTS - 1).astype(I32)
    n_used = (pad_end[-1:] // MOE_BLOCK).astype(I32)
    experts = jnp.arange(N_EXPERTS, dtype=I32)
    tk = top_idx[:, :TOP_K]
    pos = jnp.sum(jnp.where(tk[:, :, None] == experts, pad_start, 0), axis=-1) + rank[:, :TOP_K]
    pos = pos.astype(I32)
    pos_by_choice = pos.T
    pos_chunks = pos_by_choice.reshape(TOP_K, N_TOK // SC_ROWS, SC_ROWS).transpose(1, 0, 2)

    hs = _sc_scatter_rows(h2, pos_chunks, N_SLOTS)
    y = _moe_call(block_e, n_used, hs,
                  w_gate[l], b_gate[l][:, None, :], w_up[l], b_up[l][:, None, :],
                  w_down[l], b_down[l][:, None, :])
    outs = []
    chunk_tok = N_CTX // COMBINE_CHUNKS
    for group, tok0 in (("ctx", 0), ("lat", N_CTX)):
        out = None
        for ci in range(COMBINE_CHUNKS):
            t0 = tok0 + ci * chunk_tok
            idx = pos_by_choice[:, t0:t0 + chunk_tok].reshape(TOP_K * chunk_tok)
            gathered = _sc_gather_rows(y, idx).reshape(TOP_K, chunk_tok, HALF_W)
            out = _combine_call(x1, mod3, top_w, final_norm_g[None, :], gathered, out,
                                t0 // TOK_TILE, ci * chunk_tok // TOK_TILE, N_CTX // TOK_TILE,
                                "combine_%s%d" % (group, ci))
        outs.append(out)
    y_prompt = outs[0].reshape(BATCH, SEQ, D_MODEL)
    y_sample = outs[1].reshape(DEC_BATCH, DEC_SEQ, D_MODEL)
    return (y_prompt, y_sample, s_f[:, None], s_b[:, None])
```

```python
import functools

import jax
import jax.numpy as jnp
from jax import lax
from jax.experimental import pallas as pl
from jax.experimental.pallas import tpu as pltpu
from jax.experimental.pallas import tpu_sc as plsc

F32 = jnp.float32
BF16 = jnp.bfloat16
I32 = jnp.int32

D_MODEL = 1024
BATCH = 32
SEQ = 256
DEC_BATCH = 4
DEC_SEQ = 2048
GRID_W = 64
GLA_HEADS = 4
GLA_DK = 128
GLA_DV = 256
GLA_LOWRANK = 16
GLA_TAU = 16.0
GLA_CHUNK = 64
POOL_GROUPS = 4
POOL_GROUP_DIM = 128
POOL_WINDOWS = (2, 4, 8, 16)
N_EXPERTS = 32
TOP_K = 4
D_FF = 1024
SWIGLU_LIMIT = 7.0
SWIGLU_ALPHA = 1.702
MOE_BLOCK = 256
NORM_EPS = 1e-6
N_MOD = 6

QK_W = GLA_HEADS * GLA_DK
V_W = GLA_HEADS * GLA_DV
POOL_W = POOL_GROUPS * POOL_GROUP_DIM
MAIN_W = 2 * QK_W + 2 * V_W
ALR_W = 2 * GLA_LOWRANK
MG_W = 2 * D_MODEL

N_CTX = BATCH * SEQ
N_LAT = DEC_BATCH * DEC_SEQ
N_TOK = N_CTX + N_LAT
N_SLOT_BLOCKS = -(-(N_TOK * TOP_K + N_EXPERTS * (MOE_BLOCK - 1)) // MOE_BLOCK)
N_SLOTS = N_SLOT_BLOCKS * MOE_BLOCK

LANES = 128
TOK_TILE = 256
N_TILES = N_TOK // TOK_TILE
CTX_TILES = N_CTX // TOK_TILE
LAT_TILES_PER_SEQ = DEC_SEQ // TOK_TILE
ROUTE_TILE = 512
VMEM_LIMIT = 56 * 1024 * 1024

GLA_SEQS = 4
CTX_CHUNKS = SEQ // GLA_CHUNK
LAT_CHUNKS = DEC_SEQ // GLA_CHUNK
CHUNKS_PER_TILE = TOK_TILE // GLA_CHUNK
GLA_CTX_STEPS = (BATCH // GLA_SEQS) * CTX_CHUNKS
TILE_GRID = 8

NT_DIMS = (((1,), (1,)), ((), ()))
TN_DIMS = (((0,), (0,)), ((), ()))

assert DEC_BATCH == GLA_SEQS and SEQ == TOK_TILE and N_TILES == TILE_GRID * TILE_GRID


def _params(semantics, vmem=VMEM_LIMIT):
    return pltpu.CompilerParams(dimension_semantics=semantics, vmem_limit_bytes=vmem)


def _split_bf16(a):
    hi = a.astype(BF16)
    lo = (a - hi.astype(F32)).astype(BF16)
    return hi, lo


def _dot(a, b):
    return jnp.dot(a, b, preferred_element_type=F32)


def _dot3(a, b):
    a_hi, a_lo = _split_bf16(a)
    b_hi, b_lo = _split_bf16(b)
    return _dot(a_hi, b_hi) + _dot(a_lo, b_hi) + _dot(a_hi, b_lo)


def _sigmoid(x):
    return 1.0 / (1.0 + jnp.exp(-x))


HALF_W = D_MODEL // 2
HIGH_HALF_MASK = -65536


def _pack_halves(lo, hi):
    lo_bits = pltpu.bitcast(lo.astype(BF16).astype(F32), I32)
    hi_bits = pltpu.bitcast(hi.astype(BF16).astype(F32), I32)
    return lax.shift_right_logical(lo_bits, 16) | (hi_bits & HIGH_HALF_MASK)


def _unpack_halves(words):
    lo = pltpu.bitcast(lax.shift_left(words, 16), F32)
    hi = pltpu.bitcast(words & HIGH_HALF_MASK, F32)
    return lo, hi


def _rms(x):
    return x * lax.rsqrt(jnp.mean(x * x, axis=-1, keepdims=True) + NORM_EPS)


def _mod_row(t):
    return jnp.where(t < CTX_TILES, 0, 1 + (t - CTX_TILES) // LAT_TILES_PER_SEQ)


def _store_tile(t):
    u = t - CTX_TILES
    return jnp.where(t < CTX_TILES, t,
                     CTX_TILES + DEC_BATCH * (u % LAT_TILES_PER_SEQ) + u // LAT_TILES_PER_SEQ)


def _ctx_tile(t):
    return jnp.minimum(t, CTX_TILES - 1)


def _lat_tile(t):
    return jnp.maximum(t - CTX_TILES, 0)


def _mod_kernel(c_ref, w_ref, b_ref, o_ref):
    c = c_ref[...]
    o_ref[...] = _dot3(c * _sigmoid(c), w_ref[...]) + b_ref[...]


def _mod_call(cvec, w_mod, b_mod):
    rows = cvec.shape[0]
    return pl.pallas_call(
        _mod_kernel,
        out_shape=jax.ShapeDtypeStruct((rows, N_MOD * D_MODEL), F32),
        grid=(N_MOD,),
        in_specs=[
            pl.BlockSpec((rows, D_MODEL), lambda j: (0, 0)),
            pl.BlockSpec((D_MODEL, D_MODEL), lambda j: (0, j)),
            pl.BlockSpec((1, D_MODEL), lambda j: (0, j)),
        ],
        out_specs=pl.BlockSpec((rows, D_MODEL), lambda j: (0, j)),
        compiler_params=_params(("arbitrary",)),
        name="mod",
    )(cvec, w_mod, b_mod)


def _inproj_kernel(xc_ref, xl_ref, mod_ref, g_ref, wmain_ref, walr_ref, wxp_ref, wmg_ref,
                   q_ref, k_ref, v_ref, og_ref, alr_ref, xp_ref, mg_ref):
    t = pl.program_id(0)
    x = jnp.where(t < CTX_TILES, xc_ref[...], xl_ref[...])
    mod = mod_ref[0]
    shift1 = mod[:, 0:D_MODEL]
    scale1 = mod[:, D_MODEL:2 * D_MODEL]
    h = _rms(x) * g_ref[...]
    h = (h * (1.0 + scale1) + shift1).astype(BF16)
    z = _dot(h, wmain_ref[...])
    q_ref[...] = (z[:, 0:QK_W] * (GLA_DK ** -0.5)).astype(BF16)
    k_ref[...] = z[:, QK_W:2 * QK_W].astype(BF16)
    v_ref[...] = z[:, 2 * QK_W:2 * QK_W + V_W].astype(BF16)
    og_ref[...] = z[:, 2 * QK_W + V_W:MAIN_W].astype(BF16)
    alr_ref[...] = _dot(h, walr_ref[...])
    xp_ref[...] = _dot(h, wxp_ref[...])
    mg_ref[...] = _dot(h, wmg_ref[...]).astype(BF16)


def _inproj_call(x_ctx, x_lat, mod3, norm1_g, w_main, w_alr, w_xp, w_mg):
    const = lambda t: (0, 0)
    stored = lambda t: (_store_tile(t), 0)
    widths = (QK_W, QK_W, V_W, V_W, ALR_W, POOL_W, MG_W)
    dtypes = (BF16, BF16, BF16, BF16, F32, F32, BF16)
    return pl.pallas_call(
        _inproj_kernel,
        out_shape=[jax.ShapeDtypeStruct((N_TOK, w), dt) for w, dt in zip(widths, dtypes)],
        grid=(N_TILES,),
        in_specs=[
            pl.BlockSpec((TOK_TILE, D_MODEL), lambda t: (_ctx_tile(t), 0)),
            pl.BlockSpec((TOK_TILE, D_MODEL), lambda t: (_lat_tile(t), 0)),
            pl.BlockSpec((1, 1, N_MOD * D_MODEL), lambda t: (_mod_row(t), 0, 0)),
            pl.BlockSpec((1, D_MODEL), const),
            pl.BlockSpec((D_MODEL, MAIN_W), const),
            pl.BlockSpec((D_MODEL, ALR_W), const),
            pl.BlockSpec((D_MODEL, POOL_W), const),
            pl.BlockSpec((D_MODEL, MG_W), const),
        ],
        out_specs=[pl.BlockSpec((TOK_TILE, w), stored) for w in widths],
        compiler_params=_params(("arbitrary",)),
        name="inproj",
    )(x_ctx, x_lat, mod3, norm1_g, w_main, w_alr, w_xp, w_mg)


def _gla_direction(q_ref, k_ref, v_ref, alr_ref, wa_ref, ba_ref, o_ref, st_ref, slot0, rev):
    rows = GLA_SEQS * GLA_CHUNK
    alr = jnp.concatenate([alr_ref[0, s] for s in range(GLA_SEQS)], axis=0)
    a = _dot3(alr, wa_ref[...]) + ba_ref[...]
    g = (jnp.minimum(a, 0.0) - jnp.log(1.0 + jnp.exp(-jnp.abs(a)))) * (1.0 / GLA_TAU)

    row = lax.broadcasted_iota(I32, (rows, rows), 0)
    col = lax.broadcasted_iota(I32, (rows, rows), 1)
    same = (row // GLA_CHUNK) == (col // GLA_CHUNK)
    tri_all = same & ((col >= row) if rev else (col <= row))
    tri_b = jnp.where(tri_all, 1.0, 0.0).astype(BF16)
    g_hi, g_lo = _split_bf16(g)
    bcum_all = _dot(tri_b, g_hi) + _dot(tri_b, g_lo)

    r64 = lax.broadcasted_iota(I32, (GLA_CHUNK, GLA_CHUNK), 0)
    c64 = lax.broadcasted_iota(I32, (GLA_CHUNK, GLA_CHUNK), 1)
    tri = (c64 >= r64) if rev else (c64 <= r64)

    for s in range(GLA_SEQS):
        bcum = bcum_all[s * GLA_CHUNK:(s + 1) * GLA_CHUNK]
        blast = bcum[0:1] if rev else bcum[GLA_CHUNK - 1:GLA_CHUNK]
        bmid = bcum[GLA_CHUNK // 2:GLA_CHUNK // 2 + 1]
        e_q = jnp.exp(bcum - bmid)
        e_k = jnp.exp(bmid - bcum)
        e_in = jnp.exp(bcum)
        e_out = jnp.exp(blast - bcum)
        e_last = jnp.exp(blast)
        q = q_ref[0, s].astype(F32)
        k = k_ref[0, s].astype(F32)
        for h in range(GLA_HEADS):
            ks = slice(h * GLA_DK, (h + 1) * GLA_DK)
            vs = slice(h * GLA_DV, (h + 1) * GLA_DV)
            qh = q[:, ks]
            kh = k[:, ks]
            vh = v_ref[0, s, :, vs]
            att = lax.dot_general((qh * e_q[:, ks]).astype(BF16), (kh * e_k[:, ks]).astype(BF16),
                                  NT_DIMS, preferred_element_type=F32)
            att = jnp.where(tri, att, 0.0).astype(BF16)
            st = st_ref[slot0 + s, h]
            o_inter = lax.dot_general((qh * e_in[:, ks]).astype(BF16), st.astype(BF16),
                                      NT_DIMS, preferred_element_type=F32)
            o_ref[0, s, :, vs] = o_inter + _dot(att, vh)
            upd = lax.dot_general(vh, (kh * e_out[:, ks]).astype(BF16), TN_DIMS,
                                  preferred_element_type=F32)
            st_ref[slot0 + s, h] = st * e_last[:, ks] + upd


def _gla_kernel(qf_ref, kf_ref, vf_ref, af_ref, qb_ref, kb_ref, vb_ref, ab_ref,
                waf_ref, baf_ref, wab_ref, bab_ref, s0f_ref, s0b_ref,
                of_ref, ob_ref, sf_ref, sb_ref, st_ref):
    i = pl.program_id(0)
    is_ctx = i < GLA_CTX_STEPS
    chunk = jnp.where(is_ctx, i % CTX_CHUNKS, i - GLA_CTX_STEPS)

    @pl.when(is_ctx & (chunk == 0))
    def _():
        st_ref[...] = jnp.zeros(st_ref.shape, F32)

    @pl.when(i == GLA_CTX_STEPS)
    def _():
        for s in range(GLA_SEQS):
            for h in range(GLA_HEADS):
                st_ref[s, h] = s0f_ref[s, h].T
                st_ref[GLA_SEQS + s, h] = s0b_ref[s, h].T

    _gla_direction(qf_ref, kf_ref, vf_ref, af_ref, waf_ref, baf_ref, of_ref, st_ref, 0, False)
    _gla_direction(qb_ref, kb_ref, vb_ref, ab_ref, wab_ref, bab_ref, ob_ref, st_ref, GLA_SEQS, True)

    @pl.when(is_ctx & (chunk == CTX_CHUNKS - 1))
    def _():
        for s in range(GLA_SEQS):
            for h in range(GLA_HEADS):
                sf_ref[s, h] = st_ref[s, h].T
                sb_ref[s, h] = st_ref[GLA_SEQS + s, h].T


def _gla_block(i, rev):
    is_ctx = i < GLA_CTX_STEPS
    group = i // CTX_CHUNKS
    c_ctx = i % CTX_CHUNKS
    c_lat = i - GLA_CTX_STEPS
    if rev:
        c_ctx = CTX_CHUNKS - 1 - c_ctx
        c_lat = LAT_CHUNKS - 1 - c_lat
    j = c_lat // CHUNKS_PER_TILE
    per_row = TILE_GRID // GLA_SEQS
    a = jnp.where(is_ctx, group // per_row, CTX_TILES // TILE_GRID + j // per_row)
    b = jnp.where(is_ctx, group % per_row, j % per_row)
    c = jnp.where(is_ctx, c_ctx, c_lat % CHUNKS_PER_TILE)
    return (a, b, c, 0)


def _gla_call(q, k, v, alr, wa_f, ba_f, wa_b, ba_b, s0_f, s0_b):
    def view(arr):
        return arr.reshape(TILE_GRID, TILE_GRID, TOK_TILE, arr.shape[-1])

    def spec(width, rev):
        return pl.BlockSpec((1, GLA_SEQS, GLA_CHUNK, width), lambda i: _gla_block(i, rev))

    const = lambda i: (0, 0)
    st_block = (GLA_SEQS, GLA_HEADS, GLA_DK, GLA_DV)
    whole_state = pl.BlockSpec(st_block, lambda i: (0, 0, 0, 0))
    ctx_state = pl.BlockSpec(
        st_block, lambda i: (jnp.minimum(i // CTX_CHUNKS, BATCH // GLA_SEQS - 1), 0, 0, 0))
    in_specs = []
    for rev in (False, True):
        in_specs += [spec(QK_W, rev), spec(QK_W, rev), spec(V_W, rev), spec(ALR_W, rev)]
    in_specs += [pl.BlockSpec((ALR_W, QK_W), const), pl.BlockSpec((1, QK_W), const)] * 2
    in_specs += [whole_state, whole_state]
    o_shape = jax.ShapeDtypeStruct((TILE_GRID, TILE_GRID, TOK_TILE, V_W), F32)
    s_shape = jax.ShapeDtypeStruct((BATCH, GLA_HEADS, GLA_DK, GLA_DV), F32)
    qv, kv, vv, av = view(q), view(k), view(v), view(alr)
    o_f, o_b, s_f, s_b = pl.pallas_call(
        _gla_kernel,
        out_shape=[o_shape, o_shape, s_shape, s_shape],
        grid=(GLA_CTX_STEPS + LAT_CHUNKS,),
        in_specs=in_specs,
        out_specs=[spec(V_W, False), spec(V_W, True), ctx_state, ctx_state],
        scratch_shapes=[pltpu.VMEM((2 * GLA_SEQS, GLA_HEADS, GLA_DV, GLA_DK), F32)],
        compiler_params=_params(("arbitrary",)),
        name="gla",
    )(qv, kv, vv, av, qv, kv, vv, av, wa_f, ba_f, wa_b, ba_b, s0_f, s0_b)
    return o_f.reshape(N_TOK, V_W), o_b.reshape(N_TOK, V_W), s_f, s_b


def _band(n, w, block):
    row = lax.broadcasted_iota(I32, (n, n), 0)
    col = lax.broadcasted_iota(I32, (n, n), 1)
    inside = (col >= row - w // 2) & (col <= row + w // 2 - 1)
    if block < n:
        inside = inside & ((row // block) == (col // block))
    return jnp.where(inside, 1.0, 0.0).astype(BF16)


def _win_count(p, n, w):
    return jnp.minimum(p + w // 2 - 1, n - 1) - jnp.maximum(p - w // 2, 0) + 1


def _pool_ctx_kernel(x_ref, o_ref, band_ref):
    @pl.when(pl.program_id(0) == 0)
    def _():
        for gi, w in enumerate(POOL_WINDOWS):
            band_ref[gi] = _band(SEQ, w, SEQ)

    p = lax.broadcasted_iota(I32, (SEQ, POOL_GROUP_DIM), 0)
    for gi, w in enumerate(POOL_WINDOWS):
        cs = slice(gi * POOL_GROUP_DIM, (gi + 1) * POOL_GROUP_DIM)
        x = x_ref[:, cs]
        hi, lo = _split_bf16(x)
        band = band_ref[gi]
        s = _dot(band, hi) + _dot(band, lo)
        cnt = _win_count(p, SEQ, w).astype(F32)
        o_ref[:, cs] = s / cnt - x


def _pool_ctx_call(xp):
    spec = pl.BlockSpec((SEQ, POOL_W), lambda b: (b, 0))
    return pl.pallas_call(
        _pool_ctx_kernel,
        out_shape=jax.ShapeDtypeStruct((N_CTX, POOL_W), F32),
        grid=(BATCH,),
        in_specs=[spec],
        out_specs=spec,
        scratch_shapes=[pltpu.VMEM((POOL_GROUPS, SEQ, SEQ), BF16)],
        compiler_params=_params(("arbitrary",)),
        name="pool_ctx",
    )(xp)


POOL_HALO = (max(POOL_WINDOWS) // 2) * GRID_W


def _pool_lat_kernel(x_ref, o_ref, pad_ref):
    rows = DEC_SEQ // GRID_W
    p = lax.broadcasted_iota(I32, (DEC_SEQ, POOL_GROUP_DIM), 0)
    r = p // GRID_W
    cidx = p % GRID_W
    zeros = jnp.zeros((POOL_HALO, POOL_GROUP_DIM), F32)
    pad_ref[0:POOL_HALO, :] = zeros
    pad_ref[POOL_HALO + DEC_SEQ:2 * POOL_HALO + DEC_SEQ, :] = zeros
    for gi, w in enumerate(POOL_WINDOWS):
        cs = slice(gi * POOL_GROUP_DIM, (gi + 1) * POOL_GROUP_DIM)
        band = _band(TOK_TILE, w, GRID_W)
        for t in range(LAT_TILES_PER_SEQ):
            hi, lo = _split_bf16(x_ref[t, 0, :, cs])
            pad_ref[POOL_HALO + t * TOK_TILE:POOL_HALO + (t + 1) * TOK_TILE, :] = (
                _dot(band, hi) + _dot(band, lo))
        acc = jnp.zeros((DEC_SEQ, POOL_GROUP_DIM), F32)
        for dr in range(-(w // 2), w // 2):
            start = POOL_HALO + dr * GRID_W
            acc = acc + pad_ref[start:start + DEC_SEQ, :]
        cnt = (_win_count(r, rows, w) * _win_count(cidx, GRID_W, w)).astype(F32)
        pooled = acc / cnt
        for t in range(LAT_TILES_PER_SEQ):
            rs = slice(t * TOK_TILE, (t + 1) * TOK_TILE)
            o_ref[t, 0, :, cs] = pooled[rs] - x_ref[t, 0, :, cs]


def _pool_lat_call(xp):
    view = xp.reshape(N_TILES // DEC_BATCH, DEC_BATCH, TOK_TILE, POOL_W)
    blk = (LAT_TILES_PER_SEQ, 1, TOK_TILE, POOL_W)
    out = pl.pallas_call(
        _pool_lat_kernel,
        out_shape=jax.ShapeDtypeStruct((LAT_TILES_PER_SEQ, DEC_BATCH, TOK_TILE, POOL_W), F32),
        grid=(DEC_BATCH,),
        in_specs=[pl.BlockSpec(blk, lambda s: (CTX_TILES // DEC_BATCH // LAT_TILES_PER_SEQ, s, 0, 0))],
        out_specs=pl.BlockSpec(blk, lambda s: (0, s, 0, 0)),
        scratch_shapes=[pltpu.VMEM((DEC_SEQ + 2 * POOL_HALO, POOL_GROUP_DIM), F32)],
        compiler_params=_params(("arbitrary",)),
        name="pool_lat",
    )(view)
    return out.reshape(N_LAT, POOL_W)


def _post_kernel(xc_ref, xl_ref, mod_ref, of_ref, ob_ref, og_ref, pc_ref, pl_ref, mg_ref, gng_ref,
                 wpg_ref, psc_ref, wbg_ref, wbp_ref, wout_ref, n2g_ref, wr_ref, br_ref,
                 x1_ref, h2_ref, idx_ref, tw_ref):
    t = pl.program_id(0)
    is_ctx = t < CTX_TILES
    x = jnp.where(is_ctx, xc_ref[...], xl_ref[...])
    pooled = jnp.where(is_ctx, pc_ref[...], pl_ref[...])
    mod = mod_ref[0]
    gate1 = mod[:, 2 * D_MODEL:3 * D_MODEL]
    shift2 = mod[:, 3 * D_MODEL:4 * D_MODEL]
    scale2 = mod[:, 4 * D_MODEL:5 * D_MODEL]

    o = of_ref[...] + ob_ref[...]
    og = og_ref[...].astype(F32)
    gated = []
    for h in range(GLA_HEADS):
        vs = slice(h * GLA_DV, (h + 1) * GLA_DV)
        oh = _rms(o[:, vs]) * gng_ref[:, vs]
        ogh = og[:, vs]
        gated.append((oh * (ogh * _sigmoid(ogh))).astype(BF16))
    br_gla = _dot(jnp.concatenate(gated, axis=-1), wbg_ref[...])

    pm = []
    for gi in range(POOL_GROUPS):
        cs = slice(gi * POOL_GROUP_DIM, (gi + 1) * POOL_GROUP_DIM)
        pmg = _dot(pooled[:, cs].astype(BF16), wpg_ref[gi]) * psc_ref[:, cs]
        pm.append(pmg.astype(BF16))
    br_pool = _dot(jnp.concatenate(pm, axis=-1), wbp_ref[...])

    mg = mg_ref[...].astype(F32)
    merged = _sigmoid(mg[:, 0:D_MODEL]) * br_gla + _sigmoid(mg[:, D_MODEL:MG_W]) * br_pool
    m = _dot(merged.astype(BF16), wout_ref[...])
    x1 = x + gate1 * m
    x1_ref[...] = x1
    h2 = _rms(x1) * n2g_ref[...]
    h2 = h2 * (1.0 + scale2) + shift2
    h2_ref[...] = _pack_halves(h2[:, :HALF_W], h2[:, HALF_W:])

    logits = _dot3(h2, wr_ref[...]) + br_ref[...]
    lane = lax.broadcasted_iota(I32, (TOK_TILE, LANES), 1)
    lane_f = lane.astype(F32)
    neg = jnp.float32(-jnp.inf)
    cur = jnp.where(lane < N_EXPERTS, logits, neg)
    vals, idxs = [], []
    for _ in range(TOP_K):
        mx = jnp.max(cur, axis=-1, keepdims=True)
        ix = jnp.min(jnp.where(cur == mx, lane_f, float(LANES)), axis=-1, keepdims=True)
        vals.append(mx)
        idxs.append(ix)
        cur = jnp.where(lane_f == ix, neg, cur)
    ex = [jnp.exp(vv - vals[0]) for vv in vals]
    tot = ex[0] + ex[1] + ex[2] + ex[3]
    idx_out = jnp.zeros((TOK_TILE, LANES), F32)
    w_out = jnp.zeros((TOK_TILE, LANES), F32)
    for kk in range(TOP_K):
        idx_out = jnp.where(lane == kk, idxs[kk], idx_out)
        w_out = jnp.where(lane == kk, ex[kk] / tot, w_out)
    idx_ref[...] = idx_out.astype(I32)
    tw_ref[...] = w_out


def _post_call(x_ctx, x_lat, mod3, o_f, o_b, og, pooled_c, pooled_l, mg, gng, wpg, psc, wbg, wbp,
               wout, n2g, wr, br):
    row = lambda t: (t, 0)
    const = lambda t: (0, 0)
    stored = lambda t: (_store_tile(t), 0)
    return pl.pallas_call(
        _post_kernel,
        out_shape=[
            jax.ShapeDtypeStruct((N_TOK, D_MODEL), F32),
            jax.ShapeDtypeStruct((N_TOK, HALF_W), I32),
            jax.ShapeDtypeStruct((N_TOK, LANES), I32),
            jax.ShapeDtypeStruct((N_TOK, LANES), F32),
        ],
        grid=(N_TILES,),
        in_specs=[
            pl.BlockSpec((TOK_TILE, D_MODEL), lambda t: (_ctx_tile(t), 0)),
            pl.BlockSpec((TOK_TILE, D_MODEL), lambda t: (_lat_tile(t), 0)),
            pl.BlockSpec((1, 1, N_MOD * D_MODEL), lambda t: (_mod_row(t), 0, 0)),
            pl.BlockSpec((TOK_TILE, V_W), stored),
            pl.BlockSpec((TOK_TILE, V_W), stored),
            pl.BlockSpec((TOK_TILE, V_W), stored),
            pl.BlockSpec((TOK_TILE, POOL_W), lambda t: (_ctx_tile(t), 0)),
            pl.BlockSpec((TOK_TILE, POOL_W),
                         lambda t: (jnp.maximum(_store_tile(t) - CTX_TILES, 0), 0)),
            pl.BlockSpec((TOK_TILE, MG_W), stored),
            pl.BlockSpec((1, V_W), const),
            pl.BlockSpec((POOL_GROUPS, POOL_GROUP_DIM, POOL_GROUP_DIM), lambda t: (0, 0, 0)),
            pl.BlockSpec((1, POOL_W), const),
            pl.BlockSpec((V_W, D_MODEL), const),
            pl.BlockSpec((POOL_W, D_MODEL), const),
            pl.BlockSpec((D_MODEL, D_MODEL), const),
            pl.BlockSpec((1, D_MODEL), const),
            pl.BlockSpec((D_MODEL, LANES), const),
            pl.BlockSpec((1, LANES), const),
        ],
        out_specs=[
            pl.BlockSpec((TOK_TILE, D_MODEL), row),
            pl.BlockSpec((TOK_TILE, HALF_W), row),
            pl.BlockSpec((TOK_TILE, LANES), row),
            pl.BlockSpec((TOK_TILE, LANES), row),
        ],
        compiler_params=_params(("arbitrary",)),
        name="post",
    )(x_ctx, x_lat, mod3, o_f, o_b, og, pooled_c, pooled_l, mg, gng, wpg, psc, wbg, wbp, wout, n2g,
      wr, br)


def _route_kernel(idx_ref, rank_ref, cnt_ref, carry_ref, strict_ref):
    t = pl.program_id(0)

    @pl.when(t == 0)
    def _():
        carry_ref[...] = jnp.zeros((1, LANES), F32)
        row = lax.broadcasted_iota(I32, (ROUTE_TILE, ROUTE_TILE), 0)
        col = lax.broadcasted_iota(I32, (ROUTE_TILE, ROUTE_TILE), 1)
        strict_ref[...] = jnp.where(col < row, 1.0, 0.0).astype(BF16)

    idx = idx_ref[...]
    lane = lax.broadcasted_iota(I32, (ROUTE_TILE, LANES), 1)
    sel = [lane == idx[:, kk:kk + 1] for kk in range(TOP_K)]
    onehot = jnp.zeros((ROUTE_TILE, LANES), F32)
    for kk in range(TOP_K):
        onehot = onehot + jnp.where(sel[kk], 1.0, 0.0)
    before = _dot(strict_ref[...], onehot.astype(BF16)) + carry_ref[...]
    rank = jnp.zeros((ROUTE_TILE, LANES), F32)
    for kk in range(TOP_K):
        rk = jnp.sum(jnp.where(sel[kk], before, 0.0), axis=-1, keepdims=True)
        rank = jnp.where(lane == kk, rk, rank)
    rank_ref[...] = rank.astype(I32)
    carry_ref[...] = carry_ref[...] + jnp.sum(onehot, axis=0, keepdims=True)
    cnt_ref[...] = jnp.broadcast_to(carry_ref[...], (8, LANES))


def _route_call(idx):
    return pl.pallas_call(
        _route_kernel,
        out_shape=[
            jax.ShapeDtypeStruct((N_TOK, LANES), I32),
            jax.ShapeDtypeStruct((8, LANES), F32),
        ],
        grid=(N_TOK // ROUTE_TILE,),
        in_specs=[pl.BlockSpec((ROUTE_TILE, LANES), lambda t: (t, 0))],
        out_specs=[
            pl.BlockSpec((ROUTE_TILE, LANES), lambda t: (t, 0)),
            pl.BlockSpec((8, LANES), lambda t: (0, 0)),
        ],
        scratch_shapes=[pltpu.VMEM((1, LANES), F32), pltpu.VMEM((ROUTE_TILE, ROUTE_TILE), BF16)],
        compiler_params=_params(("arbitrary",)),
        name="route",
    )(idx)


def _moe_kernel(be_ref, nu_ref, ne_ref, par_ref, x_ref, wg_hbm, bg_ref, wu_hbm, bu_ref, wd_hbm,
                bd_ref, y_ref, wf_ref, wgu_ref, wdb_ref, sems):
    b = pl.program_id(0)
    n_used = nu_ref[0]
    e = be_ref[b]
    prev = be_ref[jnp.maximum(b - 1, 0)]
    live = b < n_used

    def weight_copies(expert, slot):
        return [pltpu.make_async_copy(w.at[expert], wf_ref.at[slot, i], sems.at[slot])
                for i, w in enumerate((wg_hbm, wu_hbm, wd_hbm))]

    @pl.when(live & ((b == 0) | (e != prev)))
    def _():
        slot = par_ref[b]

        @pl.when(b == 0)
        def _():
            for cp in weight_copies(e, slot):
                cp.start()

        for cp in weight_copies(e, slot):
            cp.wait()
        nxt = ne_ref[b]

        @pl.when(nxt >= 0)
        def _():
            for cp in weight_copies(nxt, 1 - slot):
                cp.start()

        wgu_ref[:, :D_FF] = wf_ref[slot, 0].astype(BF16)
        wgu_ref[:, D_FF:] = wf_ref[slot, 1].astype(BF16)
        wdb_ref[...] = wf_ref[slot, 2].astype(BF16)

    @pl.when(live)
    def _():
        x_lo, x_hi = _unpack_halves(x_ref[...])
        x = jnp.concatenate([x_lo.astype(BF16), x_hi.astype(BF16)], axis=-1)
        gu = _dot(x, wgu_ref[...])
        gate = jnp.minimum(gu[:, :D_FF] + bg_ref[0], SWIGLU_LIMIT)
        up = jnp.clip(gu[:, D_FF:] + bu_ref[0], -SWIGLU_LIMIT, SWIGLU_LIMIT)
        act = (up + 1.0) * (gate * _sigmoid(SWIGLU_ALPHA * gate))
        y = _dot(act.astype(BF16), wdb_ref[...]) + bd_ref[0]
        y_ref[...] = _pack_halves(y[:, :HALF_W], y[:, HALF_W:])

    @pl.when(jnp.logical_not(live))
    def _():
        y_ref[...] = jnp.zeros((MOE_BLOCK, HALF_W), I32)


def _moe_call(block_e, n_used, next_e, parity, hs, w_gate, b_gate, w_up, b_up, w_down, b_down):
    def blk(b, be, nu, ne, par):
        return jnp.minimum(b, nu[0] - 1)

    row = lambda b, be, nu, ne, par: (blk(b, be, nu, ne, par), 0)
    bsel = lambda b, be, nu, ne, par: (be[blk(b, be, nu, ne, par)], 0, 0)
    any_spec = pl.BlockSpec(memory_space=pl.ANY)
    assert D_MODEL == D_FF
    return pl.pallas_call(
        _moe_kernel,
        out_shape=jax.ShapeDtypeStruct((N_SLOTS, HALF_W), I32),
        grid_spec=pltpu.PrefetchScalarGridSpec(
            num_scalar_prefetch=4,
            grid=(N_SLOT_BLOCKS,),
            in_specs=[
                pl.BlockSpec((MOE_BLOCK, HALF_W), row),
                any_spec,
                pl.BlockSpec((1, 1, D_FF), bsel),
                any_spec,
                pl.BlockSpec((1, 1, D_FF), bsel),
                any_spec,
                pl.BlockSpec((1, 1, D_MODEL), bsel),
            ],
            out_specs=pl.BlockSpec((MOE_BLOCK, HALF_W), lambda b, be, nu, ne, par: (b, 0)),
            scratch_shapes=[
                pltpu.VMEM((2, 3, D_MODEL, D_FF), F32),
                pltpu.VMEM((D_MODEL, 2 * D_FF), BF16),
                pltpu.VMEM((D_FF, D_MODEL), BF16),
                pltpu.SemaphoreType.DMA((2,)),
            ],
        ),
        compiler_params=_params(("arbitrary",)),
        name="moe",
    )(block_e, n_used, next_e, parity, hs, w_gate, b_gate, w_up, b_up, w_down, b_down)


SC_CORES = 2
SC_SUBCORES = 16
SC_WORKERS = SC_CORES * SC_SUBCORES
SC_ROWS = 128
COMBINE_CHUNKS = 2
assert N_CTX == N_LAT


def _sc_gather_rows(table, idx):
    n_idx = idx.shape[0]
    width = table.shape[1]
    per_worker = n_idx // SC_WORKERS
    n_chunks = per_worker // SC_ROWS
    assert n_chunks * SC_ROWS * SC_WORKERS == n_idx
    mesh = plsc.VectorSubcoreMesh(core_axis_name="c", subcore_axis_name="s")

    @functools.partial(
        pl.kernel, mesh=mesh,
        out_type=jax.ShapeDtypeStruct((n_idx, width), table.dtype),
        scratch_types=[pltpu.VMEM((SC_ROWS,), I32), pltpu.VMEM((SC_ROWS, width), table.dtype),
                       pltpu.SemaphoreType.DMA],
        name="sc_gather",
    )
    def gather(table_hbm, idx_hbm, out_hbm, idx_v, rows_v, sem):
        worker = lax.axis_index("s") * SC_CORES + lax.axis_index("c")
        base = worker * per_worker

        @pl.loop(0, n_chunks)
        def _(ch):
            off = pl.multiple_of(base + ch * SC_ROWS, SC_ROWS)
            pltpu.sync_copy(idx_hbm.at[pl.ds(off, SC_ROWS)], idx_v)
            pltpu.async_copy(table_hbm.at[idx_v], rows_v, sem).wait()
            pltpu.sync_copy(rows_v, out_hbm.at[pl.ds(off, SC_ROWS)])

    return gather(table, idx)


def _sc_scatter_rows(rows, idx3, n_out):
    n_rows, width = rows.shape
    n_chunks = n_rows // SC_ROWS // SC_WORKERS
    assert n_chunks * SC_ROWS * SC_WORKERS == n_rows and idx3.shape == (n_rows // SC_ROWS, TOP_K, SC_ROWS)
    mesh = plsc.VectorSubcoreMesh(core_axis_name="c", subcore_axis_name="s")

    @functools.partial(
        pl.kernel, mesh=mesh,
        out_type=jax.ShapeDtypeStruct((n_out, width), rows.dtype),
        scratch_types=[pltpu.VMEM((TOP_K, SC_ROWS), I32), pltpu.VMEM((SC_ROWS, width), rows.dtype),
                       pltpu.SemaphoreType.DMA],
        name="sc_scatter",
    )
    def scatter(rows_hbm, idx_hbm, out_hbm, idx_v, rows_v, sem):
        worker = lax.axis_index("s") * SC_CORES + lax.axis_index("c")

        @pl.loop(0, n_chunks)
        def _(ch):
            chunk = worker * n_chunks + ch
            pltpu.sync_copy(idx_hbm.at[chunk], idx_v)
            pltpu.sync_copy(rows_hbm.at[pl.ds(pl.multiple_of(chunk * SC_ROWS, SC_ROWS), SC_ROWS)],
                            rows_v)
            for kk in range(TOP_K):
                pltpu.async_copy(rows_v, out_hbm.at[idx_v.at[kk]], sem).wait()

    return scatter(rows, idx3)


def _combine_kernel(x1_ref, mod_ref, tw_ref, fg_ref, g_ref, *rest):
    out_ref = rest[-1]
    tw = tw_ref[...]
    f_lo = jnp.zeros((TOK_TILE, HALF_W), F32)
    f_hi = jnp.zeros((TOK_TILE, HALF_W), F32)
    for kk in range(TOP_K):
        lo, hi = _unpack_halves(g_ref[kk])
        f_lo = f_lo + lo * tw[:, kk:kk + 1]
        f_hi = f_hi + hi * tw[:, kk:kk + 1]
    gate2 = mod_ref[0][:, 5 * D_MODEL:6 * D_MODEL]
    x2 = x1_ref[...] + gate2 * jnp.concatenate([f_lo, f_hi], axis=-1)
    out_ref[...] = _rms(x2) * fg_ref[...]


def _combine_call(x1, mod3, tw, final_g, gathered, partial, tile0, out_tile0, out_tiles, name):
    n_tiles = gathered.shape[1] // TOK_TILE
    in_specs = [
        pl.BlockSpec((TOK_TILE, D_MODEL), lambda t: (tile0 + t, 0)),
        pl.BlockSpec((1, 1, N_MOD * D_MODEL), lambda t: (_mod_row(tile0 + t), 0, 0)),
        pl.BlockSpec((TOK_TILE, LANES), lambda t: (tile0 + t, 0)),
        pl.BlockSpec((1, D_MODEL), lambda t: (0, 0)),
        pl.BlockSpec((TOP_K, TOK_TILE, HALF_W), lambda t: (0, t, 0)),
    ]
    args = [x1, mod3, tw, final_g, gathered]
    aliases = {}
    if partial is not None:
        in_specs.append(pl.BlockSpec(memory_space=pl.ANY))
        args.append(partial)
        aliases = {len(args) - 1: 0}
    return pl.pallas_call(
        _combine_kernel,
        out_shape=jax.ShapeDtypeStruct((out_tiles * TOK_TILE, D_MODEL), F32),
        grid=(n_tiles,),
        in_specs=in_specs,
        out_specs=pl.BlockSpec((TOK_TILE, D_MODEL), lambda t: (out_tile0 + t, 0)),
        input_output_aliases=aliases,
        compiler_params=_params(("arbitrary",)),
        name=name,
    )(*args)


def kernel(x_prompt, x_sample, state_gla_fwd, state_gla_bwd, c, c_ctx, norm1_g, w_mod, b_mod, w_in,
           w_alpha, b_alpha, gla_norm_g, w_pool_grp, pool_scale, w_branch_gla, w_branch_pool, w_out,
           norm2_g, w_router, b_router, w_gate, b_gate, w_up, b_up, w_down, b_down, final_norm_g):
    l = 0
    x_ctx = x_prompt.reshape(N_CTX, D_MODEL)
    x_lat = x_sample.reshape(N_LAT, D_MODEL)

    cvec = jnp.concatenate([c_ctx[None, :], c, jnp.zeros((8 - 1 - DEC_BATCH, D_MODEL), F32)], axis=0)
    mod = _mod_call(cvec, w_mod[l], b_mod[l][None, :])
    mod3 = mod.reshape(8, 1, N_MOD * D_MODEL)

    w_in_b = w_in[l].astype(BF16)
    w_main = w_in_b[:, :MAIN_W]
    w_alr = w_in_b[:, MAIN_W:MAIN_W + ALR_W]
    w_xp = w_in_b[:, MAIN_W + ALR_W:MAIN_W + ALR_W + POOL_W]
    w_mg = w_in_b[:, MAIN_W + ALR_W + POOL_W:]
    q, k, v, og, alr, xp, mg = _inproj_call(x_ctx, x_lat, mod3, norm1_g[l][None, :],
                                            w_main, w_alr, w_xp, w_mg)

    zpad = jnp.zeros((GLA_LOWRANK, QK_W), F32)
    wa_f = jnp.concatenate([w_alpha[l, 0], zpad], axis=0)
    wa_b = jnp.concatenate([zpad, w_alpha[l, 1]], axis=0)
    o_f, o_b, s_f, s_b = _gla_call(q, k, v, alr, wa_f, b_alpha[l, 0][None, :], wa_b,
                                   b_alpha[l, 1][None, :], state_gla_fwd[:, l], state_gla_bwd[:, l])

    pooled_c = _pool_ctx_call(xp)
    pooled_l = _pool_lat_call(xp)

    w_router_pad = jnp.pad(w_router[l], ((0, 0), (0, LANES - N_EXPERTS)))
    b_router_pad = jnp.pad(b_router[l], (0, LANES - N_EXPERTS))[None, :]
    x1, h2, top_idx, top_w = _post_call(
        x_ctx, x_lat, mod3, o_f, o_b, og, pooled_c, pooled_l, mg,
        gla_norm_g[l].reshape(1, V_W), w_pool_grp[l].astype(BF16), pool_scale[l][None, :],
        w_branch_gla[l].astype(BF16), w_branch_pool[l].astype(BF16), w_out[l].astype(BF16),
        norm2_g[l][None, :], w_router_pad, b_router_pad)

    rank, cnt = _route_call(top_idx)
    counts = cnt[0, :N_EXPERTS].astype(I32)
    padded = (counts + MOE_BLOCK - 1) // MOE_BLOCK * MOE_BLOCK
    pad_end = jnp.cumsum(padded).astype(I32)
    pad_start = pad_end - padded
    block_first = jnp.arange(N_SLOT_BLOCKS, dtype=I32) * MOE_BLOCK
    block_e = jnp.minimum(jnp.sum((pad_end[None, :] <= block_first[:, None]).astype(I32), axis=1),
                          N_EXPERTS - 1).astype(I32)
    n_used = (pad_end[-1:] // MOE_BLOCK).astype(I32)
    run_start = jnp.concatenate([jnp.ones((1,), I32), (block_e[1:] != block_e[:-1]).astype(I32)])
    parity = ((jnp.cumsum(run_start) - 1) % 2).astype(I32)
    after = pad_end[block_e] // MOE_BLOCK
    next_e = jnp.where(after < n_used[0], block_e[jnp.minimum(after, N_SLOT_BLOCKS - 1)], -1).astype(I32)
    experts = jnp.arange(N_EXPERTS, dtype=I32)
    tk = top_idx[:, :TOP_K]
    pos = jnp.sum(jnp.where(tk[:, :, None] == experts, pad_start, 0), axis=-1) + rank[:, :TOP_K]
    pos = pos.astype(I32)
    pos_by_choice = pos.T
    pos_chunks = pos_by_choice.reshape(TOP_K, N_TOK // SC_ROWS, SC_ROWS).transpose(1, 0, 2)

    hs = _sc_scatter_rows(h2, pos_chunks, N_SLOTS)
    y = _moe_call(block_e, n_used, next_e, parity, hs,
                  w_gate[l], b_gate[l][:, None, :], w_up[l], b_up[l][:, None, :],
                  w_down[l], b_down[l][:, None, :])
    outs = []
    chunk_tok = N_CTX // COMBINE_CHUNKS
    for group, tok0 in (("ctx", 0), ("lat", N_CTX)):
        out = None
        for ci in range(COMBINE_CHUNKS):
            t0 = tok0 + ci * chunk_tok
            idx = pos_by_choice[:, t0:t0 + chunk_tok].reshape(TOP_K * chunk_tok)
            gathered = _sc_gather_rows(y, idx).reshape(TOP_K, chunk_tok, HALF_W)
            out = _combine_call(x1, mod3, top_w, final_norm_g[None, :], gathered, out,
                                t0 // TOK_TILE, ci * chunk_tok // TOK_TILE, N_CTX // TOK_TILE,
                                "combine_%s%d" % (group, ci))
        outs.append(out)
    y_prompt = outs[0].reshape(BATCH, SEQ, D_MODEL)
    y_sample = outs[1].reshape(DEC_BATCH, DEC_SEQ, D_MODEL)
    return (y_prompt, y_sample, s_f[:, None], s_b[:, None])
```

```python
import functools

import jax
import jax.numpy as jnp
from jax import lax
from jax.experimental import pallas as pl
from jax.experimental.pallas import tpu as pltpu
from jax.experimental.pallas import tpu_sc as plsc

F32 = jnp.float32
BF16 = jnp.bfloat16
I32 = jnp.int32

D_MODEL = 1024
BATCH = 32
SEQ = 256
DEC_BATCH = 4
DEC_SEQ = 2048
GRID_W = 64
GLA_HEADS = 4
GLA_DK = 128
GLA_DV = 256
GLA_LOWRANK = 16
GLA_TAU = 16.0
GLA_CHUNK = 64
POOL_GROUPS = 4
POOL_GROUP_DIM = 128
POOL_WINDOWS = (2, 4, 8, 16)
N_EXPERTS = 32
TOP_K = 4
D_FF = 1024
SWIGLU_LIMIT = 7.0
SWIGLU_ALPHA = 1.702
MOE_BLOCK = 256
NORM_EPS = 1e-6
N_MOD = 6

QK_W = GLA_HEADS * GLA_DK
V_W = GLA_HEADS * GLA_DV
POOL_W = POOL_GROUPS * POOL_GROUP_DIM
MAIN_W = 2 * QK_W + 2 * V_W
ALR_W = 2 * GLA_LOWRANK
MG_W = 2 * D_MODEL

N_CTX = BATCH * SEQ
N_LAT = DEC_BATCH * DEC_SEQ
N_TOK = N_CTX + N_LAT
N_SLOT_BLOCKS = -(-(N_TOK * TOP_K + N_EXPERTS * (MOE_BLOCK - 1)) // MOE_BLOCK)
N_SLOTS = N_SLOT_BLOCKS * MOE_BLOCK

LANES = 128
TOK_TILE = 256
N_TILES = N_TOK // TOK_TILE
CTX_TILES = N_CTX // TOK_TILE
LAT_TILES_PER_SEQ = DEC_SEQ // TOK_TILE
ROUTE_TILE = 512
VMEM_LIMIT = 56 * 1024 * 1024

GLA_SEQS = 4
CTX_CHUNKS = SEQ // GLA_CHUNK
LAT_CHUNKS = DEC_SEQ // GLA_CHUNK
CHUNKS_PER_TILE = TOK_TILE // GLA_CHUNK
GLA_CTX_STEPS = (BATCH // GLA_SEQS) * CTX_CHUNKS
TILE_GRID = 8

NT_DIMS = (((1,), (1,)), ((), ()))
TN_DIMS = (((0,), (0,)), ((), ()))

assert DEC_BATCH == GLA_SEQS and SEQ == TOK_TILE and N_TILES == TILE_GRID * TILE_GRID


def _params(semantics, vmem=VMEM_LIMIT):
    return pltpu.CompilerParams(dimension_semantics=semantics, vmem_limit_bytes=vmem)


def _split_bf16(a):
    hi = a.astype(BF16)
    lo = (a - hi.astype(F32)).astype(BF16)
    return hi, lo


def _dot(a, b):
    return jnp.dot(a, b, preferred_element_type=F32)


def _dot3(a, b):
    a_hi, a_lo = _split_bf16(a)
    b_hi, b_lo = _split_bf16(b)
    return _dot(a_hi, b_hi) + _dot(a_lo, b_hi) + _dot(a_hi, b_lo)


def _sigmoid(x):
    return 1.0 / (1.0 + jnp.exp(-x))


HALF_W = D_MODEL // 2
HIGH_HALF_MASK = -65536


def _pack_halves(lo, hi):
    lo_bits = pltpu.bitcast(lo.astype(BF16).astype(F32), I32)
    hi_bits = pltpu.bitcast(hi.astype(BF16).astype(F32), I32)
    return lax.shift_right_logical(lo_bits, 16) | (hi_bits & HIGH_HALF_MASK)


def _unpack_halves(words):
    lo = pltpu.bitcast(lax.shift_left(words, 16), F32)
    hi = pltpu.bitcast(words & HIGH_HALF_MASK, F32)
    return lo, hi


def _rms(x):
    return x * lax.rsqrt(jnp.mean(x * x, axis=-1, keepdims=True) + NORM_EPS)


def _mod_row(t):
    return jnp.where(t < CTX_TILES, 0, 1 + (t - CTX_TILES) // LAT_TILES_PER_SEQ)


def _store_tile(t):
    u = t - CTX_TILES
    return jnp.where(t < CTX_TILES, t,
                     CTX_TILES + DEC_BATCH * (u % LAT_TILES_PER_SEQ) + u // LAT_TILES_PER_SEQ)


def _ctx_tile(t):
    return jnp.minimum(t, CTX_TILES - 1)


def _lat_tile(t):
    return jnp.maximum(t - CTX_TILES, 0)


def _mod_kernel(c_ref, w_ref, b_ref, o_ref):
    c = c_ref[...]
    o_ref[...] = _dot3(c * _sigmoid(c), w_ref[...]) + b_ref[...]


def _mod_call(cvec, w_mod, b_mod):
    rows = cvec.shape[0]
    return pl.pallas_call(
        _mod_kernel,
        out_shape=jax.ShapeDtypeStruct((rows, N_MOD * D_MODEL), F32),
        grid=(N_MOD,),
        in_specs=[
            pl.BlockSpec((rows, D_MODEL), lambda j: (0, 0)),
            pl.BlockSpec((D_MODEL, D_MODEL), lambda j: (0, j)),
            pl.BlockSpec((1, D_MODEL), lambda j: (0, j)),
        ],
        out_specs=pl.BlockSpec((rows, D_MODEL), lambda j: (0, j)),
        compiler_params=_params(("arbitrary",)),
        name="mod",
    )(cvec, w_mod, b_mod)


def _inproj_kernel(xc_ref, xl_ref, mod_ref, g_ref, wmain_ref, walr_ref, wxp_ref, wmg_ref,
                   q_ref, k_ref, v_ref, og_ref, alr_ref, xp_ref, mg_ref):
    t = pl.program_id(0)
    x = jnp.where(t < CTX_TILES, xc_ref[...], xl_ref[...])
    mod = mod_ref[0]
    shift1 = mod[:, 0:D_MODEL]
    scale1 = mod[:, D_MODEL:2 * D_MODEL]
    h = _rms(x) * g_ref[...]
    h = (h * (1.0 + scale1) + shift1).astype(BF16)
    z = _dot(h, wmain_ref[...])
    q_ref[...] = (z[:, 0:QK_W] * (GLA_DK ** -0.5)).astype(BF16)
    k_ref[...] = z[:, QK_W:2 * QK_W].astype(BF16)
    v_ref[...] = z[:, 2 * QK_W:2 * QK_W + V_W].astype(BF16)
    og = z[:, 2 * QK_W + V_W:MAIN_W]
    og_ref[...] = (og * _sigmoid(og)).astype(BF16)
    alr_ref[...] = _dot(h, walr_ref[...])
    xp_ref[...] = _dot(h, wxp_ref[...])
    mg_ref[...] = _sigmoid(_dot(h, wmg_ref[...])).astype(BF16)


def _inproj_call(x_ctx, x_lat, mod3, norm1_g, w_main, w_alr, w_xp, w_mg):
    const = lambda t: (0, 0)
    stored = lambda t: (_store_tile(t), 0)
    widths = (QK_W, QK_W, V_W, V_W, ALR_W, POOL_W, MG_W)
    dtypes = (BF16, BF16, BF16, BF16, F32, F32, BF16)
    return pl.pallas_call(
        _inproj_kernel,
        out_shape=[jax.ShapeDtypeStruct((N_TOK, w), dt) for w, dt in zip(widths, dtypes)],
        grid=(N_TILES,),
        in_specs=[
            pl.BlockSpec((TOK_TILE, D_MODEL), lambda t: (_ctx_tile(t), 0)),
            pl.BlockSpec((TOK_TILE, D_MODEL), lambda t: (_lat_tile(t), 0)),
            pl.BlockSpec((1, 1, N_MOD * D_MODEL), lambda t: (_mod_row(t), 0, 0)),
            pl.BlockSpec((1, D_MODEL), const),
            pl.BlockSpec((D_MODEL, MAIN_W), const),
            pl.BlockSpec((D_MODEL, ALR_W), const),
            pl.BlockSpec((D_MODEL, POOL_W), const),
            pl.BlockSpec((D_MODEL, MG_W), const),
        ],
        out_specs=[pl.BlockSpec((TOK_TILE, w), stored) for w in widths],
        compiler_params=_params(("arbitrary",)),
        name="inproj",
    )(x_ctx, x_lat, mod3, norm1_g, w_main, w_alr, w_xp, w_mg)


def _gla_direction(q_ref, k_ref, v_ref, alr_ref, wa_ref, ba_ref, o_ref, st_ref, slot0, rev):
    rows = GLA_SEQS * GLA_CHUNK
    alr = jnp.concatenate([alr_ref[0, s] for s in range(GLA_SEQS)], axis=0)
    a = _dot3(alr, wa_ref[...]) + ba_ref[...]
    g = (jnp.minimum(a, 0.0) - jnp.log(1.0 + jnp.exp(-jnp.abs(a)))) * (1.0 / GLA_TAU)

    row = lax.broadcasted_iota(I32, (rows, rows), 0)
    col = lax.broadcasted_iota(I32, (rows, rows), 1)
    same = (row // GLA_CHUNK) == (col // GLA_CHUNK)
    tri_all = same & ((col >= row) if rev else (col <= row))
    tri_b = jnp.where(tri_all, 1.0, 0.0).astype(BF16)
    g_hi, g_lo = _split_bf16(g)
    bcum_all = _dot(tri_b, g_hi) + _dot(tri_b, g_lo)

    r64 = lax.broadcasted_iota(I32, (GLA_CHUNK, GLA_CHUNK), 0)
    c64 = lax.broadcasted_iota(I32, (GLA_CHUNK, GLA_CHUNK), 1)
    tri = (c64 >= r64) if rev else (c64 <= r64)

    for s in range(GLA_SEQS):
        bcum = bcum_all[s * GLA_CHUNK:(s + 1) * GLA_CHUNK]
        blast = bcum[0:1] if rev else bcum[GLA_CHUNK - 1:GLA_CHUNK]
        bmid = bcum[GLA_CHUNK // 2:GLA_CHUNK // 2 + 1]
        e_q = jnp.exp(bcum - bmid)
        e_k = jnp.exp(bmid - bcum)
        e_in = jnp.exp(bcum)
        e_out = jnp.exp(blast - bcum)
        e_last = jnp.exp(blast)
        q = q_ref[0, s].astype(F32)
        k = k_ref[0, s].astype(F32)
        for h in range(GLA_HEADS):
            ks = slice(h * GLA_DK, (h + 1) * GLA_DK)
            vs = slice(h * GLA_DV, (h + 1) * GLA_DV)
            qh = q[:, ks]
            kh = k[:, ks]
            vh = v_ref[0, s, :, vs]
            att = lax.dot_general((qh * e_q[:, ks]).astype(BF16), (kh * e_k[:, ks]).astype(BF16),
                                  NT_DIMS, preferred_element_type=F32)
            att = jnp.where(tri, att, 0.0).astype(BF16)
            st = st_ref[slot0 + s, h]
            o_inter = lax.dot_general((qh * e_in[:, ks]).astype(BF16), st.astype(BF16),
                                      NT_DIMS, preferred_element_type=F32)
            o_ref[0, s, :, vs] = o_inter + _dot(att, vh)
            upd = lax.dot_general(vh, (kh * e_out[:, ks]).astype(BF16), TN_DIMS,
                                  preferred_element_type=F32)
            st_ref[slot0 + s, h] = st * e_last[:, ks] + upd


def _gla_kernel(qf_ref, kf_ref, vf_ref, af_ref, qb_ref, kb_ref, vb_ref, ab_ref,
                waf_ref, baf_ref, wab_ref, bab_ref, s0f_ref, s0b_ref,
                of_ref, ob_ref, sf_ref, sb_ref, st_ref):
    i = pl.program_id(0)
    is_ctx = i < GLA_CTX_STEPS
    chunk = jnp.where(is_ctx, i % CTX_CHUNKS, i - GLA_CTX_STEPS)

    @pl.when(is_ctx & (chunk == 0))
    def _():
        st_ref[...] = jnp.zeros(st_ref.shape, F32)

    @pl.when(i == GLA_CTX_STEPS)
    def _():
        for s in range(GLA_SEQS):
            for h in range(GLA_HEADS):
                st_ref[s, h] = s0f_ref[s, h].T
                st_ref[GLA_SEQS + s, h] = s0b_ref[s, h].T

    _gla_direction(qf_ref, kf_ref, vf_ref, af_ref, waf_ref, baf_ref, of_ref, st_ref, 0, False)
    _gla_direction(qb_ref, kb_ref, vb_ref, ab_ref, wab_ref, bab_ref, ob_ref, st_ref, GLA_SEQS, True)

    @pl.when(is_ctx & (chunk == CTX_CHUNKS - 1))
    def _():
        for s in range(GLA_SEQS):
            for h in range(GLA_HEADS):
                sf_ref[s, h] = st_ref[s, h].T
                sb_ref[s, h] = st_ref[GLA_SEQS + s, h].T


def _gla_block(i, rev):
    is_ctx = i < GLA_CTX_STEPS
    group = i // CTX_CHUNKS
    c_ctx = i % CTX_CHUNKS
    c_lat = i - GLA_CTX_STEPS
    if rev:
        c_ctx = CTX_CHUNKS - 1 - c_ctx
        c_lat = LAT_CHUNKS - 1 - c_lat
    j = c_lat // CHUNKS_PER_TILE
    per_row = TILE_GRID // GLA_SEQS
    a = jnp.where(is_ctx, group // per_row, CTX_TILES // TILE_GRID + j // per_row)
    b = jnp.where(is_ctx, group % per_row, j % per_row)
    c = jnp.where(is_ctx, c_ctx, c_lat % CHUNKS_PER_TILE)
    return (a, b, c, 0)


def _gla_call(q, k, v, alr, wa_f, ba_f, wa_b, ba_b, s0_f, s0_b):
    def view(arr):
        return arr.reshape(TILE_GRID, TILE_GRID, TOK_TILE, arr.shape[-1])

    def spec(width, rev):
        return pl.BlockSpec((1, GLA_SEQS, GLA_CHUNK, width), lambda i: _gla_block(i, rev))

    const = lambda i: (0, 0)
    st_block = (GLA_SEQS, GLA_HEADS, GLA_DK, GLA_DV)
    whole_state = pl.BlockSpec(st_block, lambda i: (0, 0, 0, 0))
    ctx_state = pl.BlockSpec(
        st_block, lambda i: (jnp.minimum(i // CTX_CHUNKS, BATCH // GLA_SEQS - 1), 0, 0, 0))
    in_specs = []
    for rev in (False, True):
        in_specs += [spec(QK_W, rev), spec(QK_W, rev), spec(V_W, rev), spec(ALR_W, rev)]
    in_specs += [pl.BlockSpec((ALR_W, QK_W), const), pl.BlockSpec((1, QK_W), const)] * 2
    in_specs += [whole_state, whole_state]
    o_shape = jax.ShapeDtypeStruct((TILE_GRID, TILE_GRID, TOK_TILE, V_W), F32)
    s_shape = jax.ShapeDtypeStruct((BATCH, GLA_HEADS, GLA_DK, GLA_DV), F32)
    qv, kv, vv, av = view(q), view(k), view(v), view(alr)
    o_f, o_b, s_f, s_b = pl.pallas_call(
        _gla_kernel,
        out_shape=[o_shape, o_shape, s_shape, s_shape],
        grid=(GLA_CTX_STEPS + LAT_CHUNKS,),
        in_specs=in_specs,
        out_specs=[spec(V_W, False), spec(V_W, True), ctx_state, ctx_state],
        scratch_shapes=[pltpu.VMEM((2 * GLA_SEQS, GLA_HEADS, GLA_DV, GLA_DK), F32)],
        compiler_params=_params(("arbitrary",)),
        name="gla",
    )(qv, kv, vv, av, qv, kv, vv, av, wa_f, ba_f, wa_b, ba_b, s0_f, s0_b)
    return o_f.reshape(N_TOK, V_W), o_b.reshape(N_TOK, V_W), s_f, s_b


def _band(n, w, block):
    row = lax.broadcasted_iota(I32, (n, n), 0)
    col = lax.broadcasted_iota(I32, (n, n), 1)
    inside = (col >= row - w // 2) & (col <= row + w // 2 - 1)
    if block < n:
        inside = inside & ((row // block) == (col // block))
    return jnp.where(inside, 1.0, 0.0).astype(BF16)


def _win_count(p, n, w):
    return jnp.minimum(p + w // 2 - 1, n - 1) - jnp.maximum(p - w // 2, 0) + 1


def _pool_ctx_kernel(x_ref, o_ref, band_ref):
    @pl.when(pl.program_id(0) == 0)
    def _():
        for gi, w in enumerate(POOL_WINDOWS):
            band_ref[gi] = _band(SEQ, w, SEQ)

    p = lax.broadcasted_iota(I32, (SEQ, POOL_GROUP_DIM), 0)
    for gi, w in enumerate(POOL_WINDOWS):
        cs = slice(gi * POOL_GROUP_DIM, (gi + 1) * POOL_GROUP_DIM)
        x = x_ref[:, cs]
        hi, lo = _split_bf16(x)
        band = band_ref[gi]
        s = _dot(band, hi) + _dot(band, lo)
        cnt = _win_count(p, SEQ, w).astype(F32)
        o_ref[:, cs] = s / cnt - x


def _pool_ctx_call(xp):
    spec = pl.BlockSpec((SEQ, POOL_W), lambda b: (b, 0))
    return pl.pallas_call(
        _pool_ctx_kernel,
        out_shape=jax.ShapeDtypeStruct((N_CTX, POOL_W), F32),
        grid=(BATCH,),
        in_specs=[spec],
        out_specs=spec,
        scratch_shapes=[pltpu.VMEM((POOL_GROUPS, SEQ, SEQ), BF16)],
        compiler_params=_params(("arbitrary",)),
        name="pool_ctx",
    )(xp)


POOL_HALO = (max(POOL_WINDOWS) // 2) * GRID_W


def _pool_lat_kernel(x_ref, o_ref, pad_ref):
    rows = DEC_SEQ // GRID_W
    p = lax.broadcasted_iota(I32, (DEC_SEQ, POOL_GROUP_DIM), 0)
    r = p // GRID_W
    cidx = p % GRID_W
    zeros = jnp.zeros((POOL_HALO, POOL_GROUP_DIM), F32)
    pad_ref[0:POOL_HALO, :] = zeros
    pad_ref[POOL_HALO + DEC_SEQ:2 * POOL_HALO + DEC_SEQ, :] = zeros
    for gi, w in enumerate(POOL_WINDOWS):
        cs = slice(gi * POOL_GROUP_DIM, (gi + 1) * POOL_GROUP_DIM)
        band = _band(TOK_TILE, w, GRID_W)
        for t in range(LAT_TILES_PER_SEQ):
            hi, lo = _split_bf16(x_ref[t, 0, :, cs])
            pad_ref[POOL_HALO + t * TOK_TILE:POOL_HALO + (t + 1) * TOK_TILE, :] = (
                _dot(band, hi) + _dot(band, lo))
        acc = jnp.zeros((DEC_SEQ, POOL_GROUP_DIM), F32)
        for dr in range(-(w // 2), w // 2):
            start = POOL_HALO + dr * GRID_W
            acc = acc + pad_ref[start:start + DEC_SEQ, :]
        cnt = (_win_count(r, rows, w) * _win_count(cidx, GRID_W, w)).astype(F32)
        pooled = acc / cnt
        for t in range(LAT_TILES_PER_SEQ):
            rs = slice(t * TOK_TILE, (t + 1) * TOK_TILE)
            o_ref[t, 0, :, cs] = pooled[rs] - x_ref[t, 0, :, cs]


def _pool_lat_call(xp):
    view = xp.reshape(N_TILES // DEC_BATCH, DEC_BATCH, TOK_TILE, POOL_W)
    blk = (LAT_TILES_PER_SEQ, 1, TOK_TILE, POOL_W)
    out = pl.pallas_call(
        _pool_lat_kernel,
        out_shape=jax.ShapeDtypeStruct((LAT_TILES_PER_SEQ, DEC_BATCH, TOK_TILE, POOL_W), F32),
        grid=(DEC_BATCH,),
        in_specs=[pl.BlockSpec(blk, lambda s: (CTX_TILES // DEC_BATCH // LAT_TILES_PER_SEQ, s, 0, 0))],
        out_specs=pl.BlockSpec(blk, lambda s: (0, s, 0, 0)),
        scratch_shapes=[pltpu.VMEM((DEC_SEQ + 2 * POOL_HALO, POOL_GROUP_DIM), F32)],
        compiler_params=_params(("arbitrary",)),
        name="pool_lat",
    )(view)
    return out.reshape(N_LAT, POOL_W)


def _post_kernel(xc_ref, xl_ref, mod_ref, of_ref, ob_ref, og_ref, pc_ref, pl_ref, mg_ref, gng_ref,
                 wpg_ref, psc_ref, wbg_ref, wbp_ref, wout_ref, n2g_ref, wr_ref, br_ref,
                 x1_ref, h2_ref, idx_ref, tw_ref):
    t = pl.program_id(0)
    is_ctx = t < CTX_TILES
    x = jnp.where(is_ctx, xc_ref[...], xl_ref[...])
    pooled = jnp.where(is_ctx, pc_ref[...], pl_ref[...])
    mod = mod_ref[0]
    gate1 = mod[:, 2 * D_MODEL:3 * D_MODEL]
    shift2 = mod[:, 3 * D_MODEL:4 * D_MODEL]
    scale2 = mod[:, 4 * D_MODEL:5 * D_MODEL]

    o = of_ref[...] + ob_ref[...]
    og = og_ref[...].astype(F32)
    gated = []
    for h in range(GLA_HEADS):
        vs = slice(h * GLA_DV, (h + 1) * GLA_DV)
        oh = _rms(o[:, vs]) * gng_ref[:, vs]
        gated.append((oh * og[:, vs]).astype(BF16))
    br_gla = _dot(jnp.concatenate(gated, axis=-1), wbg_ref[...])

    pm = []
    for gi in range(POOL_GROUPS):
        cs = slice(gi * POOL_GROUP_DIM, (gi + 1) * POOL_GROUP_DIM)
        pmg = _dot(pooled[:, cs].astype(BF16), wpg_ref[gi]) * psc_ref[:, cs]
        pm.append(pmg.astype(BF16))
    br_pool = _dot(jnp.concatenate(pm, axis=-1), wbp_ref[...])

    mg = mg_ref[...].astype(F32)
    merged = mg[:, 0:D_MODEL] * br_gla + mg[:, D_MODEL:MG_W] * br_pool
    m = _dot(merged.astype(BF16), wout_ref[...])
    x1 = x + gate1 * m
    x1_ref[...] = x1
    h2 = _rms(x1) * n2g_ref[...]
    h2 = h2 * (1.0 + scale2) + shift2
    h2_ref[...] = _pack_halves(h2[:, :HALF_W], h2[:, HALF_W:])

    logits = _dot3(h2, wr_ref[...]) + br_ref[...]
    lane = lax.broadcasted_iota(I32, (TOK_TILE, LANES), 1)
    lane_f = lane.astype(F32)
    neg = jnp.float32(-jnp.inf)
    cur = jnp.where(lane < N_EXPERTS, logits, neg)
    vals, idxs = [], []
    for _ in range(TOP_K):
        mx = jnp.max(cur, axis=-1, keepdims=True)
        ix = jnp.min(jnp.where(cur == mx, lane_f, float(LANES)), axis=-1, keepdims=True)
        vals.append(mx)
        idxs.append(ix)
        cur = jnp.where(lane_f == ix, neg, cur)
    ex = [jnp.exp(vv - vals[0]) for vv in vals]
    tot = ex[0] + ex[1] + ex[2] + ex[3]
    idx_out = jnp.zeros((TOK_TILE, LANES), F32)
    w_out = jnp.zeros((TOK_TILE, LANES), F32)
    for kk in range(TOP_K):
        idx_out = jnp.where(lane == kk, idxs[kk], idx_out)
        w_out = jnp.where(lane == kk, ex[kk] / tot, w_out)
    idx_ref[...] = idx_out.astype(I32)
    tw_ref[...] = w_out


def _post_call(x_ctx, x_lat, mod3, o_f, o_b, og, pooled_c, pooled_l, mg, gng, wpg, psc, wbg, wbp,
               wout, n2g, wr, br):
    row = lambda t: (t, 0)
    const = lambda t: (0, 0)
    stored = lambda t: (_store_tile(t), 0)
    return pl.pallas_call(
        _post_kernel,
        out_shape=[
            jax.ShapeDtypeStruct((N_TOK, D_MODEL), F32),
            jax.ShapeDtypeStruct((N_TOK, HALF_W), I32),
            jax.ShapeDtypeStruct((N_TOK, LANES), I32),
            jax.ShapeDtypeStruct((N_TOK, LANES), F32),
        ],
        grid=(N_TILES,),
        in_specs=[
            pl.BlockSpec((TOK_TILE, D_MODEL), lambda t: (_ctx_tile(t), 0)),
            pl.BlockSpec((TOK_TILE, D_MODEL), lambda t: (_lat_tile(t), 0)),
            pl.BlockSpec((1, 1, N_MOD * D_MODEL), lambda t: (_mod_row(t), 0, 0)),
            pl.BlockSpec((TOK_TILE, V_W), stored),
            pl.BlockSpec((TOK_TILE, V_W), stored),
            pl.BlockSpec((TOK_TILE, V_W), stored),
            pl.BlockSpec((TOK_TILE, POOL_W), lambda t: (_ctx_tile(t), 0)),
            pl.BlockSpec((TOK_TILE, POOL_W),
                         lambda t: (jnp.maximum(_store_tile(t) - CTX_TILES, 0), 0)),
            pl.BlockSpec((TOK_TILE, MG_W), stored),
            pl.BlockSpec((1, V_W), const),
            pl.BlockSpec((POOL_GROUPS, POOL_GROUP_DIM, POOL_GROUP_DIM), lambda t: (0, 0, 0)),
            pl.BlockSpec((1, POOL_W), const),
            pl.BlockSpec((V_W, D_MODEL), const),
            pl.BlockSpec((POOL_W, D_MODEL), const),
            pl.BlockSpec((D_MODEL, D_MODEL), const),
            pl.BlockSpec((1, D_MODEL), const),
            pl.BlockSpec((D_MODEL, LANES), const),
            pl.BlockSpec((1, LANES), const),
        ],
        out_specs=[
            pl.BlockSpec((TOK_TILE, D_MODEL), row),
            pl.BlockSpec((TOK_TILE, HALF_W), row),
            pl.BlockSpec((TOK_TILE, LANES), row),
            pl.BlockSpec((TOK_TILE, LANES), row),
        ],
        compiler_params=_params(("arbitrary",)),
        name="post",
    )(x_ctx, x_lat, mod3, o_f, o_b, og, pooled_c, pooled_l, mg, gng, wpg, psc, wbg, wbp, wout, n2g,
      wr, br)


def _route_kernel(idx_ref, rank_ref, cnt_ref, carry_ref, strict_ref):
    t = pl.program_id(0)

    @pl.when(t == 0)
    def _():
        carry_ref[...] = jnp.zeros((1, LANES), F32)
        row = lax.broadcasted_iota(I32, (ROUTE_TILE, ROUTE_TILE), 0)
        col = lax.broadcasted_iota(I32, (ROUTE_TILE, ROUTE_TILE), 1)
        strict_ref[...] = jnp.where(col < row, 1.0, 0.0).astype(BF16)

    idx = idx_ref[...]
    lane = lax.broadcasted_iota(I32, (ROUTE_TILE, LANES), 1)
    sel = [lane == idx[:, kk:kk + 1] for kk in range(TOP_K)]
    onehot = jnp.zeros((ROUTE_TILE, LANES), F32)
    for kk in range(TOP_K):
        onehot = onehot + jnp.where(sel[kk], 1.0, 0.0)
    before = _dot(strict_ref[...], onehot.astype(BF16)) + carry_ref[...]
    rank = jnp.zeros((ROUTE_TILE, LANES), F32)
    for kk in range(TOP_K):
        rk = jnp.sum(jnp.where(sel[kk], before, 0.0), axis=-1, keepdims=True)
        rank = jnp.where(lane == kk, rk, rank)
    rank_ref[...] = rank.astype(I32)
    carry_ref[...] = carry_ref[...] + jnp.sum(onehot, axis=0, keepdims=True)
    cnt_ref[...] = jnp.broadcast_to(carry_ref[...], (8, LANES))


def _route_call(idx):
    return pl.pallas_call(
        _route_kernel,
        out_shape=[
            jax.ShapeDtypeStruct((N_TOK, LANES), I32),
            jax.ShapeDtypeStruct((8, LANES), F32),
        ],
        grid=(N_TOK // ROUTE_TILE,),
        in_specs=[pl.BlockSpec((ROUTE_TILE, LANES), lambda t: (t, 0))],
        out_specs=[
            pl.BlockSpec((ROUTE_TILE, LANES), lambda t: (t, 0)),
            pl.BlockSpec((8, LANES), lambda t: (0, 0)),
        ],
        scratch_shapes=[pltpu.VMEM((1, LANES), F32), pltpu.VMEM((ROUTE_TILE, ROUTE_TILE), BF16)],
        compiler_params=_params(("arbitrary",)),
        name="route",
    )(idx)


def _moe_kernel(be_ref, nu_ref, ne_ref, par_ref, x_ref, wg_hbm, bg_ref, wu_hbm, bu_ref, wd_hbm,
                bd_ref, y_ref, wf_ref, wgu_ref, wdb_ref, sems):
    b = pl.program_id(0)
    n_used = nu_ref[0]
    e = be_ref[b]
    prev = be_ref[jnp.maximum(b - 1, 0)]
    live = b < n_used

    def weight_copies(expert, slot):
        return [pltpu.make_async_copy(w.at[expert], wf_ref.at[slot, i], sems.at[slot])
                for i, w in enumerate((wg_hbm, wu_hbm, wd_hbm))]

    @pl.when(live & ((b == 0) | (e != prev)))
    def _():
        slot = par_ref[b]

        @pl.when(b == 0)
        def _():
            for cp in weight_copies(e, slot):
                cp.start()

        for cp in weight_copies(e, slot):
            cp.wait()
        nxt = ne_ref[b]

        @pl.when(nxt >= 0)
        def _():
            for cp in weight_copies(nxt, 1 - slot):
                cp.start()

        wgu_ref[:, :D_FF] = wf_ref[slot, 0].astype(BF16)
        wgu_ref[:, D_FF:] = wf_ref[slot, 1].astype(BF16)
        wdb_ref[...] = wf_ref[slot, 2].astype(BF16)

    @pl.when(live)
    def _():
        x_lo, x_hi = _unpack_halves(x_ref[...])
        x = jnp.concatenate([x_lo.astype(BF16), x_hi.astype(BF16)], axis=-1)
        gu = _dot(x, wgu_ref[...])
        gate = jnp.minimum(gu[:, :D_FF] + bg_ref[0], SWIGLU_LIMIT)
        up = jnp.clip(gu[:, D_FF:] + bu_ref[0], -SWIGLU_LIMIT, SWIGLU_LIMIT)
        act = (up + 1.0) * (gate * _sigmoid(SWIGLU_ALPHA * gate))
        y = _dot(act.astype(BF16), wdb_ref[...]) + bd_ref[0]
        y_ref[...] = _pack_halves(y[:, :HALF_W], y[:, HALF_W:])

    @pl.when(jnp.logical_not(live))
    def _():
        y_ref[...] = jnp.zeros((MOE_BLOCK, HALF_W), I32)


def _moe_call(block_e, n_used, next_e, parity, hs, w_gate, b_gate, w_up, b_up, w_down, b_down):
    def blk(b, be, nu, ne, par):
        return jnp.minimum(b, nu[0] - 1)

    row = lambda b, be, nu, ne, par: (blk(b, be, nu, ne, par), 0)
    bsel = lambda b, be, nu, ne, par: (be[blk(b, be, nu, ne, par)], 0, 0)
    any_spec = pl.BlockSpec(memory_space=pl.ANY)
    assert D_MODEL == D_FF
    return pl.pallas_call(
        _moe_kernel,
        out_shape=jax.ShapeDtypeStruct((N_SLOTS, HALF_W), I32),
        grid_spec=pltpu.PrefetchScalarGridSpec(
            num_scalar_prefetch=4,
            grid=(N_SLOT_BLOCKS,),
            in_specs=[
                pl.BlockSpec((MOE_BLOCK, HALF_W), row),
                any_spec,
                pl.BlockSpec((1, 1, D_FF), bsel),
                any_spec,
                pl.BlockSpec((1, 1, D_FF), bsel),
                any_spec,
                pl.BlockSpec((1, 1, D_MODEL), bsel),
            ],
            out_specs=pl.BlockSpec((MOE_BLOCK, HALF_W), lambda b, be, nu, ne, par: (b, 0)),
            scratch_shapes=[
                pltpu.VMEM((2, 3, D_MODEL, D_FF), F32),
                pltpu.VMEM((D_MODEL, 2 * D_FF), BF16),
                pltpu.VMEM((D_FF, D_MODEL), BF16),
                pltpu.SemaphoreType.DMA((2,)),
            ],
        ),
        compiler_params=_params(("arbitrary",)),
        name="moe",
    )(block_e, n_used, next_e, parity, hs, w_gate, b_gate, w_up, b_up, w_down, b_down)


SC_CORES = 2
SC_SUBCORES = 16
SC_WORKERS = SC_CORES * SC_SUBCORES
SC_ROWS = 128
COMBINE_CHUNKS = 2
assert N_CTX == N_LAT


def _sc_gather_rows(table, idx):
    n_idx = idx.shape[0]
    width = table.shape[1]
    per_worker = n_idx // SC_WORKERS
    n_chunks = per_worker // SC_ROWS
    assert n_chunks * SC_ROWS * SC_WORKERS == n_idx
    mesh = plsc.VectorSubcoreMesh(core_axis_name="c", subcore_axis_name="s")

    @functools.partial(
        pl.kernel, mesh=mesh,
        out_type=jax.ShapeDtypeStruct((n_idx, width), table.dtype),
        scratch_types=[pltpu.VMEM((SC_ROWS,), I32), pltpu.VMEM((SC_ROWS, width), table.dtype),
                       pltpu.SemaphoreType.DMA],
        name="sc_gather",
    )
    def gather(table_hbm, idx_hbm, out_hbm, idx_v, rows_v, sem):
        worker = lax.axis_index("s") * SC_CORES + lax.axis_index("c")
        base = worker * per_worker

        @pl.loop(0, n_chunks)
        def _(ch):
            off = pl.multiple_of(base + ch * SC_ROWS, SC_ROWS)
            pltpu.sync_copy(idx_hbm.at[pl.ds(off, SC_ROWS)], idx_v)
            pltpu.async_copy(table_hbm.at[idx_v], rows_v, sem).wait()
            pltpu.sync_copy(rows_v, out_hbm.at[pl.ds(off, SC_ROWS)])

    return gather(table, idx)


def _sc_scatter_rows(rows, idx3, n_out):
    n_rows, width = rows.shape
    n_chunks = n_rows // SC_ROWS // SC_WORKERS
    assert n_chunks * SC_ROWS * SC_WORKERS == n_rows and idx3.shape == (n_rows // SC_ROWS, TOP_K, SC_ROWS)
    mesh = plsc.VectorSubcoreMesh(core_axis_name="c", subcore_axis_name="s")

    @functools.partial(
        pl.kernel, mesh=mesh,
        out_type=jax.ShapeDtypeStruct((n_out, width), rows.dtype),
        scratch_types=[pltpu.VMEM((TOP_K, SC_ROWS), I32), pltpu.VMEM((SC_ROWS, width), rows.dtype),
                       pltpu.SemaphoreType.DMA],
        name="sc_scatter",
    )
    def scatter(rows_hbm, idx_hbm, out_hbm, idx_v, rows_v, sem):
        worker = lax.axis_index("s") * SC_CORES + lax.axis_index("c")

        @pl.loop(0, n_chunks)
        def _(ch):
            chunk = worker * n_chunks + ch
            pltpu.sync_copy(idx_hbm.at[chunk], idx_v)
            pltpu.sync_copy(rows_hbm.at[pl.ds(pl.multiple_of(chunk * SC_ROWS, SC_ROWS), SC_ROWS)],
                            rows_v)
            for kk in range(TOP_K):
                pltpu.async_copy(rows_v, out_hbm.at[idx_v.at[kk]], sem).wait()

    return scatter(rows, idx3)


def _combine_kernel(x1_ref, mod_ref, tw_ref, fg_ref, g_ref, *rest):
    out_ref = rest[-1]
    tw = tw_ref[...]
    f_lo = jnp.zeros((TOK_TILE, HALF_W), F32)
    f_hi = jnp.zeros((TOK_TILE, HALF_W), F32)
    for kk in range(TOP_K):
        lo, hi = _unpack_halves(g_ref[kk])
        f_lo = f_lo + lo * tw[:, kk:kk + 1]
        f_hi = f_hi + hi * tw[:, kk:kk + 1]
    gate2 = mod_ref[0][:, 5 * D_MODEL:6 * D_MODEL]
    x2 = x1_ref[...] + gate2 * jnp.concatenate([f_lo, f_hi], axis=-1)
    out_ref[...] = _rms(x2) * fg_ref[...]


def _combine_call(x1, mod3, tw, final_g, gathered, partial, tile0, out_tile0, out_tiles, name):
    n_tiles = gathered.shape[1] // TOK_TILE
    in_specs = [
        pl.BlockSpec((TOK_TILE, D_MODEL), lambda t: (tile0 + t, 0)),
        pl.BlockSpec((1, 1, N_MOD * D_MODEL), lambda t: (_mod_row(tile0 + t), 0, 0)),
        pl.BlockSpec((TOK_TILE, LANES), lambda t: (tile0 + t, 0)),
        pl.BlockSpec((1, D_MODEL), lambda t: (0, 0)),
        pl.BlockSpec((TOP_K, TOK_TILE, HALF_W), lambda t: (0, t, 0)),
    ]
    args = [x1, mod3, tw, final_g, gathered]
    aliases = {}
    if partial is not None:
        in_specs.append(pl.BlockSpec(memory_space=pl.ANY))
        args.append(partial)
        aliases = {len(args) - 1: 0}
    return pl.pallas_call(
        _combine_kernel,
        out_shape=jax.ShapeDtypeStruct((out_tiles * TOK_TILE, D_MODEL), F32),
        grid=(n_tiles,),
        in_specs=in_specs,
        out_specs=pl.BlockSpec((TOK_TILE, D_MODEL), lambda t: (out_tile0 + t, 0)),
        input_output_aliases=aliases,
        compiler_params=_params(("arbitrary",)),
        name=name,
    )(*args)


def kernel(x_prompt, x_sample, state_gla_fwd, state_gla_bwd, c, c_ctx, norm1_g, w_mod, b_mod, w_in,
           w_alpha, b_alpha, gla_norm_g, w_pool_grp, pool_scale, w_branch_gla, w_branch_pool, w_out,
           norm2_g, w_router, b_router, w_gate, b_gate, w_up, b_up, w_down, b_down, final_norm_g):
    l = 0
    x_ctx = x_prompt.reshape(N_CTX, D_MODEL)
    x_lat = x_sample.reshape(N_LAT, D_MODEL)

    cvec = jnp.concatenate([c_ctx[None, :], c, jnp.zeros((8 - 1 - DEC_BATCH, D_MODEL), F32)], axis=0)
    mod = _mod_call(cvec, w_mod[l], b_mod[l][None, :])
    mod3 = mod.reshape(8, 1, N_MOD * D_MODEL)

    w_in_b = w_in[l].astype(BF16)
    w_main = w_in_b[:, :MAIN_W]
    w_alr = w_in_b[:, MAIN_W:MAIN_W + ALR_W]
    w_xp = w_in_b[:, MAIN_W + ALR_W:MAIN_W + ALR_W + POOL_W]
    w_mg = w_in_b[:, MAIN_W + ALR_W + POOL_W:]
    q, k, v, og, alr, xp, mg = _inproj_call(x_ctx, x_lat, mod3, norm1_g[l][None, :],
                                            w_main, w_alr, w_xp, w_mg)

    zpad = jnp.zeros((GLA_LOWRANK, QK_W), F32)
    wa_f = jnp.concatenate([w_alpha[l, 0], zpad], axis=0)
    wa_b = jnp.concatenate([zpad, w_alpha[l, 1]], axis=0)
    o_f, o_b, s_f, s_b = _gla_call(q, k, v, alr, wa_f, b_alpha[l, 0][None, :], wa_b,
                                   b_alpha[l, 1][None, :], state_gla_fwd[:, l], state_gla_bwd[:, l])

    pooled_c = _pool_ctx_call(xp)
    pooled_l = _pool_lat_call(xp)

    w_router_pad = jnp.pad(w_router[l], ((0, 0), (0, LANES - N_EXPERTS)))
    b_router_pad = jnp.pad(b_router[l], (0, LANES - N_EXPERTS))[None, :]
    x1, h2, top_idx, top_w = _post_call(
        x_ctx, x_lat, mod3, o_f, o_b, og, pooled_c, pooled_l, mg,
        gla_norm_g[l].reshape(1, V_W), w_pool_grp[l].astype(BF16), pool_scale[l][None, :],
        w_branch_gla[l].astype(BF16), w_branch_pool[l].astype(BF16), w_out[l].astype(BF16),
        norm2_g[l][None, :], w_router_pad, b_router_pad)

    rank, cnt = _route_call(top_idx)
    counts = cnt[0, :N_EXPERTS].astype(I32)
    padded = (counts + MOE_BLOCK - 1) // MOE_BLOCK * MOE_BLOCK
    pad_end = jnp.cumsum(padded).astype(I32)
    pad_start = pad_end - padded
    block_first = jnp.arange(N_SLOT_BLOCKS, dtype=I32) * MOE_BLOCK
    block_e = jnp.minimum(jnp.sum((pad_end[None, :] <= block_first[:, None]).astype(I32), axis=1),
                          N_EXPERTS - 1).astype(I32)
    n_used = (pad_end[-1:] // MOE_BLOCK).astype(I32)
    run_start = jnp.concatenate([jnp.ones((1,), I32), (block_e[1:] != block_e[:-1]).astype(I32)])
    parity = ((jnp.cumsum(run_start) - 1) % 2).astype(I32)
    after = pad_end[block_e] // MOE_BLOCK
    next_e = jnp.where(after < n_used[0], block_e[jnp.minimum(after, N_SLOT_BLOCKS - 1)], -1).astype(I32)
    experts = jnp.arange(N_EXPERTS, dtype=I32)
    tk = top_idx[:, :TOP_K]
    pos = jnp.sum(jnp.where(tk[:, :, None] == experts, pad_start, 0), axis=-1) + rank[:, :TOP_K]
    pos = pos.astype(I32)
    pos_by_choice = pos.T
    pos_chunks = pos_by_choice.reshape(TOP_K, N_TOK // SC_ROWS, SC_ROWS).transpose(1, 0, 2)

    hs = _sc_scatter_rows(h2, pos_chunks, N_SLOTS)
    y = _moe_call(block_e, n_used, next_e, parity, hs,
                  w_gate[l], b_gate[l][:, None, :], w_up[l], b_up[l][:, None, :],
                  w_down[l], b_down[l][:, None, :])
    outs = []
    chunk_tok = N_CTX // COMBINE_CHUNKS
    for group, tok0 in (("ctx", 0), ("lat", N_CTX)):
        out = None
        for ci in range(COMBINE_CHUNKS):
            t0 = tok0 + ci * chunk_tok
            idx = pos_by_choice[:, t0:t0 + chunk_tok].reshape(TOP_K * chunk_tok)
            gathered = _sc_gather_rows(y, idx).reshape(TOP_K, chunk_tok, HALF_W)
            out = _combine_call(x1, mod3, top_w, final_norm_g[None, :], gathered, out,
                                t0 // TOK_TILE, ci * chunk_tok // TOK_TILE, N_CTX // TOK_TILE,
                                "combine_%s%d" % (group, ci))
        outs.append(out)
    y_prompt = outs[0].reshape(BATCH, SEQ, D_MODEL)
    y_sample = outs[1].reshape(DEC_BATCH, DEC_SEQ, D_MODEL)
    return (y_prompt, y_sample, s_f[:, None], s_b[:, None])
```

```python
import functools

import jax
import jax.numpy as jnp
from jax import lax
from jax.experimental import pallas as pl
from jax.experimental.pallas import tpu as pltpu
from jax.experimental.pallas import tpu_sc as plsc

F32 = jnp.float32
BF16 = jnp.bfloat16
I32 = jnp.int32

D_MODEL = 1024
BATCH = 32
SEQ = 256
DEC_BATCH = 4
DEC_SEQ = 2048
GRID_W = 64
GLA_HEADS = 4
GLA_DK = 128
GLA_DV = 256
GLA_LOWRANK = 16
GLA_TAU = 16.0
GLA_CHUNK = 64
POOL_GROUPS = 4
POOL_GROUP_DIM = 128
POOL_WINDOWS = (2, 4, 8, 16)
N_EXPERTS = 32
TOP_K = 4
D_FF = 1024
SWIGLU_LIMIT = 7.0
SWIGLU_ALPHA = 1.702
MOE_BLOCK = 256
NORM_EPS = 1e-6
N_MOD = 6

QK_W = GLA_HEADS * GLA_DK
V_W = GLA_HEADS * GLA_DV
POOL_W = POOL_GROUPS * POOL_GROUP_DIM
MAIN_W = 2 * QK_W + 2 * V_W
ALR_W = 2 * GLA_LOWRANK
MG_W = 2 * D_MODEL

N_CTX = BATCH * SEQ
N_LAT = DEC_BATCH * DEC_SEQ
N_TOK = N_CTX + N_LAT
N_SLOT_BLOCKS = -(-(N_TOK * TOP_K + N_EXPERTS * (MOE_BLOCK - 1)) // MOE_BLOCK)
N_SLOTS = N_SLOT_BLOCKS * MOE_BLOCK

LANES = 128
TOK_TILE = 256
N_TILES = N_TOK // TOK_TILE
CTX_TILES = N_CTX // TOK_TILE
LAT_TILES_PER_SEQ = DEC_SEQ // TOK_TILE
ROUTE_TILE = 512
VMEM_LIMIT = 56 * 1024 * 1024

GLA_SEQS = 4
CTX_CHUNKS = SEQ // GLA_CHUNK
LAT_CHUNKS = DEC_SEQ // GLA_CHUNK
CHUNKS_PER_TILE = TOK_TILE // GLA_CHUNK
GLA_CTX_STEPS = (BATCH // GLA_SEQS) * CTX_CHUNKS
TILE_GRID = 8

NT_DIMS = (((1,), (1,)), ((), ()))
TN_DIMS = (((0,), (0,)), ((), ()))

assert DEC_BATCH == GLA_SEQS and SEQ == TOK_TILE and N_TILES == TILE_GRID * TILE_GRID


def _params(semantics, vmem=VMEM_LIMIT):
    return pltpu.CompilerParams(dimension_semantics=semantics, vmem_limit_bytes=vmem)


def _split_bf16(a):
    hi = a.astype(BF16)
    lo = (a - hi.astype(F32)).astype(BF16)
    return hi, lo


def _dot(a, b):
    return jnp.dot(a, b, preferred_element_type=F32)


def _dot3(a, b):
    a_hi, a_lo = _split_bf16(a)
    b_hi, b_lo = _split_bf16(b)
    return _dot(a_hi, b_hi) + _dot(a_lo, b_hi) + _dot(a_hi, b_lo)


def _sigmoid(x):
    return 1.0 / (1.0 + jnp.exp(-x))


HALF_W = D_MODEL // 2
HIGH_HALF_MASK = -65536


def _pack_halves(lo, hi):
    lo_bits = pltpu.bitcast(lo.astype(BF16).astype(F32), I32)
    hi_bits = pltpu.bitcast(hi.astype(BF16).astype(F32), I32)
    return lax.shift_right_logical(lo_bits, 16) | (hi_bits & HIGH_HALF_MASK)


def _unpack_halves(words):
    lo = pltpu.bitcast(lax.shift_left(words, 16), F32)
    hi = pltpu.bitcast(words & HIGH_HALF_MASK, F32)
    return lo, hi


def _rms(x):
    return x * lax.rsqrt(jnp.mean(x * x, axis=-1, keepdims=True) + NORM_EPS)


def _mod_row(t):
    return jnp.where(t < CTX_TILES, 0, 1 + (t - CTX_TILES) // LAT_TILES_PER_SEQ)


def _store_tile(t):
    u = t - CTX_TILES
    return jnp.where(t < CTX_TILES, t,
                     CTX_TILES + DEC_BATCH * (u % LAT_TILES_PER_SEQ) + u // LAT_TILES_PER_SEQ)


def _ctx_tile(t):
    return jnp.minimum(t, CTX_TILES - 1)


def _lat_tile(t):
    return jnp.maximum(t - CTX_TILES, 0)


def _mod_kernel(c_ref, w_ref, b_ref, o_ref):
    c = c_ref[...]
    o_ref[...] = _dot3(c * _sigmoid(c), w_ref[...]) + b_ref[...]


def _mod_call(cvec, w_mod, b_mod):
    rows = cvec.shape[0]
    return pl.pallas_call(
        _mod_kernel,
        out_shape=jax.ShapeDtypeStruct((rows, N_MOD * D_MODEL), F32),
        grid=(N_MOD,),
        in_specs=[
            pl.BlockSpec((rows, D_MODEL), lambda j: (0, 0)),
            pl.BlockSpec((D_MODEL, D_MODEL), lambda j: (0, j)),
            pl.BlockSpec((1, D_MODEL), lambda j: (0, j)),
        ],
        out_specs=pl.BlockSpec((rows, D_MODEL), lambda j: (0, j)),
        compiler_params=_params(("arbitrary",)),
        name="mod",
    )(cvec, w_mod, b_mod)


def _inproj_kernel(xc_ref, xl_ref, mod_ref, g_ref, wmain_ref, walr_ref, wxp_ref, wmg_ref,
                   q_ref, k_ref, v_ref, og_ref, alr_ref, xp_ref, mg_ref):
    t = pl.program_id(0)
    x = jnp.where(t < CTX_TILES, xc_ref[...], xl_ref[...])
    mod = mod_ref[0]
    shift1 = mod[:, 0:D_MODEL]
    scale1 = mod[:, D_MODEL:2 * D_MODEL]
    h = _rms(x) * g_ref[...]
    h = (h * (1.0 + scale1) + shift1).astype(BF16)
    z = _dot(h, wmain_ref[...])
    q_ref[...] = (z[:, 0:QK_W] * (GLA_DK ** -0.5)).astype(BF16)
    k_ref[...] = z[:, QK_W:2 * QK_W].astype(BF16)
    v_ref[...] = z[:, 2 * QK_W:2 * QK_W + V_W].astype(BF16)
    og = z[:, 2 * QK_W + V_W:MAIN_W]
    og_ref[...] = (og * _sigmoid(og)).astype(BF16)
    alr_ref[...] = _dot(h, walr_ref[...])
    xp_ref[...] = _dot(h, wxp_ref[...])
    mg_ref[...] = _sigmoid(_dot(h, wmg_ref[...])).astype(BF16)


def _inproj_call(x_ctx, x_lat, mod3, norm1_g, w_main, w_alr, w_xp, w_mg):
    const = lambda t: (0, 0)
    stored = lambda t: (_store_tile(t), 0)
    widths = (QK_W, QK_W, V_W, V_W, ALR_W, POOL_W, MG_W)
    dtypes = (BF16, BF16, BF16, BF16, F32, F32, BF16)
    return pl.pallas_call(
        _inproj_kernel,
        out_shape=[jax.ShapeDtypeStruct((N_TOK, w), dt) for w, dt in zip(widths, dtypes)],
        grid=(N_TILES,),
        in_specs=[
            pl.BlockSpec((TOK_TILE, D_MODEL), lambda t: (_ctx_tile(t), 0)),
            pl.BlockSpec((TOK_TILE, D_MODEL), lambda t: (_lat_tile(t), 0)),
            pl.BlockSpec((1, 1, N_MOD * D_MODEL), lambda t: (_mod_row(t), 0, 0)),
            pl.BlockSpec((1, D_MODEL), const),
            pl.BlockSpec((D_MODEL, MAIN_W), const),
            pl.BlockSpec((D_MODEL, ALR_W), const),
            pl.BlockSpec((D_MODEL, POOL_W), const),
            pl.BlockSpec((D_MODEL, MG_W), const),
        ],
        out_specs=[pl.BlockSpec((TOK_TILE, w), stored) for w in widths],
        compiler_params=_params(("arbitrary",)),
        name="inproj",
    )(x_ctx, x_lat, mod3, norm1_g, w_main, w_alr, w_xp, w_mg)


def _gla_direction(q_ref, k_ref, v_ref, alr_ref, wa_ref, ba_ref, o_ref, st_ref, d, rev):
    rows = GLA_SEQS * GLA_CHUNK
    stack = lambda ref, cols: jnp.concatenate([ref[0, s, :, cols] for s in range(GLA_SEQS)], axis=0)
    alr = stack(alr_ref, slice(None))
    a = _dot3(alr, wa_ref[...]) + ba_ref[...]
    g = (jnp.minimum(a, 0.0) - jnp.log(1.0 + jnp.exp(-jnp.abs(a)))) * (1.0 / GLA_TAU)

    row = lax.broadcasted_iota(I32, (rows, rows), 0)
    col = lax.broadcasted_iota(I32, (rows, rows), 1)
    same = (row // GLA_CHUNK) == (col // GLA_CHUNK)
    tri = same & ((col >= row) if rev else (col <= row))
    tri_b = jnp.where(tri, 1.0, 0.0).astype(BF16)
    g_hi, g_lo = _split_bf16(g)
    bcum = _dot(tri_b, g_hi) + _dot(tri_b, g_lo)

    def per_seq_row(r):
        return jnp.concatenate(
            [jnp.broadcast_to(bcum[s * GLA_CHUNK + r:s * GLA_CHUNK + r + 1], (GLA_CHUNK, QK_W))
             for s in range(GLA_SEQS)], axis=0)

    r_last = 0 if rev else GLA_CHUNK - 1
    blast = per_seq_row(r_last)
    bmid = per_seq_row(GLA_CHUNK // 2)
    e_q = jnp.exp(bcum - bmid)
    e_k = jnp.exp(bmid - bcum)
    e_in = jnp.exp(bcum)
    e_out = jnp.exp(blast - bcum)
    q = stack(q_ref, slice(None)).astype(F32)
    k = stack(k_ref, slice(None)).astype(F32)

    wide = (rows, GLA_SEQS * GLA_DK)
    own = (lax.broadcasted_iota(I32, wide, 0) // GLA_CHUNK) == (lax.broadcasted_iota(I32, wide, 1) // GLA_DK)

    def block_diag(x):
        return jnp.where(own, jnp.concatenate([x] * GLA_SEQS, axis=1), 0.0).astype(BF16)

    for h in range(GLA_HEADS):
        ks = slice(h * GLA_DK, (h + 1) * GLA_DK)
        vs = slice(h * GLA_DV, (h + 1) * GLA_DV)
        qh = q[:, ks]
        kh = k[:, ks]
        vh = stack(v_ref, vs)
        att = lax.dot_general((qh * e_q[:, ks]).astype(BF16), (kh * e_k[:, ks]).astype(BF16),
                              NT_DIMS, preferred_element_type=F32)
        att = jnp.where(tri, att, 0.0).astype(BF16)
        st = st_ref[d, h]
        o_inter = lax.dot_general(block_diag(qh * e_in[:, ks]), st.astype(BF16), NT_DIMS,
                                  preferred_element_type=F32)
        o_h = o_inter + _dot(att, vh)
        for s in range(GLA_SEQS):
            o_ref[0, s, :, vs] = o_h[s * GLA_CHUNK:(s + 1) * GLA_CHUNK]
        upd = lax.dot_general(vh, block_diag(kh * e_out[:, ks]), TN_DIMS,
                              preferred_element_type=F32)
        e_last = jnp.concatenate(
            [jnp.exp(bcum[s * GLA_CHUNK + r_last:s * GLA_CHUNK + r_last + 1, ks])
             for s in range(GLA_SEQS)], axis=1)
        st_ref[d, h] = st * e_last + upd


def _gla_kernel(qf_ref, kf_ref, vf_ref, af_ref, qb_ref, kb_ref, vb_ref, ab_ref,
                waf_ref, baf_ref, wab_ref, bab_ref, s0f_ref, s0b_ref,
                of_ref, ob_ref, sf_ref, sb_ref, st_ref):
    i = pl.program_id(0)
    is_ctx = i < GLA_CTX_STEPS
    chunk = jnp.where(is_ctx, i % CTX_CHUNKS, i - GLA_CTX_STEPS)

    @pl.when(is_ctx & (chunk == 0))
    def _():
        st_ref[...] = jnp.zeros(st_ref.shape, F32)

    @pl.when(i == GLA_CTX_STEPS)
    def _():
        for s in range(GLA_SEQS):
            ls = slice(s * GLA_DK, (s + 1) * GLA_DK)
            for h in range(GLA_HEADS):
                st_ref[0, h, :, ls] = s0f_ref[s, h].T
                st_ref[1, h, :, ls] = s0b_ref[s, h].T

    _gla_direction(qf_ref, kf_ref, vf_ref, af_ref, waf_ref, baf_ref, of_ref, st_ref, 0, False)
    _gla_direction(qb_ref, kb_ref, vb_ref, ab_ref, wab_ref, bab_ref, ob_ref, st_ref, 1, True)

    @pl.when(is_ctx & (chunk == CTX_CHUNKS - 1))
    def _():
        for s in range(GLA_SEQS):
            ls = slice(s * GLA_DK, (s + 1) * GLA_DK)
            for h in range(GLA_HEADS):
                sf_ref[s, h] = st_ref[0, h, :, ls].T
                sb_ref[s, h] = st_ref[1, h, :, ls].T


def _gla_block(i, rev):
    is_ctx = i < GLA_CTX_STEPS
    group = i // CTX_CHUNKS
    c_ctx = i % CTX_CHUNKS
    c_lat = i - GLA_CTX_STEPS
    if rev:
        c_ctx = CTX_CHUNKS - 1 - c_ctx
        c_lat = LAT_CHUNKS - 1 - c_lat
    j = c_lat // CHUNKS_PER_TILE
    per_row = TILE_GRID // GLA_SEQS
    a = jnp.where(is_ctx, group // per_row, CTX_TILES // TILE_GRID + j // per_row)
    b = jnp.where(is_ctx, group % per_row, j % per_row)
    c = jnp.where(is_ctx, c_ctx, c_lat % CHUNKS_PER_TILE)
    return (a, b, c, 0)


def _gla_call(q, k, v, alr, wa_f, ba_f, wa_b, ba_b, s0_f, s0_b):
    def view(arr):
        return arr.reshape(TILE_GRID, TILE_GRID, TOK_TILE, arr.shape[-1])

    def spec(width, rev):
        return pl.BlockSpec((1, GLA_SEQS, GLA_CHUNK, width), lambda i: _gla_block(i, rev))

    const = lambda i: (0, 0)
    st_block = (GLA_SEQS, GLA_HEADS, GLA_DK, GLA_DV)
    whole_state = pl.BlockSpec(st_block, lambda i: (0, 0, 0, 0))
    ctx_state = pl.BlockSpec(
        st_block, lambda i: (jnp.minimum(i // CTX_CHUNKS, BATCH // GLA_SEQS - 1), 0, 0, 0))
    in_specs = []
    for rev in (False, True):
        in_specs += [spec(QK_W, rev), spec(QK_W, rev), spec(V_W, rev), spec(ALR_W, rev)]
    in_specs += [pl.BlockSpec((ALR_W, QK_W), const), pl.BlockSpec((1, QK_W), const)] * 2
    in_specs += [whole_state, whole_state]
    o_shape = jax.ShapeDtypeStruct((TILE_GRID, TILE_GRID, TOK_TILE, V_W), F32)
    s_shape = jax.ShapeDtypeStruct((BATCH, GLA_HEADS, GLA_DK, GLA_DV), F32)
    qv, kv, vv, av = view(q), view(k), view(v), view(alr)
    o_f, o_b, s_f, s_b = pl.pallas_call(
        _gla_kernel,
        out_shape=[o_shape, o_shape, s_shape, s_shape],
        grid=(GLA_CTX_STEPS + LAT_CHUNKS,),
        in_specs=in_specs,
        out_specs=[spec(V_W, False), spec(V_W, True), ctx_state, ctx_state],
        scratch_shapes=[pltpu.VMEM((2, GLA_HEADS, GLA_DV, GLA_SEQS * GLA_DK), F32)],
        compiler_params=_params(("arbitrary",)),
        name="gla",
    )(qv, kv, vv, av, qv, kv, vv, av, wa_f, ba_f, wa_b, ba_b, s0_f, s0_b)
    return o_f.reshape(N_TOK, V_W), o_b.reshape(N_TOK, V_W), s_f, s_b


def _band(n, w, block):
    row = lax.broadcasted_iota(I32, (n, n), 0)
    col = lax.broadcasted_iota(I32, (n, n), 1)
    inside = (col >= row - w // 2) & (col <= row + w // 2 - 1)
    if block < n:
        inside = inside & ((row // block) == (col // block))
    return jnp.where(inside, 1.0, 0.0).astype(BF16)


def _win_count(p, n, w):
    return jnp.minimum(p + w // 2 - 1, n - 1) - jnp.maximum(p - w // 2, 0) + 1


def _pool_ctx_kernel(x_ref, o_ref, band_ref):
    @pl.when(pl.program_id(0) == 0)
    def _():
        for gi, w in enumerate(POOL_WINDOWS):
            band_ref[gi] = _band(SEQ, w, SEQ)

    p = lax.broadcasted_iota(I32, (SEQ, POOL_GROUP_DIM), 0)
    for gi, w in enumerate(POOL_WINDOWS):
        cs = slice(gi * POOL_GROUP_DIM, (gi + 1) * POOL_GROUP_DIM)
        x = x_ref[:, cs]
        hi, lo = _split_bf16(x)
        band = band_ref[gi]
        s = _dot(band, hi) + _dot(band, lo)
        cnt = _win_count(p, SEQ, w).astype(F32)
        o_ref[:, cs] = s / cnt - x


def _pool_ctx_call(xp):
    spec = pl.BlockSpec((SEQ, POOL_W), lambda b: (b, 0))
    return pl.pallas_call(
        _pool_ctx_kernel,
        out_shape=jax.ShapeDtypeStruct((N_CTX, POOL_W), F32),
        grid=(BATCH,),
        in_specs=[spec],
        out_specs=spec,
        scratch_shapes=[pltpu.VMEM((POOL_GROUPS, SEQ, SEQ), BF16)],
        compiler_params=_params(("arbitrary",)),
        name="pool_ctx",
    )(xp)


POOL_HALO = (max(POOL_WINDOWS) // 2) * GRID_W


def _pool_lat_kernel(x_ref, o_ref, pad_ref):
    rows = DEC_SEQ // GRID_W
    p = lax.broadcasted_iota(I32, (DEC_SEQ, POOL_GROUP_DIM), 0)
    r = p // GRID_W
    cidx = p % GRID_W
    zeros = jnp.zeros((POOL_HALO, POOL_GROUP_DIM), F32)
    pad_ref[0:POOL_HALO, :] = zeros
    pad_ref[POOL_HALO + DEC_SEQ:2 * POOL_HALO + DEC_SEQ, :] = zeros
    for gi, w in enumerate(POOL_WINDOWS):
        cs = slice(gi * POOL_GROUP_DIM, (gi + 1) * POOL_GROUP_DIM)
        band = _band(TOK_TILE, w, GRID_W)
        for t in range(LAT_TILES_PER_SEQ):
            hi, lo = _split_bf16(x_ref[t, 0, :, cs])
            pad_ref[POOL_HALO + t * TOK_TILE:POOL_HALO + (t + 1) * TOK_TILE, :] = (
                _dot(band, hi) + _dot(band, lo))
        acc = jnp.zeros((DEC_SEQ, POOL_GROUP_DIM), F32)
        for dr in range(-(w // 2), w // 2):
            start = POOL_HALO + dr * GRID_W
            acc = acc + pad_ref[start:start + DEC_SEQ, :]
        cnt = (_win_count(r, rows, w) * _win_count(cidx, GRID_W, w)).astype(F32)
        pooled = acc / cnt
        for t in range(LAT_TILES_PER_SEQ):
            rs = slice(t * TOK_TILE, (t + 1) * TOK_TILE)
            o_ref[t, 0, :, cs] = pooled[rs] - x_ref[t, 0, :, cs]


def _pool_lat_call(xp):
    view = xp.reshape(N_TILES // DEC_BATCH, DEC_BATCH, TOK_TILE, POOL_W)
    blk = (LAT_TILES_PER_SEQ, 1, TOK_TILE, POOL_W)
    out = pl.pallas_call(
        _pool_lat_kernel,
        out_shape=jax.ShapeDtypeStruct((LAT_TILES_PER_SEQ, DEC_BATCH, TOK_TILE, POOL_W), F32),
        grid=(DEC_BATCH,),
        in_specs=[pl.BlockSpec(blk, lambda s: (CTX_TILES // DEC_BATCH // LAT_TILES_PER_SEQ, s, 0, 0))],
        out_specs=pl.BlockSpec(blk, lambda s: (0, s, 0, 0)),
        scratch_shapes=[pltpu.VMEM((DEC_SEQ + 2 * POOL_HALO, POOL_GROUP_DIM), F32)],
        compiler_params=_params(("arbitrary",)),
        name="pool_lat",
    )(view)
    return out.reshape(N_LAT, POOL_W)


def _post_kernel(xc_ref, xl_ref, mod_ref, of_ref, ob_ref, og_ref, pc_ref, pl_ref, mg_ref, gng_ref,
                 wpg_ref, psc_ref, wbg_ref, wbp_ref, wout_ref, n2g_ref, wr_ref, br_ref,
                 x1_ref, h2_ref, idx_ref, tw_ref):
    t = pl.program_id(0)
    is_ctx = t < CTX_TILES
    x = jnp.where(is_ctx, xc_ref[...], xl_ref[...])
    pooled = jnp.where(is_ctx, pc_ref[...], pl_ref[...])
    mod = mod_ref[0]
    gate1 = mod[:, 2 * D_MODEL:3 * D_MODEL]
    shift2 = mod[:, 3 * D_MODEL:4 * D_MODEL]
    scale2 = mod[:, 4 * D_MODEL:5 * D_MODEL]

    o = of_ref[...] + ob_ref[...]
    og = og_ref[...].astype(F32)
    gated = []
    for h in range(GLA_HEADS):
        vs = slice(h * GLA_DV, (h + 1) * GLA_DV)
        oh = _rms(o[:, vs]) * gng_ref[:, vs]
        gated.append((oh * og[:, vs]).astype(BF16))
    br_gla = _dot(jnp.concatenate(gated, axis=-1), wbg_ref[...])

    pm = []
    for gi in range(POOL_GROUPS):
        cs = slice(gi * POOL_GROUP_DIM, (gi + 1) * POOL_GROUP_DIM)
        pmg = _dot(pooled[:, cs].astype(BF16), wpg_ref[gi]) * psc_ref[:, cs]
        pm.append(pmg.astype(BF16))
    br_pool = _dot(jnp.concatenate(pm, axis=-1), wbp_ref[...])

    mg = mg_ref[...].astype(F32)
    merged = mg[:, 0:D_MODEL] * br_gla + mg[:, D_MODEL:MG_W] * br_pool
    m = _dot(merged.astype(BF16), wout_ref[...])
    x1 = x + gate1 * m
    x1_ref[...] = x1
    h2 = _rms(x1) * n2g_ref[...]
    h2 = h2 * (1.0 + scale2) + shift2
    h2_ref[...] = _pack_halves(h2[:, :HALF_W], h2[:, HALF_W:])

    logits = _dot3(h2, wr_ref[...]) + br_ref[...]
    lane = lax.broadcasted_iota(I32, (TOK_TILE, LANES), 1)
    lane_f = lane.astype(F32)
    neg = jnp.float32(-jnp.inf)
    cur = jnp.where(lane < N_EXPERTS, logits, neg)
    vals, idxs = [], []
    for _ in range(TOP_K):
        mx = jnp.max(cur, axis=-1, keepdims=True)
        ix = jnp.min(jnp.where(cur == mx, lane_f, float(LANES)), axis=-1, keepdims=True)
        vals.append(mx)
        idxs.append(ix)
        cur = jnp.where(lane_f == ix, neg, cur)
    ex = [jnp.exp(vv - vals[0]) for vv in vals]
    tot = ex[0] + ex[1] + ex[2] + ex[3]
    idx_out = jnp.zeros((TOK_TILE, LANES), F32)
    w_out = jnp.zeros((TOK_TILE, LANES), F32)
    for kk in range(TOP_K):
        idx_out = jnp.where(lane == kk, idxs[kk], idx_out)
        w_out = jnp.where(lane == kk, ex[kk] / tot, w_out)
    idx_ref[...] = idx_out.astype(I32)
    tw_ref[...] = w_out


def _post_call(x_ctx, x_lat, mod3, o_f, o_b, og, pooled_c, pooled_l, mg, gng, wpg, psc, wbg, wbp,
               wout, n2g, wr, br):
    row = lambda t: (t, 0)
    const = lambda t: (0, 0)
    stored = lambda t: (_store_tile(t), 0)
    return pl.pallas_call(
        _post_kernel,
        out_shape=[
            jax.ShapeDtypeStruct((N_TOK, D_MODEL), F32),
            jax.ShapeDtypeStruct((N_TOK, HALF_W), I32),
            jax.ShapeDtypeStruct((N_TOK, LANES), I32),
            jax.ShapeDtypeStruct((N_TOK, LANES), F32),
        ],
        grid=(N_TILES,),
        in_specs=[
            pl.BlockSpec((TOK_TILE, D_MODEL), lambda t: (_ctx_tile(t), 0)),
            pl.BlockSpec((TOK_TILE, D_MODEL), lambda t: (_lat_tile(t), 0)),
            pl.BlockSpec((1, 1, N_MOD * D_MODEL), lambda t: (_mod_row(t), 0, 0)),
            pl.BlockSpec((TOK_TILE, V_W), stored),
            pl.BlockSpec((TOK_TILE, V_W), stored),
            pl.BlockSpec((TOK_TILE, V_W), stored),
            pl.BlockSpec((TOK_TILE, POOL_W), lambda t: (_ctx_tile(t), 0)),
            pl.BlockSpec((TOK_TILE, POOL_W),
                         lambda t: (jnp.maximum(_store_tile(t) - CTX_TILES, 0), 0)),
            pl.BlockSpec((TOK_TILE, MG_W), stored),
            pl.BlockSpec((1, V_W), const),
            pl.BlockSpec((POOL_GROUPS, POOL_GROUP_DIM, POOL_GROUP_DIM), lambda t: (0, 0, 0)),
            pl.BlockSpec((1, POOL_W), const),
            pl.BlockSpec((V_W, D_MODEL), const),
            pl.BlockSpec((POOL_W, D_MODEL), const),
            pl.BlockSpec((D_MODEL, D_MODEL), const),
            pl.BlockSpec((1, D_MODEL), const),
            pl.BlockSpec((D_MODEL, LANES), const),
            pl.BlockSpec((1, LANES), const),
        ],
        out_specs=[
            pl.BlockSpec((TOK_TILE, D_MODEL), row),
            pl.BlockSpec((TOK_TILE, HALF_W), row),
            pl.BlockSpec((TOK_TILE, LANES), row),
            pl.BlockSpec((TOK_TILE, LANES), row),
        ],
        compiler_params=_params(("arbitrary",)),
        name="post",
    )(x_ctx, x_lat, mod3, o_f, o_b, og, pooled_c, pooled_l, mg, gng, wpg, psc, wbg, wbp, wout, n2g,
      wr, br)


def _route_kernel(idx_ref, rank_ref, cnt_ref, carry_ref, strict_ref):
    t = pl.program_id(0)

    @pl.when(t == 0)
    def _():
        carry_ref[...] = jnp.zeros((1, LANES), F32)
        row = lax.broadcasted_iota(I32, (ROUTE_TILE, ROUTE_TILE), 0)
        col = lax.broadcasted_iota(I32, (ROUTE_TILE, ROUTE_TILE), 1)
        strict_ref[...] = jnp.where(col < row, 1.0, 0.0).astype(BF16)

    idx = idx_ref[...]
    lane = lax.broadcasted_iota(I32, (ROUTE_TILE, LANES), 1)
    sel = [lane == idx[:, kk:kk + 1] for kk in range(TOP_K)]
    onehot = jnp.zeros((ROUTE_TILE, LANES), F32)
    for kk in range(TOP_K):
        onehot = onehot + jnp.where(sel[kk], 1.0, 0.0)
    before = _dot(strict_ref[...], onehot.astype(BF16)) + carry_ref[...]
    rank = jnp.zeros((ROUTE_TILE, LANES), F32)
    for kk in range(TOP_K):
        rk = jnp.sum(jnp.where(sel[kk], before, 0.0), axis=-1, keepdims=True)
        rank = jnp.where(lane == kk, rk, rank)
    rank_ref[...] = rank.astype(I32)
    carry_ref[...] = carry_ref[...] + jnp.sum(onehot, axis=0, keepdims=True)
    cnt_ref[...] = jnp.broadcast_to(carry_ref[...], (8, LANES))


def _route_call(idx):
    return pl.pallas_call(
        _route_kernel,
        out_shape=[
            jax.ShapeDtypeStruct((N_TOK, LANES), I32),
            jax.ShapeDtypeStruct((8, LANES), F32),
        ],
        grid=(N_TOK // ROUTE_TILE,),
        in_specs=[pl.BlockSpec((ROUTE_TILE, LANES), lambda t: (t, 0))],
        out_specs=[
            pl.BlockSpec((ROUTE_TILE, LANES), lambda t: (t, 0)),
            pl.BlockSpec((8, LANES), lambda t: (0, 0)),
        ],
        scratch_shapes=[pltpu.VMEM((1, LANES), F32), pltpu.VMEM((ROUTE_TILE, ROUTE_TILE), BF16)],
        compiler_params=_params(("arbitrary",)),
        name="route",
    )(idx)


def _moe_kernel(be_ref, nu_ref, ne_ref, par_ref, x_ref, wg_hbm, bg_ref, wu_hbm, bu_ref, wd_hbm,
                bd_ref, y_ref, wf_ref, wgu_ref, wdb_ref, sems):
    b = pl.program_id(0)
    n_used = nu_ref[0]
    e = be_ref[b]
    prev = be_ref[jnp.maximum(b - 1, 0)]
    live = b < n_used

    def weight_copies(expert, slot):
        return [pltpu.make_async_copy(w.at[expert], wf_ref.at[slot, i], sems.at[slot])
                for i, w in enumerate((wg_hbm, wu_hbm, wd_hbm))]

    @pl.when(live & ((b == 0) | (e != prev)))
    def _():
        slot = par_ref[b]

        @pl.when(b == 0)
        def _():
            for cp in weight_copies(e, slot):
                cp.start()

        for cp in weight_copies(e, slot):
            cp.wait()
        nxt = ne_ref[b]

        @pl.when(nxt >= 0)
        def _():
            for cp in weight_copies(nxt, 1 - slot):
                cp.start()

        wgu_ref[:, :D_FF] = wf_ref[slot, 0].astype(BF16)
        wgu_ref[:, D_FF:] = wf_ref[slot, 1].astype(BF16)
        wdb_ref[...] = wf_ref[slot, 2].astype(BF16)

    @pl.when(live)
    def _():
        x_lo, x_hi = _unpack_halves(x_ref[...])
        x = jnp.concatenate([x_lo.astype(BF16), x_hi.astype(BF16)], axis=-1)
        gu = _dot(x, wgu_ref[...])
        gate = jnp.minimum(gu[:, :D_FF] + bg_ref[0], SWIGLU_LIMIT)
        up = jnp.clip(gu[:, D_FF:] + bu_ref[0], -SWIGLU_LIMIT, SWIGLU_LIMIT)
        act = (up + 1.0) * (gate * _sigmoid(SWIGLU_ALPHA * gate))
        y = _dot(act.astype(BF16), wdb_ref[...]) + bd_ref[0]
        y_ref[...] = _pack_halves(y[:, :HALF_W], y[:, HALF_W:])

    @pl.when(jnp.logical_not(live))
    def _():
        y_ref[...] = jnp.zeros((MOE_BLOCK, HALF_W), I32)


def _moe_call(block_e, n_used, next_e, parity, hs, w_gate, b_gate, w_up, b_up, w_down, b_down):
    def blk(b, be, nu, ne, par):
        return jnp.minimum(b, nu[0] - 1)

    row = lambda b, be, nu, ne, par: (blk(b, be, nu, ne, par), 0)
    bsel = lambda b, be, nu, ne, par: (be[blk(b, be, nu, ne, par)], 0, 0)
    any_spec = pl.BlockSpec(memory_space=pl.ANY)
    assert D_MODEL == D_FF
    return pl.pallas_call(
        _moe_kernel,
        out_shape=jax.ShapeDtypeStruct((N_SLOTS, HALF_W), I32),
        grid_spec=pltpu.PrefetchScalarGridSpec(
            num_scalar_prefetch=4,
            grid=(N_SLOT_BLOCKS,),
            in_specs=[
                pl.BlockSpec((MOE_BLOCK, HALF_W), row),
                any_spec,
                pl.BlockSpec((1, 1, D_FF), bsel),
                any_spec,
                pl.BlockSpec((1, 1, D_FF), bsel),
                any_spec,
                pl.BlockSpec((1, 1, D_MODEL), bsel),
            ],
            out_specs=pl.BlockSpec((MOE_BLOCK, HALF_W), lambda b, be, nu, ne, par: (b, 0)),
            scratch_shapes=[
                pltpu.VMEM((2, 3, D_MODEL, D_FF), F32),
                pltpu.VMEM((D_MODEL, 2 * D_FF), BF16),
                pltpu.VMEM((D_FF, D_MODEL), BF16),
                pltpu.SemaphoreType.DMA((2,)),
            ],
        ),
        compiler_params=_params(("arbitrary",)),
        name="moe",
    )(block_e, n_used, next_e, parity, hs, w_gate, b_gate, w_up, b_up, w_down, b_down)


SC_CORES = 2
SC_SUBCORES = 16
SC_WORKERS = SC_CORES * SC_SUBCORES
SC_ROWS = 128
COMBINE_CHUNKS = 2
assert N_CTX == N_LAT


def _sc_gather_rows(table, idx):
    n_idx = idx.shape[0]
    width = table.shape[1]
    per_worker = n_idx // SC_WORKERS
    n_chunks = per_worker // SC_ROWS
    assert n_chunks * SC_ROWS * SC_WORKERS == n_idx
    mesh = plsc.VectorSubcoreMesh(core_axis_name="c", subcore_axis_name="s")

    @functools.partial(
        pl.kernel, mesh=mesh,
        out_type=jax.ShapeDtypeStruct((n_idx, width), table.dtype),
        scratch_types=[pltpu.VMEM((SC_ROWS,), I32), pltpu.VMEM((SC_ROWS, width), table.dtype),
                       pltpu.SemaphoreType.DMA],
        name="sc_gather",
    )
    def gather(table_hbm, idx_hbm, out_hbm, idx_v, rows_v, sem):
        worker = lax.axis_index("s") * SC_CORES + lax.axis_index("c")
        base = worker * per_worker

        @pl.loop(0, n_chunks)
        def _(ch):
            off = pl.multiple_of(base + ch * SC_ROWS, SC_ROWS)
            pltpu.sync_copy(idx_hbm.at[pl.ds(off, SC_ROWS)], idx_v)
            pltpu.async_copy(table_hbm.at[idx_v], rows_v, sem).wait()
            pltpu.sync_copy(rows_v, out_hbm.at[pl.ds(off, SC_ROWS)])

    return gather(table, idx)


def _sc_scatter_rows(rows, idx3, n_out):
    n_rows, width = rows.shape
    n_chunks = n_rows // SC_ROWS // SC_WORKERS
    assert n_chunks * SC_ROWS * SC_WORKERS == n_rows and idx3.shape == (n_rows // SC_ROWS, TOP_K, SC_ROWS)
    mesh = plsc.VectorSubcoreMesh(core_axis_name="c", subcore_axis_name="s")

    @functools.partial(
        pl.kernel, mesh=mesh,
        out_type=jax.ShapeDtypeStruct((n_out, width), rows.dtype),
        scratch_types=[pltpu.VMEM((TOP_K, SC_ROWS), I32), pltpu.VMEM((SC_ROWS, width), rows.dtype),
                       pltpu.SemaphoreType.DMA],
        name="sc_scatter",
    )
    def scatter(rows_hbm, idx_hbm, out_hbm, idx_v, rows_v, sem):
        worker = lax.axis_index("s") * SC_CORES + lax.axis_index("c")

        @pl.loop(0, n_chunks)
        def _(ch):
            chunk = worker * n_chunks + ch
            pltpu.sync_copy(idx_hbm.at[chunk], idx_v)
            pltpu.sync_copy(rows_hbm.at[pl.ds(pl.multiple_of(chunk * SC_ROWS, SC_ROWS), SC_ROWS)],
                            rows_v)
            for kk in range(TOP_K):
                pltpu.async_copy(rows_v, out_hbm.at[idx_v.at[kk]], sem).wait()

    return scatter(rows, idx3)


def _combine_kernel(x1_ref, mod_ref, tw_ref, fg_ref, g_ref, *rest):
    out_ref = rest[-1]
    tw = tw_ref[...]
    f_lo = jnp.zeros((TOK_TILE, HALF_W), F32)
    f_hi = jnp.zeros((TOK_TILE, HALF_W), F32)
    for kk in range(TOP_K):
        lo, hi = _unpack_halves(g_ref[kk])
        f_lo = f_lo + lo * tw[:, kk:kk + 1]
        f_hi = f_hi + hi * tw[:, kk:kk + 1]
    gate2 = mod_ref[0][:, 5 * D_MODEL:6 * D_MODEL]
    x2 = x1_ref[...] + gate2 * jnp.concatenate([f_lo, f_hi], axis=-1)
    out_ref[...] = _rms(x2) * fg_ref[...]


def _combine_call(x1, mod3, tw, final_g, gathered, partial, tile0, out_tile0, out_tiles, name):
    n_tiles = gathered.shape[1] // TOK_TILE
    in_specs = [
        pl.BlockSpec((TOK_TILE, D_MODEL), lambda t: (tile0 + t, 0)),
        pl.BlockSpec((1, 1, N_MOD * D_MODEL), lambda t: (_mod_row(tile0 + t), 0, 0)),
        pl.BlockSpec((TOK_TILE, LANES), lambda t: (tile0 + t, 0)),
        pl.BlockSpec((1, D_MODEL), lambda t: (0, 0)),
        pl.BlockSpec((TOP_K, TOK_TILE, HALF_W), lambda t: (0, t, 0)),
    ]
    args = [x1, mod3, tw, final_g, gathered]
    aliases = {}
    if partial is not None:
        in_specs.append(pl.BlockSpec(memory_space=pl.ANY))
        args.append(partial)
        aliases = {len(args) - 1: 0}
    return pl.pallas_call(
        _combine_kernel,
        out_shape=jax.ShapeDtypeStruct((out_tiles * TOK_TILE, D_MODEL), F32),
        grid=(n_tiles,),
        in_specs=in_specs,
        out_specs=pl.BlockSpec((TOK_TILE, D_MODEL), lambda t: (out_tile0 + t, 0)),
        input_output_aliases=aliases,
        compiler_params=_params(("arbitrary",)),
        name=name,
    )(*args)


def kernel(x_prompt, x_sample, state_gla_fwd, state_gla_bwd, c, c_ctx, norm1_g, w_mod, b_mod, w_in,
           w_alpha, b_alpha, gla_norm_g, w_pool_grp, pool_scale, w_branch_gla, w_branch_pool, w_out,
           norm2_g, w_router, b_router, w_gate, b_gate, w_up, b_up, w_down, b_down, final_norm_g):
    l = 0
    x_ctx = x_prompt.reshape(N_CTX, D_MODEL)
    x_lat = x_sample.reshape(N_LAT, D_MODEL)

    cvec = jnp.concatenate([c_ctx[None, :], c, jnp.zeros((8 - 1 - DEC_BATCH, D_MODEL), F32)], axis=0)
    mod = _mod_call(cvec, w_mod[l], b_mod[l][None, :])
    mod3 = mod.reshape(8, 1, N_MOD * D_MODEL)

    w_in_b = w_in[l].astype(BF16)
    w_main = w_in_b[:, :MAIN_W]
    w_alr = w_in_b[:, MAIN_W:MAIN_W + ALR_W]
    w_xp = w_in_b[:, MAIN_W + ALR_W:MAIN_W + ALR_W + POOL_W]
    w_mg = w_in_b[:, MAIN_W + ALR_W + POOL_W:]
    q, k, v, og, alr, xp, mg = _inproj_call(x_ctx, x_lat, mod3, norm1_g[l][None, :],
                                            w_main, w_alr, w_xp, w_mg)

    zpad = jnp.zeros((GLA_LOWRANK, QK_W), F32)
    wa_f = jnp.concatenate([w_alpha[l, 0], zpad], axis=0)
    wa_b = jnp.concatenate([zpad, w_alpha[l, 1]], axis=0)
    o_f, o_b, s_f, s_b = _gla_call(q, k, v, alr, wa_f, b_alpha[l, 0][None, :], wa_b,
                                   b_alpha[l, 1][None, :], state_gla_fwd[:, l], state_gla_bwd[:, l])

    pooled_c = _pool_ctx_call(xp)
    pooled_l = _pool_lat_call(xp)

    w_router_pad = jnp.pad(w_router[l], ((0, 0), (0, LANES - N_EXPERTS)))
    b_router_pad = jnp.pad(b_router[l], (0, LANES - N_EXPERTS))[None, :]
    x1, h2, top_idx, top_w = _post_call(
        x_ctx, x_lat, mod3, o_f, o_b, og, pooled_c, pooled_l, mg,
        gla_norm_g[l].reshape(1, V_W), w_pool_grp[l].astype(BF16), pool_scale[l][None, :],
        w_branch_gla[l].astype(BF16), w_branch_pool[l].astype(BF16), w_out[l].astype(BF16),
        norm2_g[l][None, :], w_router_pad, b_router_pad)

    rank, cnt = _route_call(top_idx)
    counts = cnt[0, :N_EXPERTS].astype(I32)
    padded = (counts + MOE_BLOCK - 1) // MOE_BLOCK * MOE_BLOCK
    pad_end = jnp.cumsum(padded).astype(I32)
    pad_start = pad_end - padded
    block_first = jnp.arange(N_SLOT_BLOCKS, dtype=I32) * MOE_BLOCK
    block_e = jnp.minimum(jnp.sum((pad_end[None, :] <= block_first[:, None]).astype(I32), axis=1),
                          N_EXPERTS - 1).astype(I32)
    n_used = (pad_end[-1:] // MOE_BLOCK).astype(I32)
    run_start = jnp.concatenate([jnp.ones((1,), I32), (block_e[1:] != block_e[:-1]).astype(I32)])
    parity = ((jnp.cumsum(run_start) - 1) % 2).astype(I32)
    after = pad_end[block_e] // MOE_BLOCK
    next_e = jnp.where(after < n_used[0], block_e[jnp.minimum(after, N_SLOT_BLOCKS - 1)], -1).astype(I32)
    experts = jnp.arange(N_EXPERTS, dtype=I32)
    tk = top_idx[:, :TOP_K]
    pos = jnp.sum(jnp.where(tk[:, :, None] == experts, pad_start, 0), axis=-1) + rank[:, :TOP_K]
    pos = pos.astype(I32)
    pos_by_choice = pos.T
    pos_chunks = pos_by_choice.reshape(TOP_K, N_TOK // SC_ROWS, SC_ROWS).transpose(1, 0, 2)

    hs = _sc_scatter_rows(h2, pos_chunks, N_SLOTS)
    y = _moe_call(block_e, n_used, next_e, parity, hs,
                  w_gate[l], b_gate[l][:, None, :], w_up[l], b_up[l][:, None, :],
                  w_down[l], b_down[l][:, None, :])
    outs = []
    chunk_tok = N_CTX // COMBINE_CHUNKS
    for group, tok0 in (("ctx", 0), ("lat", N_CTX)):
        out = None
        for ci in range(COMBINE_CHUNKS):
            t0 = tok0 + ci * chunk_tok
            idx = pos_by_choice[:, t0:t0 + chunk_tok].reshape(TOP_K * chunk_tok)
            gathered = _sc_gather_rows(y, idx).reshape(TOP_K, chunk_tok, HALF_W)
            out = _combine_call(x1, mod3, top_w, final_norm_g[None, :], gathered, out,
                                t0 // TOK_TILE, ci * chunk_tok // TOK_TILE, N_CTX // TOK_TILE,
                                "combine_%s%d" % (group, ci))
        outs.append(out)
    y_prompt = outs[0].reshape(BATCH, SEQ, D_MODEL)
    y_sample = outs[1].reshape(DEC_BATCH, DEC_SEQ, D_MODEL)
    return (y_prompt, y_sample, s_f[:, None], s_b[:, None])
```

```python
import functools

import jax
import jax.numpy as jnp
from jax import lax
from jax.experimental import pallas as pl
from jax.experimental.pallas import tpu as pltpu
from jax.experimental.pallas import tpu_sc as plsc

F32 = jnp.float32
BF16 = jnp.bfloat16
I32 = jnp.int32

D_MODEL = 1024
BATCH = 32
SEQ = 256
DEC_BATCH = 4
DEC_SEQ = 2048
GRID_W = 64
GLA_HEADS = 4
GLA_DK = 128
GLA_DV = 256
GLA_LOWRANK = 16
GLA_TAU = 16.0
GLA_CHUNK = 64
POOL_GROUPS = 4
POOL_GROUP_DIM = 128
POOL_WINDOWS = (2, 4, 8, 16)
N_EXPERTS = 32
TOP_K = 4
D_FF = 1024
SWIGLU_LIMIT = 7.0
SWIGLU_ALPHA = 1.702
MOE_BLOCK = 256
NORM_EPS = 1e-6
N_MOD = 6

QK_W = GLA_HEADS * GLA_DK
V_W = GLA_HEADS * GLA_DV
POOL_W = POOL_GROUPS * POOL_GROUP_DIM
MAIN_W = 2 * QK_W + 2 * V_W
ALR_W = 2 * GLA_LOWRANK
MG_W = 2 * D_MODEL

N_CTX = BATCH * SEQ
N_LAT = DEC_BATCH * DEC_SEQ
N_TOK = N_CTX + N_LAT
N_SLOT_BLOCKS = -(-(N_TOK * TOP_K + N_EXPERTS * (MOE_BLOCK - 1)) // MOE_BLOCK)
N_SLOTS = N_SLOT_BLOCKS * MOE_BLOCK

LANES = 128
TOK_TILE = 256
N_TILES = N_TOK // TOK_TILE
CTX_TILES = N_CTX // TOK_TILE
LAT_TILES_PER_SEQ = DEC_SEQ // TOK_TILE
ROUTE_TILE = 1024
VMEM_LIMIT = 56 * 1024 * 1024

GLA_SEQS = 4
CTX_CHUNKS = SEQ // GLA_CHUNK
LAT_CHUNKS = DEC_SEQ // GLA_CHUNK
CHUNKS_PER_TILE = TOK_TILE // GLA_CHUNK
GLA_CTX_STEPS = (BATCH // GLA_SEQS) * CTX_CHUNKS
TILE_GRID = 8

NT_DIMS = (((1,), (1,)), ((), ()))
TN_DIMS = (((0,), (0,)), ((), ()))

assert DEC_BATCH == GLA_SEQS and SEQ == TOK_TILE and N_TILES == TILE_GRID * TILE_GRID


def _params(semantics, vmem=VMEM_LIMIT):
    return pltpu.CompilerParams(dimension_semantics=semantics, vmem_limit_bytes=vmem)


def _split_bf16(a):
    hi = a.astype(BF16)
    lo = (a - hi.astype(F32)).astype(BF16)
    return hi, lo


def _dot(a, b):
    return jnp.dot(a, b, preferred_element_type=F32)


def _dot3(a, b):
    a_hi, a_lo = _split_bf16(a)
    b_hi, b_lo = _split_bf16(b)
    return _dot(a_hi, b_hi) + _dot(a_lo, b_hi) + _dot(a_hi, b_lo)


def _sigmoid(x):
    return 1.0 / (1.0 + jnp.exp(-x))


HALF_W = D_MODEL // 2
HIGH_HALF_MASK = -65536


def _pack_halves(lo, hi):
    lo_bits = pltpu.bitcast(lo.astype(BF16).astype(F32), I32)
    hi_bits = pltpu.bitcast(hi.astype(BF16).astype(F32), I32)
    return lax.shift_right_logical(lo_bits, 16) | (hi_bits & HIGH_HALF_MASK)


def _unpack_halves(words):
    lo = pltpu.bitcast(lax.shift_left(words, 16), F32)
    hi = pltpu.bitcast(words & HIGH_HALF_MASK, F32)
    return lo, hi


def _rms(x):
    return x * lax.rsqrt(jnp.mean(x * x, axis=-1, keepdims=True) + NORM_EPS)


def _mod_row(t):
    return jnp.where(t < CTX_TILES, 0, 1 + (t - CTX_TILES) // LAT_TILES_PER_SEQ)


def _store_tile(t):
    u = t - CTX_TILES
    return jnp.where(t < CTX_TILES, t,
                     CTX_TILES + DEC_BATCH * (u % LAT_TILES_PER_SEQ) + u // LAT_TILES_PER_SEQ)


def _ctx_tile(t):
    return jnp.minimum(t, CTX_TILES - 1)


def _lat_tile(t):
    return jnp.maximum(t - CTX_TILES, 0)


def _mod_kernel(c_ref, w_ref, b_ref, o_ref):
    c = c_ref[...]
    o_ref[...] = _dot3(c * _sigmoid(c), w_ref[...]) + b_ref[...]


def _mod_call(cvec, w_mod, b_mod):
    rows = cvec.shape[0]
    return pl.pallas_call(
        _mod_kernel,
        out_shape=jax.ShapeDtypeStruct((rows, N_MOD * D_MODEL), F32),
        grid=(N_MOD,),
        in_specs=[
            pl.BlockSpec((rows, D_MODEL), lambda j: (0, 0)),
            pl.BlockSpec((D_MODEL, D_MODEL), lambda j: (0, j)),
            pl.BlockSpec((1, D_MODEL), lambda j: (0, j)),
        ],
        out_specs=pl.BlockSpec((rows, D_MODEL), lambda j: (0, j)),
        compiler_params=_params(("arbitrary",)),
        name="mod",
    )(cvec, w_mod, b_mod)


def _inproj_kernel(xc_ref, xl_ref, mod_ref, g_ref, wmain_ref, walr_ref, wxp_ref, wmg_ref,
                   q_ref, k_ref, v_ref, og_ref, alr_ref, xp_ref, mg_ref):
    t = pl.program_id(0)
    x = jnp.where(t < CTX_TILES, xc_ref[...], xl_ref[...])
    mod = mod_ref[0]
    shift1 = mod[:, 0:D_MODEL]
    scale1 = mod[:, D_MODEL:2 * D_MODEL]
    h = _rms(x) * g_ref[...]
    h = (h * (1.0 + scale1) + shift1).astype(BF16)
    z = _dot(h, wmain_ref[...])
    q_ref[...] = (z[:, 0:QK_W] * (GLA_DK ** -0.5)).astype(BF16)
    k_ref[...] = z[:, QK_W:2 * QK_W].astype(BF16)
    v_ref[...] = z[:, 2 * QK_W:2 * QK_W + V_W].astype(BF16)
    og = z[:, 2 * QK_W + V_W:MAIN_W]
    og_ref[...] = (og * _sigmoid(og)).astype(BF16)
    alr_ref[...] = _dot(h, walr_ref[...])
    xp_ref[...] = _dot(h, wxp_ref[...])
    mg_ref[...] = _sigmoid(_dot(h, wmg_ref[...])).astype(BF16)


def _inproj_call(x_ctx, x_lat, mod3, norm1_g, w_main, w_alr, w_xp, w_mg):
    const = lambda t: (0, 0)
    stored = lambda t: (_store_tile(t), 0)
    widths = (QK_W, QK_W, V_W, V_W, ALR_W, POOL_W, MG_W)
    dtypes = (BF16, BF16, BF16, BF16, F32, F32, BF16)
    return pl.pallas_call(
        _inproj_kernel,
        out_shape=[jax.ShapeDtypeStruct((N_TOK, w), dt) for w, dt in zip(widths, dtypes)],
        grid=(N_TILES,),
        in_specs=[
            pl.BlockSpec((TOK_TILE, D_MODEL), lambda t: (_ctx_tile(t), 0)),
            pl.BlockSpec((TOK_TILE, D_MODEL), lambda t: (_lat_tile(t), 0)),
            pl.BlockSpec((1, 1, N_MOD * D_MODEL), lambda t: (_mod_row(t), 0, 0)),
            pl.BlockSpec((1, D_MODEL), const),
            pl.BlockSpec((D_MODEL, MAIN_W), const),
            pl.BlockSpec((D_MODEL, ALR_W), const),
            pl.BlockSpec((D_MODEL, POOL_W), const),
            pl.BlockSpec((D_MODEL, MG_W), const),
        ],
        out_specs=[pl.BlockSpec((TOK_TILE, w), stored) for w in widths],
        compiler_params=_params(("arbitrary",)),
        name="inproj",
    )(x_ctx, x_lat, mod3, norm1_g, w_main, w_alr, w_xp, w_mg)


def _gla_direction(q_ref, k_ref, v_ref, alr_ref, wa_ref, ba_ref, o_ref, st_ref, d, rev):
    rows = GLA_SEQS * GLA_CHUNK
    stack = lambda ref, cols: jnp.concatenate([ref[0, s, :, cols] for s in range(GLA_SEQS)], axis=0)
    alr = stack(alr_ref, slice(None))
    a = _dot3(alr, wa_ref[...]) + ba_ref[...]
    g = (jnp.minimum(a, 0.0) - jnp.log(1.0 + jnp.exp(-jnp.abs(a)))) * (1.0 / GLA_TAU)

    row = lax.broadcasted_iota(I32, (rows, rows), 0)
    col = lax.broadcasted_iota(I32, (rows, rows), 1)
    same = (row // GLA_CHUNK) == (col // GLA_CHUNK)
    tri = same & ((col >= row) if rev else (col <= row))
    tri_b = jnp.where(tri, 1.0, 0.0).astype(BF16)
    g_hi, g_lo = _split_bf16(g)
    bcum = _dot(tri_b, g_hi) + _dot(tri_b, g_lo)

    def per_seq_row(r):
        return jnp.concatenate(
            [jnp.broadcast_to(bcum[s * GLA_CHUNK + r:s * GLA_CHUNK + r + 1], (GLA_CHUNK, QK_W))
             for s in range(GLA_SEQS)], axis=0)

    r_last = 0 if rev else GLA_CHUNK - 1
    blast = per_seq_row(r_last)
    bmid = per_seq_row(GLA_CHUNK // 2)
    e_q = jnp.exp(bcum - bmid)
    e_k = jnp.exp(bmid - bcum)
    e_in = jnp.exp(bcum)
    e_out = jnp.exp(blast - bcum)
    q = stack(q_ref, slice(None)).astype(F32)
    k = stack(k_ref, slice(None)).astype(F32)

    wide = (rows, GLA_SEQS * GLA_DK)
    own = (lax.broadcasted_iota(I32, wide, 0) // GLA_CHUNK) == (lax.broadcasted_iota(I32, wide, 1) // GLA_DK)

    def block_diag(x):
        return jnp.where(own, jnp.concatenate([x] * GLA_SEQS, axis=1), 0.0).astype(BF16)

    for h in range(GLA_HEADS):
        ks = slice(h * GLA_DK, (h + 1) * GLA_DK)
        vs = slice(h * GLA_DV, (h + 1) * GLA_DV)
        qh = q[:, ks]
        kh = k[:, ks]
        vh = stack(v_ref, vs)
        att = lax.dot_general((qh * e_q[:, ks]).astype(BF16), (kh * e_k[:, ks]).astype(BF16),
                              NT_DIMS, preferred_element_type=F32)
        att = jnp.where(tri, att, 0.0).astype(BF16)
        st = st_ref[d, h]
        o_inter = lax.dot_general(block_diag(qh * e_in[:, ks]), st.astype(BF16), NT_DIMS,
                                  preferred_element_type=F32)
        o_h = o_inter + _dot(att, vh)
        for s in range(GLA_SEQS):
            o_ref[0, s, :, vs] = o_h[s * GLA_CHUNK:(s + 1) * GLA_CHUNK]
        upd = lax.dot_general(vh, block_diag(kh * e_out[:, ks]), TN_DIMS,
                              preferred_element_type=F32)
        e_last = jnp.concatenate(
            [jnp.exp(bcum[s * GLA_CHUNK + r_last:s * GLA_CHUNK + r_last + 1, ks])
             for s in range(GLA_SEQS)], axis=1)
        st_ref[d, h] = st * e_last + upd


def _gla_kernel(qf_ref, kf_ref, vf_ref, af_ref, qb_ref, kb_ref, vb_ref, ab_ref,
                waf_ref, baf_ref, wab_ref, bab_ref, s0f_ref, s0b_ref,
                of_ref, ob_ref, sf_ref, sb_ref, st_ref):
    i = pl.program_id(0)
    is_ctx = i < GLA_CTX_STEPS
    chunk = jnp.where(is_ctx, i % CTX_CHUNKS, i - GLA_CTX_STEPS)

    @pl.when(is_ctx & (chunk == 0))
    def _():
        st_ref[...] = jnp.zeros(st_ref.shape, F32)

    @pl.when(i == GLA_CTX_STEPS)
    def _():
        for s in range(GLA_SEQS):
            ls = slice(s * GLA_DK, (s + 1) * GLA_DK)
            for h in range(GLA_HEADS):
                st_ref[0, h, :, ls] = s0f_ref[s, h].T
                st_ref[1, h, :, ls] = s0b_ref[s, h].T

    _gla_direction(qf_ref, kf_ref, vf_ref, af_ref, waf_ref, baf_ref, of_ref, st_ref, 0, False)
    _gla_direction(qb_ref, kb_ref, vb_ref, ab_ref, wab_ref, bab_ref, ob_ref, st_ref, 1, True)

    @pl.when(is_ctx & (chunk == CTX_CHUNKS - 1))
    def _():
        for s in range(GLA_SEQS):
            ls = slice(s * GLA_DK, (s + 1) * GLA_DK)
            for h in range(GLA_HEADS):
                sf_ref[s, h] = st_ref[0, h, :, ls].T
                sb_ref[s, h] = st_ref[1, h, :, ls].T


def _gla_block(i, rev):
    is_ctx = i < GLA_CTX_STEPS
    group = i // CTX_CHUNKS
    c_ctx = i % CTX_CHUNKS
    c_lat = i - GLA_CTX_STEPS
    if rev:
        c_ctx = CTX_CHUNKS - 1 - c_ctx
        c_lat = LAT_CHUNKS - 1 - c_lat
    j = c_lat // CHUNKS_PER_TILE
    per_row = TILE_GRID // GLA_SEQS
    a = jnp.where(is_ctx, group // per_row, CTX_TILES // TILE_GRID + j // per_row)
    b = jnp.where(is_ctx, group % per_row, j % per_row)
    c = jnp.where(is_ctx, c_ctx, c_lat % CHUNKS_PER_TILE)
    return (a, b, c, 0)


def _gla_call(q, k, v, alr, wa_f, ba_f, wa_b, ba_b, s0_f, s0_b):
    def view(arr):
        return arr.reshape(TILE_GRID, TILE_GRID, TOK_TILE, arr.shape[-1])

    def spec(width, rev):
        return pl.BlockSpec((1, GLA_SEQS, GLA_CHUNK, width), lambda i: _gla_block(i, rev))

    const = lambda i: (0, 0)
    st_block = (GLA_SEQS, GLA_HEADS, GLA_DK, GLA_DV)
    whole_state = pl.BlockSpec(st_block, lambda i: (0, 0, 0, 0))
    ctx_state = pl.BlockSpec(
        st_block, lambda i: (jnp.minimum(i // CTX_CHUNKS, BATCH // GLA_SEQS - 1), 0, 0, 0))
    in_specs = []
    for rev in (False, True):
        in_specs += [spec(QK_W, rev), spec(QK_W, rev), spec(V_W, rev), spec(ALR_W, rev)]
    in_specs += [pl.BlockSpec((ALR_W, QK_W), const), pl.BlockSpec((1, QK_W), const)] * 2
    in_specs += [whole_state, whole_state]
    o_shape = jax.ShapeDtypeStruct((TILE_GRID, TILE_GRID, TOK_TILE, V_W), F32)
    s_shape = jax.ShapeDtypeStruct((BATCH, GLA_HEADS, GLA_DK, GLA_DV), F32)
    qv, kv, vv, av = view(q), view(k), view(v), view(alr)
    o_f, o_b, s_f, s_b = pl.pallas_call(
        _gla_kernel,
        out_shape=[o_shape, o_shape, s_shape, s_shape],
        grid=(GLA_CTX_STEPS + LAT_CHUNKS,),
        in_specs=in_specs,
        out_specs=[spec(V_W, False), spec(V_W, True), ctx_state, ctx_state],
        scratch_shapes=[pltpu.VMEM((2, GLA_HEADS, GLA_DV, GLA_SEQS * GLA_DK), F32)],
        compiler_params=_params(("arbitrary",)),
        name="gla",
    )(qv, kv, vv, av, qv, kv, vv, av, wa_f, ba_f, wa_b, ba_b, s0_f, s0_b)
    return o_f.reshape(N_TOK, V_W), o_b.reshape(N_TOK, V_W), s_f, s_b


def _band(n, w, block):
    row = lax.broadcasted_iota(I32, (n, n), 0)
    col = lax.broadcasted_iota(I32, (n, n), 1)
    inside = (col >= row - w // 2) & (col <= row + w // 2 - 1)
    if block < n:
        inside = inside & ((row // block) == (col // block))
    return jnp.where(inside, 1.0, 0.0).astype(BF16)


def _win_count(p, n, w):
    return jnp.minimum(p + w // 2 - 1, n - 1) - jnp.maximum(p - w // 2, 0) + 1


def _pool_ctx_kernel(x_ref, o_ref, band_ref):
    @pl.when(pl.program_id(0) == 0)
    def _():
        for gi, w in enumerate(POOL_WINDOWS):
            band_ref[gi] = _band(SEQ, w, SEQ)

    p = lax.broadcasted_iota(I32, (SEQ, POOL_GROUP_DIM), 0)
    for gi, w in enumerate(POOL_WINDOWS):
        cs = slice(gi * POOL_GROUP_DIM, (gi + 1) * POOL_GROUP_DIM)
        x = x_ref[:, cs]
        hi, lo = _split_bf16(x)
        band = band_ref[gi]
        s = _dot(band, hi) + _dot(band, lo)
        cnt = _win_count(p, SEQ, w).astype(F32)
        o_ref[:, cs] = s / cnt - x


def _pool_ctx_call(xp):
    spec = pl.BlockSpec((SEQ, POOL_W), lambda b: (b, 0))
    return pl.pallas_call(
        _pool_ctx_kernel,
        out_shape=jax.ShapeDtypeStruct((N_CTX, POOL_W), F32),
        grid=(BATCH,),
        in_specs=[spec],
        out_specs=spec,
        scratch_shapes=[pltpu.VMEM((POOL_GROUPS, SEQ, SEQ), BF16)],
        compiler_params=_params(("arbitrary",)),
        name="pool_ctx",
    )(xp)


POOL_HALO = (max(POOL_WINDOWS) // 2) * GRID_W


def _pool_lat_kernel(x_ref, o_ref, pad_ref):
    rows = DEC_SEQ // GRID_W
    p = lax.broadcasted_iota(I32, (DEC_SEQ, POOL_GROUP_DIM), 0)
    r = p // GRID_W
    cidx = p % GRID_W
    zeros = jnp.zeros((POOL_HALO, POOL_GROUP_DIM), F32)
    pad_ref[0:POOL_HALO, :] = zeros
    pad_ref[POOL_HALO + DEC_SEQ:2 * POOL_HALO + DEC_SEQ, :] = zeros
    for gi, w in enumerate(POOL_WINDOWS):
        cs = slice(gi * POOL_GROUP_DIM, (gi + 1) * POOL_GROUP_DIM)
        band = _band(TOK_TILE, w, GRID_W)
        for t in range(LAT_TILES_PER_SEQ):
            hi, lo = _split_bf16(x_ref[t, 0, :, cs])
            pad_ref[POOL_HALO + t * TOK_TILE:POOL_HALO + (t + 1) * TOK_TILE, :] = (
                _dot(band, hi) + _dot(band, lo))
        acc = jnp.zeros((DEC_SEQ, POOL_GROUP_DIM), F32)
        for dr in range(-(w // 2), w // 2):
            start = POOL_HALO + dr * GRID_W
            acc = acc + pad_ref[start:start + DEC_SEQ, :]
        cnt = (_win_count(r, rows, w) * _win_count(cidx, GRID_W, w)).astype(F32)
        pooled = acc / cnt
        for t in range(LAT_TILES_PER_SEQ):
            rs = slice(t * TOK_TILE, (t + 1) * TOK_TILE)
            o_ref[t, 0, :, cs] = pooled[rs] - x_ref[t, 0, :, cs]


def _pool_lat_call(xp):
    view = xp.reshape(N_TILES // DEC_BATCH, DEC_BATCH, TOK_TILE, POOL_W)
    blk = (LAT_TILES_PER_SEQ, 1, TOK_TILE, POOL_W)
    out = pl.pallas_call(
        _pool_lat_kernel,
        out_shape=jax.ShapeDtypeStruct((LAT_TILES_PER_SEQ, DEC_BATCH, TOK_TILE, POOL_W), F32),
        grid=(DEC_BATCH,),
        in_specs=[pl.BlockSpec(blk, lambda s: (CTX_TILES // DEC_BATCH // LAT_TILES_PER_SEQ, s, 0, 0))],
        out_specs=pl.BlockSpec(blk, lambda s: (0, s, 0, 0)),
        scratch_shapes=[pltpu.VMEM((DEC_SEQ + 2 * POOL_HALO, POOL_GROUP_DIM), F32)],
        compiler_params=_params(("arbitrary",)),
        name="pool_lat",
    )(view)
    return out.reshape(N_LAT, POOL_W)


def _post_kernel(xc_ref, xl_ref, mod_ref, of_ref, ob_ref, og_ref, pc_ref, pl_ref, mg_ref, gng_ref,
                 wpg_ref, psc_ref, wbg_ref, wbp_ref, wout_ref, n2g_ref, wr_ref, br_ref,
                 x1_ref, h2_ref, idx_ref, tw_ref):
    t = pl.program_id(0)
    is_ctx = t < CTX_TILES
    x = jnp.where(is_ctx, xc_ref[...], xl_ref[...])
    pooled = jnp.where(is_ctx, pc_ref[...], pl_ref[...])
    mod = mod_ref[0]
    gate1 = mod[:, 2 * D_MODEL:3 * D_MODEL]
    shift2 = mod[:, 3 * D_MODEL:4 * D_MODEL]
    scale2 = mod[:, 4 * D_MODEL:5 * D_MODEL]

    o = of_ref[...] + ob_ref[...]
    og = og_ref[...].astype(F32)
    gated = []
    for h in range(GLA_HEADS):
        vs = slice(h * GLA_DV, (h + 1) * GLA_DV)
        oh = _rms(o[:, vs]) * gng_ref[:, vs]
        gated.append((oh * og[:, vs]).astype(BF16))
    br_gla = _dot(jnp.concatenate(gated, axis=-1), wbg_ref[...])

    pm = []
    for gi in range(POOL_GROUPS):
        cs = slice(gi * POOL_GROUP_DIM, (gi + 1) * POOL_GROUP_DIM)
        pmg = _dot(pooled[:, cs].astype(BF16), wpg_ref[gi]) * psc_ref[:, cs]
        pm.append(pmg.astype(BF16))
    br_pool = _dot(jnp.concatenate(pm, axis=-1), wbp_ref[...])

    mg = mg_ref[...].astype(F32)
    merged = mg[:, 0:D_MODEL] * br_gla + mg[:, D_MODEL:MG_W] * br_pool
    m = _dot(merged.astype(BF16), wout_ref[...])
    x1 = x + gate1 * m
    x1_ref[...] = x1
    h2 = _rms(x1) * n2g_ref[...]
    h2 = h2 * (1.0 + scale2) + shift2
    h2_ref[...] = _pack_halves(h2[:, :HALF_W], h2[:, HALF_W:])

    logits = _dot3(h2, wr_ref[...]) + br_ref[...]
    lane = lax.broadcasted_iota(I32, (TOK_TILE, LANES), 1)
    lane_f = lane.astype(F32)
    neg = jnp.float32(-jnp.inf)
    cur = jnp.where(lane < N_EXPERTS, logits, neg)
    vals, idxs = [], []
    for _ in range(TOP_K):
        mx = jnp.max(cur, axis=-1, keepdims=True)
        ix = jnp.min(jnp.where(cur == mx, lane_f, float(LANES)), axis=-1, keepdims=True)
        vals.append(mx)
        idxs.append(ix)
        cur = jnp.where(lane_f == ix, neg, cur)
    ex = [jnp.exp(vv - vals[0]) for vv in vals]
    tot = ex[0] + ex[1] + ex[2] + ex[3]
    idx_out = jnp.zeros((TOK_TILE, LANES), F32)
    w_out = jnp.zeros((TOK_TILE, LANES), F32)
    for kk in range(TOP_K):
        idx_out = jnp.where(lane == kk, idxs[kk], idx_out)
        w_out = jnp.where(lane == kk, ex[kk] / tot, w_out)
    idx_ref[...] = idx_out.astype(I32)
    tw_ref[...] = w_out


def _post_call(x_ctx, x_lat, mod3, o_f, o_b, og, pooled_c, pooled_l, mg, gng, wpg, psc, wbg, wbp,
               wout, n2g, wr, br):
    row = lambda t: (t, 0)
    const = lambda t: (0, 0)
    stored = lambda t: (_store_tile(t), 0)
    return pl.pallas_call(
        _post_kernel,
        out_shape=[
            jax.ShapeDtypeStruct((N_TOK, D_MODEL), F32),
            jax.ShapeDtypeStruct((N_TOK, HALF_W), I32),
            jax.ShapeDtypeStruct((N_TOK, LANES), I32),
            jax.ShapeDtypeStruct((N_TOK, LANES), F32),
        ],
        grid=(N_TILES,),
        in_specs=[
            pl.BlockSpec((TOK_TILE, D_MODEL), lambda t: (_ctx_tile(t), 0)),
            pl.BlockSpec((TOK_TILE, D_MODEL), lambda t: (_lat_tile(t), 0)),
            pl.BlockSpec((1, 1, N_MOD * D_MODEL), lambda t: (_mod_row(t), 0, 0)),
            pl.BlockSpec((TOK_TILE, V_W), stored),
            pl.BlockSpec((TOK_TILE, V_W), stored),
            pl.BlockSpec((TOK_TILE, V_W), stored),
            pl.BlockSpec((TOK_TILE, POOL_W), lambda t: (_ctx_tile(t), 0)),
            pl.BlockSpec((TOK_TILE, POOL_W),
                         lambda t: (jnp.maximum(_store_tile(t) - CTX_TILES, 0), 0)),
            pl.BlockSpec((TOK_TILE, MG_W), stored),
            pl.BlockSpec((1, V_W), const),
            pl.BlockSpec((POOL_GROUPS, POOL_GROUP_DIM, POOL_GROUP_DIM), lambda t: (0, 0, 0)),
            pl.BlockSpec((1, POOL_W), const),
            pl.BlockSpec((V_W, D_MODEL), const),
            pl.BlockSpec((POOL_W, D_MODEL), const),
            pl.BlockSpec((D_MODEL, D_MODEL), const),
            pl.BlockSpec((1, D_MODEL), const),
            pl.BlockSpec((D_MODEL, LANES), const),
            pl.BlockSpec((1, LANES), const),
        ],
        out_specs=[
            pl.BlockSpec((TOK_TILE, D_MODEL), row),
            pl.BlockSpec((TOK_TILE, HALF_W), row),
            pl.BlockSpec((TOK_TILE, LANES), row),
            pl.BlockSpec((TOK_TILE, LANES), row),
        ],
        compiler_params=_params(("arbitrary",)),
        name="post",
    )(x_ctx, x_lat, mod3, o_f, o_b, og, pooled_c, pooled_l, mg, gng, wpg, psc, wbg, wbp, wout, n2g,
      wr, br)


def _route_kernel(idx_ref, rank_ref, cnt_ref, carry_ref, strict_ref):
    t = pl.program_id(0)

    @pl.when(t == 0)
    def _():
        carry_ref[...] = jnp.zeros((1, LANES), F32)
        row = lax.broadcasted_iota(I32, (ROUTE_TILE, ROUTE_TILE), 0)
        col = lax.broadcasted_iota(I32, (ROUTE_TILE, ROUTE_TILE), 1)
        strict_ref[...] = jnp.where(col < row, 1.0, 0.0).astype(BF16)

    idx = idx_ref[...]
    lane = lax.broadcasted_iota(I32, (ROUTE_TILE, LANES), 1)
    sel = [lane == idx[:, kk:kk + 1] for kk in range(TOP_K)]
    onehot = jnp.zeros((ROUTE_TILE, LANES), F32)
    for kk in range(TOP_K):
        onehot = onehot + jnp.where(sel[kk], 1.0, 0.0)
    before = _dot(strict_ref[...], onehot.astype(BF16)) + carry_ref[...]
    rank = jnp.zeros((ROUTE_TILE, LANES), F32)
    for kk in range(TOP_K):
        rk = jnp.sum(jnp.where(sel[kk], before, 0.0), axis=-1, keepdims=True)
        rank = jnp.where(lane == kk, rk, rank)
    rank_ref[...] = rank.astype(I32)
    carry_ref[...] = carry_ref[...] + jnp.sum(onehot, axis=0, keepdims=True)
    cnt_ref[...] = jnp.broadcast_to(carry_ref[...], (8, LANES))


def _route_call(idx):
    return pl.pallas_call(
        _route_kernel,
        out_shape=[
            jax.ShapeDtypeStruct((N_TOK, LANES), I32),
            jax.ShapeDtypeStruct((8, LANES), F32),
        ],
        grid=(N_TOK // ROUTE_TILE,),
        in_specs=[pl.BlockSpec((ROUTE_TILE, LANES), lambda t: (t, 0))],
        out_specs=[
            pl.BlockSpec((ROUTE_TILE, LANES), lambda t: (t, 0)),
            pl.BlockSpec((8, LANES), lambda t: (0, 0)),
        ],
        scratch_shapes=[pltpu.VMEM((1, LANES), F32), pltpu.VMEM((ROUTE_TILE, ROUTE_TILE), BF16)],
        compiler_params=_params(("arbitrary",)),
        name="route",
    )(idx)


def _moe_kernel(be_ref, nu_ref, ne_ref, par_ref, x_ref, wg_hbm, bg_ref, wu_hbm, bu_ref, wd_hbm,
                bd_ref, y_ref, wf_ref, wgu_ref, wdb_ref, sems):
    b = pl.program_id(0)
    n_used = nu_ref[0]
    e = be_ref[b]
    prev = be_ref[jnp.maximum(b - 1, 0)]
    live = b < n_used

    def weight_copies(expert, slot):
        return [pltpu.make_async_copy(w.at[expert], wf_ref.at[slot, i], sems.at[slot])
                for i, w in enumerate((wg_hbm, wu_hbm, wd_hbm))]

    @pl.when(live & ((b == 0) | (e != prev)))
    def _():
        slot = par_ref[b]

        @pl.when(b == 0)
        def _():
            for cp in weight_copies(e, slot):
                cp.start()

        for cp in weight_copies(e, slot):
            cp.wait()
        nxt = ne_ref[b]

        @pl.when(nxt >= 0)
        def _():
            for cp in weight_copies(nxt, 1 - slot):
                cp.start()

        wgu_ref[:, :D_FF] = wf_ref[slot, 0].astype(BF16)
        wgu_ref[:, D_FF:] = wf_ref[slot, 1].astype(BF16)
        wdb_ref[...] = wf_ref[slot, 2].astype(BF16)

    @pl.when(live)
    def _():
        x_lo, x_hi = _unpack_halves(x_ref[...])
        x = jnp.concatenate([x_lo.astype(BF16), x_hi.astype(BF16)], axis=-1)
        gu = _dot(x, wgu_ref[...])
        gate = jnp.minimum(gu[:, :D_FF] + bg_ref[0], SWIGLU_LIMIT)
        up = jnp.clip(gu[:, D_FF:] + bu_ref[0], -SWIGLU_LIMIT, SWIGLU_LIMIT)
        act = (up + 1.0) * (gate * _sigmoid(SWIGLU_ALPHA * gate))
        y = _dot(act.astype(BF16), wdb_ref[...]) + bd_ref[0]
        y_ref[...] = _pack_halves(y[:, :HALF_W], y[:, HALF_W:])

    @pl.when(jnp.logical_not(live))
    def _():
        y_ref[...] = jnp.zeros((MOE_BLOCK, HALF_W), I32)


def _moe_call(block_e, n_used, next_e, parity, hs, w_gate, b_gate, w_up, b_up, w_down, b_down):
    def blk(b, be, nu, ne, par):
        return jnp.minimum(b, nu[0] - 1)

    row = lambda b, be, nu, ne, par: (blk(b, be, nu, ne, par), 0)
    bsel = lambda b, be, nu, ne, par: (be[blk(b, be, nu, ne, par)], 0, 0)
    any_spec = pl.BlockSpec(memory_space=pl.ANY)
    assert D_MODEL == D_FF
    return pl.pallas_call(
        _moe_kernel,
        out_shape=jax.ShapeDtypeStruct((N_SLOTS, HALF_W), I32),
        grid_spec=pltpu.PrefetchScalarGridSpec(
            num_scalar_prefetch=4,
            grid=(N_SLOT_BLOCKS,),
            in_specs=[
                pl.BlockSpec((MOE_BLOCK, HALF_W), row),
                any_spec,
                pl.BlockSpec((1, 1, D_FF), bsel),
                any_spec,
                pl.BlockSpec((1, 1, D_FF), bsel),
                any_spec,
                pl.BlockSpec((1, 1, D_MODEL), bsel),
            ],
            out_specs=pl.BlockSpec((MOE_BLOCK, HALF_W), lambda b, be, nu, ne, par: (b, 0)),
            scratch_shapes=[
                pltpu.VMEM((2, 3, D_MODEL, D_FF), F32),
                pltpu.VMEM((D_MODEL, 2 * D_FF), BF16),
                pltpu.VMEM((D_FF, D_MODEL), BF16),
                pltpu.SemaphoreType.DMA((2,)),
            ],
        ),
        compiler_params=_params(("arbitrary",)),
        name="moe",
    )(block_e, n_used, next_e, parity, hs, w_gate, b_gate, w_up, b_up, w_down, b_down)


SC_CORES = 2
SC_SUBCORES = 16
SC_WORKERS = SC_CORES * SC_SUBCORES
SC_ROWS = 128
COMBINE_CHUNKS = 4
assert N_CTX == N_LAT


def _sc_gather_rows(table, idx):
    n_idx = idx.shape[0]
    width = table.shape[1]
    per_worker = n_idx // SC_WORKERS
    n_chunks = per_worker // SC_ROWS
    assert n_chunks * SC_ROWS * SC_WORKERS == n_idx
    mesh = plsc.VectorSubcoreMesh(core_axis_name="c", subcore_axis_name="s")

    @functools.partial(
        pl.kernel, mesh=mesh,
        out_type=jax.ShapeDtypeStruct((n_idx, width), table.dtype),
        scratch_types=[pltpu.VMEM((SC_ROWS,), I32), pltpu.VMEM((SC_ROWS, width), table.dtype),
                       pltpu.SemaphoreType.DMA],
        name="sc_gather",
    )
    def gather(table_hbm, idx_hbm, out_hbm, idx_v, rows_v, sem):
        worker = lax.axis_index("s") * SC_CORES + lax.axis_index("c")
        base = worker * per_worker

        @pl.loop(0, n_chunks)
        def _(ch):
            off = pl.multiple_of(base + ch * SC_ROWS, SC_ROWS)
            pltpu.sync_copy(idx_hbm.at[pl.ds(off, SC_ROWS)], idx_v)
            pltpu.async_copy(table_hbm.at[idx_v], rows_v, sem).wait()
            pltpu.sync_copy(rows_v, out_hbm.at[pl.ds(off, SC_ROWS)])

    return gather(table, idx)


def _sc_scatter_rows(rows, idx3, n_out):
    n_rows, width = rows.shape
    n_chunks = n_rows // SC_ROWS // SC_WORKERS
    assert n_chunks * SC_ROWS * SC_WORKERS == n_rows and idx3.shape == (n_rows // SC_ROWS, TOP_K, SC_ROWS)
    mesh = plsc.VectorSubcoreMesh(core_axis_name="c", subcore_axis_name="s")

    @functools.partial(
        pl.kernel, mesh=mesh,
        out_type=jax.ShapeDtypeStruct((n_out, width), rows.dtype),
        scratch_types=[pltpu.VMEM((TOP_K, SC_ROWS), I32), pltpu.VMEM((SC_ROWS, width), rows.dtype),
                       pltpu.SemaphoreType.DMA],
        name="sc_scatter",
    )
    def scatter(rows_hbm, idx_hbm, out_hbm, idx_v, rows_v, sem):
        worker = lax.axis_index("s") * SC_CORES + lax.axis_index("c")

        @pl.loop(0, n_chunks)
        def _(ch):
            chunk = worker * n_chunks + ch
            pltpu.sync_copy(idx_hbm.at[chunk], idx_v)
            pltpu.sync_copy(rows_hbm.at[pl.ds(pl.multiple_of(chunk * SC_ROWS, SC_ROWS), SC_ROWS)],
                            rows_v)
            for kk in range(TOP_K):
                pltpu.async_copy(rows_v, out_hbm.at[idx_v.at[kk]], sem).wait()

    return scatter(rows, idx3)


def _combine_kernel(x1_ref, mod_ref, tw_ref, fg_ref, g_ref, *rest):
    out_ref = rest[-1]
    tw = tw_ref[...]
    f_lo = jnp.zeros((TOK_TILE, HALF_W), F32)
    f_hi = jnp.zeros((TOK_TILE, HALF_W), F32)
    for kk in range(TOP_K):
        lo, hi = _unpack_halves(g_ref[kk])
        f_lo = f_lo + lo * tw[:, kk:kk + 1]
        f_hi = f_hi + hi * tw[:, kk:kk + 1]
    gate2 = mod_ref[0][:, 5 * D_MODEL:6 * D_MODEL]
    x2 = x1_ref[...] + gate2 * jnp.concatenate([f_lo, f_hi], axis=-1)
    out_ref[...] = _rms(x2) * fg_ref[...]


def _combine_call(x1, mod3, tw, final_g, gathered, partial, tile0, out_tile0, out_tiles, name):
    n_tiles = gathered.shape[1] // TOK_TILE
    in_specs = [
        pl.BlockSpec((TOK_TILE, D_MODEL), lambda t: (tile0 + t, 0)),
        pl.BlockSpec((1, 1, N_MOD * D_MODEL), lambda t: (_mod_row(tile0 + t), 0, 0)),
        pl.BlockSpec((TOK_TILE, LANES), lambda t: (tile0 + t, 0)),
        pl.BlockSpec((1, D_MODEL), lambda t: (0, 0)),
        pl.BlockSpec((TOP_K, TOK_TILE, HALF_W), lambda t: (0, t, 0)),
    ]
    args = [x1, mod3, tw, final_g, gathered]
    aliases = {}
    if partial is not None:
        in_specs.append(pl.BlockSpec(memory_space=pl.ANY))
        args.append(partial)
        aliases = {len(args) - 1: 0}
    return pl.pallas_call(
        _combine_kernel,
        out_shape=jax.ShapeDtypeStruct((out_tiles * TOK_TILE, D_MODEL), F32),
        grid=(n_tiles,),
        in_specs=in_specs,
        out_specs=pl.BlockSpec((TOK_TILE, D_MODEL), lambda t: (out_tile0 + t, 0)),
        input_output_aliases=aliases,
        compiler_params=_params(("arbitrary",)),
        name=name,
    )(*args)


def kernel(x_prompt, x_sample, state_gla_fwd, state_gla_bwd, c, c_ctx, norm1_g, w_mod, b_mod, w_in,
           w_alpha, b_alpha, gla_norm_g, w_pool_grp, pool_scale, w_branch_gla, w_branch_pool, w_out,
           norm2_g, w_router, b_router, w_gate, b_gate, w_up, b_up, w_down, b_down, final_norm_g):
    l = 0
    x_ctx = x_prompt.reshape(N_CTX, D_MODEL)
    x_lat = x_sample.reshape(N_LAT, D_MODEL)

    cvec = jnp.concatenate([c_ctx[None, :], c, jnp.zeros((8 - 1 - DEC_BATCH, D_MODEL), F32)], axis=0)
    mod = _mod_call(cvec, w_mod[l], b_mod[l][None, :])
    mod3 = mod.reshape(8, 1, N_MOD * D_MODEL)

    w_in_b = w_in[l].astype(BF16)
    w_main = w_in_b[:, :MAIN_W]
    w_alr = w_in_b[:, MAIN_W:MAIN_W + ALR_W]
    w_xp = w_in_b[:, MAIN_W + ALR_W:MAIN_W + ALR_W + POOL_W]
    w_mg = w_in_b[:, MAIN_W + ALR_W + POOL_W:]
    q, k, v, og, alr, xp, mg = _inproj_call(x_ctx, x_lat, mod3, norm1_g[l][None, :],
                                            w_main, w_alr, w_xp, w_mg)

    zpad = jnp.zeros((GLA_LOWRANK, QK_W), F32)
    wa_f = jnp.concatenate([w_alpha[l, 0], zpad], axis=0)
    wa_b = jnp.concatenate([zpad, w_alpha[l, 1]], axis=0)
    o_f, o_b, s_f, s_b = _gla_call(q, k, v, alr, wa_f, b_alpha[l, 0][None, :], wa_b,
                                   b_alpha[l, 1][None, :], state_gla_fwd[:, l], state_gla_bwd[:, l])

    pooled_c = _pool_ctx_call(xp)
    pooled_l = _pool_lat_call(xp)

    w_router_pad = jnp.pad(w_router[l], ((0, 0), (0, LANES - N_EXPERTS)))
    b_router_pad = jnp.pad(b_router[l], (0, LANES - N_EXPERTS))[None, :]
    x1, h2, top_idx, top_w = _post_call(
        x_ctx, x_lat, mod3, o_f, o_b, og, pooled_c, pooled_l, mg,
        gla_norm_g[l].reshape(1, V_W), w_pool_grp[l].astype(BF16), pool_scale[l][None, :],
        w_branch_gla[l].astype(BF16), w_branch_pool[l].astype(BF16), w_out[l].astype(BF16),
        norm2_g[l][None, :], w_router_pad, b_router_pad)

    rank, cnt = _route_call(top_idx)
    counts = cnt[0, :N_EXPERTS].astype(I32)
    padded = (counts + MOE_BLOCK - 1) // MOE_BLOCK * MOE_BLOCK
    pad_end = jnp.cumsum(padded).astype(I32)
    pad_start = pad_end - padded
    block_first = jnp.arange(N_SLOT_BLOCKS, dtype=I32) * MOE_BLOCK
    block_e = jnp.minimum(jnp.sum((pad_end[None, :] <= block_first[:, None]).astype(I32), axis=1),
                          N_EXPERTS - 1).astype(I32)
    n_used = (pad_end[-1:] // MOE_BLOCK).astype(I32)
    run_start = jnp.concatenate([jnp.ones((1,), I32), (block_e[1:] != block_e[:-1]).astype(I32)])
    parity = ((jnp.cumsum(run_start) - 1) % 2).astype(I32)
    after = pad_end[block_e] // MOE_BLOCK
    next_e = jnp.where(after < n_used[0], block_e[jnp.minimum(after, N_SLOT_BLOCKS - 1)], -1).astype(I32)
    experts = jnp.arange(N_EXPERTS, dtype=I32)
    tk = top_idx[:, :TOP_K]
    pos = jnp.sum(jnp.where(tk[:, :, None] == experts, pad_start, 0), axis=-1) + rank[:, :TOP_K]
    pos = pos.astype(I32)
    pos_by_choice = pos.T
    pos_chunks = pos_by_choice.reshape(TOP_K, N_TOK // SC_ROWS, SC_ROWS).transpose(1, 0, 2)

    hs = _sc_scatter_rows(h2, pos_chunks, N_SLOTS)
    y = _moe_call(block_e, n_used, next_e, parity, hs,
                  w_gate[l], b_gate[l][:, None, :], w_up[l], b_up[l][:, None, :],
                  w_down[l], b_down[l][:, None, :])
    outs = []
    chunk_tok = N_CTX // COMBINE_CHUNKS
    for group, tok0 in (("ctx", 0), ("lat", N_CTX)):
        out = None
        for ci in range(COMBINE_CHUNKS):
            t0 = tok0 + ci * chunk_tok
            idx = pos_by_choice[:, t0:t0 + chunk_tok].reshape(TOP_K * chunk_tok)
            gathered = _sc_gather_rows(y, idx).reshape(TOP_K, chunk_tok, HALF_W)
            out = _combine_call(x1, mod3, top_w, final_norm_g[None, :], gathered, out,
                                t0 // TOK_TILE, ci * chunk_tok // TOK_TILE, N_CTX // TOK_TILE,
                                "combine_%s%d" % (group, ci))
        outs.append(out)
    y_prompt = outs[0].reshape(BATCH, SEQ, D_MODEL)
    y_sample = outs[1].reshape(DEC_BATCH, DEC_SEQ, D_MODEL)
    return (y_prompt, y_sample, s_f[:, None], s_b[:, None])
```

```python
import functools

import jax
import jax.numpy as jnp
from jax import lax
from jax.experimental import pallas as pl
from jax.experimental.pallas import tpu as pltpu
from jax.experimental.pallas import tpu_sc as plsc

F32 = jnp.float32
BF16 = jnp.bfloat16
I32 = jnp.int32

D_MODEL = 1024
BATCH = 32
SEQ = 256
DEC_BATCH = 4
DEC_SEQ = 2048
GRID_W = 64
GLA_HEADS = 4
GLA_DK = 128
GLA_DV = 256
GLA_LOWRANK = 16
GLA_TAU = 16.0
GLA_CHUNK = 64
POOL_GROUPS = 4
POOL_GROUP_DIM = 128
POOL_WINDOWS = (2, 4, 8, 16)
N_EXPERTS = 32
TOP_K = 4
D_FF = 1024
SWIGLU_LIMIT = 7.0
SWIGLU_ALPHA = 1.702
MOE_BLOCK = 256
NORM_EPS = 1e-6
N_MOD = 6

QK_W = GLA_HEADS * GLA_DK
V_W = GLA_HEADS * GLA_DV
POOL_W = POOL_GROUPS * POOL_GROUP_DIM
MAIN_W = 2 * QK_W + 2 * V_W
ALR_W = 2 * GLA_LOWRANK
MG_W = 2 * D_MODEL

N_CTX = BATCH * SEQ
N_LAT = DEC_BATCH * DEC_SEQ
N_TOK = N_CTX + N_LAT
N_SLOT_BLOCKS = -(-(N_TOK * TOP_K + N_EXPERTS * (MOE_BLOCK - 1)) // MOE_BLOCK)
N_SLOTS = N_SLOT_BLOCKS * MOE_BLOCK

LANES = 128
TOK_TILE = 256
N_TILES = N_TOK // TOK_TILE
CTX_TILES = N_CTX // TOK_TILE
LAT_TILES_PER_SEQ = DEC_SEQ // TOK_TILE
ROUTE_TILE = 512
VMEM_LIMIT = 56 * 1024 * 1024

GLA_SEQS = 4
CTX_CHUNKS = SEQ // GLA_CHUNK
LAT_CHUNKS = DEC_SEQ // GLA_CHUNK
CHUNKS_PER_TILE = TOK_TILE // GLA_CHUNK
GLA_CTX_STEPS = (BATCH // GLA_SEQS) * CTX_CHUNKS
TILE_GRID = 8

NT_DIMS = (((1,), (1,)), ((), ()))
TN_DIMS = (((0,), (0,)), ((), ()))

assert DEC_BATCH == GLA_SEQS and SEQ == TOK_TILE and N_TILES == TILE_GRID * TILE_GRID


def _params(semantics, vmem=VMEM_LIMIT):
    return pltpu.CompilerParams(dimension_semantics=semantics, vmem_limit_bytes=vmem)


def _split_bf16(a):
    hi = a.astype(BF16)
    lo = (a - hi.astype(F32)).astype(BF16)
    return hi, lo


def _dot(a, b):
    return jnp.dot(a, b, preferred_element_type=F32)


def _dot3(a, b):
    a_hi, a_lo = _split_bf16(a)
    b_hi, b_lo = _split_bf16(b)
    return _dot(a_hi, b_hi) + _dot(a_lo, b_hi) + _dot(a_hi, b_lo)


def _sigmoid(x):
    return 1.0 / (1.0 + jnp.exp(-x))


HALF_W = D_MODEL // 2
HIGH_HALF_MASK = -65536


def _pack_halves(lo, hi):
    lo_bits = pltpu.bitcast(lo.astype(BF16).astype(F32), I32)
    hi_bits = pltpu.bitcast(hi.astype(BF16).astype(F32), I32)
    return lax.shift_right_logical(lo_bits, 16) | (hi_bits & HIGH_HALF_MASK)


def _unpack_halves(words):
    lo = pltpu.bitcast(lax.shift_left(words, 16), F32)
    hi = pltpu.bitcast(words & HIGH_HALF_MASK, F32)
    return lo, hi


def _rms(x):
    return x * lax.rsqrt(jnp.mean(x * x, axis=-1, keepdims=True) + NORM_EPS)


def _mod_row(t):
    return jnp.where(t < CTX_TILES, 0, 1 + (t - CTX_TILES) // LAT_TILES_PER_SEQ)


def _store_tile(t):
    u = t - CTX_TILES
    return jnp.where(t < CTX_TILES, t,
                     CTX_TILES + DEC_BATCH * (u % LAT_TILES_PER_SEQ) + u // LAT_TILES_PER_SEQ)


def _ctx_tile(t):
    return jnp.minimum(t, CTX_TILES - 1)


def _lat_tile(t):
    return jnp.maximum(t - CTX_TILES, 0)


def _mod_kernel(c_ref, w_ref, b_ref, o_ref):
    c = c_ref[...]
    o_ref[...] = _dot3(c * _sigmoid(c), w_ref[...]) + b_ref[...]


def _mod_call(cvec, w_mod, b_mod):
    rows = cvec.shape[0]
    return pl.pallas_call(
        _mod_kernel,
        out_shape=jax.ShapeDtypeStruct((rows, N_MOD * D_MODEL), F32),
        grid=(N_MOD,),
        in_specs=[
            pl.BlockSpec((rows, D_MODEL), lambda j: (0, 0)),
            pl.BlockSpec((D_MODEL, D_MODEL), lambda j: (0, j)),
            pl.BlockSpec((1, D_MODEL), lambda j: (0, j)),
        ],
        out_specs=pl.BlockSpec((rows, D_MODEL), lambda j: (0, j)),
        compiler_params=_params(("arbitrary",)),
        name="mod",
    )(cvec, w_mod, b_mod)


def _inproj_kernel(xc_ref, xl_ref, mod_ref, g_ref, wmain_ref, walr_ref, wxp_ref, wmg_ref,
                   q_ref, k_ref, v_ref, og_ref, alr_ref, xp_ref, mg_ref):
    t = pl.program_id(0)
    x = jnp.where(t < CTX_TILES, xc_ref[...], xl_ref[...])
    mod = mod_ref[0]
    shift1 = mod[:, 0:D_MODEL]
    scale1 = mod[:, D_MODEL:2 * D_MODEL]
    h = _rms(x) * g_ref[...]
    h = (h * (1.0 + scale1) + shift1).astype(BF16)
    z = _dot(h, wmain_ref[...])
    q_ref[...] = (z[:, 0:QK_W] * (GLA_DK ** -0.5)).astype(BF16)
    k_ref[...] = z[:, QK_W:2 * QK_W].astype(BF16)
    v_ref[...] = z[:, 2 * QK_W:2 * QK_W + V_W].astype(BF16)
    og = z[:, 2 * QK_W + V_W:MAIN_W]
    og_ref[...] = (og * _sigmoid(og)).astype(BF16)
    alr_ref[...] = _dot(h, walr_ref[...])
    xp_ref[...] = _dot(h, wxp_ref[...])
    mg_ref[...] = _sigmoid(_dot(h, wmg_ref[...])).astype(BF16)


def _inproj_call(x_ctx, x_lat, mod3, norm1_g, w_main, w_alr, w_xp, w_mg):
    const = lambda t: (0, 0)
    stored = lambda t: (_store_tile(t), 0)
    widths = (QK_W, QK_W, V_W, V_W, ALR_W, POOL_W, MG_W)
    dtypes = (BF16, BF16, BF16, BF16, F32, F32, BF16)
    return pl.pallas_call(
        _inproj_kernel,
        out_shape=[jax.ShapeDtypeStruct((N_TOK, w), dt) for w, dt in zip(widths, dtypes)],
        grid=(N_TILES,),
        in_specs=[
            pl.BlockSpec((TOK_TILE, D_MODEL), lambda t: (_ctx_tile(t), 0)),
            pl.BlockSpec((TOK_TILE, D_MODEL), lambda t: (_lat_tile(t), 0)),
            pl.BlockSpec((1, 1, N_MOD * D_MODEL), lambda t: (_mod_row(t), 0, 0)),
            pl.BlockSpec((1, D_MODEL), const),
            pl.BlockSpec((D_MODEL, MAIN_W), const),
            pl.BlockSpec((D_MODEL, ALR_W), const),
            pl.BlockSpec((D_MODEL, POOL_W), const),
            pl.BlockSpec((D_MODEL, MG_W), const),
        ],
        out_specs=[pl.BlockSpec((TOK_TILE, w), stored) for w in widths],
        compiler_params=_params(("arbitrary",)),
        name="inproj",
    )(x_ctx, x_lat, mod3, norm1_g, w_main, w_alr, w_xp, w_mg)


def _gla_direction(q_ref, k_ref, v_ref, alr_ref, wa_ref, ba_ref, o_ref, st_ref, d, rev):
    rows = GLA_SEQS * GLA_CHUNK
    stack = lambda ref, cols: jnp.concatenate([ref[0, s, :, cols] for s in range(GLA_SEQS)], axis=0)
    alr = stack(alr_ref, slice(None))
    a = _dot3(alr, wa_ref[...]) + ba_ref[...]
    g = (jnp.minimum(a, 0.0) - jnp.log(1.0 + jnp.exp(-jnp.abs(a)))) * (1.0 / GLA_TAU)

    row = lax.broadcasted_iota(I32, (rows, rows), 0)
    col = lax.broadcasted_iota(I32, (rows, rows), 1)
    same = (row // GLA_CHUNK) == (col // GLA_CHUNK)
    tri = same & ((col >= row) if rev else (col <= row))
    tri_b = jnp.where(tri, 1.0, 0.0).astype(BF16)
    g_hi, g_lo = _split_bf16(g)
    bcum = _dot(tri_b, g_hi) + _dot(tri_b, g_lo)

    def per_seq_row(r):
        return jnp.concatenate(
            [jnp.broadcast_to(bcum[s * GLA_CHUNK + r:s * GLA_CHUNK + r + 1], (GLA_CHUNK, QK_W))
             for s in range(GLA_SEQS)], axis=0)

    r_last = 0 if rev else GLA_CHUNK - 1
    blast = per_seq_row(r_last)
    bmid = per_seq_row(GLA_CHUNK // 2)
    e_q = jnp.exp(bcum - bmid)
    e_k = jnp.exp(bmid - bcum)
    e_in = jnp.exp(bcum)
    e_out = jnp.exp(blast - bcum)
    q = stack(q_ref, slice(None)).astype(F32)
    k = stack(k_ref, slice(None)).astype(F32)

    wide = (rows, GLA_SEQS * GLA_DK)
    own = (lax.broadcasted_iota(I32, wide, 0) // GLA_CHUNK) == (lax.broadcasted_iota(I32, wide, 1) // GLA_DK)

    def block_diag(x):
        return jnp.where(own, jnp.concatenate([x] * GLA_SEQS, axis=1), 0.0).astype(BF16)

    for h in range(GLA_HEADS):
        ks = slice(h * GLA_DK, (h + 1) * GLA_DK)
        vs = slice(h * GLA_DV, (h + 1) * GLA_DV)
        qh = q[:, ks]
        kh = k[:, ks]
        vh = stack(v_ref, vs)
        att = lax.dot_general((qh * e_q[:, ks]).astype(BF16), (kh * e_k[:, ks]).astype(BF16),
                              NT_DIMS, preferred_element_type=F32)
        att = jnp.where(tri, att, 0.0).astype(BF16)
        st = st_ref[d, h]
        o_inter = lax.dot_general(block_diag(qh * e_in[:, ks]), st.astype(BF16), NT_DIMS,
                                  preferred_element_type=F32)
        o_h = o_inter + _dot(att, vh)
        for s in range(GLA_SEQS):
            o_ref[0, s, :, vs] = o_h[s * GLA_CHUNK:(s + 1) * GLA_CHUNK]
        upd = lax.dot_general(vh, block_diag(kh * e_out[:, ks]), TN_DIMS,
                              preferred_element_type=F32)
        e_last = jnp.concatenate(
            [jnp.exp(bcum[s * GLA_CHUNK + r_last:s * GLA_CHUNK + r_last + 1, ks])
             for s in range(GLA_SEQS)], axis=1)
        st_ref[d, h] = st * e_last + upd


def _gla_kernel(qf_ref, kf_ref, vf_ref, af_ref, qb_ref, kb_ref, vb_ref, ab_ref,
                waf_ref, baf_ref, wab_ref, bab_ref, s0f_ref, s0b_ref,
                of_ref, ob_ref, sf_ref, sb_ref, st_ref):
    i = pl.program_id(0)
    is_ctx = i < GLA_CTX_STEPS
    chunk = jnp.where(is_ctx, i % CTX_CHUNKS, i - GLA_CTX_STEPS)

    @pl.when(is_ctx & (chunk == 0))
    def _():
        st_ref[...] = jnp.zeros(st_ref.shape, F32)

    @pl.when(i == GLA_CTX_STEPS)
    def _():
        for s in range(GLA_SEQS):
            ls = slice(s * GLA_DK, (s + 1) * GLA_DK)
            for h in range(GLA_HEADS):
                st_ref[0, h, :, ls] = s0f_ref[s, h].T
                st_ref[1, h, :, ls] = s0b_ref[s, h].T

    _gla_direction(qf_ref, kf_ref, vf_ref, af_ref, waf_ref, baf_ref, of_ref, st_ref, 0, False)
    _gla_direction(qb_ref, kb_ref, vb_ref, ab_ref, wab_ref, bab_ref, ob_ref, st_ref, 1, True)

    @pl.when(is_ctx & (chunk == CTX_CHUNKS - 1))
    def _():
        for s in range(GLA_SEQS):
            ls = slice(s * GLA_DK, (s + 1) * GLA_DK)
            for h in range(GLA_HEADS):
                sf_ref[s, h] = st_ref[0, h, :, ls].T
                sb_ref[s, h] = st_ref[1, h, :, ls].T


def _gla_block(i, rev):
    is_ctx = i < GLA_CTX_STEPS
    group = i // CTX_CHUNKS
    c_ctx = i % CTX_CHUNKS
    c_lat = i - GLA_CTX_STEPS
    if rev:
        c_ctx = CTX_CHUNKS - 1 - c_ctx
        c_lat = LAT_CHUNKS - 1 - c_lat
    j = c_lat // CHUNKS_PER_TILE
    per_row = TILE_GRID // GLA_SEQS
    a = jnp.where(is_ctx, group // per_row, CTX_TILES // TILE_GRID + j // per_row)
    b = jnp.where(is_ctx, group % per_row, j % per_row)
    c = jnp.where(is_ctx, c_ctx, c_lat % CHUNKS_PER_TILE)
    return (a, b, c, 0)


def _gla_call(q, k, v, alr, wa_f, ba_f, wa_b, ba_b, s0_f, s0_b):
    def view(arr):
        return arr.reshape(TILE_GRID, TILE_GRID, TOK_TILE, arr.shape[-1])

    def spec(width, rev):
        return pl.BlockSpec((1, GLA_SEQS, GLA_CHUNK, width), lambda i: _gla_block(i, rev))

    const = lambda i: (0, 0)
    st_block = (GLA_SEQS, GLA_HEADS, GLA_DK, GLA_DV)
    whole_state = pl.BlockSpec(st_block, lambda i: (0, 0, 0, 0))
    ctx_state = pl.BlockSpec(
        st_block, lambda i: (jnp.minimum(i // CTX_CHUNKS, BATCH // GLA_SEQS - 1), 0, 0, 0))
    in_specs = []
    for rev in (False, True):
        in_specs += [spec(QK_W, rev), spec(QK_W, rev), spec(V_W, rev), spec(ALR_W, rev)]
    in_specs += [pl.BlockSpec((ALR_W, QK_W), const), pl.BlockSpec((1, QK_W), const)] * 2
    in_specs += [whole_state, whole_state]
    o_shape = jax.ShapeDtypeStruct((TILE_GRID, TILE_GRID, TOK_TILE, V_W), F32)
    s_shape = jax.ShapeDtypeStruct((BATCH, GLA_HEADS, GLA_DK, GLA_DV), F32)
    qv, kv, vv, av = view(q), view(k), view(v), view(alr)
    o_f, o_b, s_f, s_b = pl.pallas_call(
        _gla_kernel,
        out_shape=[o_shape, o_shape, s_shape, s_shape],
        grid=(GLA_CTX_STEPS + LAT_CHUNKS,),
        in_specs=in_specs,
        out_specs=[spec(V_W, False), spec(V_W, True), ctx_state, ctx_state],
        scratch_shapes=[pltpu.VMEM((2, GLA_HEADS, GLA_DV, GLA_SEQS * GLA_DK), F32)],
        compiler_params=_params(("arbitrary",)),
        name="gla",
    )(qv, kv, vv, av, qv, kv, vv, av, wa_f, ba_f, wa_b, ba_b, s0_f, s0_b)
    return o_f.reshape(N_TOK, V_W), o_b.reshape(N_TOK, V_W), s_f, s_b


def _band(n, w, block):
    row = lax.broadcasted_iota(I32, (n, n), 0)
    col = lax.broadcasted_iota(I32, (n, n), 1)
    inside = (col >= row - w // 2) & (col <= row + w // 2 - 1)
    if block < n:
        inside = inside & ((row // block) == (col // block))
    return jnp.where(inside, 1.0, 0.0).astype(BF16)


def _win_count(p, n, w):
    return jnp.minimum(p + w // 2 - 1, n - 1) - jnp.maximum(p - w // 2, 0) + 1


def _pool_ctx_kernel(x_ref, o_ref, band_ref):
    @pl.when(pl.program_id(0) == 0)
    def _():
        for gi, w in enumerate(POOL_WINDOWS):
            band_ref[gi] = _band(SEQ, w, SEQ)

    p = lax.broadcasted_iota(I32, (SEQ, POOL_GROUP_DIM), 0)
    for gi, w in enumerate(POOL_WINDOWS):
        cs = slice(gi * POOL_GROUP_DIM, (gi + 1) * POOL_GROUP_DIM)
        x = x_ref[:, cs]
        hi, lo = _split_bf16(x)
        band = band_ref[gi]
        s = _dot(band, hi) + _dot(band, lo)
        cnt = _win_count(p, SEQ, w).astype(F32)
        o_ref[:, cs] = s / cnt - x


def _pool_ctx_call(xp):
    spec = pl.BlockSpec((SEQ, POOL_W), lambda b: (b, 0))
    return pl.pallas_call(
        _pool_ctx_kernel,
        out_shape=jax.ShapeDtypeStruct((N_CTX, POOL_W), F32),
        grid=(BATCH,),
        in_specs=[spec],
        out_specs=spec,
        scratch_shapes=[pltpu.VMEM((POOL_GROUPS, SEQ, SEQ), BF16)],
        compiler_params=_params(("arbitrary",)),
        name="pool_ctx",
    )(xp)


POOL_HALO = (max(POOL_WINDOWS) // 2) * GRID_W


def _pool_lat_kernel(x_ref, o_ref, pad_ref):
    rows = DEC_SEQ // GRID_W
    p = lax.broadcasted_iota(I32, (DEC_SEQ, POOL_GROUP_DIM), 0)
    r = p // GRID_W
    cidx = p % GRID_W
    zeros = jnp.zeros((POOL_HALO, POOL_GROUP_DIM), F32)
    pad_ref[0:POOL_HALO, :] = zeros
    pad_ref[POOL_HALO + DEC_SEQ:2 * POOL_HALO + DEC_SEQ, :] = zeros
    for gi, w in enumerate(POOL_WINDOWS):
        cs = slice(gi * POOL_GROUP_DIM, (gi + 1) * POOL_GROUP_DIM)
        band = _band(TOK_TILE, w, GRID_W)
        for t in range(LAT_TILES_PER_SEQ):
            hi, lo = _split_bf16(x_ref[t, 0, :, cs])
            pad_ref[POOL_HALO + t * TOK_TILE:POOL_HALO + (t + 1) * TOK_TILE, :] = (
                _dot(band, hi) + _dot(band, lo))
        acc = jnp.zeros((DEC_SEQ, POOL_GROUP_DIM), F32)
        for dr in range(-(w // 2), w // 2):
            start = POOL_HALO + dr * GRID_W
            acc = acc + pad_ref[start:start + DEC_SEQ, :]
        cnt = (_win_count(r, rows, w) * _win_count(cidx, GRID_W, w)).astype(F32)
        pooled = acc / cnt
        for t in range(LAT_TILES_PER_SEQ):
            rs = slice(t * TOK_TILE, (t + 1) * TOK_TILE)
            o_ref[t, 0, :, cs] = pooled[rs] - x_ref[t, 0, :, cs]


def _pool_lat_call(xp):
    view = xp.reshape(N_TILES // DEC_BATCH, DEC_BATCH, TOK_TILE, POOL_W)
    blk = (LAT_TILES_PER_SEQ, 1, TOK_TILE, POOL_W)
    out = pl.pallas_call(
        _pool_lat_kernel,
        out_shape=jax.ShapeDtypeStruct((LAT_TILES_PER_SEQ, DEC_BATCH, TOK_TILE, POOL_W), F32),
        grid=(DEC_BATCH,),
        in_specs=[pl.BlockSpec(blk, lambda s: (CTX_TILES // DEC_BATCH // LAT_TILES_PER_SEQ, s, 0, 0))],
        out_specs=pl.BlockSpec(blk, lambda s: (0, s, 0, 0)),
        scratch_shapes=[pltpu.VMEM((DEC_SEQ + 2 * POOL_HALO, POOL_GROUP_DIM), F32)],
        compiler_params=_params(("arbitrary",)),
        name="pool_lat",
    )(view)
    return out.reshape(N_LAT, POOL_W)


def _post_kernel(xc_ref, xl_ref, mod_ref, of_ref, ob_ref, og_ref, pc_ref, pl_ref, mg_ref, gng_ref,
                 wpg_ref, psc_ref, wbg_ref, wbp_ref, wout_ref, n2g_ref, wr_ref, br_ref,
                 x1_ref, h2_ref, idx_ref, tw_ref):
    t = pl.program_id(0)
    is_ctx = t < CTX_TILES
    x = jnp.where(is_ctx, xc_ref[...], xl_ref[...])
    pooled = jnp.where(is_ctx, pc_ref[...], pl_ref[...])
    mod = mod_ref[0]
    gate1 = mod[:, 2 * D_MODEL:3 * D_MODEL]
    shift2 = mod[:, 3 * D_MODEL:4 * D_MODEL]
    scale2 = mod[:, 4 * D_MODEL:5 * D_MODEL]

    o = of_ref[...] + ob_ref[...]
    og = og_ref[...].astype(F32)
    gated = []
    for h in range(GLA_HEADS):
        vs = slice(h * GLA_DV, (h + 1) * GLA_DV)
        oh = _rms(o[:, vs]) * gng_ref[:, vs]
        gated.append((oh * og[:, vs]).astype(BF16))
    br_gla = _dot(jnp.concatenate(gated, axis=-1), wbg_ref[...])

    pm = []
    for gi in range(POOL_GROUPS):
        cs = slice(gi * POOL_GROUP_DIM, (gi + 1) * POOL_GROUP_DIM)
        pmg = _dot(pooled[:, cs].astype(BF16), wpg_ref[gi]) * psc_ref[:, cs]
        pm.append(pmg.astype(BF16))
    br_pool = _dot(jnp.concatenate(pm, axis=-1), wbp_ref[...])

    mg = mg_ref[...].astype(F32)
    merged = mg[:, 0:D_MODEL] * br_gla + mg[:, D_MODEL:MG_W] * br_pool
    m = _dot(merged.astype(BF16), wout_ref[...])
    x1 = x + gate1 * m
    x1_ref[...] = x1
    h2 = _rms(x1) * n2g_ref[...]
    h2 = h2 * (1.0 + scale2) + shift2
    h2_ref[...] = _pack_halves(h2[:, :HALF_W], h2[:, HALF_W:])

    logits = _dot3(h2, wr_ref[...]) + br_ref[...]
    lane = lax.broadcasted_iota(I32, (TOK_TILE, LANES), 1)
    lane_f = lane.astype(F32)
    neg = jnp.float32(-jnp.inf)
    cur = jnp.where(lane < N_EXPERTS, logits, neg)
    vals, idxs = [], []
    for _ in range(TOP_K):
        mx = jnp.max(cur, axis=-1, keepdims=True)
        ix = jnp.min(jnp.where(cur == mx, lane_f, float(LANES)), axis=-1, keepdims=True)
        vals.append(mx)
        idxs.append(ix)
        cur = jnp.where(lane_f == ix, neg, cur)
    ex = [jnp.exp(vv - vals[0]) for vv in vals]
    tot = ex[0] + ex[1] + ex[2] + ex[3]
    idx_out = jnp.zeros((TOK_TILE, LANES), F32)
    w_out = jnp.zeros((TOK_TILE, LANES), F32)
    for kk in range(TOP_K):
        idx_out = jnp.where(lane == kk, idxs[kk], idx_out)
        w_out = jnp.where(lane == kk, ex[kk] / tot, w_out)
    idx_ref[...] = idx_out.astype(I32)
    tw_ref[...] = w_out


def _post_call(x_ctx, x_lat, mod3, o_f, o_b, og, pooled_c, pooled_l, mg, gng, wpg, psc, wbg, wbp,
               wout, n2g, wr, br):
    row = lambda t: (t, 0)
    const = lambda t: (0, 0)
    stored = lambda t: (_store_tile(t), 0)
    return pl.pallas_call(
        _post_kernel,
        out_shape=[
            jax.ShapeDtypeStruct((N_TOK, D_MODEL), F32),
            jax.ShapeDtypeStruct((N_TOK, HALF_W), I32),
            jax.ShapeDtypeStruct((N_TOK, LANES), I32),
            jax.ShapeDtypeStruct((N_TOK, LANES), F32),
        ],
        grid=(N_TILES,),
        in_specs=[
            pl.BlockSpec((TOK_TILE, D_MODEL), lambda t: (_ctx_tile(t), 0)),
            pl.BlockSpec((TOK_TILE, D_MODEL), lambda t: (_lat_tile(t), 0)),
            pl.BlockSpec((1, 1, N_MOD * D_MODEL), lambda t: (_mod_row(t), 0, 0)),
            pl.BlockSpec((TOK_TILE, V_W), stored),
            pl.BlockSpec((TOK_TILE, V_W), stored),
            pl.BlockSpec((TOK_TILE, V_W), stored),
            pl.BlockSpec((TOK_TILE, POOL_W), lambda t: (_ctx_tile(t), 0)),
            pl.BlockSpec((TOK_TILE, POOL_W),
                         lambda t: (jnp.maximum(_store_tile(t) - CTX_TILES, 0), 0)),
            pl.BlockSpec((TOK_TILE, MG_W), stored),
            pl.BlockSpec((1, V_W), const),
            pl.BlockSpec((POOL_GROUPS, POOL_GROUP_DIM, POOL_GROUP_DIM), lambda t: (0, 0, 0)),
            pl.BlockSpec((1, POOL_W), const),
            pl.BlockSpec((V_W, D_MODEL), const),
            pl.BlockSpec((POOL_W, D_MODEL), const),
            pl.BlockSpec((D_MODEL, D_MODEL), const),
            pl.BlockSpec((1, D_MODEL), const),
            pl.BlockSpec((D_MODEL, LANES), const),
            pl.BlockSpec((1, LANES), const),
        ],
        out_specs=[
            pl.BlockSpec((TOK_TILE, D_MODEL), row),
            pl.BlockSpec((TOK_TILE, HALF_W), row),
            pl.BlockSpec((TOK_TILE, LANES), row),
            pl.BlockSpec((TOK_TILE, LANES), row),
        ],
        compiler_params=_params(("arbitrary",)),
        name="post",
    )(x_ctx, x_lat, mod3, o_f, o_b, og, pooled_c, pooled_l, mg, gng, wpg, psc, wbg, wbp, wout, n2g,
      wr, br)


def _route_kernel(idx_ref, rank_ref, cnt_ref, carry_ref, strict_ref):
    t = pl.program_id(0)

    @pl.when(t == 0)
    def _():
        carry_ref[...] = jnp.zeros((1, LANES), F32)
        row = lax.broadcasted_iota(I32, (ROUTE_TILE, ROUTE_TILE), 0)
        col = lax.broadcasted_iota(I32, (ROUTE_TILE, ROUTE_TILE), 1)
        strict_ref[...] = jnp.where(col < row, 1.0, 0.0).astype(BF16)

    idx = idx_ref[...]
    lane = lax.broadcasted_iota(I32, (ROUTE_TILE, LANES), 1)
    sel = [lane == idx[:, kk:kk + 1] for kk in range(TOP_K)]
    onehot = jnp.zeros((ROUTE_TILE, LANES), F32)
    for kk in range(TOP_K):
        onehot = onehot + jnp.where(sel[kk], 1.0, 0.0)
    before = _dot(strict_ref[...], onehot.astype(BF16)) + carry_ref[...]
    rank = jnp.zeros((ROUTE_TILE, LANES), F32)
    for kk in range(TOP_K):
        rk = jnp.sum(jnp.where(sel[kk], before, 0.0), axis=-1, keepdims=True)
        rank = jnp.where(lane == kk, rk, rank)
    rank_ref[...] = rank.astype(I32)
    carry_ref[...] = carry_ref[...] + jnp.sum(onehot, axis=0, keepdims=True)
    cnt_ref[...] = jnp.broadcast_to(carry_ref[...], (8, LANES))


def _route_call(idx):
    return pl.pallas_call(
        _route_kernel,
        out_shape=[
            jax.ShapeDtypeStruct((N_TOK, LANES), I32),
            jax.ShapeDtypeStruct((8, LANES), F32),
        ],
        grid=(N_TOK // ROUTE_TILE,),
        in_specs=[pl.BlockSpec((ROUTE_TILE, LANES), lambda t: (t, 0))],
        out_specs=[
            pl.BlockSpec((ROUTE_TILE, LANES), lambda t: (t, 0)),
            pl.BlockSpec((8, LANES), lambda t: (0, 0)),
        ],
        scratch_shapes=[pltpu.VMEM((1, LANES), F32), pltpu.VMEM((ROUTE_TILE, ROUTE_TILE), BF16)],
        compiler_params=_params(("arbitrary",)),
        name="route",
    )(idx)


def _moe_kernel(be_ref, nu_ref, ne_ref, par_ref, x_ref, wg_hbm, bg_ref, wu_hbm, bu_ref, wd_hbm,
                bd_ref, y_ref, wf_ref, wgu_ref, wdb_ref, sems):
    b = pl.program_id(0)
    n_used = nu_ref[0]
    e = be_ref[b]
    prev = be_ref[jnp.maximum(b - 1, 0)]
    live = b < n_used

    def weight_copies(expert, slot):
        return [pltpu.make_async_copy(w.at[expert], wf_ref.at[slot, i], sems.at[slot])
                for i, w in enumerate((wg_hbm, wu_hbm, wd_hbm))]

    @pl.when(live & ((b == 0) | (e != prev)))
    def _():
        slot = par_ref[b]

        @pl.when(b == 0)
        def _():
            for cp in weight_copies(e, slot):
                cp.start()

        for cp in weight_copies(e, slot):
            cp.wait()
        nxt = ne_ref[b]

        @pl.when(nxt >= 0)
        def _():
            for cp in weight_copies(nxt, 1 - slot):
                cp.start()

        wgu_ref[:, :D_FF] = wf_ref[slot, 0].astype(BF16)
        wgu_ref[:, D_FF:] = wf_ref[slot, 1].astype(BF16)
        wdb_ref[...] = wf_ref[slot, 2].astype(BF16)

    @pl.when(live)
    def _():
        x_lo, x_hi = _unpack_halves(x_ref[...])
        x = jnp.concatenate([x_lo.astype(BF16), x_hi.astype(BF16)], axis=-1)
        gu = _dot(x, wgu_ref[...])
        gate = jnp.minimum(gu[:, :D_FF] + bg_ref[0], SWIGLU_LIMIT)
        up = jnp.clip(gu[:, D_FF:] + bu_ref[0], -SWIGLU_LIMIT, SWIGLU_LIMIT)
        act = (up + 1.0) * (gate * _sigmoid(SWIGLU_ALPHA * gate))
        y = _dot(act.astype(BF16), wdb_ref[...]) + bd_ref[0]
        y_ref[...] = _pack_halves(y[:, :HALF_W], y[:, HALF_W:])

    @pl.when(jnp.logical_not(live))
    def _():
        y_ref[...] = jnp.zeros((MOE_BLOCK, HALF_W), I32)


def _moe_call(block_e, n_used, next_e, parity, hs, w_gate, b_gate, w_up, b_up, w_down, b_down):
    def blk(b, be, nu, ne, par):
        return jnp.minimum(b, nu[0] - 1)

    row = lambda b, be, nu, ne, par: (blk(b, be, nu, ne, par), 0)
    bsel = lambda b, be, nu, ne, par: (be[blk(b, be, nu, ne, par)], 0, 0)
    any_spec = pl.BlockSpec(memory_space=pl.ANY)
    assert D_MODEL == D_FF
    return pl.pallas_call(
        _moe_kernel,
        out_shape=jax.ShapeDtypeStruct((N_SLOTS, HALF_W), I32),
        grid_spec=pltpu.PrefetchScalarGridSpec(
            num_scalar_prefetch=4,
            grid=(N_SLOT_BLOCKS,),
            in_specs=[
                pl.BlockSpec((MOE_BLOCK, HALF_W), row),
                any_spec,
                pl.BlockSpec((1, 1, D_FF), bsel),
                any_spec,
                pl.BlockSpec((1, 1, D_FF), bsel),
                any_spec,
                pl.BlockSpec((1, 1, D_MODEL), bsel),
            ],
            out_specs=pl.BlockSpec((MOE_BLOCK, HALF_W), lambda b, be, nu, ne, par: (b, 0)),
            scratch_shapes=[
                pltpu.VMEM((2, 3, D_MODEL, D_FF), F32),
                pltpu.VMEM((D_MODEL, 2 * D_FF), BF16),
                pltpu.VMEM((D_FF, D_MODEL), BF16),
                pltpu.SemaphoreType.DMA((2,)),
            ],
        ),
        compiler_params=_params(("arbitrary",)),
        name="moe",
    )(block_e, n_used, next_e, parity, hs, w_gate, b_gate, w_up, b_up, w_down, b_down)


SC_CORES = 2
SC_SUBCORES = 16
SC_WORKERS = SC_CORES * SC_SUBCORES
SC_ROWS = 128
COMBINE_CHUNKS = 2
COMBINE_TILE = 2 * TOK_TILE
assert N_CTX == N_LAT


def _sc_gather_rows(table, idx):
    n_idx = idx.shape[0]
    width = table.shape[1]
    per_worker = n_idx // SC_WORKERS
    n_chunks = per_worker // SC_ROWS
    assert n_chunks * SC_ROWS * SC_WORKERS == n_idx
    mesh = plsc.VectorSubcoreMesh(core_axis_name="c", subcore_axis_name="s")

    @functools.partial(
        pl.kernel, mesh=mesh,
        out_type=jax.ShapeDtypeStruct((n_idx, width), table.dtype),
        scratch_types=[pltpu.VMEM((SC_ROWS,), I32), pltpu.VMEM((SC_ROWS, width), table.dtype),
                       pltpu.SemaphoreType.DMA],
        name="sc_gather",
    )
    def gather(table_hbm, idx_hbm, out_hbm, idx_v, rows_v, sem):
        worker = lax.axis_index("s") * SC_CORES + lax.axis_index("c")
        base = worker * per_worker

        @pl.loop(0, n_chunks)
        def _(ch):
            off = pl.multiple_of(base + ch * SC_ROWS, SC_ROWS)
            pltpu.sync_copy(idx_hbm.at[pl.ds(off, SC_ROWS)], idx_v)
            pltpu.async_copy(table_hbm.at[idx_v], rows_v, sem).wait()
            pltpu.sync_copy(rows_v, out_hbm.at[pl.ds(off, SC_ROWS)])

    return gather(table, idx)


def _sc_scatter_rows(rows, idx3, n_out):
    n_rows, width = rows.shape
    n_chunks = n_rows // SC_ROWS // SC_WORKERS
    assert n_chunks * SC_ROWS * SC_WORKERS == n_rows and idx3.shape == (n_rows // SC_ROWS, TOP_K, SC_ROWS)
    mesh = plsc.VectorSubcoreMesh(core_axis_name="c", subcore_axis_name="s")

    @functools.partial(
        pl.kernel, mesh=mesh,
        out_type=jax.ShapeDtypeStruct((n_out, width), rows.dtype),
        scratch_types=[pltpu.VMEM((TOP_K, SC_ROWS), I32), pltpu.VMEM((SC_ROWS, width), rows.dtype),
                       pltpu.SemaphoreType.DMA],
        name="sc_scatter",
    )
    def scatter(rows_hbm, idx_hbm, out_hbm, idx_v, rows_v, sem):
        worker = lax.axis_index("s") * SC_CORES + lax.axis_index("c")

        @pl.loop(0, n_chunks)
        def _(ch):
            chunk = worker * n_chunks + ch
            pltpu.sync_copy(idx_hbm.at[chunk], idx_v)
            pltpu.sync_copy(rows_hbm.at[pl.ds(pl.multiple_of(chunk * SC_ROWS, SC_ROWS), SC_ROWS)],
                            rows_v)
            for kk in range(TOP_K):
                pltpu.async_copy(rows_v, out_hbm.at[idx_v.at[kk]], sem).wait()

    return scatter(rows, idx3)


def _combine_kernel(x1_ref, mod_ref, tw_ref, fg_ref, g_ref, *rest):
    out_ref = rest[-1]
    tw = tw_ref[...]
    f_lo = jnp.zeros((COMBINE_TILE, HALF_W), F32)
    f_hi = jnp.zeros((COMBINE_TILE, HALF_W), F32)
    for kk in range(TOP_K):
        lo, hi = _unpack_halves(g_ref[kk])
        f_lo = f_lo + lo * tw[:, kk:kk + 1]
        f_hi = f_hi + hi * tw[:, kk:kk + 1]
    gate2 = mod_ref[0][:, 5 * D_MODEL:6 * D_MODEL]
    x2 = x1_ref[...] + gate2 * jnp.concatenate([f_lo, f_hi], axis=-1)
    out_ref[...] = _rms(x2) * fg_ref[...]


def _combine_call(x1, mod3, tw, final_g, gathered, partial, tile0, out_tile0, out_tiles, name):
    n_tiles = gathered.shape[1] // COMBINE_TILE
    in_specs = [
        pl.BlockSpec((COMBINE_TILE, D_MODEL), lambda t: (tile0 + t, 0)),
        pl.BlockSpec((1, 1, N_MOD * D_MODEL), lambda t: (_mod_row((tile0 + t) * (COMBINE_TILE // TOK_TILE)), 0, 0)),
        pl.BlockSpec((COMBINE_TILE, LANES), lambda t: (tile0 + t, 0)),
        pl.BlockSpec((1, D_MODEL), lambda t: (0, 0)),
        pl.BlockSpec((TOP_K, COMBINE_TILE, HALF_W), lambda t: (0, t, 0)),
    ]
    args = [x1, mod3, tw, final_g, gathered]
    aliases = {}
    if partial is not None:
        in_specs.append(pl.BlockSpec(memory_space=pl.ANY))
        args.append(partial)
        aliases = {len(args) - 1: 0}
    return pl.pallas_call(
        _combine_kernel,
        out_shape=jax.ShapeDtypeStruct((out_tiles * COMBINE_TILE, D_MODEL), F32),
        grid=(n_tiles,),
        in_specs=in_specs,
        out_specs=pl.BlockSpec((COMBINE_TILE, D_MODEL), lambda t: (out_tile0 + t, 0)),
        input_output_aliases=aliases,
        compiler_params=_params(("arbitrary",)),
        name=name,
    )(*args)


def kernel(x_prompt, x_sample, state_gla_fwd, state_gla_bwd, c, c_ctx, norm1_g, w_mod, b_mod, w_in,
           w_alpha, b_alpha, gla_norm_g, w_pool_grp, pool_scale, w_branch_gla, w_branch_pool, w_out,
           norm2_g, w_router, b_router, w_gate, b_gate, w_up, b_up, w_down, b_down, final_norm_g):
    l = 0
    x_ctx = x_prompt.reshape(N_CTX, D_MODEL)
    x_lat = x_sample.reshape(N_LAT, D_MODEL)

    cvec = jnp.concatenate([c_ctx[None, :], c, jnp.zeros((8 - 1 - DEC_BATCH, D_MODEL), F32)], axis=0)
    mod = _mod_call(cvec, w_mod[l], b_mod[l][None, :])
    mod3 = mod.reshape(8, 1, N_MOD * D_MODEL)

    w_in_b = w_in[l].astype(BF16)
    w_main = w_in_b[:, :MAIN_W]
    w_alr = w_in_b[:, MAIN_W:MAIN_W + ALR_W]
    w_xp = w_in_b[:, MAIN_W + ALR_W:MAIN_W + ALR_W + POOL_W]
    w_mg = w_in_b[:, MAIN_W + ALR_W + POOL_W:]
    q, k, v, og, alr, xp, mg = _inproj_call(x_ctx, x_lat, mod3, norm1_g[l][None, :],
                                            w_main, w_alr, w_xp, w_mg)

    zpad = jnp.zeros((GLA_LOWRANK, QK_W), F32)
    wa_f = jnp.concatenate([w_alpha[l, 0], zpad], axis=0)
    wa_b = jnp.concatenate([zpad, w_alpha[l, 1]], axis=0)
    o_f, o_b, s_f, s_b = _gla_call(q, k, v, alr, wa_f, b_alpha[l, 0][None, :], wa_b,
                                   b_alpha[l, 1][None, :], state_gla_fwd[:, l], state_gla_bwd[:, l])

    pooled_c = _pool_ctx_call(xp)
    pooled_l = _pool_lat_call(xp)

    w_router_pad = jnp.pad(w_router[l], ((0, 0), (0, LANES - N_EXPERTS)))
    b_router_pad = jnp.pad(b_router[l], (0, LANES - N_EXPERTS))[None, :]
    x1, h2, top_idx, top_w = _post_call(
        x_ctx, x_lat, mod3, o_f, o_b, og, pooled_c, pooled_l, mg,
        gla_norm_g[l].reshape(1, V_W), w_pool_grp[l].astype(BF16), pool_scale[l][None, :],
        w_branch_gla[l].astype(BF16), w_branch_pool[l].astype(BF16), w_out[l].astype(BF16),
        norm2_g[l][None, :], w_router_pad, b_router_pad)

    rank, cnt = _route_call(top_idx)
    counts = cnt[0, :N_EXPERTS].astype(I32)
    padded = (counts + MOE_BLOCK - 1) // MOE_BLOCK * MOE_BLOCK
    pad_end = jnp.cumsum(padded).astype(I32)
    pad_start = pad_end - padded
    block_first = jnp.arange(N_SLOT_BLOCKS, dtype=I32) * MOE_BLOCK
    block_e = jnp.minimum(jnp.sum((pad_end[None, :] <= block_first[:, None]).astype(I32), axis=1),
                          N_EXPERTS - 1).astype(I32)
    n_used = (pad_end[-1:] // MOE_BLOCK).astype(I32)
    run_start = jnp.concatenate([jnp.ones((1,), I32), (block_e[1:] != block_e[:-1]).astype(I32)])
    parity = ((jnp.cumsum(run_start) - 1) % 2).astype(I32)
    after = pad_end[block_e] // MOE_BLOCK
    next_e = jnp.where(after < n_used[0], block_e[jnp.minimum(after, N_SLOT_BLOCKS - 1)], -1).astype(I32)
    experts = jnp.arange(N_EXPERTS, dtype=I32)
    tk = top_idx[:, :TOP_K]
    pos = jnp.sum(jnp.where(tk[:, :, None] == experts, pad_start, 0), axis=-1) + rank[:, :TOP_K]
    pos = pos.astype(I32)
    pos_by_choice = pos.T
    pos_chunks = pos_by_choice.reshape(TOP_K, N_TOK // SC_ROWS, SC_ROWS).transpose(1, 0, 2)

    hs = _sc_scatter_rows(h2, pos_chunks, N_SLOTS)
    y = _moe_call(block_e, n_used, next_e, parity, hs,
                  w_gate[l], b_gate[l][:, None, :], w_up[l], b_up[l][:, None, :],
                  w_down[l], b_down[l][:, None, :])
    outs = []
    chunk_tok = N_CTX // COMBINE_CHUNKS
    for group, tok0 in (("ctx", 0), ("lat", N_CTX)):
        out = None
        for ci in range(COMBINE_CHUNKS):
            t0 = tok0 + ci * chunk_tok
            idx = pos_by_choice[:, t0:t0 + chunk_tok].reshape(TOP_K * chunk_tok)
            gathered = _sc_gather_rows(y, idx).reshape(TOP_K, chunk_tok, HALF_W)
            out = _combine_call(x1, mod3, top_w, final_norm_g[None, :], gathered, out,
                                t0 // COMBINE_TILE, ci * chunk_tok // COMBINE_TILE, N_CTX // COMBINE_TILE,
                                "combine_%s%d" % (group, ci))
        outs.append(out)
    y_prompt = outs[0].reshape(BATCH, SEQ, D_MODEL)
    y_sample = outs[1].reshape(DEC_BATCH, DEC_SEQ, D_MODEL)
    return (y_prompt, y_sample, s_f[:, None], s_b[:, None])
```

```python
import functools

import jax
import jax.numpy as jnp
from jax import lax
from jax.experimental import pallas as pl
from jax.experimental.pallas import tpu as pltpu
from jax.experimental.pallas import tpu_sc as plsc

F32 = jnp.float32
BF16 = jnp.bfloat16
I32 = jnp.int32

D_MODEL = 1024
BATCH = 32
SEQ = 256
DEC_BATCH = 4
DEC_SEQ = 2048
GRID_W = 64
GLA_HEADS = 4
GLA_DK = 128
GLA_DV = 256
GLA_LOWRANK = 16
GLA_TAU = 16.0
GLA_CHUNK = 64
POOL_GROUPS = 4
POOL_GROUP_DIM = 128
POOL_WINDOWS = (2, 4, 8, 16)
N_EXPERTS = 32
TOP_K = 4
D_FF = 1024
SWIGLU_LIMIT = 7.0
SWIGLU_ALPHA = 1.702
MOE_BLOCK = 256
NORM_EPS = 1e-6
N_MOD = 6

QK_W = GLA_HEADS * GLA_DK
V_W = GLA_HEADS * GLA_DV
POOL_W = POOL_GROUPS * POOL_GROUP_DIM
MAIN_W = 2 * QK_W + 2 * V_W
ALR_W = 2 * GLA_LOWRANK
MG_W = 2 * D_MODEL

N_CTX = BATCH * SEQ
N_LAT = DEC_BATCH * DEC_SEQ
N_TOK = N_CTX + N_LAT
N_SLOT_BLOCKS = -(-(N_TOK * TOP_K + N_EXPERTS * (MOE_BLOCK - 1)) // MOE_BLOCK)
N_SLOTS = N_SLOT_BLOCKS * MOE_BLOCK

LANES = 128
TOK_TILE = 256
N_TILES = N_TOK // TOK_TILE
CTX_TILES = N_CTX // TOK_TILE
LAT_TILES_PER_SEQ = DEC_SEQ // TOK_TILE
ROUTE_TILE = 512
VMEM_LIMIT = 56 * 1024 * 1024

GLA_SEQS = 4
CTX_CHUNKS = SEQ // GLA_CHUNK
LAT_CHUNKS = DEC_SEQ // GLA_CHUNK
CHUNKS_PER_TILE = TOK_TILE // GLA_CHUNK
GLA_CTX_STEPS = (BATCH // GLA_SEQS) * CTX_CHUNKS
TILE_GRID = 8

NT_DIMS = (((1,), (1,)), ((), ()))
TN_DIMS = (((0,), (0,)), ((), ()))

assert DEC_BATCH == GLA_SEQS and SEQ == TOK_TILE and N_TILES == TILE_GRID * TILE_GRID


def _params(semantics, vmem=VMEM_LIMIT):
    return pltpu.CompilerParams(dimension_semantics=semantics, vmem_limit_bytes=vmem)


def _split_bf16(a):
    hi = a.astype(BF16)
    lo = (a - hi.astype(F32)).astype(BF16)
    return hi, lo


def _dot(a, b):
    return jnp.dot(a, b, preferred_element_type=F32)


def _dot3(a, b):
    a_hi, a_lo = _split_bf16(a)
    b_hi, b_lo = _split_bf16(b)
    return _dot(a_hi, b_hi) + _dot(a_lo, b_hi) + _dot(a_hi, b_lo)


def _sigmoid(x):
    return 1.0 / (1.0 + jnp.exp(-x))


HALF_W = D_MODEL // 2
HIGH_HALF_MASK = -65536


def _pack_halves(lo, hi):
    lo_bits = pltpu.bitcast(lo.astype(BF16).astype(F32), I32)
    hi_bits = pltpu.bitcast(hi.astype(BF16).astype(F32), I32)
    return lax.shift_right_logical(lo_bits, 16) | (hi_bits & HIGH_HALF_MASK)


def _unpack_halves(words):
    lo = pltpu.bitcast(lax.shift_left(words, 16), F32)
    hi = pltpu.bitcast(words & HIGH_HALF_MASK, F32)
    return lo, hi


def _rms(x):
    return x * lax.rsqrt(jnp.mean(x * x, axis=-1, keepdims=True) + NORM_EPS)


def _mod_row(t):
    return jnp.where(t < CTX_TILES, 0, 1 + (t - CTX_TILES) // LAT_TILES_PER_SEQ)


def _store_tile(t):
    u = t - CTX_TILES
    return jnp.where(t < CTX_TILES, t,
                     CTX_TILES + DEC_BATCH * (u % LAT_TILES_PER_SEQ) + u // LAT_TILES_PER_SEQ)


def _ctx_tile(t):
    return jnp.minimum(t, CTX_TILES - 1)


def _lat_tile(t):
    return jnp.maximum(t - CTX_TILES, 0)


def _mod_kernel(c_ref, w_ref, b_ref, o_ref):
    c = c_ref[...]
    o_ref[...] = _dot3(c * _sigmoid(c), w_ref[...]) + b_ref[...]


def _mod_call(cvec, w_mod, b_mod):
    rows = cvec.shape[0]
    return pl.pallas_call(
        _mod_kernel,
        out_shape=jax.ShapeDtypeStruct((rows, N_MOD * D_MODEL), F32),
        grid=(N_MOD,),
        in_specs=[
            pl.BlockSpec((rows, D_MODEL), lambda j: (0, 0)),
            pl.BlockSpec((D_MODEL, D_MODEL), lambda j: (0, j)),
            pl.BlockSpec((1, D_MODEL), lambda j: (0, j)),
        ],
        out_specs=pl.BlockSpec((rows, D_MODEL), lambda j: (0, j)),
        compiler_params=_params(("arbitrary",)),
        name="mod",
    )(cvec, w_mod, b_mod)


def _inproj_kernel(xc_ref, xl_ref, mod_ref, g_ref, wmain_ref, walr_ref, wxp_ref, wmg_ref,
                   q_ref, k_ref, v_ref, og_ref, alr_ref, xp_ref, mg_ref):
    t = pl.program_id(0)
    x = jnp.where(t < CTX_TILES, xc_ref[...], xl_ref[...])
    mod = mod_ref[0]
    shift1 = mod[:, 0:D_MODEL]
    scale1 = mod[:, D_MODEL:2 * D_MODEL]
    h = _rms(x) * g_ref[...]
    h = (h * (1.0 + scale1) + shift1).astype(BF16)
    z = _dot(h, wmain_ref[...])
    q_ref[...] = (z[:, 0:QK_W] * (GLA_DK ** -0.5)).astype(BF16)
    k_ref[...] = z[:, QK_W:2 * QK_W].astype(BF16)
    v_ref[...] = z[:, 2 * QK_W:2 * QK_W + V_W].astype(BF16)
    og = z[:, 2 * QK_W + V_W:MAIN_W]
    og_ref[...] = (og * _sigmoid(og)).astype(BF16)
    alr_ref[...] = _dot(h, walr_ref[...])
    xp_ref[...] = _dot(h, wxp_ref[...])
    mg_ref[...] = _sigmoid(_dot(h, wmg_ref[...])).astype(BF16)


def _inproj_call(x_ctx, x_lat, mod3, norm1_g, w_main, w_alr, w_xp, w_mg):
    const = lambda t: (0, 0)
    stored = lambda t: (_store_tile(t), 0)
    widths = (QK_W, QK_W, V_W, V_W, ALR_W, POOL_W, MG_W)
    dtypes = (BF16, BF16, BF16, BF16, F32, F32, BF16)
    return pl.pallas_call(
        _inproj_kernel,
        out_shape=[jax.ShapeDtypeStruct((N_TOK, w), dt) for w, dt in zip(widths, dtypes)],
        grid=(N_TILES,),
        in_specs=[
            pl.BlockSpec((TOK_TILE, D_MODEL), lambda t: (_ctx_tile(t), 0)),
            pl.BlockSpec((TOK_TILE, D_MODEL), lambda t: (_lat_tile(t), 0)),
            pl.BlockSpec((1, 1, N_MOD * D_MODEL), lambda t: (_mod_row(t), 0, 0)),
            pl.BlockSpec((1, D_MODEL), const),
            pl.BlockSpec((D_MODEL, MAIN_W), const),
            pl.BlockSpec((D_MODEL, ALR_W), const),
            pl.BlockSpec((D_MODEL, POOL_W), const),
            pl.BlockSpec((D_MODEL, MG_W), const),
        ],
        out_specs=[pl.BlockSpec((TOK_TILE, w), stored) for w in widths],
        compiler_params=_params(("arbitrary",)),
        name="inproj",
    )(x_ctx, x_lat, mod3, norm1_g, w_main, w_alr, w_xp, w_mg)


def _gla_direction(q_ref, k_ref, v_ref, alr_ref, wa_ref, ba_ref, o_ref, st_ref, d, rev):
    rows = GLA_SEQS * GLA_CHUNK
    stack = lambda ref, cols: jnp.concatenate([ref[0, s, :, cols] for s in range(GLA_SEQS)], axis=0)
    alr = stack(alr_ref, slice(None))
    a = _dot3(alr, wa_ref[...]) + ba_ref[...]
    g = (jnp.minimum(a, 0.0) - jnp.log(1.0 + jnp.exp(-jnp.abs(a)))) * (1.0 / GLA_TAU)

    row = lax.broadcasted_iota(I32, (rows, rows), 0)
    col = lax.broadcasted_iota(I32, (rows, rows), 1)
    same = (row // GLA_CHUNK) == (col // GLA_CHUNK)
    tri = same & ((col >= row) if rev else (col <= row))
    tri_b = jnp.where(tri, 1.0, 0.0).astype(BF16)
    g_hi, g_lo = _split_bf16(g)
    bcum = _dot(tri_b, g_hi) + _dot(tri_b, g_lo)

    def per_seq_row(r):
        return jnp.concatenate(
            [jnp.broadcast_to(bcum[s * GLA_CHUNK + r:s * GLA_CHUNK + r + 1], (GLA_CHUNK, QK_W))
             for s in range(GLA_SEQS)], axis=0)

    r_last = 0 if rev else GLA_CHUNK - 1
    blast = per_seq_row(r_last)
    bmid = per_seq_row(GLA_CHUNK // 2)
    e_q = jnp.exp(bcum - bmid)
    e_k = jnp.exp(bmid - bcum)
    e_in = jnp.exp(bcum)
    e_out = jnp.exp(blast - bcum)
    q = stack(q_ref, slice(None)).astype(F32)
    k = stack(k_ref, slice(None)).astype(F32)

    wide = (rows, GLA_SEQS * GLA_DK)
    own = (lax.broadcasted_iota(I32, wide, 0) // GLA_CHUNK) == (lax.broadcasted_iota(I32, wide, 1) // GLA_DK)

    def block_diag(x):
        return jnp.where(own, jnp.concatenate([x] * GLA_SEQS, axis=1), 0.0).astype(BF16)

    for h in range(GLA_HEADS):
        ks = slice(h * GLA_DK, (h + 1) * GLA_DK)
        vs = slice(h * GLA_DV, (h + 1) * GLA_DV)
        qh = q[:, ks]
        kh = k[:, ks]
        vh = stack(v_ref, vs)
        att = lax.dot_general((qh * e_q[:, ks]).astype(BF16), (kh * e_k[:, ks]).astype(BF16),
                              NT_DIMS, preferred_element_type=F32)
        att = jnp.where(tri, att, 0.0).astype(BF16)
        st = st_ref[d, h]
        o_inter = lax.dot_general(block_diag(qh * e_in[:, ks]), st.astype(BF16), NT_DIMS,
                                  preferred_element_type=F32)
        o_h = o_inter + _dot(att, vh)
        for s in range(GLA_SEQS):
            o_ref[0, s, :, vs] = o_h[s * GLA_CHUNK:(s + 1) * GLA_CHUNK]
        upd = lax.dot_general(vh, block_diag(kh * e_out[:, ks]), TN_DIMS,
                              preferred_element_type=F32)
        e_last = jnp.concatenate(
            [jnp.exp(bcum[s * GLA_CHUNK + r_last:s * GLA_CHUNK + r_last + 1, ks])
             for s in range(GLA_SEQS)], axis=1)
        st_ref[d, h] = st * e_last + upd


def _gla_kernel(qf_ref, kf_ref, vf_ref, af_ref, qb_ref, kb_ref, vb_ref, ab_ref,
                waf_ref, baf_ref, wab_ref, bab_ref, s0f_ref, s0b_ref,
                of_ref, ob_ref, sf_ref, sb_ref, st_ref):
    i = pl.program_id(0)
    is_ctx = i < GLA_CTX_STEPS
    chunk = jnp.where(is_ctx, i % CTX_CHUNKS, i - GLA_CTX_STEPS)

    @pl.when(is_ctx & (chunk == 0))
    def _():
        st_ref[...] = jnp.zeros(st_ref.shape, F32)

    @pl.when(i == GLA_CTX_STEPS)
    def _():
        for s in range(GLA_SEQS):
            ls = slice(s * GLA_DK, (s + 1) * GLA_DK)
            for h in range(GLA_HEADS):
                st_ref[0, h, :, ls] = s0f_ref[s, h].T
                st_ref[1, h, :, ls] = s0b_ref[s, h].T

    _gla_direction(qf_ref, kf_ref, vf_ref, af_ref, waf_ref, baf_ref, of_ref, st_ref, 0, False)
    _gla_direction(qb_ref, kb_ref, vb_ref, ab_ref, wab_ref, bab_ref, ob_ref, st_ref, 1, True)

    @pl.when(is_ctx & (chunk == CTX_CHUNKS - 1))
    def _():
        for s in range(GLA_SEQS):
            ls = slice(s * GLA_DK, (s + 1) * GLA_DK)
            for h in range(GLA_HEADS):
                sf_ref[s, h] = st_ref[0, h, :, ls].T
                sb_ref[s, h] = st_ref[1, h, :, ls].T


def _gla_block(i, rev):
    is_ctx = i < GLA_CTX_STEPS
    group = i // CTX_CHUNKS
    c_ctx = i % CTX_CHUNKS
    c_lat = i - GLA_CTX_STEPS
    if rev:
        c_ctx = CTX_CHUNKS - 1 - c_ctx
        c_lat = LAT_CHUNKS - 1 - c_lat
    j = c_lat // CHUNKS_PER_TILE
    per_row = TILE_GRID // GLA_SEQS
    a = jnp.where(is_ctx, group // per_row, CTX_TILES // TILE_GRID + j // per_row)
    b = jnp.where(is_ctx, group % per_row, j % per_row)
    c = jnp.where(is_ctx, c_ctx, c_lat % CHUNKS_PER_TILE)
    return (a, b, c, 0)


def _gla_call(q, k, v, alr, wa_f, ba_f, wa_b, ba_b, s0_f, s0_b):
    def view(arr):
        return arr.reshape(TILE_GRID, TILE_GRID, TOK_TILE, arr.shape[-1])

    def spec(width, rev):
        return pl.BlockSpec((1, GLA_SEQS, GLA_CHUNK, width), lambda i: _gla_block(i, rev))

    const = lambda i: (0, 0)
    st_block = (GLA_SEQS, GLA_HEADS, GLA_DK, GLA_DV)
    whole_state = pl.BlockSpec(st_block, lambda i: (0, 0, 0, 0))
    ctx_state = pl.BlockSpec(
        st_block, lambda i: (jnp.minimum(i // CTX_CHUNKS, BATCH // GLA_SEQS - 1), 0, 0, 0))
    in_specs = []
    for rev in (False, True):
        in_specs += [spec(QK_W, rev), spec(QK_W, rev), spec(V_W, rev), spec(ALR_W, rev)]
    in_specs += [pl.BlockSpec((ALR_W, QK_W), const), pl.BlockSpec((1, QK_W), const)] * 2
    in_specs += [whole_state, whole_state]
    o_shape = jax.ShapeDtypeStruct((TILE_GRID, TILE_GRID, TOK_TILE, V_W), F32)
    s_shape = jax.ShapeDtypeStruct((BATCH, GLA_HEADS, GLA_DK, GLA_DV), F32)
    qv, kv, vv, av = view(q), view(k), view(v), view(alr)
    o_f, o_b, s_f, s_b = pl.pallas_call(
        _gla_kernel,
        out_shape=[o_shape, o_shape, s_shape, s_shape],
        grid=(GLA_CTX_STEPS + LAT_CHUNKS,),
        in_specs=in_specs,
        out_specs=[spec(V_W, False), spec(V_W, True), ctx_state, ctx_state],
        scratch_shapes=[pltpu.VMEM((2, GLA_HEADS, GLA_DV, GLA_SEQS * GLA_DK), F32)],
        compiler_params=_params(("arbitrary",)),
        name="gla",
    )(qv, kv, vv, av, qv, kv, vv, av, wa_f, ba_f, wa_b, ba_b, s0_f, s0_b)
    return o_f.reshape(N_TOK, V_W), o_b.reshape(N_TOK, V_W), s_f, s_b


def _band(n, w, block):
    row = lax.broadcasted_iota(I32, (n, n), 0)
    col = lax.broadcasted_iota(I32, (n, n), 1)
    inside = (col >= row - w // 2) & (col <= row + w // 2 - 1)
    if block < n:
        inside = inside & ((row // block) == (col // block))
    return jnp.where(inside, 1.0, 0.0).astype(BF16)


def _win_count(p, n, w):
    return jnp.minimum(p + w // 2 - 1, n - 1) - jnp.maximum(p - w // 2, 0) + 1


def _pool_ctx_kernel(x_ref, o_ref, band_ref):
    @pl.when(pl.program_id(0) == 0)
    def _():
        for gi, w in enumerate(POOL_WINDOWS):
            band_ref[gi] = _band(SEQ, w, SEQ)

    p = lax.broadcasted_iota(I32, (SEQ, POOL_GROUP_DIM), 0)
    for gi, w in enumerate(POOL_WINDOWS):
        cs = slice(gi * POOL_GROUP_DIM, (gi + 1) * POOL_GROUP_DIM)
        x = x_ref[:, cs]
        hi, lo = _split_bf16(x)
        band = band_ref[gi]
        s = _dot(band, hi) + _dot(band, lo)
        cnt = _win_count(p, SEQ, w).astype(F32)
        o_ref[:, cs] = s / cnt - x


def _pool_ctx_call(xp):
    spec = pl.BlockSpec((SEQ, POOL_W), lambda b: (b, 0))
    return pl.pallas_call(
        _pool_ctx_kernel,
        out_shape=jax.ShapeDtypeStruct((N_CTX, POOL_W), F32),
        grid=(BATCH,),
        in_specs=[spec],
        out_specs=spec,
        scratch_shapes=[pltpu.VMEM((POOL_GROUPS, SEQ, SEQ), BF16)],
        compiler_params=_params(("arbitrary",)),
        name="pool_ctx",
    )(xp)


POOL_HALO = (max(POOL_WINDOWS) // 2) * GRID_W


def _pool_lat_kernel(x_ref, o_ref, pad_ref):
    rows = DEC_SEQ // GRID_W
    p = lax.broadcasted_iota(I32, (DEC_SEQ, POOL_GROUP_DIM), 0)
    r = p // GRID_W
    cidx = p % GRID_W
    zeros = jnp.zeros((POOL_HALO, POOL_GROUP_DIM), F32)
    pad_ref[0:POOL_HALO, :] = zeros
    pad_ref[POOL_HALO + DEC_SEQ:2 * POOL_HALO + DEC_SEQ, :] = zeros
    for gi, w in enumerate(POOL_WINDOWS):
        cs = slice(gi * POOL_GROUP_DIM, (gi + 1) * POOL_GROUP_DIM)
        band = _band(TOK_TILE, w, GRID_W)
        for t in range(LAT_TILES_PER_SEQ):
            hi, lo = _split_bf16(x_ref[t, 0, :, cs])
            pad_ref[POOL_HALO + t * TOK_TILE:POOL_HALO + (t + 1) * TOK_TILE, :] = (
                _dot(band, hi) + _dot(band, lo))
        acc = jnp.zeros((DEC_SEQ, POOL_GROUP_DIM), F32)
        for dr in range(-(w // 2), w // 2):
            start = POOL_HALO + dr * GRID_W
            acc = acc + pad_ref[start:start + DEC_SEQ, :]
        cnt = (_win_count(r, rows, w) * _win_count(cidx, GRID_W, w)).astype(F32)
        pooled = acc / cnt
        for t in range(LAT_TILES_PER_SEQ):
            rs = slice(t * TOK_TILE, (t + 1) * TOK_TILE)
            o_ref[t, 0, :, cs] = pooled[rs] - x_ref[t, 0, :, cs]


def _pool_lat_call(xp):
    view = xp.reshape(N_TILES // DEC_BATCH, DEC_BATCH, TOK_TILE, POOL_W)
    blk = (LAT_TILES_PER_SEQ, 1, TOK_TILE, POOL_W)
    out = pl.pallas_call(
        _pool_lat_kernel,
        out_shape=jax.ShapeDtypeStruct((LAT_TILES_PER_SEQ, DEC_BATCH, TOK_TILE, POOL_W), F32),
        grid=(DEC_BATCH,),
        in_specs=[pl.BlockSpec(blk, lambda s: (CTX_TILES // DEC_BATCH // LAT_TILES_PER_SEQ, s, 0, 0))],
        out_specs=pl.BlockSpec(blk, lambda s: (0, s, 0, 0)),
        scratch_shapes=[pltpu.VMEM((DEC_SEQ + 2 * POOL_HALO, POOL_GROUP_DIM), F32)],
        compiler_params=_params(("arbitrary",)),
        name="pool_lat",
    )(view)
    return out.reshape(N_LAT, POOL_W)


def _post_kernel(xc_ref, xl_ref, mod_ref, of_ref, ob_ref, og_ref, pc_ref, pl_ref, mg_ref, gng_ref,
                 wpg_ref, psc_ref, wbg_ref, wbp_ref, wout_ref, n2g_ref, wr_ref, br_ref,
                 x1_ref, h2_ref, idx_ref, tw_ref):
    t = pl.program_id(0)
    is_ctx = t < CTX_TILES
    x = jnp.where(is_ctx, xc_ref[...], xl_ref[...])
    pooled = jnp.where(is_ctx, pc_ref[...], pl_ref[...])
    mod = mod_ref[0]
    gate1 = mod[:, 2 * D_MODEL:3 * D_MODEL]
    shift2 = mod[:, 3 * D_MODEL:4 * D_MODEL]
    scale2 = mod[:, 4 * D_MODEL:5 * D_MODEL]

    o = of_ref[...] + ob_ref[...]
    og = og_ref[...].astype(F32)
    gated = []
    for h in range(GLA_HEADS):
        vs = slice(h * GLA_DV, (h + 1) * GLA_DV)
        oh = _rms(o[:, vs]) * gng_ref[:, vs]
        gated.append((oh * og[:, vs]).astype(BF16))
    br_gla = _dot(jnp.concatenate(gated, axis=-1), wbg_ref[...])

    pm = []
    for gi in range(POOL_GROUPS):
        cs = slice(gi * POOL_GROUP_DIM, (gi + 1) * POOL_GROUP_DIM)
        pmg = _dot(pooled[:, cs].astype(BF16), wpg_ref[gi]) * psc_ref[:, cs]
        pm.append(pmg.astype(BF16))
    br_pool = _dot(jnp.concatenate(pm, axis=-1), wbp_ref[...])

    mg = mg_ref[...].astype(F32)
    merged = mg[:, 0:D_MODEL] * br_gla + mg[:, D_MODEL:MG_W] * br_pool
    m = _dot(merged.astype(BF16), wout_ref[...])
    x1 = x + gate1 * m
    x1_ref[...] = x1
    h2 = _rms(x1) * n2g_ref[...]
    h2 = h2 * (1.0 + scale2) + shift2
    h2_ref[...] = _pack_halves(h2[:, :HALF_W], h2[:, HALF_W:])

    logits = _dot3(h2, wr_ref[...]) + br_ref[...]
    lane = lax.broadcasted_iota(I32, (TOK_TILE, LANES), 1)
    lane_f = lane.astype(F32)
    neg = jnp.float32(-jnp.inf)
    cur = jnp.where(lane < N_EXPERTS, logits, neg)
    vals, idxs = [], []
    for _ in range(TOP_K):
        mx = jnp.max(cur, axis=-1, keepdims=True)
        ix = jnp.min(jnp.where(cur == mx, lane_f, float(LANES)), axis=-1, keepdims=True)
        vals.append(mx)
        idxs.append(ix)
        cur = jnp.where(lane_f == ix, neg, cur)
    ex = [jnp.exp(vv - vals[0]) for vv in vals]
    tot = ex[0] + ex[1] + ex[2] + ex[3]
    idx_out = jnp.zeros((TOK_TILE, LANES), F32)
    w_out = jnp.zeros((TOK_TILE, LANES), F32)
    for kk in range(TOP_K):
        idx_out = jnp.where(lane == kk, idxs[kk], idx_out)
        w_out = jnp.where(lane == kk, ex[kk] / tot, w_out)
    idx_ref[...] = idx_out.astype(I32)
    tw_ref[...] = w_out


def _post_call(x_ctx, x_lat, mod3, o_f, o_b, og, pooled_c, pooled_l, mg, gng, wpg, psc, wbg, wbp,
               wout, n2g, wr, br):
    row = lambda t: (t, 0)
    const = lambda t: (0, 0)
    stored = lambda t: (_store_tile(t), 0)
    return pl.pallas_call(
        _post_kernel,
        out_shape=[
            jax.ShapeDtypeStruct((N_TOK, D_MODEL), F32),
            jax.ShapeDtypeStruct((N_TOK, HALF_W), I32),
            jax.ShapeDtypeStruct((N_TOK, LANES), I32),
            jax.ShapeDtypeStruct((N_TOK, LANES), F32),
        ],
        grid=(N_TILES,),
        in_specs=[
            pl.BlockSpec((TOK_TILE, D_MODEL), lambda t: (_ctx_tile(t), 0)),
            pl.BlockSpec((TOK_TILE, D_MODEL), lambda t: (_lat_tile(t), 0)),
            pl.BlockSpec((1, 1, N_MOD * D_MODEL), lambda t: (_mod_row(t), 0, 0)),
            pl.BlockSpec((TOK_TILE, V_W), stored),
            pl.BlockSpec((TOK_TILE, V_W), stored),
            pl.BlockSpec((TOK_TILE, V_W), stored),
            pl.BlockSpec((TOK_TILE, POOL_W), lambda t: (_ctx_tile(t), 0)),
            pl.BlockSpec((TOK_TILE, POOL_W),
                         lambda t: (jnp.maximum(_store_tile(t) - CTX_TILES, 0), 0)),
            pl.BlockSpec((TOK_TILE, MG_W), stored),
            pl.BlockSpec((1, V_W), const),
            pl.BlockSpec((POOL_GROUPS, POOL_GROUP_DIM, POOL_GROUP_DIM), lambda t: (0, 0, 0)),
            pl.BlockSpec((1, POOL_W), const),
            pl.BlockSpec((V_W, D_MODEL), const),
            pl.BlockSpec((POOL_W, D_MODEL), const),
            pl.BlockSpec((D_MODEL, D_MODEL), const),
            pl.BlockSpec((1, D_MODEL), const),
            pl.BlockSpec((D_MODEL, LANES), const),
            pl.BlockSpec((1, LANES), const),
        ],
        out_specs=[
            pl.BlockSpec((TOK_TILE, D_MODEL), row),
            pl.BlockSpec((TOK_TILE, HALF_W), row),
            pl.BlockSpec((TOK_TILE, LANES), row),
            pl.BlockSpec((TOK_TILE, LANES), row),
        ],
        compiler_params=_params(("arbitrary",)),
        name="post",
    )(x_ctx, x_lat, mod3, o_f, o_b, og, pooled_c, pooled_l, mg, gng, wpg, psc, wbg, wbp, wout, n2g,
      wr, br)


def _route_kernel(idx_ref, rank_ref, cnt_ref, carry_ref, strict_ref):
    t = pl.program_id(0)

    @pl.when(t == 0)
    def _():
        carry_ref[...] = jnp.zeros((1, LANES), F32)
        row = lax.broadcasted_iota(I32, (ROUTE_TILE, ROUTE_TILE), 0)
        col = lax.broadcasted_iota(I32, (ROUTE_TILE, ROUTE_TILE), 1)
        strict_ref[...] = jnp.where(col < row, 1.0, 0.0).astype(BF16)

    idx = idx_ref[...]
    lane = lax.broadcasted_iota(I32, (ROUTE_TILE, LANES), 1)
    sel = [lane == idx[:, kk:kk + 1] for kk in range(TOP_K)]
    onehot = jnp.zeros((ROUTE_TILE, LANES), F32)
    for kk in range(TOP_K):
        onehot = onehot + jnp.where(sel[kk], 1.0, 0.0)
    before = _dot(strict_ref[...], onehot.astype(BF16)) + carry_ref[...]
    rank = jnp.zeros((ROUTE_TILE, LANES), F32)
    for kk in range(TOP_K):
        rk = jnp.sum(jnp.where(sel[kk], before, 0.0), axis=-1, keepdims=True)
        rank = jnp.where(lane == kk, rk, rank)
    rank_ref[...] = rank.astype(I32)
    carry_ref[...] = carry_ref[...] + jnp.sum(onehot, axis=0, keepdims=True)
    cnt_ref[...] = jnp.broadcast_to(carry_ref[...], (8, LANES))


def _route_call(idx):
    return pl.pallas_call(
        _route_kernel,
        out_shape=[
            jax.ShapeDtypeStruct((N_TOK, LANES), I32),
            jax.ShapeDtypeStruct((8, LANES), F32),
        ],
        grid=(N_TOK // ROUTE_TILE,),
        in_specs=[pl.BlockSpec((ROUTE_TILE, LANES), lambda t: (t, 0))],
        out_specs=[
            pl.BlockSpec((ROUTE_TILE, LANES), lambda t: (t, 0)),
            pl.BlockSpec((8, LANES), lambda t: (0, 0)),
        ],
        scratch_shapes=[pltpu.VMEM((1, LANES), F32), pltpu.VMEM((ROUTE_TILE, ROUTE_TILE), BF16)],
        compiler_params=_params(("arbitrary",)),
        name="route",
    )(idx)


def _moe_kernel(be_ref, nu_ref, ne_ref, par_ref, x_ref, wg_hbm, bg_ref, wu_hbm, bu_ref, wd_hbm,
                bd_ref, y_ref, wf_ref, wgu_ref, wdb_ref, sems):
    b = pl.program_id(0)
    n_used = nu_ref[0]
    e = be_ref[b]
    prev = be_ref[jnp.maximum(b - 1, 0)]
    live = b < n_used

    def weight_copies(expert, slot):
        return [pltpu.make_async_copy(w.at[expert], wf_ref.at[slot, i], sems.at[slot])
                for i, w in enumerate((wg_hbm, wu_hbm, wd_hbm))]

    @pl.when(live & ((b == 0) | (e != prev)))
    def _():
        slot = par_ref[b]

        @pl.when(b == 0)
        def _():
            for cp in weight_copies(e, slot):
                cp.start()

        for cp in weight_copies(e, slot):
            cp.wait()
        nxt = ne_ref[b]

        @pl.when(nxt >= 0)
        def _():
            for cp in weight_copies(nxt, 1 - slot):
                cp.start()

        wgu_ref[:, :D_FF] = wf_ref[slot, 0].astype(BF16)
        wgu_ref[:, D_FF:] = wf_ref[slot, 1].astype(BF16)
        wdb_ref[...] = wf_ref[slot, 2].astype(BF16)

    @pl.when(live)
    def _():
        x_lo, x_hi = _unpack_halves(x_ref[...])
        x = jnp.concatenate([x_lo.astype(BF16), x_hi.astype(BF16)], axis=-1)
        gu = _dot(x, wgu_ref[...])
        gate = jnp.minimum(gu[:, :D_FF] + bg_ref[0], SWIGLU_LIMIT)
        up = jnp.clip(gu[:, D_FF:] + bu_ref[0], -SWIGLU_LIMIT, SWIGLU_LIMIT)
        act = (up + 1.0) * (gate * _sigmoid(SWIGLU_ALPHA * gate))
        y = _dot(act.astype(BF16), wdb_ref[...]) + bd_ref[0]
        y_ref[...] = _pack_halves(y[:, :HALF_W], y[:, HALF_W:])

    @pl.when(jnp.logical_not(live))
    def _():
        y_ref[...] = jnp.zeros((MOE_BLOCK, HALF_W), I32)


def _moe_call(block_e, n_used, next_e, parity, hs, w_gate, b_gate, w_up, b_up, w_down, b_down):
    def blk(b, be, nu, ne, par):
        return jnp.minimum(b, nu[0] - 1)

    row = lambda b, be, nu, ne, par: (blk(b, be, nu, ne, par), 0)
    bsel = lambda b, be, nu, ne, par: (be[blk(b, be, nu, ne, par)], 0, 0)
    any_spec = pl.BlockSpec(memory_space=pl.ANY)
    assert D_MODEL == D_FF
    return pl.pallas_call(
        _moe_kernel,
        out_shape=jax.ShapeDtypeStruct((N_SLOTS, HALF_W), I32),
        grid_spec=pltpu.PrefetchScalarGridSpec(
            num_scalar_prefetch=4,
            grid=(N_SLOT_BLOCKS,),
            in_specs=[
                pl.BlockSpec((MOE_BLOCK, HALF_W), row),
                any_spec,
                pl.BlockSpec((1, 1, D_FF), bsel),
                any_spec,
                pl.BlockSpec((1, 1, D_FF), bsel),
                any_spec,
                pl.BlockSpec((1, 1, D_MODEL), bsel),
            ],
            out_specs=pl.BlockSpec((MOE_BLOCK, HALF_W), lambda b, be, nu, ne, par: (b, 0)),
            scratch_shapes=[
                pltpu.VMEM((2, 3, D_MODEL, D_FF), F32),
                pltpu.VMEM((D_MODEL, 2 * D_FF), BF16),
                pltpu.VMEM((D_FF, D_MODEL), BF16),
                pltpu.SemaphoreType.DMA((2,)),
            ],
        ),
        compiler_params=_params(("arbitrary",)),
        name="moe",
    )(block_e, n_used, next_e, parity, hs, w_gate, b_gate, w_up, b_up, w_down, b_down)


SC_CORES = 2
SC_SUBCORES = 16
SC_WORKERS = SC_CORES * SC_SUBCORES
SC_ROWS = 128
COMBINE_CHUNKS = 2
COMBINE_TILE = 2 * TOK_TILE
assert N_CTX == N_LAT


def _sc_gather_rows(table, idx):
    n_idx = idx.shape[0]
    width = table.shape[1]
    rows = SC_ROWS // 2
    per_worker = n_idx // SC_WORKERS
    n_chunks = per_worker // rows
    assert n_chunks * rows * SC_WORKERS == n_idx and n_chunks >= 2
    mesh = plsc.VectorSubcoreMesh(core_axis_name="c", subcore_axis_name="s")

    @functools.partial(
        pl.kernel, mesh=mesh,
        out_type=jax.ShapeDtypeStruct((n_idx, width), table.dtype),
        scratch_types=[pltpu.VMEM((2, rows), I32), pltpu.VMEM((2, rows, width), table.dtype),
                       pltpu.SemaphoreType.DMA((2,)), pltpu.SemaphoreType.DMA((2,))],
        name="sc_gather",
    )
    def gather(table_hbm, idx_hbm, out_hbm, idx_v, rows_v, sem_g, sem_w):
        worker = lax.axis_index("s") * SC_CORES + lax.axis_index("c")
        base = worker * per_worker

        def chunk_rows(ch):
            return pl.ds(pl.multiple_of(base + ch * rows, rows), rows)

        def start_gather(ch, b):
            pltpu.sync_copy(idx_hbm.at[chunk_rows(ch)], idx_v.at[b])
            pltpu.async_copy(table_hbm.at[idx_v.at[b]], rows_v.at[b], sem_g.at[b])

        def wait_gather(b):
            pltpu.make_async_copy(table_hbm.at[pl.ds(0, rows)], rows_v.at[b], sem_g.at[b]).wait()

        def write_copy(ch, b):
            return pltpu.make_async_copy(rows_v.at[b], out_hbm.at[chunk_rows(ch)], sem_w.at[b])

        start_gather(0, 0)
        for ch in range(n_chunks):
            b = ch % 2
            if ch + 1 < n_chunks:
                if ch >= 1:
                    write_copy(ch - 1, 1 - b).wait()
                start_gather(ch + 1, 1 - b)
            wait_gather(b)
            write_copy(ch, b).start()
        write_copy(n_chunks - 2, n_chunks % 2).wait()
        write_copy(n_chunks - 1, (n_chunks - 1) % 2).wait()

    return gather(table, idx)


def _sc_scatter_rows(rows, idx3, n_out):
    n_rows, width = rows.shape
    n_chunks = n_rows // SC_ROWS // SC_WORKERS
    assert n_chunks * SC_ROWS * SC_WORKERS == n_rows and idx3.shape == (n_rows // SC_ROWS, TOP_K, SC_ROWS)
    mesh = plsc.VectorSubcoreMesh(core_axis_name="c", subcore_axis_name="s")

    @functools.partial(
        pl.kernel, mesh=mesh,
        out_type=jax.ShapeDtypeStruct((n_out, width), rows.dtype),
        scratch_types=[pltpu.VMEM((TOP_K, SC_ROWS), I32), pltpu.VMEM((SC_ROWS, width), rows.dtype),
                       pltpu.SemaphoreType.DMA],
        name="sc_scatter",
    )
    def scatter(rows_hbm, idx_hbm, out_hbm, idx_v, rows_v, sem):
        worker = lax.axis_index("s") * SC_CORES + lax.axis_index("c")

        @pl.loop(0, n_chunks)
        def _(ch):
            chunk = worker * n_chunks + ch
            pltpu.sync_copy(idx_hbm.at[chunk], idx_v)
            pltpu.sync_copy(rows_hbm.at[pl.ds(pl.multiple_of(chunk * SC_ROWS, SC_ROWS), SC_ROWS)],
                            rows_v)
            for kk in range(TOP_K):
                pltpu.async_copy(rows_v, out_hbm.at[idx_v.at[kk]], sem).wait()

    return scatter(rows, idx3)


def _combine_kernel(x1_ref, mod_ref, tw_ref, fg_ref, g_ref, *rest):
    out_ref = rest[-1]
    tw = tw_ref[...]
    f_lo = jnp.zeros((COMBINE_TILE, HALF_W), F32)
    f_hi = jnp.zeros((COMBINE_TILE, HALF_W), F32)
    for kk in range(TOP_K):
        lo, hi = _unpack_halves(g_ref[kk])
        f_lo = f_lo + lo * tw[:, kk:kk + 1]
        f_hi = f_hi + hi * tw[:, kk:kk + 1]
    gate2 = mod_ref[0][:, 5 * D_MODEL:6 * D_MODEL]
    x2 = x1_ref[...] + gate2 * jnp.concatenate([f_lo, f_hi], axis=-1)
    out_ref[...] = _rms(x2) * fg_ref[...]


def _combine_call(x1, mod3, tw, final_g, gathered, partial, tile0, out_tile0, out_tiles, name):
    n_tiles = gathered.shape[1] // COMBINE_TILE
    in_specs = [
        pl.BlockSpec((COMBINE_TILE, D_MODEL), lambda t: (tile0 + t, 0)),
        pl.BlockSpec((1, 1, N_MOD * D_MODEL), lambda t: (_mod_row((tile0 + t) * (COMBINE_TILE // TOK_TILE)), 0, 0)),
        pl.BlockSpec((COMBINE_TILE, LANES), lambda t: (tile0 + t, 0)),
        pl.BlockSpec((1, D_MODEL), lambda t: (0, 0)),
        pl.BlockSpec((TOP_K, COMBINE_TILE, HALF_W), lambda t: (0, t, 0)),
    ]
    args = [x1, mod3, tw, final_g, gathered]
    aliases = {}
    if partial is not None:
        in_specs.append(pl.BlockSpec(memory_space=pl.ANY))
        args.append(partial)
        aliases = {len(args) - 1: 0}
    return pl.pallas_call(
        _combine_kernel,
        out_shape=jax.ShapeDtypeStruct((out_tiles * COMBINE_TILE, D_MODEL), F32),
        grid=(n_tiles,),
        in_specs=in_specs,
        out_specs=pl.BlockSpec((COMBINE_TILE, D_MODEL), lambda t: (out_tile0 + t, 0)),
        input_output_aliases=aliases,
        compiler_params=_params(("arbitrary",)),
        name=name,
    )(*args)


def kernel(x_prompt, x_sample, state_gla_fwd, state_gla_bwd, c, c_ctx, norm1_g, w_mod, b_mod, w_in,
           w_alpha, b_alpha, gla_norm_g, w_pool_grp, pool_scale, w_branch_gla, w_branch_pool, w_out,
           norm2_g, w_router, b_router, w_gate, b_gate, w_up, b_up, w_down, b_down, final_norm_g):
    l = 0
    x_ctx = x_prompt.reshape(N_CTX, D_MODEL)
    x_lat = x_sample.reshape(N_LAT, D_MODEL)

    cvec = jnp.concatenate([c_ctx[None, :], c, jnp.zeros((8 - 1 - DEC_BATCH, D_MODEL), F32)], axis=0)
    mod = _mod_call(cvec, w_mod[l], b_mod[l][None, :])
    mod3 = mod.reshape(8, 1, N_MOD * D_MODEL)

    w_in_b = w_in[l].astype(BF16)
    w_main = w_in_b[:, :MAIN_W]
    w_alr = w_in_b[:, MAIN_W:MAIN_W + ALR_W]
    w_xp = w_in_b[:, MAIN_W + ALR_W:MAIN_W + ALR_W + POOL_W]
    w_mg = w_in_b[:, MAIN_W + ALR_W + POOL_W:]
    q, k, v, og, alr, xp, mg = _inproj_call(x_ctx, x_lat, mod3, norm1_g[l][None, :],
                                            w_main, w_alr, w_xp, w_mg)

    zpad = jnp.zeros((GLA_LOWRANK, QK_W), F32)
    wa_f = jnp.concatenate([w_alpha[l, 0], zpad], axis=0)
    wa_b = jnp.concatenate([zpad, w_alpha[l, 1]], axis=0)
    o_f, o_b, s_f, s_b = _gla_call(q, k, v, alr, wa_f, b_alpha[l, 0][None, :], wa_b,
                                   b_alpha[l, 1][None, :], state_gla_fwd[:, l], state_gla_bwd[:, l])

    pooled_c = _pool_ctx_call(xp)
    pooled_l = _pool_lat_call(xp)

    w_router_pad = jnp.pad(w_router[l], ((0, 0), (0, LANES - N_EXPERTS)))
    b_router_pad = jnp.pad(b_router[l], (0, LANES - N_EXPERTS))[None, :]
    x1, h2, top_idx, top_w = _post_call(
        x_ctx, x_lat, mod3, o_f, o_b, og, pooled_c, pooled_l, mg,
        gla_norm_g[l].reshape(1, V_W), w_pool_grp[l].astype(BF16), pool_scale[l][None, :],
        w_branch_gla[l].astype(BF16), w_branch_pool[l].astype(BF16), w_out[l].astype(BF16),
        norm2_g[l][None, :], w_router_pad, b_router_pad)

    rank, cnt = _route_call(top_idx)
    counts = cnt[0, :N_EXPERTS].astype(I32)
    padded = (counts + MOE_BLOCK - 1) // MOE_BLOCK * MOE_BLOCK
    pad_end = jnp.cumsum(padded).astype(I32)
    pad_start = pad_end - padded
    block_first = jnp.arange(N_SLOT_BLOCKS, dtype=I32) * MOE_BLOCK
    block_e = jnp.minimum(jnp.sum((pad_end[None, :] <= block_first[:, None]).astype(I32), axis=1),
                          N_EXPERTS - 1).astype(I32)
    n_used = (pad_end[-1:] // MOE_BLOCK).astype(I32)
    run_start = jnp.concatenate([jnp.ones((1,), I32), (block_e[1:] != block_e[:-1]).astype(I32)])
    parity = ((jnp.cumsum(run_start) - 1) % 2).astype(I32)
    after = pad_end[block_e] // MOE_BLOCK
    next_e = jnp.where(after < n_used[0], block_e[jnp.minimum(after, N_SLOT_BLOCKS - 1)], -1).astype(I32)
    experts = jnp.arange(N_EXPERTS, dtype=I32)
    tk = top_idx[:, :TOP_K]
    pos = jnp.sum(jnp.where(tk[:, :, None] == experts, pad_start, 0), axis=-1) + rank[:, :TOP_K]
    pos = pos.astype(I32)
    pos_by_choice = pos.T
    pos_chunks = pos_by_choice.reshape(TOP_K, N_TOK // SC_ROWS, SC_ROWS).transpose(1, 0, 2)

    hs = _sc_scatter_rows(h2, pos_chunks, N_SLOTS)
    y = _moe_call(block_e, n_used, next_e, parity, hs,
                  w_gate[l], b_gate[l][:, None, :], w_up[l], b_up[l][:, None, :],
                  w_down[l], b_down[l][:, None, :])
    outs = []
    chunk_tok = N_CTX // COMBINE_CHUNKS
    for group, tok0 in (("ctx", 0), ("lat", N_CTX)):
        out = None
        for ci in range(COMBINE_CHUNKS):
            t0 = tok0 + ci * chunk_tok
            idx = pos_by_choice[:, t0:t0 + chunk_tok].reshape(TOP_K * chunk_tok)
            gathered = _sc_gather_rows(y, idx).reshape(TOP_K, chunk_tok, HALF_W)
            out = _combine_call(x1, mod3, top_w, final_norm_g[None, :], gathered, out,
                                t0 // COMBINE_TILE, ci * chunk_tok // COMBINE_TILE, N_CTX // COMBINE_TILE,
                                "combine_%s%d" % (group, ci))
        outs.append(out)
    y_prompt = outs[0].reshape(BATCH, SEQ, D_MODEL)
    y_sample = outs[1].reshape(DEC_BATCH, DEC_SEQ, D_MODEL)
    return (y_prompt, y_sample, s_f[:, None], s_b[:, None])
```

```python
import functools

import jax
import jax.numpy as jnp
from jax import lax
from jax.experimental import pallas as pl
from jax.experimental.pallas import tpu as pltpu
from jax.experimental.pallas import tpu_sc as plsc

F32 = jnp.float32
BF16 = jnp.bfloat16
I32 = jnp.int32

D_MODEL = 1024
BATCH = 32
SEQ = 256
DEC_BATCH = 4
DEC_SEQ = 2048
GRID_W = 64
GLA_HEADS = 4
GLA_DK = 128
GLA_DV = 256
GLA_LOWRANK = 16
GLA_TAU = 16.0
GLA_CHUNK = 64
POOL_GROUPS = 4
POOL_GROUP_DIM = 128
POOL_WINDOWS = (2, 4, 8, 16)
N_EXPERTS = 32
TOP_K = 4
D_FF = 1024
SWIGLU_LIMIT = 7.0
SWIGLU_ALPHA = 1.702
MOE_BLOCK = 256
NORM_EPS = 1e-6
N_MOD = 6

QK_W = GLA_HEADS * GLA_DK
V_W = GLA_HEADS * GLA_DV
POOL_W = POOL_GROUPS * POOL_GROUP_DIM
MAIN_W = 2 * QK_W + 2 * V_W
ALR_W = 2 * GLA_LOWRANK
MG_W = 2 * D_MODEL

N_CTX = BATCH * SEQ
N_LAT = DEC_BATCH * DEC_SEQ
N_TOK = N_CTX + N_LAT
N_SLOT_BLOCKS = -(-(N_TOK * TOP_K + N_EXPERTS * (MOE_BLOCK - 1)) // MOE_BLOCK)
N_SLOTS = N_SLOT_BLOCKS * MOE_BLOCK

LANES = 128
TOK_TILE = 256
N_TILES = N_TOK // TOK_TILE
CTX_TILES = N_CTX // TOK_TILE
LAT_TILES_PER_SEQ = DEC_SEQ // TOK_TILE
ROUTE_TILE = 512
VMEM_LIMIT = 56 * 1024 * 1024

GLA_SEQS = 4
CTX_CHUNKS = SEQ // GLA_CHUNK
LAT_CHUNKS = DEC_SEQ // GLA_CHUNK
CHUNKS_PER_TILE = TOK_TILE // GLA_CHUNK
GLA_CTX_STEPS = (BATCH // GLA_SEQS) * CTX_CHUNKS
TILE_GRID = 8

NT_DIMS = (((1,), (1,)), ((), ()))
TN_DIMS = (((0,), (0,)), ((), ()))

assert DEC_BATCH == GLA_SEQS and SEQ == TOK_TILE and N_TILES == TILE_GRID * TILE_GRID


def _params(semantics, vmem=VMEM_LIMIT):
    return pltpu.CompilerParams(dimension_semantics=semantics, vmem_limit_bytes=vmem)


def _split_bf16(a):
    hi = a.astype(BF16)
    lo = (a - hi.astype(F32)).astype(BF16)
    return hi, lo


def _dot(a, b):
    return jnp.dot(a, b, preferred_element_type=F32)


def _dot3(a, b):
    a_hi, a_lo = _split_bf16(a)
    b_hi, b_lo = _split_bf16(b)
    return _dot(a_hi, b_hi) + _dot(a_lo, b_hi) + _dot(a_hi, b_lo)


def _sigmoid(x):
    return 1.0 / (1.0 + jnp.exp(-x))


HALF_W = D_MODEL // 2
HIGH_HALF_MASK = -65536


def _pack_halves(lo, hi):
    lo_bits = pltpu.bitcast(lo.astype(BF16).astype(F32), I32)
    hi_bits = pltpu.bitcast(hi.astype(BF16).astype(F32), I32)
    return lax.shift_right_logical(lo_bits, 16) | (hi_bits & HIGH_HALF_MASK)


def _unpack_halves(words):
    lo = pltpu.bitcast(lax.shift_left(words, 16), F32)
    hi = pltpu.bitcast(words & HIGH_HALF_MASK, F32)
    return lo, hi


def _rms(x):
    return x * lax.rsqrt(jnp.mean(x * x, axis=-1, keepdims=True) + NORM_EPS)


def _mod_row(t):
    return jnp.where(t < CTX_TILES, 0, 1 + (t - CTX_TILES) // LAT_TILES_PER_SEQ)


def _store_tile(t):
    u = t - CTX_TILES
    return jnp.where(t < CTX_TILES, t,
                     CTX_TILES + DEC_BATCH * (u % LAT_TILES_PER_SEQ) + u // LAT_TILES_PER_SEQ)


def _ctx_tile(t):
    return jnp.minimum(t, CTX_TILES - 1)


def _lat_tile(t):
    return jnp.maximum(t - CTX_TILES, 0)


def _mod_kernel(c_ref, w_ref, b_ref, o_ref):
    c = c_ref[...]
    o_ref[...] = _dot3(c * _sigmoid(c), w_ref[...]) + b_ref[...]


def _mod_call(cvec, w_mod, b_mod):
    rows = cvec.shape[0]
    return pl.pallas_call(
        _mod_kernel,
        out_shape=jax.ShapeDtypeStruct((rows, N_MOD * D_MODEL), F32),
        grid=(N_MOD,),
        in_specs=[
            pl.BlockSpec((rows, D_MODEL), lambda j: (0, 0)),
            pl.BlockSpec((D_MODEL, D_MODEL), lambda j: (0, j)),
            pl.BlockSpec((1, D_MODEL), lambda j: (0, j)),
        ],
        out_specs=pl.BlockSpec((rows, D_MODEL), lambda j: (0, j)),
        compiler_params=_params(("arbitrary",)),
        name="mod",
    )(cvec, w_mod, b_mod)


def _inproj_kernel(xc_ref, xl_ref, mod_ref, g_ref, wmain_ref, walr_ref, wxp_ref, wmg_ref,
                   q_ref, k_ref, v_ref, og_ref, alr_ref, xp_ref, mg_ref):
    t = pl.program_id(0)
    x = jnp.where(t < CTX_TILES, xc_ref[...], xl_ref[...])
    mod = mod_ref[0]
    shift1 = mod[:, 0:D_MODEL]
    scale1 = mod[:, D_MODEL:2 * D_MODEL]
    h = _rms(x) * g_ref[...]
    h = (h * (1.0 + scale1) + shift1).astype(BF16)
    z = _dot(h, wmain_ref[...])
    q_ref[...] = (z[:, 0:QK_W] * (GLA_DK ** -0.5)).astype(BF16)
    k_ref[...] = z[:, QK_W:2 * QK_W].astype(BF16)
    v_ref[...] = z[:, 2 * QK_W:2 * QK_W + V_W].astype(BF16)
    og = z[:, 2 * QK_W + V_W:MAIN_W]
    og_ref[...] = (og * _sigmoid(og)).astype(BF16)
    alr_ref[...] = _dot(h, walr_ref[...])
    xp_ref[...] = _dot(h, wxp_ref[...])
    mg_ref[...] = _sigmoid(_dot(h, wmg_ref[...])).astype(BF16)


def _inproj_call(x_ctx, x_lat, mod3, norm1_g, w_main, w_alr, w_xp, w_mg):
    const = lambda t: (0, 0)
    stored = lambda t: (_store_tile(t), 0)
    widths = (QK_W, QK_W, V_W, V_W, ALR_W, POOL_W, MG_W)
    dtypes = (BF16, BF16, BF16, BF16, F32, F32, BF16)
    return pl.pallas_call(
        _inproj_kernel,
        out_shape=[jax.ShapeDtypeStruct((N_TOK, w), dt) for w, dt in zip(widths, dtypes)],
        grid=(N_TILES,),
        in_specs=[
            pl.BlockSpec((TOK_TILE, D_MODEL), lambda t: (_ctx_tile(t), 0)),
            pl.BlockSpec((TOK_TILE, D_MODEL), lambda t: (_lat_tile(t), 0)),
            pl.BlockSpec((1, 1, N_MOD * D_MODEL), lambda t: (_mod_row(t), 0, 0)),
            pl.BlockSpec((1, D_MODEL), const),
            pl.BlockSpec((D_MODEL, MAIN_W), const),
            pl.BlockSpec((D_MODEL, ALR_W), const),
            pl.BlockSpec((D_MODEL, POOL_W), const),
            pl.BlockSpec((D_MODEL, MG_W), const),
        ],
        out_specs=[pl.BlockSpec((TOK_TILE, w), stored) for w in widths],
        compiler_params=_params(("arbitrary",)),
        name="inproj",
    )(x_ctx, x_lat, mod3, norm1_g, w_main, w_alr, w_xp, w_mg)


def _gla_direction(q_ref, k_ref, v_ref, alr_ref, wa_ref, ba_ref, o_ref, st_ref, d, rev):
    rows = GLA_SEQS * GLA_CHUNK
    stack = lambda ref, cols: jnp.concatenate([ref[0, s, :, cols] for s in range(GLA_SEQS)], axis=0)
    alr = stack(alr_ref, slice(None))
    a = _dot3(alr, wa_ref[...]) + ba_ref[...]
    g = (jnp.minimum(a, 0.0) - jnp.log(1.0 + jnp.exp(-jnp.abs(a)))) * (1.0 / GLA_TAU)

    row = lax.broadcasted_iota(I32, (rows, rows), 0)
    col = lax.broadcasted_iota(I32, (rows, rows), 1)
    same = (row // GLA_CHUNK) == (col // GLA_CHUNK)
    tri = same & ((col >= row) if rev else (col <= row))
    tri_b = jnp.where(tri, 1.0, 0.0).astype(BF16)
    g_hi, g_lo = _split_bf16(g)
    bcum = _dot(tri_b, g_hi) + _dot(tri_b, g_lo)

    def per_seq_row(r):
        return jnp.concatenate(
            [jnp.broadcast_to(bcum[s * GLA_CHUNK + r:s * GLA_CHUNK + r + 1], (GLA_CHUNK, QK_W))
             for s in range(GLA_SEQS)], axis=0)

    r_last = 0 if rev else GLA_CHUNK - 1
    blast = per_seq_row(r_last)
    bmid = per_seq_row(GLA_CHUNK // 2)
    e_q = jnp.exp(bcum - bmid)
    e_k = jnp.exp(bmid - bcum)
    e_in = jnp.exp(bcum)
    e_out = jnp.exp(blast - bcum)
    q = stack(q_ref, slice(None)).astype(F32)
    k = stack(k_ref, slice(None)).astype(F32)

    wide = (rows, GLA_SEQS * GLA_DK)
    own = (lax.broadcasted_iota(I32, wide, 0) // GLA_CHUNK) == (lax.broadcasted_iota(I32, wide, 1) // GLA_DK)

    def block_diag(x):
        return jnp.where(own, jnp.concatenate([x] * GLA_SEQS, axis=1), 0.0).astype(BF16)

    for h in range(GLA_HEADS):
        ks = slice(h * GLA_DK, (h + 1) * GLA_DK)
        vs = slice(h * GLA_DV, (h + 1) * GLA_DV)
        qh = q[:, ks]
        kh = k[:, ks]
        vh = stack(v_ref, vs)
        att = lax.dot_general((qh * e_q[:, ks]).astype(BF16), (kh * e_k[:, ks]).astype(BF16),
                              NT_DIMS, preferred_element_type=F32)
        att = jnp.where(tri, att, 0.0).astype(BF16)
        st = st_ref[d, h]
        o_inter = lax.dot_general(block_diag(qh * e_in[:, ks]), st.astype(BF16), NT_DIMS,
                                  preferred_element_type=F32)
        o_h = o_inter + _dot(att, vh)
        for s in range(GLA_SEQS):
            o_ref[0, s, :, vs] = o_h[s * GLA_CHUNK:(s + 1) * GLA_CHUNK]
        upd = lax.dot_general(vh, block_diag(kh * e_out[:, ks]), TN_DIMS,
                              preferred_element_type=F32)
        e_last = jnp.concatenate(
            [jnp.exp(bcum[s * GLA_CHUNK + r_last:s * GLA_CHUNK + r_last + 1, ks])
             for s in range(GLA_SEQS)], axis=1)
        st_ref[d, h] = st * e_last + upd


def _gla_kernel(qf_ref, kf_ref, vf_ref, af_ref, qb_ref, kb_ref, vb_ref, ab_ref,
                waf_ref, baf_ref, wab_ref, bab_ref, s0f_ref, s0b_ref,
                of_ref, ob_ref, sf_ref, sb_ref, st_ref):
    i = pl.program_id(0)
    is_ctx = i < GLA_CTX_STEPS
    chunk = jnp.where(is_ctx, i % CTX_CHUNKS, i - GLA_CTX_STEPS)

    @pl.when(is_ctx & (chunk == 0))
    def _():
        st_ref[...] = jnp.zeros(st_ref.shape, F32)

    @pl.when(i == GLA_CTX_STEPS)
    def _():
        for s in range(GLA_SEQS):
            ls = slice(s * GLA_DK, (s + 1) * GLA_DK)
            for h in range(GLA_HEADS):
                st_ref[0, h, :, ls] = s0f_ref[s, h].T
                st_ref[1, h, :, ls] = s0b_ref[s, h].T

    _gla_direction(qf_ref, kf_ref, vf_ref, af_ref, waf_ref, baf_ref, of_ref, st_ref, 0, False)
    _gla_direction(qb_ref, kb_ref, vb_ref, ab_ref, wab_ref, bab_ref, ob_ref, st_ref, 1, True)

    @pl.when(is_ctx & (chunk == CTX_CHUNKS - 1))
    def _():
        for s in range(GLA_SEQS):
            ls = slice(s * GLA_DK, (s + 1) * GLA_DK)
            for h in range(GLA_HEADS):
                sf_ref[s, h] = st_ref[0, h, :, ls].T
                sb_ref[s, h] = st_ref[1, h, :, ls].T


def _gla_block(i, rev):
    is_ctx = i < GLA_CTX_STEPS
    group = i // CTX_CHUNKS
    c_ctx = i % CTX_CHUNKS
    c_lat = i - GLA_CTX_STEPS
    if rev:
        c_ctx = CTX_CHUNKS - 1 - c_ctx
        c_lat = LAT_CHUNKS - 1 - c_lat
    j = c_lat // CHUNKS_PER_TILE
    per_row = TILE_GRID // GLA_SEQS
    a = jnp.where(is_ctx, group // per_row, CTX_TILES // TILE_GRID + j // per_row)
    b = jnp.where(is_ctx, group % per_row, j % per_row)
    c = jnp.where(is_ctx, c_ctx, c_lat % CHUNKS_PER_TILE)
    return (a, b, c, 0)


def _gla_call(q, k, v, alr, wa_f, ba_f, wa_b, ba_b, s0_f, s0_b):
    def view(arr):
        return arr.reshape(TILE_GRID, TILE_GRID, TOK_TILE, arr.shape[-1])

    def spec(width, rev):
        return pl.BlockSpec((1, GLA_SEQS, GLA_CHUNK, width), lambda i: _gla_block(i, rev))

    const = lambda i: (0, 0)
    st_block = (GLA_SEQS, GLA_HEADS, GLA_DK, GLA_DV)
    whole_state = pl.BlockSpec(st_block, lambda i: (0, 0, 0, 0))
    ctx_state = pl.BlockSpec(
        st_block, lambda i: (jnp.minimum(i // CTX_CHUNKS, BATCH // GLA_SEQS - 1), 0, 0, 0))
    in_specs = []
    for rev in (False, True):
        in_specs += [spec(QK_W, rev), spec(QK_W, rev), spec(V_W, rev), spec(ALR_W, rev)]
    in_specs += [pl.BlockSpec((ALR_W, QK_W), const), pl.BlockSpec((1, QK_W), const)] * 2
    in_specs += [whole_state, whole_state]
    o_shape = jax.ShapeDtypeStruct((TILE_GRID, TILE_GRID, TOK_TILE, V_W), F32)
    s_shape = jax.ShapeDtypeStruct((BATCH, GLA_HEADS, GLA_DK, GLA_DV), F32)
    qv, kv, vv, av = view(q), view(k), view(v), view(alr)
    o_f, o_b, s_f, s_b = pl.pallas_call(
        _gla_kernel,
        out_shape=[o_shape, o_shape, s_shape, s_shape],
        grid=(GLA_CTX_STEPS + LAT_CHUNKS,),
        in_specs=in_specs,
        out_specs=[spec(V_W, False), spec(V_W, True), ctx_state, ctx_state],
        scratch_shapes=[pltpu.VMEM((2, GLA_HEADS, GLA_DV, GLA_SEQS * GLA_DK), F32)],
        compiler_params=_params(("arbitrary",)),
        name="gla",
    )(qv, kv, vv, av, qv, kv, vv, av, wa_f, ba_f, wa_b, ba_b, s0_f, s0_b)
    return o_f.reshape(N_TOK, V_W), o_b.reshape(N_TOK, V_W), s_f, s_b


def _band(n, w, block):
    row = lax.broadcasted_iota(I32, (n, n), 0)
    col = lax.broadcasted_iota(I32, (n, n), 1)
    inside = (col >= row - w // 2) & (col <= row + w // 2 - 1)
    if block < n:
        inside = inside & ((row // block) == (col // block))
    return jnp.where(inside, 1.0, 0.0).astype(BF16)


def _win_count(p, n, w):
    return jnp.minimum(p + w // 2 - 1, n - 1) - jnp.maximum(p - w // 2, 0) + 1


def _pool_ctx_kernel(x_ref, o_ref, band_ref):
    @pl.when(pl.program_id(0) == 0)
    def _():
        for gi, w in enumerate(POOL_WINDOWS):
            band_ref[gi] = _band(SEQ, w, SEQ)

    p = lax.broadcasted_iota(I32, (SEQ, POOL_GROUP_DIM), 0)
    for gi, w in enumerate(POOL_WINDOWS):
        cs = slice(gi * POOL_GROUP_DIM, (gi + 1) * POOL_GROUP_DIM)
        x = x_ref[:, cs]
        hi, lo = _split_bf16(x)
        band = band_ref[gi]
        s = _dot(band, hi) + _dot(band, lo)
        cnt = _win_count(p, SEQ, w).astype(F32)
        o_ref[:, cs] = s / cnt - x


def _pool_ctx_call(xp):
    spec = pl.BlockSpec((SEQ, POOL_W), lambda b: (b, 0))
    return pl.pallas_call(
        _pool_ctx_kernel,
        out_shape=jax.ShapeDtypeStruct((N_CTX, POOL_W), F32),
        grid=(BATCH,),
        in_specs=[spec],
        out_specs=spec,
        scratch_shapes=[pltpu.VMEM((POOL_GROUPS, SEQ, SEQ), BF16)],
        compiler_params=_params(("arbitrary",)),
        name="pool_ctx",
    )(xp)


POOL_HALO = (max(POOL_WINDOWS) // 2) * GRID_W


def _pool_lat_kernel(x_ref, o_ref, pad_ref):
    rows = DEC_SEQ // GRID_W
    p = lax.broadcasted_iota(I32, (DEC_SEQ, POOL_GROUP_DIM), 0)
    r = p // GRID_W
    cidx = p % GRID_W
    zeros = jnp.zeros((POOL_HALO, POOL_GROUP_DIM), F32)
    pad_ref[0:POOL_HALO, :] = zeros
    pad_ref[POOL_HALO + DEC_SEQ:2 * POOL_HALO + DEC_SEQ, :] = zeros
    for gi, w in enumerate(POOL_WINDOWS):
        cs = slice(gi * POOL_GROUP_DIM, (gi + 1) * POOL_GROUP_DIM)
        band = _band(TOK_TILE, w, GRID_W)
        for t in range(LAT_TILES_PER_SEQ):
            hi, lo = _split_bf16(x_ref[t, 0, :, cs])
            pad_ref[POOL_HALO + t * TOK_TILE:POOL_HALO + (t + 1) * TOK_TILE, :] = (
                _dot(band, hi) + _dot(band, lo))
        acc = jnp.zeros((DEC_SEQ, POOL_GROUP_DIM), F32)
        for dr in range(-(w // 2), w // 2):
            start = POOL_HALO + dr * GRID_W
            acc = acc + pad_ref[start:start + DEC_SEQ, :]
        cnt = (_win_count(r, rows, w) * _win_count(cidx, GRID_W, w)).astype(F32)
        pooled = acc / cnt
        for t in range(LAT_TILES_PER_SEQ):
            rs = slice(t * TOK_TILE, (t + 1) * TOK_TILE)
            o_ref[t, 0, :, cs] = pooled[rs] - x_ref[t, 0, :, cs]


def _pool_lat_call(xp):
    view = xp.reshape(N_TILES // DEC_BATCH, DEC_BATCH, TOK_TILE, POOL_W)
    blk = (LAT_TILES_PER_SEQ, 1, TOK_TILE, POOL_W)
    out = pl.pallas_call(
        _pool_lat_kernel,
        out_shape=jax.ShapeDtypeStruct((LAT_TILES_PER_SEQ, DEC_BATCH, TOK_TILE, POOL_W), F32),
        grid=(DEC_BATCH,),
        in_specs=[pl.BlockSpec(blk, lambda s: (CTX_TILES // DEC_BATCH // LAT_TILES_PER_SEQ, s, 0, 0))],
        out_specs=pl.BlockSpec(blk, lambda s: (0, s, 0, 0)),
        scratch_shapes=[pltpu.VMEM((DEC_SEQ + 2 * POOL_HALO, POOL_GROUP_DIM), F32)],
        compiler_params=_params(("arbitrary",)),
        name="pool_lat",
    )(view)
    return out.reshape(N_LAT, POOL_W)


def _post_kernel(xc_ref, xl_ref, mod_ref, of_ref, ob_ref, og_ref, pc_ref, pl_ref, mg_ref, gng_ref,
                 wpg_ref, psc_ref, wbg_ref, wbp_ref, wout_ref, n2g_ref, wr_ref, br_ref,
                 x1_ref, h2_ref, idx_ref, tw_ref):
    t = pl.program_id(0)
    is_ctx = t < CTX_TILES
    x = jnp.where(is_ctx, xc_ref[...], xl_ref[...])
    pooled = jnp.where(is_ctx, pc_ref[...], pl_ref[...])
    mod = mod_ref[0]
    gate1 = mod[:, 2 * D_MODEL:3 * D_MODEL]
    shift2 = mod[:, 3 * D_MODEL:4 * D_MODEL]
    scale2 = mod[:, 4 * D_MODEL:5 * D_MODEL]

    o = of_ref[...] + ob_ref[...]
    og = og_ref[...].astype(F32)
    gated = []
    for h in range(GLA_HEADS):
        vs = slice(h * GLA_DV, (h + 1) * GLA_DV)
        oh = _rms(o[:, vs]) * gng_ref[:, vs]
        gated.append((oh * og[:, vs]).astype(BF16))
    br_gla = _dot(jnp.concatenate(gated, axis=-1), wbg_ref[...])

    pm = []
    for gi in range(POOL_GROUPS):
        cs = slice(gi * POOL_GROUP_DIM, (gi + 1) * POOL_GROUP_DIM)
        pmg = _dot(pooled[:, cs].astype(BF16), wpg_ref[gi]) * psc_ref[:, cs]
        pm.append(pmg.astype(BF16))
    br_pool = _dot(jnp.concatenate(pm, axis=-1), wbp_ref[...])

    mg = mg_ref[...].astype(F32)
    merged = mg[:, 0:D_MODEL] * br_gla + mg[:, D_MODEL:MG_W] * br_pool
    m = _dot(merged.astype(BF16), wout_ref[...])
    x1 = x + gate1 * m
    x1_ref[...] = x1
    h2 = _rms(x1) * n2g_ref[...]
    h2 = h2 * (1.0 + scale2) + shift2
    h2_ref[...] = _pack_halves(h2[:, :HALF_W], h2[:, HALF_W:])

    logits = _dot3(h2, wr_ref[...]) + br_ref[...]
    lane = lax.broadcasted_iota(I32, (TOK_TILE, LANES), 1)
    lane_f = lane.astype(F32)
    neg = jnp.float32(-jnp.inf)
    cur = jnp.where(lane < N_EXPERTS, logits, neg)
    vals, idxs = [], []
    for _ in range(TOP_K):
        mx = jnp.max(cur, axis=-1, keepdims=True)
        ix = jnp.min(jnp.where(cur == mx, lane_f, float(LANES)), axis=-1, keepdims=True)
        vals.append(mx)
        idxs.append(ix)
        cur = jnp.where(lane_f == ix, neg, cur)
    ex = [jnp.exp(vv - vals[0]) for vv in vals]
    tot = ex[0] + ex[1] + ex[2] + ex[3]
    idx_out = jnp.zeros((TOK_TILE, LANES), F32)
    w_out = jnp.zeros((TOK_TILE, LANES), F32)
    for kk in range(TOP_K):
        idx_out = jnp.where(lane == kk, idxs[kk], idx_out)
        w_out = jnp.where(lane == kk, ex[kk] / tot, w_out)
    idx_ref[...] = idx_out.astype(I32)
    tw_ref[...] = w_out


def _post_call(x_ctx, x_lat, mod3, o_f, o_b, og, pooled_c, pooled_l, mg, gng, wpg, psc, wbg, wbp,
               wout, n2g, wr, br):
    row = lambda t: (t, 0)
    const = lambda t: (0, 0)
    stored = lambda t: (_store_tile(t), 0)
    return pl.pallas_call(
        _post_kernel,
        out_shape=[
            jax.ShapeDtypeStruct((N_TOK, D_MODEL), F32),
            jax.ShapeDtypeStruct((N_TOK, HALF_W), I32),
            jax.ShapeDtypeStruct((N_TOK, LANES), I32),
            jax.ShapeDtypeStruct((N_TOK, LANES), F32),
        ],
        grid=(N_TILES,),
        in_specs=[
            pl.BlockSpec((TOK_TILE, D_MODEL), lambda t: (_ctx_tile(t), 0)),
            pl.BlockSpec((TOK_TILE, D_MODEL), lambda t: (_lat_tile(t), 0)),
            pl.BlockSpec((1, 1, N_MOD * D_MODEL), lambda t: (_mod_row(t), 0, 0)),
            pl.BlockSpec((TOK_TILE, V_W), stored),
            pl.BlockSpec((TOK_TILE, V_W), stored),
            pl.BlockSpec((TOK_TILE, V_W), stored),
            pl.BlockSpec((TOK_TILE, POOL_W), lambda t: (_ctx_tile(t), 0)),
            pl.BlockSpec((TOK_TILE, POOL_W),
                         lambda t: (jnp.maximum(_store_tile(t) - CTX_TILES, 0), 0)),
            pl.BlockSpec((TOK_TILE, MG_W), stored),
            pl.BlockSpec((1, V_W), const),
            pl.BlockSpec((POOL_GROUPS, POOL_GROUP_DIM, POOL_GROUP_DIM), lambda t: (0, 0, 0)),
            pl.BlockSpec((1, POOL_W), const),
            pl.BlockSpec((V_W, D_MODEL), const),
            pl.BlockSpec((POOL_W, D_MODEL), const),
            pl.BlockSpec((D_MODEL, D_MODEL), const),
            pl.BlockSpec((1, D_MODEL), const),
            pl.BlockSpec((D_MODEL, LANES), const),
            pl.BlockSpec((1, LANES), const),
        ],
        out_specs=[
            pl.BlockSpec((TOK_TILE, D_MODEL), row),
            pl.BlockSpec((TOK_TILE, HALF_W), row),
            pl.BlockSpec((TOK_TILE, LANES), row),
            pl.BlockSpec((TOK_TILE, LANES), row),
        ],
        compiler_params=_params(("arbitrary",)),
        name="post",
    )(x_ctx, x_lat, mod3, o_f, o_b, og, pooled_c, pooled_l, mg, gng, wpg, psc, wbg, wbp, wout, n2g,
      wr, br)


def _route_kernel(idx_ref, rank_ref, cnt_ref, carry_ref, strict_ref):
    t = pl.program_id(0)

    @pl.when(t == 0)
    def _():
        carry_ref[...] = jnp.zeros((1, LANES), F32)
        row = lax.broadcasted_iota(I32, (ROUTE_TILE, ROUTE_TILE), 0)
        col = lax.broadcasted_iota(I32, (ROUTE_TILE, ROUTE_TILE), 1)
        strict_ref[...] = jnp.where(col < row, 1.0, 0.0).astype(BF16)

    idx = idx_ref[...]
    lane = lax.broadcasted_iota(I32, (ROUTE_TILE, LANES), 1)
    sel = [lane == idx[:, kk:kk + 1] for kk in range(TOP_K)]
    onehot = jnp.zeros((ROUTE_TILE, LANES), F32)
    for kk in range(TOP_K):
        onehot = onehot + jnp.where(sel[kk], 1.0, 0.0)
    before = _dot(strict_ref[...], onehot.astype(BF16)) + carry_ref[...]
    rank = jnp.zeros((ROUTE_TILE, LANES), F32)
    for kk in range(TOP_K):
        rk = jnp.sum(jnp.where(sel[kk], before, 0.0), axis=-1, keepdims=True)
        rank = jnp.where(lane == kk, rk, rank)
    rank_ref[...] = rank.astype(I32)
    carry_ref[...] = carry_ref[...] + jnp.sum(onehot, axis=0, keepdims=True)
    cnt_ref[...] = jnp.broadcast_to(carry_ref[...], (8, LANES))


def _route_call(idx):
    return pl.pallas_call(
        _route_kernel,
        out_shape=[
            jax.ShapeDtypeStruct((N_TOK, LANES), I32),
            jax.ShapeDtypeStruct((8, LANES), F32),
        ],
        grid=(N_TOK // ROUTE_TILE,),
        in_specs=[pl.BlockSpec((ROUTE_TILE, LANES), lambda t: (t, 0))],
        out_specs=[
            pl.BlockSpec((ROUTE_TILE, LANES), lambda t: (t, 0)),
            pl.BlockSpec((8, LANES), lambda t: (0, 0)),
        ],
        scratch_shapes=[pltpu.VMEM((1, LANES), F32), pltpu.VMEM((ROUTE_TILE, ROUTE_TILE), BF16)],
        compiler_params=_params(("arbitrary",)),
        name="route",
    )(idx)


def _moe_kernel(be_ref, nu_ref, ne_ref, par_ref, x_ref, wg_hbm, bg_ref, wu_hbm, bu_ref, wd_hbm,
                bd_ref, y_ref, wf_ref, wgu_ref, wdb_ref, sems):
    b = pl.program_id(0)
    n_used = nu_ref[0]
    e = be_ref[b]
    prev = be_ref[jnp.maximum(b - 1, 0)]
    live = b < n_used

    def weight_copies(expert, slot):
        return [pltpu.make_async_copy(w.at[expert], wf_ref.at[slot, i], sems.at[slot])
                for i, w in enumerate((wg_hbm, wu_hbm, wd_hbm))]

    @pl.when(live & ((b == 0) | (e != prev)))
    def _():
        slot = par_ref[b]

        @pl.when(b == 0)
        def _():
            for cp in weight_copies(e, slot):
                cp.start()

        for cp in weight_copies(e, slot):
            cp.wait()
        nxt = ne_ref[b]

        @pl.when(nxt >= 0)
        def _():
            for cp in weight_copies(nxt, 1 - slot):
                cp.start()

        wgu_ref[:, :D_FF] = wf_ref[slot, 0].astype(BF16)
        wgu_ref[:, D_FF:] = wf_ref[slot, 1].astype(BF16)
        wdb_ref[...] = wf_ref[slot, 2].astype(BF16)

    @pl.when(live)
    def _():
        x_lo, x_hi = _unpack_halves(x_ref[...])
        x = jnp.concatenate([x_lo.astype(BF16), x_hi.astype(BF16)], axis=-1)
        gu = _dot(x, wgu_ref[...])
        gate = jnp.minimum(gu[:, :D_FF] + bg_ref[0], SWIGLU_LIMIT)
        up = jnp.clip(gu[:, D_FF:] + bu_ref[0], -SWIGLU_LIMIT, SWIGLU_LIMIT)
        act = (up + 1.0) * (gate * _sigmoid(SWIGLU_ALPHA * gate))
        y = _dot(act.astype(BF16), wdb_ref[...]) + bd_ref[0]
        y_ref[...] = _pack_halves(y[:, :HALF_W], y[:, HALF_W:])

    @pl.when(jnp.logical_not(live))
    def _():
        y_ref[...] = jnp.zeros((MOE_BLOCK, HALF_W), I32)


def _moe_call(block_e, n_used, next_e, parity, hs, w_gate, b_gate, w_up, b_up, w_down, b_down):
    def blk(b, be, nu, ne, par):
        return jnp.minimum(b, nu[0] - 1)

    row = lambda b, be, nu, ne, par: (blk(b, be, nu, ne, par), 0)
    bsel = lambda b, be, nu, ne, par: (be[blk(b, be, nu, ne, par)], 0, 0)
    any_spec = pl.BlockSpec(memory_space=pl.ANY)
    assert D_MODEL == D_FF
    return pl.pallas_call(
        _moe_kernel,
        out_shape=jax.ShapeDtypeStruct((N_SLOTS, HALF_W), I32),
        grid_spec=pltpu.PrefetchScalarGridSpec(
            num_scalar_prefetch=4,
            grid=(N_SLOT_BLOCKS,),
            in_specs=[
                pl.BlockSpec((MOE_BLOCK, HALF_W), row),
                any_spec,
                pl.BlockSpec((1, 1, D_FF), bsel),
                any_spec,
                pl.BlockSpec((1, 1, D_FF), bsel),
                any_spec,
                pl.BlockSpec((1, 1, D_MODEL), bsel),
            ],
            out_specs=pl.BlockSpec((MOE_BLOCK, HALF_W), lambda b, be, nu, ne, par: (b, 0)),
            scratch_shapes=[
                pltpu.VMEM((2, 3, D_MODEL, D_FF), F32),
                pltpu.VMEM((D_MODEL, 2 * D_FF), BF16),
                pltpu.VMEM((D_FF, D_MODEL), BF16),
                pltpu.SemaphoreType.DMA((2,)),
            ],
        ),
        compiler_params=_params(("arbitrary",)),
        name="moe",
    )(block_e, n_used, next_e, parity, hs, w_gate, b_gate, w_up, b_up, w_down, b_down)


SC_CORES = 2
SC_SUBCORES = 16
SC_WORKERS = SC_CORES * SC_SUBCORES
SC_ROWS = 128
COMBINE_CHUNKS = 2
COMBINE_TILE = 2 * TOK_TILE
assert N_CTX == N_LAT


def _sc_gather_rows(table, idx):
    n_idx = idx.shape[0]
    width = table.shape[1]
    rows = SC_ROWS // 2
    per_worker = n_idx // SC_WORKERS
    n_chunks = per_worker // rows
    assert n_chunks * rows * SC_WORKERS == n_idx and n_chunks >= 2
    mesh = plsc.VectorSubcoreMesh(core_axis_name="c", subcore_axis_name="s")

    @functools.partial(
        pl.kernel, mesh=mesh,
        out_type=jax.ShapeDtypeStruct((n_idx, width), table.dtype),
        scratch_types=[pltpu.VMEM((2, rows), I32), pltpu.VMEM((2, rows, width), table.dtype),
                       pltpu.SemaphoreType.DMA((2,)), pltpu.SemaphoreType.DMA((2,))],
        name="sc_gather",
    )
    def gather(table_hbm, idx_hbm, out_hbm, idx_v, rows_v, sem_g, sem_w):
        worker = lax.axis_index("s") * SC_CORES + lax.axis_index("c")
        base = worker * per_worker

        def chunk_rows(ch):
            return pl.ds(pl.multiple_of(base + ch * rows, rows), rows)

        def start_gather(ch, b):
            pltpu.sync_copy(idx_hbm.at[chunk_rows(ch)], idx_v.at[b])
            pltpu.async_copy(table_hbm.at[idx_v.at[b]], rows_v.at[b], sem_g.at[b])

        def wait_gather(b):
            pltpu.make_async_copy(table_hbm.at[pl.ds(0, rows)], rows_v.at[b], sem_g.at[b]).wait()

        def write_copy(ch, b):
            return pltpu.make_async_copy(rows_v.at[b], out_hbm.at[chunk_rows(ch)], sem_w.at[b])

        start_gather(0, 0)
        for ch in range(n_chunks):
            b = ch % 2
            if ch + 1 < n_chunks:
                if ch >= 1:
                    write_copy(ch - 1, 1 - b).wait()
                start_gather(ch + 1, 1 - b)
            wait_gather(b)
            write_copy(ch, b).start()
        write_copy(n_chunks - 2, n_chunks % 2).wait()
        write_copy(n_chunks - 1, (n_chunks - 1) % 2).wait()

    return gather(table, idx)


def _sc_scatter_rows(rows, idx3, n_out):
    n_rows, width = rows.shape
    n_chunks = n_rows // SC_ROWS // SC_WORKERS
    assert n_chunks * SC_ROWS * SC_WORKERS == n_rows and idx3.shape == (n_rows // SC_ROWS, TOP_K, SC_ROWS)
    mesh = plsc.VectorSubcoreMesh(core_axis_name="c", subcore_axis_name="s")

    @functools.partial(
        pl.kernel, mesh=mesh,
        out_type=jax.ShapeDtypeStruct((n_out, width), rows.dtype),
        scratch_types=[pltpu.VMEM((TOP_K, SC_ROWS), I32), pltpu.VMEM((SC_ROWS, width), rows.dtype),
                       pltpu.SemaphoreType.DMA],
        name="sc_scatter",
    )
    def scatter(rows_hbm, idx_hbm, out_hbm, idx_v, rows_v, sem):
        worker = lax.axis_index("s") * SC_CORES + lax.axis_index("c")

        @pl.loop(0, n_chunks)
        def _(ch):
            chunk = worker * n_chunks + ch
            pltpu.sync_copy(idx_hbm.at[chunk], idx_v)
            pltpu.sync_copy(rows_hbm.at[pl.ds(pl.multiple_of(chunk * SC_ROWS, SC_ROWS), SC_ROWS)],
                            rows_v)
            for kk in range(TOP_K):
                pltpu.async_copy(rows_v, out_hbm.at[idx_v.at[kk]], sem).wait()

    return scatter(rows, idx3)


def _combine_kernel(x1_ref, mod_ref, tw_ref, fg_ref, g_ref, *rest):
    out_ref = rest[-1]
    tw = tw_ref[...]
    f_lo = jnp.zeros((COMBINE_TILE, HALF_W), F32)
    f_hi = jnp.zeros((COMBINE_TILE, HALF_W), F32)
    for kk in range(TOP_K):
        lo, hi = _unpack_halves(g_ref[kk])
        f_lo = f_lo + lo * tw[:, kk:kk + 1]
        f_hi = f_hi + hi * tw[:, kk:kk + 1]
    gate2 = mod_ref[0][:, 5 * D_MODEL:6 * D_MODEL]
    x2 = x1_ref[...] + gate2 * jnp.concatenate([f_lo, f_hi], axis=-1)
    out_ref[...] = _rms(x2) * fg_ref[...]


def _combine_call(x1, mod3, tw, final_g, gathered, partial, tile0, out_tile0, out_tiles, name):
    n_tiles = gathered.shape[1] // COMBINE_TILE
    in_specs = [
        pl.BlockSpec((COMBINE_TILE, D_MODEL), lambda t: (tile0 + t, 0)),
        pl.BlockSpec((1, 1, N_MOD * D_MODEL), lambda t: (_mod_row((tile0 + t) * (COMBINE_TILE // TOK_TILE)), 0, 0)),
        pl.BlockSpec((COMBINE_TILE, LANES), lambda t: (tile0 + t, 0)),
        pl.BlockSpec((1, D_MODEL), lambda t: (0, 0)),
        pl.BlockSpec((TOP_K, COMBINE_TILE, HALF_W), lambda t: (0, t, 0)),
    ]
    args = [x1, mod3, tw, final_g, gathered]
    aliases = {}
    if partial is not None:
        in_specs.append(pl.BlockSpec(memory_space=pl.ANY))
        args.append(partial)
        aliases = {len(args) - 1: 0}
    return pl.pallas_call(
        _combine_kernel,
        out_shape=jax.ShapeDtypeStruct((out_tiles * COMBINE_TILE, D_MODEL), F32),
        grid=(n_tiles,),
        in_specs=in_specs,
        out_specs=pl.BlockSpec((COMBINE_TILE, D_MODEL), lambda t: (out_tile0 + t, 0)),
        input_output_aliases=aliases,
        compiler_params=_params(("arbitrary",)),
        name=name,
    )(*args)


def kernel(x_prompt, x_sample, state_gla_fwd, state_gla_bwd, c, c_ctx, norm1_g, w_mod, b_mod, w_in,
           w_alpha, b_alpha, gla_norm_g, w_pool_grp, pool_scale, w_branch_gla, w_branch_pool, w_out,
           norm2_g, w_router, b_router, w_gate, b_gate, w_up, b_up, w_down, b_down, final_norm_g):
    l = 0
    x_ctx = x_prompt.reshape(N_CTX, D_MODEL)
    x_lat = x_sample.reshape(N_LAT, D_MODEL)

    cvec = jnp.concatenate([c_ctx[None, :], c, jnp.zeros((8 - 1 - DEC_BATCH, D_MODEL), F32)], axis=0)
    mod = _mod_call(cvec, w_mod[l], b_mod[l][None, :])
    mod3 = mod.reshape(8, 1, N_MOD * D_MODEL)

    w_in_b = w_in[l].astype(BF16)
    w_main = w_in_b[:, :MAIN_W]
    w_alr = w_in_b[:, MAIN_W:MAIN_W + ALR_W]
    w_xp = w_in_b[:, MAIN_W + ALR_W:MAIN_W + ALR_W + POOL_W]
    w_mg = w_in_b[:, MAIN_W + ALR_W + POOL_W:]
    q, k, v, og, alr, xp, mg = _inproj_call(x_ctx, x_lat, mod3, norm1_g[l][None, :],
                                            w_main, w_alr, w_xp, w_mg)

    zpad = jnp.zeros((GLA_LOWRANK, QK_W), F32)
    wa_f = jnp.concatenate([w_alpha[l, 0], zpad], axis=0)
    wa_b = jnp.concatenate([zpad, w_alpha[l, 1]], axis=0)
    o_f, o_b, s_f, s_b = _gla_call(q, k, v, alr, wa_f, b_alpha[l, 0][None, :], wa_b,
                                   b_alpha[l, 1][None, :], state_gla_fwd[:, l], state_gla_bwd[:, l])

    pooled_c = _pool_ctx_call(xp)
    pooled_l = _pool_lat_call(xp)

    w_router_pad = jnp.pad(w_router[l], ((0, 0), (0, LANES - N_EXPERTS)))
    b_router_pad = jnp.pad(b_router[l], (0, LANES - N_EXPERTS))[None, :]
    x1, h2, top_idx, top_w = _post_call(
        x_ctx, x_lat, mod3, o_f, o_b, og, pooled_c, pooled_l, mg,
        gla_norm_g[l].reshape(1, V_W), w_pool_grp[l].astype(BF16), pool_scale[l][None, :],
        w_branch_gla[l].astype(BF16), w_branch_pool[l].astype(BF16), w_out[l].astype(BF16),
        norm2_g[l][None, :], w_router_pad, b_router_pad)

    rank, cnt = _route_call(top_idx)
    counts = cnt[0, :N_EXPERTS].astype(I32)
    padded = (counts + MOE_BLOCK - 1) // MOE_BLOCK * MOE_BLOCK
    pad_end = jnp.cumsum(padded).astype(I32)
    pad_start = pad_end - padded
    block_first = jnp.arange(N_SLOT_BLOCKS, dtype=I32) * MOE_BLOCK
    block_e = jnp.minimum(jnp.sum((pad_end[None, :] <= block_first[:, None]).astype(I32), axis=1),
                          N_EXPERTS - 1).astype(I32)
    n_used = (pad_end[-1:] // MOE_BLOCK).astype(I32)
    run_start = jnp.concatenate([jnp.ones((1,), I32), (block_e[1:] != block_e[:-1]).astype(I32)])
    parity = ((jnp.cumsum(run_start) - 1) % 2).astype(I32)
    after = pad_end[block_e] // MOE_BLOCK
    next_e = jnp.where(after < n_used[0], block_e[jnp.minimum(after, N_SLOT_BLOCKS - 1)], -1).astype(I32)
    experts = jnp.arange(N_EXPERTS, dtype=I32)[:, None, None]
    tk = top_idx[:, :TOP_K].T
    first_slot = jnp.sum(jnp.where(tk[None] == experts, pad_start[:, None, None], 0), axis=0)
    pos_by_choice = (first_slot + rank[:, :TOP_K].T).astype(I32)
    pos_chunks = pos_by_choice.reshape(TOP_K, N_TOK // SC_ROWS, SC_ROWS).transpose(1, 0, 2)

    hs = _sc_scatter_rows(h2, pos_chunks, N_SLOTS)
    y = _moe_call(block_e, n_used, next_e, parity, hs,
                  w_gate[l], b_gate[l][:, None, :], w_up[l], b_up[l][:, None, :],
                  w_down[l], b_down[l][:, None, :])
    outs = []
    chunk_tok = N_CTX // COMBINE_CHUNKS
    for group, tok0 in (("ctx", 0), ("lat", N_CTX)):
        out = None
        for ci in range(COMBINE_CHUNKS):
            t0 = tok0 + ci * chunk_tok
            idx = pos_by_choice[:, t0:t0 + chunk_tok].reshape(TOP_K * chunk_tok)
            gathered = _sc_gather_rows(y, idx).reshape(TOP_K, chunk_tok, HALF_W)
            out = _combine_call(x1, mod3, top_w, final_norm_g[None, :], gathered, out,
                                t0 // COMBINE_TILE, ci * chunk_tok // COMBINE_TILE, N_CTX // COMBINE_TILE,
                                "combine_%s%d" % (group, ci))
        outs.append(out)
    y_prompt = outs[0].reshape(BATCH, SEQ, D_MODEL)
    y_sample = outs[1].reshape(DEC_BATCH, DEC_SEQ, D_MODEL)
    return (y_prompt, y_sample, s_f[:, None], s_b[:, None])
```

```python
import functools

import jax
import jax.numpy as jnp
from jax import lax
from jax.experimental import pallas as pl
from jax.experimental.pallas import tpu as pltpu
from jax.experimental.pallas import tpu_sc as plsc

F32 = jnp.float32
BF16 = jnp.bfloat16
I32 = jnp.int32

D_MODEL = 1024
BATCH = 32
SEQ = 256
DEC_BATCH = 4
DEC_SEQ = 2048
GRID_W = 64
GLA_HEADS = 4
GLA_DK = 128
GLA_DV = 256
GLA_LOWRANK = 16
GLA_TAU = 16.0
GLA_CHUNK = 64
POOL_GROUPS = 4
POOL_GROUP_DIM = 128
POOL_WINDOWS = (2, 4, 8, 16)
N_EXPERTS = 32
TOP_K = 4
D_FF = 1024
SWIGLU_LIMIT = 7.0
SWIGLU_ALPHA = 1.702
MOE_BLOCK = 256
NORM_EPS = 1e-6
N_MOD = 6

QK_W = GLA_HEADS * GLA_DK
V_W = GLA_HEADS * GLA_DV
POOL_W = POOL_GROUPS * POOL_GROUP_DIM
MAIN_W = 2 * QK_W + 2 * V_W
ALR_W = 2 * GLA_LOWRANK
MG_W = 2 * D_MODEL

N_CTX = BATCH * SEQ
N_LAT = DEC_BATCH * DEC_SEQ
N_TOK = N_CTX + N_LAT
N_SLOT_BLOCKS = -(-(N_TOK * TOP_K + N_EXPERTS * (MOE_BLOCK - 1)) // MOE_BLOCK)
N_SLOTS = N_SLOT_BLOCKS * MOE_BLOCK

LANES = 128
TOK_TILE = 256
N_TILES = N_TOK // TOK_TILE
CTX_TILES = N_CTX // TOK_TILE
LAT_TILES_PER_SEQ = DEC_SEQ // TOK_TILE
ROUTE_TILE = 512
VMEM_LIMIT = 56 * 1024 * 1024

GLA_SEQS = 4
CTX_CHUNKS = SEQ // GLA_CHUNK
LAT_CHUNKS = DEC_SEQ // GLA_CHUNK
CHUNKS_PER_TILE = TOK_TILE // GLA_CHUNK
GLA_CTX_STEPS = (BATCH // GLA_SEQS) * CTX_CHUNKS
TILE_GRID = 8

NT_DIMS = (((1,), (1,)), ((), ()))
TN_DIMS = (((0,), (0,)), ((), ()))

assert DEC_BATCH == GLA_SEQS and SEQ == TOK_TILE and N_TILES == TILE_GRID * TILE_GRID


def _params(semantics, vmem=VMEM_LIMIT):
    return pltpu.CompilerParams(dimension_semantics=semantics, vmem_limit_bytes=vmem)


def _split_bf16(a):
    hi = a.astype(BF16)
    lo = (a - hi.astype(F32)).astype(BF16)
    return hi, lo


def _dot(a, b):
    return jnp.dot(a, b, preferred_element_type=F32)


def _dot3(a, b):
    a_hi, a_lo = _split_bf16(a)
    b_hi, b_lo = _split_bf16(b)
    return _dot(a_hi, b_hi) + _dot(a_lo, b_hi) + _dot(a_hi, b_lo)


def _dot3_short(a, b):
    a_hi, a_lo = _split_bf16(a)
    b_hi, b_lo = _split_bf16(b)
    return _dot(jnp.concatenate([a_hi, a_lo, a_hi], axis=1), jnp.concatenate([b_hi, b_hi, b_lo], axis=0))


def _sigmoid(x):
    return 1.0 / (1.0 + jnp.exp(-x))


HALF_W = D_MODEL // 2
HIGH_HALF_MASK = -65536


def _pack_halves(lo, hi):
    lo_bits = pltpu.bitcast(lo.astype(BF16).astype(F32), I32)
    hi_bits = pltpu.bitcast(hi.astype(BF16).astype(F32), I32)
    return lax.shift_right_logical(lo_bits, 16) | (hi_bits & HIGH_HALF_MASK)


def _unpack_halves(words):
    lo = pltpu.bitcast(lax.shift_left(words, 16), F32)
    hi = pltpu.bitcast(words & HIGH_HALF_MASK, F32)
    return lo, hi


def _rms(x):
    return x * lax.rsqrt(jnp.mean(x * x, axis=-1, keepdims=True) + NORM_EPS)


def _mod_row(t):
    return jnp.where(t < CTX_TILES, 0, 1 + (t - CTX_TILES) // LAT_TILES_PER_SEQ)


def _store_tile(t):
    u = t - CTX_TILES
    return jnp.where(t < CTX_TILES, t,
                     CTX_TILES + DEC_BATCH * (u % LAT_TILES_PER_SEQ) + u // LAT_TILES_PER_SEQ)


def _ctx_tile(t):
    return jnp.minimum(t, CTX_TILES - 1)


def _lat_tile(t):
    return jnp.maximum(t - CTX_TILES, 0)


def _mod_kernel(c_ref, w_ref, b_ref, o_ref):
    c = c_ref[...]
    o_ref[...] = _dot3(c * _sigmoid(c), w_ref[...]) + b_ref[...]


def _mod_call(cvec, w_mod, b_mod):
    rows = cvec.shape[0]
    return pl.pallas_call(
        _mod_kernel,
        out_shape=jax.ShapeDtypeStruct((rows, N_MOD * D_MODEL), F32),
        grid=(N_MOD,),
        in_specs=[
            pl.BlockSpec((rows, D_MODEL), lambda j: (0, 0)),
            pl.BlockSpec((D_MODEL, D_MODEL), lambda j: (0, j)),
            pl.BlockSpec((1, D_MODEL), lambda j: (0, j)),
        ],
        out_specs=pl.BlockSpec((rows, D_MODEL), lambda j: (0, j)),
        compiler_params=_params(("arbitrary",)),
        name="mod",
    )(cvec, w_mod, b_mod)


def _inproj_kernel(xc_ref, xl_ref, mod_ref, g_ref, wmain_ref, walr_ref, wxp_ref, wmg_ref,
                   q_ref, k_ref, v_ref, og_ref, alr_ref, xp_ref, mg_ref):
    t = pl.program_id(0)
    x = jnp.where(t < CTX_TILES, xc_ref[...], xl_ref[...])
    mod = mod_ref[0]
    shift1 = mod[:, 0:D_MODEL]
    scale1 = mod[:, D_MODEL:2 * D_MODEL]
    h = _rms(x) * g_ref[...]
    h = (h * (1.0 + scale1) + shift1).astype(BF16)
    z = _dot(h, wmain_ref[...])
    q_ref[...] = (z[:, 0:QK_W] * (GLA_DK ** -0.5)).astype(BF16)
    k_ref[...] = z[:, QK_W:2 * QK_W].astype(BF16)
    v_ref[...] = z[:, 2 * QK_W:2 * QK_W + V_W].astype(BF16)
    og = z[:, 2 * QK_W + V_W:MAIN_W]
    og_ref[...] = (og * _sigmoid(og)).astype(BF16)
    alr_ref[...] = _dot(h, walr_ref[...])
    xp_ref[...] = _dot(h, wxp_ref[...])
    mg_ref[...] = _sigmoid(_dot(h, wmg_ref[...])).astype(BF16)


def _inproj_call(x_ctx, x_lat, mod3, norm1_g, w_main, w_alr, w_xp, w_mg):
    const = lambda t: (0, 0)
    stored = lambda t: (_store_tile(t), 0)
    widths = (QK_W, QK_W, V_W, V_W, ALR_W, POOL_W, MG_W)
    dtypes = (BF16, BF16, BF16, BF16, F32, F32, BF16)
    return pl.pallas_call(
        _inproj_kernel,
        out_shape=[jax.ShapeDtypeStruct((N_TOK, w), dt) for w, dt in zip(widths, dtypes)],
        grid=(N_TILES,),
        in_specs=[
            pl.BlockSpec((TOK_TILE, D_MODEL), lambda t: (_ctx_tile(t), 0)),
            pl.BlockSpec((TOK_TILE, D_MODEL), lambda t: (_lat_tile(t), 0)),
            pl.BlockSpec((1, 1, N_MOD * D_MODEL), lambda t: (_mod_row(t), 0, 0)),
            pl.BlockSpec((1, D_MODEL), const),
            pl.BlockSpec((D_MODEL, MAIN_W), const),
            pl.BlockSpec((D_MODEL, ALR_W), const),
            pl.BlockSpec((D_MODEL, POOL_W), const),
            pl.BlockSpec((D_MODEL, MG_W), const),
        ],
        out_specs=[pl.BlockSpec((TOK_TILE, w), stored) for w in widths],
        compiler_params=_params(("arbitrary",)),
        name="inproj",
    )(x_ctx, x_lat, mod3, norm1_g, w_main, w_alr, w_xp, w_mg)


def _gla_direction(q_ref, k_ref, v_ref, alr_ref, wa_ref, ba_ref, o_ref, st_ref, d, rev):
    rows = GLA_SEQS * GLA_CHUNK
    stack = lambda ref, cols: jnp.concatenate([ref[0, s, :, cols] for s in range(GLA_SEQS)], axis=0)
    alr = stack(alr_ref, slice(None))
    a = _dot3_short(alr, wa_ref[...]) + ba_ref[...]
    g = (jnp.minimum(a, 0.0) - jnp.log(1.0 + jnp.exp(-jnp.abs(a)))) * (1.0 / GLA_TAU)

    row = lax.broadcasted_iota(I32, (rows, rows), 0)
    col = lax.broadcasted_iota(I32, (rows, rows), 1)
    same = (row // GLA_CHUNK) == (col // GLA_CHUNK)
    tri = same & ((col >= row) if rev else (col <= row))
    tri_b = jnp.where(tri, 1.0, 0.0).astype(BF16)
    g_hi, g_lo = _split_bf16(g)
    bcum = _dot(tri_b, g_hi) + _dot(tri_b, g_lo)

    def per_seq_row(r):
        return jnp.concatenate(
            [jnp.broadcast_to(bcum[s * GLA_CHUNK + r:s * GLA_CHUNK + r + 1], (GLA_CHUNK, QK_W))
             for s in range(GLA_SEQS)], axis=0)

    r_last = 0 if rev else GLA_CHUNK - 1
    blast = per_seq_row(r_last)
    bmid = per_seq_row(GLA_CHUNK // 2)
    e_q = jnp.exp(bcum - bmid)
    e_k = jnp.exp(bmid - bcum)
    e_in = jnp.exp(bcum)
    e_out = jnp.exp(blast - bcum)
    q = stack(q_ref, slice(None)).astype(F32)
    k = stack(k_ref, slice(None)).astype(F32)

    wide = (rows, GLA_SEQS * GLA_DK)
    own = (lax.broadcasted_iota(I32, wide, 0) // GLA_CHUNK) == (lax.broadcasted_iota(I32, wide, 1) // GLA_DK)

    def block_diag(x):
        return jnp.where(own, jnp.concatenate([x] * GLA_SEQS, axis=1), 0.0).astype(BF16)

    for h in range(GLA_HEADS):
        ks = slice(h * GLA_DK, (h + 1) * GLA_DK)
        vs = slice(h * GLA_DV, (h + 1) * GLA_DV)
        qh = q[:, ks]
        kh = k[:, ks]
        vh = stack(v_ref, vs)
        att = lax.dot_general((qh * e_q[:, ks]).astype(BF16), (kh * e_k[:, ks]).astype(BF16),
                              NT_DIMS, preferred_element_type=F32)
        att = jnp.where(tri, att, 0.0).astype(BF16)
        st = st_ref[d, h]
        o_inter = lax.dot_general(block_diag(qh * e_in[:, ks]), st.astype(BF16), NT_DIMS,
                                  preferred_element_type=F32)
        o_h = o_inter + _dot(att, vh)
        for s in range(GLA_SEQS):
            o_ref[0, s, :, vs] = o_h[s * GLA_CHUNK:(s + 1) * GLA_CHUNK]
        upd = lax.dot_general(vh, block_diag(kh * e_out[:, ks]), TN_DIMS,
                              preferred_element_type=F32)
        e_last = jnp.concatenate(
            [jnp.exp(bcum[s * GLA_CHUNK + r_last:s * GLA_CHUNK + r_last + 1, ks])
             for s in range(GLA_SEQS)], axis=1)
        st_ref[d, h] = st * e_last + upd


def _gla_kernel(qf_ref, kf_ref, vf_ref, af_ref, qb_ref, kb_ref, vb_ref, ab_ref,
                waf_ref, baf_ref, wab_ref, bab_ref, s0f_ref, s0b_ref,
                of_ref, ob_ref, sf_ref, sb_ref, st_ref):
    i = pl.program_id(0)
    is_ctx = i < GLA_CTX_STEPS
    chunk = jnp.where(is_ctx, i % CTX_CHUNKS, i - GLA_CTX_STEPS)

    @pl.when(is_ctx & (chunk == 0))
    def _():
        st_ref[...] = jnp.zeros(st_ref.shape, F32)

    @pl.when(i == GLA_CTX_STEPS)
    def _():
        for s in range(GLA_SEQS):
            ls = slice(s * GLA_DK, (s + 1) * GLA_DK)
            for h in range(GLA_HEADS):
                st_ref[0, h, :, ls] = s0f_ref[s, h].T
                st_ref[1, h, :, ls] = s0b_ref[s, h].T

    _gla_direction(qf_ref, kf_ref, vf_ref, af_ref, waf_ref, baf_ref, of_ref, st_ref, 0, False)
    _gla_direction(qb_ref, kb_ref, vb_ref, ab_ref, wab_ref, bab_ref, ob_ref, st_ref, 1, True)

    @pl.when(is_ctx & (chunk == CTX_CHUNKS - 1))
    def _():
        for s in range(GLA_SEQS):
            ls = slice(s * GLA_DK, (s + 1) * GLA_DK)
            for h in range(GLA_HEADS):
                sf_ref[s, h] = st_ref[0, h, :, ls].T
                sb_ref[s, h] = st_ref[1, h, :, ls].T


def _gla_block(i, rev):
    is_ctx = i < GLA_CTX_STEPS
    group = i // CTX_CHUNKS
    c_ctx = i % CTX_CHUNKS
    c_lat = i - GLA_CTX_STEPS
    if rev:
        c_ctx = CTX_CHUNKS - 1 - c_ctx
        c_lat = LAT_CHUNKS - 1 - c_lat
    j = c_lat // CHUNKS_PER_TILE
    per_row = TILE_GRID // GLA_SEQS
    a = jnp.where(is_ctx, group // per_row, CTX_TILES // TILE_GRID + j // per_row)
    b = jnp.where(is_ctx, group % per_row, j % per_row)
    c = jnp.where(is_ctx, c_ctx, c_lat % CHUNKS_PER_TILE)
    return (a, b, c, 0)


def _gla_call(q, k, v, alr, wa_f, ba_f, wa_b, ba_b, s0_f, s0_b):
    def view(arr):
        return arr.reshape(TILE_GRID, TILE_GRID, TOK_TILE, arr.shape[-1])

    def spec(width, rev):
        return pl.BlockSpec((1, GLA_SEQS, GLA_CHUNK, width), lambda i: _gla_block(i, rev))

    const = lambda i: (0, 0)
    st_block = (GLA_SEQS, GLA_HEADS, GLA_DK, GLA_DV)
    whole_state = pl.BlockSpec(st_block, lambda i: (0, 0, 0, 0))
    ctx_state = pl.BlockSpec(
        st_block, lambda i: (jnp.minimum(i // CTX_CHUNKS, BATCH // GLA_SEQS - 1), 0, 0, 0))
    in_specs = []
    for rev in (False, True):
        in_specs += [spec(QK_W, rev), spec(QK_W, rev), spec(V_W, rev), spec(ALR_W, rev)]
    in_specs += [pl.BlockSpec((ALR_W, QK_W), const), pl.BlockSpec((1, QK_W), const)] * 2
    in_specs += [whole_state, whole_state]
    o_shape = jax.ShapeDtypeStruct((TILE_GRID, TILE_GRID, TOK_TILE, V_W), F32)
    s_shape = jax.ShapeDtypeStruct((BATCH, GLA_HEADS, GLA_DK, GLA_DV), F32)
    qv, kv, vv, av = view(q), view(k), view(v), view(alr)
    o_f, o_b, s_f, s_b = pl.pallas_call(
        _gla_kernel,
        out_shape=[o_shape, o_shape, s_shape, s_shape],
        grid=(GLA_CTX_STEPS + LAT_CHUNKS,),
        in_specs=in_specs,
        out_specs=[spec(V_W, False), spec(V_W, True), ctx_state, ctx_state],
        scratch_shapes=[pltpu.VMEM((2, GLA_HEADS, GLA_DV, GLA_SEQS * GLA_DK), F32)],
        compiler_params=_params(("arbitrary",)),
        name="gla",
    )(qv, kv, vv, av, qv, kv, vv, av, wa_f, ba_f, wa_b, ba_b, s0_f, s0_b)
    return o_f.reshape(N_TOK, V_W), o_b.reshape(N_TOK, V_W), s_f, s_b


def _band(n, w, block):
    row = lax.broadcasted_iota(I32, (n, n), 0)
    col = lax.broadcasted_iota(I32, (n, n), 1)
    inside = (col >= row - w // 2) & (col <= row + w // 2 - 1)
    if block < n:
        inside = inside & ((row // block) == (col // block))
    return jnp.where(inside, 1.0, 0.0).astype(BF16)


def _win_count(p, n, w):
    return jnp.minimum(p + w // 2 - 1, n - 1) - jnp.maximum(p - w // 2, 0) + 1


def _pool_ctx_kernel(x_ref, o_ref, band_ref):
    @pl.when(pl.program_id(0) == 0)
    def _():
        for gi, w in enumerate(POOL_WINDOWS):
            band_ref[gi] = _band(SEQ, w, SEQ)

    p = lax.broadcasted_iota(I32, (SEQ, POOL_GROUP_DIM), 0)
    for gi, w in enumerate(POOL_WINDOWS):
        cs = slice(gi * POOL_GROUP_DIM, (gi + 1) * POOL_GROUP_DIM)
        x = x_ref[:, cs]
        hi, lo = _split_bf16(x)
        band = band_ref[gi]
        s = _dot(band, hi) + _dot(band, lo)
        cnt = _win_count(p, SEQ, w).astype(F32)
        o_ref[:, cs] = s / cnt - x


def _pool_ctx_call(xp):
    spec = pl.BlockSpec((SEQ, POOL_W), lambda b: (b, 0))
    return pl.pallas_call(
        _pool_ctx_kernel,
        out_shape=jax.ShapeDtypeStruct((N_CTX, POOL_W), F32),
        grid=(BATCH,),
        in_specs=[spec],
        out_specs=spec,
        scratch_shapes=[pltpu.VMEM((POOL_GROUPS, SEQ, SEQ), BF16)],
        compiler_params=_params(("arbitrary",)),
        name="pool_ctx",
    )(xp)


POOL_HALO = (max(POOL_WINDOWS) // 2) * GRID_W


def _pool_lat_kernel(x_ref, o_ref, pad_ref):
    rows = DEC_SEQ // GRID_W
    p = lax.broadcasted_iota(I32, (DEC_SEQ, POOL_GROUP_DIM), 0)
    r = p // GRID_W
    cidx = p % GRID_W
    zeros = jnp.zeros((POOL_HALO, POOL_GROUP_DIM), F32)
    pad_ref[0:POOL_HALO, :] = zeros
    pad_ref[POOL_HALO + DEC_SEQ:2 * POOL_HALO + DEC_SEQ, :] = zeros
    for gi, w in enumerate(POOL_WINDOWS):
        cs = slice(gi * POOL_GROUP_DIM, (gi + 1) * POOL_GROUP_DIM)
        band = _band(TOK_TILE, w, GRID_W)
        for t in range(LAT_TILES_PER_SEQ):
            hi, lo = _split_bf16(x_ref[t, 0, :, cs])
            pad_ref[POOL_HALO + t * TOK_TILE:POOL_HALO + (t + 1) * TOK_TILE, :] = (
                _dot(band, hi) + _dot(band, lo))
        acc = jnp.zeros((DEC_SEQ, POOL_GROUP_DIM), F32)
        for dr in range(-(w // 2), w // 2):
            start = POOL_HALO + dr * GRID_W
            acc = acc + pad_ref[start:start + DEC_SEQ, :]
        cnt = (_win_count(r, rows, w) * _win_count(cidx, GRID_W, w)).astype(F32)
        pooled = acc / cnt
        for t in range(LAT_TILES_PER_SEQ):
            rs = slice(t * TOK_TILE, (t + 1) * TOK_TILE)
            o_ref[t, 0, :, cs] = pooled[rs] - x_ref[t, 0, :, cs]


def _pool_lat_call(xp):
    view = xp.reshape(N_TILES // DEC_BATCH, DEC_BATCH, TOK_TILE, POOL_W)
    blk = (LAT_TILES_PER_SEQ, 1, TOK_TILE, POOL_W)
    out = pl.pallas_call(
        _pool_lat_kernel,
        out_shape=jax.ShapeDtypeStruct((LAT_TILES_PER_SEQ, DEC_BATCH, TOK_TILE, POOL_W), F32),
        grid=(DEC_BATCH,),
        in_specs=[pl.BlockSpec(blk, lambda s: (CTX_TILES // DEC_BATCH // LAT_TILES_PER_SEQ, s, 0, 0))],
        out_specs=pl.BlockSpec(blk, lambda s: (0, s, 0, 0)),
        scratch_shapes=[pltpu.VMEM((DEC_SEQ + 2 * POOL_HALO, POOL_GROUP_DIM), F32)],
        compiler_params=_params(("arbitrary",)),
        name="pool_lat",
    )(view)
    return out.reshape(N_LAT, POOL_W)


def _post_kernel(xc_ref, xl_ref, mod_ref, of_ref, ob_ref, og_ref, pc_ref, pl_ref, mg_ref, gng_ref,
                 wpg_ref, psc_ref, wbg_ref, wbp_ref, wout_ref, n2g_ref, wr_ref, br_ref,
                 x1_ref, h2_ref, idx_ref, tw_ref):
    t = pl.program_id(0)
    is_ctx = t < CTX_TILES
    x = jnp.where(is_ctx, xc_ref[...], xl_ref[...])
    pooled = jnp.where(is_ctx, pc_ref[...], pl_ref[...])
    mod = mod_ref[0]
    gate1 = mod[:, 2 * D_MODEL:3 * D_MODEL]
    shift2 = mod[:, 3 * D_MODEL:4 * D_MODEL]
    scale2 = mod[:, 4 * D_MODEL:5 * D_MODEL]

    o = of_ref[...] + ob_ref[...]
    og = og_ref[...].astype(F32)
    gated = []
    for h in range(GLA_HEADS):
        vs = slice(h * GLA_DV, (h + 1) * GLA_DV)
        oh = _rms(o[:, vs]) * gng_ref[:, vs]
        gated.append((oh * og[:, vs]).astype(BF16))
    br_gla = _dot(jnp.concatenate(gated, axis=-1), wbg_ref[...])

    pm = []
    for gi in range(POOL_GROUPS):
        cs = slice(gi * POOL_GROUP_DIM, (gi + 1) * POOL_GROUP_DIM)
        pmg = _dot(pooled[:, cs].astype(BF16), wpg_ref[gi]) * psc_ref[:, cs]
        pm.append(pmg.astype(BF16))
    br_pool = _dot(jnp.concatenate(pm, axis=-1), wbp_ref[...])

    mg = mg_ref[...].astype(F32)
    merged = mg[:, 0:D_MODEL] * br_gla + mg[:, D_MODEL:MG_W] * br_pool
    m = _dot(merged.astype(BF16), wout_ref[...])
    x1 = x + gate1 * m
    x1_ref[...] = x1
    h2 = _rms(x1) * n2g_ref[...]
    h2 = h2 * (1.0 + scale2) + shift2
    h2_ref[...] = _pack_halves(h2[:, :HALF_W], h2[:, HALF_W:])

    logits = _dot3(h2, wr_ref[...]) + br_ref[...]
    lane = lax.broadcasted_iota(I32, (TOK_TILE, LANES), 1)
    lane_f = lane.astype(F32)
    neg = jnp.float32(-jnp.inf)
    cur = jnp.where(lane < N_EXPERTS, logits, neg)
    vals, idxs = [], []
    for _ in range(TOP_K):
        mx = jnp.max(cur, axis=-1, keepdims=True)
        ix = jnp.min(jnp.where(cur == mx, lane_f, float(LANES)), axis=-1, keepdims=True)
        vals.append(mx)
        idxs.append(ix)
        cur = jnp.where(lane_f == ix, neg, cur)
    ex = [jnp.exp(vv - vals[0]) for vv in vals]
    tot = ex[0] + ex[1] + ex[2] + ex[3]
    idx_out = jnp.zeros((TOK_TILE, LANES), F32)
    w_out = jnp.zeros((TOK_TILE, LANES), F32)
    for kk in range(TOP_K):
        idx_out = jnp.where(lane == kk, idxs[kk], idx_out)
        w_out = jnp.where(lane == kk, ex[kk] / tot, w_out)
    idx_ref[...] = idx_out.astype(I32)
    tw_ref[...] = w_out


def _post_call(x_ctx, x_lat, mod3, o_f, o_b, og, pooled_c, pooled_l, mg, gng, wpg, psc, wbg, wbp,
               wout, n2g, wr, br):
    row = lambda t: (t, 0)
    const = lambda t: (0, 0)
    stored = lambda t: (_store_tile(t), 0)
    return pl.pallas_call(
        _post_kernel,
        out_shape=[
            jax.ShapeDtypeStruct((N_TOK, D_MODEL), F32),
            jax.ShapeDtypeStruct((N_TOK, HALF_W), I32),
            jax.ShapeDtypeStruct((N_TOK, LANES), I32),
            jax.ShapeDtypeStruct((N_TOK, LANES), F32),
        ],
        grid=(N_TILES,),
        in_specs=[
            pl.BlockSpec((TOK_TILE, D_MODEL), lambda t: (_ctx_tile(t), 0)),
            pl.BlockSpec((TOK_TILE, D_MODEL), lambda t: (_lat_tile(t), 0)),
            pl.BlockSpec((1, 1, N_MOD * D_MODEL), lambda t: (_mod_row(t), 0, 0)),
            pl.BlockSpec((TOK_TILE, V_W), stored),
            pl.BlockSpec((TOK_TILE, V_W), stored),
            pl.BlockSpec((TOK_TILE, V_W), stored),
            pl.BlockSpec((TOK_TILE, POOL_W), lambda t: (_ctx_tile(t), 0)),
            pl.BlockSpec((TOK_TILE, POOL_W),
                         lambda t: (jnp.maximum(_store_tile(t) - CTX_TILES, 0), 0)),
            pl.BlockSpec((TOK_TILE, MG_W), stored),
            pl.BlockSpec((1, V_W), const),
            pl.BlockSpec((POOL_GROUPS, POOL_GROUP_DIM, POOL_GROUP_DIM), lambda t: (0, 0, 0)),
            pl.BlockSpec((1, POOL_W), const),
            pl.BlockSpec((V_W, D_MODEL), const),
            pl.BlockSpec((POOL_W, D_MODEL), const),
            pl.BlockSpec((D_MODEL, D_MODEL), const),
            pl.BlockSpec((1, D_MODEL), const),
            pl.BlockSpec((D_MODEL, LANES), const),
            pl.BlockSpec((1, LANES), const),
        ],
        out_specs=[
            pl.BlockSpec((TOK_TILE, D_MODEL), row),
            pl.BlockSpec((TOK_TILE, HALF_W), row),
            pl.BlockSpec((TOK_TILE, LANES), row),
            pl.BlockSpec((TOK_TILE, LANES), row),
        ],
        compiler_params=_params(("arbitrary",)),
        name="post",
    )(x_ctx, x_lat, mod3, o_f, o_b, og, pooled_c, pooled_l, mg, gng, wpg, psc, wbg, wbp, wout, n2g,
      wr, br)


def _route_kernel(idx_ref, rank_ref, cnt_ref, carry_ref, strict_ref):
    t = pl.program_id(0)

    @pl.when(t == 0)
    def _():
        carry_ref[...] = jnp.zeros((1, LANES), F32)
        row = lax.broadcasted_iota(I32, (ROUTE_TILE, ROUTE_TILE), 0)
        col = lax.broadcasted_iota(I32, (ROUTE_TILE, ROUTE_TILE), 1)
        strict_ref[...] = jnp.where(col < row, 1.0, 0.0).astype(BF16)

    idx = idx_ref[...]
    lane = lax.broadcasted_iota(I32, (ROUTE_TILE, LANES), 1)
    sel = [lane == idx[:, kk:kk + 1] for kk in range(TOP_K)]
    onehot = jnp.zeros((ROUTE_TILE, LANES), F32)
    for kk in range(TOP_K):
        onehot = onehot + jnp.where(sel[kk], 1.0, 0.0)
    before = _dot(strict_ref[...], onehot.astype(BF16)) + carry_ref[...]
    rank = jnp.zeros((ROUTE_TILE, LANES), F32)
    for kk in range(TOP_K):
        rk = jnp.sum(jnp.where(sel[kk], before, 0.0), axis=-1, keepdims=True)
        rank = jnp.where(lane == kk, rk, rank)
    rank_ref[...] = rank.astype(I32)
    carry_ref[...] = carry_ref[...] + jnp.sum(onehot, axis=0, keepdims=True)
    cnt_ref[...] = jnp.broadcast_to(carry_ref[...], (8, LANES))


def _route_call(idx):
    return pl.pallas_call(
        _route_kernel,
        out_shape=[
            jax.ShapeDtypeStruct((N_TOK, LANES), I32),
            jax.ShapeDtypeStruct((8, LANES), F32),
        ],
        grid=(N_TOK // ROUTE_TILE,),
        in_specs=[pl.BlockSpec((ROUTE_TILE, LANES), lambda t: (t, 0))],
        out_specs=[
            pl.BlockSpec((ROUTE_TILE, LANES), lambda t: (t, 0)),
            pl.BlockSpec((8, LANES), lambda t: (0, 0)),
        ],
        scratch_shapes=[pltpu.VMEM((1, LANES), F32), pltpu.VMEM((ROUTE_TILE, ROUTE_TILE), BF16)],
        compiler_params=_params(("arbitrary",)),
        name="route",
    )(idx)


def _moe_kernel(be_ref, nu_ref, ne_ref, par_ref, x_ref, wg_hbm, bg_ref, wu_hbm, bu_ref, wd_hbm,
                bd_ref, y_ref, wf_ref, wgu_ref, wdb_ref, sems):
    b = pl.program_id(0)
    n_used = nu_ref[0]
    e = be_ref[b]
    prev = be_ref[jnp.maximum(b - 1, 0)]
    live = b < n_used

    def weight_copies(expert, slot):
        return [pltpu.make_async_copy(w.at[expert], wf_ref.at[slot, i], sems.at[slot])
                for i, w in enumerate((wg_hbm, wu_hbm, wd_hbm))]

    @pl.when(live & ((b == 0) | (e != prev)))
    def _():
        slot = par_ref[b]

        @pl.when(b == 0)
        def _():
            for cp in weight_copies(e, slot):
                cp.start()

        for cp in weight_copies(e, slot):
            cp.wait()
        nxt = ne_ref[b]

        @pl.when(nxt >= 0)
        def _():
            for cp in weight_copies(nxt, 1 - slot):
                cp.start()

        wgu_ref[:, :D_FF] = wf_ref[slot, 0].astype(BF16)
        wgu_ref[:, D_FF:] = wf_ref[slot, 1].astype(BF16)
        wdb_ref[...] = wf_ref[slot, 2].astype(BF16)

    @pl.when(live)
    def _():
        x_lo, x_hi = _unpack_halves(x_ref[...])
        x = jnp.concatenate([x_lo.astype(BF16), x_hi.astype(BF16)], axis=-1)
        gu = _dot(x, wgu_ref[...])
        gate = jnp.minimum(gu[:, :D_FF] + bg_ref[0], SWIGLU_LIMIT)
        up = jnp.clip(gu[:, D_FF:] + bu_ref[0], -SWIGLU_LIMIT, SWIGLU_LIMIT)
        act = (up + 1.0) * (gate * _sigmoid(SWIGLU_ALPHA * gate))
        y = _dot(act.astype(BF16), wdb_ref[...]) + bd_ref[0]
        y_ref[...] = _pack_halves(y[:, :HALF_W], y[:, HALF_W:])

    @pl.when(jnp.logical_not(live))
    def _():
        y_ref[...] = jnp.zeros((MOE_BLOCK, HALF_W), I32)


def _moe_call(block_e, n_used, next_e, parity, hs, w_gate, b_gate, w_up, b_up, w_down, b_down):
    def blk(b, be, nu, ne, par):
        return jnp.minimum(b, nu[0] - 1)

    row = lambda b, be, nu, ne, par: (blk(b, be, nu, ne, par), 0)
    bsel = lambda b, be, nu, ne, par: (be[blk(b, be, nu, ne, par)], 0, 0)
    any_spec = pl.BlockSpec(memory_space=pl.ANY)
    assert D_MODEL == D_FF
    return pl.pallas_call(
        _moe_kernel,
        out_shape=jax.ShapeDtypeStruct((N_SLOTS, HALF_W), I32),
        grid_spec=pltpu.PrefetchScalarGridSpec(
            num_scalar_prefetch=4,
            grid=(N_SLOT_BLOCKS,),
            in_specs=[
                pl.BlockSpec((MOE_BLOCK, HALF_W), row),
                any_spec,
                pl.BlockSpec((1, 1, D_FF), bsel),
                any_spec,
                pl.BlockSpec((1, 1, D_FF), bsel),
                any_spec,
                pl.BlockSpec((1, 1, D_MODEL), bsel),
            ],
            out_specs=pl.BlockSpec((MOE_BLOCK, HALF_W), lambda b, be, nu, ne, par: (b, 0)),
            scratch_shapes=[
                pltpu.VMEM((2, 3, D_MODEL, D_FF), F32),
                pltpu.VMEM((D_MODEL, 2 * D_FF), BF16),
                pltpu.VMEM((D_FF, D_MODEL), BF16),
                pltpu.SemaphoreType.DMA((2,)),
            ],
        ),
        compiler_params=_params(("arbitrary",)),
        name="moe",
    )(block_e, n_used, next_e, parity, hs, w_gate, b_gate, w_up, b_up, w_down, b_down)


SC_CORES = 2
SC_SUBCORES = 16
SC_WORKERS = SC_CORES * SC_SUBCORES
SC_ROWS = 128
COMBINE_CHUNKS = 2
COMBINE_TILE = 2 * TOK_TILE
assert N_CTX == N_LAT


def _sc_gather_rows(table, idx):
    n_idx = idx.shape[0]
    width = table.shape[1]
    rows = SC_ROWS // 2
    per_worker = n_idx // SC_WORKERS
    n_chunks = per_worker // rows
    assert n_chunks * rows * SC_WORKERS == n_idx and n_chunks >= 2
    mesh = plsc.VectorSubcoreMesh(core_axis_name="c", subcore_axis_name="s")

    @functools.partial(
        pl.kernel, mesh=mesh,
        out_type=jax.ShapeDtypeStruct((n_idx, width), table.dtype),
        scratch_types=[pltpu.VMEM((2, rows), I32), pltpu.VMEM((2, rows, width), table.dtype),
                       pltpu.SemaphoreType.DMA((2,)), pltpu.SemaphoreType.DMA((2,))],
        name="sc_gather",
    )
    def gather(table_hbm, idx_hbm, out_hbm, idx_v, rows_v, sem_g, sem_w):
        worker = lax.axis_index("s") * SC_CORES + lax.axis_index("c")
        base = worker * per_worker

        def chunk_rows(ch):
            return pl.ds(pl.multiple_of(base + ch * rows, rows), rows)

        def start_gather(ch, b):
            pltpu.sync_copy(idx_hbm.at[chunk_rows(ch)], idx_v.at[b])
            pltpu.async_copy(table_hbm.at[idx_v.at[b]], rows_v.at[b], sem_g.at[b])

        def wait_gather(b):
            pltpu.make_async_copy(table_hbm.at[pl.ds(0, rows)], rows_v.at[b], sem_g.at[b]).wait()

        def write_copy(ch, b):
            return pltpu.make_async_copy(rows_v.at[b], out_hbm.at[chunk_rows(ch)], sem_w.at[b])

        start_gather(0, 0)
        for ch in range(n_chunks):
            b = ch % 2
            if ch + 1 < n_chunks:
                if ch >= 1:
                    write_copy(ch - 1, 1 - b).wait()
                start_gather(ch + 1, 1 - b)
            wait_gather(b)
            write_copy(ch, b).start()
        write_copy(n_chunks - 2, n_chunks % 2).wait()
        write_copy(n_chunks - 1, (n_chunks - 1) % 2).wait()

    return gather(table, idx)


def _sc_scatter_rows(rows, idx3, n_out):
    n_rows, width = rows.shape
    n_chunks = n_rows // SC_ROWS // SC_WORKERS
    assert n_chunks * SC_ROWS * SC_WORKERS == n_rows and idx3.shape == (n_rows // SC_ROWS, TOP_K, SC_ROWS)
    mesh = plsc.VectorSubcoreMesh(core_axis_name="c", subcore_axis_name="s")

    @functools.partial(
        pl.kernel, mesh=mesh,
        out_type=jax.ShapeDtypeStruct((n_out, width), rows.dtype),
        scratch_types=[pltpu.VMEM((TOP_K, SC_ROWS), I32), pltpu.VMEM((SC_ROWS, width), rows.dtype),
                       pltpu.SemaphoreType.DMA],
        name="sc_scatter",
    )
    def scatter(rows_hbm, idx_hbm, out_hbm, idx_v, rows_v, sem):
        worker = lax.axis_index("s") * SC_CORES + lax.axis_index("c")

        @pl.loop(0, n_chunks)
        def _(ch):
            chunk = worker * n_chunks + ch
            pltpu.sync_copy(idx_hbm.at[chunk], idx_v)
            pltpu.sync_copy(rows_hbm.at[pl.ds(pl.multiple_of(chunk * SC_ROWS, SC_ROWS), SC_ROWS)],
                            rows_v)
            for kk in range(TOP_K):
                pltpu.async_copy(rows_v, out_hbm.at[idx_v.at[kk]], sem).wait()

    return scatter(rows, idx3)


def _combine_kernel(x1_ref, mod_ref, tw_ref, fg_ref, g_ref, *rest):
    out_ref = rest[-1]
    tw = tw_ref[...]
    f_lo = jnp.zeros((COMBINE_TILE, HALF_W), F32)
    f_hi = jnp.zeros((COMBINE_TILE, HALF_W), F32)
    for kk in range(TOP_K):
        lo, hi = _unpack_halves(g_ref[kk])
        f_lo = f_lo + lo * tw[:, kk:kk + 1]
        f_hi = f_hi + hi * tw[:, kk:kk + 1]
    gate2 = mod_ref[0][:, 5 * D_MODEL:6 * D_MODEL]
    x2 = x1_ref[...] + gate2 * jnp.concatenate([f_lo, f_hi], axis=-1)
    out_ref[...] = _rms(x2) * fg_ref[...]


def _combine_call(x1, mod3, tw, final_g, gathered, partial, tile0, out_tile0, out_tiles, name):
    n_tiles = gathered.shape[1] // COMBINE_TILE
    in_specs = [
        pl.BlockSpec((COMBINE_TILE, D_MODEL), lambda t: (tile0 + t, 0)),
        pl.BlockSpec((1, 1, N_MOD * D_MODEL), lambda t: (_mod_row((tile0 + t) * (COMBINE_TILE // TOK_TILE)), 0, 0)),
        pl.BlockSpec((COMBINE_TILE, LANES), lambda t: (tile0 + t, 0)),
        pl.BlockSpec((1, D_MODEL), lambda t: (0, 0)),
        pl.BlockSpec((TOP_K, COMBINE_TILE, HALF_W), lambda t: (0, t, 0)),
    ]
    args = [x1, mod3, tw, final_g, gathered]
    aliases = {}
    if partial is not None:
        in_specs.append(pl.BlockSpec(memory_space=pl.ANY))
        args.append(partial)
        aliases = {len(args) - 1: 0}
    return pl.pallas_call(
        _combine_kernel,
        out_shape=jax.ShapeDtypeStruct((out_tiles * COMBINE_TILE, D_MODEL), F32),
        grid=(n_tiles,),
        in_specs=in_specs,
        out_specs=pl.BlockSpec((COMBINE_TILE, D_MODEL), lambda t: (out_tile0 + t, 0)),
        input_output_aliases=aliases,
        compiler_params=_params(("arbitrary",)),
        name=name,
    )(*args)


def kernel(x_prompt, x_sample, state_gla_fwd, state_gla_bwd, c, c_ctx, norm1_g, w_mod, b_mod, w_in,
           w_alpha, b_alpha, gla_norm_g, w_pool_grp, pool_scale, w_branch_gla, w_branch_pool, w_out,
           norm2_g, w_router, b_router, w_gate, b_gate, w_up, b_up, w_down, b_down, final_norm_g):
    l = 0
    x_ctx = x_prompt.reshape(N_CTX, D_MODEL)
    x_lat = x_sample.reshape(N_LAT, D_MODEL)

    cvec = jnp.concatenate([c_ctx[None, :], c, jnp.zeros((8 - 1 - DEC_BATCH, D_MODEL), F32)], axis=0)
    mod = _mod_call(cvec, w_mod[l], b_mod[l][None, :])
    mod3 = mod.reshape(8, 1, N_MOD * D_MODEL)

    w_in_b = w_in[l].astype(BF16)
    w_main = w_in_b[:, :MAIN_W]
    w_alr = w_in_b[:, MAIN_W:MAIN_W + ALR_W]
    w_xp = w_in_b[:, MAIN_W + ALR_W:MAIN_W + ALR_W + POOL_W]
    w_mg = w_in_b[:, MAIN_W + ALR_W + POOL_W:]
    q, k, v, og, alr, xp, mg = _inproj_call(x_ctx, x_lat, mod3, norm1_g[l][None, :],
                                            w_main, w_alr, w_xp, w_mg)

    zpad = jnp.zeros((GLA_LOWRANK, QK_W), F32)
    wa_f = jnp.concatenate([w_alpha[l, 0], zpad], axis=0)
    wa_b = jnp.concatenate([zpad, w_alpha[l, 1]], axis=0)
    o_f, o_b, s_f, s_b = _gla_call(q, k, v, alr, wa_f, b_alpha[l, 0][None, :], wa_b,
                                   b_alpha[l, 1][None, :], state_gla_fwd[:, l], state_gla_bwd[:, l])

    pooled_c = _pool_ctx_call(xp)
    pooled_l = _pool_lat_call(xp)

    w_router_pad = jnp.pad(w_router[l], ((0, 0), (0, LANES - N_EXPERTS)))
    b_router_pad = jnp.pad(b_router[l], (0, LANES - N_EXPERTS))[None, :]
    x1, h2, top_idx, top_w = _post_call(
        x_ctx, x_lat, mod3, o_f, o_b, og, pooled_c, pooled_l, mg,
        gla_norm_g[l].reshape(1, V_W), w_pool_grp[l].astype(BF16), pool_scale[l][None, :],
        w_branch_gla[l].astype(BF16), w_branch_pool[l].astype(BF16), w_out[l].astype(BF16),
        norm2_g[l][None, :], w_router_pad, b_router_pad)

    rank, cnt = _route_call(top_idx)
    counts = cnt[0, :N_EXPERTS].astype(I32)
    padded = (counts + MOE_BLOCK - 1) // MOE_BLOCK * MOE_BLOCK
    pad_end = jnp.cumsum(padded).astype(I32)
    pad_start = pad_end - padded
    block_first = jnp.arange(N_SLOT_BLOCKS, dtype=I32) * MOE_BLOCK
    block_e = jnp.minimum(jnp.sum((pad_end[None, :] <= block_first[:, None]).astype(I32), axis=1),
                          N_EXPERTS - 1).astype(I32)
    n_used = (pad_end[-1:] // MOE_BLOCK).astype(I32)
    run_start = jnp.concatenate([jnp.ones((1,), I32), (block_e[1:] != block_e[:-1]).astype(I32)])
    parity = ((jnp.cumsum(run_start) - 1) % 2).astype(I32)
    after = pad_end[block_e] // MOE_BLOCK
    next_e = jnp.where(after < n_used[0], block_e[jnp.minimum(after, N_SLOT_BLOCKS - 1)], -1).astype(I32)
    experts = jnp.arange(N_EXPERTS, dtype=I32)
    tk = top_idx[:, :TOP_K]
    pos = jnp.sum(jnp.where(tk[:, :, None] == experts, pad_start, 0), axis=-1) + rank[:, :TOP_K]
    pos = pos.astype(I32)
    pos_by_choice = pos.T
    pos_chunks = pos_by_choice.reshape(TOP_K, N_TOK // SC_ROWS, SC_ROWS).transpose(1, 0, 2)

    hs = _sc_scatter_rows(h2, pos_chunks, N_SLOTS)
    y = _moe_call(block_e, n_used, next_e, parity, hs,
                  w_gate[l], b_gate[l][:, None, :], w_up[l], b_up[l][:, None, :],
                  w_down[l], b_down[l][:, None, :])
    outs = []
    chunk_tok = N_CTX // COMBINE_CHUNKS
    for group, tok0 in (("ctx", 0), ("lat", N_CTX)):
        out = None
        for ci in range(COMBINE_CHUNKS):
            t0 = tok0 + ci * chunk_tok
            idx = pos_by_choice[:, t0:t0 + chunk_tok].reshape(TOP_K * chunk_tok)
            gathered = _sc_gather_rows(y, idx).reshape(TOP_K, chunk_tok, HALF_W)
            out = _combine_call(x1, mod3, top_w, final_norm_g[None, :], gathered, out,
                                t0 // COMBINE_TILE, ci * chunk_tok // COMBINE_TILE, N_CTX // COMBINE_TILE,
                                "combine_%s%d" % (group, ci))
        outs.append(out)
    y_prompt = outs[0].reshape(BATCH, SEQ, D_MODEL)
    y_sample = outs[1].reshape(DEC_BATCH, DEC_SEQ, D_MODEL)
    return (y_prompt, y_sample, s_f[:, None], s_b[:, None])
```

```python
import functools

import jax
import jax.numpy as jnp
from jax import lax
from jax.experimental import pallas as pl
from jax.experimental.pallas import tpu as pltpu
from jax.experimental.pallas import tpu_sc as plsc

F32 = jnp.float32
BF16 = jnp.bfloat16
I32 = jnp.int32

D_MODEL = 1024
BATCH = 32
SEQ = 256
DEC_BATCH = 4
DEC_SEQ = 2048
GRID_W = 64
GLA_HEADS = 4
GLA_DK = 128
GLA_DV = 256
GLA_LOWRANK = 16
GLA_TAU = 16.0
GLA_CHUNK = 64
POOL_GROUPS = 4
POOL_GROUP_DIM = 128
POOL_WINDOWS = (2, 4, 8, 16)
N_EXPERTS = 32
TOP_K = 4
D_FF = 1024
SWIGLU_LIMIT = 7.0
SWIGLU_ALPHA = 1.702
MOE_BLOCK = 256
NORM_EPS = 1e-6
N_MOD = 6

QK_W = GLA_HEADS * GLA_DK
V_W = GLA_HEADS * GLA_DV
POOL_W = POOL_GROUPS * POOL_GROUP_DIM
MAIN_W = 2 * QK_W + 2 * V_W
ALR_W = 2 * GLA_LOWRANK
MG_W = 2 * D_MODEL

N_CTX = BATCH * SEQ
N_LAT = DEC_BATCH * DEC_SEQ
N_TOK = N_CTX + N_LAT
N_SLOT_BLOCKS = -(-(N_TOK * TOP_K + N_EXPERTS * (MOE_BLOCK - 1)) // MOE_BLOCK)
N_SLOTS = N_SLOT_BLOCKS * MOE_BLOCK

LANES = 128
TOK_TILE = 256
N_TILES = N_TOK // TOK_TILE
CTX_TILES = N_CTX // TOK_TILE
LAT_TILES_PER_SEQ = DEC_SEQ // TOK_TILE
ROUTE_TILE = 512
VMEM_LIMIT = 56 * 1024 * 1024

GLA_SEQS = 4
CTX_CHUNKS = SEQ // GLA_CHUNK
LAT_CHUNKS = DEC_SEQ // GLA_CHUNK
CHUNKS_PER_TILE = TOK_TILE // GLA_CHUNK
GLA_CTX_STEPS = (BATCH // GLA_SEQS) * CTX_CHUNKS
TILE_GRID = 8

NT_DIMS = (((1,), (1,)), ((), ()))
TN_DIMS = (((0,), (0,)), ((), ()))

assert DEC_BATCH == GLA_SEQS and SEQ == TOK_TILE and N_TILES == TILE_GRID * TILE_GRID


def _params(semantics, vmem=VMEM_LIMIT):
    return pltpu.CompilerParams(dimension_semantics=semantics, vmem_limit_bytes=vmem)


def _split_bf16(a):
    hi = a.astype(BF16)
    lo = (a - hi.astype(F32)).astype(BF16)
    return hi, lo


def _dot(a, b):
    return jnp.dot(a, b, preferred_element_type=F32)


def _dot3(a, b):
    a_hi, a_lo = _split_bf16(a)
    b_hi, b_lo = _split_bf16(b)
    return _dot(a_hi, b_hi) + _dot(a_lo, b_hi) + _dot(a_hi, b_lo)


def _dot3_short(a, b):
    a_hi, a_lo = _split_bf16(a)
    b_hi, b_lo = _split_bf16(b)
    return _dot(jnp.concatenate([a_hi, a_lo, a_hi], axis=1), jnp.concatenate([b_hi, b_hi, b_lo], axis=0))


def _sigmoid(x):
    return 1.0 / (1.0 + jnp.exp(-x))


HALF_W = D_MODEL // 2
HIGH_HALF_MASK = -65536


def _pack_halves(lo, hi):
    lo_bits = pltpu.bitcast(lo.astype(BF16).astype(F32), I32)
    hi_bits = pltpu.bitcast(hi.astype(BF16).astype(F32), I32)
    return lax.shift_right_logical(lo_bits, 16) | (hi_bits & HIGH_HALF_MASK)


def _unpack_halves(words):
    lo = pltpu.bitcast(lax.shift_left(words, 16), F32)
    hi = pltpu.bitcast(words & HIGH_HALF_MASK, F32)
    return lo, hi


def _rms(x):
    return x * lax.rsqrt(jnp.mean(x * x, axis=-1, keepdims=True) + NORM_EPS)


def _mod_row(t):
    return jnp.where(t < CTX_TILES, 0, 1 + (t - CTX_TILES) // LAT_TILES_PER_SEQ)


def _store_tile(t):
    u = t - CTX_TILES
    return jnp.where(t < CTX_TILES, t,
                     CTX_TILES + DEC_BATCH * (u % LAT_TILES_PER_SEQ) + u // LAT_TILES_PER_SEQ)


def _ctx_tile(t):
    return jnp.minimum(t, CTX_TILES - 1)


def _lat_tile(t):
    return jnp.maximum(t - CTX_TILES, 0)


def _mod_kernel(c_ref, w_ref, b_ref, o_ref):
    c = c_ref[...]
    o_ref[...] = _dot3(c * _sigmoid(c), w_ref[...]) + b_ref[...]


def _mod_call(cvec, w_mod, b_mod):
    rows = cvec.shape[0]
    return pl.pallas_call(
        _mod_kernel,
        out_shape=jax.ShapeDtypeStruct((rows, N_MOD * D_MODEL), F32),
        grid=(N_MOD,),
        in_specs=[
            pl.BlockSpec((rows, D_MODEL), lambda j: (0, 0)),
            pl.BlockSpec((D_MODEL, D_MODEL), lambda j: (0, j)),
            pl.BlockSpec((1, D_MODEL), lambda j: (0, j)),
        ],
        out_specs=pl.BlockSpec((rows, D_MODEL), lambda j: (0, j)),
        compiler_params=_params(("arbitrary",)),
        name="mod",
    )(cvec, w_mod, b_mod)


def _inproj_kernel(xc_ref, xl_ref, mod_ref, g_ref, wmain_ref, walr_ref, wxp_ref, wmg_ref,
                   q_ref, k_ref, v_ref, og_ref, alr_ref, xp_ref, mg_ref):
    t = pl.program_id(0)
    x = jnp.where(t < CTX_TILES, xc_ref[...], xl_ref[...])
    mod = mod_ref[0]
    shift1 = mod[:, 0:D_MODEL]
    scale1 = mod[:, D_MODEL:2 * D_MODEL]
    h = _rms(x) * g_ref[...]
    h = (h * (1.0 + scale1) + shift1).astype(BF16)
    z = _dot(h, wmain_ref[...])
    q_ref[...] = (z[:, 0:QK_W] * (GLA_DK ** -0.5)).astype(BF16)
    k_ref[...] = z[:, QK_W:2 * QK_W].astype(BF16)
    v_ref[...] = z[:, 2 * QK_W:2 * QK_W + V_W].astype(BF16)
    og = z[:, 2 * QK_W + V_W:MAIN_W]
    og_ref[...] = (og * _sigmoid(og)).astype(BF16)
    alr_ref[...] = _dot(h, walr_ref[...])
    xp_ref[...] = _dot(h, wxp_ref[...])
    mg_ref[...] = _sigmoid(_dot(h, wmg_ref[...])).astype(BF16)


def _inproj_call(x_ctx, x_lat, mod3, norm1_g, w_main, w_alr, w_xp, w_mg):
    const = lambda t: (0, 0)
    stored = lambda t: (_store_tile(t), 0)
    widths = (QK_W, QK_W, V_W, V_W, ALR_W, POOL_W, MG_W)
    dtypes = (BF16, BF16, BF16, BF16, F32, F32, BF16)
    return pl.pallas_call(
        _inproj_kernel,
        out_shape=[jax.ShapeDtypeStruct((N_TOK, w), dt) for w, dt in zip(widths, dtypes)],
        grid=(N_TILES,),
        in_specs=[
            pl.BlockSpec((TOK_TILE, D_MODEL), lambda t: (_ctx_tile(t), 0)),
            pl.BlockSpec((TOK_TILE, D_MODEL), lambda t: (_lat_tile(t), 0)),
            pl.BlockSpec((1, 1, N_MOD * D_MODEL), lambda t: (_mod_row(t), 0, 0)),
            pl.BlockSpec((1, D_MODEL), const),
            pl.BlockSpec((D_MODEL, MAIN_W), const),
            pl.BlockSpec((D_MODEL, ALR_W), const),
            pl.BlockSpec((D_MODEL, POOL_W), const),
            pl.BlockSpec((D_MODEL, MG_W), const),
        ],
        out_specs=[pl.BlockSpec((TOK_TILE, w), stored) for w in widths],
        compiler_params=_params(("arbitrary",)),
        name="inproj",
    )(x_ctx, x_lat, mod3, norm1_g, w_main, w_alr, w_xp, w_mg)


def _gla_direction(q_ref, k_ref, v_ref, alr_ref, wa_ref, ba_ref, o_ref, st_ref, d, rev):
    rows = GLA_SEQS * GLA_CHUNK
    stack = lambda ref, cols: jnp.concatenate([ref[0, s, :, cols] for s in range(GLA_SEQS)], axis=0)
    alr = stack(alr_ref, slice(None))
    a = _dot3_short(alr, wa_ref[...]) + ba_ref[...]
    g = (jnp.minimum(a, 0.0) - jnp.log(1.0 + jnp.exp(-jnp.abs(a)))) * (1.0 / GLA_TAU)

    row = lax.broadcasted_iota(I32, (rows, rows), 0)
    col = lax.broadcasted_iota(I32, (rows, rows), 1)
    same = (row // GLA_CHUNK) == (col // GLA_CHUNK)
    tri = same & ((col >= row) if rev else (col <= row))
    tri_b = jnp.where(tri, 1.0, 0.0).astype(BF16)
    g_hi, g_lo = _split_bf16(g)
    bcum = _dot(tri_b, g_hi) + _dot(tri_b, g_lo)

    def per_seq_row(r):
        return jnp.concatenate(
            [jnp.broadcast_to(bcum[s * GLA_CHUNK + r:s * GLA_CHUNK + r + 1], (GLA_CHUNK, QK_W))
             for s in range(GLA_SEQS)], axis=0)

    r_last = 0 if rev else GLA_CHUNK - 1
    blast = per_seq_row(r_last)
    bmid = per_seq_row(GLA_CHUNK // 2)
    e_q = jnp.exp(bcum - bmid)
    e_k = jnp.exp(bmid - bcum)
    e_in = jnp.exp(bcum)
    e_out = jnp.exp(blast - bcum)
    q = stack(q_ref, slice(None)).astype(F32)
    k = stack(k_ref, slice(None)).astype(F32)

    wide = (rows, GLA_SEQS * GLA_DK)
    own = (lax.broadcasted_iota(I32, wide, 0) // GLA_CHUNK) == (lax.broadcasted_iota(I32, wide, 1) // GLA_DK)

    def block_diag(x):
        return jnp.where(own, jnp.concatenate([x] * GLA_SEQS, axis=1), 0.0).astype(BF16)

    for h in range(GLA_HEADS):
        ks = slice(h * GLA_DK, (h + 1) * GLA_DK)
        vs = slice(h * GLA_DV, (h + 1) * GLA_DV)
        qh = q[:, ks]
        kh = k[:, ks]
        vh = stack(v_ref, vs)
        att = lax.dot_general((qh * e_q[:, ks]).astype(BF16), (kh * e_k[:, ks]).astype(BF16),
                              NT_DIMS, preferred_element_type=F32)
        att = jnp.where(tri, att, 0.0).astype(BF16)
        st = st_ref[d, h]
        o_inter = lax.dot_general(block_diag(qh * e_in[:, ks]), st.astype(BF16), NT_DIMS,
                                  preferred_element_type=F32)
        o_h = o_inter + _dot(att, vh)
        for s in range(GLA_SEQS):
            o_ref[0, s, :, vs] = o_h[s * GLA_CHUNK:(s + 1) * GLA_CHUNK].astype(BF16)
        upd = lax.dot_general(vh, block_diag(kh * e_out[:, ks]), TN_DIMS,
                              preferred_element_type=F32)
        e_last = jnp.concatenate(
            [jnp.exp(bcum[s * GLA_CHUNK + r_last:s * GLA_CHUNK + r_last + 1, ks])
             for s in range(GLA_SEQS)], axis=1)
        st_ref[d, h] = st * e_last + upd


def _gla_kernel(qf_ref, kf_ref, vf_ref, af_ref, qb_ref, kb_ref, vb_ref, ab_ref,
                waf_ref, baf_ref, wab_ref, bab_ref, s0f_ref, s0b_ref,
                of_ref, ob_ref, sf_ref, sb_ref, st_ref):
    i = pl.program_id(0)
    is_ctx = i < GLA_CTX_STEPS
    chunk = jnp.where(is_ctx, i % CTX_CHUNKS, i - GLA_CTX_STEPS)

    @pl.when(is_ctx & (chunk == 0))
    def _():
        st_ref[...] = jnp.zeros(st_ref.shape, F32)

    @pl.when(i == GLA_CTX_STEPS)
    def _():
        for s in range(GLA_SEQS):
            ls = slice(s * GLA_DK, (s + 1) * GLA_DK)
            for h in range(GLA_HEADS):
                st_ref[0, h, :, ls] = s0f_ref[s, h].T
                st_ref[1, h, :, ls] = s0b_ref[s, h].T

    _gla_direction(qf_ref, kf_ref, vf_ref, af_ref, waf_ref, baf_ref, of_ref, st_ref, 0, False)
    _gla_direction(qb_ref, kb_ref, vb_ref, ab_ref, wab_ref, bab_ref, ob_ref, st_ref, 1, True)

    @pl.when(is_ctx & (chunk == CTX_CHUNKS - 1))
    def _():
        for s in range(GLA_SEQS):
            ls = slice(s * GLA_DK, (s + 1) * GLA_DK)
            for h in range(GLA_HEADS):
                sf_ref[s, h] = st_ref[0, h, :, ls].T
                sb_ref[s, h] = st_ref[1, h, :, ls].T


def _gla_block(i, rev):
    is_ctx = i < GLA_CTX_STEPS
    group = i // CTX_CHUNKS
    c_ctx = i % CTX_CHUNKS
    c_lat = i - GLA_CTX_STEPS
    if rev:
        c_ctx = CTX_CHUNKS - 1 - c_ctx
        c_lat = LAT_CHUNKS - 1 - c_lat
    j = c_lat // CHUNKS_PER_TILE
    per_row = TILE_GRID // GLA_SEQS
    a = jnp.where(is_ctx, group // per_row, CTX_TILES // TILE_GRID + j // per_row)
    b = jnp.where(is_ctx, group % per_row, j % per_row)
    c = jnp.where(is_ctx, c_ctx, c_lat % CHUNKS_PER_TILE)
    return (a, b, c, 0)


def _gla_call(q, k, v, alr, wa_f, ba_f, wa_b, ba_b, s0_f, s0_b):
    def view(arr):
        return arr.reshape(TILE_GRID, TILE_GRID, TOK_TILE, arr.shape[-1])

    def spec(width, rev):
        return pl.BlockSpec((1, GLA_SEQS, GLA_CHUNK, width), lambda i: _gla_block(i, rev))

    const = lambda i: (0, 0)
    st_block = (GLA_SEQS, GLA_HEADS, GLA_DK, GLA_DV)
    whole_state = pl.BlockSpec(st_block, lambda i: (0, 0, 0, 0))
    ctx_state = pl.BlockSpec(
        st_block, lambda i: (jnp.minimum(i // CTX_CHUNKS, BATCH // GLA_SEQS - 1), 0, 0, 0))
    in_specs = []
    for rev in (False, True):
        in_specs += [spec(QK_W, rev), spec(QK_W, rev), spec(V_W, rev), spec(ALR_W, rev)]
    in_specs += [pl.BlockSpec((ALR_W, QK_W), const), pl.BlockSpec((1, QK_W), const)] * 2
    in_specs += [whole_state, whole_state]
    o_shape = jax.ShapeDtypeStruct((TILE_GRID, TILE_GRID, TOK_TILE, V_W), BF16)
    s_shape = jax.ShapeDtypeStruct((BATCH, GLA_HEADS, GLA_DK, GLA_DV), F32)
    qv, kv, vv, av = view(q), view(k), view(v), view(alr)
    o_f, o_b, s_f, s_b = pl.pallas_call(
        _gla_kernel,
        out_shape=[o_shape, o_shape, s_shape, s_shape],
        grid=(GLA_CTX_STEPS + LAT_CHUNKS,),
        in_specs=in_specs,
        out_specs=[spec(V_W, False), spec(V_W, True), ctx_state, ctx_state],
        scratch_shapes=[pltpu.VMEM((2, GLA_HEADS, GLA_DV, GLA_SEQS * GLA_DK), F32)],
        compiler_params=_params(("arbitrary",)),
        name="gla",
    )(qv, kv, vv, av, qv, kv, vv, av, wa_f, ba_f, wa_b, ba_b, s0_f, s0_b)
    return o_f.reshape(N_TOK, V_W), o_b.reshape(N_TOK, V_W), s_f, s_b


def _band(n, w, block):
    row = lax.broadcasted_iota(I32, (n, n), 0)
    col = lax.broadcasted_iota(I32, (n, n), 1)
    inside = (col >= row - w // 2) & (col <= row + w // 2 - 1)
    if block < n:
        inside = inside & ((row // block) == (col // block))
    return jnp.where(inside, 1.0, 0.0).astype(BF16)


def _win_count(p, n, w):
    return jnp.minimum(p + w // 2 - 1, n - 1) - jnp.maximum(p - w // 2, 0) + 1


def _pool_ctx_kernel(x_ref, o_ref, band_ref):
    @pl.when(pl.program_id(0) == 0)
    def _():
        for gi, w in enumerate(POOL_WINDOWS):
            band_ref[gi] = _band(SEQ, w, SEQ)

    p = lax.broadcasted_iota(I32, (SEQ, POOL_GROUP_DIM), 0)
    for gi, w in enumerate(POOL_WINDOWS):
        cs = slice(gi * POOL_GROUP_DIM, (gi + 1) * POOL_GROUP_DIM)
        x = x_ref[:, cs]
        hi, lo = _split_bf16(x)
        band = band_ref[gi]
        s = _dot(band, hi) + _dot(band, lo)
        cnt = _win_count(p, SEQ, w).astype(F32)
        o_ref[:, cs] = s / cnt - x


def _pool_ctx_call(xp):
    spec = pl.BlockSpec((SEQ, POOL_W), lambda b: (b, 0))
    return pl.pallas_call(
        _pool_ctx_kernel,
        out_shape=jax.ShapeDtypeStruct((N_CTX, POOL_W), F32),
        grid=(BATCH,),
        in_specs=[spec],
        out_specs=spec,
        scratch_shapes=[pltpu.VMEM((POOL_GROUPS, SEQ, SEQ), BF16)],
        compiler_params=_params(("arbitrary",)),
        name="pool_ctx",
    )(xp)


POOL_HALO = (max(POOL_WINDOWS) // 2) * GRID_W


def _pool_lat_kernel(x_ref, o_ref, pad_ref):
    rows = DEC_SEQ // GRID_W
    p = lax.broadcasted_iota(I32, (DEC_SEQ, POOL_GROUP_DIM), 0)
    r = p // GRID_W
    cidx = p % GRID_W
    zeros = jnp.zeros((POOL_HALO, POOL_GROUP_DIM), F32)
    pad_ref[0:POOL_HALO, :] = zeros
    pad_ref[POOL_HALO + DEC_SEQ:2 * POOL_HALO + DEC_SEQ, :] = zeros
    for gi, w in enumerate(POOL_WINDOWS):
        cs = slice(gi * POOL_GROUP_DIM, (gi + 1) * POOL_GROUP_DIM)
        band = _band(TOK_TILE, w, GRID_W)
        for t in range(LAT_TILES_PER_SEQ):
            hi, lo = _split_bf16(x_ref[t, 0, :, cs])
            pad_ref[POOL_HALO + t * TOK_TILE:POOL_HALO + (t + 1) * TOK_TILE, :] = (
                _dot(band, hi) + _dot(band, lo))
        acc = jnp.zeros((DEC_SEQ, POOL_GROUP_DIM), F32)
        for dr in range(-(w // 2), w // 2):
            start = POOL_HALO + dr * GRID_W
            acc = acc + pad_ref[start:start + DEC_SEQ, :]
        cnt = (_win_count(r, rows, w) * _win_count(cidx, GRID_W, w)).astype(F32)
        pooled = acc / cnt
        for t in range(LAT_TILES_PER_SEQ):
            rs = slice(t * TOK_TILE, (t + 1) * TOK_TILE)
            o_ref[t, 0, :, cs] = pooled[rs] - x_ref[t, 0, :, cs]


def _pool_lat_call(xp):
    view = xp.reshape(N_TILES // DEC_BATCH, DEC_BATCH, TOK_TILE, POOL_W)
    blk = (LAT_TILES_PER_SEQ, 1, TOK_TILE, POOL_W)
    out = pl.pallas_call(
        _pool_lat_kernel,
        out_shape=jax.ShapeDtypeStruct((LAT_TILES_PER_SEQ, DEC_BATCH, TOK_TILE, POOL_W), F32),
        grid=(DEC_BATCH,),
        in_specs=[pl.BlockSpec(blk, lambda s: (CTX_TILES // DEC_BATCH // LAT_TILES_PER_SEQ, s, 0, 0))],
        out_specs=pl.BlockSpec(blk, lambda s: (0, s, 0, 0)),
        scratch_shapes=[pltpu.VMEM((DEC_SEQ + 2 * POOL_HALO, POOL_GROUP_DIM), F32)],
        compiler_params=_params(("arbitrary",)),
        name="pool_lat",
    )(view)
    return out.reshape(N_LAT, POOL_W)


def _post_kernel(xc_ref, xl_ref, mod_ref, of_ref, ob_ref, og_ref, pc_ref, pl_ref, mg_ref, gng_ref,
                 wpg_ref, psc_ref, wbg_ref, wbp_ref, wout_ref, n2g_ref, wr_ref, br_ref,
                 x1_ref, h2_ref, idx_ref, tw_ref):
    t = pl.program_id(0)
    is_ctx = t < CTX_TILES
    x = jnp.where(is_ctx, xc_ref[...], xl_ref[...])
    pooled = jnp.where(is_ctx, pc_ref[...], pl_ref[...])
    mod = mod_ref[0]
    gate1 = mod[:, 2 * D_MODEL:3 * D_MODEL]
    shift2 = mod[:, 3 * D_MODEL:4 * D_MODEL]
    scale2 = mod[:, 4 * D_MODEL:5 * D_MODEL]

    o = of_ref[...].astype(F32) + ob_ref[...].astype(F32)
    og = og_ref[...].astype(F32)
    gated = []
    for h in range(GLA_HEADS):
        vs = slice(h * GLA_DV, (h + 1) * GLA_DV)
        oh = _rms(o[:, vs]) * gng_ref[:, vs]
        gated.append((oh * og[:, vs]).astype(BF16))
    br_gla = _dot(jnp.concatenate(gated, axis=-1), wbg_ref[...])

    pm = []
    for gi in range(POOL_GROUPS):
        cs = slice(gi * POOL_GROUP_DIM, (gi + 1) * POOL_GROUP_DIM)
        pmg = _dot(pooled[:, cs].astype(BF16), wpg_ref[gi]) * psc_ref[:, cs]
        pm.append(pmg.astype(BF16))
    br_pool = _dot(jnp.concatenate(pm, axis=-1), wbp_ref[...])

    mg = mg_ref[...].astype(F32)
    merged = mg[:, 0:D_MODEL] * br_gla + mg[:, D_MODEL:MG_W] * br_pool
    m = _dot(merged.astype(BF16), wout_ref[...])
    x1 = x + gate1 * m
    x1_ref[...] = x1
    h2 = _rms(x1) * n2g_ref[...]
    h2 = h2 * (1.0 + scale2) + shift2
    h2_ref[...] = _pack_halves(h2[:, :HALF_W], h2[:, HALF_W:])

    logits = _dot3(h2, wr_ref[...]) + br_ref[...]
    lane = lax.broadcasted_iota(I32, (TOK_TILE, LANES), 1)
    lane_f = lane.astype(F32)
    neg = jnp.float32(-jnp.inf)
    cur = jnp.where(lane < N_EXPERTS, logits, neg)
    vals, idxs = [], []
    for _ in range(TOP_K):
        mx = jnp.max(cur, axis=-1, keepdims=True)
        ix = jnp.min(jnp.where(cur == mx, lane_f, float(LANES)), axis=-1, keepdims=True)
        vals.append(mx)
        idxs.append(ix)
        cur = jnp.where(lane_f == ix, neg, cur)
    ex = [jnp.exp(vv - vals[0]) for vv in vals]
    tot = ex[0] + ex[1] + ex[2] + ex[3]
    idx_out = jnp.zeros((TOK_TILE, LANES), F32)
    w_out = jnp.zeros((TOK_TILE, LANES), F32)
    for kk in range(TOP_K):
        idx_out = jnp.where(lane == kk, idxs[kk], idx_out)
        w_out = jnp.where(lane == kk, ex[kk] / tot, w_out)
    idx_ref[...] = idx_out.astype(I32)
    tw_ref[...] = w_out


def _post_call(x_ctx, x_lat, mod3, o_f, o_b, og, pooled_c, pooled_l, mg, gng, wpg, psc, wbg, wbp,
               wout, n2g, wr, br):
    row = lambda t: (t, 0)
    const = lambda t: (0, 0)
    stored = lambda t: (_store_tile(t), 0)
    return pl.pallas_call(
        _post_kernel,
        out_shape=[
            jax.ShapeDtypeStruct((N_TOK, D_MODEL), F32),
            jax.ShapeDtypeStruct((N_TOK, HALF_W), I32),
            jax.ShapeDtypeStruct((N_TOK, LANES), I32),
            jax.ShapeDtypeStruct((N_TOK, LANES), F32),
        ],
        grid=(N_TILES,),
        in_specs=[
            pl.BlockSpec((TOK_TILE, D_MODEL), lambda t: (_ctx_tile(t), 0)),
            pl.BlockSpec((TOK_TILE, D_MODEL), lambda t: (_lat_tile(t), 0)),
            pl.BlockSpec((1, 1, N_MOD * D_MODEL), lambda t: (_mod_row(t), 0, 0)),
            pl.BlockSpec((TOK_TILE, V_W), stored),
            pl.BlockSpec((TOK_TILE, V_W), stored),
            pl.BlockSpec((TOK_TILE, V_W), stored),
            pl.BlockSpec((TOK_TILE, POOL_W), lambda t: (_ctx_tile(t), 0)),
            pl.BlockSpec((TOK_TILE, POOL_W),
                         lambda t: (jnp.maximum(_store_tile(t) - CTX_TILES, 0), 0)),
            pl.BlockSpec((TOK_TILE, MG_W), stored),
            pl.BlockSpec((1, V_W), const),
            pl.BlockSpec((POOL_GROUPS, POOL_GROUP_DIM, POOL_GROUP_DIM), lambda t: (0, 0, 0)),
            pl.BlockSpec((1, POOL_W), const),
            pl.BlockSpec((V_W, D_MODEL), const),
            pl.BlockSpec((POOL_W, D_MODEL), const),
            pl.BlockSpec((D_MODEL, D_MODEL), const),
            pl.BlockSpec((1, D_MODEL), const),
            pl.BlockSpec((D_MODEL, LANES), const),
            pl.BlockSpec((1, LANES), const),
        ],
        out_specs=[
            pl.BlockSpec((TOK_TILE, D_MODEL), row),
            pl.BlockSpec((TOK_TILE, HALF_W), row),
            pl.BlockSpec((TOK_TILE, LANES), row),
            pl.BlockSpec((TOK_TILE, LANES), row),
        ],
        compiler_params=_params(("arbitrary",)),
        name="post",
    )(x_ctx, x_lat, mod3, o_f, o_b, og, pooled_c, pooled_l, mg, gng, wpg, psc, wbg, wbp, wout, n2g,
      wr, br)


def _route_kernel(idx_ref, rank_ref, cnt_ref, carry_ref, strict_ref):
    t = pl.program_id(0)

    @pl.when(t == 0)
    def _():
        carry_ref[...] = jnp.zeros((1, LANES), F32)
        row = lax.broadcasted_iota(I32, (ROUTE_TILE, ROUTE_TILE), 0)
        col = lax.broadcasted_iota(I32, (ROUTE_TILE, ROUTE_TILE), 1)
        strict_ref[...] = jnp.where(col < row, 1.0, 0.0).astype(BF16)

    idx = idx_ref[...]
    lane = lax.broadcasted_iota(I32, (ROUTE_TILE, LANES), 1)
    sel = [lane == idx[:, kk:kk + 1] for kk in range(TOP_K)]
    onehot = jnp.zeros((ROUTE_TILE, LANES), F32)
    for kk in range(TOP_K):
        onehot = onehot + jnp.where(sel[kk], 1.0, 0.0)
    before = _dot(strict_ref[...], onehot.astype(BF16)) + carry_ref[...]
    rank = jnp.zeros((ROUTE_TILE, LANES), F32)
    for kk in range(TOP_K):
        rk = jnp.sum(jnp.where(sel[kk], before, 0.0), axis=-1, keepdims=True)
        rank = jnp.where(lane == kk, rk, rank)
    rank_ref[...] = rank.astype(I32)
    carry_ref[...] = carry_ref[...] + jnp.sum(onehot, axis=0, keepdims=True)
    cnt_ref[...] = jnp.broadcast_to(carry_ref[...], (8, LANES))


def _route_call(idx):
    return pl.pallas_call(
        _route_kernel,
        out_shape=[
            jax.ShapeDtypeStruct((N_TOK, LANES), I32),
            jax.ShapeDtypeStruct((8, LANES), F32),
        ],
        grid=(N_TOK // ROUTE_TILE,),
        in_specs=[pl.BlockSpec((ROUTE_TILE, LANES), lambda t: (t, 0))],
        out_specs=[
            pl.BlockSpec((ROUTE_TILE, LANES), lambda t: (t, 0)),
            pl.BlockSpec((8, LANES), lambda t: (0, 0)),
        ],
        scratch_shapes=[pltpu.VMEM((1, LANES), F32), pltpu.VMEM((ROUTE_TILE, ROUTE_TILE), BF16)],
        compiler_params=_params(("arbitrary",)),
        name="route",
    )(idx)


def _moe_kernel(be_ref, nu_ref, ne_ref, par_ref, x_ref, wg_hbm, bg_ref, wu_hbm, bu_ref, wd_hbm,
                bd_ref, y_ref, wf_ref, wgu_ref, wdb_ref, sems):
    b = pl.program_id(0)
    n_used = nu_ref[0]
    e = be_ref[b]
    prev = be_ref[jnp.maximum(b - 1, 0)]
    live = b < n_used

    def weight_copies(expert, slot):
        return [pltpu.make_async_copy(w.at[expert], wf_ref.at[slot, i], sems.at[slot])
                for i, w in enumerate((wg_hbm, wu_hbm, wd_hbm))]

    @pl.when(live & ((b == 0) | (e != prev)))
    def _():
        slot = par_ref[b]

        @pl.when(b == 0)
        def _():
            for cp in weight_copies(e, slot):
                cp.start()

        for cp in weight_copies(e, slot):
            cp.wait()
        nxt = ne_ref[b]

        @pl.when(nxt >= 0)
        def _():
            for cp in weight_copies(nxt, 1 - slot):
                cp.start()

        wgu_ref[:, :D_FF] = wf_ref[slot, 0].astype(BF16)
        wgu_ref[:, D_FF:] = wf_ref[slot, 1].astype(BF16)
        wdb_ref[...] = wf_ref[slot, 2].astype(BF16)

    @pl.when(live)
    def _():
        x_lo, x_hi = _unpack_halves(x_ref[...])
        x = jnp.concatenate([x_lo.astype(BF16), x_hi.astype(BF16)], axis=-1)
        gu = _dot(x, wgu_ref[...])
        gate = jnp.minimum(gu[:, :D_FF] + bg_ref[0], SWIGLU_LIMIT)
        up = jnp.clip(gu[:, D_FF:] + bu_ref[0], -SWIGLU_LIMIT, SWIGLU_LIMIT)
        act = (up + 1.0) * (gate * _sigmoid(SWIGLU_ALPHA * gate))
        y = _dot(act.astype(BF16), wdb_ref[...]) + bd_ref[0]
        y_ref[...] = _pack_halves(y[:, :HALF_W], y[:, HALF_W:])

    @pl.when(jnp.logical_not(live))
    def _():
        y_ref[...] = jnp.zeros((MOE_BLOCK, HALF_W), I32)


def _moe_call(block_e, n_used, next_e, parity, hs, w_gate, b_gate, w_up, b_up, w_down, b_down):
    def blk(b, be, nu, ne, par):
        return jnp.minimum(b, nu[0] - 1)

    row = lambda b, be, nu, ne, par: (blk(b, be, nu, ne, par), 0)
    bsel = lambda b, be, nu, ne, par: (be[blk(b, be, nu, ne, par)], 0, 0)
    any_spec = pl.BlockSpec(memory_space=pl.ANY)
    assert D_MODEL == D_FF
    return pl.pallas_call(
        _moe_kernel,
        out_shape=jax.ShapeDtypeStruct((N_SLOTS, HALF_W), I32),
        grid_spec=pltpu.PrefetchScalarGridSpec(
            num_scalar_prefetch=4,
            grid=(N_SLOT_BLOCKS,),
            in_specs=[
                pl.BlockSpec((MOE_BLOCK, HALF_W), row),
                any_spec,
                pl.BlockSpec((1, 1, D_FF), bsel),
                any_spec,
                pl.BlockSpec((1, 1, D_FF), bsel),
                any_spec,
                pl.BlockSpec((1, 1, D_MODEL), bsel),
            ],
            out_specs=pl.BlockSpec((MOE_BLOCK, HALF_W), lambda b, be, nu, ne, par: (b, 0)),
            scratch_shapes=[
                pltpu.VMEM((2, 3, D_MODEL, D_FF), F32),
                pltpu.VMEM((D_MODEL, 2 * D_FF), BF16),
                pltpu.VMEM((D_FF, D_MODEL), BF16),
                pltpu.SemaphoreType.DMA((2,)),
            ],
        ),
        compiler_params=_params(("arbitrary",)),
        name="moe",
    )(block_e, n_used, next_e, parity, hs, w_gate, b_gate, w_up, b_up, w_down, b_down)


SC_CORES = 2
SC_SUBCORES = 16
SC_WORKERS = SC_CORES * SC_SUBCORES
SC_ROWS = 128
COMBINE_CHUNKS = 2
COMBINE_TILE = 2 * TOK_TILE
assert N_CTX == N_LAT


def _sc_gather_rows(table, idx):
    n_idx = idx.shape[0]
    width = table.shape[1]
    rows = SC_ROWS // 2
    per_worker = n_idx // SC_WORKERS
    n_chunks = per_worker // rows
    assert n_chunks * rows * SC_WORKERS == n_idx and n_chunks >= 2
    mesh = plsc.VectorSubcoreMesh(core_axis_name="c", subcore_axis_name="s")

    @functools.partial(
        pl.kernel, mesh=mesh,
        out_type=jax.ShapeDtypeStruct((n_idx, width), table.dtype),
        scratch_types=[pltpu.VMEM((2, rows), I32), pltpu.VMEM((2, rows, width), table.dtype),
                       pltpu.SemaphoreType.DMA((2,)), pltpu.SemaphoreType.DMA((2,))],
        name="sc_gather",
    )
    def gather(table_hbm, idx_hbm, out_hbm, idx_v, rows_v, sem_g, sem_w):
        worker = lax.axis_index("s") * SC_CORES + lax.axis_index("c")
        base = worker * per_worker

        def chunk_rows(ch):
            return pl.ds(pl.multiple_of(base + ch * rows, rows), rows)

        def start_gather(ch, b):
            pltpu.sync_copy(idx_hbm.at[chunk_rows(ch)], idx_v.at[b])
            pltpu.async_copy(table_hbm.at[idx_v.at[b]], rows_v.at[b], sem_g.at[b])

        def wait_gather(b):
            pltpu.make_async_copy(table_hbm.at[pl.ds(0, rows)], rows_v.at[b], sem_g.at[b]).wait()

        def write_copy(ch, b):
            return pltpu.make_async_copy(rows_v.at[b], out_hbm.at[chunk_rows(ch)], sem_w.at[b])

        start_gather(0, 0)
        for ch in range(n_chunks):
            b = ch % 2
            if ch + 1 < n_chunks:
                if ch >= 1:
                    write_copy(ch - 1, 1 - b).wait()
                start_gather(ch + 1, 1 - b)
            wait_gather(b)
            write_copy(ch, b).start()
        write_copy(n_chunks - 2, n_chunks % 2).wait()
        write_copy(n_chunks - 1, (n_chunks - 1) % 2).wait()

    return gather(table, idx)


def _sc_scatter_rows(rows, idx3, n_out):
    n_rows, width = rows.shape
    n_chunks = n_rows // SC_ROWS // SC_WORKERS
    assert n_chunks * SC_ROWS * SC_WORKERS == n_rows and idx3.shape == (n_rows // SC_ROWS, TOP_K, SC_ROWS)
    mesh = plsc.VectorSubcoreMesh(core_axis_name="c", subcore_axis_name="s")

    @functools.partial(
        pl.kernel, mesh=mesh,
        out_type=jax.ShapeDtypeStruct((n_out, width), rows.dtype),
        scratch_types=[pltpu.VMEM((TOP_K, SC_ROWS), I32), pltpu.VMEM((SC_ROWS, width), rows.dtype),
                       pltpu.SemaphoreType.DMA],
        name="sc_scatter",
    )
    def scatter(rows_hbm, idx_hbm, out_hbm, idx_v, rows_v, sem):
        worker = lax.axis_index("s") * SC_CORES + lax.axis_index("c")

        @pl.loop(0, n_chunks)
        def _(ch):
            chunk = worker * n_chunks + ch
            pltpu.sync_copy(idx_hbm.at[chunk], idx_v)
            pltpu.sync_copy(rows_hbm.at[pl.ds(pl.multiple_of(chunk * SC_ROWS, SC_ROWS), SC_ROWS)],
                            rows_v)
            for kk in range(TOP_K):
                pltpu.async_copy(rows_v, out_hbm.at[idx_v.at[kk]], sem).wait()

    return scatter(rows, idx3)


def _combine_kernel(x1_ref, mod_ref, tw_ref, fg_ref, g_ref, *rest):
    out_ref = rest[-1]
    tw = tw_ref[...]
    f_lo = jnp.zeros((COMBINE_TILE, HALF_W), F32)
    f_hi = jnp.zeros((COMBINE_TILE, HALF_W), F32)
    for kk in range(TOP_K):
        lo, hi = _unpack_halves(g_ref[kk])
        f_lo = f_lo + lo * tw[:, kk:kk + 1]
        f_hi = f_hi + hi * tw[:, kk:kk + 1]
    gate2 = mod_ref[0][:, 5 * D_MODEL:6 * D_MODEL]
    x2 = x1_ref[...] + gate2 * jnp.concatenate([f_lo, f_hi], axis=-1)
    out_ref[...] = _rms(x2) * fg_ref[...]


def _combine_call(x1, mod3, tw, final_g, gathered, partial, tile0, out_tile0, out_tiles, name):
    n_tiles = gathered.shape[1] // COMBINE_TILE
    in_specs = [
        pl.BlockSpec((COMBINE_TILE, D_MODEL), lambda t: (tile0 + t, 0)),
        pl.BlockSpec((1, 1, N_MOD * D_MODEL), lambda t: (_mod_row((tile0 + t) * (COMBINE_TILE // TOK_TILE)), 0, 0)),
        pl.BlockSpec((COMBINE_TILE, LANES), lambda t: (tile0 + t, 0)),
        pl.BlockSpec((1, D_MODEL), lambda t: (0, 0)),
        pl.BlockSpec((TOP_K, COMBINE_TILE, HALF_W), lambda t: (0, t, 0)),
    ]
    args = [x1, mod3, tw, final_g, gathered]
    aliases = {}
    if partial is not None:
        in_specs.append(pl.BlockSpec(memory_space=pl.ANY))
        args.append(partial)
        aliases = {len(args) - 1: 0}
    return pl.pallas_call(
        _combine_kernel,
        out_shape=jax.ShapeDtypeStruct((out_tiles * COMBINE_TILE, D_MODEL), F32),
        grid=(n_tiles,),
        in_specs=in_specs,
        out_specs=pl.BlockSpec((COMBINE_TILE, D_MODEL), lambda t: (out_tile0 + t, 0)),
        input_output_aliases=aliases,
        compiler_params=_params(("arbitrary",)),
        name=name,
    )(*args)


def kernel(x_prompt, x_sample, state_gla_fwd, state_gla_bwd, c, c_ctx, norm1_g, w_mod, b_mod, w_in,
           w_alpha, b_alpha, gla_norm_g, w_pool_grp, pool_scale, w_branch_gla, w_branch_pool, w_out,
           norm2_g, w_router, b_router, w_gate, b_gate, w_up, b_up, w_down, b_down, final_norm_g):
    l = 0
    x_ctx = x_prompt.reshape(N_CTX, D_MODEL)
    x_lat = x_sample.reshape(N_LAT, D_MODEL)

    cvec = jnp.concatenate([c_ctx[None, :], c, jnp.zeros((8 - 1 - DEC_BATCH, D_MODEL), F32)], axis=0)
    mod = _mod_call(cvec, w_mod[l], b_mod[l][None, :])
    mod3 = mod.reshape(8, 1, N_MOD * D_MODEL)

    w_in_b = w_in[l].astype(BF16)
    w_main = w_in_b[:, :MAIN_W]
    w_alr = w_in_b[:, MAIN_W:MAIN_W + ALR_W]
    w_xp = w_in_b[:, MAIN_W + ALR_W:MAIN_W + ALR_W + POOL_W]
    w_mg = w_in_b[:, MAIN_W + ALR_W + POOL_W:]
    q, k, v, og, alr, xp, mg = _inproj_call(x_ctx, x_lat, mod3, norm1_g[l][None, :],
                                            w_main, w_alr, w_xp, w_mg)

    zpad = jnp.zeros((GLA_LOWRANK, QK_W), F32)
    wa_f = jnp.concatenate([w_alpha[l, 0], zpad], axis=0)
    wa_b = jnp.concatenate([zpad, w_alpha[l, 1]], axis=0)
    o_f, o_b, s_f, s_b = _gla_call(q, k, v, alr, wa_f, b_alpha[l, 0][None, :], wa_b,
                                   b_alpha[l, 1][None, :], state_gla_fwd[:, l], state_gla_bwd[:, l])

    pooled_c = _pool_ctx_call(xp)
    pooled_l = _pool_lat_call(xp)

    w_router_pad = jnp.pad(w_router[l], ((0, 0), (0, LANES - N_EXPERTS)))
    b_router_pad = jnp.pad(b_router[l], (0, LANES - N_EXPERTS))[None, :]
    x1, h2, top_idx, top_w = _post_call(
        x_ctx, x_lat, mod3, o_f, o_b, og, pooled_c, pooled_l, mg,
        gla_norm_g[l].reshape(1, V_W), w_pool_grp[l].astype(BF16), pool_scale[l][None, :],
        w_branch_gla[l].astype(BF16), w_branch_pool[l].astype(BF16), w_out[l].astype(BF16),
        norm2_g[l][None, :], w_router_pad, b_router_pad)

    rank, cnt = _route_call(top_idx)
    counts = cnt[0, :N_EXPERTS].astype(I32)
    padded = (counts + MOE_BLOCK - 1) // MOE_BLOCK * MOE_BLOCK
    pad_end = jnp.cumsum(padded).astype(I32)
    pad_start = pad_end - padded
    block_first = jnp.arange(N_SLOT_BLOCKS, dtype=I32) * MOE_BLOCK
    block_e = jnp.minimum(jnp.sum((pad_end[None, :] <= block_first[:, None]).astype(I32), axis=1),
                          N_EXPERTS - 1).astype(I32)
    n_used = (pad_end[-1:] // MOE_BLOCK).astype(I32)
    run_start = jnp.concatenate([jnp.ones((1,), I32), (block_e[1:] != block_e[:-1]).astype(I32)])
    parity = ((jnp.cumsum(run_start) - 1) % 2).astype(I32)
    after = pad_end[block_e] // MOE_BLOCK
    next_e = jnp.where(after < n_used[0], block_e[jnp.minimum(after, N_SLOT_BLOCKS - 1)], -1).astype(I32)
    experts = jnp.arange(N_EXPERTS, dtype=I32)
    tk = top_idx[:, :TOP_K]
    pos = jnp.sum(jnp.where(tk[:, :, None] == experts, pad_start, 0), axis=-1) + rank[:, :TOP_K]
    pos = pos.astype(I32)
    pos_by_choice = pos.T
    pos_chunks = pos_by_choice.reshape(TOP_K, N_TOK // SC_ROWS, SC_ROWS).transpose(1, 0, 2)

    hs = _sc_scatter_rows(h2, pos_chunks, N_SLOTS)
    y = _moe_call(block_e, n_used, next_e, parity, hs,
                  w_gate[l], b_gate[l][:, None, :], w_up[l], b_up[l][:, None, :],
                  w_down[l], b_down[l][:, None, :])
    outs = []
    chunk_tok = N_CTX // COMBINE_CHUNKS
    for group, tok0 in (("ctx", 0), ("lat", N_CTX)):
        out = None
        for ci in range(COMBINE_CHUNKS):
            t0 = tok0 + ci * chunk_tok
            idx = pos_by_choice[:, t0:t0 + chunk_tok].reshape(TOP_K * chunk_tok)
            gathered = _sc_gather_rows(y, idx).reshape(TOP_K, chunk_tok, HALF_W)
            out = _combine_call(x1, mod3, top_w, final_norm_g[None, :], gathered, out,
                                t0 // COMBINE_TILE, ci * chunk_tok // COMBINE_TILE, N_CTX // COMBINE_TILE,
                                "combine_%s%d" % (group, ci))
        outs.append(out)
    y_prompt = outs[0].reshape(BATCH, SEQ, D_MODEL)
    y_sample = outs[1].reshape(DEC_BATCH, DEC_SEQ, D_MODEL)
    return (y_prompt, y_sample, s_f[:, None], s_b[:, None])
```

```python
import functools

import jax
import jax.numpy as jnp
from jax import lax
from jax.experimental import pallas as pl
from jax.experimental.pallas import tpu as pltpu
from jax.experimental.pallas import tpu_sc as plsc

F32 = jnp.float32
BF16 = jnp.bfloat16
I32 = jnp.int32

D_MODEL = 1024
BATCH = 32
SEQ = 256
DEC_BATCH = 4
DEC_SEQ = 2048
GRID_W = 64
GLA_HEADS = 4
GLA_DK = 128
GLA_DV = 256
GLA_LOWRANK = 16
GLA_TAU = 16.0
GLA_CHUNK = 64
POOL_GROUPS = 4
POOL_GROUP_DIM = 128
POOL_WINDOWS = (2, 4, 8, 16)
N_EXPERTS = 32
TOP_K = 4
D_FF = 1024
SWIGLU_LIMIT = 7.0
SWIGLU_ALPHA = 1.702
MOE_BLOCK = 256
NORM_EPS = 1e-6
N_MOD = 6

QK_W = GLA_HEADS * GLA_DK
V_W = GLA_HEADS * GLA_DV
POOL_W = POOL_GROUPS * POOL_GROUP_DIM
MAIN_W = 2 * QK_W + 2 * V_W
ALR_W = 2 * GLA_LOWRANK
MG_W = 2 * D_MODEL

N_CTX = BATCH * SEQ
N_LAT = DEC_BATCH * DEC_SEQ
N_TOK = N_CTX + N_LAT
N_SLOT_BLOCKS = -(-(N_TOK * TOP_K + N_EXPERTS * (MOE_BLOCK - 1)) // MOE_BLOCK)
N_SLOTS = N_SLOT_BLOCKS * MOE_BLOCK

LANES = 128
TOK_TILE = 256
N_TILES = N_TOK // TOK_TILE
CTX_TILES = N_CTX // TOK_TILE
LAT_TILES_PER_SEQ = DEC_SEQ // TOK_TILE
ROUTE_TILE = 512
VMEM_LIMIT = 56 * 1024 * 1024

GLA_SEQS = 4
CTX_CHUNKS = SEQ // GLA_CHUNK
LAT_CHUNKS = DEC_SEQ // GLA_CHUNK
CHUNKS_PER_TILE = TOK_TILE // GLA_CHUNK
GLA_CTX_STEPS = (BATCH // GLA_SEQS) * CTX_CHUNKS
TILE_GRID = 8

NT_DIMS = (((1,), (1,)), ((), ()))
TN_DIMS = (((0,), (0,)), ((), ()))

assert DEC_BATCH == GLA_SEQS and SEQ == TOK_TILE and N_TILES == TILE_GRID * TILE_GRID


def _params(semantics, vmem=VMEM_LIMIT):
    return pltpu.CompilerParams(dimension_semantics=semantics, vmem_limit_bytes=vmem)


def _split_bf16(a):
    hi = a.astype(BF16)
    lo = (a - hi.astype(F32)).astype(BF16)
    return hi, lo


def _dot(a, b):
    return jnp.dot(a, b, preferred_element_type=F32)


def _dot3(a, b):
    a_hi, a_lo = _split_bf16(a)
    b_hi, b_lo = _split_bf16(b)
    return _dot(a_hi, b_hi) + _dot(a_lo, b_hi) + _dot(a_hi, b_lo)


def _dot3_short(a, b):
    a_hi, a_lo = _split_bf16(a)
    b_hi, b_lo = _split_bf16(b)
    return _dot(jnp.concatenate([a_hi, a_lo, a_hi], axis=1), jnp.concatenate([b_hi, b_hi, b_lo], axis=0))


def _sigmoid(x):
    return 1.0 / (1.0 + jnp.exp(-x))


HALF_W = D_MODEL // 2
HIGH_HALF_MASK = -65536


def _pack_halves(lo, hi):
    lo_bits = pltpu.bitcast(lo.astype(BF16).astype(F32), I32)
    hi_bits = pltpu.bitcast(hi.astype(BF16).astype(F32), I32)
    return lax.shift_right_logical(lo_bits, 16) | (hi_bits & HIGH_HALF_MASK)


def _unpack_halves(words):
    lo = pltpu.bitcast(lax.shift_left(words, 16), F32)
    hi = pltpu.bitcast(words & HIGH_HALF_MASK, F32)
    return lo, hi


def _rms(x):
    return x * lax.rsqrt(jnp.mean(x * x, axis=-1, keepdims=True) + NORM_EPS)


def _mod_row(t):
    return jnp.where(t < CTX_TILES, 0, 1 + (t - CTX_TILES) // LAT_TILES_PER_SEQ)


def _store_tile(t):
    u = t - CTX_TILES
    return jnp.where(t < CTX_TILES, t,
                     CTX_TILES + DEC_BATCH * (u % LAT_TILES_PER_SEQ) + u // LAT_TILES_PER_SEQ)


def _ctx_tile(t):
    return jnp.minimum(t, CTX_TILES - 1)


def _lat_tile(t):
    return jnp.maximum(t - CTX_TILES, 0)


def _mod_kernel(c_ref, w_ref, b_ref, o_ref):
    c = c_ref[...]
    o_ref[...] = _dot3(c * _sigmoid(c), w_ref[...]) + b_ref[...]


def _mod_call(cvec, w_mod, b_mod):
    rows = cvec.shape[0]
    return pl.pallas_call(
        _mod_kernel,
        out_shape=jax.ShapeDtypeStruct((rows, N_MOD * D_MODEL), F32),
        grid=(N_MOD,),
        in_specs=[
            pl.BlockSpec((rows, D_MODEL), lambda j: (0, 0)),
            pl.BlockSpec((D_MODEL, D_MODEL), lambda j: (0, j)),
            pl.BlockSpec((1, D_MODEL), lambda j: (0, j)),
        ],
        out_specs=pl.BlockSpec((rows, D_MODEL), lambda j: (0, j)),
        compiler_params=_params(("arbitrary",)),
        name="mod",
    )(cvec, w_mod, b_mod)


def _inproj_kernel(xc_ref, xl_ref, mod_ref, g_ref, wmain_ref, walr_ref, wxp_ref, wmg_ref,
                   q_ref, k_ref, v_ref, og_ref, alr_ref, xp_ref, mg_ref):
    t = pl.program_id(0)
    x = jnp.where(t < CTX_TILES, xc_ref[...], xl_ref[...])
    mod = mod_ref[0]
    shift1 = mod[:, 0:D_MODEL]
    scale1 = mod[:, D_MODEL:2 * D_MODEL]
    h = _rms(x) * g_ref[...]
    h = (h * (1.0 + scale1) + shift1).astype(BF16)
    z = _dot(h, wmain_ref[...])
    q_ref[...] = (z[:, 0:QK_W] * (GLA_DK ** -0.5)).astype(BF16)
    k_ref[...] = z[:, QK_W:2 * QK_W].astype(BF16)
    v_ref[...] = z[:, 2 * QK_W:2 * QK_W + V_W].astype(BF16)
    og = z[:, 2 * QK_W + V_W:MAIN_W]
    og_ref[...] = (og * _sigmoid(og)).astype(BF16)
    alr_ref[...] = _dot(h, walr_ref[...])
    xp_ref[...] = _dot(h, wxp_ref[...])
    mg_ref[...] = _sigmoid(_dot(h, wmg_ref[...])).astype(BF16)


def _inproj_call(x_ctx, x_lat, mod3, norm1_g, w_main, w_alr, w_xp, w_mg):
    const = lambda t: (0, 0)
    stored = lambda t: (_store_tile(t), 0)
    widths = (QK_W, QK_W, V_W, V_W, ALR_W, POOL_W, MG_W)
    dtypes = (BF16, BF16, BF16, BF16, F32, F32, BF16)
    return pl.pallas_call(
        _inproj_kernel,
        out_shape=[jax.ShapeDtypeStruct((N_TOK, w), dt) for w, dt in zip(widths, dtypes)],
        grid=(N_TILES,),
        in_specs=[
            pl.BlockSpec((TOK_TILE, D_MODEL), lambda t: (_ctx_tile(t), 0)),
            pl.BlockSpec((TOK_TILE, D_MODEL), lambda t: (_lat_tile(t), 0)),
            pl.BlockSpec((1, 1, N_MOD * D_MODEL), lambda t: (_mod_row(t), 0, 0)),
            pl.BlockSpec((1, D_MODEL), const),
            pl.BlockSpec((D_MODEL, MAIN_W), const),
            pl.BlockSpec((D_MODEL, ALR_W), const),
            pl.BlockSpec((D_MODEL, POOL_W), const),
            pl.BlockSpec((D_MODEL, MG_W), const),
        ],
        out_specs=[pl.BlockSpec((TOK_TILE, w), stored) for w in widths],
        compiler_params=_params(("arbitrary",)),
        name="inproj",
    )(x_ctx, x_lat, mod3, norm1_g, w_main, w_alr, w_xp, w_mg)


def _gla_direction(q_ref, k_ref, v_ref, alr_ref, wa_ref, ba_ref, o_ref, st_ref, d, rev):
    rows = GLA_SEQS * GLA_CHUNK
    stack = lambda ref, cols: jnp.concatenate([ref[0, s, :, cols] for s in range(GLA_SEQS)], axis=0)
    alr = stack(alr_ref, slice(None))
    a = _dot3_short(alr, wa_ref[...]) + ba_ref[...]
    g = (jnp.minimum(a, 0.0) - jnp.log(1.0 + jnp.exp(-jnp.abs(a)))) * (1.0 / GLA_TAU)

    row = lax.broadcasted_iota(I32, (rows, rows), 0)
    col = lax.broadcasted_iota(I32, (rows, rows), 1)
    same = (row // GLA_CHUNK) == (col // GLA_CHUNK)
    tri = same & ((col >= row) if rev else (col <= row))
    tri_b = jnp.where(tri, 1.0, 0.0).astype(BF16)
    g_hi, g_lo = _split_bf16(g)
    bcum = _dot(tri_b, g_hi) + _dot(tri_b, g_lo)

    def per_seq_row(r):
        return jnp.concatenate(
            [jnp.broadcast_to(bcum[s * GLA_CHUNK + r:s * GLA_CHUNK + r + 1], (GLA_CHUNK, QK_W))
             for s in range(GLA_SEQS)], axis=0)

    r_last = 0 if rev else GLA_CHUNK - 1
    blast = per_seq_row(r_last)
    bmid = per_seq_row(GLA_CHUNK // 2)
    e_q = jnp.exp(bcum - bmid)
    e_k = jnp.exp(bmid - bcum)
    e_in = jnp.exp(bcum)
    e_out = jnp.exp(blast - bcum)
    q = stack(q_ref, slice(None)).astype(F32)
    k = stack(k_ref, slice(None)).astype(F32)

    wide = (rows, GLA_SEQS * GLA_DK)
    own = (lax.broadcasted_iota(I32, wide, 0) // GLA_CHUNK) == (lax.broadcasted_iota(I32, wide, 1) // GLA_DK)

    def block_diag(x):
        return jnp.where(own, jnp.concatenate([x] * GLA_SEQS, axis=1), 0.0).astype(BF16)

    for h in range(GLA_HEADS):
        ks = slice(h * GLA_DK, (h + 1) * GLA_DK)
        vs = slice(h * GLA_DV, (h + 1) * GLA_DV)
        qh = q[:, ks]
        kh = k[:, ks]
        vh = stack(v_ref, vs)
        att = lax.dot_general((qh * e_q[:, ks]).astype(BF16), (kh * e_k[:, ks]).astype(BF16),
                              NT_DIMS, preferred_element_type=F32)
        att = jnp.where(tri, att, 0.0).astype(BF16)
        st = st_ref[d, h]
        o_inter = lax.dot_general(block_diag(qh * e_in[:, ks]), st.astype(BF16), NT_DIMS,
                                  preferred_element_type=F32)
        o_h = o_inter + _dot(att, vh)
        for s in range(GLA_SEQS):
            o_ref[0, s, :, vs] = o_h[s * GLA_CHUNK:(s + 1) * GLA_CHUNK]
        upd = lax.dot_general(vh, block_diag(kh * e_out[:, ks]), TN_DIMS,
                              preferred_element_type=F32)
        e_last = jnp.concatenate(
            [jnp.exp(bcum[s * GLA_CHUNK + r_last:s * GLA_CHUNK + r_last + 1, ks])
             for s in range(GLA_SEQS)], axis=1)
        st_ref[d, h] = st * e_last + upd


def _gla_kernel(qf_ref, kf_ref, vf_ref, af_ref, qb_ref, kb_ref, vb_ref, ab_ref,
                waf_ref, baf_ref, wab_ref, bab_ref, s0f_ref, s0b_ref,
                of_ref, ob_ref, sf_ref, sb_ref, st_ref):
    i = pl.program_id(0)
    is_ctx = i < GLA_CTX_STEPS
    chunk = jnp.where(is_ctx, i % CTX_CHUNKS, i - GLA_CTX_STEPS)

    @pl.when(is_ctx & (chunk == 0))
    def _():
        st_ref[...] = jnp.zeros(st_ref.shape, F32)

    @pl.when(i == GLA_CTX_STEPS)
    def _():
        for s in range(GLA_SEQS):
            ls = slice(s * GLA_DK, (s + 1) * GLA_DK)
            for h in range(GLA_HEADS):
                st_ref[0, h, :, ls] = s0f_ref[s, h].T
                st_ref[1, h, :, ls] = s0b_ref[s, h].T

    _gla_direction(qf_ref, kf_ref, vf_ref, af_ref, waf_ref, baf_ref, of_ref, st_ref, 0, False)
    _gla_direction(qb_ref, kb_ref, vb_ref, ab_ref, wab_ref, bab_ref, ob_ref, st_ref, 1, True)

    @pl.when(is_ctx & (chunk == CTX_CHUNKS - 1))
    def _():
        for s in range(GLA_SEQS):
            ls = slice(s * GLA_DK, (s + 1) * GLA_DK)
            for h in range(GLA_HEADS):
                sf_ref[s, h] = st_ref[0, h, :, ls].T
                sb_ref[s, h] = st_ref[1, h, :, ls].T


def _gla_block(i, rev):
    is_ctx = i < GLA_CTX_STEPS
    group = i // CTX_CHUNKS
    c_ctx = i % CTX_CHUNKS
    c_lat = i - GLA_CTX_STEPS
    if rev:
        c_ctx = CTX_CHUNKS - 1 - c_ctx
        c_lat = LAT_CHUNKS - 1 - c_lat
    j = c_lat // CHUNKS_PER_TILE
    per_row = TILE_GRID // GLA_SEQS
    a = jnp.where(is_ctx, group // per_row, CTX_TILES // TILE_GRID + j // per_row)
    b = jnp.where(is_ctx, group % per_row, j % per_row)
    c = jnp.where(is_ctx, c_ctx, c_lat % CHUNKS_PER_TILE)
    return (a, b, c, 0)


def _gla_call(q, k, v, alr, wa_f, ba_f, wa_b, ba_b, s0_f, s0_b):
    def view(arr):
        return arr.reshape(TILE_GRID, TILE_GRID, TOK_TILE, arr.shape[-1])

    def spec(width, rev):
        return pl.BlockSpec((1, GLA_SEQS, GLA_CHUNK, width), lambda i: _gla_block(i, rev))

    const = lambda i: (0, 0)
    st_block = (GLA_SEQS, GLA_HEADS, GLA_DK, GLA_DV)
    whole_state = pl.BlockSpec(st_block, lambda i: (0, 0, 0, 0))
    ctx_state = pl.BlockSpec(
        st_block, lambda i: (jnp.minimum(i // CTX_CHUNKS, BATCH // GLA_SEQS - 1), 0, 0, 0))
    in_specs = []
    for rev in (False, True):
        in_specs += [spec(QK_W, rev), spec(QK_W, rev), spec(V_W, rev), spec(ALR_W, rev)]
    in_specs += [pl.BlockSpec((ALR_W, QK_W), const), pl.BlockSpec((1, QK_W), const)] * 2
    in_specs += [whole_state, whole_state]
    o_shape = jax.ShapeDtypeStruct((TILE_GRID, TILE_GRID, TOK_TILE, V_W), F32)
    s_shape = jax.ShapeDtypeStruct((BATCH, GLA_HEADS, GLA_DK, GLA_DV), F32)
    qv, kv, vv, av = view(q), view(k), view(v), view(alr)
    o_f, o_b, s_f, s_b = pl.pallas_call(
        _gla_kernel,
        out_shape=[o_shape, o_shape, s_shape, s_shape],
        grid=(GLA_CTX_STEPS + LAT_CHUNKS,),
        in_specs=in_specs,
        out_specs=[spec(V_W, False), spec(V_W, True), ctx_state, ctx_state],
        scratch_shapes=[pltpu.VMEM((2, GLA_HEADS, GLA_DV, GLA_SEQS * GLA_DK), F32)],
        compiler_params=_params(("arbitrary",)),
        name="gla",
    )(qv, kv, vv, av, qv, kv, vv, av, wa_f, ba_f, wa_b, ba_b, s0_f, s0_b)
    return o_f.reshape(N_TOK, V_W), o_b.reshape(N_TOK, V_W), s_f, s_b


def _band(n, w, block):
    row = lax.broadcasted_iota(I32, (n, n), 0)
    col = lax.broadcasted_iota(I32, (n, n), 1)
    inside = (col >= row - w // 2) & (col <= row + w // 2 - 1)
    if block < n:
        inside = inside & ((row // block) == (col // block))
    return jnp.where(inside, 1.0, 0.0).astype(BF16)


def _win_count(p, n, w):
    return jnp.minimum(p + w // 2 - 1, n - 1) - jnp.maximum(p - w // 2, 0) + 1


def _pool_ctx_kernel(x_ref, o_ref, band_ref):
    @pl.when(pl.program_id(0) == 0)
    def _():
        for gi, w in enumerate(POOL_WINDOWS):
            band_ref[gi] = _band(SEQ, w, SEQ)

    p = lax.broadcasted_iota(I32, (SEQ, POOL_GROUP_DIM), 0)
    for gi, w in enumerate(POOL_WINDOWS):
        cs = slice(gi * POOL_GROUP_DIM, (gi + 1) * POOL_GROUP_DIM)
        x = x_ref[:, cs]
        hi, lo = _split_bf16(x)
        band = band_ref[gi]
        s = _dot(band, hi) + _dot(band, lo)
        cnt = _win_count(p, SEQ, w).astype(F32)
        o_ref[:, cs] = s / cnt - x


def _pool_ctx_call(xp):
    spec = pl.BlockSpec((SEQ, POOL_W), lambda b: (b, 0))
    return pl.pallas_call(
        _pool_ctx_kernel,
        out_shape=jax.ShapeDtypeStruct((N_CTX, POOL_W), F32),
        grid=(BATCH,),
        in_specs=[spec],
        out_specs=spec,
        scratch_shapes=[pltpu.VMEM((POOL_GROUPS, SEQ, SEQ), BF16)],
        compiler_params=_params(("arbitrary",)),
        name="pool_ctx",
    )(xp)


POOL_HALO = (max(POOL_WINDOWS) // 2) * GRID_W


def _pool_lat_kernel(x_ref, o_ref, pad_ref):
    rows = DEC_SEQ // GRID_W
    p = lax.broadcasted_iota(I32, (DEC_SEQ, POOL_GROUP_DIM), 0)
    r = p // GRID_W
    cidx = p % GRID_W
    zeros = jnp.zeros((POOL_HALO, POOL_GROUP_DIM), F32)
    pad_ref[0:POOL_HALO, :] = zeros
    pad_ref[POOL_HALO + DEC_SEQ:2 * POOL_HALO + DEC_SEQ, :] = zeros
    for gi, w in enumerate(POOL_WINDOWS):
        cs = slice(gi * POOL_GROUP_DIM, (gi + 1) * POOL_GROUP_DIM)
        band = _band(TOK_TILE, w, GRID_W)
        for t in range(LAT_TILES_PER_SEQ):
            hi, lo = _split_bf16(x_ref[t, 0, :, cs])
            pad_ref[POOL_HALO + t * TOK_TILE:POOL_HALO + (t + 1) * TOK_TILE, :] = (
                _dot(band, hi) + _dot(band, lo))
        acc = jnp.zeros((DEC_SEQ, POOL_GROUP_DIM), F32)
        for dr in range(-(w // 2), w // 2):
            start = POOL_HALO + dr * GRID_W
            acc = acc + pad_ref[start:start + DEC_SEQ, :]
        cnt = (_win_count(r, rows, w) * _win_count(cidx, GRID_W, w)).astype(F32)
        pooled = acc / cnt
        for t in range(LAT_TILES_PER_SEQ):
            rs = slice(t * TOK_TILE, (t + 1) * TOK_TILE)
            o_ref[t, 0, :, cs] = pooled[rs] - x_ref[t, 0, :, cs]


def _pool_lat_call(xp):
    view = xp.reshape(N_TILES // DEC_BATCH, DEC_BATCH, TOK_TILE, POOL_W)
    blk = (LAT_TILES_PER_SEQ, 1, TOK_TILE, POOL_W)
    out = pl.pallas_call(
        _pool_lat_kernel,
        out_shape=jax.ShapeDtypeStruct((LAT_TILES_PER_SEQ, DEC_BATCH, TOK_TILE, POOL_W), F32),
        grid=(DEC_BATCH,),
        in_specs=[pl.BlockSpec(blk, lambda s: (CTX_TILES // DEC_BATCH // LAT_TILES_PER_SEQ, s, 0, 0))],
        out_specs=pl.BlockSpec(blk, lambda s: (0, s, 0, 0)),
        scratch_shapes=[pltpu.VMEM((DEC_SEQ + 2 * POOL_HALO, POOL_GROUP_DIM), F32)],
        compiler_params=_params(("arbitrary",)),
        name="pool_lat",
    )(view)
    return out.reshape(N_LAT, POOL_W)


POST_PARTS = 2
POST_TILE = POST_PARTS * TOK_TILE


def _post_kernel(xc_ref, xl_ref, mod_ref, *refs):
    stored = [refs[5 * p:5 * p + 5] for p in range(POST_PARTS)]
    (pc_ref, gng_ref, wpg_ref, psc_ref, wbg_ref, wbp_ref, wout_ref, n2g_ref, wr_ref, br_ref,
     x1_ref, h2_ref, idx_ref, tw_ref) = refs[5 * POST_PARTS:]
    t = pl.program_id(0)
    is_ctx = t < CTX_TILES // POST_PARTS
    mod = mod_ref[0]
    gate1 = mod[:, 2 * D_MODEL:3 * D_MODEL]
    shift2 = mod[:, 3 * D_MODEL:4 * D_MODEL]
    scale2 = mod[:, 4 * D_MODEL:5 * D_MODEL]
    for p in range(POST_PARTS):
        of_ref, ob_ref, og_ref, pl_ref, mg_ref = stored[p]
        rs = slice(p * TOK_TILE, (p + 1) * TOK_TILE)
        x = jnp.where(is_ctx, xc_ref[rs, :], xl_ref[rs, :])
        pooled = jnp.where(is_ctx, pc_ref[rs, :], pl_ref[...])

        o = of_ref[...] + ob_ref[...]
        og = og_ref[...].astype(F32)
        gated = []
        for h in range(GLA_HEADS):
            vs = slice(h * GLA_DV, (h + 1) * GLA_DV)
            oh = _rms(o[:, vs]) * gng_ref[:, vs]
            gated.append((oh * og[:, vs]).astype(BF16))
        br_gla = _dot(jnp.concatenate(gated, axis=-1), wbg_ref[...])

        pm = []
        for gi in range(POOL_GROUPS):
            cs = slice(gi * POOL_GROUP_DIM, (gi + 1) * POOL_GROUP_DIM)
            pmg = _dot(pooled[:, cs].astype(BF16), wpg_ref[gi]) * psc_ref[:, cs]
            pm.append(pmg.astype(BF16))
        br_pool = _dot(jnp.concatenate(pm, axis=-1), wbp_ref[...])

        mg = mg_ref[...].astype(F32)
        merged = mg[:, 0:D_MODEL] * br_gla + mg[:, D_MODEL:MG_W] * br_pool
        m = _dot(merged.astype(BF16), wout_ref[...])
        x1 = x + gate1 * m
        x1_ref[rs, :] = x1
        h2 = _rms(x1) * n2g_ref[...]
        h2 = h2 * (1.0 + scale2) + shift2
        h2_ref[rs, :] = _pack_halves(h2[:, :HALF_W], h2[:, HALF_W:])

        logits = _dot3(h2, wr_ref[...]) + br_ref[...]
        lane = lax.broadcasted_iota(I32, (TOK_TILE, LANES), 1)
        lane_f = lane.astype(F32)
        neg = jnp.float32(-jnp.inf)
        cur = jnp.where(lane < N_EXPERTS, logits, neg)
        vals, idxs = [], []
        for _ in range(TOP_K):
            mx = jnp.max(cur, axis=-1, keepdims=True)
            ix = jnp.min(jnp.where(cur == mx, lane_f, float(LANES)), axis=-1, keepdims=True)
            vals.append(mx)
            idxs.append(ix)
            cur = jnp.where(lane_f == ix, neg, cur)
        ex = [jnp.exp(vv - vals[0]) for vv in vals]
        tot = ex[0] + ex[1] + ex[2] + ex[3]
        idx_out = jnp.zeros((TOK_TILE, LANES), F32)
        w_out = jnp.zeros((TOK_TILE, LANES), F32)
        for kk in range(TOP_K):
            idx_out = jnp.where(lane == kk, idxs[kk], idx_out)
            w_out = jnp.where(lane == kk, ex[kk] / tot, w_out)
        idx_ref[rs, :] = idx_out.astype(I32)
        tw_ref[rs, :] = w_out


def _post_call(x_ctx, x_lat, mod3, o_f, o_b, og, pooled_c, pooled_l, mg, gng, wpg, psc, wbg, wbp,
               wout, n2g, wr, br):
    n_steps = N_TILES // POST_PARTS
    ctx_steps = CTX_TILES // POST_PARTS
    row = lambda t: (t, 0)
    const = lambda t: (0, 0)
    ctx_blk = lambda t: (jnp.minimum(t, ctx_steps - 1), 0)
    lat_blk = lambda t: (jnp.maximum(t - ctx_steps, 0), 0)
    in_specs = [
        pl.BlockSpec((POST_TILE, D_MODEL), ctx_blk),
        pl.BlockSpec((POST_TILE, D_MODEL), lat_blk),
        pl.BlockSpec((1, 1, N_MOD * D_MODEL), lambda t: (_mod_row(POST_PARTS * t), 0, 0)),
    ]
    args = [x_ctx, x_lat, mod3]
    for p in range(POST_PARTS):
        stored = lambda t, p=p: (_store_tile(POST_PARTS * t + p), 0)
        stored_lat = lambda t, p=p: (jnp.maximum(_store_tile(POST_PARTS * t + p) - CTX_TILES, 0), 0)
        in_specs += [
            pl.BlockSpec((TOK_TILE, V_W), stored),
            pl.BlockSpec((TOK_TILE, V_W), stored),
            pl.BlockSpec((TOK_TILE, V_W), stored),
            pl.BlockSpec((TOK_TILE, POOL_W), stored_lat),
            pl.BlockSpec((TOK_TILE, MG_W), stored),
        ]
        args += [o_f, o_b, og, pooled_l, mg]
    in_specs += [
        pl.BlockSpec((POST_TILE, POOL_W), ctx_blk),
        pl.BlockSpec((1, V_W), const),
        pl.BlockSpec((POOL_GROUPS, POOL_GROUP_DIM, POOL_GROUP_DIM), lambda t: (0, 0, 0)),
        pl.BlockSpec((1, POOL_W), const),
        pl.BlockSpec((V_W, D_MODEL), const),
        pl.BlockSpec((POOL_W, D_MODEL), const),
        pl.BlockSpec((D_MODEL, D_MODEL), const),
        pl.BlockSpec((1, D_MODEL), const),
        pl.BlockSpec((D_MODEL, LANES), const),
        pl.BlockSpec((1, LANES), const),
    ]
    args += [pooled_c, gng, wpg, psc, wbg, wbp, wout, n2g, wr, br]
    return pl.pallas_call(
        _post_kernel,
        out_shape=[
            jax.ShapeDtypeStruct((N_TOK, D_MODEL), F32),
            jax.ShapeDtypeStruct((N_TOK, HALF_W), I32),
            jax.ShapeDtypeStruct((N_TOK, LANES), I32),
            jax.ShapeDtypeStruct((N_TOK, LANES), F32),
        ],
        grid=(n_steps,),
        in_specs=in_specs,
        out_specs=[
            pl.BlockSpec((POST_TILE, D_MODEL), row),
            pl.BlockSpec((POST_TILE, HALF_W), row),
            pl.BlockSpec((POST_TILE, LANES), row),
            pl.BlockSpec((POST_TILE, LANES), row),
        ],
        compiler_params=_params(("arbitrary",)),
        name="post",
    )(*args)


def _route_kernel(idx_ref, rank_ref, cnt_ref, carry_ref, strict_ref):
    t = pl.program_id(0)

    @pl.when(t == 0)
    def _():
        carry_ref[...] = jnp.zeros((1, LANES), F32)
        row = lax.broadcasted_iota(I32, (ROUTE_TILE, ROUTE_TILE), 0)
        col = lax.broadcasted_iota(I32, (ROUTE_TILE, ROUTE_TILE), 1)
        strict_ref[...] = jnp.where(col < row, 1.0, 0.0).astype(BF16)

    idx = idx_ref[...]
    lane = lax.broadcasted_iota(I32, (ROUTE_TILE, LANES), 1)
    sel = [lane == idx[:, kk:kk + 1] for kk in range(TOP_K)]
    onehot = jnp.zeros((ROUTE_TILE, LANES), F32)
    for kk in range(TOP_K):
        onehot = onehot + jnp.where(sel[kk], 1.0, 0.0)
    before = _dot(strict_ref[...], onehot.astype(BF16)) + carry_ref[...]
    rank = jnp.zeros((ROUTE_TILE, LANES), F32)
    for kk in range(TOP_K):
        rk = jnp.sum(jnp.where(sel[kk], before, 0.0), axis=-1, keepdims=True)
        rank = jnp.where(lane == kk, rk, rank)
    rank_ref[...] = rank.astype(I32)
    carry_ref[...] = carry_ref[...] + jnp.sum(onehot, axis=0, keepdims=True)
    cnt_ref[...] = jnp.broadcast_to(carry_ref[...], (8, LANES))


def _route_call(idx):
    return pl.pallas_call(
        _route_kernel,
        out_shape=[
            jax.ShapeDtypeStruct((N_TOK, LANES), I32),
            jax.ShapeDtypeStruct((8, LANES), F32),
        ],
        grid=(N_TOK // ROUTE_TILE,),
        in_specs=[pl.BlockSpec((ROUTE_TILE, LANES), lambda t: (t, 0))],
        out_specs=[
            pl.BlockSpec((ROUTE_TILE, LANES), lambda t: (t, 0)),
            pl.BlockSpec((8, LANES), lambda t: (0, 0)),
        ],
        scratch_shapes=[pltpu.VMEM((1, LANES), F32), pltpu.VMEM((ROUTE_TILE, ROUTE_TILE), BF16)],
        compiler_params=_params(("arbitrary",)),
        name="route",
    )(idx)


def _moe_kernel(be_ref, nu_ref, ne_ref, par_ref, x_ref, wg_hbm, bg_ref, wu_hbm, bu_ref, wd_hbm,
                bd_ref, y_ref, wf_ref, wgu_ref, wdb_ref, sems):
    b = pl.program_id(0)
    n_used = nu_ref[0]
    e = be_ref[b]
    prev = be_ref[jnp.maximum(b - 1, 0)]
    live = b < n_used

    def weight_copies(expert, slot):
        return [pltpu.make_async_copy(w.at[expert], wf_ref.at[slot, i], sems.at[slot])
                for i, w in enumerate((wg_hbm, wu_hbm, wd_hbm))]

    @pl.when(live & ((b == 0) | (e != prev)))
    def _():
        slot = par_ref[b]

        @pl.when(b == 0)
        def _():
            for cp in weight_copies(e, slot):
                cp.start()

        for cp in weight_copies(e, slot):
            cp.wait()
        nxt = ne_ref[b]

        @pl.when(nxt >= 0)
        def _():
            for cp in weight_copies(nxt, 1 - slot):
                cp.start()

        wgu_ref[:, :D_FF] = wf_ref[slot, 0].astype(BF16)
        wgu_ref[:, D_FF:] = wf_ref[slot, 1].astype(BF16)
        wdb_ref[...] = wf_ref[slot, 2].astype(BF16)

    @pl.when(live)
    def _():
        x_lo, x_hi = _unpack_halves(x_ref[...])
        x = jnp.concatenate([x_lo.astype(BF16), x_hi.astype(BF16)], axis=-1)
        gu = _dot(x, wgu_ref[...])
        gate = jnp.minimum(gu[:, :D_FF] + bg_ref[0], SWIGLU_LIMIT)
        up = jnp.clip(gu[:, D_FF:] + bu_ref[0], -SWIGLU_LIMIT, SWIGLU_LIMIT)
        act = (up + 1.0) * (gate * _sigmoid(SWIGLU_ALPHA * gate))
        y = _dot(act.astype(BF16), wdb_ref[...]) + bd_ref[0]
        y_ref[...] = _pack_halves(y[:, :HALF_W], y[:, HALF_W:])

    @pl.when(jnp.logical_not(live))
    def _():
        y_ref[...] = jnp.zeros((MOE_BLOCK, HALF_W), I32)


def _moe_call(block_e, n_used, next_e, parity, hs, w_gate, b_gate, w_up, b_up, w_down, b_down):
    def blk(b, be, nu, ne, par):
        return jnp.minimum(b, nu[0] - 1)

    row = lambda b, be, nu, ne, par: (blk(b, be, nu, ne, par), 0)
    bsel = lambda b, be, nu, ne, par: (be[blk(b, be, nu, ne, par)], 0, 0)
    any_spec = pl.BlockSpec(memory_space=pl.ANY)
    assert D_MODEL == D_FF
    return pl.pallas_call(
        _moe_kernel,
        out_shape=jax.ShapeDtypeStruct((N_SLOTS, HALF_W), I32),
        grid_spec=pltpu.PrefetchScalarGridSpec(
            num_scalar_prefetch=4,
            grid=(N_SLOT_BLOCKS,),
            in_specs=[
                pl.BlockSpec((MOE_BLOCK, HALF_W), row),
                any_spec,
                pl.BlockSpec((1, 1, D_FF), bsel),
                any_spec,
                pl.BlockSpec((1, 1, D_FF), bsel),
                any_spec,
                pl.BlockSpec((1, 1, D_MODEL), bsel),
            ],
            out_specs=pl.BlockSpec((MOE_BLOCK, HALF_W), lambda b, be, nu, ne, par: (b, 0)),
            scratch_shapes=[
                pltpu.VMEM((2, 3, D_MODEL, D_FF), F32),
                pltpu.VMEM((D_MODEL, 2 * D_FF), BF16),
                pltpu.VMEM((D_FF, D_MODEL), BF16),
                pltpu.SemaphoreType.DMA((2,)),
            ],
        ),
        compiler_params=_params(("arbitrary",)),
        name="moe",
    )(block_e, n_used, next_e, parity, hs, w_gate, b_gate, w_up, b_up, w_down, b_down)


SC_CORES = 2
SC_SUBCORES = 16
SC_WORKERS = SC_CORES * SC_SUBCORES
SC_ROWS = 128
COMBINE_CHUNKS = 2
COMBINE_TILE = 2 * TOK_TILE
assert N_CTX == N_LAT


def _sc_gather_rows(table, idx):
    n_idx = idx.shape[0]
    width = table.shape[1]
    rows = SC_ROWS // 2
    per_worker = n_idx // SC_WORKERS
    n_chunks = per_worker // rows
    assert n_chunks * rows * SC_WORKERS == n_idx and n_chunks >= 2
    mesh = plsc.VectorSubcoreMesh(core_axis_name="c", subcore_axis_name="s")

    @functools.partial(
        pl.kernel, mesh=mesh,
        out_type=jax.ShapeDtypeStruct((n_idx, width), table.dtype),
        scratch_types=[pltpu.VMEM((2, rows), I32), pltpu.VMEM((2, rows, width), table.dtype),
                       pltpu.SemaphoreType.DMA((2,)), pltpu.SemaphoreType.DMA((2,))],
        name="sc_gather",
    )
    def gather(table_hbm, idx_hbm, out_hbm, idx_v, rows_v, sem_g, sem_w):
        worker = lax.axis_index("s") * SC_CORES + lax.axis_index("c")
        base = worker * per_worker

        def chunk_rows(ch):
            return pl.ds(pl.multiple_of(base + ch * rows, rows), rows)

        def start_gather(ch, b):
            pltpu.sync_copy(idx_hbm.at[chunk_rows(ch)], idx_v.at[b])
            pltpu.async_copy(table_hbm.at[idx_v.at[b]], rows_v.at[b], sem_g.at[b])

        def wait_gather(b):
            pltpu.make_async_copy(table_hbm.at[pl.ds(0, rows)], rows_v.at[b], sem_g.at[b]).wait()

        def write_copy(ch, b):
            return pltpu.make_async_copy(rows_v.at[b], out_hbm.at[chunk_rows(ch)], sem_w.at[b])

        start_gather(0, 0)
        for ch in range(n_chunks):
            b = ch % 2
            if ch + 1 < n_chunks:
                if ch >= 1:
                    write_copy(ch - 1, 1 - b).wait()
                start_gather(ch + 1, 1 - b)
            wait_gather(b)
            write_copy(ch, b).start()
        write_copy(n_chunks - 2, n_chunks % 2).wait()
        write_copy(n_chunks - 1, (n_chunks - 1) % 2).wait()

    return gather(table, idx)


def _sc_scatter_rows(rows, idx3, n_out):
    n_rows, width = rows.shape
    n_chunks = n_rows // SC_ROWS // SC_WORKERS
    assert n_chunks * SC_ROWS * SC_WORKERS == n_rows and idx3.shape == (n_rows // SC_ROWS, TOP_K, SC_ROWS)
    mesh = plsc.VectorSubcoreMesh(core_axis_name="c", subcore_axis_name="s")

    @functools.partial(
        pl.kernel, mesh=mesh,
        out_type=jax.ShapeDtypeStruct((n_out, width), rows.dtype),
        scratch_types=[pltpu.VMEM((TOP_K, SC_ROWS), I32), pltpu.VMEM((SC_ROWS, width), rows.dtype),
                       pltpu.SemaphoreType.DMA],
        name="sc_scatter",
    )
    def scatter(rows_hbm, idx_hbm, out_hbm, idx_v, rows_v, sem):
        worker = lax.axis_index("s") * SC_CORES + lax.axis_index("c")

        @pl.loop(0, n_chunks)
        def _(ch):
            chunk = worker * n_chunks + ch
            pltpu.sync_copy(idx_hbm.at[chunk], idx_v)
            pltpu.sync_copy(rows_hbm.at[pl.ds(pl.multiple_of(chunk * SC_ROWS, SC_ROWS), SC_ROWS)],
                            rows_v)
            for kk in range(TOP_K):
                pltpu.async_copy(rows_v, out_hbm.at[idx_v.at[kk]], sem).wait()

    return scatter(rows, idx3)


def _combine_kernel(x1_ref, mod_ref, tw_ref, fg_ref, g_ref, *rest):
    out_ref = rest[-1]
    tw = tw_ref[...]
    f_lo = jnp.zeros((COMBINE_TILE, HALF_W), F32)
    f_hi = jnp.zeros((COMBINE_TILE, HALF_W), F32)
    for kk in range(TOP_K):
        lo, hi = _unpack_halves(g_ref[kk])
        f_lo = f_lo + lo * tw[:, kk:kk + 1]
        f_hi = f_hi + hi * tw[:, kk:kk + 1]
    gate2 = mod_ref[0][:, 5 * D_MODEL:6 * D_MODEL]
    x2 = x1_ref[...] + gate2 * jnp.concatenate([f_lo, f_hi], axis=-1)
    out_ref[...] = _rms(x2) * fg_ref[...]


def _combine_call(x1, mod3, tw, final_g, gathered, partial, tile0, out_tile0, out_tiles, name):
    n_tiles = gathered.shape[1] // COMBINE_TILE
    in_specs = [
        pl.BlockSpec((COMBINE_TILE, D_MODEL), lambda t: (tile0 + t, 0)),
        pl.BlockSpec((1, 1, N_MOD * D_MODEL), lambda t: (_mod_row((tile0 + t) * (COMBINE_TILE // TOK_TILE)), 0, 0)),
        pl.BlockSpec((COMBINE_TILE, LANES), lambda t: (tile0 + t, 0)),
        pl.BlockSpec((1, D_MODEL), lambda t: (0, 0)),
        pl.BlockSpec((TOP_K, COMBINE_TILE, HALF_W), lambda t: (0, t, 0)),
    ]
    args = [x1, mod3, tw, final_g, gathered]
    aliases = {}
    if partial is not None:
        in_specs.append(pl.BlockSpec(memory_space=pl.ANY))
        args.append(partial)
        aliases = {len(args) - 1: 0}
    return pl.pallas_call(
        _combine_kernel,
        out_shape=jax.ShapeDtypeStruct((out_tiles * COMBINE_TILE, D_MODEL), F32),
        grid=(n_tiles,),
        in_specs=in_specs,
        out_specs=pl.BlockSpec((COMBINE_TILE, D_MODEL), lambda t: (out_tile0 + t, 0)),
        input_output_aliases=aliases,
        compiler_params=_params(("arbitrary",)),
        name=name,
    )(*args)


def kernel(x_prompt, x_sample, state_gla_fwd, state_gla_bwd, c, c_ctx, norm1_g, w_mod, b_mod, w_in,
           w_alpha, b_alpha, gla_norm_g, w_pool_grp, pool_scale, w_branch_gla, w_branch_pool, w_out,
           norm2_g, w_router, b_router, w_gate, b_gate, w_up, b_up, w_down, b_down, final_norm_g):
    l = 0
    x_ctx = x_prompt.reshape(N_CTX, D_MODEL)
    x_lat = x_sample.reshape(N_LAT, D_MODEL)

    cvec = jnp.concatenate([c_ctx[None, :], c, jnp.zeros((8 - 1 - DEC_BATCH, D_MODEL), F32)], axis=0)
    mod = _mod_call(cvec, w_mod[l], b_mod[l][None, :])
    mod3 = mod.reshape(8, 1, N_MOD * D_MODEL)

    w_in_b = w_in[l].astype(BF16)
    w_main = w_in_b[:, :MAIN_W]
    w_alr = w_in_b[:, MAIN_W:MAIN_W + ALR_W]
    w_xp = w_in_b[:, MAIN_W + ALR_W:MAIN_W + ALR_W + POOL_W]
    w_mg = w_in_b[:, MAIN_W + ALR_W + POOL_W:]
    q, k, v, og, alr, xp, mg = _inproj_call(x_ctx, x_lat, mod3, norm1_g[l][None, :],
                                            w_main, w_alr, w_xp, w_mg)

    zpad = jnp.zeros((GLA_LOWRANK, QK_W), F32)
    wa_f = jnp.concatenate([w_alpha[l, 0], zpad], axis=0)
    wa_b = jnp.concatenate([zpad, w_alpha[l, 1]], axis=0)
    o_f, o_b, s_f, s_b = _gla_call(q, k, v, alr, wa_f, b_alpha[l, 0][None, :], wa_b,
                                   b_alpha[l, 1][None, :], state_gla_fwd[:, l], state_gla_bwd[:, l])

    pooled_c = _pool_ctx_call(xp)
    pooled_l = _pool_lat_call(xp)

    w_router_pad = jnp.pad(w_router[l], ((0, 0), (0, LANES - N_EXPERTS)))
    b_router_pad = jnp.pad(b_router[l], (0, LANES - N_EXPERTS))[None, :]
    x1, h2, top_idx, top_w = _post_call(
        x_ctx, x_lat, mod3, o_f, o_b, og, pooled_c, pooled_l, mg,
        gla_norm_g[l].reshape(1, V_W), w_pool_grp[l].astype(BF16), pool_scale[l][None, :],
        w_branch_gla[l].astype(BF16), w_branch_pool[l].astype(BF16), w_out[l].astype(BF16),
        norm2_g[l][None, :], w_router_pad, b_router_pad)

    rank, cnt = _route_call(top_idx)
    counts = cnt[0, :N_EXPERTS].astype(I32)
    padded = (counts + MOE_BLOCK - 1) // MOE_BLOCK * MOE_BLOCK
    pad_end = jnp.cumsum(padded).astype(I32)
    pad_start = pad_end - padded
    block_first = jnp.arange(N_SLOT_BLOCKS, dtype=I32) * MOE_BLOCK
    block_e = jnp.minimum(jnp.sum((pad_end[None, :] <= block_first[:, None]).astype(I32), axis=1),
                          N_EXPERTS - 1).astype(I32)
    n_used = (pad_end[-1:] // MOE_BLOCK).astype(I32)
    run_start = jnp.concatenate([jnp.ones((1,), I32), (block_e[1:] != block_e[:-1]).astype(I32)])
    parity = ((jnp.cumsum(run_start) - 1) % 2).astype(I32)
    after = pad_end[block_e] // MOE_BLOCK
    next_e = jnp.where(after < n_used[0], block_e[jnp.minimum(after, N_SLOT_BLOCKS - 1)], -1).astype(I32)
    experts = jnp.arange(N_EXPERTS, dtype=I32)
    tk = top_idx[:, :TOP_K]
    pos = jnp.sum(jnp.where(tk[:, :, None] == experts, pad_start, 0), axis=-1) + rank[:, :TOP_K]
    pos = pos.astype(I32)
    pos_by_choice = pos.T
    pos_chunks = pos_by_choice.reshape(TOP_K, N_TOK // SC_ROWS, SC_ROWS).transpose(1, 0, 2)

    hs = _sc_scatter_rows(h2, pos_chunks, N_SLOTS)
    y = _moe_call(block_e, n_used, next_e, parity, hs,
                  w_gate[l], b_gate[l][:, None, :], w_up[l], b_up[l][:, None, :],
                  w_down[l], b_down[l][:, None, :])
    outs = []
    chunk_tok = N_CTX // COMBINE_CHUNKS
    for group, tok0 in (("ctx", 0), ("lat", N_CTX)):
        out = None
        for ci in range(COMBINE_CHUNKS):
            t0 = tok0 + ci * chunk_tok
            idx = pos_by_choice[:, t0:t0 + chunk_tok].reshape(TOP_K * chunk_tok)
            gathered = _sc_gather_rows(y, idx).reshape(TOP_K, chunk_tok, HALF_W)
            out = _combine_call(x1, mod3, top_w, final_norm_g[None, :], gathered, out,
                                t0 // COMBINE_TILE, ci * chunk_tok // COMBINE_TILE, N_CTX // COMBINE_TILE,
                                "combine_%s%d" % (group, ci))
        outs.append(out)
    y_prompt = outs[0].reshape(BATCH, SEQ, D_MODEL)
    y_sample = outs[1].reshape(DEC_BATCH, DEC_SEQ, D_MODEL)
    return (y_prompt, y_sample, s_f[:, None], s_b[:, None])
```

```python
import functools

import jax
import jax.numpy as jnp
from jax import lax
from jax.experimental import pallas as pl
from jax.experimental.pallas import tpu as pltpu
from jax.experimental.pallas import tpu_sc as plsc

F32 = jnp.float32
BF16 = jnp.bfloat16
I32 = jnp.int32

D_MODEL = 1024
BATCH = 32
SEQ = 256
DEC_BATCH = 4
DEC_SEQ = 2048
GRID_W = 64
GLA_HEADS = 4
GLA_DK = 128
GLA_DV = 256
GLA_LOWRANK = 16
GLA_TAU = 16.0
GLA_CHUNK = 64
POOL_GROUPS = 4
POOL_GROUP_DIM = 128
POOL_WINDOWS = (2, 4, 8, 16)
N_EXPERTS = 32
TOP_K = 4
D_FF = 1024
SWIGLU_LIMIT = 7.0
SWIGLU_ALPHA = 1.702
MOE_BLOCK = 256
NORM_EPS = 1e-6
N_MOD = 6

QK_W = GLA_HEADS * GLA_DK
V_W = GLA_HEADS * GLA_DV
POOL_W = POOL_GROUPS * POOL_GROUP_DIM
MAIN_W = 2 * QK_W + 2 * V_W
ALR_W = 2 * GLA_LOWRANK
MG_W = 2 * D_MODEL

N_CTX = BATCH * SEQ
N_LAT = DEC_BATCH * DEC_SEQ
N_TOK = N_CTX + N_LAT
N_SLOT_BLOCKS = -(-(N_TOK * TOP_K + N_EXPERTS * (MOE_BLOCK - 1)) // MOE_BLOCK)
N_SLOTS = N_SLOT_BLOCKS * MOE_BLOCK

LANES = 128
TOK_TILE = 256
N_TILES = N_TOK // TOK_TILE
CTX_TILES = N_CTX // TOK_TILE
LAT_TILES_PER_SEQ = DEC_SEQ // TOK_TILE
ROUTE_TILE = 512
VMEM_LIMIT = 56 * 1024 * 1024

GLA_SEQS = 4
CTX_CHUNKS = SEQ // GLA_CHUNK
LAT_CHUNKS = DEC_SEQ // GLA_CHUNK
CHUNKS_PER_TILE = TOK_TILE // GLA_CHUNK
GLA_CTX_STEPS = (BATCH // GLA_SEQS) * CTX_CHUNKS
TILE_GRID = 8

NT_DIMS = (((1,), (1,)), ((), ()))
TN_DIMS = (((0,), (0,)), ((), ()))

assert DEC_BATCH == GLA_SEQS and SEQ == TOK_TILE and N_TILES == TILE_GRID * TILE_GRID


def _params(semantics, vmem=VMEM_LIMIT):
    return pltpu.CompilerParams(dimension_semantics=semantics, vmem_limit_bytes=vmem)


def _split_bf16(a):
    hi = a.astype(BF16)
    lo = (a - hi.astype(F32)).astype(BF16)
    return hi, lo


def _dot(a, b):
    return jnp.dot(a, b, preferred_element_type=F32)


def _dot3(a, b):
    a_hi, a_lo = _split_bf16(a)
    b_hi, b_lo = _split_bf16(b)
    return _dot(a_hi, b_hi) + _dot(a_lo, b_hi) + _dot(a_hi, b_lo)


def _dot3_short(a, b):
    a_hi, a_lo = _split_bf16(a)
    b_hi, b_lo = _split_bf16(b)
    return _dot(jnp.concatenate([a_hi, a_lo, a_hi], axis=1), jnp.concatenate([b_hi, b_hi, b_lo], axis=0))


def _sigmoid(x):
    return 1.0 / (1.0 + jnp.exp(-x))


HALF_W = D_MODEL // 2
HIGH_HALF_MASK = -65536


def _pack_halves(lo, hi):
    lo_bits = pltpu.bitcast(lo.astype(BF16).astype(F32), I32)
    hi_bits = pltpu.bitcast(hi.astype(BF16).astype(F32), I32)
    return lax.shift_right_logical(lo_bits, 16) | (hi_bits & HIGH_HALF_MASK)


def _unpack_halves(words):
    lo = pltpu.bitcast(lax.shift_left(words, 16), F32)
    hi = pltpu.bitcast(words & HIGH_HALF_MASK, F32)
    return lo, hi


def _rms(x):
    return x * lax.rsqrt(jnp.mean(x * x, axis=-1, keepdims=True) + NORM_EPS)


def _mod_row(t):
    return jnp.where(t < CTX_TILES, 0, 1 + (t - CTX_TILES) // LAT_TILES_PER_SEQ)


def _store_tile(t):
    u = t - CTX_TILES
    return jnp.where(t < CTX_TILES, t,
                     CTX_TILES + DEC_BATCH * (u % LAT_TILES_PER_SEQ) + u // LAT_TILES_PER_SEQ)


def _ctx_tile(t):
    return jnp.minimum(t, CTX_TILES - 1)


def _lat_tile(t):
    return jnp.maximum(t - CTX_TILES, 0)


def _mod_kernel(c_ref, w_ref, b_ref, o_ref):
    c = c_ref[...]
    o_ref[...] = _dot3(c * _sigmoid(c), w_ref[...]) + b_ref[...]


def _mod_call(cvec, w_mod, b_mod):
    rows = cvec.shape[0]
    return pl.pallas_call(
        _mod_kernel,
        out_shape=jax.ShapeDtypeStruct((rows, N_MOD * D_MODEL), F32),
        grid=(N_MOD,),
        in_specs=[
            pl.BlockSpec((rows, D_MODEL), lambda j: (0, 0)),
            pl.BlockSpec((D_MODEL, D_MODEL), lambda j: (0, j)),
            pl.BlockSpec((1, D_MODEL), lambda j: (0, j)),
        ],
        out_specs=pl.BlockSpec((rows, D_MODEL), lambda j: (0, j)),
        compiler_params=_params(("arbitrary",)),
        name="mod",
    )(cvec, w_mod, b_mod)


def _inproj_kernel(xc_ref, xl_ref, mod_ref, g_ref, wmain_ref, walr_ref, wxp_ref, wmg_ref,
                   q_ref, k_ref, v_ref, og_ref, alr_ref, xp_ref, mg_ref):
    t = pl.program_id(0)
    x = jnp.where(t < CTX_TILES, xc_ref[...], xl_ref[...])
    mod = mod_ref[0]
    shift1 = mod[:, 0:D_MODEL]
    scale1 = mod[:, D_MODEL:2 * D_MODEL]
    h = _rms(x) * g_ref[...]
    h = (h * (1.0 + scale1) + shift1).astype(BF16)
    z = _dot(h, wmain_ref[...])
    q_ref[...] = (z[:, 0:QK_W] * (GLA_DK ** -0.5)).astype(BF16)
    k_ref[...] = z[:, QK_W:2 * QK_W].astype(BF16)
    v_ref[...] = z[:, 2 * QK_W:2 * QK_W + V_W].astype(BF16)
    og = z[:, 2 * QK_W + V_W:MAIN_W]
    og_ref[...] = (og * _sigmoid(og)).astype(BF16)
    alr_ref[...] = _dot(h, walr_ref[...])
    xp_ref[...] = _dot(h, wxp_ref[...])
    mg_ref[...] = _sigmoid(_dot(h, wmg_ref[...])).astype(BF16)


def _inproj_call(x_ctx, x_lat, mod3, norm1_g, w_main, w_alr, w_xp, w_mg):
    const = lambda t: (0, 0)
    stored = lambda t: (_store_tile(t), 0)
    widths = (QK_W, QK_W, V_W, V_W, ALR_W, POOL_W, MG_W)
    dtypes = (BF16, BF16, BF16, BF16, F32, F32, BF16)
    return pl.pallas_call(
        _inproj_kernel,
        out_shape=[jax.ShapeDtypeStruct((N_TOK, w), dt) for w, dt in zip(widths, dtypes)],
        grid=(N_TILES,),
        in_specs=[
            pl.BlockSpec((TOK_TILE, D_MODEL), lambda t: (_ctx_tile(t), 0)),
            pl.BlockSpec((TOK_TILE, D_MODEL), lambda t: (_lat_tile(t), 0)),
            pl.BlockSpec((1, 1, N_MOD * D_MODEL), lambda t: (_mod_row(t), 0, 0)),
            pl.BlockSpec((1, D_MODEL), const),
            pl.BlockSpec((D_MODEL, MAIN_W), const),
            pl.BlockSpec((D_MODEL, ALR_W), const),
            pl.BlockSpec((D_MODEL, POOL_W), const),
            pl.BlockSpec((D_MODEL, MG_W), const),
        ],
        out_specs=[pl.BlockSpec((TOK_TILE, w), stored) for w in widths],
        compiler_params=_params(("arbitrary",)),
        name="inproj",
    )(x_ctx, x_lat, mod3, norm1_g, w_main, w_alr, w_xp, w_mg)


def _gla_direction(q_ref, k_ref, v_ref, alr_ref, wa_ref, ba_ref, o_ref, st_ref, d, rev):
    rows = GLA_SEQS * GLA_CHUNK
    stack = lambda ref, cols: jnp.concatenate([ref[0, s, :, cols] for s in range(GLA_SEQS)], axis=0)
    alr = stack(alr_ref, slice(None))
    a = _dot3_short(alr, wa_ref[...]) + ba_ref[...]
    g = (jnp.minimum(a, 0.0) - jnp.log(1.0 + jnp.exp(-jnp.abs(a)))) * (1.0 / GLA_TAU)

    row = lax.broadcasted_iota(I32, (rows, rows), 0)
    col = lax.broadcasted_iota(I32, (rows, rows), 1)
    same = (row // GLA_CHUNK) == (col // GLA_CHUNK)
    tri = same & ((col >= row) if rev else (col <= row))
    tri_b = jnp.where(tri, 1.0, 0.0).astype(BF16)
    g_hi, g_lo = _split_bf16(g)
    bcum = _dot(tri_b, g_hi) + _dot(tri_b, g_lo)

    def per_seq_row(r):
        return jnp.concatenate(
            [jnp.broadcast_to(bcum[s * GLA_CHUNK + r:s * GLA_CHUNK + r + 1], (GLA_CHUNK, QK_W))
             for s in range(GLA_SEQS)], axis=0)

    r_last = 0 if rev else GLA_CHUNK - 1
    blast = per_seq_row(r_last)
    bmid = per_seq_row(GLA_CHUNK // 2)
    e_q = jnp.exp(bcum - bmid)
    e_k = jnp.exp(bmid - bcum)
    e_in = jnp.exp(bcum)
    e_out = jnp.exp(blast - bcum)
    q = stack(q_ref, slice(None)).astype(F32)
    k = stack(k_ref, slice(None)).astype(F32)

    wide = (rows, GLA_SEQS * GLA_DK)
    own = (lax.broadcasted_iota(I32, wide, 0) // GLA_CHUNK) == (lax.broadcasted_iota(I32, wide, 1) // GLA_DK)

    def block_diag(x):
        return jnp.where(own, jnp.concatenate([x] * GLA_SEQS, axis=1), 0.0).astype(BF16)

    for h in range(GLA_HEADS):
        ks = slice(h * GLA_DK, (h + 1) * GLA_DK)
        vs = slice(h * GLA_DV, (h + 1) * GLA_DV)
        qh = q[:, ks]
        kh = k[:, ks]
        vh = stack(v_ref, vs)
        att = lax.dot_general((qh * e_q[:, ks]).astype(BF16), (kh * e_k[:, ks]).astype(BF16),
                              NT_DIMS, preferred_element_type=F32)
        att = jnp.where(tri, att, 0.0).astype(BF16)
        st = st_ref[d, h]
        o_inter = lax.dot_general(block_diag(qh * e_in[:, ks]), st.astype(BF16), NT_DIMS,
                                  preferred_element_type=F32)
        o_h = o_inter + _dot(att, vh)
        for s in range(GLA_SEQS):
            o_ref[0, s, :, vs] = o_h[s * GLA_CHUNK:(s + 1) * GLA_CHUNK]
        upd = lax.dot_general(vh, block_diag(kh * e_out[:, ks]), TN_DIMS,
                              preferred_element_type=F32)
        e_last = jnp.concatenate(
            [jnp.exp(bcum[s * GLA_CHUNK + r_last:s * GLA_CHUNK + r_last + 1, ks])
             for s in range(GLA_SEQS)], axis=1)
        st_ref[d, h] = st * e_last + upd


def _gla_kernel(qf_ref, kf_ref, vf_ref, af_ref, qb_ref, kb_ref, vb_ref, ab_ref,
                waf_ref, baf_ref, wab_ref, bab_ref, s0f_ref, s0b_ref,
                of_ref, ob_ref, sf_ref, sb_ref, st_ref):
    i = pl.program_id(0)
    is_ctx = i < GLA_CTX_STEPS
    chunk = jnp.where(is_ctx, i % CTX_CHUNKS, i - GLA_CTX_STEPS)

    @pl.when(is_ctx & (chunk == 0))
    def _():
        st_ref[...] = jnp.zeros(st_ref.shape, F32)

    @pl.when(i == GLA_CTX_STEPS)
    def _():
        for s in range(GLA_SEQS):
            ls = slice(s * GLA_DK, (s + 1) * GLA_DK)
            for h in range(GLA_HEADS):
                st_ref[0, h, :, ls] = s0f_ref[s, h].T
                st_ref[1, h, :, ls] = s0b_ref[s, h].T

    _gla_direction(qf_ref, kf_ref, vf_ref, af_ref, waf_ref, baf_ref, of_ref, st_ref, 0, False)
    _gla_direction(qb_ref, kb_ref, vb_ref, ab_ref, wab_ref, bab_ref, ob_ref, st_ref, 1, True)

    @pl.when(is_ctx & (chunk == CTX_CHUNKS - 1))
    def _():
        for s in range(GLA_SEQS):
            ls = slice(s * GLA_DK, (s + 1) * GLA_DK)
            for h in range(GLA_HEADS):
                sf_ref[s, h] = st_ref[0, h, :, ls].T
                sb_ref[s, h] = st_ref[1, h, :, ls].T


def _gla_block(i, rev):
    is_ctx = i < GLA_CTX_STEPS
    group = i // CTX_CHUNKS
    c_ctx = i % CTX_CHUNKS
    c_lat = i - GLA_CTX_STEPS
    if rev:
        c_ctx = CTX_CHUNKS - 1 - c_ctx
        c_lat = LAT_CHUNKS - 1 - c_lat
    j = c_lat // CHUNKS_PER_TILE
    per_row = TILE_GRID // GLA_SEQS
    a = jnp.where(is_ctx, group // per_row, CTX_TILES // TILE_GRID + j // per_row)
    b = jnp.where(is_ctx, group % per_row, j % per_row)
    c = jnp.where(is_ctx, c_ctx, c_lat % CHUNKS_PER_TILE)
    return (a, b, c, 0)


def _gla_call(q, k, v, alr, wa_f, ba_f, wa_b, ba_b, s0_f, s0_b):
    def view(arr):
        return arr.reshape(TILE_GRID, TILE_GRID, TOK_TILE, arr.shape[-1])

    def spec(width, rev):
        return pl.BlockSpec((1, GLA_SEQS, GLA_CHUNK, width), lambda i: _gla_block(i, rev))

    const = lambda i: (0, 0)
    st_block = (GLA_SEQS, GLA_HEADS, GLA_DK, GLA_DV)
    whole_state = pl.BlockSpec(st_block, lambda i: (0, 0, 0, 0))
    ctx_state = pl.BlockSpec(
        st_block, lambda i: (jnp.minimum(i // CTX_CHUNKS, BATCH // GLA_SEQS - 1), 0, 0, 0))
    in_specs = []
    for rev in (False, True):
        in_specs += [spec(QK_W, rev), spec(QK_W, rev), spec(V_W, rev), spec(ALR_W, rev)]
    in_specs += [pl.BlockSpec((ALR_W, QK_W), const), pl.BlockSpec((1, QK_W), const)] * 2
    in_specs += [whole_state, whole_state]
    o_shape = jax.ShapeDtypeStruct((TILE_GRID, TILE_GRID, TOK_TILE, V_W), F32)
    s_shape = jax.ShapeDtypeStruct((BATCH, GLA_HEADS, GLA_DK, GLA_DV), F32)
    qv, kv, vv, av = view(q), view(k), view(v), view(alr)
    o_f, o_b, s_f, s_b = pl.pallas_call(
        _gla_kernel,
        out_shape=[o_shape, o_shape, s_shape, s_shape],
        grid=(GLA_CTX_STEPS + LAT_CHUNKS,),
        in_specs=in_specs,
        out_specs=[spec(V_W, False), spec(V_W, True), ctx_state, ctx_state],
        scratch_shapes=[pltpu.VMEM((2, GLA_HEADS, GLA_DV, GLA_SEQS * GLA_DK), F32)],
        compiler_params=_params(("arbitrary",)),
        name="gla",
    )(qv, kv, vv, av, qv, kv, vv, av, wa_f, ba_f, wa_b, ba_b, s0_f, s0_b)
    return o_f.reshape(N_TOK, V_W), o_b.reshape(N_TOK, V_W), s_f, s_b


def _band(n, w, block):
    row = lax.broadcasted_iota(I32, (n, n), 0)
    col = lax.broadcasted_iota(I32, (n, n), 1)
    inside = (col >= row - w // 2) & (col <= row + w // 2 - 1)
    if block < n:
        inside = inside & ((row // block) == (col // block))
    return jnp.where(inside, 1.0, 0.0).astype(BF16)


def _win_count(p, n, w):
    return jnp.minimum(p + w // 2 - 1, n - 1) - jnp.maximum(p - w // 2, 0) + 1


def _pool_ctx_kernel(x_ref, o_ref, band_ref):
    @pl.when(pl.program_id(0) == 0)
    def _():
        for gi, w in enumerate(POOL_WINDOWS):
            band_ref[gi] = _band(SEQ, w, SEQ)

    p = lax.broadcasted_iota(I32, (SEQ, POOL_GROUP_DIM), 0)
    for part in range(POOL_CTX_SEQS):
        rs = slice(part * SEQ, (part + 1) * SEQ)
        for gi, w in enumerate(POOL_WINDOWS):
            cs = slice(gi * POOL_GROUP_DIM, (gi + 1) * POOL_GROUP_DIM)
            x = x_ref[rs, cs]
            hi, lo = _split_bf16(x)
            band = band_ref[gi]
            s = _dot(band, hi) + _dot(band, lo)
            cnt = _win_count(p, SEQ, w).astype(F32)
            o_ref[rs, cs] = s / cnt - x


POOL_CTX_SEQS = 4


def _pool_ctx_call(xp):
    spec = pl.BlockSpec((POOL_CTX_SEQS * SEQ, POOL_W), lambda b: (b, 0))
    return pl.pallas_call(
        _pool_ctx_kernel,
        out_shape=jax.ShapeDtypeStruct((N_CTX, POOL_W), F32),
        grid=(BATCH // POOL_CTX_SEQS,),
        in_specs=[spec],
        out_specs=spec,
        scratch_shapes=[pltpu.VMEM((POOL_GROUPS, SEQ, SEQ), BF16)],
        compiler_params=_params(("arbitrary",)),
        name="pool_ctx",
    )(xp)


POOL_HALO = (max(POOL_WINDOWS) // 2) * GRID_W


def _pool_lat_kernel(x_ref, o_ref, pad_ref):
    rows = DEC_SEQ // GRID_W
    p = lax.broadcasted_iota(I32, (DEC_SEQ, POOL_GROUP_DIM), 0)
    r = p // GRID_W
    cidx = p % GRID_W
    zeros = jnp.zeros((POOL_HALO, POOL_GROUP_DIM), F32)
    pad_ref[0:POOL_HALO, :] = zeros
    pad_ref[POOL_HALO + DEC_SEQ:2 * POOL_HALO + DEC_SEQ, :] = zeros
    for gi, w in enumerate(POOL_WINDOWS):
        cs = slice(gi * POOL_GROUP_DIM, (gi + 1) * POOL_GROUP_DIM)
        band = _band(TOK_TILE, w, GRID_W)
        for t in range(LAT_TILES_PER_SEQ):
            hi, lo = _split_bf16(x_ref[t, 0, :, cs])
            pad_ref[POOL_HALO + t * TOK_TILE:POOL_HALO + (t + 1) * TOK_TILE, :] = (
                _dot(band, hi) + _dot(band, lo))
        acc = jnp.zeros((DEC_SEQ, POOL_GROUP_DIM), F32)
        for dr in range(-(w // 2), w // 2):
            start = POOL_HALO + dr * GRID_W
            acc = acc + pad_ref[start:start + DEC_SEQ, :]
        cnt = (_win_count(r, rows, w) * _win_count(cidx, GRID_W, w)).astype(F32)
        pooled = acc / cnt
        for t in range(LAT_TILES_PER_SEQ):
            rs = slice(t * TOK_TILE, (t + 1) * TOK_TILE)
            o_ref[t, 0, :, cs] = pooled[rs] - x_ref[t, 0, :, cs]


def _pool_lat_call(xp):
    view = xp.reshape(N_TILES // DEC_BATCH, DEC_BATCH, TOK_TILE, POOL_W)
    blk = (LAT_TILES_PER_SEQ, 1, TOK_TILE, POOL_W)
    out = pl.pallas_call(
        _pool_lat_kernel,
        out_shape=jax.ShapeDtypeStruct((LAT_TILES_PER_SEQ, DEC_BATCH, TOK_TILE, POOL_W), F32),
        grid=(DEC_BATCH,),
        in_specs=[pl.BlockSpec(blk, lambda s: (CTX_TILES // DEC_BATCH // LAT_TILES_PER_SEQ, s, 0, 0))],
        out_specs=pl.BlockSpec(blk, lambda s: (0, s, 0, 0)),
        scratch_shapes=[pltpu.VMEM((DEC_SEQ + 2 * POOL_HALO, POOL_GROUP_DIM), F32)],
        compiler_params=_params(("arbitrary",)),
        name="pool_lat",
    )(view)
    return out.reshape(N_LAT, POOL_W)


POST_PARTS = 2
POST_TILE = POST_PARTS * TOK_TILE


def _post_kernel(xc_ref, xl_ref, mod_ref, *refs):
    stored = [refs[5 * p:5 * p + 5] for p in range(POST_PARTS)]
    (pc_ref, gng_ref, wpg_ref, psc_ref, wbg_ref, wbp_ref, wout_ref, n2g_ref, wr_ref, br_ref,
     x1_ref, h2_ref, idx_ref, tw_ref) = refs[5 * POST_PARTS:]
    t = pl.program_id(0)
    is_ctx = t < CTX_TILES // POST_PARTS
    mod = mod_ref[0]
    gate1 = mod[:, 2 * D_MODEL:3 * D_MODEL]
    shift2 = mod[:, 3 * D_MODEL:4 * D_MODEL]
    scale2 = mod[:, 4 * D_MODEL:5 * D_MODEL]
    for p in range(POST_PARTS):
        of_ref, ob_ref, og_ref, pl_ref, mg_ref = stored[p]
        rs = slice(p * TOK_TILE, (p + 1) * TOK_TILE)
        x = jnp.where(is_ctx, xc_ref[rs, :], xl_ref[rs, :])
        pooled = jnp.where(is_ctx, pc_ref[rs, :], pl_ref[...])

        o = of_ref[...] + ob_ref[...]
        og = og_ref[...].astype(F32)
        gated = []
        for h in range(GLA_HEADS):
            vs = slice(h * GLA_DV, (h + 1) * GLA_DV)
            oh = _rms(o[:, vs]) * gng_ref[:, vs]
            gated.append((oh * og[:, vs]).astype(BF16))
        br_gla = _dot(jnp.concatenate(gated, axis=-1), wbg_ref[...])

        pm = []
        for gi in range(POOL_GROUPS):
            cs = slice(gi * POOL_GROUP_DIM, (gi + 1) * POOL_GROUP_DIM)
            pmg = _dot(pooled[:, cs].astype(BF16), wpg_ref[gi]) * psc_ref[:, cs]
            pm.append(pmg.astype(BF16))
        br_pool = _dot(jnp.concatenate(pm, axis=-1), wbp_ref[...])

        mg = mg_ref[...].astype(F32)
        merged = mg[:, 0:D_MODEL] * br_gla + mg[:, D_MODEL:MG_W] * br_pool
        m = _dot(merged.astype(BF16), wout_ref[...])
        x1 = x + gate1 * m
        x1_ref[rs, :] = x1
        h2 = _rms(x1) * n2g_ref[...]
        h2 = h2 * (1.0 + scale2) + shift2
        h2_ref[rs, :] = _pack_halves(h2[:, :HALF_W], h2[:, HALF_W:])

        logits = _dot3(h2, wr_ref[...]) + br_ref[...]
        lane = lax.broadcasted_iota(I32, (TOK_TILE, LANES), 1)
        lane_f = lane.astype(F32)
        neg = jnp.float32(-jnp.inf)
        cur = jnp.where(lane < N_EXPERTS, logits, neg)
        vals, idxs = [], []
        for _ in range(TOP_K):
            mx = jnp.max(cur, axis=-1, keepdims=True)
            ix = jnp.min(jnp.where(cur == mx, lane_f, float(LANES)), axis=-1, keepdims=True)
            vals.append(mx)
            idxs.append(ix)
            cur = jnp.where(lane_f == ix, neg, cur)
        ex = [jnp.exp(vv - vals[0]) for vv in vals]
        tot = ex[0] + ex[1] + ex[2] + ex[3]
        idx_out = jnp.zeros((TOK_TILE, LANES), F32)
        w_out = jnp.zeros((TOK_TILE, LANES), F32)
        for kk in range(TOP_K):
            idx_out = jnp.where(lane == kk, idxs[kk], idx_out)
            w_out = jnp.where(lane == kk, ex[kk] / tot, w_out)
        idx_ref[rs, :] = idx_out.astype(I32)
        tw_ref[rs, :] = w_out


def _post_call(x_ctx, x_lat, mod3, o_f, o_b, og, pooled_c, pooled_l, mg, gng, wpg, psc, wbg, wbp,
               wout, n2g, wr, br):
    n_steps = N_TILES // POST_PARTS
    ctx_steps = CTX_TILES // POST_PARTS
    row = lambda t: (t, 0)
    const = lambda t: (0, 0)
    ctx_blk = lambda t: (jnp.minimum(t, ctx_steps - 1), 0)
    lat_blk = lambda t: (jnp.maximum(t - ctx_steps, 0), 0)
    in_specs = [
        pl.BlockSpec((POST_TILE, D_MODEL), ctx_blk),
        pl.BlockSpec((POST_TILE, D_MODEL), lat_blk),
        pl.BlockSpec((1, 1, N_MOD * D_MODEL), lambda t: (_mod_row(POST_PARTS * t), 0, 0)),
    ]
    args = [x_ctx, x_lat, mod3]
    for p in range(POST_PARTS):
        stored = lambda t, p=p: (_store_tile(POST_PARTS * t + p), 0)
        stored_lat = lambda t, p=p: (jnp.maximum(_store_tile(POST_PARTS * t + p) - CTX_TILES, 0), 0)
        in_specs += [
            pl.BlockSpec((TOK_TILE, V_W), stored),
            pl.BlockSpec((TOK_TILE, V_W), stored),
            pl.BlockSpec((TOK_TILE, V_W), stored),
            pl.BlockSpec((TOK_TILE, POOL_W), stored_lat),
            pl.BlockSpec((TOK_TILE, MG_W), stored),
        ]
        args += [o_f, o_b, og, pooled_l, mg]
    in_specs += [
        pl.BlockSpec((POST_TILE, POOL_W), ctx_blk),
        pl.BlockSpec((1, V_W), const),
        pl.BlockSpec((POOL_GROUPS, POOL_GROUP_DIM, POOL_GROUP_DIM), lambda t: (0, 0, 0)),
        pl.BlockSpec((1, POOL_W), const),
        pl.BlockSpec((V_W, D_MODEL), const),
        pl.BlockSpec((POOL_W, D_MODEL), const),
        pl.BlockSpec((D_MODEL, D_MODEL), const),
        pl.BlockSpec((1, D_MODEL), const),
        pl.BlockSpec((D_MODEL, LANES), const),
        pl.BlockSpec((1, LANES), const),
    ]
    args += [pooled_c, gng, wpg, psc, wbg, wbp, wout, n2g, wr, br]
    return pl.pallas_call(
        _post_kernel,
        out_shape=[
            jax.ShapeDtypeStruct((N_TOK, D_MODEL), F32),
            jax.ShapeDtypeStruct((N_TOK, HALF_W), I32),
            jax.ShapeDtypeStruct((N_TOK, LANES), I32),
            jax.ShapeDtypeStruct((N_TOK, LANES), F32),
        ],
        grid=(n_steps,),
        in_specs=in_specs,
        out_specs=[
            pl.BlockSpec((POST_TILE, D_MODEL), row),
            pl.BlockSpec((POST_TILE, HALF_W), row),
            pl.BlockSpec((POST_TILE, LANES), row),
            pl.BlockSpec((POST_TILE, LANES), row),
        ],
        compiler_params=_params(("arbitrary",)),
        name="post",
    )(*args)


def _route_kernel(idx_ref, rank_ref, cnt_ref, carry_ref, strict_ref):
    t = pl.program_id(0)

    @pl.when(t == 0)
    def _():
        carry_ref[...] = jnp.zeros((1, LANES), F32)
        row = lax.broadcasted_iota(I32, (ROUTE_TILE, ROUTE_TILE), 0)
        col = lax.broadcasted_iota(I32, (ROUTE_TILE, ROUTE_TILE), 1)
        strict_ref[...] = jnp.where(col < row, 1.0, 0.0).astype(BF16)

    idx = idx_ref[...]
    lane = lax.broadcasted_iota(I32, (ROUTE_TILE, LANES), 1)
    sel = [lane == idx[:, kk:kk + 1] for kk in range(TOP_K)]
    onehot = jnp.zeros((ROUTE_TILE, LANES), F32)
    for kk in range(TOP_K):
        onehot = onehot + jnp.where(sel[kk], 1.0, 0.0)
    before = _dot(strict_ref[...], onehot.astype(BF16)) + carry_ref[...]
    rank = jnp.zeros((ROUTE_TILE, LANES), F32)
    for kk in range(TOP_K):
        rk = jnp.sum(jnp.where(sel[kk], before, 0.0), axis=-1, keepdims=True)
        rank = jnp.where(lane == kk, rk, rank)
    rank_ref[...] = rank.astype(I32)
    carry_ref[...] = carry_ref[...] + jnp.sum(onehot, axis=0, keepdims=True)
    cnt_ref[...] = jnp.broadcast_to(carry_ref[...], (8, LANES))


def _route_call(idx):
    return pl.pallas_call(
        _route_kernel,
        out_shape=[
            jax.ShapeDtypeStruct((N_TOK, LANES), I32),
            jax.ShapeDtypeStruct((8, LANES), F32),
        ],
        grid=(N_TOK // ROUTE_TILE,),
        in_specs=[pl.BlockSpec((ROUTE_TILE, LANES), lambda t: (t, 0))],
        out_specs=[
            pl.BlockSpec((ROUTE_TILE, LANES), lambda t: (t, 0)),
            pl.BlockSpec((8, LANES), lambda t: (0, 0)),
        ],
        scratch_shapes=[pltpu.VMEM((1, LANES), F32), pltpu.VMEM((ROUTE_TILE, ROUTE_TILE), BF16)],
        compiler_params=_params(("arbitrary",)),
        name="route",
    )(idx)


def _moe_kernel(be_ref, nu_ref, ne_ref, par_ref, x_ref, wg_hbm, bg_ref, wu_hbm, bu_ref, wd_hbm,
                bd_ref, y_ref, wf_ref, wgu_ref, wdb_ref, sems):
    b = pl.program_id(0)
    n_used = nu_ref[0]
    e = be_ref[b]
    prev = be_ref[jnp.maximum(b - 1, 0)]
    live = b < n_used

    def weight_copies(expert, slot):
        return [pltpu.make_async_copy(w.at[expert], wf_ref.at[slot, i], sems.at[slot])
                for i, w in enumerate((wg_hbm, wu_hbm, wd_hbm))]

    @pl.when(live & ((b == 0) | (e != prev)))
    def _():
        slot = par_ref[b]

        @pl.when(b == 0)
        def _():
            for cp in weight_copies(e, slot):
                cp.start()

        for cp in weight_copies(e, slot):
            cp.wait()
        nxt = ne_ref[b]

        @pl.when(nxt >= 0)
        def _():
            for cp in weight_copies(nxt, 1 - slot):
                cp.start()

        wgu_ref[:, :D_FF] = wf_ref[slot, 0].astype(BF16)
        wgu_ref[:, D_FF:] = wf_ref[slot, 1].astype(BF16)
        wdb_ref[...] = wf_ref[slot, 2].astype(BF16)

    @pl.when(live)
    def _():
        x_lo, x_hi = _unpack_halves(x_ref[...])
        x = jnp.concatenate([x_lo.astype(BF16), x_hi.astype(BF16)], axis=-1)
        gu = _dot(x, wgu_ref[...])
        gate = jnp.minimum(gu[:, :D_FF] + bg_ref[0], SWIGLU_LIMIT)
        up = jnp.clip(gu[:, D_FF:] + bu_ref[0], -SWIGLU_LIMIT, SWIGLU_LIMIT)
        act = (up + 1.0) * (gate * _sigmoid(SWIGLU_ALPHA * gate))
        y = _dot(act.astype(BF16), wdb_ref[...]) + bd_ref[0]
        y_ref[...] = _pack_halves(y[:, :HALF_W], y[:, HALF_W:])

    @pl.when(jnp.logical_not(live))
    def _():
        y_ref[...] = jnp.zeros((MOE_BLOCK, HALF_W), I32)


def _moe_call(block_e, n_used, next_e, parity, hs, w_gate, b_gate, w_up, b_up, w_down, b_down):
    def blk(b, be, nu, ne, par):
        return jnp.minimum(b, nu[0] - 1)

    row = lambda b, be, nu, ne, par: (blk(b, be, nu, ne, par), 0)
    bsel = lambda b, be, nu, ne, par: (be[blk(b, be, nu, ne, par)], 0, 0)
    any_spec = pl.BlockSpec(memory_space=pl.ANY)
    assert D_MODEL == D_FF
    return pl.pallas_call(
        _moe_kernel,
        out_shape=jax.ShapeDtypeStruct((N_SLOTS, HALF_W), I32),
        grid_spec=pltpu.PrefetchScalarGridSpec(
            num_scalar_prefetch=4,
            grid=(N_SLOT_BLOCKS,),
            in_specs=[
                pl.BlockSpec((MOE_BLOCK, HALF_W), row),
                any_spec,
                pl.BlockSpec((1, 1, D_FF), bsel),
                any_spec,
                pl.BlockSpec((1, 1, D_FF), bsel),
                any_spec,
                pl.BlockSpec((1, 1, D_MODEL), bsel),
            ],
            out_specs=pl.BlockSpec((MOE_BLOCK, HALF_W), lambda b, be, nu, ne, par: (b, 0)),
            scratch_shapes=[
                pltpu.VMEM((2, 3, D_MODEL, D_FF), F32),
                pltpu.VMEM((D_MODEL, 2 * D_FF), BF16),
                pltpu.VMEM((D_FF, D_MODEL), BF16),
                pltpu.SemaphoreType.DMA((2,)),
            ],
        ),
        compiler_params=_params(("arbitrary",)),
        name="moe",
    )(block_e, n_used, next_e, parity, hs, w_gate, b_gate, w_up, b_up, w_down, b_down)


SC_CORES = 2
SC_SUBCORES = 16
SC_WORKERS = SC_CORES * SC_SUBCORES
SC_ROWS = 128
COMBINE_CHUNKS = 2
COMBINE_TILE = 2 * TOK_TILE
assert N_CTX == N_LAT


def _sc_gather_rows(table, idx):
    n_idx = idx.shape[0]
    width = table.shape[1]
    rows = SC_ROWS // 2
    per_worker = n_idx // SC_WORKERS
    n_chunks = per_worker // rows
    assert n_chunks * rows * SC_WORKERS == n_idx and n_chunks >= 2
    mesh = plsc.VectorSubcoreMesh(core_axis_name="c", subcore_axis_name="s")

    @functools.partial(
        pl.kernel, mesh=mesh,
        out_type=jax.ShapeDtypeStruct((n_idx, width), table.dtype),
        scratch_types=[pltpu.VMEM((2, rows), I32), pltpu.VMEM((2, rows, width), table.dtype),
                       pltpu.SemaphoreType.DMA((2,)), pltpu.SemaphoreType.DMA((2,))],
        name="sc_gather",
    )
    def gather(table_hbm, idx_hbm, out_hbm, idx_v, rows_v, sem_g, sem_w):
        worker = lax.axis_index("s") * SC_CORES + lax.axis_index("c")
        base = worker * per_worker

        def chunk_rows(ch):
            return pl.ds(pl.multiple_of(base + ch * rows, rows), rows)

        def start_gather(ch, b):
            pltpu.sync_copy(idx_hbm.at[chunk_rows(ch)], idx_v.at[b])
            pltpu.async_copy(table_hbm.at[idx_v.at[b]], rows_v.at[b], sem_g.at[b])

        def wait_gather(b):
            pltpu.make_async_copy(table_hbm.at[pl.ds(0, rows)], rows_v.at[b], sem_g.at[b]).wait()

        def write_copy(ch, b):
            return pltpu.make_async_copy(rows_v.at[b], out_hbm.at[chunk_rows(ch)], sem_w.at[b])

        start_gather(0, 0)
        for ch in range(n_chunks):
            b = ch % 2
            if ch + 1 < n_chunks:
                if ch >= 1:
                    write_copy(ch - 1, 1 - b).wait()
                start_gather(ch + 1, 1 - b)
            wait_gather(b)
            write_copy(ch, b).start()
        write_copy(n_chunks - 2, n_chunks % 2).wait()
        write_copy(n_chunks - 1, (n_chunks - 1) % 2).wait()

    return gather(table, idx)


def _sc_scatter_rows(rows, idx3, n_out):
    n_rows, width = rows.shape
    n_chunks = n_rows // SC_ROWS // SC_WORKERS
    assert n_chunks * SC_ROWS * SC_WORKERS == n_rows and idx3.shape == (n_rows // SC_ROWS, TOP_K, SC_ROWS)
    mesh = plsc.VectorSubcoreMesh(core_axis_name="c", subcore_axis_name="s")

    @functools.partial(
        pl.kernel, mesh=mesh,
        out_type=jax.ShapeDtypeStruct((n_out, width), rows.dtype),
        scratch_types=[pltpu.VMEM((TOP_K, SC_ROWS), I32), pltpu.VMEM((SC_ROWS, width), rows.dtype),
                       pltpu.SemaphoreType.DMA],
        name="sc_scatter",
    )
    def scatter(rows_hbm, idx_hbm, out_hbm, idx_v, rows_v, sem):
        worker = lax.axis_index("s") * SC_CORES + lax.axis_index("c")

        @pl.loop(0, n_chunks)
        def _(ch):
            chunk = worker * n_chunks + ch
            pltpu.sync_copy(idx_hbm.at[chunk], idx_v)
            pltpu.sync_copy(rows_hbm.at[pl.ds(pl.multiple_of(chunk * SC_ROWS, SC_ROWS), SC_ROWS)],
                            rows_v)
            for kk in range(TOP_K):
                pltpu.async_copy(rows_v, out_hbm.at[idx_v.at[kk]], sem).wait()

    return scatter(rows, idx3)


def _combine_kernel(x1_ref, mod_ref, tw_ref, fg_ref, g_ref, *rest):
    out_ref = rest[-1]
    tw = tw_ref[...]
    f_lo = jnp.zeros((COMBINE_TILE, HALF_W), F32)
    f_hi = jnp.zeros((COMBINE_TILE, HALF_W), F32)
    for kk in range(TOP_K):
        lo, hi = _unpack_halves(g_ref[kk])
        f_lo = f_lo + lo * tw[:, kk:kk + 1]
        f_hi = f_hi + hi * tw[:, kk:kk + 1]
    gate2 = mod_ref[0][:, 5 * D_MODEL:6 * D_MODEL]
    x2 = x1_ref[...] + gate2 * jnp.concatenate([f_lo, f_hi], axis=-1)
    out_ref[...] = _rms(x2) * fg_ref[...]


def _combine_call(x1, mod3, tw, final_g, gathered, partial, tile0, out_tile0, out_tiles, name):
    n_tiles = gathered.shape[1] // COMBINE_TILE
    in_specs = [
        pl.BlockSpec((COMBINE_TILE, D_MODEL), lambda t: (tile0 + t, 0)),
        pl.BlockSpec((1, 1, N_MOD * D_MODEL), lambda t: (_mod_row((tile0 + t) * (COMBINE_TILE // TOK_TILE)), 0, 0)),
        pl.BlockSpec((COMBINE_TILE, LANES), lambda t: (tile0 + t, 0)),
        pl.BlockSpec((1, D_MODEL), lambda t: (0, 0)),
        pl.BlockSpec((TOP_K, COMBINE_TILE, HALF_W), lambda t: (0, t, 0)),
    ]
    args = [x1, mod3, tw, final_g, gathered]
    aliases = {}
    if partial is not None:
        in_specs.append(pl.BlockSpec(memory_space=pl.ANY))
        args.append(partial)
        aliases = {len(args) - 1: 0}
    return pl.pallas_call(
        _combine_kernel,
        out_shape=jax.ShapeDtypeStruct((out_tiles * COMBINE_TILE, D_MODEL), F32),
        grid=(n_tiles,),
        in_specs=in_specs,
        out_specs=pl.BlockSpec((COMBINE_TILE, D_MODEL), lambda t: (out_tile0 + t, 0)),
        input_output_aliases=aliases,
        compiler_params=_params(("arbitrary",)),
        name=name,
    )(*args)


def kernel(x_prompt, x_sample, state_gla_fwd, state_gla_bwd, c, c_ctx, norm1_g, w_mod, b_mod, w_in,
           w_alpha, b_alpha, gla_norm_g, w_pool_grp, pool_scale, w_branch_gla, w_branch_pool, w_out,
           norm2_g, w_router, b_router, w_gate, b_gate, w_up, b_up, w_down, b_down, final_norm_g):
    l = 0
    x_ctx = x_prompt.reshape(N_CTX, D_MODEL)
    x_lat = x_sample.reshape(N_LAT, D_MODEL)

    cvec = jnp.concatenate([c_ctx[None, :], c, jnp.zeros((8 - 1 - DEC_BATCH, D_MODEL), F32)], axis=0)
    mod = _mod_call(cvec, w_mod[l], b_mod[l][None, :])
    mod3 = mod.reshape(8, 1, N_MOD * D_MODEL)

    w_in_b = w_in[l].astype(BF16)
    w_main = w_in_b[:, :MAIN_W]
    w_alr = w_in_b[:, MAIN_W:MAIN_W + ALR_W]
    w_xp = w_in_b[:, MAIN_W + ALR_W:MAIN_W + ALR_W + POOL_W]
    w_mg = w_in_b[:, MAIN_W + ALR_W + POOL_W:]
    q, k, v, og, alr, xp, mg = _inproj_call(x_ctx, x_lat, mod3, norm1_g[l][None, :],
                                            w_main, w_alr, w_xp, w_mg)

    zpad = jnp.zeros((GLA_LOWRANK, QK_W), F32)
    wa_f = jnp.concatenate([w_alpha[l, 0], zpad], axis=0)
    wa_b = jnp.concatenate([zpad, w_alpha[l, 1]], axis=0)
    o_f, o_b, s_f, s_b = _gla_call(q, k, v, alr, wa_f, b_alpha[l, 0][None, :], wa_b,
                                   b_alpha[l, 1][None, :], state_gla_fwd[:, l], state_gla_bwd[:, l])

    pooled_c = _pool_ctx_call(xp)
    pooled_l = _pool_lat_call(xp)

    w_router_pad = jnp.pad(w_router[l], ((0, 0), (0, LANES - N_EXPERTS)))
    b_router_pad = jnp.pad(b_router[l], (0, LANES - N_EXPERTS))[None, :]
    x1, h2, top_idx, top_w = _post_call(
        x_ctx, x_lat, mod3, o_f, o_b, og, pooled_c, pooled_l, mg,
        gla_norm_g[l].reshape(1, V_W), w_pool_grp[l].astype(BF16), pool_scale[l][None, :],
        w_branch_gla[l].astype(BF16), w_branch_pool[l].astype(BF16), w_out[l].astype(BF16),
        norm2_g[l][None, :], w_router_pad, b_router_pad)

    rank, cnt = _route_call(top_idx)
    counts = cnt[0, :N_EXPERTS].astype(I32)
    padded = (counts + MOE_BLOCK - 1) // MOE_BLOCK * MOE_BLOCK
    pad_end = jnp.cumsum(padded).astype(I32)
    pad_start = pad_end - padded
    block_first = jnp.arange(N_SLOT_BLOCKS, dtype=I32) * MOE_BLOCK
    block_e = jnp.minimum(jnp.sum((pad_end[None, :] <= block_first[:, None]).astype(I32), axis=1),
                          N_EXPERTS - 1).astype(I32)
    n_used = (pad_end[-1:] // MOE_BLOCK).astype(I32)
    run_start = jnp.concatenate([jnp.ones((1,), I32), (block_e[1:] != block_e[:-1]).astype(I32)])
    parity = ((jnp.cumsum(run_start) - 1) % 2).astype(I32)
    after = pad_end[block_e] // MOE_BLOCK
    next_e = jnp.where(after < n_used[0], block_e[jnp.minimum(after, N_SLOT_BLOCKS - 1)], -1).astype(I32)
    experts = jnp.arange(N_EXPERTS, dtype=I32)
    tk = top_idx[:, :TOP_K]
    pos = jnp.sum(jnp.where(tk[:, :, None] == experts, pad_start, 0), axis=-1) + rank[:, :TOP_K]
    pos = pos.astype(I32)
    pos_by_choice = pos.T
    pos_chunks = pos_by_choice.reshape(TOP_K, N_TOK // SC_ROWS, SC_ROWS).transpose(1, 0, 2)

    hs = _sc_scatter_rows(h2, pos_chunks, N_SLOTS)
    y = _moe_call(block_e, n_used, next_e, parity, hs,
                  w_gate[l], b_gate[l][:, None, :], w_up[l], b_up[l][:, None, :],
                  w_down[l], b_down[l][:, None, :])
    outs = []
    chunk_tok = N_CTX // COMBINE_CHUNKS
    for group, tok0 in (("ctx", 0), ("lat", N_CTX)):
        out = None
        for ci in range(COMBINE_CHUNKS):
            t0 = tok0 + ci * chunk_tok
            idx = pos_by_choice[:, t0:t0 + chunk_tok].reshape(TOP_K * chunk_tok)
            gathered = _sc_gather_rows(y, idx).reshape(TOP_K, chunk_tok, HALF_W)
            out = _combine_call(x1, mod3, top_w, final_norm_g[None, :], gathered, out,
                                t0 // COMBINE_TILE, ci * chunk_tok // COMBINE_TILE, N_CTX // COMBINE_TILE,
                                "combine_%s%d" % (group, ci))
        outs.append(out)
    y_prompt = outs[0].reshape(BATCH, SEQ, D_MODEL)
    y_sample = outs[1].reshape(DEC_BATCH, DEC_SEQ, D_MODEL)
    return (y_prompt, y_sample, s_f[:, None], s_b[:, None])
```

```python
import functools

import jax
import jax.numpy as jnp
from jax import lax
from jax.experimental import pallas as pl
from jax.experimental.pallas import tpu as pltpu
from jax.experimental.pallas import tpu_sc as plsc

F32 = jnp.float32
BF16 = jnp.bfloat16
I32 = jnp.int32

D_MODEL = 1024
BATCH = 32
SEQ = 256
DEC_BATCH = 4
DEC_SEQ = 2048
GRID_W = 64
GLA_HEADS = 4
GLA_DK = 128
GLA_DV = 256
GLA_LOWRANK = 16
GLA_TAU = 16.0
GLA_CHUNK = 64
POOL_GROUPS = 4
POOL_GROUP_DIM = 128
POOL_WINDOWS = (2, 4, 8, 16)
N_EXPERTS = 32
TOP_K = 4
D_FF = 1024
SWIGLU_LIMIT = 7.0
SWIGLU_ALPHA = 1.702
MOE_BLOCK = 256
NORM_EPS = 1e-6
N_MOD = 6

QK_W = GLA_HEADS * GLA_DK
V_W = GLA_HEADS * GLA_DV
POOL_W = POOL_GROUPS * POOL_GROUP_DIM
MAIN_W = 2 * QK_W + 2 * V_W
ALR_W = 2 * GLA_LOWRANK
MG_W = 2 * D_MODEL

N_CTX = BATCH * SEQ
N_LAT = DEC_BATCH * DEC_SEQ
N_TOK = N_CTX + N_LAT
N_SLOT_BLOCKS = -(-(N_TOK * TOP_K + N_EXPERTS * (MOE_BLOCK - 1)) // MOE_BLOCK)
N_SLOTS = N_SLOT_BLOCKS * MOE_BLOCK

LANES = 128
TOK_TILE = 256
N_TILES = N_TOK // TOK_TILE
CTX_TILES = N_CTX // TOK_TILE
LAT_TILES_PER_SEQ = DEC_SEQ // TOK_TILE
ROUTE_TILE = 1024
VMEM_LIMIT = 56 * 1024 * 1024

GLA_SEQS = 4
CTX_CHUNKS = SEQ // GLA_CHUNK
LAT_CHUNKS = DEC_SEQ // GLA_CHUNK
CHUNKS_PER_TILE = TOK_TILE // GLA_CHUNK
GLA_CTX_STEPS = (BATCH // GLA_SEQS) * CTX_CHUNKS
TILE_GRID = 8

NT_DIMS = (((1,), (1,)), ((), ()))
TN_DIMS = (((0,), (0,)), ((), ()))

assert DEC_BATCH == GLA_SEQS and SEQ == TOK_TILE and N_TILES == TILE_GRID * TILE_GRID


def _params(semantics, vmem=VMEM_LIMIT):
    return pltpu.CompilerParams(dimension_semantics=semantics, vmem_limit_bytes=vmem)


def _split_bf16(a):
    hi = a.astype(BF16)
    lo = (a - hi.astype(F32)).astype(BF16)
    return hi, lo


def _dot(a, b):
    return jnp.dot(a, b, preferred_element_type=F32)


def _dot3(a, b):
    a_hi, a_lo = _split_bf16(a)
    b_hi, b_lo = _split_bf16(b)
    return _dot(a_hi, b_hi) + _dot(a_lo, b_hi) + _dot(a_hi, b_lo)


def _dot3_short(a, b):
    a_hi, a_lo = _split_bf16(a)
    b_hi, b_lo = _split_bf16(b)
    return _dot(jnp.concatenate([a_hi, a_lo, a_hi], axis=1), jnp.concatenate([b_hi, b_hi, b_lo], axis=0))


def _sigmoid(x):
    return 1.0 / (1.0 + jnp.exp(-x))


HALF_W = D_MODEL // 2
HIGH_HALF_MASK = -65536


def _pack_halves(lo, hi):
    lo_bits = pltpu.bitcast(lo.astype(BF16).astype(F32), I32)
    hi_bits = pltpu.bitcast(hi.astype(BF16).astype(F32), I32)
    return lax.shift_right_logical(lo_bits, 16) | (hi_bits & HIGH_HALF_MASK)


def _unpack_halves(words):
    lo = pltpu.bitcast(lax.shift_left(words, 16), F32)
    hi = pltpu.bitcast(words & HIGH_HALF_MASK, F32)
    return lo, hi


def _rms(x):
    return x * lax.rsqrt(jnp.mean(x * x, axis=-1, keepdims=True) + NORM_EPS)


def _mod_row(t):
    return jnp.where(t < CTX_TILES, 0, 1 + (t - CTX_TILES) // LAT_TILES_PER_SEQ)


def _store_tile(t):
    u = t - CTX_TILES
    return jnp.where(t < CTX_TILES, t,
                     CTX_TILES + DEC_BATCH * (u % LAT_TILES_PER_SEQ) + u // LAT_TILES_PER_SEQ)


def _ctx_tile(t):
    return jnp.minimum(t, CTX_TILES - 1)


def _lat_tile(t):
    return jnp.maximum(t - CTX_TILES, 0)


def _mod_kernel(c_ref, w_ref, b_ref, o_ref):
    c = c_ref[...]
    o_ref[...] = _dot3(c * _sigmoid(c), w_ref[...]) + b_ref[...]


def _mod_call(cvec, w_mod, b_mod):
    rows = cvec.shape[0]
    return pl.pallas_call(
        _mod_kernel,
        out_shape=jax.ShapeDtypeStruct((rows, N_MOD * D_MODEL), F32),
        grid=(N_MOD,),
        in_specs=[
            pl.BlockSpec((rows, D_MODEL), lambda j: (0, 0)),
            pl.BlockSpec((D_MODEL, D_MODEL), lambda j: (0, j)),
            pl.BlockSpec((1, D_MODEL), lambda j: (0, j)),
        ],
        out_specs=pl.BlockSpec((rows, D_MODEL), lambda j: (0, j)),
        compiler_params=_params(("arbitrary",)),
        name="mod",
    )(cvec, w_mod, b_mod)


def _inproj_kernel(xc_ref, xl_ref, mod_ref, g_ref, wmain_ref, walr_ref, wxp_ref, wmg_ref,
                   q_ref, k_ref, v_ref, og_ref, alr_ref, xp_ref, mg_ref):
    t = pl.program_id(0)
    x = jnp.where(t < CTX_TILES, xc_ref[...], xl_ref[...])
    mod = mod_ref[0]
    shift1 = mod[:, 0:D_MODEL]
    scale1 = mod[:, D_MODEL:2 * D_MODEL]
    h = _rms(x) * g_ref[...]
    h = (h * (1.0 + scale1) + shift1).astype(BF16)
    z = _dot(h, wmain_ref[...])
    q_ref[...] = (z[:, 0:QK_W] * (GLA_DK ** -0.5)).astype(BF16)
    k_ref[...] = z[:, QK_W:2 * QK_W].astype(BF16)
    v_ref[...] = z[:, 2 * QK_W:2 * QK_W + V_W].astype(BF16)
    og = z[:, 2 * QK_W + V_W:MAIN_W]
    og_ref[...] = (og * _sigmoid(og)).astype(BF16)
    alr_ref[...] = _dot(h, walr_ref[...])
    xp_ref[...] = _dot(h, wxp_ref[...])
    mg_ref[...] = _sigmoid(_dot(h, wmg_ref[...])).astype(BF16)


def _inproj_call(x_ctx, x_lat, mod3, norm1_g, w_main, w_alr, w_xp, w_mg):
    const = lambda t: (0, 0)
    stored = lambda t: (_store_tile(t), 0)
    widths = (QK_W, QK_W, V_W, V_W, ALR_W, POOL_W, MG_W)
    dtypes = (BF16, BF16, BF16, BF16, F32, F32, BF16)
    return pl.pallas_call(
        _inproj_kernel,
        out_shape=[jax.ShapeDtypeStruct((N_TOK, w), dt) for w, dt in zip(widths, dtypes)],
        grid=(N_TILES,),
        in_specs=[
            pl.BlockSpec((TOK_TILE, D_MODEL), lambda t: (_ctx_tile(t), 0)),
            pl.BlockSpec((TOK_TILE, D_MODEL), lambda t: (_lat_tile(t), 0)),
            pl.BlockSpec((1, 1, N_MOD * D_MODEL), lambda t: (_mod_row(t), 0, 0)),
            pl.BlockSpec((1, D_MODEL), const),
            pl.BlockSpec((D_MODEL, MAIN_W), const),
            pl.BlockSpec((D_MODEL, ALR_W), const),
            pl.BlockSpec((D_MODEL, POOL_W), const),
            pl.BlockSpec((D_MODEL, MG_W), const),
        ],
        out_specs=[pl.BlockSpec((TOK_TILE, w), stored) for w in widths],
        compiler_params=_params(("arbitrary",)),
        name="inproj",
    )(x_ctx, x_lat, mod3, norm1_g, w_main, w_alr, w_xp, w_mg)


def _gla_direction(q_ref, k_ref, v_ref, alr_ref, wa_ref, ba_ref, o_ref, st_ref, d, rev):
    rows = GLA_SEQS * GLA_CHUNK
    stack = lambda ref, cols: jnp.concatenate([ref[0, s, :, cols] for s in range(GLA_SEQS)], axis=0)
    alr = stack(alr_ref, slice(None))
    a = _dot3_short(alr, wa_ref[...]) + ba_ref[...]
    g = (jnp.minimum(a, 0.0) - jnp.log(1.0 + jnp.exp(-jnp.abs(a)))) * (1.0 / GLA_TAU)

    row = lax.broadcasted_iota(I32, (rows, rows), 0)
    col = lax.broadcasted_iota(I32, (rows, rows), 1)
    same = (row // GLA_CHUNK) == (col // GLA_CHUNK)
    tri = same & ((col >= row) if rev else (col <= row))
    tri_b = jnp.where(tri, 1.0, 0.0).astype(BF16)
    g_hi, g_lo = _split_bf16(g)
    bcum = _dot(tri_b, g_hi) + _dot(tri_b, g_lo)

    def per_seq_row(r):
        return jnp.concatenate(
            [jnp.broadcast_to(bcum[s * GLA_CHUNK + r:s * GLA_CHUNK + r + 1], (GLA_CHUNK, QK_W))
             for s in range(GLA_SEQS)], axis=0)

    r_last = 0 if rev else GLA_CHUNK - 1
    blast = per_seq_row(r_last)
    bmid = per_seq_row(GLA_CHUNK // 2)
    e_q = jnp.exp(bcum - bmid)
    e_k = jnp.exp(bmid - bcum)
    e_in = jnp.exp(bcum)
    e_out = jnp.exp(blast - bcum)
    q = stack(q_ref, slice(None)).astype(F32)
    k = stack(k_ref, slice(None)).astype(F32)

    wide = (rows, GLA_SEQS * GLA_DK)
    own = (lax.broadcasted_iota(I32, wide, 0) // GLA_CHUNK) == (lax.broadcasted_iota(I32, wide, 1) // GLA_DK)

    def block_diag(x):
        return jnp.where(own, jnp.concatenate([x] * GLA_SEQS, axis=1), 0.0).astype(BF16)

    for h in range(GLA_HEADS):
        ks = slice(h * GLA_DK, (h + 1) * GLA_DK)
        vs = slice(h * GLA_DV, (h + 1) * GLA_DV)
        qh = q[:, ks]
        kh = k[:, ks]
        vh = stack(v_ref, vs)
        att = lax.dot_general((qh * e_q[:, ks]).astype(BF16), (kh * e_k[:, ks]).astype(BF16),
                              NT_DIMS, preferred_element_type=F32)
        att = jnp.where(tri, att, 0.0).astype(BF16)
        st = st_ref[d, h]
        o_inter = lax.dot_general(block_diag(qh * e_in[:, ks]), st.astype(BF16), NT_DIMS,
                                  preferred_element_type=F32)
        o_h = o_inter + _dot(att, vh)
        for s in range(GLA_SEQS):
            o_ref[0, s, :, vs] = o_h[s * GLA_CHUNK:(s + 1) * GLA_CHUNK]
        upd = lax.dot_general(vh, block_diag(kh * e_out[:, ks]), TN_DIMS,
                              preferred_element_type=F32)
        e_last = jnp.concatenate(
            [jnp.exp(bcum[s * GLA_CHUNK + r_last:s * GLA_CHUNK + r_last + 1, ks])
             for s in range(GLA_SEQS)], axis=1)
        st_ref[d, h] = st * e_last + upd


def _gla_kernel(qf_ref, kf_ref, vf_ref, af_ref, qb_ref, kb_ref, vb_ref, ab_ref,
                waf_ref, baf_ref, wab_ref, bab_ref, s0f_ref, s0b_ref,
                of_ref, ob_ref, sf_ref, sb_ref, st_ref):
    i = pl.program_id(0)
    is_ctx = i < GLA_CTX_STEPS
    chunk = jnp.where(is_ctx, i % CTX_CHUNKS, i - GLA_CTX_STEPS)

    @pl.when(is_ctx & (chunk == 0))
    def _():
        st_ref[...] = jnp.zeros(st_ref.shape, F32)

    @pl.when(i == GLA_CTX_STEPS)
    def _():
        for s in range(GLA_SEQS):
            ls = slice(s * GLA_DK, (s + 1) * GLA_DK)
            for h in range(GLA_HEADS):
                st_ref[0, h, :, ls] = s0f_ref[s, h].T
                st_ref[1, h, :, ls] = s0b_ref[s, h].T

    _gla_direction(qf_ref, kf_ref, vf_ref, af_ref, waf_ref, baf_ref, of_ref, st_ref, 0, False)
    _gla_direction(qb_ref, kb_ref, vb_ref, ab_ref, wab_ref, bab_ref, ob_ref, st_ref, 1, True)

    @pl.when(is_ctx & (chunk == CTX_CHUNKS - 1))
    def _():
        for s in range(GLA_SEQS):
            ls = slice(s * GLA_DK, (s + 1) * GLA_DK)
            for h in range(GLA_HEADS):
                sf_ref[s, h] = st_ref[0, h, :, ls].T
                sb_ref[s, h] = st_ref[1, h, :, ls].T


def _gla_block(i, rev):
    is_ctx = i < GLA_CTX_STEPS
    group = i // CTX_CHUNKS
    c_ctx = i % CTX_CHUNKS
    c_lat = i - GLA_CTX_STEPS
    if rev:
        c_ctx = CTX_CHUNKS - 1 - c_ctx
        c_lat = LAT_CHUNKS - 1 - c_lat
    j = c_lat // CHUNKS_PER_TILE
    per_row = TILE_GRID // GLA_SEQS
    a = jnp.where(is_ctx, group // per_row, CTX_TILES // TILE_GRID + j // per_row)
    b = jnp.where(is_ctx, group % per_row, j % per_row)
    c = jnp.where(is_ctx, c_ctx, c_lat % CHUNKS_PER_TILE)
    return (a, b, c, 0)


def _gla_call(q, k, v, alr, wa_f, ba_f, wa_b, ba_b, s0_f, s0_b):
    def view(arr):
        return arr.reshape(TILE_GRID, TILE_GRID, TOK_TILE, arr.shape[-1])

    def spec(width, rev):
        return pl.BlockSpec((1, GLA_SEQS, GLA_CHUNK, width), lambda i: _gla_block(i, rev))

    const = lambda i: (0, 0)
    st_block = (GLA_SEQS, GLA_HEADS, GLA_DK, GLA_DV)
    whole_state = pl.BlockSpec(st_block, lambda i: (0, 0, 0, 0))
    ctx_state = pl.BlockSpec(
        st_block, lambda i: (jnp.minimum(i // CTX_CHUNKS, BATCH // GLA_SEQS - 1), 0, 0, 0))
    in_specs = []
    for rev in (False, True):
        in_specs += [spec(QK_W, rev), spec(QK_W, rev), spec(V_W, rev), spec(ALR_W, rev)]
    in_specs += [pl.BlockSpec((ALR_W, QK_W), const), pl.BlockSpec((1, QK_W), const)] * 2
    in_specs += [whole_state, whole_state]
    o_shape = jax.ShapeDtypeStruct((TILE_GRID, TILE_GRID, TOK_TILE, V_W), F32)
    s_shape = jax.ShapeDtypeStruct((BATCH, GLA_HEADS, GLA_DK, GLA_DV), F32)
    qv, kv, vv, av = view(q), view(k), view(v), view(alr)
    o_f, o_b, s_f, s_b = pl.pallas_call(
        _gla_kernel,
        out_shape=[o_shape, o_shape, s_shape, s_shape],
        grid=(GLA_CTX_STEPS + LAT_CHUNKS,),
        in_specs=in_specs,
        out_specs=[spec(V_W, False), spec(V_W, True), ctx_state, ctx_state],
        scratch_shapes=[pltpu.VMEM((2, GLA_HEADS, GLA_DV, GLA_SEQS * GLA_DK), F32)],
        compiler_params=_params(("arbitrary",)),
        name="gla",
    )(qv, kv, vv, av, qv, kv, vv, av, wa_f, ba_f, wa_b, ba_b, s0_f, s0_b)
    return o_f.reshape(N_TOK, V_W), o_b.reshape(N_TOK, V_W), s_f, s_b


def _band(n, w, block):
    row = lax.broadcasted_iota(I32, (n, n), 0)
    col = lax.broadcasted_iota(I32, (n, n), 1)
    inside = (col >= row - w // 2) & (col <= row + w // 2 - 1)
    if block < n:
        inside = inside & ((row // block) == (col // block))
    return jnp.where(inside, 1.0, 0.0).astype(BF16)


def _win_count(p, n, w):
    return jnp.minimum(p + w // 2 - 1, n - 1) - jnp.maximum(p - w // 2, 0) + 1


def _pool_ctx_kernel(x_ref, o_ref, band_ref):
    @pl.when(pl.program_id(0) == 0)
    def _():
        for gi, w in enumerate(POOL_WINDOWS):
            band_ref[gi] = _band(SEQ, w, SEQ)

    p = lax.broadcasted_iota(I32, (SEQ, POOL_GROUP_DIM), 0)
    for part in range(POOL_CTX_SEQS):
        rs = slice(part * SEQ, (part + 1) * SEQ)
        for gi, w in enumerate(POOL_WINDOWS):
            cs = slice(gi * POOL_GROUP_DIM, (gi + 1) * POOL_GROUP_DIM)
            x = x_ref[rs, cs]
            hi, lo = _split_bf16(x)
            band = band_ref[gi]
            s = _dot(band, hi) + _dot(band, lo)
            cnt = _win_count(p, SEQ, w).astype(F32)
            o_ref[rs, cs] = s / cnt - x


POOL_CTX_SEQS = 4


def _pool_ctx_call(xp):
    spec = pl.BlockSpec((POOL_CTX_SEQS * SEQ, POOL_W), lambda b: (b, 0))
    return pl.pallas_call(
        _pool_ctx_kernel,
        out_shape=jax.ShapeDtypeStruct((N_CTX, POOL_W), F32),
        grid=(BATCH // POOL_CTX_SEQS,),
        in_specs=[spec],
        out_specs=spec,
        scratch_shapes=[pltpu.VMEM((POOL_GROUPS, SEQ, SEQ), BF16)],
        compiler_params=_params(("arbitrary",)),
        name="pool_ctx",
    )(xp)


POOL_HALO = (max(POOL_WINDOWS) // 2) * GRID_W


def _pool_lat_kernel(x_ref, o_ref, pad_ref):
    rows = DEC_SEQ // GRID_W
    p = lax.broadcasted_iota(I32, (DEC_SEQ, POOL_GROUP_DIM), 0)
    r = p // GRID_W
    cidx = p % GRID_W
    zeros = jnp.zeros((POOL_HALO, POOL_GROUP_DIM), F32)
    pad_ref[0:POOL_HALO, :] = zeros
    pad_ref[POOL_HALO + DEC_SEQ:2 * POOL_HALO + DEC_SEQ, :] = zeros
    for gi, w in enumerate(POOL_WINDOWS):
        cs = slice(gi * POOL_GROUP_DIM, (gi + 1) * POOL_GROUP_DIM)
        band = _band(TOK_TILE, w, GRID_W)
        for t in range(LAT_TILES_PER_SEQ):
            hi, lo = _split_bf16(x_ref[t, 0, :, cs])
            pad_ref[POOL_HALO + t * TOK_TILE:POOL_HALO + (t + 1) * TOK_TILE, :] = (
                _dot(band, hi) + _dot(band, lo))
        acc = jnp.zeros((DEC_SEQ, POOL_GROUP_DIM), F32)
        for dr in range(-(w // 2), w // 2):
            start = POOL_HALO + dr * GRID_W
            acc = acc + pad_ref[start:start + DEC_SEQ, :]
        cnt = (_win_count(r, rows, w) * _win_count(cidx, GRID_W, w)).astype(F32)
        pooled = acc / cnt
        for t in range(LAT_TILES_PER_SEQ):
            rs = slice(t * TOK_TILE, (t + 1) * TOK_TILE)
            o_ref[t, 0, :, cs] = pooled[rs] - x_ref[t, 0, :, cs]


def _pool_lat_call(xp):
    view = xp.reshape(N_TILES // DEC_BATCH, DEC_BATCH, TOK_TILE, POOL_W)
    blk = (LAT_TILES_PER_SEQ, 1, TOK_TILE, POOL_W)
    out = pl.pallas_call(
        _pool_lat_kernel,
        out_shape=jax.ShapeDtypeStruct((LAT_TILES_PER_SEQ, DEC_BATCH, TOK_TILE, POOL_W), F32),
        grid=(DEC_BATCH,),
        in_specs=[pl.BlockSpec(blk, lambda s: (CTX_TILES // DEC_BATCH // LAT_TILES_PER_SEQ, s, 0, 0))],
        out_specs=pl.BlockSpec(blk, lambda s: (0, s, 0, 0)),
        scratch_shapes=[pltpu.VMEM((DEC_SEQ + 2 * POOL_HALO, POOL_GROUP_DIM), F32)],
        compiler_params=_params(("arbitrary",)),
        name="pool_lat",
    )(view)
    return out.reshape(N_LAT, POOL_W)


POST_PARTS = 2
POST_TILE = POST_PARTS * TOK_TILE


def _post_kernel(xc_ref, xl_ref, mod_ref, *refs):
    stored = [refs[5 * p:5 * p + 5] for p in range(POST_PARTS)]
    (pc_ref, gng_ref, wpg_ref, psc_ref, wbg_ref, wbp_ref, wout_ref, n2g_ref, wr_ref, br_ref,
     x1_ref, h2_ref, idx_ref, tw_ref) = refs[5 * POST_PARTS:]
    t = pl.program_id(0)
    is_ctx = t < CTX_TILES // POST_PARTS
    mod = mod_ref[0]
    gate1 = mod[:, 2 * D_MODEL:3 * D_MODEL]
    shift2 = mod[:, 3 * D_MODEL:4 * D_MODEL]
    scale2 = mod[:, 4 * D_MODEL:5 * D_MODEL]
    for p in range(POST_PARTS):
        of_ref, ob_ref, og_ref, pl_ref, mg_ref = stored[p]
        rs = slice(p * TOK_TILE, (p + 1) * TOK_TILE)
        x = jnp.where(is_ctx, xc_ref[rs, :], xl_ref[rs, :])
        pooled = jnp.where(is_ctx, pc_ref[rs, :], pl_ref[...])

        o = of_ref[...] + ob_ref[...]
        og = og_ref[...].astype(F32)
        gated = []
        for h in range(GLA_HEADS):
            vs = slice(h * GLA_DV, (h + 1) * GLA_DV)
            oh = _rms(o[:, vs]) * gng_ref[:, vs]
            gated.append((oh * og[:, vs]).astype(BF16))
        br_gla = _dot(jnp.concatenate(gated, axis=-1), wbg_ref[...])

        pm = []
        for gi in range(POOL_GROUPS):
            cs = slice(gi * POOL_GROUP_DIM, (gi + 1) * POOL_GROUP_DIM)
            pmg = _dot(pooled[:, cs].astype(BF16), wpg_ref[gi]) * psc_ref[:, cs]
            pm.append(pmg.astype(BF16))
        br_pool = _dot(jnp.concatenate(pm, axis=-1), wbp_ref[...])

        mg = mg_ref[...].astype(F32)
        merged = mg[:, 0:D_MODEL] * br_gla + mg[:, D_MODEL:MG_W] * br_pool
        m = _dot(merged.astype(BF16), wout_ref[...])
        x1 = x + gate1 * m
        x1_ref[rs, :] = x1
        h2 = _rms(x1) * n2g_ref[...]
        h2 = h2 * (1.0 + scale2) + shift2
        h2_ref[rs, :] = _pack_halves(h2[:, :HALF_W], h2[:, HALF_W:])

        logits = _dot3(h2, wr_ref[...]) + br_ref[...]
        lane = lax.broadcasted_iota(I32, (TOK_TILE, LANES), 1)
        lane_f = lane.astype(F32)
        neg = jnp.float32(-jnp.inf)
        cur = jnp.where(lane < N_EXPERTS, logits, neg)
        vals, idxs = [], []
        for _ in range(TOP_K):
            mx = jnp.max(cur, axis=-1, keepdims=True)
            ix = jnp.min(jnp.where(cur == mx, lane_f, float(LANES)), axis=-1, keepdims=True)
            vals.append(mx)
            idxs.append(ix)
            cur = jnp.where(lane_f == ix, neg, cur)
        ex = [jnp.exp(vv - vals[0]) for vv in vals]
        tot = ex[0] + ex[1] + ex[2] + ex[3]
        idx_out = jnp.zeros((TOK_TILE, LANES), F32)
        w_out = jnp.zeros((TOK_TILE, LANES), F32)
        for kk in range(TOP_K):
            idx_out = jnp.where(lane == kk, idxs[kk], idx_out)
            w_out = jnp.where(lane == kk, ex[kk] / tot, w_out)
        idx_ref[rs, :] = idx_out.astype(I32)
        tw_ref[rs, :] = w_out


def _post_call(x_ctx, x_lat, mod3, o_f, o_b, og, pooled_c, pooled_l, mg, gng, wpg, psc, wbg, wbp,
               wout, n2g, wr, br):
    n_steps = N_TILES // POST_PARTS
    ctx_steps = CTX_TILES // POST_PARTS
    row = lambda t: (t, 0)
    const = lambda t: (0, 0)
    ctx_blk = lambda t: (jnp.minimum(t, ctx_steps - 1), 0)
    lat_blk = lambda t: (jnp.maximum(t - ctx_steps, 0), 0)
    in_specs = [
        pl.BlockSpec((POST_TILE, D_MODEL), ctx_blk),
        pl.BlockSpec((POST_TILE, D_MODEL), lat_blk),
        pl.BlockSpec((1, 1, N_MOD * D_MODEL), lambda t: (_mod_row(POST_PARTS * t), 0, 0)),
    ]
    args = [x_ctx, x_lat, mod3]
    for p in range(POST_PARTS):
        stored = lambda t, p=p: (_store_tile(POST_PARTS * t + p), 0)
        stored_lat = lambda t, p=p: (jnp.maximum(_store_tile(POST_PARTS * t + p) - CTX_TILES, 0), 0)
        in_specs += [
            pl.BlockSpec((TOK_TILE, V_W), stored),
            pl.BlockSpec((TOK_TILE, V_W), stored),
            pl.BlockSpec((TOK_TILE, V_W), stored),
            pl.BlockSpec((TOK_TILE, POOL_W), stored_lat),
            pl.BlockSpec((TOK_TILE, MG_W), stored),
        ]
        args += [o_f, o_b, og, pooled_l, mg]
    in_specs += [
        pl.BlockSpec((POST_TILE, POOL_W), ctx_blk),
        pl.BlockSpec((1, V_W), const),
        pl.BlockSpec((POOL_GROUPS, POOL_GROUP_DIM, POOL_GROUP_DIM), lambda t: (0, 0, 0)),
        pl.BlockSpec((1, POOL_W), const),
        pl.BlockSpec((V_W, D_MODEL), const),
        pl.BlockSpec((POOL_W, D_MODEL), const),
        pl.BlockSpec((D_MODEL, D_MODEL), const),
        pl.BlockSpec((1, D_MODEL), const),
        pl.BlockSpec((D_MODEL, LANES), const),
        pl.BlockSpec((1, LANES), const),
    ]
    args += [pooled_c, gng, wpg, psc, wbg, wbp, wout, n2g, wr, br]
    return pl.pallas_call(
        _post_kernel,
        out_shape=[
            jax.ShapeDtypeStruct((N_TOK, D_MODEL), F32),
            jax.ShapeDtypeStruct((N_TOK, HALF_W), I32),
            jax.ShapeDtypeStruct((N_TOK, LANES), I32),
            jax.ShapeDtypeStruct((N_TOK, LANES), F32),
        ],
        grid=(n_steps,),
        in_specs=in_specs,
        out_specs=[
            pl.BlockSpec((POST_TILE, D_MODEL), row),
            pl.BlockSpec((POST_TILE, HALF_W), row),
            pl.BlockSpec((POST_TILE, LANES), row),
            pl.BlockSpec((POST_TILE, LANES), row),
        ],
        compiler_params=_params(("arbitrary",)),
        name="post",
    )(*args)


def _route_kernel(idx_ref, rank_ref, cnt_ref, carry_ref, strict_ref):
    t = pl.program_id(0)

    @pl.when(t == 0)
    def _():
        carry_ref[...] = jnp.zeros((1, LANES), F32)
        row = lax.broadcasted_iota(I32, (ROUTE_TILE, ROUTE_TILE), 0)
        col = lax.broadcasted_iota(I32, (ROUTE_TILE, ROUTE_TILE), 1)
        strict_ref[...] = jnp.where(col < row, 1.0, 0.0).astype(BF16)

    idx = idx_ref[...]
    lane = lax.broadcasted_iota(I32, (ROUTE_TILE, LANES), 1)
    sel = [lane == idx[:, kk:kk + 1] for kk in range(TOP_K)]
    onehot = jnp.zeros((ROUTE_TILE, LANES), F32)
    for kk in range(TOP_K):
        onehot = onehot + jnp.where(sel[kk], 1.0, 0.0)
    before = _dot(strict_ref[...], onehot.astype(BF16)) + carry_ref[...]
    rank = jnp.zeros((ROUTE_TILE, LANES), F32)
    for kk in range(TOP_K):
        rk = jnp.sum(jnp.where(sel[kk], before, 0.0), axis=-1, keepdims=True)
        rank = jnp.where(lane == kk, rk, rank)
    rank_ref[...] = rank.astype(I32)
    carry_ref[...] = carry_ref[...] + jnp.sum(onehot, axis=0, keepdims=True)
    cnt_ref[...] = jnp.broadcast_to(carry_ref[...], (8, LANES))


def _route_call(idx):
    return pl.pallas_call(
        _route_kernel,
        out_shape=[
            jax.ShapeDtypeStruct((N_TOK, LANES), I32),
            jax.ShapeDtypeStruct((8, LANES), F32),
        ],
        grid=(N_TOK // ROUTE_TILE,),
        in_specs=[pl.BlockSpec((ROUTE_TILE, LANES), lambda t: (t, 0))],
        out_specs=[
            pl.BlockSpec((ROUTE_TILE, LANES), lambda t: (t, 0)),
            pl.BlockSpec((8, LANES), lambda t: (0, 0)),
        ],
        scratch_shapes=[pltpu.VMEM((1, LANES), F32), pltpu.VMEM((ROUTE_TILE, ROUTE_TILE), BF16)],
        compiler_params=_params(("arbitrary",)),
        name="route",
    )(idx)


def _moe_kernel(be_ref, nu_ref, ne_ref, par_ref, x_ref, wg_hbm, bg_ref, wu_hbm, bu_ref, wd_hbm,
                bd_ref, y_ref, wf_ref, wgu_ref, wdb_ref, sems):
    b = pl.program_id(0)
    n_used = nu_ref[0]
    e = be_ref[b]
    prev = be_ref[jnp.maximum(b - 1, 0)]
    live = b < n_used

    def weight_copies(expert, slot):
        return [pltpu.make_async_copy(w.at[expert], wf_ref.at[slot, i], sems.at[slot])
                for i, w in enumerate((wg_hbm, wu_hbm, wd_hbm))]

    @pl.when(live & ((b == 0) | (e != prev)))
    def _():
        slot = par_ref[b]

        @pl.when(b == 0)
        def _():
            for cp in weight_copies(e, slot):
                cp.start()

        for cp in weight_copies(e, slot):
            cp.wait()
        nxt = ne_ref[b]

        @pl.when(nxt >= 0)
        def _():
            for cp in weight_copies(nxt, 1 - slot):
                cp.start()

        wgu_ref[:, :D_FF] = wf_ref[slot, 0].astype(BF16)
        wgu_ref[:, D_FF:] = wf_ref[slot, 1].astype(BF16)
        wdb_ref[...] = wf_ref[slot, 2].astype(BF16)

    @pl.when(live)
    def _():
        x_lo, x_hi = _unpack_halves(x_ref[...])
        x = jnp.concatenate([x_lo.astype(BF16), x_hi.astype(BF16)], axis=-1)
        gu = _dot(x, wgu_ref[...])
        gate = jnp.minimum(gu[:, :D_FF] + bg_ref[0], SWIGLU_LIMIT)
        up = jnp.clip(gu[:, D_FF:] + bu_ref[0], -SWIGLU_LIMIT, SWIGLU_LIMIT)
        act = (up + 1.0) * (gate * _sigmoid(SWIGLU_ALPHA * gate))
        y = _dot(act.astype(BF16), wdb_ref[...]) + bd_ref[0]
        y_ref[...] = _pack_halves(y[:, :HALF_W], y[:, HALF_W:])

    @pl.when(jnp.logical_not(live))
    def _():
        y_ref[...] = jnp.zeros((MOE_BLOCK, HALF_W), I32)


def _moe_call(block_e, n_used, next_e, parity, hs, w_gate, b_gate, w_up, b_up, w_down, b_down):
    def blk(b, be, nu, ne, par):
        return jnp.minimum(b, nu[0] - 1)

    row = lambda b, be, nu, ne, par: (blk(b, be, nu, ne, par), 0)
    bsel = lambda b, be, nu, ne, par: (be[blk(b, be, nu, ne, par)], 0, 0)
    any_spec = pl.BlockSpec(memory_space=pl.ANY)
    assert D_MODEL == D_FF
    return pl.pallas_call(
        _moe_kernel,
        out_shape=jax.ShapeDtypeStruct((N_SLOTS, HALF_W), I32),
        grid_spec=pltpu.PrefetchScalarGridSpec(
            num_scalar_prefetch=4,
            grid=(N_SLOT_BLOCKS,),
            in_specs=[
                pl.BlockSpec((MOE_BLOCK, HALF_W), row),
                any_spec,
                pl.BlockSpec((1, 1, D_FF), bsel),
                any_spec,
                pl.BlockSpec((1, 1, D_FF), bsel),
                any_spec,
                pl.BlockSpec((1, 1, D_MODEL), bsel),
            ],
            out_specs=pl.BlockSpec((MOE_BLOCK, HALF_W), lambda b, be, nu, ne, par: (b, 0)),
            scratch_shapes=[
                pltpu.VMEM((2, 3, D_MODEL, D_FF), F32),
                pltpu.VMEM((D_MODEL, 2 * D_FF), BF16),
                pltpu.VMEM((D_FF, D_MODEL), BF16),
                pltpu.SemaphoreType.DMA((2,)),
            ],
        ),
        compiler_params=_params(("arbitrary",)),
        name="moe",
    )(block_e, n_used, next_e, parity, hs, w_gate, b_gate, w_up, b_up, w_down, b_down)


SC_CORES = 2
SC_SUBCORES = 16
SC_WORKERS = SC_CORES * SC_SUBCORES
SC_ROWS = 128
COMBINE_CHUNKS = 2
COMBINE_TILE = 2 * TOK_TILE
assert N_CTX == N_LAT


def _sc_gather_rows(table, idx):
    n_idx = idx.shape[0]
    width = table.shape[1]
    rows = SC_ROWS // 2
    per_worker = n_idx // SC_WORKERS
    n_chunks = per_worker // rows
    assert n_chunks * rows * SC_WORKERS == n_idx and n_chunks >= 2
    mesh = plsc.VectorSubcoreMesh(core_axis_name="c", subcore_axis_name="s")

    @functools.partial(
        pl.kernel, mesh=mesh,
        out_type=jax.ShapeDtypeStruct((n_idx, width), table.dtype),
        scratch_types=[pltpu.VMEM((2, rows), I32), pltpu.VMEM((2, rows, width), table.dtype),
                       pltpu.SemaphoreType.DMA((2,)), pltpu.SemaphoreType.DMA((2,))],
        name="sc_gather",
    )
    def gather(table_hbm, idx_hbm, out_hbm, idx_v, rows_v, sem_g, sem_w):
        worker = lax.axis_index("s") * SC_CORES + lax.axis_index("c")
        base = worker * per_worker

        def chunk_rows(ch):
            return pl.ds(pl.multiple_of(base + ch * rows, rows), rows)

        def start_gather(ch, b):
            pltpu.sync_copy(idx_hbm.at[chunk_rows(ch)], idx_v.at[b])
            pltpu.async_copy(table_hbm.at[idx_v.at[b]], rows_v.at[b], sem_g.at[b])

        def wait_gather(b):
            pltpu.make_async_copy(table_hbm.at[pl.ds(0, rows)], rows_v.at[b], sem_g.at[b]).wait()

        def write_copy(ch, b):
            return pltpu.make_async_copy(rows_v.at[b], out_hbm.at[chunk_rows(ch)], sem_w.at[b])

        start_gather(0, 0)
        for ch in range(n_chunks):
            b = ch % 2
            if ch + 1 < n_chunks:
                if ch >= 1:
                    write_copy(ch - 1, 1 - b).wait()
                start_gather(ch + 1, 1 - b)
            wait_gather(b)
            write_copy(ch, b).start()
        write_copy(n_chunks - 2, n_chunks % 2).wait()
        write_copy(n_chunks - 1, (n_chunks - 1) % 2).wait()

    return gather(table, idx)


def _sc_scatter_rows(rows, idx3, n_out):
    n_rows, width = rows.shape
    n_chunks = n_rows // SC_ROWS // SC_WORKERS
    assert n_chunks * SC_ROWS * SC_WORKERS == n_rows and idx3.shape == (n_rows // SC_ROWS, TOP_K, SC_ROWS)
    mesh = plsc.VectorSubcoreMesh(core_axis_name="c", subcore_axis_name="s")

    @functools.partial(
        pl.kernel, mesh=mesh,
        out_type=jax.ShapeDtypeStruct((n_out, width), rows.dtype),
        scratch_types=[pltpu.VMEM((TOP_K, SC_ROWS), I32), pltpu.VMEM((SC_ROWS, width), rows.dtype),
                       pltpu.SemaphoreType.DMA],
        name="sc_scatter",
    )
    def scatter(rows_hbm, idx_hbm, out_hbm, idx_v, rows_v, sem):
        worker = lax.axis_index("s") * SC_CORES + lax.axis_index("c")

        @pl.loop(0, n_chunks)
        def _(ch):
            chunk = worker * n_chunks + ch
            pltpu.sync_copy(idx_hbm.at[chunk], idx_v)
            pltpu.sync_copy(rows_hbm.at[pl.ds(pl.multiple_of(chunk * SC_ROWS, SC_ROWS), SC_ROWS)],
                            rows_v)
            for kk in range(TOP_K):
                pltpu.async_copy(rows_v, out_hbm.at[idx_v.at[kk]], sem).wait()

    return scatter(rows, idx3)


def _combine_kernel(x1_ref, mod_ref, tw_ref, fg_ref, g_ref, *rest):
    out_ref = rest[-1]
    tw = tw_ref[...]
    f_lo = jnp.zeros((COMBINE_TILE, HALF_W), F32)
    f_hi = jnp.zeros((COMBINE_TILE, HALF_W), F32)
    for kk in range(TOP_K):
        lo, hi = _unpack_halves(g_ref[kk])
        f_lo = f_lo + lo * tw[:, kk:kk + 1]
        f_hi = f_hi + hi * tw[:, kk:kk + 1]
    gate2 = mod_ref[0][:, 5 * D_MODEL:6 * D_MODEL]
    x2 = x1_ref[...] + gate2 * jnp.concatenate([f_lo, f_hi], axis=-1)
    out_ref[...] = _rms(x2) * fg_ref[...]


def _combine_call(x1, mod3, tw, final_g, gathered, partial, tile0, out_tile0, out_tiles, name):
    n_tiles = gathered.shape[1] // COMBINE_TILE
    in_specs = [
        pl.BlockSpec((COMBINE_TILE, D_MODEL), lambda t: (tile0 + t, 0)),
        pl.BlockSpec((1, 1, N_MOD * D_MODEL), lambda t: (_mod_row((tile0 + t) * (COMBINE_TILE // TOK_TILE)), 0, 0)),
        pl.BlockSpec((COMBINE_TILE, LANES), lambda t: (tile0 + t, 0)),
        pl.BlockSpec((1, D_MODEL), lambda t: (0, 0)),
        pl.BlockSpec((TOP_K, COMBINE_TILE, HALF_W), lambda t: (0, t, 0)),
    ]
    args = [x1, mod3, tw, final_g, gathered]
    aliases = {}
    if partial is not None:
        in_specs.append(pl.BlockSpec(memory_space=pl.ANY))
        args.append(partial)
        aliases = {len(args) - 1: 0}
    return pl.pallas_call(
        _combine_kernel,
        out_shape=jax.ShapeDtypeStruct((out_tiles * COMBINE_TILE, D_MODEL), F32),
        grid=(n_tiles,),
        in_specs=in_specs,
        out_specs=pl.BlockSpec((COMBINE_TILE, D_MODEL), lambda t: (out_tile0 + t, 0)),
        input_output_aliases=aliases,
        compiler_params=_params(("arbitrary",)),
        name=name,
    )(*args)


def kernel(x_prompt, x_sample, state_gla_fwd, state_gla_bwd, c, c_ctx, norm1_g, w_mod, b_mod, w_in,
           w_alpha, b_alpha, gla_norm_g, w_pool_grp, pool_scale, w_branch_gla, w_branch_pool, w_out,
           norm2_g, w_router, b_router, w_gate, b_gate, w_up, b_up, w_down, b_down, final_norm_g):
    l = 0
    x_ctx = x_prompt.reshape(N_CTX, D_MODEL)
    x_lat = x_sample.reshape(N_LAT, D_MODEL)

    cvec = jnp.concatenate([c_ctx[None, :], c, jnp.zeros((8 - 1 - DEC_BATCH, D_MODEL), F32)], axis=0)
    mod = _mod_call(cvec, w_mod[l], b_mod[l][None, :])
    mod3 = mod.reshape(8, 1, N_MOD * D_MODEL)

    w_in_b = w_in[l].astype(BF16)
    w_main = w_in_b[:, :MAIN_W]
    w_alr = w_in_b[:, MAIN_W:MAIN_W + ALR_W]
    w_xp = w_in_b[:, MAIN_W + ALR_W:MAIN_W + ALR_W + POOL_W]
    w_mg = w_in_b[:, MAIN_W + ALR_W + POOL_W:]
    q, k, v, og, alr, xp, mg = _inproj_call(x_ctx, x_lat, mod3, norm1_g[l][None, :],
                                            w_main, w_alr, w_xp, w_mg)

    zpad = jnp.zeros((GLA_LOWRANK, QK_W), F32)
    wa_f = jnp.concatenate([w_alpha[l, 0], zpad], axis=0)
    wa_b = jnp.concatenate([zpad, w_alpha[l, 1]], axis=0)
    o_f, o_b, s_f, s_b = _gla_call(q, k, v, alr, wa_f, b_alpha[l, 0][None, :], wa_b,
                                   b_alpha[l, 1][None, :], state_gla_fwd[:, l], state_gla_bwd[:, l])

    pooled_c = _pool_ctx_call(xp)
    pooled_l = _pool_lat_call(xp)

    w_router_pad = jnp.pad(w_router[l], ((0, 0), (0, LANES - N_EXPERTS)))
    b_router_pad = jnp.pad(b_router[l], (0, LANES - N_EXPERTS))[None, :]
    x1, h2, top_idx, top_w = _post_call(
        x_ctx, x_lat, mod3, o_f, o_b, og, pooled_c, pooled_l, mg,
        gla_norm_g[l].reshape(1, V_W), w_pool_grp[l].astype(BF16), pool_scale[l][None, :],
        w_branch_gla[l].astype(BF16), w_branch_pool[l].astype(BF16), w_out[l].astype(BF16),
        norm2_g[l][None, :], w_router_pad, b_router_pad)

    rank, cnt = _route_call(top_idx)
    counts = cnt[0, :N_EXPERTS].astype(I32)
    padded = (counts + MOE_BLOCK - 1) // MOE_BLOCK * MOE_BLOCK
    pad_end = jnp.cumsum(padded).astype(I32)
    pad_start = pad_end - padded
    block_first = jnp.arange(N_SLOT_BLOCKS, dtype=I32) * MOE_BLOCK
    block_e = jnp.minimum(jnp.sum((pad_end[None, :] <= block_first[:, None]).astype(I32), axis=1),
                          N_EXPERTS - 1).astype(I32)
    n_used = (pad_end[-1:] // MOE_BLOCK).astype(I32)
    run_start = jnp.concatenate([jnp.ones((1,), I32), (block_e[1:] != block_e[:-1]).astype(I32)])
    parity = ((jnp.cumsum(run_start) - 1) % 2).astype(I32)
    after = pad_end[block_e] // MOE_BLOCK
    next_e = jnp.where(after < n_used[0], block_e[jnp.minimum(after, N_SLOT_BLOCKS - 1)], -1).astype(I32)
    experts = jnp.arange(N_EXPERTS, dtype=I32)
    tk = top_idx[:, :TOP_K]
    pos = jnp.sum(jnp.where(tk[:, :, None] == experts, pad_start, 0), axis=-1) + rank[:, :TOP_K]
    pos = pos.astype(I32)
    pos_by_choice = pos.T
    pos_chunks = pos_by_choice.reshape(TOP_K, N_TOK // SC_ROWS, SC_ROWS).transpose(1, 0, 2)

    hs = _sc_scatter_rows(h2, pos_chunks, N_SLOTS)
    y = _moe_call(block_e, n_used, next_e, parity, hs,
                  w_gate[l], b_gate[l][:, None, :], w_up[l], b_up[l][:, None, :],
                  w_down[l], b_down[l][:, None, :])
    outs = []
    chunk_tok = N_CTX // COMBINE_CHUNKS
    for group, tok0 in (("ctx", 0), ("lat", N_CTX)):
        out = None
        for ci in range(COMBINE_CHUNKS):
            t0 = tok0 + ci * chunk_tok
            idx = pos_by_choice[:, t0:t0 + chunk_tok].reshape(TOP_K * chunk_tok)
            gathered = _sc_gather_rows(y, idx).reshape(TOP_K, chunk_tok, HALF_W)
            out = _combine_call(x1, mod3, top_w, final_norm_g[None, :], gathered, out,
                                t0 // COMBINE_TILE, ci * chunk_tok // COMBINE_TILE, N_CTX // COMBINE_TILE,
                                "combine_%s%d" % (group, ci))
        outs.append(out)
    y_prompt = outs[0].reshape(BATCH, SEQ, D_MODEL)
    y_sample = outs[1].reshape(DEC_BATCH, DEC_SEQ, D_MODEL)
    return (y_prompt, y_sample, s_f[:, None], s_b[:, None])
```

```python
import functools

import jax
import jax.numpy as jnp
from jax import lax
from jax.experimental import pallas as pl
from jax.experimental.pallas import tpu as pltpu
from jax.experimental.pallas import tpu_sc as plsc

F32 = jnp.float32
BF16 = jnp.bfloat16
I32 = jnp.int32

D_MODEL = 1024
BATCH = 32
SEQ = 256
DEC_BATCH = 4
DEC_SEQ = 2048
GRID_W = 64
GLA_HEADS = 4
GLA_DK = 128
GLA_DV = 256
GLA_LOWRANK = 16
GLA_TAU = 16.0
GLA_CHUNK = 64
POOL_GROUPS = 4
POOL_GROUP_DIM = 128
POOL_WINDOWS = (2, 4, 8, 16)
N_EXPERTS = 32
TOP_K = 4
D_FF = 1024
SWIGLU_LIMIT = 7.0
SWIGLU_ALPHA = 1.702
MOE_BLOCK = 256
NORM_EPS = 1e-6
N_MOD = 6

QK_W = GLA_HEADS * GLA_DK
V_W = GLA_HEADS * GLA_DV
POOL_W = POOL_GROUPS * POOL_GROUP_DIM
MAIN_W = 2 * QK_W + 2 * V_W
ALR_W = 2 * GLA_LOWRANK
MG_W = 2 * D_MODEL

N_CTX = BATCH * SEQ
N_LAT = DEC_BATCH * DEC_SEQ
N_TOK = N_CTX + N_LAT
N_SLOT_BLOCKS = -(-(N_TOK * TOP_K + N_EXPERTS * (MOE_BLOCK - 1)) // MOE_BLOCK)
N_SLOTS = N_SLOT_BLOCKS * MOE_BLOCK

LANES = 128
TOK_TILE = 256
N_TILES = N_TOK // TOK_TILE
CTX_TILES = N_CTX // TOK_TILE
LAT_TILES_PER_SEQ = DEC_SEQ // TOK_TILE
ROUTE_TILE = 2048
VMEM_LIMIT = 56 * 1024 * 1024

GLA_SEQS = 4
CTX_CHUNKS = SEQ // GLA_CHUNK
LAT_CHUNKS = DEC_SEQ // GLA_CHUNK
CHUNKS_PER_TILE = TOK_TILE // GLA_CHUNK
GLA_CTX_STEPS = (BATCH // GLA_SEQS) * CTX_CHUNKS
TILE_GRID = 8

NT_DIMS = (((1,), (1,)), ((), ()))
TN_DIMS = (((0,), (0,)), ((), ()))

assert DEC_BATCH == GLA_SEQS and SEQ == TOK_TILE and N_TILES == TILE_GRID * TILE_GRID


def _params(semantics, vmem=VMEM_LIMIT):
    return pltpu.CompilerParams(dimension_semantics=semantics, vmem_limit_bytes=vmem)


def _split_bf16(a):
    hi = a.astype(BF16)
    lo = (a - hi.astype(F32)).astype(BF16)
    return hi, lo


def _dot(a, b):
    return jnp.dot(a, b, preferred_element_type=F32)


def _dot3(a, b):
    a_hi, a_lo = _split_bf16(a)
    b_hi, b_lo = _split_bf16(b)
    return _dot(a_hi, b_hi) + _dot(a_lo, b_hi) + _dot(a_hi, b_lo)


def _dot3_short(a, b):
    a_hi, a_lo = _split_bf16(a)
    b_hi, b_lo = _split_bf16(b)
    return _dot(jnp.concatenate([a_hi, a_lo, a_hi], axis=1), jnp.concatenate([b_hi, b_hi, b_lo], axis=0))


def _sigmoid(x):
    return 1.0 / (1.0 + jnp.exp(-x))


HALF_W = D_MODEL // 2
HIGH_HALF_MASK = -65536


def _pack_halves(lo, hi):
    lo_bits = pltpu.bitcast(lo.astype(BF16).astype(F32), I32)
    hi_bits = pltpu.bitcast(hi.astype(BF16).astype(F32), I32)
    return lax.shift_right_logical(lo_bits, 16) | (hi_bits & HIGH_HALF_MASK)


def _unpack_halves(words):
    lo = pltpu.bitcast(lax.shift_left(words, 16), F32)
    hi = pltpu.bitcast(words & HIGH_HALF_MASK, F32)
    return lo, hi


def _rms(x):
    return x * lax.rsqrt(jnp.mean(x * x, axis=-1, keepdims=True) + NORM_EPS)


def _mod_row(t):
    return jnp.where(t < CTX_TILES, 0, 1 + (t - CTX_TILES) // LAT_TILES_PER_SEQ)


def _store_tile(t):
    u = t - CTX_TILES
    return jnp.where(t < CTX_TILES, t,
                     CTX_TILES + DEC_BATCH * (u % LAT_TILES_PER_SEQ) + u // LAT_TILES_PER_SEQ)


def _ctx_tile(t):
    return jnp.minimum(t, CTX_TILES - 1)


def _lat_tile(t):
    return jnp.maximum(t - CTX_TILES, 0)


def _mod_kernel(c_ref, w_ref, b_ref, o_ref):
    c = c_ref[...]
    o_ref[...] = _dot3(c * _sigmoid(c), w_ref[...]) + b_ref[...]


def _mod_call(cvec, w_mod, b_mod):
    rows = cvec.shape[0]
    return pl.pallas_call(
        _mod_kernel,
        out_shape=jax.ShapeDtypeStruct((rows, N_MOD * D_MODEL), F32),
        grid=(N_MOD,),
        in_specs=[
            pl.BlockSpec((rows, D_MODEL), lambda j: (0, 0)),
            pl.BlockSpec((D_MODEL, D_MODEL), lambda j: (0, j)),
            pl.BlockSpec((1, D_MODEL), lambda j: (0, j)),
        ],
        out_specs=pl.BlockSpec((rows, D_MODEL), lambda j: (0, j)),
        compiler_params=_params(("arbitrary",)),
        name="mod",
    )(cvec, w_mod, b_mod)


def _inproj_kernel(xc_ref, xl_ref, mod_ref, g_ref, wmain_ref, walr_ref, wxp_ref, wmg_ref,
                   q_ref, k_ref, v_ref, og_ref, alr_ref, xp_ref, mg_ref):
    t = pl.program_id(0)
    x = jnp.where(t < CTX_TILES, xc_ref[...], xl_ref[...])
    mod = mod_ref[0]
    shift1 = mod[:, 0:D_MODEL]
    scale1 = mod[:, D_MODEL:2 * D_MODEL]
    h = _rms(x) * g_ref[...]
    h = (h * (1.0 + scale1) + shift1).astype(BF16)
    z = _dot(h, wmain_ref[...])
    q_ref[...] = (z[:, 0:QK_W] * (GLA_DK ** -0.5)).astype(BF16)
    k_ref[...] = z[:, QK_W:2 * QK_W].astype(BF16)
    v_ref[...] = z[:, 2 * QK_W:2 * QK_W + V_W].astype(BF16)
    og = z[:, 2 * QK_W + V_W:MAIN_W]
    og_ref[...] = (og * _sigmoid(og)).astype(BF16)
    alr_ref[...] = _dot(h, walr_ref[...])
    xp_ref[...] = _dot(h, wxp_ref[...])
    mg_ref[...] = _sigmoid(_dot(h, wmg_ref[...])).astype(BF16)


def _inproj_call(x_ctx, x_lat, mod3, norm1_g, w_main, w_alr, w_xp, w_mg):
    const = lambda t: (0, 0)
    stored = lambda t: (_store_tile(t), 0)
    widths = (QK_W, QK_W, V_W, V_W, ALR_W, POOL_W, MG_W)
    dtypes = (BF16, BF16, BF16, BF16, F32, F32, BF16)
    return pl.pallas_call(
        _inproj_kernel,
        out_shape=[jax.ShapeDtypeStruct((N_TOK, w), dt) for w, dt in zip(widths, dtypes)],
        grid=(N_TILES,),
        in_specs=[
            pl.BlockSpec((TOK_TILE, D_MODEL), lambda t: (_ctx_tile(t), 0)),
            pl.BlockSpec((TOK_TILE, D_MODEL), lambda t: (_lat_tile(t), 0)),
            pl.BlockSpec((1, 1, N_MOD * D_MODEL), lambda t: (_mod_row(t), 0, 0)),
            pl.BlockSpec((1, D_MODEL), const),
            pl.BlockSpec((D_MODEL, MAIN_W), const),
            pl.BlockSpec((D_MODEL, ALR_W), const),
            pl.BlockSpec((D_MODEL, POOL_W), const),
            pl.BlockSpec((D_MODEL, MG_W), const),
        ],
        out_specs=[pl.BlockSpec((TOK_TILE, w), stored) for w in widths],
        compiler_params=_params(("arbitrary",)),
        name="inproj",
    )(x_ctx, x_lat, mod3, norm1_g, w_main, w_alr, w_xp, w_mg)


def _gla_direction(q_ref, k_ref, v_ref, alr_ref, wa_ref, ba_ref, o_ref, st_ref, d, rev):
    rows = GLA_SEQS * GLA_CHUNK
    stack = lambda ref, cols: jnp.concatenate([ref[0, s, :, cols] for s in range(GLA_SEQS)], axis=0)
    alr = stack(alr_ref, slice(None))
    a = _dot3_short(alr, wa_ref[...]) + ba_ref[...]
    g = (jnp.minimum(a, 0.0) - jnp.log(1.0 + jnp.exp(-jnp.abs(a)))) * (1.0 / GLA_TAU)

    row = lax.broadcasted_iota(I32, (rows, rows), 0)
    col = lax.broadcasted_iota(I32, (rows, rows), 1)
    same = (row // GLA_CHUNK) == (col // GLA_CHUNK)
    tri = same & ((col >= row) if rev else (col <= row))
    tri_b = jnp.where(tri, 1.0, 0.0).astype(BF16)
    g_hi, g_lo = _split_bf16(g)
    bcum = _dot(tri_b, g_hi) + _dot(tri_b, g_lo)

    def per_seq_row(r):
        return jnp.concatenate(
            [jnp.broadcast_to(bcum[s * GLA_CHUNK + r:s * GLA_CHUNK + r + 1], (GLA_CHUNK, QK_W))
             for s in range(GLA_SEQS)], axis=0)

    r_last = 0 if rev else GLA_CHUNK - 1
    blast = per_seq_row(r_last)
    bmid = per_seq_row(GLA_CHUNK // 2)
    e_q = jnp.exp(bcum - bmid)
    e_k = jnp.exp(bmid - bcum)
    e_in = jnp.exp(bcum)
    e_out = jnp.exp(blast - bcum)
    q = stack(q_ref, slice(None)).astype(F32)
    k = stack(k_ref, slice(None)).astype(F32)

    wide = (rows, GLA_SEQS * GLA_DK)
    own = (lax.broadcasted_iota(I32, wide, 0) // GLA_CHUNK) == (lax.broadcasted_iota(I32, wide, 1) // GLA_DK)

    def block_diag(x):
        return jnp.where(own, jnp.concatenate([x] * GLA_SEQS, axis=1), 0.0).astype(BF16)

    for h in range(GLA_HEADS):
        ks = slice(h * GLA_DK, (h + 1) * GLA_DK)
        vs = slice(h * GLA_DV, (h + 1) * GLA_DV)
        qh = q[:, ks]
        kh = k[:, ks]
        vh = stack(v_ref, vs)
        att = lax.dot_general((qh * e_q[:, ks]).astype(BF16), (kh * e_k[:, ks]).astype(BF16),
                              NT_DIMS, preferred_element_type=F32)
        att = jnp.where(tri, att, 0.0).astype(BF16)
        st = st_ref[d, h]
        o_inter = lax.dot_general(block_diag(qh * e_in[:, ks]), st.astype(BF16), NT_DIMS,
                                  preferred_element_type=F32)
        o_h = o_inter + _dot(att, vh)
        for s in range(GLA_SEQS):
            o_ref[0, s, :, vs] = o_h[s * GLA_CHUNK:(s + 1) * GLA_CHUNK]
        upd = lax.dot_general(vh, block_diag(kh * e_out[:, ks]), TN_DIMS,
                              preferred_element_type=F32)
        e_last = jnp.concatenate(
            [jnp.exp(bcum[s * GLA_CHUNK + r_last:s * GLA_CHUNK + r_last + 1, ks])
             for s in range(GLA_SEQS)], axis=1)
        st_ref[d, h] = st * e_last + upd


def _gla_kernel(qf_ref, kf_ref, vf_ref, af_ref, qb_ref, kb_ref, vb_ref, ab_ref,
                waf_ref, baf_ref, wab_ref, bab_ref, s0f_ref, s0b_ref,
                of_ref, ob_ref, sf_ref, sb_ref, st_ref):
    i = pl.program_id(0)
    is_ctx = i < GLA_CTX_STEPS
    chunk = jnp.where(is_ctx, i % CTX_CHUNKS, i - GLA_CTX_STEPS)

    @pl.when(is_ctx & (chunk == 0))
    def _():
        st_ref[...] = jnp.zeros(st_ref.shape, F32)

    @pl.when(i == GLA_CTX_STEPS)
    def _():
        for s in range(GLA_SEQS):
            ls = slice(s * GLA_DK, (s + 1) * GLA_DK)
            for h in range(GLA_HEADS):
                st_ref[0, h, :, ls] = s0f_ref[s, h].T
                st_ref[1, h, :, ls] = s0b_ref[s, h].T

    _gla_direction(qf_ref, kf_ref, vf_ref, af_ref, waf_ref, baf_ref, of_ref, st_ref, 0, False)
    _gla_direction(qb_ref, kb_ref, vb_ref, ab_ref, wab_ref, bab_ref, ob_ref, st_ref, 1, True)

    @pl.when(is_ctx & (chunk == CTX_CHUNKS - 1))
    def _():
        for s in range(GLA_SEQS):
            ls = slice(s * GLA_DK, (s + 1) * GLA_DK)
            for h in range(GLA_HEADS):
                sf_ref[s, h] = st_ref[0, h, :, ls].T
                sb_ref[s, h] = st_ref[1, h, :, ls].T


def _gla_block(i, rev):
    is_ctx = i < GLA_CTX_STEPS
    group = i // CTX_CHUNKS
    c_ctx = i % CTX_CHUNKS
    c_lat = i - GLA_CTX_STEPS
    if rev:
        c_ctx = CTX_CHUNKS - 1 - c_ctx
        c_lat = LAT_CHUNKS - 1 - c_lat
    j = c_lat // CHUNKS_PER_TILE
    per_row = TILE_GRID // GLA_SEQS
    a = jnp.where(is_ctx, group // per_row, CTX_TILES // TILE_GRID + j // per_row)
    b = jnp.where(is_ctx, group % per_row, j % per_row)
    c = jnp.where(is_ctx, c_ctx, c_lat % CHUNKS_PER_TILE)
    return (a, b, c, 0)


def _gla_call(q, k, v, alr, wa_f, ba_f, wa_b, ba_b, s0_f, s0_b):
    def view(arr):
        return arr.reshape(TILE_GRID, TILE_GRID, TOK_TILE, arr.shape[-1])

    def spec(width, rev):
        return pl.BlockSpec((1, GLA_SEQS, GLA_CHUNK, width), lambda i: _gla_block(i, rev))

    const = lambda i: (0, 0)
    st_block = (GLA_SEQS, GLA_HEADS, GLA_DK, GLA_DV)
    whole_state = pl.BlockSpec(st_block, lambda i: (0, 0, 0, 0))
    ctx_state = pl.BlockSpec(
        st_block, lambda i: (jnp.minimum(i // CTX_CHUNKS, BATCH // GLA_SEQS - 1), 0, 0, 0))
    in_specs = []
    for rev in (False, True):
        in_specs += [spec(QK_W, rev), spec(QK_W, rev), spec(V_W, rev), spec(ALR_W, rev)]
    in_specs += [pl.BlockSpec((ALR_W, QK_W), const), pl.BlockSpec((1, QK_W), const)] * 2
    in_specs += [whole_state, whole_state]
    o_shape = jax.ShapeDtypeStruct((TILE_GRID, TILE_GRID, TOK_TILE, V_W), F32)
    s_shape = jax.ShapeDtypeStruct((BATCH, GLA_HEADS, GLA_DK, GLA_DV), F32)
    qv, kv, vv, av = view(q), view(k), view(v), view(alr)
    o_f, o_b, s_f, s_b = pl.pallas_call(
        _gla_kernel,
        out_shape=[o_shape, o_shape, s_shape, s_shape],
        grid=(GLA_CTX_STEPS + LAT_CHUNKS,),
        in_specs=in_specs,
        out_specs=[spec(V_W, False), spec(V_W, True), ctx_state, ctx_state],
        scratch_shapes=[pltpu.VMEM((2, GLA_HEADS, GLA_DV, GLA_SEQS * GLA_DK), F32)],
        compiler_params=_params(("arbitrary",)),
        name="gla",
    )(qv, kv, vv, av, qv, kv, vv, av, wa_f, ba_f, wa_b, ba_b, s0_f, s0_b)
    return o_f.reshape(N_TOK, V_W), o_b.reshape(N_TOK, V_W), s_f, s_b


def _band(n, w, block):
    row = lax.broadcasted_iota(I32, (n, n), 0)
    col = lax.broadcasted_iota(I32, (n, n), 1)
    inside = (col >= row - w // 2) & (col <= row + w // 2 - 1)
    if block < n:
        inside = inside & ((row // block) == (col // block))
    return jnp.where(inside, 1.0, 0.0).astype(BF16)


def _win_count(p, n, w):
    return jnp.minimum(p + w // 2 - 1, n - 1) - jnp.maximum(p - w // 2, 0) + 1


def _pool_ctx_kernel(x_ref, o_ref, band_ref):
    @pl.when(pl.program_id(0) == 0)
    def _():
        for gi, w in enumerate(POOL_WINDOWS):
            band_ref[gi] = _band(SEQ, w, SEQ)

    p = lax.broadcasted_iota(I32, (SEQ, POOL_GROUP_DIM), 0)
    for part in range(POOL_CTX_SEQS):
        rs = slice(part * SEQ, (part + 1) * SEQ)
        for gi, w in enumerate(POOL_WINDOWS):
            cs = slice(gi * POOL_GROUP_DIM, (gi + 1) * POOL_GROUP_DIM)
            x = x_ref[rs, cs]
            hi, lo = _split_bf16(x)
            band = band_ref[gi]
            s = _dot(band, hi) + _dot(band, lo)
            cnt = _win_count(p, SEQ, w).astype(F32)
            o_ref[rs, cs] = s / cnt - x


POOL_CTX_SEQS = 4


def _pool_ctx_call(xp):
    spec = pl.BlockSpec((POOL_CTX_SEQS * SEQ, POOL_W), lambda b: (b, 0))
    return pl.pallas_call(
        _pool_ctx_kernel,
        out_shape=jax.ShapeDtypeStruct((N_CTX, POOL_W), F32),
        grid=(BATCH // POOL_CTX_SEQS,),
        in_specs=[spec],
        out_specs=spec,
        scratch_shapes=[pltpu.VMEM((POOL_GROUPS, SEQ, SEQ), BF16)],
        compiler_params=_params(("arbitrary",)),
        name="pool_ctx",
    )(xp)


POOL_HALO = (max(POOL_WINDOWS) // 2) * GRID_W


def _pool_lat_kernel(x_ref, o_ref, pad_ref):
    rows = DEC_SEQ // GRID_W
    p = lax.broadcasted_iota(I32, (DEC_SEQ, POOL_GROUP_DIM), 0)
    r = p // GRID_W
    cidx = p % GRID_W
    zeros = jnp.zeros((POOL_HALO, POOL_GROUP_DIM), F32)
    pad_ref[0:POOL_HALO, :] = zeros
    pad_ref[POOL_HALO + DEC_SEQ:2 * POOL_HALO + DEC_SEQ, :] = zeros
    for gi, w in enumerate(POOL_WINDOWS):
        cs = slice(gi * POOL_GROUP_DIM, (gi + 1) * POOL_GROUP_DIM)
        band = _band(TOK_TILE, w, GRID_W)
        for t in range(LAT_TILES_PER_SEQ):
            hi, lo = _split_bf16(x_ref[t, 0, :, cs])
            pad_ref[POOL_HALO + t * TOK_TILE:POOL_HALO + (t + 1) * TOK_TILE, :] = (
                _dot(band, hi) + _dot(band, lo))
        acc = jnp.zeros((DEC_SEQ, POOL_GROUP_DIM), F32)
        for dr in range(-(w // 2), w // 2):
            start = POOL_HALO + dr * GRID_W
            acc = acc + pad_ref[start:start + DEC_SEQ, :]
        cnt = (_win_count(r, rows, w) * _win_count(cidx, GRID_W, w)).astype(F32)
        pooled = acc / cnt
        for t in range(LAT_TILES_PER_SEQ):
            rs = slice(t * TOK_TILE, (t + 1) * TOK_TILE)
            o_ref[t, 0, :, cs] = pooled[rs] - x_ref[t, 0, :, cs]


def _pool_lat_call(xp):
    view = xp.reshape(N_TILES // DEC_BATCH, DEC_BATCH, TOK_TILE, POOL_W)
    blk = (LAT_TILES_PER_SEQ, 1, TOK_TILE, POOL_W)
    out = pl.pallas_call(
        _pool_lat_kernel,
        out_shape=jax.ShapeDtypeStruct((LAT_TILES_PER_SEQ, DEC_BATCH, TOK_TILE, POOL_W), F32),
        grid=(DEC_BATCH,),
        in_specs=[pl.BlockSpec(blk, lambda s: (CTX_TILES // DEC_BATCH // LAT_TILES_PER_SEQ, s, 0, 0))],
        out_specs=pl.BlockSpec(blk, lambda s: (0, s, 0, 0)),
        scratch_shapes=[pltpu.VMEM((DEC_SEQ + 2 * POOL_HALO, POOL_GROUP_DIM), F32)],
        compiler_params=_params(("arbitrary",)),
        name="pool_lat",
    )(view)
    return out.reshape(N_LAT, POOL_W)


POST_PARTS = 2
POST_TILE = POST_PARTS * TOK_TILE


def _post_kernel(xc_ref, xl_ref, mod_ref, *refs):
    stored = [refs[5 * p:5 * p + 5] for p in range(POST_PARTS)]
    (pc_ref, gng_ref, wpg_ref, psc_ref, wbg_ref, wbp_ref, wout_ref, n2g_ref, wr_ref, br_ref,
     x1_ref, h2_ref, idx_ref, tw_ref) = refs[5 * POST_PARTS:]
    t = pl.program_id(0)
    is_ctx = t < CTX_TILES // POST_PARTS
    mod = mod_ref[0]
    gate1 = mod[:, 2 * D_MODEL:3 * D_MODEL]
    shift2 = mod[:, 3 * D_MODEL:4 * D_MODEL]
    scale2 = mod[:, 4 * D_MODEL:5 * D_MODEL]
    for p in range(POST_PARTS):
        of_ref, ob_ref, og_ref, pl_ref, mg_ref = stored[p]
        rs = slice(p * TOK_TILE, (p + 1) * TOK_TILE)
        x = jnp.where(is_ctx, xc_ref[rs, :], xl_ref[rs, :])
        pooled = jnp.where(is_ctx, pc_ref[rs, :], pl_ref[...])

        o = of_ref[...] + ob_ref[...]
        og = og_ref[...].astype(F32)
        gated = []
        for h in range(GLA_HEADS):
            vs = slice(h * GLA_DV, (h + 1) * GLA_DV)
            oh = _rms(o[:, vs]) * gng_ref[:, vs]
            gated.append((oh * og[:, vs]).astype(BF16))
        br_gla = _dot(jnp.concatenate(gated, axis=-1), wbg_ref[...])

        pm = []
        for gi in range(POOL_GROUPS):
            cs = slice(gi * POOL_GROUP_DIM, (gi + 1) * POOL_GROUP_DIM)
            pmg = _dot(pooled[:, cs].astype(BF16), wpg_ref[gi]) * psc_ref[:, cs]
            pm.append(pmg.astype(BF16))
        br_pool = _dot(jnp.concatenate(pm, axis=-1), wbp_ref[...])

        mg = mg_ref[...].astype(F32)
        merged = mg[:, 0:D_MODEL] * br_gla + mg[:, D_MODEL:MG_W] * br_pool
        m = _dot(merged.astype(BF16), wout_ref[...])
        x1 = x + gate1 * m
        x1_ref[rs, :] = x1
        h2 = _rms(x1) * n2g_ref[...]
        h2 = h2 * (1.0 + scale2) + shift2
        h2_ref[rs, :] = _pack_halves(h2[:, :HALF_W], h2[:, HALF_W:])

        logits = _dot3(h2, wr_ref[...]) + br_ref[...]
        lane = lax.broadcasted_iota(I32, (TOK_TILE, LANES), 1)
        lane_f = lane.astype(F32)
        neg = jnp.float32(-jnp.inf)
        cur = jnp.where(lane < N_EXPERTS, logits, neg)
        vals, idxs = [], []
        for _ in range(TOP_K):
            mx = jnp.max(cur, axis=-1, keepdims=True)
            ix = jnp.min(jnp.where(cur == mx, lane_f, float(LANES)), axis=-1, keepdims=True)
            vals.append(mx)
            idxs.append(ix)
            cur = jnp.where(lane_f == ix, neg, cur)
        ex = [jnp.exp(vv - vals[0]) for vv in vals]
        tot = ex[0] + ex[1] + ex[2] + ex[3]
        idx_out = jnp.zeros((TOK_TILE, LANES), F32)
        w_out = jnp.zeros((TOK_TILE, LANES), F32)
        for kk in range(TOP_K):
            idx_out = jnp.where(lane == kk, idxs[kk], idx_out)
            w_out = jnp.where(lane == kk, ex[kk] / tot, w_out)
        idx_ref[rs, :] = idx_out.astype(I32)
        tw_ref[rs, :] = w_out


def _post_call(x_ctx, x_lat, mod3, o_f, o_b, og, pooled_c, pooled_l, mg, gng, wpg, psc, wbg, wbp,
               wout, n2g, wr, br):
    n_steps = N_TILES // POST_PARTS
    ctx_steps = CTX_TILES // POST_PARTS
    row = lambda t: (t, 0)
    const = lambda t: (0, 0)
    ctx_blk = lambda t: (jnp.minimum(t, ctx_steps - 1), 0)
    lat_blk = lambda t: (jnp.maximum(t - ctx_steps, 0), 0)
    in_specs = [
        pl.BlockSpec((POST_TILE, D_MODEL), ctx_blk),
        pl.BlockSpec((POST_TILE, D_MODEL), lat_blk),
        pl.BlockSpec((1, 1, N_MOD * D_MODEL), lambda t: (_mod_row(POST_PARTS * t), 0, 0)),
    ]
    args = [x_ctx, x_lat, mod3]
    for p in range(POST_PARTS):
        stored = lambda t, p=p: (_store_tile(POST_PARTS * t + p), 0)
        stored_lat = lambda t, p=p: (jnp.maximum(_store_tile(POST_PARTS * t + p) - CTX_TILES, 0), 0)
        in_specs += [
            pl.BlockSpec((TOK_TILE, V_W), stored),
            pl.BlockSpec((TOK_TILE, V_W), stored),
            pl.BlockSpec((TOK_TILE, V_W), stored),
            pl.BlockSpec((TOK_TILE, POOL_W), stored_lat),
            pl.BlockSpec((TOK_TILE, MG_W), stored),
        ]
        args += [o_f, o_b, og, pooled_l, mg]
    in_specs += [
        pl.BlockSpec((POST_TILE, POOL_W), ctx_blk),
        pl.BlockSpec((1, V_W), const),
        pl.BlockSpec((POOL_GROUPS, POOL_GROUP_DIM, POOL_GROUP_DIM), lambda t: (0, 0, 0)),
        pl.BlockSpec((1, POOL_W), const),
        pl.BlockSpec((V_W, D_MODEL), const),
        pl.BlockSpec((POOL_W, D_MODEL), const),
        pl.BlockSpec((D_MODEL, D_MODEL), const),
        pl.BlockSpec((1, D_MODEL), const),
        pl.BlockSpec((D_MODEL, LANES), const),
        pl.BlockSpec((1, LANES), const),
    ]
    args += [pooled_c, gng, wpg, psc, wbg, wbp, wout, n2g, wr, br]
    return pl.pallas_call(
        _post_kernel,
        out_shape=[
            jax.ShapeDtypeStruct((N_TOK, D_MODEL), F32),
            jax.ShapeDtypeStruct((N_TOK, HALF_W), I32),
            jax.ShapeDtypeStruct((N_TOK, LANES), I32),
            jax.ShapeDtypeStruct((N_TOK, LANES), F32),
        ],
        grid=(n_steps,),
        in_specs=in_specs,
        out_specs=[
            pl.BlockSpec((POST_TILE, D_MODEL), row),
            pl.BlockSpec((POST_TILE, HALF_W), row),
            pl.BlockSpec((POST_TILE, LANES), row),
            pl.BlockSpec((POST_TILE, LANES), row),
        ],
        compiler_params=_params(("arbitrary",)),
        name="post",
    )(*args)


def _route_kernel(idx_ref, rank_ref, cnt_ref, carry_ref, strict_ref):
    t = pl.program_id(0)

    @pl.when(t == 0)
    def _():
        carry_ref[...] = jnp.zeros((1, LANES), F32)
        row = lax.broadcasted_iota(I32, (ROUTE_TILE, ROUTE_TILE), 0)
        col = lax.broadcasted_iota(I32, (ROUTE_TILE, ROUTE_TILE), 1)
        strict_ref[...] = jnp.where(col < row, 1.0, 0.0).astype(BF16)

    idx = idx_ref[...]
    lane = lax.broadcasted_iota(I32, (ROUTE_TILE, LANES), 1)
    sel = [lane == idx[:, kk:kk + 1] for kk in range(TOP_K)]
    onehot = jnp.zeros((ROUTE_TILE, LANES), F32)
    for kk in range(TOP_K):
        onehot = onehot + jnp.where(sel[kk], 1.0, 0.0)
    before = _dot(strict_ref[...], onehot.astype(BF16)) + carry_ref[...]
    rank = jnp.zeros((ROUTE_TILE, LANES), F32)
    for kk in range(TOP_K):
        rk = jnp.sum(jnp.where(sel[kk], before, 0.0), axis=-1, keepdims=True)
        rank = jnp.where(lane == kk, rk, rank)
    rank_ref[...] = rank.astype(I32)
    carry_ref[...] = carry_ref[...] + jnp.sum(onehot, axis=0, keepdims=True)
    cnt_ref[...] = jnp.broadcast_to(carry_ref[...], (8, LANES))


def _route_call(idx):
    return pl.pallas_call(
        _route_kernel,
        out_shape=[
            jax.ShapeDtypeStruct((N_TOK, LANES), I32),
            jax.ShapeDtypeStruct((8, LANES), F32),
        ],
        grid=(N_TOK // ROUTE_TILE,),
        in_specs=[pl.BlockSpec((ROUTE_TILE, LANES), lambda t: (t, 0))],
        out_specs=[
            pl.BlockSpec((ROUTE_TILE, LANES), lambda t: (t, 0)),
            pl.BlockSpec((8, LANES), lambda t: (0, 0)),
        ],
        scratch_shapes=[pltpu.VMEM((1, LANES), F32), pltpu.VMEM((ROUTE_TILE, ROUTE_TILE), BF16)],
        compiler_params=_params(("arbitrary",)),
        name="route",
    )(idx)


def _moe_kernel(be_ref, nu_ref, ne_ref, par_ref, x_ref, wg_hbm, bg_ref, wu_hbm, bu_ref, wd_hbm,
                bd_ref, y_ref, wf_ref, wgu_ref, wdb_ref, sems):
    b = pl.program_id(0)
    n_used = nu_ref[0]
    e = be_ref[b]
    prev = be_ref[jnp.maximum(b - 1, 0)]
    live = b < n_used

    def weight_copies(expert, slot):
        return [pltpu.make_async_copy(w.at[expert], wf_ref.at[slot, i], sems.at[slot])
                for i, w in enumerate((wg_hbm, wu_hbm, wd_hbm))]

    @pl.when(live & ((b == 0) | (e != prev)))
    def _():
        slot = par_ref[b]

        @pl.when(b == 0)
        def _():
            for cp in weight_copies(e, slot):
                cp.start()

        for cp in weight_copies(e, slot):
            cp.wait()
        nxt = ne_ref[b]

        @pl.when(nxt >= 0)
        def _():
            for cp in weight_copies(nxt, 1 - slot):
                cp.start()

        wgu_ref[:, :D_FF] = wf_ref[slot, 0].astype(BF16)
        wgu_ref[:, D_FF:] = wf_ref[slot, 1].astype(BF16)
        wdb_ref[...] = wf_ref[slot, 2].astype(BF16)

    @pl.when(live)
    def _():
        x_lo, x_hi = _unpack_halves(x_ref[...])
        x = jnp.concatenate([x_lo.astype(BF16), x_hi.astype(BF16)], axis=-1)
        gu = _dot(x, wgu_ref[...])
        gate = jnp.minimum(gu[:, :D_FF] + bg_ref[0], SWIGLU_LIMIT)
        up = jnp.clip(gu[:, D_FF:] + bu_ref[0], -SWIGLU_LIMIT, SWIGLU_LIMIT)
        act = (up + 1.0) * (gate * _sigmoid(SWIGLU_ALPHA * gate))
        y = _dot(act.astype(BF16), wdb_ref[...]) + bd_ref[0]
        y_ref[...] = _pack_halves(y[:, :HALF_W], y[:, HALF_W:])

    @pl.when(jnp.logical_not(live))
    def _():
        y_ref[...] = jnp.zeros((MOE_BLOCK, HALF_W), I32)


def _moe_call(block_e, n_used, next_e, parity, hs, w_gate, b_gate, w_up, b_up, w_down, b_down):
    def blk(b, be, nu, ne, par):
        return jnp.minimum(b, nu[0] - 1)

    row = lambda b, be, nu, ne, par: (blk(b, be, nu, ne, par), 0)
    bsel = lambda b, be, nu, ne, par: (be[blk(b, be, nu, ne, par)], 0, 0)
    any_spec = pl.BlockSpec(memory_space=pl.ANY)
    assert D_MODEL == D_FF
    return pl.pallas_call(
        _moe_kernel,
        out_shape=jax.ShapeDtypeStruct((N_SLOTS, HALF_W), I32),
        grid_spec=pltpu.PrefetchScalarGridSpec(
            num_scalar_prefetch=4,
            grid=(N_SLOT_BLOCKS,),
            in_specs=[
                pl.BlockSpec((MOE_BLOCK, HALF_W), row),
                any_spec,
                pl.BlockSpec((1, 1, D_FF), bsel),
                any_spec,
                pl.BlockSpec((1, 1, D_FF), bsel),
                any_spec,
                pl.BlockSpec((1, 1, D_MODEL), bsel),
            ],
            out_specs=pl.BlockSpec((MOE_BLOCK, HALF_W), lambda b, be, nu, ne, par: (b, 0)),
            scratch_shapes=[
                pltpu.VMEM((2, 3, D_MODEL, D_FF), F32),
                pltpu.VMEM((D_MODEL, 2 * D_FF), BF16),
                pltpu.VMEM((D_FF, D_MODEL), BF16),
                pltpu.SemaphoreType.DMA((2,)),
            ],
        ),
        compiler_params=_params(("arbitrary",)),
        name="moe",
    )(block_e, n_used, next_e, parity, hs, w_gate, b_gate, w_up, b_up, w_down, b_down)


SC_CORES = 2
SC_SUBCORES = 16
SC_WORKERS = SC_CORES * SC_SUBCORES
SC_ROWS = 128
COMBINE_CHUNKS = 2
COMBINE_TILE = 2 * TOK_TILE
assert N_CTX == N_LAT


def _sc_gather_rows(table, idx):
    n_idx = idx.shape[0]
    width = table.shape[1]
    rows = SC_ROWS // 2
    per_worker = n_idx // SC_WORKERS
    n_chunks = per_worker // rows
    assert n_chunks * rows * SC_WORKERS == n_idx and n_chunks >= 2
    mesh = plsc.VectorSubcoreMesh(core_axis_name="c", subcore_axis_name="s")

    @functools.partial(
        pl.kernel, mesh=mesh,
        out_type=jax.ShapeDtypeStruct((n_idx, width), table.dtype),
        scratch_types=[pltpu.VMEM((2, rows), I32), pltpu.VMEM((2, rows, width), table.dtype),
                       pltpu.SemaphoreType.DMA((2,)), pltpu.SemaphoreType.DMA((2,))],
        name="sc_gather",
    )
    def gather(table_hbm, idx_hbm, out_hbm, idx_v, rows_v, sem_g, sem_w):
        worker = lax.axis_index("s") * SC_CORES + lax.axis_index("c")
        base = worker * per_worker

        def chunk_rows(ch):
            return pl.ds(pl.multiple_of(base + ch * rows, rows), rows)

        def start_gather(ch, b):
            pltpu.sync_copy(idx_hbm.at[chunk_rows(ch)], idx_v.at[b])
            pltpu.async_copy(table_hbm.at[idx_v.at[b]], rows_v.at[b], sem_g.at[b])

        def wait_gather(b):
            pltpu.make_async_copy(table_hbm.at[pl.ds(0, rows)], rows_v.at[b], sem_g.at[b]).wait()

        def write_copy(ch, b):
            return pltpu.make_async_copy(rows_v.at[b], out_hbm.at[chunk_rows(ch)], sem_w.at[b])

        start_gather(0, 0)
        for ch in range(n_chunks):
            b = ch % 2
            if ch + 1 < n_chunks:
                if ch >= 1:
                    write_copy(ch - 1, 1 - b).wait()
                start_gather(ch + 1, 1 - b)
            wait_gather(b)
            write_copy(ch, b).start()
        write_copy(n_chunks - 2, n_chunks % 2).wait()
        write_copy(n_chunks - 1, (n_chunks - 1) % 2).wait()

    return gather(table, idx)


def _sc_scatter_rows(rows, idx3, n_out):
    n_rows, width = rows.shape
    n_chunks = n_rows // SC_ROWS // SC_WORKERS
    assert n_chunks * SC_ROWS * SC_WORKERS == n_rows and idx3.shape == (n_rows // SC_ROWS, TOP_K, SC_ROWS)
    mesh = plsc.VectorSubcoreMesh(core_axis_name="c", subcore_axis_name="s")

    @functools.partial(
        pl.kernel, mesh=mesh,
        out_type=jax.ShapeDtypeStruct((n_out, width), rows.dtype),
        scratch_types=[pltpu.VMEM((TOP_K, SC_ROWS), I32), pltpu.VMEM((SC_ROWS, width), rows.dtype),
                       pltpu.SemaphoreType.DMA],
        name="sc_scatter",
    )
    def scatter(rows_hbm, idx_hbm, out_hbm, idx_v, rows_v, sem):
        worker = lax.axis_index("s") * SC_CORES + lax.axis_index("c")

        @pl.loop(0, n_chunks)
        def _(ch):
            chunk = worker * n_chunks + ch
            pltpu.sync_copy(idx_hbm.at[chunk], idx_v)
            pltpu.sync_copy(rows_hbm.at[pl.ds(pl.multiple_of(chunk * SC_ROWS, SC_ROWS), SC_ROWS)],
                            rows_v)
            for kk in range(TOP_K):
                pltpu.async_copy(rows_v, out_hbm.at[idx_v.at[kk]], sem).wait()

    return scatter(rows, idx3)


def _combine_kernel(x1_ref, mod_ref, tw_ref, fg_ref, g_ref, *rest):
    out_ref = rest[-1]
    tw = tw_ref[...]
    f_lo = jnp.zeros((COMBINE_TILE, HALF_W), F32)
    f_hi = jnp.zeros((COMBINE_TILE, HALF_W), F32)
    for kk in range(TOP_K):
        lo, hi = _unpack_halves(g_ref[kk])
        f_lo = f_lo + lo * tw[:, kk:kk + 1]
        f_hi = f_hi + hi * tw[:, kk:kk + 1]
    gate2 = mod_ref[0][:, 5 * D_MODEL:6 * D_MODEL]
    x2 = x1_ref[...] + gate2 * jnp.concatenate([f_lo, f_hi], axis=-1)
    out_ref[...] = _rms(x2) * fg_ref[...]


def _combine_call(x1, mod3, tw, final_g, gathered, partial, tile0, out_tile0, out_tiles, name):
    n_tiles = gathered.shape[1] // COMBINE_TILE
    in_specs = [
        pl.BlockSpec((COMBINE_TILE, D_MODEL), lambda t: (tile0 + t, 0)),
        pl.BlockSpec((1, 1, N_MOD * D_MODEL), lambda t: (_mod_row((tile0 + t) * (COMBINE_TILE // TOK_TILE)), 0, 0)),
        pl.BlockSpec((COMBINE_TILE, LANES), lambda t: (tile0 + t, 0)),
        pl.BlockSpec((1, D_MODEL), lambda t: (0, 0)),
        pl.BlockSpec((TOP_K, COMBINE_TILE, HALF_W), lambda t: (0, t, 0)),
    ]
    args = [x1, mod3, tw, final_g, gathered]
    aliases = {}
    if partial is not None:
        in_specs.append(pl.BlockSpec(memory_space=pl.ANY))
        args.append(partial)
        aliases = {len(args) - 1: 0}
    return pl.pallas_call(
        _combine_kernel,
        out_shape=jax.ShapeDtypeStruct((out_tiles * COMBINE_TILE, D_MODEL), F32),
        grid=(n_tiles,),
        in_specs=in_specs,
        out_specs=pl.BlockSpec((COMBINE_TILE, D_MODEL), lambda t: (out_tile0 + t, 0)),
        input_output_aliases=aliases,
        compiler_params=_params(("arbitrary",)),
        name=name,
    )(*args)


def kernel(x_prompt, x_sample, state_gla_fwd, state_gla_bwd, c, c_ctx, norm1_g, w_mod, b_mod, w_in,
           w_alpha, b_alpha, gla_norm_g, w_pool_grp, pool_scale, w_branch_gla, w_branch_pool, w_out,
           norm2_g, w_router, b_router, w_gate, b_gate, w_up, b_up, w_down, b_down, final_norm_g):
    l = 0
    x_ctx = x_prompt.reshape(N_CTX, D_MODEL)
    x_lat = x_sample.reshape(N_LAT, D_MODEL)

    cvec = jnp.concatenate([c_ctx[None, :], c, jnp.zeros((8 - 1 - DEC_BATCH, D_MODEL), F32)], axis=0)
    mod = _mod_call(cvec, w_mod[l], b_mod[l][None, :])
    mod3 = mod.reshape(8, 1, N_MOD * D_MODEL)

    w_in_b = w_in[l].astype(BF16)
    w_main = w_in_b[:, :MAIN_W]
    w_alr = w_in_b[:, MAIN_W:MAIN_W + ALR_W]
    w_xp = w_in_b[:, MAIN_W + ALR_W:MAIN_W + ALR_W + POOL_W]
    w_mg = w_in_b[:, MAIN_W + ALR_W + POOL_W:]
    q, k, v, og, alr, xp, mg = _inproj_call(x_ctx, x_lat, mod3, norm1_g[l][None, :],
                                            w_main, w_alr, w_xp, w_mg)

    zpad = jnp.zeros((GLA_LOWRANK, QK_W), F32)
    wa_f = jnp.concatenate([w_alpha[l, 0], zpad], axis=0)
    wa_b = jnp.concatenate([zpad, w_alpha[l, 1]], axis=0)
    o_f, o_b, s_f, s_b = _gla_call(q, k, v, alr, wa_f, b_alpha[l, 0][None, :], wa_b,
                                   b_alpha[l, 1][None, :], state_gla_fwd[:, l], state_gla_bwd[:, l])

    pooled_c = _pool_ctx_call(xp)
    pooled_l = _pool_lat_call(xp)

    w_router_pad = jnp.pad(w_router[l], ((0, 0), (0, LANES - N_EXPERTS)))
    b_router_pad = jnp.pad(b_router[l], (0, LANES - N_EXPERTS))[None, :]
    x1, h2, top_idx, top_w = _post_call(
        x_ctx, x_lat, mod3, o_f, o_b, og, pooled_c, pooled_l, mg,
        gla_norm_g[l].reshape(1, V_W), w_pool_grp[l].astype(BF16), pool_scale[l][None, :],
        w_branch_gla[l].astype(BF16), w_branch_pool[l].astype(BF16), w_out[l].astype(BF16),
        norm2_g[l][None, :], w_router_pad, b_router_pad)

    rank, cnt = _route_call(top_idx)
    counts = cnt[0, :N_EXPERTS].astype(I32)
    padded = (counts + MOE_BLOCK - 1) // MOE_BLOCK * MOE_BLOCK
    pad_end = jnp.cumsum(padded).astype(I32)
    pad_start = pad_end - padded
    block_first = jnp.arange(N_SLOT_BLOCKS, dtype=I32) * MOE_BLOCK
    block_e = jnp.minimum(jnp.sum((pad_end[None, :] <= block_first[:, None]).astype(I32), axis=1),
                          N_EXPERTS - 1).astype(I32)
    n_used = (pad_end[-1:] // MOE_BLOCK).astype(I32)
    run_start = jnp.concatenate([jnp.ones((1,), I32), (block_e[1:] != block_e[:-1]).astype(I32)])
    parity = ((jnp.cumsum(run_start) - 1) % 2).astype(I32)
    after = pad_end[block_e] // MOE_BLOCK
    next_e = jnp.where(after < n_used[0], block_e[jnp.minimum(after, N_SLOT_BLOCKS - 1)], -1).astype(I32)
    experts = jnp.arange(N_EXPERTS, dtype=I32)
    tk = top_idx[:, :TOP_K]
    pos = jnp.sum(jnp.where(tk[:, :, None] == experts, pad_start, 0), axis=-1) + rank[:, :TOP_K]
    pos = pos.astype(I32)
    pos_by_choice = pos.T
    pos_chunks = pos_by_choice.reshape(TOP_K, N_TOK // SC_ROWS, SC_ROWS).transpose(1, 0, 2)

    hs = _sc_scatter_rows(h2, pos_chunks, N_SLOTS)
    y = _moe_call(block_e, n_used, next_e, parity, hs,
                  w_gate[l], b_gate[l][:, None, :], w_up[l], b_up[l][:, None, :],
                  w_down[l], b_down[l][:, None, :])
    outs = []
    chunk_tok = N_CTX // COMBINE_CHUNKS
    for group, tok0 in (("ctx", 0), ("lat", N_CTX)):
        out = None
        for ci in range(COMBINE_CHUNKS):
            t0 = tok0 + ci * chunk_tok
            idx = pos_by_choice[:, t0:t0 + chunk_tok].reshape(TOP_K * chunk_tok)
            gathered = _sc_gather_rows(y, idx).reshape(TOP_K, chunk_tok, HALF_W)
            out = _combine_call(x1, mod3, top_w, final_norm_g[None, :], gathered, out,
                                t0 // COMBINE_TILE, ci * chunk_tok // COMBINE_TILE, N_CTX // COMBINE_TILE,
                                "combine_%s%d" % (group, ci))
        outs.append(out)
    y_prompt = outs[0].reshape(BATCH, SEQ, D_MODEL)
    y_sample = outs[1].reshape(DEC_BATCH, DEC_SEQ, D_MODEL)
    return (y_prompt, y_sample, s_f[:, None], s_b[:, None])
```
